```python
import math
import jax, jax.numpy as jnp
from jax import lax
import numpy as np

D_MODEL = 1024
BATCH = 8
SEQ = 2048
DEPTH = 1

PLE_DIM = 256
D_FF = 2816
CONV_CH = 512
CONV_K = 3
SSM_WIDTH = 512
SSM_GROUP = 16
SSM_GROUPS = SSM_WIDTH // SSM_GROUP
SSM_STATE = 64
ALPHA = (2.0 * DEPTH) ** 0.25
BETA = (8.0 * DEPTH) ** -0.25
LN_EPS = 1e-5
IN_COLS = 3 * CONV_CH + SSM_WIDTH + 2 * D_MODEL

kernel_name = "hybrid_conv_s5_macaron_deepnorm_block"


def layer_norm(x, g, b):
    xf = x.astype(jnp.float32)
    mu = jnp.mean(xf, axis=-1, keepdims=True)
    xc = xf - mu
    var = jnp.mean(xc * xc, axis=-1, keepdims=True)
    y = xc * lax.rsqrt(var + LN_EPS) * g.astype(jnp.float32) + b.astype(jnp.float32)
    return y.astype(x.dtype)


def swiglu(x, w_in, w_out):
    gate, up = jnp.split(x @ w_in, 2, axis=-1)
    return (jax.nn.silu(gate) * up) @ w_out


def causal_depthwise_conv(z, w, b):
    c = z.shape[-1]
    y = lax.conv_general_dilated(
        z, w[:, None, :].astype(z.dtype), window_strides=(1,),
        padding=[(CONV_K - 1, 0)], dimension_numbers=("NWC", "WIO", "NWC"),
        feature_group_count=c)
    return y + b


def s5_scan(u, lam_re, lam_im, log_step, b_re, b_im, c_re, c_im, d_skip):
    f32 = jnp.float32
    lam = lax.complex(lam_re.astype(f32), lam_im.astype(f32))
    dt = jnp.exp(log_step.astype(f32))[:, None]
    lam_bar = jnp.exp(lam * dt)
    b_c = lax.complex(b_re.astype(f32), b_im.astype(f32))
    c_c = lax.complex(c_re.astype(f32), c_im.astype(f32))
    b_bar = ((lam_bar - 1.0) / lam)[..., None] * b_c
    uf = u.astype(f32)
    bu = jnp.einsum("blgi,gni->blgn", uf.astype(jnp.complex64), b_bar)
    a = jnp.broadcast_to(lam_bar, bu.shape)

    def combine(left, right):
        a1, s1 = left
        a2, s2 = right
        return a1 * a2, a2 * s1 + s2

    _, states = lax.associative_scan(combine, (a, bu), axis=1)
    y = jnp.einsum("gin,blgn->blgi", c_c, states).real + d_skip.astype(f32) * uf
    bsz, seq = u.shape[0], u.shape[1]
    return y.reshape(bsz, seq, SSM_WIDTH).astype(u.dtype)


def token_mixer(h, w_in, conv_w, conv_b, conv_w_out, lam_re, lam_im, log_step,
                b_re, b_im, c_re, c_im, d_skip, w_glu, w_out):
    bsz, seq, _ = h.shape
    proj = h @ w_in
    cb, cc, ch, su, g_conv, g_ssm = jnp.split(
        proj, [CONV_CH, 2 * CONV_CH, 3 * CONV_CH, 3 * CONV_CH + SSM_WIDTH,
               3 * CONV_CH + SSM_WIDTH + D_MODEL], axis=-1)
    z = causal_depthwise_conv(cc * ch, conv_w, conv_b)
    y_conv = (cb * z) @ conv_w_out
    s = s5_scan(su.reshape(bsz, seq, SSM_GROUPS, SSM_GROUP), lam_re, lam_im, log_step,
                b_re, b_im, c_re, c_im, d_skip)
    s = jax.nn.gelu(s)
    ga, gb = jnp.split(s @ w_glu, 2, axis=-1)
    y_ssm = ga * jax.nn.sigmoid(gb)
    merged = jax.nn.sigmoid(g_conv) * y_conv + jax.nn.sigmoid(g_ssm) * y_ssm
    return merged @ w_out


def _fwd_setup_inputs(seed: int = 0) -> dict:
    key = jax.random.key(seed)
    ks = jax.random.split(key, 40)
    f32 = jnp.float32
    nrm = lambda k, shape, s: (jax.random.normal(k, shape, f32) * s)
    L = DEPTH

    def gain(k):
        return 1.0 + nrm(k, (L, D_MODEL), 0.01)

    def bias(k, n=D_MODEL):
        return nrm(k, (L, n), 0.01)

    n_idx = jnp.arange(SSM_STATE, dtype=f32)
    lam_re = -0.5 + nrm(ks[20], (L, SSM_GROUPS, SSM_STATE), 0.01)
    lam_im = math.pi * n_idx[None, None, :] + nrm(ks[21], (L, SSM_GROUPS, SSM_STATE), 0.01)
    log_step = jax.random.uniform(ks[22], (L, SSM_GROUPS), f32,
                                  math.log(0.001), math.log(0.1))
    return {
        "x": jax.random.normal(ks[0], (BATCH, SEQ, D_MODEL), f32),
        "p": jax.random.normal(ks[1], (DEPTH, BATCH, SEQ, PLE_DIM), f32),
        "ffn1_w_in": nrm(ks[2], (L, D_MODEL, 2 * D_FF), D_MODEL ** -0.5),
        "ffn1_w_out": nrm(ks[3], (L, D_FF, D_MODEL), BETA * D_FF ** -0.5),
        "ln1_g": gain(ks[4]),
        "ln1_b": bias(ks[5]),
        "mix_w_in": nrm(ks[6], (L, D_MODEL, IN_COLS), D_MODEL ** -0.5),
        "conv_w": nrm(ks[7], (L, CONV_K, CONV_CH), CONV_K ** -0.5),
        "conv_b": bias(ks[8], CONV_CH),
        "conv_w_out": nrm(ks[9], (L, CONV_CH, D_MODEL), BETA * CONV_CH ** -0.5),
        "ssm_lam_re": lam_re,
        "ssm_lam_im": lam_im,
        "ssm_log_step": log_step,
        "ssm_b_re": nrm(ks[23], (L, SSM_GROUPS, SSM_STATE, SSM_GROUP), (2.0 * SSM_GROUP) ** -0.5),
        "ssm_b_im": nrm(ks[24], (L, SSM_GROUPS, SSM_STATE, SSM_GROUP), (2.0 * SSM_GROUP) ** -0.5),
        "ssm_c_re": nrm(ks[25], (L, SSM_GROUPS, SSM_GROUP, SSM_STATE), (2.0 * SSM_STATE) ** -0.5),
        "ssm_c_im": nrm(ks[26], (L, SSM_GROUPS, SSM_GROUP, SSM_STATE), (2.0 * SSM_STATE) ** -0.5),
        "ssm_d": nrm(ks[27], (L, SSM_GROUPS, SSM_GROUP), 1.0),
        "ssm_w_glu": nrm(ks[28], (L, SSM_WIDTH, 2 * D_MODEL), BETA * SSM_WIDTH ** -0.5),
        "mix_w_out": nrm(ks[29], (L, D_MODEL, D_MODEL), BETA * D_MODEL ** -0.5),
        "ln2_g": gain(ks[10]),
        "ln2_b": bias(ks[11]),
        "ffn2_w_in": nrm(ks[12], (L, D_MODEL, 2 * D_FF), D_MODEL ** -0.5),
        "ffn2_w_out": nrm(ks[13], (L, D_FF, D_MODEL), BETA * D_FF ** -0.5),
        "ln3_g": gain(ks[14]),
        "ln3_b": bias(ks[15]),
        "ple_w_in": nrm(ks[16], (L, PLE_DIM, D_MODEL), BETA * PLE_DIM ** -0.5),
        "ple_w_gate": nrm(ks[17], (L, D_MODEL, D_MODEL), D_MODEL ** -0.5),
        "ln4_g": gain(ks[18]),
        "ln4_b": bias(ks[19]),
    }


def _fwd_reference(x, p, ffn1_w_in, ffn1_w_out, ln1_g, ln1_b, mix_w_in, conv_w, conv_b,
              conv_w_out, ssm_lam_re, ssm_lam_im, ssm_log_step, ssm_b_re, ssm_b_im,
              ssm_c_re, ssm_c_im, ssm_d, ssm_w_glu, mix_w_out, ln2_g, ln2_b,
              ffn2_w_in, ffn2_w_out, ln3_g, ln3_b, ple_w_in, ple_w_gate, ln4_g, ln4_b):
    for i in range(DEPTH):
        x = layer_norm(ALPHA * x + 0.5 * swiglu(x, ffn1_w_in[i], ffn1_w_out[i]),
                       ln1_g[i], ln1_b[i])
        mix = token_mixer(x, mix_w_in[i], conv_w[i], conv_b[i], conv_w_out[i],
                          ssm_lam_re[i], ssm_lam_im[i], ssm_log_step[i],
                          ssm_b_re[i], ssm_b_im[i], ssm_c_re[i], ssm_c_im[i], ssm_d[i],
                          ssm_w_glu[i], mix_w_out[i])
        x = layer_norm(ALPHA * x + mix, ln2_g[i], ln2_b[i])
        x = layer_norm(ALPHA * x + 0.5 * swiglu(x, ffn2_w_in[i], ffn2_w_out[i]),
                       ln3_g[i], ln3_b[i])
        e = (p[i] @ ple_w_in[i]) * jax.nn.sigmoid(x @ ple_w_gate[i])
        x = layer_norm(ALPHA * x + e, ln4_g[i], ln4_b[i])
    return x


import jax as _jax
import jax.numpy as _jnp

TWIN_FORMAT = 'train_step'
FWD_PARAMS = ['x', 'p', 'ffn1_w_in', 'ffn1_w_out', 'ln1_g', 'ln1_b', 'mix_w_in', 'conv_w', 'conv_b', 'conv_w_out', 'ssm_lam_re', 'ssm_lam_im', 'ssm_log_step', 'ssm_b_re', 'ssm_b_im', 'ssm_c_re', 'ssm_c_im', 'ssm_d', 'ssm_w_glu', 'mix_w_out', 'ln2_g', 'ln2_b', 'ffn2_w_in', 'ffn2_w_out', 'ln3_g', 'ln3_b', 'ple_w_in', 'ple_w_gate', 'ln4_g', 'ln4_b']
TWIN_WEIGHTS = ['ffn1_w_in', 'ffn1_w_out', 'ln1_g', 'ln1_b', 'mix_w_in', 'conv_w', 'conv_b', 'conv_w_out', 'ssm_lam_re', 'ssm_lam_im', 'ssm_log_step', 'ssm_b_re', 'ssm_b_im', 'ssm_c_re', 'ssm_c_im', 'ssm_d', 'ssm_w_glu', 'mix_w_out', 'ln2_g', 'ln2_b', 'ffn2_w_in', 'ffn2_w_out', 'ln3_g', 'ln3_b', 'ple_w_in', 'ple_w_gate', 'ln4_g', 'ln4_b']
TWIN_DIFF_INPUT = 'x'
TWIN_INPUTS = ['x', 'p', 'ffn1_w_in', 'ffn1_w_out', 'ln1_g', 'ln1_b', 'mix_w_in', 'conv_w', 'conv_b', 'conv_w_out', 'ssm_lam_re', 'ssm_lam_im', 'ssm_log_step', 'ssm_b_re', 'ssm_b_im', 'ssm_c_re', 'ssm_c_im', 'ssm_d', 'ssm_w_glu', 'mix_w_out', 'ln2_g', 'ln2_b', 'ffn2_w_in', 'ffn2_w_out', 'ln3_g', 'ln3_b', 'ple_w_in', 'ple_w_gate', 'ln4_g', 'ln4_b', 'loss_target', 'm_ffn1_w_in', 'm_ffn1_w_out', 'm_ln1_g', 'm_ln1_b', 'm_mix_w_in', 'm_conv_w', 'm_conv_b', 'm_conv_w_out', 'm_ssm_lam_re', 'm_ssm_lam_im', 'm_ssm_log_step', 'm_ssm_b_re', 'm_ssm_b_im', 'm_ssm_c_re', 'm_ssm_c_im', 'm_ssm_d', 'm_ssm_w_glu', 'm_mix_w_out', 'm_ln2_g', 'm_ln2_b', 'm_ffn2_w_in', 'm_ffn2_w_out', 'm_ln3_g', 'm_ln3_b', 'm_ple_w_in', 'm_ple_w_gate', 'm_ln4_g', 'm_ln4_b', 'v_ffn1_w_in', 'v_ffn1_w_out', 'v_ln1_g', 'v_ln1_b', 'v_mix_w_in', 'v_conv_w', 'v_conv_b', 'v_conv_w_out', 'v_ssm_lam_re', 'v_ssm_lam_im', 'v_ssm_log_step', 'v_ssm_b_re', 'v_ssm_b_im', 'v_ssm_c_re', 'v_ssm_c_im', 'v_ssm_d', 'v_ssm_w_glu', 'v_mix_w_out', 'v_ln2_g', 'v_ln2_b', 'v_ffn2_w_in', 'v_ffn2_w_out', 'v_ln3_g', 'v_ln3_b', 'v_ple_w_in', 'v_ple_w_gate', 'v_ln4_g', 'v_ln4_b']
TWIN_OUTPUTS = ['loss', 'grad_x', 'grad_ffn1_w_in', 'grad_ffn1_w_out', 'grad_ln1_g', 'grad_ln1_b', 'grad_mix_w_in', 'grad_conv_w', 'grad_conv_b', 'grad_conv_w_out', 'grad_ssm_lam_re', 'grad_ssm_lam_im', 'grad_ssm_log_step', 'grad_ssm_b_re', 'grad_ssm_b_im', 'grad_ssm_c_re', 'grad_ssm_c_im', 'grad_ssm_d', 'grad_ssm_w_glu', 'grad_mix_w_out', 'grad_ln2_g', 'grad_ln2_b', 'grad_ffn2_w_in', 'grad_ffn2_w_out', 'grad_ln3_g', 'grad_ln3_b', 'grad_ple_w_in', 'grad_ple_w_gate', 'grad_ln4_g', 'grad_ln4_b', 'delta_ffn1_w_in', 'delta_ffn1_w_out', 'delta_ln1_g', 'delta_ln1_b', 'delta_mix_w_in', 'delta_conv_w', 'delta_conv_b', 'delta_conv_w_out', 'delta_ssm_lam_re', 'delta_ssm_lam_im', 'delta_ssm_log_step', 'delta_ssm_b_re', 'delta_ssm_b_im', 'delta_ssm_c_re', 'delta_ssm_c_im', 'delta_ssm_d', 'delta_ssm_w_glu', 'delta_mix_w_out', 'delta_ln2_g', 'delta_ln2_b', 'delta_ffn2_w_in', 'delta_ffn2_w_out', 'delta_ln3_g', 'delta_ln3_b', 'delta_ple_w_in', 'delta_ple_w_gate', 'delta_ln4_g', 'delta_ln4_b', 'new_m_ffn1_w_in', 'new_m_ffn1_w_out', 'new_m_ln1_g', 'new_m_ln1_b', 'new_m_mix_w_in', 'new_m_conv_w', 'new_m_conv_b', 'new_m_conv_w_out', 'new_m_ssm_lam_re', 'new_m_ssm_lam_im', 'new_m_ssm_log_step', 'new_m_ssm_b_re', 'new_m_ssm_b_im', 'new_m_ssm_c_re', 'new_m_ssm_c_im', 'new_m_ssm_d', 'new_m_ssm_w_glu', 'new_m_mix_w_out', 'new_m_ln2_g', 'new_m_ln2_b', 'new_m_ffn2_w_in', 'new_m_ffn2_w_out', 'new_m_ln3_g', 'new_m_ln3_b', 'new_m_ple_w_in', 'new_m_ple_w_gate', 'new_m_ln4_g', 'new_m_ln4_b', 'new_v_ffn1_w_in', 'new_v_ffn1_w_out', 'new_v_ln1_g', 'new_v_ln1_b', 'new_v_mix_w_in', 'new_v_conv_w', 'new_v_conv_b', 'new_v_conv_w_out', 'new_v_ssm_lam_re', 'new_v_ssm_lam_im', 'new_v_ssm_log_step', 'new_v_ssm_b_re', 'new_v_ssm_b_im', 'new_v_ssm_c_re', 'new_v_ssm_c_im', 'new_v_ssm_d', 'new_v_ssm_w_glu', 'new_v_mix_w_out', 'new_v_ln2_g', 'new_v_ln2_b', 'new_v_ffn2_w_in', 'new_v_ffn2_w_out', 'new_v_ln3_g', 'new_v_ln3_b', 'new_v_ple_w_in', 'new_v_ple_w_gate', 'new_v_ln4_g', 'new_v_ln4_b']
TWIN_LEAF_KINDS = {'loss': 'loss', 'grad_x': 'grad_x', 'grad_ffn1_w_in': 'grad_w', 'grad_ffn1_w_out': 'grad_w', 'grad_ln1_g': 'grad_w', 'grad_ln1_b': 'grad_w', 'grad_mix_w_in': 'grad_w', 'grad_conv_w': 'grad_w', 'grad_conv_b': 'grad_w', 'grad_conv_w_out': 'grad_w', 'grad_ssm_lam_re': 'grad_w', 'grad_ssm_lam_im': 'grad_w', 'grad_ssm_log_step': 'grad_w', 'grad_ssm_b_re': 'grad_w', 'grad_ssm_b_im': 'grad_w', 'grad_ssm_c_re': 'grad_w', 'grad_ssm_c_im': 'grad_w', 'grad_ssm_d': 'grad_w', 'grad_ssm_w_glu': 'grad_w', 'grad_mix_w_out': 'grad_w', 'grad_ln2_g': 'grad_w', 'grad_ln2_b': 'grad_w', 'grad_ffn2_w_in': 'grad_w', 'grad_ffn2_w_out': 'grad_w', 'grad_ln3_g': 'grad_w', 'grad_ln3_b': 'grad_w', 'grad_ple_w_in': 'grad_w', 'grad_ple_w_gate': 'grad_w', 'grad_ln4_g': 'grad_w', 'grad_ln4_b': 'grad_w', 'delta_ffn1_w_in': 'delta_w', 'delta_ffn1_w_out': 'delta_w', 'delta_ln1_g': 'delta_w', 'delta_ln1_b': 'delta_w', 'delta_mix_w_in': 'delta_w', 'delta_conv_w': 'delta_w', 'delta_conv_b': 'delta_w', 'delta_conv_w_out': 'delta_w', 'delta_ssm_lam_re': 'delta_w', 'delta_ssm_lam_im': 'delta_w', 'delta_ssm_log_step': 'delta_w', 'delta_ssm_b_re': 'delta_w', 'delta_ssm_b_im': 'delta_w', 'delta_ssm_c_re': 'delta_w', 'delta_ssm_c_im': 'delta_w', 'delta_ssm_d': 'delta_w', 'delta_ssm_w_glu': 'delta_w', 'delta_mix_w_out': 'delta_w', 'delta_ln2_g': 'delta_w', 'delta_ln2_b': 'delta_w', 'delta_ffn2_w_in': 'delta_w', 'delta_ffn2_w_out': 'delta_w', 'delta_ln3_g': 'delta_w', 'delta_ln3_b': 'delta_w', 'delta_ple_w_in': 'delta_w', 'delta_ple_w_gate': 'delta_w', 'delta_ln4_g': 'delta_w', 'delta_ln4_b': 'delta_w', 'new_m_ffn1_w_in': 'new_m', 'new_m_ffn1_w_out': 'new_m', 'new_m_ln1_g': 'new_m', 'new_m_ln1_b': 'new_m', 'new_m_mix_w_in': 'new_m', 'new_m_conv_w': 'new_m', 'new_m_conv_b': 'new_m', 'new_m_conv_w_out': 'new_m', 'new_m_ssm_lam_re': 'new_m', 'new_m_ssm_lam_im': 'new_m', 'new_m_ssm_log_step': 'new_m', 'new_m_ssm_b_re': 'new_m', 'new_m_ssm_b_im': 'new_m', 'new_m_ssm_c_re': 'new_m', 'new_m_ssm_c_im': 'new_m', 'new_m_ssm_d': 'new_m', 'new_m_ssm_w_glu': 'new_m', 'new_m_mix_w_out': 'new_m', 'new_m_ln2_g': 'new_m', 'new_m_ln2_b': 'new_m', 'new_m_ffn2_w_in': 'new_m', 'new_m_ffn2_w_out': 'new_m', 'new_m_ln3_g': 'new_m', 'new_m_ln3_b': 'new_m', 'new_m_ple_w_in': 'new_m', 'new_m_ple_w_gate': 'new_m', 'new_m_ln4_g': 'new_m', 'new_m_ln4_b': 'new_m', 'new_v_ffn1_w_in': 'new_v', 'new_v_ffn1_w_out': 'new_v', 'new_v_ln1_g': 'new_v', 'new_v_ln1_b': 'new_v', 'new_v_mix_w_in': 'new_v', 'new_v_conv_w': 'new_v', 'new_v_conv_b': 'new_v', 'new_v_conv_w_out': 'new_v', 'new_v_ssm_lam_re': 'new_v', 'new_v_ssm_lam_im': 'new_v', 'new_v_ssm_log_step': 'new_v', 'new_v_ssm_b_re': 'new_v', 'new_v_ssm_b_im': 'new_v', 'new_v_ssm_c_re': 'new_v', 'new_v_ssm_c_im': 'new_v', 'new_v_ssm_d': 'new_v', 'new_v_ssm_w_glu': 'new_v', 'new_v_mix_w_out': 'new_v', 'new_v_ln2_g': 'new_v', 'new_v_ln2_b': 'new_v', 'new_v_ffn2_w_in': 'new_v', 'new_v_ffn2_w_out': 'new_v', 'new_v_ln3_g': 'new_v', 'new_v_ln3_b': 'new_v', 'new_v_ple_w_in': 'new_v', 'new_v_ple_w_gate': 'new_v', 'new_v_ln4_g': 'new_v', 'new_v_ln4_b': 'new_v'}


def _forward(args):
    return _fwd_reference(*[args[k] for k in FWD_PARAMS])


def _output_shape():
    out = _jax.eval_shape(lambda: _forward(_fwd_setup_inputs(0)))
    return out.shape, out.dtype

N_MICROBATCH = 1
ADAM_LR = 0.001
ADAM_B1 = 0.9
ADAM_B2 = 0.999
ADAM_EPS = 1e-08
ADAM_WD = 0.01
ADAM_STEP = 10
PER_EXAMPLE_BATCH_AXIS = {'x': 0, 'p': 1, 'loss_target': 0}
SHARED_INPUTS = []
_WEIGHT_DTYPES = {'ffn1_w_in': _jnp.float32, 'ffn1_w_out': _jnp.float32, 'ln1_g': _jnp.float32, 'ln1_b': _jnp.float32, 'mix_w_in': _jnp.float32, 'conv_w': _jnp.float32, 'conv_b': _jnp.float32, 'conv_w_out': _jnp.float32, 'ssm_lam_re': _jnp.float32, 'ssm_lam_im': _jnp.float32, 'ssm_log_step': _jnp.float32, 'ssm_b_re': _jnp.float32, 'ssm_b_im': _jnp.float32, 'ssm_c_re': _jnp.float32, 'ssm_c_im': _jnp.float32, 'ssm_d': _jnp.float32, 'ssm_w_glu': _jnp.float32, 'mix_w_out': _jnp.float32, 'ln2_g': _jnp.float32, 'ln2_b': _jnp.float32, 'ffn2_w_in': _jnp.float32, 'ffn2_w_out': _jnp.float32, 'ln3_g': _jnp.float32, 'ln3_b': _jnp.float32, 'ple_w_in': _jnp.float32, 'ple_w_gate': _jnp.float32, 'ln4_g': _jnp.float32, 'ln4_b': _jnp.float32}
MOMENT_SCALE = {'ffn1_w_in': 1.141344e-02, 'ffn1_w_out': 3.130289e-02, 'ln1_g': 2.574547e-01, 'ln1_b': 1.554615e-01, 'mix_w_in': 1.763167e-02, 'conv_w': 2.802431e-02, 'conv_b': 2.985511e-02, 'conv_w_out': 3.256065e-02, 'ssm_lam_re': 3.844132e-04, 'ssm_lam_im': 4.527088e-04, 'ssm_log_step': 2.811757e-01, 'ssm_b_re': 2.848860e-04, 'ssm_b_im': 2.885468e-04, 'ssm_c_re': 5.845631e-04, 'ssm_c_im': 5.783806e-04, 'ssm_d': 1.083278e-02, 'ssm_w_glu': 8.099066e-03, 'mix_w_out': 3.426867e-02, 'ln2_g': 2.638524e-01, 'ln2_b': 1.585432e-01, 'ffn2_w_in': 1.098794e-02, 'ffn2_w_out': 3.010735e-02, 'ln3_g': 2.690462e-01, 'ln3_b': 1.586683e-01, 'ple_w_in': 5.519528e-02, 'ple_w_gate': 1.277151e-02, 'ln4_g': 1.600323e+01, 'ln4_b': 3.959698e-01}


def _to_microbatches(a, axis):
    t = _jnp.moveaxis(a, axis, 0)
    t = t.reshape((N_MICROBATCH, t.shape[0] // N_MICROBATCH) + t.shape[1:])
    return _jnp.moveaxis(t, 1, axis + 1)


def setup_inputs(seed: int = 0) -> dict:
    inp = _fwd_setup_inputs(seed)
    key = _jax.random.fold_in(_jax.random.key(seed), 7919)
    shape, _ = _output_shape()
    out = dict(inp)
    out["loss_target"] = _jax.random.normal(_jax.random.fold_in(key, 0), shape, _jnp.float32)
    for i, name in enumerate(TWIN_WEIGHTS):
        w = inp[name].astype(_jnp.float32)
        if MOMENT_SCALE is None:
            s = _jnp.sqrt(_jnp.mean(_jnp.square(w)) + 1e-30)
        else:
            s = MOMENT_SCALE[name]
        km, kv = _jax.random.split(_jax.random.fold_in(key, i + 1))
        out[name] = w
        out["m_" + name] = s * _jax.random.normal(km, w.shape, _jnp.float32)
        out["v_" + name] = (s * s) * _jax.random.uniform(kv, w.shape, _jnp.float32, 0.5, 1.5)
    if N_MICROBATCH > 1:
        for name, axis in PER_EXAMPLE_BATCH_AXIS.items():
            out[name] = _to_microbatches(out[name], axis)
    return {'x': out['x'], 'p': out['p'], 'ffn1_w_in': out['ffn1_w_in'], 'ffn1_w_out': out['ffn1_w_out'], 'ln1_g': out['ln1_g'], 'ln1_b': out['ln1_b'], 'mix_w_in': out['mix_w_in'], 'conv_w': out['conv_w'], 'conv_b': out['conv_b'], 'conv_w_out': out['conv_w_out'], 'ssm_lam_re': out['ssm_lam_re'], 'ssm_lam_im': out['ssm_lam_im'], 'ssm_log_step': out['ssm_log_step'], 'ssm_b_re': out['ssm_b_re'], 'ssm_b_im': out['ssm_b_im'], 'ssm_c_re': out['ssm_c_re'], 'ssm_c_im': out['ssm_c_im'], 'ssm_d': out['ssm_d'], 'ssm_w_glu': out['ssm_w_glu'], 'mix_w_out': out['mix_w_out'], 'ln2_g': out['ln2_g'], 'ln2_b': out['ln2_b'], 'ffn2_w_in': out['ffn2_w_in'], 'ffn2_w_out': out['ffn2_w_out'], 'ln3_g': out['ln3_g'], 'ln3_b': out['ln3_b'], 'ple_w_in': out['ple_w_in'], 'ple_w_gate': out['ple_w_gate'], 'ln4_g': out['ln4_g'], 'ln4_b': out['ln4_b'], 'loss_target': out['loss_target'], 'm_ffn1_w_in': out['m_ffn1_w_in'], 'm_ffn1_w_out': out['m_ffn1_w_out'], 'm_ln1_g': out['m_ln1_g'], 'm_ln1_b': out['m_ln1_b'], 'm_mix_w_in': out['m_mix_w_in'], 'm_conv_w': out['m_conv_w'], 'm_conv_b': out['m_conv_b'], 'm_conv_w_out': out['m_conv_w_out'], 'm_ssm_lam_re': out['m_ssm_lam_re'], 'm_ssm_lam_im': out['m_ssm_lam_im'], 'm_ssm_log_step': out['m_ssm_log_step'], 'm_ssm_b_re': out['m_ssm_b_re'], 'm_ssm_b_im': out['m_ssm_b_im'], 'm_ssm_c_re': out['m_ssm_c_re'], 'm_ssm_c_im': out['m_ssm_c_im'], 'm_ssm_d': out['m_ssm_d'], 'm_ssm_w_glu': out['m_ssm_w_glu'], 'm_mix_w_out': out['m_mix_w_out'], 'm_ln2_g': out['m_ln2_g'], 'm_ln2_b': out['m_ln2_b'], 'm_ffn2_w_in': out['m_ffn2_w_in'], 'm_ffn2_w_out': out['m_ffn2_w_out'], 'm_ln3_g': out['m_ln3_g'], 'm_ln3_b': out['m_ln3_b'], 'm_ple_w_in': out['m_ple_w_in'], 'm_ple_w_gate': out['m_ple_w_gate'], 'm_ln4_g': out['m_ln4_g'], 'm_ln4_b': out['m_ln4_b'], 'v_ffn1_w_in': out['v_ffn1_w_in'], 'v_ffn1_w_out': out['v_ffn1_w_out'], 'v_ln1_g': out['v_ln1_g'], 'v_ln1_b': out['v_ln1_b'], 'v_mix_w_in': out['v_mix_w_in'], 'v_conv_w': out['v_conv_w'], 'v_conv_b': out['v_conv_b'], 'v_conv_w_out': out['v_conv_w_out'], 'v_ssm_lam_re': out['v_ssm_lam_re'], 'v_ssm_lam_im': out['v_ssm_lam_im'], 'v_ssm_log_step': out['v_ssm_log_step'], 'v_ssm_b_re': out['v_ssm_b_re'], 'v_ssm_b_im': out['v_ssm_b_im'], 'v_ssm_c_re': out['v_ssm_c_re'], 'v_ssm_c_im': out['v_ssm_c_im'], 'v_ssm_d': out['v_ssm_d'], 'v_ssm_w_glu': out['v_ssm_w_glu'], 'v_mix_w_out': out['v_mix_w_out'], 'v_ln2_g': out['v_ln2_g'], 'v_ln2_b': out['v_ln2_b'], 'v_ffn2_w_in': out['v_ffn2_w_in'], 'v_ffn2_w_out': out['v_ffn2_w_out'], 'v_ln3_g': out['v_ln3_g'], 'v_ln3_b': out['v_ln3_b'], 'v_ple_w_in': out['v_ple_w_in'], 'v_ple_w_gate': out['v_ple_w_gate'], 'v_ln4_g': out['v_ln4_g'], 'v_ln4_b': out['v_ln4_b']}


def _loss(weights, diff, rest, loss_target):
    with _jax.named_scope("forward"):
        args = {**rest, TWIN_DIFF_INPUT: diff, **{k: w.astype(_WEIGHT_DTYPES[k]) for k, w in weights.items()}}
        y = _forward(args)
    with _jax.named_scope("loss_head"):
        err = _jnp.square(y.astype(_jnp.float32) - loss_target)
        return 0.5 * _jnp.sum(_jnp.mean(err, axis=-1)) if err.ndim else 0.5 * err


def _adamw(w, g, m, v):
    m = ADAM_B1 * m + (1.0 - ADAM_B1) * g
    v = ADAM_B2 * v + (1.0 - ADAM_B2) * _jnp.square(g)
    m_hat = m / (1.0 - ADAM_B1 ** ADAM_STEP)
    v_hat = v / (1.0 - ADAM_B2 ** ADAM_STEP)
    delta = -ADAM_LR * (m_hat / (_jnp.sqrt(v_hat) + ADAM_EPS) + ADAM_WD * w)
    return delta, m, v


def reference(x, p, ffn1_w_in, ffn1_w_out, ln1_g, ln1_b, mix_w_in, conv_w, conv_b, conv_w_out, ssm_lam_re, ssm_lam_im, ssm_log_step, ssm_b_re, ssm_b_im, ssm_c_re, ssm_c_im, ssm_d, ssm_w_glu, mix_w_out, ln2_g, ln2_b, ffn2_w_in, ffn2_w_out, ln3_g, ln3_b, ple_w_in, ple_w_gate, ln4_g, ln4_b, loss_target, m_ffn1_w_in, m_ffn1_w_out, m_ln1_g, m_ln1_b, m_mix_w_in, m_conv_w, m_conv_b, m_conv_w_out, m_ssm_lam_re, m_ssm_lam_im, m_ssm_log_step, m_ssm_b_re, m_ssm_b_im, m_ssm_c_re, m_ssm_c_im, m_ssm_d, m_ssm_w_glu, m_mix_w_out, m_ln2_g, m_ln2_b, m_ffn2_w_in, m_ffn2_w_out, m_ln3_g, m_ln3_b, m_ple_w_in, m_ple_w_gate, m_ln4_g, m_ln4_b, v_ffn1_w_in, v_ffn1_w_out, v_ln1_g, v_ln1_b, v_mix_w_in, v_conv_w, v_conv_b, v_conv_w_out, v_ssm_lam_re, v_ssm_lam_im, v_ssm_log_step, v_ssm_b_re, v_ssm_b_im, v_ssm_c_re, v_ssm_c_im, v_ssm_d, v_ssm_w_glu, v_mix_w_out, v_ln2_g, v_ln2_b, v_ffn2_w_in, v_ffn2_w_out, v_ln3_g, v_ln3_b, v_ple_w_in, v_ple_w_gate, v_ln4_g, v_ln4_b):
    given = dict(x=x, p=p, ffn1_w_in=ffn1_w_in, ffn1_w_out=ffn1_w_out, ln1_g=ln1_g, ln1_b=ln1_b, mix_w_in=mix_w_in, conv_w=conv_w, conv_b=conv_b, conv_w_out=conv_w_out, ssm_lam_re=ssm_lam_re, ssm_lam_im=ssm_lam_im, ssm_log_step=ssm_log_step, ssm_b_re=ssm_b_re, ssm_b_im=ssm_b_im, ssm_c_re=ssm_c_re, ssm_c_im=ssm_c_im, ssm_d=ssm_d, ssm_w_glu=ssm_w_glu, mix_w_out=mix_w_out, ln2_g=ln2_g, ln2_b=ln2_b, ffn2_w_in=ffn2_w_in, ffn2_w_out=ffn2_w_out, ln3_g=ln3_g, ln3_b=ln3_b, ple_w_in=ple_w_in, ple_w_gate=ple_w_gate, ln4_g=ln4_g, ln4_b=ln4_b, loss_target=loss_target, m_ffn1_w_in=m_ffn1_w_in, m_ffn1_w_out=m_ffn1_w_out, m_ln1_g=m_ln1_g, m_ln1_b=m_ln1_b, m_mix_w_in=m_mix_w_in, m_conv_w=m_conv_w, m_conv_b=m_conv_b, m_conv_w_out=m_conv_w_out, m_ssm_lam_re=m_ssm_lam_re, m_ssm_lam_im=m_ssm_lam_im, m_ssm_log_step=m_ssm_log_step, m_ssm_b_re=m_ssm_b_re, m_ssm_b_im=m_ssm_b_im, m_ssm_c_re=m_ssm_c_re, m_ssm_c_im=m_ssm_c_im, m_ssm_d=m_ssm_d, m_ssm_w_glu=m_ssm_w_glu, m_mix_w_out=m_mix_w_out, m_ln2_g=m_ln2_g, m_ln2_b=m_ln2_b, m_ffn2_w_in=m_ffn2_w_in, m_ffn2_w_out=m_ffn2_w_out, m_ln3_g=m_ln3_g, m_ln3_b=m_ln3_b, m_ple_w_in=m_ple_w_in, m_ple_w_gate=m_ple_w_gate, m_ln4_g=m_ln4_g, m_ln4_b=m_ln4_b, v_ffn1_w_in=v_ffn1_w_in, v_ffn1_w_out=v_ffn1_w_out, v_ln1_g=v_ln1_g, v_ln1_b=v_ln1_b, v_mix_w_in=v_mix_w_in, v_conv_w=v_conv_w, v_conv_b=v_conv_b, v_conv_w_out=v_conv_w_out, v_ssm_lam_re=v_ssm_lam_re, v_ssm_lam_im=v_ssm_lam_im, v_ssm_log_step=v_ssm_log_step, v_ssm_b_re=v_ssm_b_re, v_ssm_b_im=v_ssm_b_im, v_ssm_c_re=v_ssm_c_re, v_ssm_c_im=v_ssm_c_im, v_ssm_d=v_ssm_d, v_ssm_w_glu=v_ssm_w_glu, v_mix_w_out=v_mix_w_out, v_ln2_g=v_ln2_g, v_ln2_b=v_ln2_b, v_ffn2_w_in=v_ffn2_w_in, v_ffn2_w_out=v_ffn2_w_out, v_ln3_g=v_ln3_g, v_ln3_b=v_ln3_b, v_ple_w_in=v_ple_w_in, v_ple_w_gate=v_ple_w_gate, v_ln4_g=v_ln4_g, v_ln4_b=v_ln4_b)
    weights = {n: given[n] for n in TWIN_WEIGHTS}
    shared = {n: given[n] for n in SHARED_INPUTS}
    per_example = {n: given[n] for n in ['x', 'p']}
    grad_fn = _jax.value_and_grad(_loss, argnums=(0, 1))

    def one_microbatch(ex, loss_target):
        ex = dict(ex)
        diff = ex.pop(TWIN_DIFF_INPUT)
        return grad_fn(weights, diff, {**shared, **ex}, loss_target)

    if N_MICROBATCH == 1:
        loss, (grad_w, grad_x) = one_microbatch(per_example, given["loss_target"])
    else:
        def body(carry, xs):
            loss_sum, grad_sum = carry
            l_k, (gw_k, gx_k) = one_microbatch(xs[0], xs[1])
            with _jax.named_scope("update"):
                return (loss_sum + l_k, _jax.tree.map(_jnp.add, grad_sum, gw_k)), gx_k

        init = (_jnp.zeros((), _jnp.float32), _jax.tree.map(_jnp.zeros_like, weights))
        (loss, grad_w), grad_x = _jax.lax.scan(body, init, (per_example, given["loss_target"]))
    with _jax.named_scope("update"):
        delta_w, new_m, new_v = {}, {}, {}
        for n in TWIN_WEIGHTS:
            delta_w[n], new_m[n], new_v[n] = _adamw(weights[n], grad_w[n], given["m_" + n], given["v_" + n])
    return (loss, grad_x, *[grad_w[n] for n in TWIN_WEIGHTS], *[delta_w[n] for n in TWIN_WEIGHTS],
            *[new_m[n] for n in TWIN_WEIGHTS], *[new_v[n] for n in TWIN_WEIGHTS])
```

```python
import functools
import math

import jax
import jax.numpy as jnp
from jax import lax
from jax.experimental import pallas as pl
from jax.experimental.pallas import tpu as pltpu

F32 = jnp.float32
BF16 = jnp.bfloat16
MESH = pl.DeviceIdType.MESH

N_DEV = 8
ALPHA = 2.0 ** 0.25
LN_EPS = 1e-5
CONV_CH = 512
SSM_W = 512
SSM_GROUPS = 32
SSM_GROUP = 16
SSM_STATE = 64
SSM_CH = SSM_GROUPS * SSM_STATE
SCAN_CB = 512
SCAN_NCB = SSM_CH // SCAN_CB
SCAN_TT = 256
ADAM_LR = 0.001
ADAM_B1 = 0.9
ADAM_B2 = 0.999
ADAM_EPS = 1e-08
ADAM_WD = 0.01
ADAM_STEP = 10
VMEM_LIMIT = 56 * 1024 * 1024


def _cparams(sem=None, **kw):
    return pltpu.CompilerParams(dimension_semantics=sem, vmem_limit_bytes=VMEM_LIMIT, **kw)


def _mm(a, b, *, name, ja=None, jb=None, jo=None, a_flat=False, b_flat=False, o_flat=False,
        ta=False, tb=False, tm, tn, tk, nj=1, out_dtype=F32):
    def dims(arr, j, flat):
        if j is None:
            return arr.shape
        if flat:
            return (arr.shape[0], arr.shape[1] // nj)
        assert arr.shape[0] == nj, (name, arr.shape, nj)
        return arr.shape[1:]

    ar, ac = dims(a, ja, a_flat)
    br, bc = dims(b, jb, b_flat)
    m, k = (ac, ar) if ta else (ar, ac)
    k2, n = (bc, br) if tb else (br, bc)
    assert k == k2, (name, a.shape, b.shape)
    assert m % tm == 0 and n % tn == 0 and k % tk == 0, (name, m, n, k, tm, tn, tk)
    njb = nj if 'b' in (ja, jb) else 1
    njc = nj if 'c' in (ja, jb) else 1
    nk = k // tk
    nred = njc * nk
    grid = (njb, m // tm, n // tn, njc, nk)

    def make_spec(j, flat, blk, rfn, cfn, cols_per_j):
        def jsel(g, c):
            return g if j == 'b' else c
        if j is None:
            return pl.BlockSpec(blk, lambda g, i, jn, c, kk: (rfn(i, jn, kk), cfn(i, jn, kk)))
        if flat:
            nb = cols_per_j // blk[1]
            return pl.BlockSpec(blk, lambda g, i, jn, c, kk: (rfn(i, jn, kk), jsel(g, c) * nb + cfn(i, jn, kk)))
        return pl.BlockSpec((None,) + blk,
                            lambda g, i, jn, c, kk: (jsel(g, c), rfn(i, jn, kk), cfn(i, jn, kk)))

    if ta:
        a_spec = make_spec(ja, a_flat, (tk, tm), lambda i, jn, kk: kk, lambda i, jn, kk: i, ac)
    else:
        a_spec = make_spec(ja, a_flat, (tm, tk), lambda i, jn, kk: i, lambda i, jn, kk: kk, ac)
    if tb:
        b_spec = make_spec(jb, b_flat, (tn, tk), lambda i, jn, kk: jn, lambda i, jn, kk: kk, bc)
    else:
        b_spec = make_spec(jb, b_flat, (tk, tn), lambda i, jn, kk: kk, lambda i, jn, kk: jn, bc)
    o_spec = make_spec(jo, o_flat, (tm, tn), lambda i, jn, kk: i, lambda i, jn, kk: jn, n)
    if jo is None:
        out_shape = (m, n)
    elif o_flat:
        out_shape = (m, nj * n)
    else:
        out_shape = (nj, m, n)

    dn = (((0 if ta else 1,), (1 if tb else 0,)), ((), ()))

    def body(a_ref, b_ref, o_ref, *scratch):
        p = lax.dot_general(a_ref[...], b_ref[...], dn, preferred_element_type=F32)
        if nred == 1:
            o_ref[...] = p.astype(o_ref.dtype)
        else:
            acc = scratch[0]
            r = pl.program_id(3) * nk + pl.program_id(4)

            @pl.when(r == 0)
            def _():
                acc[...] = p

            @pl.when(r > 0)
            def _():
                acc[...] += p

            @pl.when(r == nred - 1)
            def _():
                o_ref[...] = acc[...].astype(o_ref.dtype)

    return pl.pallas_call(
        body, name=name, grid=grid, in_specs=[a_spec, b_spec], out_specs=o_spec,
        out_shape=jax.ShapeDtypeStruct(out_shape, out_dtype),
        scratch_shapes=[] if nred == 1 else [pltpu.VMEM((tm, tn), F32)],
        compiler_params=_cparams(("parallel", "parallel", "parallel", "arbitrary", "arbitrary")),
    )(a, b)


def _sigmoid(v):
    return 1.0 / (1.0 + jnp.exp(-v))


def _row_spec(tm, cols, colblk=0):
    return pl.BlockSpec((tm, cols), lambda i: (i, colblk))


def _vec_spec(cols):
    return pl.BlockSpec((1, cols), lambda i: (0, 0))


def _swiglu_fwd(h, *, name, tm=256):
    nj, t, w = h.shape
    half = nj // 2

    def body(g_ref, u_ref, a_ref):
        g = g_ref[...]
        a_ref[...] = (g * _sigmoid(g) * u_ref[...]).astype(BF16)

    return pl.pallas_call(
        body, name=name, grid=(half, t // tm),
        in_specs=[pl.BlockSpec((None, tm, w), lambda j, i: (j, i, 0)),
                  pl.BlockSpec((None, tm, w), lambda j, i: (j + half, i, 0))],
        out_specs=pl.BlockSpec((None, tm, w), lambda j, i: (j, i, 0)),
        out_shape=jax.ShapeDtypeStruct((half, t, w), BF16),
        compiler_params=_cparams(("parallel", "parallel")),
    )(h, h)


def _swiglu_bwd(da, h, *, name, tm=256):
    nj, t, w = h.shape
    half = nj // 2

    def body(da_ref, g_ref, u_ref, dh_ref):
        s = pl.program_id(0)
        g = g_ref[...]
        sg = _sigmoid(g)
        d = da_ref[...]
        dg = d * u_ref[...] * (sg * (1.0 + g * (1.0 - sg)))
        du = d * (g * sg)
        dh_ref[...] = jnp.where(s == 0, dg, du).astype(BF16)

    out = pl.pallas_call(
        body, name=name, grid=(2, half, t // tm),
        in_specs=[pl.BlockSpec((None, tm, w), lambda s, j, i: (j, i, 0)),
                  pl.BlockSpec((None, tm, w), lambda s, j, i: (j, i, 0)),
                  pl.BlockSpec((None, tm, w), lambda s, j, i: (j + half, i, 0))],
        out_specs=pl.BlockSpec((None, None, tm, w), lambda s, j, i: (s, j, i, 0)),
        out_shape=jax.ShapeDtypeStruct((2, half, t, w), BF16),
        compiler_params=_cparams(("parallel", "parallel", "parallel")),
    )(da, h, h)
    return out.reshape(nj, t, w)


def _ln_stats(r):
    mu = jnp.mean(r, axis=-1, keepdims=True)
    xc = r - mu
    var = jnp.mean(xc * xc, axis=-1, keepdims=True)
    rstd = lax.rsqrt(var + LN_EPS)
    return xc * rstd, rstd


def _ln_fwd(xin, fparts, gamma, beta, *, name, fs=1.0, ple=False, tm=256):
    t, d = xin.shape
    nf = len(fparts)

    def body(*refs):
        x_ref = refs[0]
        f_refs = refs[1:1 + nf]
        g_ref, b_ref, y_ref, yb_ref = refs[1 + nf:]
        if ple:
            f = f_refs[0][...] * _sigmoid(f_refs[1][...])
        else:
            f = fs * f_refs[0][...]
        xh, _ = _ln_stats(ALPHA * x_ref[...] + f)
        y = xh * g_ref[...] + b_ref[...]
        y_ref[...] = y
        yb_ref[...] = y.astype(BF16)

    return pl.pallas_call(
        body, name=name, grid=(t // tm,),
        in_specs=[_row_spec(tm, d)] * (1 + nf) + [_vec_spec(d), _vec_spec(d)],
        out_specs=[_row_spec(tm, d), _row_spec(tm, d)],
        out_shape=[jax.ShapeDtypeStruct((t, d), F32), jax.ShapeDtypeStruct((t, d), BF16)],
        compiler_params=_cparams(("parallel",)),
    )(xin, *fparts, gamma, beta)


def _ln_bwd(xin, fparts, gamma, beta, grads, *, name, fs=1.0, ple=False, target=None, tm=256):
    t, d = xin.shape
    nf = len(fparts)
    ng = len(grads)
    coefs = [c for _, c in grads]
    use_t = target is not None
    n_fout = 2 if ple else 1

    def body(*refs):
        pos = 0
        x_ref = refs[pos]; pos += 1
        f_refs = refs[pos:pos + nf]; pos += nf
        g_ref, b_ref = refs[pos:pos + 2]; pos += 2
        gr_refs = refs[pos:pos + ng]; pos += ng
        if use_t:
            t_ref = refs[pos]; pos += 1
        dr_ref = refs[pos]; pos += 1
        fo_refs = refs[pos:pos + n_fout]; pos += n_fout
        dg_ref, db_ref = refs[pos:pos + 2]; pos += 2
        if use_t:
            loss_ref = refs[pos]; pos += 1
        i = pl.program_id(0)

        if ple:
            pe = f_refs[0][...]
            sg = _sigmoid(f_refs[1][...])
            f = pe * sg
        else:
            f = fs * f_refs[0][...]
        xh, rstd = _ln_stats(ALPHA * x_ref[...] + f)
        gam = g_ref[...]
        if use_t:
            diff = xh * gam + b_ref[...] - t_ref[...]
            dy = diff * (1.0 / d)
            lpart = 0.5 * jnp.sum(jnp.sum(diff * diff, axis=-1, keepdims=True), axis=0, keepdims=True) * (1.0 / d)
        else:
            dy = coefs[0] * gr_refs[0][...]
            for c, r in zip(coefs[1:], gr_refs[1:]):
                dy = dy + c * r[...]
        dxh = dy * gam
        m1 = jnp.mean(dxh, axis=-1, keepdims=True)
        m2 = jnp.mean(dxh * xh, axis=-1, keepdims=True)
        dr = rstd * (dxh - m1 - xh * m2)
        dr_ref[...] = dr
        if ple:
            fo_refs[0][...] = (dr * sg).astype(BF16)
            fo_refs[1][...] = (dr * pe * (sg * (1.0 - sg))).astype(BF16)
        else:
            fo_refs[0][...] = (fs * dr).astype(BF16)
        dgp = jnp.sum(dy * xh, axis=0, keepdims=True)
        dbp = jnp.sum(dy, axis=0, keepdims=True)

        @pl.when(i == 0)
        def _():
            dg_ref[...] = dgp
            db_ref[...] = dbp
            if use_t:
                loss_ref[...] = jnp.broadcast_to(lpart, loss_ref.shape)

        @pl.when(i > 0)
        def _():
            dg_ref[...] += dgp
            db_ref[...] += dbp
            if use_t:
                loss_ref[...] += jnp.broadcast_to(lpart, loss_ref.shape)

    ins = [xin, *fparts, gamma, beta, *[g for g, _ in grads]] + ([target] if use_t else [])
    in_specs = ([_row_spec(tm, d)] * (1 + nf) + [_vec_spec(d), _vec_spec(d)] + [_row_spec(tm, d)] * ng
                + ([_row_spec(tm, d)] if use_t else []))
    out_specs = [_row_spec(tm, d)] * (1 + n_fout) + [_vec_spec(d), _vec_spec(d)] + ([_vec_spec(128)] if use_t else [])
    out_shape = ([jax.ShapeDtypeStruct((t, d), F32)] + [jax.ShapeDtypeStruct((t, d), BF16)] * n_fout
                 + [jax.ShapeDtypeStruct((1, d), F32)] * 2 + ([jax.ShapeDtypeStruct((1, 128), F32)] if use_t else []))
    return pl.pallas_call(
        body, name=name, grid=(t // tm,), in_specs=in_specs, out_specs=out_specs, out_shape=out_shape,
        compiler_params=_cparams(("arbitrary",)),
    )(*ins)


def _axpy(a, ca, b, *, name, tm=256):
    t, d = a.shape

    def body(a_ref, b_ref, o_ref):
        o_ref[...] = ca * a_ref[...] + b_ref[...]

    return pl.pallas_call(
        body, name=name, grid=(t // tm,), in_specs=[_row_spec(tm, d)] * 2, out_specs=_row_spec(tm, d),
        out_shape=jax.ShapeDtypeStruct((t, d), F32), compiler_params=_cparams(("parallel",)),
    )(a, b)


def _conv_fwd(proj, cw, cb, *, name, tm=256):
    t = proj.shape[0]
    c = CONV_CH
    hb = tm // 8

    def body(b_ref, c_ref, h_ref, cp_ref, hp_ref, w_ref, bias_ref, o_ref, q_scr):
        i = pl.program_id(0)
        q = c_ref[...] * h_ref[...]
        halo = jnp.where(i > 0, cp_ref[...] * hp_ref[...], 0.0)
        q_scr[0:8, :] = halo
        q_scr[8:, :] = q
        z = (w_ref[2:3, :] * q + w_ref[1:2, :] * q_scr[pl.ds(7, tm), :] + w_ref[0:1, :] * q_scr[pl.ds(6, tm), :]
             + bias_ref[...])
        o_ref[...] = (b_ref[...] * z).astype(BF16)

    prev = lambda blk: pl.BlockSpec((8, c), lambda i: (jnp.maximum(i * hb - 1, 0), blk))
    return pl.pallas_call(
        body, name=name, grid=(t // tm,),
        in_specs=[_row_spec(tm, c, 0), _row_spec(tm, c, 1), _row_spec(tm, c, 2), prev(1), prev(2),
                  pl.BlockSpec((8, c), lambda i: (0, 0)), _vec_spec(c)],
        out_specs=_row_spec(tm, c),
        out_shape=jax.ShapeDtypeStruct((t, c), BF16),
        scratch_shapes=[pltpu.VMEM((tm + 8, c), F32)],
        compiler_params=_cparams(("parallel",)),
    )(proj, proj, proj, proj, proj, cw, cb)


def _conv_bwd(dyc, proj, cw, cb, *, name, tm=256):
    t = proj.shape[0]
    c = CONV_CH
    hb = tm // 8
    nblk = t // 8

    def body(d_ref, b_ref, c_ref, h_ref, cp_ref, hp_ref, dn_ref, bn_ref, w_ref, bias_ref,
             db_ref, dc_ref, dh_ref, dw_ref, q_scr, z_scr):
        i = pl.program_id(0)
        last = pl.num_programs(0) - 1
        cc = c_ref[...]
        ch = h_ref[...]
        q = cc * ch
        halo = jnp.where(i > 0, cp_ref[...] * hp_ref[...], 0.0)
        q_scr[0:8, :] = halo
        q_scr[8:, :] = q
        w0, w1, w2 = w_ref[0:1, :], w_ref[1:2, :], w_ref[2:3, :]
        qm1 = q_scr[pl.ds(7, tm), :]
        qm2 = q_scr[pl.ds(6, tm), :]
        z = w2 * q + w1 * qm1 + w0 * qm2 + bias_ref[...]
        d = d_ref[...]
        bb = b_ref[...]
        db_ref[...] = (d * z).astype(BF16)
        dz = d * bb
        z_scr[0:tm, :] = dz
        z_scr[tm:, :] = jnp.where(i < last, dn_ref[...] * bn_ref[...], 0.0)
        dq = w2 * dz + w1 * z_scr[pl.ds(1, tm), :] + w0 * z_scr[pl.ds(2, tm), :]
        dc_ref[...] = (dq * ch).astype(BF16)
        dh_ref[...] = (dq * cc).astype(BF16)
        row = lax.broadcasted_iota(jnp.int32, (8, c), 0)
        part = jnp.zeros((8, c), F32)
        for k, term in enumerate((dz * qm2, dz * qm1, dz * q, dz)):
            part = jnp.where(row == k, jnp.sum(term, axis=0, keepdims=True), part)

        @pl.when(i == 0)
        def _():
            dw_ref[...] = part

        @pl.when(i > 0)
        def _():
            dw_ref[...] += part

    prev = lambda blk: pl.BlockSpec((8, c), lambda i: (jnp.maximum(i * hb - 1, 0), blk))
    nxt_p = pl.BlockSpec((8, c), lambda i: (jnp.minimum((i + 1) * hb, nblk - 1), 0))
    nxt_d = pl.BlockSpec((8, c), lambda i: (jnp.minimum((i + 1) * hb, nblk - 1), 0))
    return pl.pallas_call(
        body, name=name, grid=(t // tm,),
        in_specs=[_row_spec(tm, c), _row_spec(tm, c, 0), _row_spec(tm, c, 1), _row_spec(tm, c, 2),
                  prev(1), prev(2), nxt_d, nxt_p, pl.BlockSpec((8, c), lambda i: (0, 0)), _vec_spec(c)],
        out_specs=[_row_spec(tm, c)] * 3 + [pl.BlockSpec((8, c), lambda i: (0, 0))],
        out_shape=[jax.ShapeDtypeStruct((t, c), BF16)] * 3 + [jax.ShapeDtypeStruct((8, c), F32)],
        scratch_shapes=[pltpu.VMEM((tm + 8, c), F32), pltpu.VMEM((tm + 8, c), F32)],
        compiler_params=_cparams(("arbitrary",)),
    )(dyc, proj, proj, proj, proj, proj, dyc, proj, cw, cb)


def _gate_fwd(yconv, glu, proj, *, name, tm=256):
    t, d = yconv.shape

    def body(yc_ref, ga_ref, gb_ref, gc_ref, gs_ref, o_ref):
        yssm = ga_ref[...] * _sigmoid(gb_ref[...])
        o_ref[...] = (_sigmoid(gc_ref[...]) * yc_ref[...] + _sigmoid(gs_ref[...]) * yssm).astype(BF16)

    return pl.pallas_call(
        body, name=name, grid=(t // tm,),
        in_specs=[_row_spec(tm, d), _row_spec(tm, d, 0), _row_spec(tm, d, 1), _row_spec(tm, d, 2), _row_spec(tm, d, 3)],
        out_specs=_row_spec(tm, d), out_shape=jax.ShapeDtypeStruct((t, d), BF16),
        compiler_params=_cparams(("parallel",)),
    )(yconv, glu, glu, proj, proj)


def _gate_bwd(dm, yconv, glu, proj, *, name, tm=256):
    t, d = yconv.shape

    def body(dm_ref, yc_ref, ga_ref, gb_ref, gc_ref, gs_ref, dyc_ref, dglu_ref, dgate_ref):
        dmv = dm_ref[...]
        sc = _sigmoid(gc_ref[...])
        ss = _sigmoid(gs_ref[...])
        sb = _sigmoid(gb_ref[...])
        ga = ga_ref[...]
        yssm = ga * sb
        dyc_ref[...] = (dmv * sc).astype(BF16)
        dgate_ref[:, 0:d] = (dmv * yc_ref[...] * (sc * (1.0 - sc))).astype(BF16)
        dys = dmv * ss
        dgate_ref[:, d:2 * d] = (dmv * yssm * (ss * (1.0 - ss))).astype(BF16)
        dglu_ref[:, 0:d] = (dys * sb).astype(BF16)
        dglu_ref[:, d:2 * d] = (dys * ga * (sb * (1.0 - sb))).astype(BF16)

    return pl.pallas_call(
        body, name=name, grid=(t // tm,),
        in_specs=[_row_spec(tm, d), _row_spec(tm, d), _row_spec(tm, d, 0), _row_spec(tm, d, 1),
                  _row_spec(tm, d, 2), _row_spec(tm, d, 3)],
        out_specs=[_row_spec(tm, d), _row_spec(tm, 2 * d), _row_spec(tm, 2 * d)],
        out_shape=[jax.ShapeDtypeStruct((t, d), BF16), jax.ShapeDtypeStruct((t, 2 * d), BF16),
                   jax.ShapeDtypeStruct((t, 2 * d), BF16)],
        compiler_params=_cparams(("parallel",)),
    )(dm, yconv, glu, glu, proj, proj)


_GELU_C = math.sqrt(2.0 / math.pi)


def _gelu(v):
    return 0.5 * v * (1.0 + jnp.tanh(_GELU_C * (v + 0.044715 * v * v * v)))


def _gelu_grad(v):
    th = jnp.tanh(_GELU_C * (v + 0.044715 * v * v * v))
    return 0.5 * (1.0 + th) + 0.5 * v * (1.0 - th * th) * (_GELU_C * (1.0 + 3.0 * 0.044715 * v * v))


def _cmul(ar, ai, br, bi):
    return ar * br - ai * bi, ar * bi + ai * br


def _scan_fwd(proj, bmat, tab, *, name):
    t = proj.shape[0]
    tt, cbw = SCAN_TT, SCAN_CB
    w2 = 2 * cbw

    def body(u_ref, b_ref, tab_ref, s_ref, sb_ref, bu_scr, carry):
        ti = pl.program_id(1)

        @pl.when(ti == 0)
        def _():
            carry[...] = jnp.zeros_like(carry)

        bu_scr[...] = jnp.dot(u_ref[...].astype(BF16), b_ref[...], preferred_element_type=F32)
        row = lax.broadcasted_iota(jnp.int32, (8, cbw), 0)

        def blk(bi, c):
            cr, ci = c
            r0 = pl.multiple_of(bi * 8, 8)
            xr = bu_scr[pl.ds(r0, 8), 0:cbw]
            xi = bu_scr[pl.ds(r0, 8), cbw:w2]
            for k, sh in enumerate((1, 2, 4)):
                kr = tab_ref[k:k + 1, 0:cbw]
                ki = tab_ref[k:k + 1, cbw:w2]
                sr = jnp.where(row >= sh, pltpu.roll(xr, sh, 0), 0.0)
                si = jnp.where(row >= sh, pltpu.roll(xi, sh, 0), 0.0)
                pr, pi = _cmul(kr, ki, sr, si)
                xr = xr + pr
                xi = xi + pi
            pr, pi = _cmul(tab_ref[8:16, 0:cbw], tab_ref[8:16, cbw:w2], cr, ci)
            xr = xr + pr
            xi = xi + pi
            s_ref[pl.ds(r0, 8), 0:cbw] = xr
            s_ref[pl.ds(r0, 8), cbw:w2] = xi
            return (jnp.broadcast_to(xr[7:8, :], (8, cbw)), jnp.broadcast_to(xi[7:8, :], (8, cbw)))

        cr, ci = lax.fori_loop(0, tt // 8, blk, (carry[:, 0:cbw], carry[:, cbw:w2]))
        carry[:, 0:cbw] = cr
        carry[:, cbw:w2] = ci
        sb_ref[...] = s_ref[...].astype(BF16)

    return pl.pallas_call(
        body, name=name, grid=(SCAN_NCB, t // tt),
        in_specs=[pl.BlockSpec((tt, SSM_W), lambda cb, ti: (ti, 3)),
                  pl.BlockSpec((SSM_W, w2), lambda cb, ti: (0, cb)),
                  pl.BlockSpec((16, w2), lambda cb, ti: (0, cb))],
        out_specs=[pl.BlockSpec((tt, w2), lambda cb, ti: (ti, cb))] * 2,
        out_shape=[jax.ShapeDtypeStruct((t, 2 * SSM_CH), F32), jax.ShapeDtypeStruct((t, 2 * SSM_CH), BF16)],
        scratch_shapes=[pltpu.VMEM((tt, w2), F32), pltpu.VMEM((8, w2), F32)],
        compiler_params=_cparams(("parallel", "arbitrary")),
    )(proj, bmat, tab)


def _scan_bwd(dyb, cmat_t, s, tabb, *, name):
    t = s.shape[0]
    tt, cbw = SCAN_TT, SCAN_CB
    w2 = 2 * cbw
    nt = t // tt
    hb = tt // 8

    def body(dy_ref, c_ref, s_ref, sp_ref, tab_ref, h_ref, da_ref, g_scr, s_scr, carry):
        ti = pl.program_id(1)

        @pl.when(ti == 0)
        def _():
            carry[...] = jnp.zeros_like(carry)
            da_ref[...] = jnp.zeros_like(da_ref)

        g_scr[...] = jnp.dot(dy_ref[...], c_ref[...], preferred_element_type=F32)
        s_scr[0:8, :] = jnp.where(ti < nt - 1, sp_ref[...], 0.0)
        s_scr[8:, :] = s_ref[...]
        row = lax.broadcasted_iota(jnp.int32, (8, cbw), 0)

        def blk(k, c):
            cr, ci, ar, ai = c
            bi = hb - 1 - k
            r0 = pl.multiple_of(bi * 8, 8)
            xr = g_scr[pl.ds(r0, 8), 0:cbw]
            xi = g_scr[pl.ds(r0, 8), cbw:w2]
            for j, sh in enumerate((1, 2, 4)):
                kr = tab_ref[j:j + 1, 0:cbw]
                ki = tab_ref[j:j + 1, cbw:w2]
                sr = jnp.where(row < 8 - sh, pltpu.roll(xr, 8 - sh, 0), 0.0)
                si = jnp.where(row < 8 - sh, pltpu.roll(xi, 8 - sh, 0), 0.0)
                pr, pi = _cmul(kr, ki, sr, si)
                xr = xr + pr
                xi = xi + pi
            pr, pi = _cmul(tab_ref[8:16, 0:cbw], tab_ref[8:16, cbw:w2], cr, ci)
            xr = xr + pr
            xi = xi + pi
            h_ref[pl.ds(r0, 8), 0:cbw] = xr.astype(BF16)
            h_ref[pl.ds(r0, 8), cbw:w2] = xi.astype(BF16)
            pvr = s_scr[pl.ds(r0, 8), 0:cbw]
            pvi = s_scr[pl.ds(r0, 8), cbw:w2]
            cur_r = s_scr[pl.ds(r0 + 8, 8), 0:cbw]
            cur_i = s_scr[pl.ds(r0 + 8, 8), cbw:w2]
            spr = jnp.where(row == 0, jnp.broadcast_to(pvr[7:8, :], (8, cbw)), pltpu.roll(cur_r, 1, 0))
            spi = jnp.where(row == 0, jnp.broadcast_to(pvi[7:8, :], (8, cbw)), pltpu.roll(cur_i, 1, 0))
            ar = ar + spr * xr + spi * xi
            ai = ai + spr * xi - spi * xr
            return (jnp.broadcast_to(xr[0:1, :], (8, cbw)), jnp.broadcast_to(xi[0:1, :], (8, cbw)), ar, ai)

        z = jnp.zeros((8, cbw), F32)
        cr, ci, ar, ai = lax.fori_loop(0, hb, blk, (carry[:, 0:cbw], carry[:, cbw:w2], z, z))
        carry[:, 0:cbw] = cr
        carry[:, cbw:w2] = ci
        da_ref[:, 0:cbw] += ar
        da_ref[:, cbw:w2] += ai

    rt = lambda ti: nt - 1 - ti
    return pl.pallas_call(
        body, name=name, grid=(SCAN_NCB, nt),
        in_specs=[pl.BlockSpec((tt, SSM_W), lambda cb, ti: (rt(ti), 0)),
                  pl.BlockSpec((SSM_W, w2), lambda cb, ti: (0, cb)),
                  pl.BlockSpec((tt, w2), lambda cb, ti: (rt(ti), cb)),
                  pl.BlockSpec((8, w2), lambda cb, ti: (jnp.maximum(rt(ti) * hb - 1, 0), cb)),
                  pl.BlockSpec((16, w2), lambda cb, ti: (0, cb))],
        out_specs=[pl.BlockSpec((tt, w2), lambda cb, ti: (rt(ti), cb)),
                   pl.BlockSpec((8, w2), lambda cb, ti: (0, cb))],
        out_shape=[jax.ShapeDtypeStruct((t, 2 * SSM_CH), BF16), jax.ShapeDtypeStruct((8, 2 * SSM_CH), F32)],
        scratch_shapes=[pltpu.VMEM((tt, w2), F32), pltpu.VMEM((tt + 8, w2), F32), pltpu.VMEM((8, w2), F32)],
        compiler_params=_cparams(("parallel", "arbitrary")),
    )(dyb, cmat_t, s, s, tabb)


def _s5_out(ymm, proj, dvec, *, name, tm=256):
    t, w = ymm.shape

    def body(y_ref, u_ref, d_ref, yo_ref, sg_ref, ub_ref):
        u = u_ref[...]
        y = y_ref[...] + d_ref[...] * u
        yo_ref[...] = y
        sg_ref[...] = _gelu(y).astype(BF16)
        ub_ref[...] = u.astype(BF16)

    return pl.pallas_call(
        body, name=name, grid=(t // tm,),
        in_specs=[_row_spec(tm, w), _row_spec(tm, w, 3), _vec_spec(w)],
        out_specs=[_row_spec(tm, w)] * 3,
        out_shape=[jax.ShapeDtypeStruct((t, w), F32), jax.ShapeDtypeStruct((t, w), BF16), jax.ShapeDtypeStruct((t, w), BF16)],
        compiler_params=_cparams(("parallel",)),
    )(ymm, proj, dvec)


def _s5_bwd_in(dsg, y, proj, *, name, tm=256):
    t, w = y.shape

    def body(d_ref, y_ref, u_ref, dy_ref, dyb_ref, dd_ref):
        i = pl.program_id(0)
        dy = d_ref[...] * _gelu_grad(y_ref[...])
        dy_ref[...] = dy
        dyb_ref[...] = dy.astype(BF16)
        part = jnp.sum(dy * u_ref[...], axis=0, keepdims=True)

        @pl.when(i == 0)
        def _():
            dd_ref[...] = part

        @pl.when(i > 0)
        def _():
            dd_ref[...] += part

    return pl.pallas_call(
        body, name=name, grid=(t // tm,),
        in_specs=[_row_spec(tm, w), _row_spec(tm, w), _row_spec(tm, w, 3)],
        out_specs=[_row_spec(tm, w), _row_spec(tm, w), _vec_spec(w)],
        out_shape=[jax.ShapeDtypeStruct((t, w), F32), jax.ShapeDtypeStruct((t, w), BF16), jax.ShapeDtypeStruct((1, w), F32)],
        compiler_params=_cparams(("arbitrary",)),
    )(dsg, y, proj)


def _s5_du(dumm, dy, dvec, *, name, tm=256):
    t, w = dy.shape

    def body(a_ref, dy_ref, d_ref, o_ref):
        o_ref[...] = (a_ref[...] + d_ref[...] * dy_ref[...]).astype(BF16)

    return pl.pallas_call(
        body, name=name, grid=(t // tm,), in_specs=[_row_spec(tm, w), _row_spec(tm, w), _vec_spec(w)],
        out_specs=_row_spec(tm, w), out_shape=jax.ShapeDtypeStruct((t, w), BF16),
        compiler_params=_cparams(("parallel",)),
    )(dumm, dy, dvec)


def _s5_discretise(lam_re, lam_im, log_step, b_re, b_im):
    lam = lax.complex(lam_re, lam_im)
    dt = jnp.exp(log_step)[:, None]
    a = jnp.exp(lam * dt)
    bbar = ((a - 1.0) / lam)[..., None] * lax.complex(b_re, b_im)
    return jnp.real(a), jnp.imag(a), jnp.real(bbar), jnp.imag(bbar)


def _perm_cols(re, im):
    lead = re.shape[:-1]
    r = re.reshape(lead + (SCAN_NCB, 1, SCAN_CB))
    i = im.reshape(lead + (SCAN_NCB, 1, SCAN_CB))
    return jnp.concatenate([r, i], axis=-2).reshape(lead + (2 * SSM_CH,))


def _unperm_cols(x):
    lead = x.shape[:-1]
    y = x.reshape(lead + (SCAN_NCB, 2, SCAN_CB))
    return y[..., 0, :].reshape(lead + (SSM_CH,)), y[..., 1, :].reshape(lead + (SSM_CH,))


def _block_diag(x):
    g, r, c = x.shape
    eye = jnp.eye(g, dtype=x.dtype)
    return (eye[:, None, :, None] * x[:, :, None, :]).reshape(g * r, g * c)


def _block_diag_extract(x, r, c):
    g = x.shape[0] // r
    eye = jnp.eye(g, dtype=x.dtype)
    return jnp.sum(x.reshape(g, r, g, c) * eye[:, None, :, None], axis=2)


def _pow_table(ar, ai, descending=False):
    ar = ar.reshape(1, SSM_CH)
    ai = ai.reshape(1, SSM_CH)
    pw = [(ar, ai)]
    for _ in range(7):
        pw.append(_cmul(pw[-1][0], pw[-1][1], ar, ai))
    zero = (jnp.zeros_like(ar), jnp.zeros_like(ar))
    rows = [pw[0], pw[1], pw[3]] + [zero] * 5 + (pw[::-1] if descending else pw)
    re = jnp.concatenate([r for r, _ in rows], axis=0)
    im = jnp.concatenate([i for _, i in rows], axis=0)
    return _perm_cols(re, im)


def _place():
    x, y, c = lax.axis_index("x"), lax.axis_index("y"), lax.axis_index("c")
    chips = [(1 - x, y), (x, 1 - y), (1 - x, 1 - y)]
    return x, y, c, chips


def _dev(px, py, pc):
    return 4 * px + 2 * py + pc


def _all_gather(shards, *, name):
    n = len(shards)
    any_spec = pl.BlockSpec(memory_space=pl.ANY)

    def body(*refs):
        ins, outs = refs[:n], refs[n:2 * n]
        send, recv, lsem = refs[2 * n:]
        x, y, c, chips = _place()
        me, sib = (x, y, c), (x, y, 1 - c)

        def copy(w, k, block, to, src=None):
            dst = outs[w].at[_dev(*block)]
            return pltpu.make_async_remote_copy(
                src_ref=dst if src is None else src, dst_ref=dst,
                send_sem=send.at[w * 7 + k], recv_sem=recv.at[w * 7 + k], device_id=to, device_id_type=MESH)

        mine = [pltpu.make_async_copy(ins[w], outs[w].at[_dev(*me)], lsem.at[w]) for w in range(n)]
        for cp in mine:
            cp.start()
        first = []
        for w in range(n):
            first.append(copy(w, 0, me, sib, src=ins[w]))
            first += [copy(w, 1 + j, me, (*chip, c), src=ins[w]) for j, chip in enumerate(chips)]
        for cp in first:
            cp.start()
        passed = []
        for j, chip in enumerate(chips):
            for w in range(n):
                copy(w, 1 + j, (*chip, c), me).wait_recv()
                fwd = copy(w, 4 + j, (*chip, c), sib)
                fwd.start()
                passed.append(fwd)
        for w in range(n):
            copy(w, 0, sib, me).wait_recv()
        for j, chip in enumerate(chips):
            for w in range(n):
                copy(w, 4 + j, (*chip, 1 - c), me).wait_recv()
        for cp in first + passed:
            cp.wait_send()
        for cp in mine:
            cp.wait()

    return pl.pallas_call(
        body, name=name, in_specs=[any_spec] * n, out_specs=[any_spec] * n,
        out_shape=[jax.ShapeDtypeStruct((N_DEV,) + s.shape, s.dtype) for s in shards],
        scratch_shapes=[pltpu.SemaphoreType.DMA((7 * n,)), pltpu.SemaphoreType.DMA((7 * n,)),
                        pltpu.SemaphoreType.DMA((n,))],
    )(*shards)


def _sibling_exchange(grads, *, name):
    n = len(grads)
    any_spec = pl.BlockSpec(memory_space=pl.ANY)

    def body(*refs):
        ins, outs = refs[:n], refs[n:2 * n]
        send, recv = refs[2 * n:]
        x, y, c, chips = _place()
        sib = (x, y, 1 - c)
        owners = [(x, y)] + chips
        cps = []
        for w in range(n):
            for k, chip in enumerate(owners):
                cps.append(pltpu.make_async_remote_copy(
                    src_ref=ins[w].at[_dev(*chip, 1 - c)], dst_ref=outs[w].at[k],
                    send_sem=send.at[w * 4 + k], recv_sem=recv.at[w * 4 + k], device_id=sib, device_id_type=MESH))
        for cp in cps:
            cp.start()
        for cp in cps:
            cp.wait_recv()
        for cp in cps:
            cp.wait_send()

    return pl.pallas_call(
        body, name=name, in_specs=[any_spec] * n, out_specs=[any_spec] * n,
        out_shape=[jax.ShapeDtypeStruct((4,) + g.shape[1:], g.dtype) for g in grads],
        scratch_shapes=[pltpu.SemaphoreType.DMA((4 * n,)), pltpu.SemaphoreType.DMA((4 * n,))],
    )(*grads)


def _chip_exchange(parts, *, name):
    n = len(parts)
    any_spec = pl.BlockSpec(memory_space=pl.ANY)

    def body(*refs):
        ins, outs = refs[:n], refs[n:2 * n]
        send, recv = refs[2 * n:]
        x, y, c, chips = _place()
        cps = []
        for w in range(n):
            for j, chip in enumerate(chips):
                cps.append(pltpu.make_async_remote_copy(
                    src_ref=ins[w].at[j], dst_ref=outs[w].at[j],
                    send_sem=send.at[w * 3 + j], recv_sem=recv.at[w * 3 + j], device_id=(*chip, c), device_id_type=MESH))
        for cp in cps:
            cp.start()
        for cp in cps:
            cp.wait_recv()
        for cp in cps:
            cp.wait_send()

    return pl.pallas_call(
        body, name=name, in_specs=[any_spec] * n, out_specs=[any_spec] * n,
        out_shape=[jax.ShapeDtypeStruct(p.shape, p.dtype) for p in parts],
        scratch_shapes=[pltpu.SemaphoreType.DMA((3 * n,)), pltpu.SemaphoreType.DMA((3 * n,))],
    )(*parts)


def _row_tile(r):
    for t in (256, 128, 64, 32, 16, 8):
        if r % t == 0:
            return t
    return r


def _chip_partial(g, sib, ids, *, name):
    _, r, c = g.shape
    tr = _row_tile(r)

    def body(ids_ref, g_ref, s_ref, o_ref):
        o_ref[...] = (g_ref[...] + s_ref[...]).astype(BF16)

    return pl.pallas_call(
        body, name=name,
        grid_spec=pltpu.PrefetchScalarGridSpec(
            num_scalar_prefetch=1, grid=(3, r // tr),
            in_specs=[pl.BlockSpec((None, tr, c), lambda j, i, ids_ref: (ids_ref[j], i, 0)),
                      pl.BlockSpec((None, tr, c), lambda j, i, ids_ref: (j + 1, i, 0))],
            out_specs=pl.BlockSpec((None, tr, c), lambda j, i, ids_ref: (j, i, 0))),
        out_shape=jax.ShapeDtypeStruct((3, r, c), BF16),
        compiler_params=_cparams(("parallel", "parallel")),
    )(ids, g, sib)


def _adamw_math(w, g, m, v):
    m = ADAM_B1 * m + (1.0 - ADAM_B1) * g
    v = ADAM_B2 * v + (1.0 - ADAM_B2) * (g * g)
    m_hat = m / (1.0 - ADAM_B1 ** ADAM_STEP)
    v_hat = v / (1.0 - ADAM_B2 ** ADAM_STEP)
    delta = -ADAM_LR * (m_hat / (jnp.sqrt(v_hat) + ADAM_EPS) + ADAM_WD * w)
    return delta, m, v


def _shard_update(g, sib, rem, me, w, m, v, *, name):
    r, c = w.shape
    tr = _row_tile(r)

    def body(me_ref, g_ref, s_ref, r0_ref, r1_ref, r2_ref, w_ref, m_ref, v_ref, go_ref, d_ref, mo_ref, vo_ref):
        gt = g_ref[...] + s_ref[...]
        gt = gt + r0_ref[...].astype(F32)
        gt = gt + r1_ref[...].astype(F32)
        gt = gt + r2_ref[...].astype(F32)
        go_ref[...] = gt
        d, mn, vn = _adamw_math(w_ref[...], gt, m_ref[...], v_ref[...])
        d_ref[...] = d
        mo_ref[...] = mn
        vo_ref[...] = vn

    blk = lambda k: pl.BlockSpec((None, tr, c), lambda i, me_ref: (k, i, 0))
    plain = pl.BlockSpec((tr, c), lambda i, me_ref: (i, 0))
    return pl.pallas_call(
        body, name=name,
        grid_spec=pltpu.PrefetchScalarGridSpec(
            num_scalar_prefetch=1, grid=(r // tr,),
            in_specs=[pl.BlockSpec((None, tr, c), lambda i, me_ref: (me_ref[0], i, 0)), blk(0), blk(0), blk(1), blk(2),
                      plain, plain, plain],
            out_specs=[plain] * 4),
        out_shape=[jax.ShapeDtypeStruct((r, c), F32)] * 4,
        compiler_params=_cparams(("parallel",)),
    )(me, g, sib, rem, rem, rem, w, m, v)


def _small_update(gathered, w, m, v, *, name):
    _, r, c = gathered.shape

    def body(g_ref, w_ref, m_ref, v_ref, go_ref, d_ref, mo_ref, vo_ref):
        gt = g_ref[0]
        for k in range(1, N_DEV):
            gt = gt + g_ref[k]
        go_ref[...] = gt
        d, mn, vn = _adamw_math(w_ref[...], gt, m_ref[...], v_ref[...])
        d_ref[...] = d
        mo_ref[...] = mn
        vo_ref[...] = vn

    return pl.pallas_call(
        body, name=name, out_shape=[jax.ShapeDtypeStruct((r, c), F32)] * 4,
        compiler_params=pltpu.CompilerParams(vmem_limit_bytes=VMEM_LIMIT),
    )(gathered, w, m, v)


SMALL_UNIT = 1024


def _pack(parts):
    flat = []
    for p in parts:
        f = p.reshape(-1).astype(F32)
        pad = (-f.shape[0]) % SMALL_UNIT
        flat.append(jnp.pad(f, (0, pad)))
    return jnp.concatenate(flat).reshape(-1, 128)


def _unpack(buf, shapes):
    flat = buf.reshape(-1)
    out, off = [], 0
    for s in shapes:
        nel = math.prod(s)
        out.append(flat[off:off + nel].reshape(s))
        off += nel + ((-nel) % SMALL_UNIT)
    return out


def kernel(x, p, ffn1_w_in, ffn1_w_out, ln1_g, ln1_b, mix_w_in, conv_w, conv_b, conv_w_out, ssm_lam_re, ssm_lam_im, ssm_log_step, ssm_b_re, ssm_b_im, ssm_c_re, ssm_c_im, ssm_d, ssm_w_glu, mix_w_out, ln2_g, ln2_b, ffn2_w_in, ffn2_w_out, ln3_g, ln3_b, ple_w_in, ple_w_gate, ln4_g, ln4_b, loss_target, m_ffn1_w_in, m_ffn1_w_out, m_ln1_g, m_ln1_b, m_mix_w_in, m_conv_w, m_conv_b, m_conv_w_out, m_ssm_lam_re, m_ssm_lam_im, m_ssm_log_step, m_ssm_b_re, m_ssm_b_im, m_ssm_c_re, m_ssm_c_im, m_ssm_d, m_ssm_w_glu, m_mix_w_out, m_ln2_g, m_ln2_b, m_ffn2_w_in, m_ffn2_w_out, m_ln3_g, m_ln3_b, m_ple_w_in, m_ple_w_gate, m_ln4_g, m_ln4_b, v_ffn1_w_in, v_ffn1_w_out, v_ln1_g, v_ln1_b, v_mix_w_in, v_conv_w, v_conv_b, v_conv_w_out, v_ssm_lam_re, v_ssm_lam_im, v_ssm_log_step, v_ssm_b_re, v_ssm_b_im, v_ssm_c_re, v_ssm_c_im, v_ssm_d, v_ssm_w_glu, v_mix_w_out, v_ln2_g, v_ln2_b, v_ffn2_w_in, v_ffn2_w_out, v_ln3_g, v_ln3_b, v_ple_w_in, v_ple_w_gate, v_ln4_g, v_ln4_b):
    args = dict(locals())
    big = ['ffn1_w_in', 'ffn1_w_out', 'mix_w_in', 'conv_w_out', 'ssm_w_glu', 'mix_w_out',
           'ffn2_w_in', 'ffn2_w_out', 'ple_w_in', 'ple_w_gate']
    small = ['ln1_g', 'ln1_b', 'conv_b', 'ssm_lam_re', 'ssm_lam_im', 'ssm_log_step', 'ssm_b_re', 'ssm_b_im',
             'ssm_c_re', 'ssm_c_im', 'ssm_d', 'ln2_g', 'ln2_b', 'ln3_g', 'ln3_b', 'ln4_g', 'ln4_b']
    order = ['ffn1_w_in', 'ffn1_w_out', 'ln1_g', 'ln1_b', 'mix_w_in', 'conv_w', 'conv_b', 'conv_w_out',
             'ssm_lam_re', 'ssm_lam_im', 'ssm_log_step', 'ssm_b_re', 'ssm_b_im', 'ssm_c_re', 'ssm_c_im', 'ssm_d',
             'ssm_w_glu', 'mix_w_out', 'ln2_g', 'ln2_b', 'ffn2_w_in', 'ffn2_w_out', 'ln3_g', 'ln3_b',
             'ple_w_in', 'ple_w_gate', 'ln4_g', 'ln4_b']

    t = x.shape[1]
    d = x.shape[2]
    xc_, yc_, cc_ = lax.axis_index("x"), lax.axis_index("y"), lax.axis_index("c")
    me = (4 * xc_ + 2 * yc_ + cc_).astype(jnp.int32)
    cw_cols = conv_w.shape[2]

    shards = [args[nm][0].astype(BF16) for nm in big]
    cw_pad = jnp.zeros((8, 128), F32).at[0:3, 0:cw_cols].set(conv_w[0])
    gathered = _all_gather(shards + [cw_pad], name="gather_weights")
    wg = dict(zip(big, gathered[:-1]))
    cw_full = jnp.transpose(gathered[-1][:, 0:3, 0:cw_cols], (1, 0, 2)).reshape(3, N_DEV * cw_cols)
    cw8 = jnp.zeros((8, CONV_CH), F32).at[0:3, :].set(cw_full)
    w1in, w2in = wg['ffn1_w_in'], wg['ffn2_w_in']
    wf = w1in.shape[2]
    w1out = wg['ffn1_w_out'].reshape(4, wf, d)
    w2out = wg['ffn2_w_out'].reshape(4, wf, d)
    wmix = wg['mix_w_in']
    wco = wg['conv_w_out']
    wglu = wg['ssm_w_glu']
    wmo = wg['mix_w_out'].reshape(d, d)
    wpin = wg['ple_w_in']
    wgate = wg['ple_w_gate'].reshape(d, d)

    s5_in = (ssm_lam_re[0], ssm_lam_im[0], ssm_log_step[0], ssm_b_re[0], ssm_b_im[0])
    (a_re, a_im, bb_re, bb_im), s5_vjp = jax.vjp(_s5_discretise, *s5_in)
    tab_f = _pow_table(a_re, a_im)
    tab_b = _pow_table(a_re, -a_im, descending=True)
    bmat = _perm_cols(_block_diag(jnp.transpose(bb_re, (0, 2, 1))), _block_diag(jnp.transpose(bb_im, (0, 2, 1))))
    cmat_t = _perm_cols(_block_diag(ssm_c_re[0]), _block_diag(-ssm_c_im[0]))
    bmat_b = bmat.astype(BF16)
    bmat_tb = jnp.transpose(bmat).astype(BF16)
    cmat_tb = cmat_t.astype(BF16)
    cmat_b = jnp.transpose(cmat_t).astype(BF16)
    dvec = ssm_d[0].reshape(1, SSM_W)

    xf = x[0]
    x_b = xf.astype(BF16)
    p_b = p[0, 0].astype(BF16)
    tgt = loss_target[0]
    tq = min(512, t)

    def ffn_fwd(xin_f, xin_b, win, wout, gam, bet, tag):
        h = _mm(xin_b, win, name=f"{tag}_in", jb='b', jo='b', nj=8, tm=tq, tn=wf, tk=d)
        a = _swiglu_fwd(h, name=f"{tag}_act")
        f = _mm(a, wout, name=f"{tag}_out", ja='c', jb='c', nj=4, tm=tq, tn=d, tk=wf)
        y, yb = _ln_fwd(xin_f, [f], gam, bet, name=f"{tag}_ln", fs=0.5)
        return h, a, f, y, yb

    h1, a1, f1, x1, x1b = ffn_fwd(xf, x_b, w1in, w1out, ln1_g, ln1_b, "ffn1")
    proj = _mm(x1b, wmix, name="mix_in", jb='b', jo='b', o_flat=True, nj=8, tm=tq, tn=512, tk=d)
    ycin = _conv_fwd(proj, cw8, conv_b, name="conv_fwd")
    yconv = _mm(ycin, wco, name="conv_out", jb='b', jo='b', o_flat=True, nj=8, tm=tq, tn=128, tk=CONV_CH)
    s_f, s_b = _scan_fwd(proj, bmat_b, tab_f, name="scan_fwd")
    ymm = _mm(s_b, cmat_b, name="ssm_read", tm=tq, tn=SSM_W, tk=1024)
    ys, sg, u_b = _s5_out(ymm, proj, dvec, name="ssm_out")
    glu = _mm(sg, wglu, name="glu_in", jb='b', jo='b', o_flat=True, nj=8, tm=tq, tn=256, tk=SSM_W)
    merged = _gate_fwd(yconv, glu, proj, name="gate_fwd")
    mix = _mm(merged, wmo, name="mix_out", tm=tq, tn=d, tk=d)
    x2, x2b = _ln_fwd(x1, [mix], ln2_g, ln2_b, name="mix_ln")
    h2, a2, f2, x3, x3b = ffn_fwd(x2, x2b, w2in, w2out, ln3_g, ln3_b, "ffn2")
    pe = _mm(p_b, wpin, name="ple_in", jb='b', jo='b', o_flat=True, nj=8, tm=tq, tn=128, tk=p_b.shape[1])
    gp = _mm(x3b, wgate, name="ple_gate", tm=tq, tn=d, tk=d)

    dr4, dpe_b, dgp_b, dg4, db4, loss_part = _ln_bwd(x3, [pe, gp], ln4_g, ln4_b, [], name="ple_ln_bwd",
                                                     ple=True, target=tgt)
    g_ple_in = _mm(p_b, dpe_b, name="ple_in_wg", jb='b', jo='b', b_flat=True, ta=True, nj=8,
                   tm=p_b.shape[1], tn=128, tk=tq)
    g_ple_gate = _mm(x3b, dgp_b, name="ple_gate_wg", ta=True, tm=d, tn=d, tk=tq)
    dx3_g = _mm(dgp_b, wgate, name="ple_gate_dg", tb=True, tm=tq, tn=d, tk=d)

    def ffn_bwd(xin_f, xin_b, h, a, f, win, wout, gam, bet, grads, tag):
        dr, df_b, dgam, dbet = _ln_bwd(xin_f, [f], gam, bet, grads, name=f"{tag}_ln_bwd", fs=0.5)
        da = _mm(df_b, wout, name=f"{tag}_out_dg", jb='b', jo='b', tb=True, nj=4, tm=tq, tn=wf, tk=d)
        g_out = _mm(a, df_b, name=f"{tag}_out_wg", ja='b', jo='b', ta=True, nj=4, tm=wf, tn=d, tk=tq)
        dh = _swiglu_bwd(da, h, name=f"{tag}_act_bwd")
        dx = _mm(dh, win, name=f"{tag}_in_dg", ja='c', jb='c', tb=True, nj=8, tm=tq, tn=d, tk=wf)
        g_in = _mm(xin_b, dh, name=f"{tag}_in_wg", jb='b', jo='b', ta=True, nj=8, tm=d, tn=wf, tk=tq)
        return dr, dx, g_in, g_out, dgam, dbet

    dr3, dx2_f, g_ffn2_in, g_ffn2_out, dg3, db3 = ffn_bwd(
        x2, x2b, h2, a2, f2, w2in, w2out, ln3_g, ln3_b, [(dr4, ALPHA), (dx3_g, 1.0)], "ffn2")

    dr2, dmix_b, dg2, db2 = _ln_bwd(x1, [mix], ln2_g, ln2_b, [(dr3, ALPHA), (dx2_f, 1.0)], name="mix_ln_bwd")
    dmerged = _mm(dmix_b, wmo, name="mix_out_dg", tb=True, tm=tq, tn=d, tk=d)
    g_mix_out = _mm(merged, dmix_b, name="mix_out_wg", ta=True, tm=d, tn=d, tk=tq)
    dyconv_b, dglu_b, dgate_b = _gate_bwd(dmerged, yconv, glu, proj, name="gate_bwd")
    g_conv_out = _mm(ycin, dyconv_b, name="conv_out_wg", jb='b', jo='b', b_flat=True, ta=True, nj=8,
                     tm=CONV_CH, tn=128, tk=tq)
    dycin = _mm(dyconv_b, wco, name="conv_out_dg", ja='c', jb='c', a_flat=True, tb=True, nj=8,
                tm=tq, tn=CONV_CH, tk=128)
    g_glu = _mm(sg, dglu_b, name="glu_in_wg", jb='b', jo='b', b_flat=True, ta=True, nj=8,
                tm=SSM_W, tn=256, tk=tq)
    dsg = _mm(dglu_b, wglu, name="glu_in_dg", ja='c', jb='c', a_flat=True, tb=True, nj=8,
              tm=tq, tn=SSM_W, tk=256)
    dys, dys_b, dd = _s5_bwd_in(dsg, ys, proj, name="ssm_out_bwd")
    h_b, da_acc = _scan_bwd(dys_b, cmat_tb, s_f, tab_b, name="scan_bwd")
    dumm = _mm(h_b, bmat_tb, name="ssm_write_dg", tm=tq, tn=SSM_W, tk=1024)
    du_b = _s5_du(dumm, dys, dvec, name="ssm_du")
    g_bmat = _mm(u_b, h_b, name="ssm_write_wg", ta=True, tm=SSM_W, tn=1024, tk=tq)
    g_cmat = _mm(dys_b, s_b, name="ssm_read_wg", ta=True, tm=SSM_W, tn=1024, tk=tq)
    dcb_b, dcc_b, dch_b, dconv = _conv_bwd(dycin, proj, cw8, conv_b, name="conv_bwd")
    dproj = jnp.concatenate([dcb_b, dcc_b, dch_b, du_b, dgate_b], axis=1)
    g_mix_in = _mm(x1b, dproj, name="mix_in_wg", jb='b', jo='b', b_flat=True, ta=True, nj=8,
                   tm=d, tn=512, tk=tq)
    dx1_m = _mm(dproj, wmix, name="mix_in_dg", ja='c', jb='c', a_flat=True, tb=True, nj=8,
                tm=tq, tn=d, tk=512)

    dr1, dx_f, g_ffn1_in, g_ffn1_out, dg1, db1 = ffn_bwd(
        xf, x_b, h1, a1, f1, w1in, w1out, ln1_g, ln1_b, [(dr2, ALPHA), (dx1_m, 1.0)], "ffn1")
    grad_x = _axpy(dr1, ALPHA, dx_f, name="grad_x")

    da_sum = jnp.sum(da_acc, axis=0)
    da_re, da_im = _unperm_cols(da_sum)
    gb_re_d, gb_im_d = _unperm_cols(g_bmat)
    gbb_re = jnp.transpose(_block_diag_extract(gb_re_d, SSM_GROUP, SSM_STATE), (0, 2, 1))
    gbb_im = jnp.transpose(_block_diag_extract(gb_im_d, SSM_GROUP, SSM_STATE), (0, 2, 1))
    gc_re_d, gc_im_d = _unperm_cols(g_cmat)
    g_c_re = _block_diag_extract(gc_re_d, SSM_GROUP, SSM_STATE)
    g_c_im = -_block_diag_extract(gc_im_d, SSM_GROUP, SSM_STATE)
    g_lam_re, g_lam_im, g_log_step, g_b_re, g_b_im = s5_vjp(
        (da_re.reshape(SSM_GROUPS, SSM_STATE), da_im.reshape(SSM_GROUPS, SSM_STATE), gbb_re, gbb_im))
    g_d = dd.reshape(SSM_GROUPS, SSM_GROUP)

    small_g = {'ln1_g': dg1, 'ln1_b': db1, 'conv_b': dconv[3:4], 'ssm_lam_re': g_lam_re, 'ssm_lam_im': g_lam_im,
               'ssm_log_step': g_log_step, 'ssm_b_re': g_b_re, 'ssm_b_im': g_b_im, 'ssm_c_re': g_c_re,
               'ssm_c_im': g_c_im, 'ssm_d': g_d, 'ln2_g': dg2, 'ln2_b': db2, 'ln3_g': dg3, 'ln3_b': db3,
               'ln4_g': dg4, 'ln4_b': db4}
    small_shapes = [args[nm].shape for nm in small] + [(3, CONV_CH), (1,)]
    g_pack = _pack([small_g[nm] for nm in small] + [dconv[0:3], loss_part[0:1, 0:1]])
    (g_all,) = _all_gather([g_pack], name="gather_small")

    def full_cw(a):
        return lax.dynamic_update_slice(jnp.zeros((3, CONV_CH), F32), a[0], (0, me * cw_cols))

    zero1 = jnp.zeros((1,), F32)
    w_pack = _pack([args[nm] for nm in small] + [full_cw(conv_w), zero1])
    m_pack = _pack([args['m_' + nm] for nm in small] + [full_cw(m_conv_w), zero1])
    v_pack = _pack([args['v_' + nm] for nm in small] + [full_cw(v_conv_w), zero1])
    sg_sum, sd, sm, sv = _small_update(g_all, w_pack, m_pack, v_pack, name="small_update")
    res = {}
    for key, buf in (('grad_', sg_sum), ('delta_', sd), ('new_m_', sm), ('new_v_', sv)):
        parts = _unpack(buf, small_shapes)
        for nm, val in zip(small, parts[:len(small)]):
            res[key + nm] = val
        res[key + 'conv_w'] = lax.dynamic_slice(parts[len(small)], (0, me * cw_cols), (3, cw_cols))[None]
        if key == 'grad_':
            loss = parts[-1][0]

    big_g = {'ffn1_w_in': g_ffn1_in, 'ffn1_w_out': g_ffn1_out, 'mix_w_in': g_mix_in, 'conv_w_out': g_conv_out,
             'ssm_w_glu': g_glu, 'mix_w_out': g_mix_out, 'ffn2_w_in': g_ffn2_in, 'ffn2_w_out': g_ffn2_out,
             'ple_w_in': g_ple_in, 'ple_w_gate': g_ple_gate}
    blocked = [big_g[nm].reshape((N_DEV,) + args[nm].shape[1:]) for nm in big]
    sib_parts = _sibling_exchange(blocked, name="grad_sibling")
    ids = jnp.stack([_dev(1 - xc_, yc_, cc_), _dev(xc_, 1 - yc_, cc_), _dev(1 - xc_, 1 - yc_, cc_)]).astype(jnp.int32)
    chip_parts = [_chip_partial(g, s, ids, name=f"chip_sum_{nm}") for nm, g, s in zip(big, blocked, sib_parts)]
    rem_parts = _chip_exchange(chip_parts, name="grad_chips")
    me1 = me.reshape(1)
    for nm, g, s, r in zip(big, blocked, sib_parts, rem_parts):
        gt, dl, mn, vn = _shard_update(g, s, r, me1, args[nm][0], args['m_' + nm][0], args['v_' + nm][0],
                                       name=f"update_{nm}")
        res['grad_' + nm], res['delta_' + nm], res['new_m_' + nm], res['new_v_' + nm] = gt[None], dl[None], mn[None], vn[None]

    outs = [loss, grad_x[None]]
    for key in ('grad_', 'delta_', 'new_m_', 'new_v_'):
        outs += [res[key + nm] for nm in order]
    return tuple(outs)
```

```python
import functools
import math

import jax
import jax.numpy as jnp
from jax import lax
from jax.experimental import pallas as pl
from jax.experimental.pallas import tpu as pltpu

F32 = jnp.float32
BF16 = jnp.bfloat16
MESH = pl.DeviceIdType.MESH

N_DEV = 8
ALPHA = 2.0 ** 0.25
LN_EPS = 1e-5
CONV_CH = 512
SSM_W = 512
SSM_GROUPS = 32
SSM_GROUP = 16
SSM_STATE = 64
SSM_CH = SSM_GROUPS * SSM_STATE
SCAN_CB = 512
SCAN_NCB = SSM_CH // SCAN_CB
SCAN_GPB = SSM_GROUPS // SCAN_NCB
SCAN_UW = SCAN_GPB * SSM_GROUP
SCAN_TT = 256
ADAM_LR = 0.001
ADAM_B1 = 0.9
ADAM_B2 = 0.999
ADAM_EPS = 1e-08
ADAM_WD = 0.01
ADAM_STEP = 10
VMEM_LIMIT = 56 * 1024 * 1024


def _cparams(sem=None, **kw):
    return pltpu.CompilerParams(dimension_semantics=sem, vmem_limit_bytes=VMEM_LIMIT, **kw)


def _mm(a, b, *, name, ja=None, jb=None, jo=None, a_flat=False, b_flat=False, o_flat=False,
        ta=False, tb=False, tm, tn, tk, nj=1, out_dtype=F32):
    def dims(arr, j, flat):
        if j is None:
            return arr.shape
        if flat:
            return (arr.shape[0], arr.shape[1] // nj)
        assert arr.shape[0] == nj, (name, arr.shape, nj)
        return arr.shape[1:]

    ar, ac = dims(a, ja, a_flat)
    br, bc = dims(b, jb, b_flat)
    m, k = (ac, ar) if ta else (ar, ac)
    k2, n = (bc, br) if tb else (br, bc)
    assert k == k2, (name, a.shape, b.shape)
    assert m % tm == 0 and n % tn == 0 and k % tk == 0, (name, m, n, k, tm, tn, tk)
    njb = nj if 'b' in (ja, jb) else 1
    njc = nj if 'c' in (ja, jb) else 1
    nk = k // tk
    nred = njc * nk
    grid = (njb, m // tm, n // tn, njc, nk)

    def make_spec(j, flat, blk, rfn, cfn, cols_per_j):
        def jsel(g, c):
            return g if j == 'b' else c
        if j is None:
            return pl.BlockSpec(blk, lambda g, i, jn, c, kk: (rfn(i, jn, kk), cfn(i, jn, kk)))
        if flat:
            nb = cols_per_j // blk[1]
            return pl.BlockSpec(blk, lambda g, i, jn, c, kk: (rfn(i, jn, kk), jsel(g, c) * nb + cfn(i, jn, kk)))
        return pl.BlockSpec((None,) + blk,
                            lambda g, i, jn, c, kk: (jsel(g, c), rfn(i, jn, kk), cfn(i, jn, kk)))

    if ta:
        a_spec = make_spec(ja, a_flat, (tk, tm), lambda i, jn, kk: kk, lambda i, jn, kk: i, ac)
    else:
        a_spec = make_spec(ja, a_flat, (tm, tk), lambda i, jn, kk: i, lambda i, jn, kk: kk, ac)
    if tb:
        b_spec = make_spec(jb, b_flat, (tn, tk), lambda i, jn, kk: jn, lambda i, jn, kk: kk, bc)
    else:
        b_spec = make_spec(jb, b_flat, (tk, tn), lambda i, jn, kk: kk, lambda i, jn, kk: jn, bc)
    o_spec = make_spec(jo, o_flat, (tm, tn), lambda i, jn, kk: i, lambda i, jn, kk: jn, n)
    if jo is None:
        out_shape = (m, n)
    elif o_flat:
        out_shape = (m, nj * n)
    else:
        out_shape = (nj, m, n)

    dn = (((0 if ta else 1,), (1 if tb else 0,)), ((), ()))

    def body(a_ref, b_ref, o_ref, *scratch):
        p = lax.dot_general(a_ref[...], b_ref[...], dn, preferred_element_type=F32)
        if nred == 1:
            o_ref[...] = p.astype(o_ref.dtype)
        else:
            acc = scratch[0]
            r = pl.program_id(3) * nk + pl.program_id(4)

            @pl.when(r == 0)
            def _():
                acc[...] = p

            @pl.when(r > 0)
            def _():
                acc[...] += p

            @pl.when(r == nred - 1)
            def _():
                o_ref[...] = acc[...].astype(o_ref.dtype)

    return pl.pallas_call(
        body, name=name, grid=grid, in_specs=[a_spec, b_spec], out_specs=o_spec,
        out_shape=jax.ShapeDtypeStruct(out_shape, out_dtype),
        scratch_shapes=[] if nred == 1 else [pltpu.VMEM((tm, tn), F32)],
        compiler_params=_cparams(("parallel", "parallel", "parallel", "arbitrary", "arbitrary")),
    )(a, b)


def _sigmoid(v):
    return 1.0 / (1.0 + jnp.exp(-v))


def _row_spec(tm, cols, colblk=0):
    return pl.BlockSpec((tm, cols), lambda i: (i, colblk))


def _vec_spec(cols):
    return pl.BlockSpec((1, cols), lambda i: (0, 0))


def _ffn_in(xb, win, *, name, tm):
    t, d = xb.shape
    nj, _, w = win.shape
    half = nj // 2

    def body(x_ref, wg_ref, wu_ref, a_ref, gu_ref):
        xv = x_ref[...]
        g = jnp.dot(xv, wg_ref[...], preferred_element_type=F32)
        u = jnp.dot(xv, wu_ref[...], preferred_element_type=F32)
        a_ref[...] = (g * _sigmoid(g) * u).astype(BF16)
        gu_ref[0] = g.astype(BF16)
        gu_ref[1] = u.astype(BF16)

    return pl.pallas_call(
        body, name=name, grid=(half, t // tm),
        in_specs=[pl.BlockSpec((tm, d), lambda j, i: (i, 0)),
                  pl.BlockSpec((None, d, w), lambda j, i: (j, 0, 0)),
                  pl.BlockSpec((None, d, w), lambda j, i: (j + half, 0, 0))],
        out_specs=[pl.BlockSpec((None, tm, w), lambda j, i: (j, i, 0)),
                   pl.BlockSpec((2, None, tm, w), lambda j, i: (0, j, i, 0))],
        out_shape=[jax.ShapeDtypeStruct((half, t, w), BF16), jax.ShapeDtypeStruct((2, half, t, w), BF16)],
        compiler_params=_cparams(("parallel", "parallel")),
    )(xb, win, win)


def _ffn_out_dg(dfb, wout, gu, *, name, tm):
    t, d = dfb.shape
    half, w, _ = wout.shape
    dn = (((1,), (1,)), ((), ()))

    def body(df_ref, w_ref, gu_ref, dh_ref):
        da = lax.dot_general(df_ref[...], w_ref[...], dn, preferred_element_type=F32)
        g = gu_ref[0].astype(F32)
        u = gu_ref[1].astype(F32)
        sg = _sigmoid(g)
        dh_ref[0] = (da * u * (sg * (1.0 + g * (1.0 - sg)))).astype(BF16)
        dh_ref[1] = (da * (g * sg)).astype(BF16)

    out = pl.pallas_call(
        body, name=name, grid=(half, t // tm),
        in_specs=[pl.BlockSpec((tm, d), lambda j, i: (i, 0)),
                  pl.BlockSpec((None, w, d), lambda j, i: (j, 0, 0)),
                  pl.BlockSpec((2, None, tm, w), lambda j, i: (0, j, i, 0))],
        out_specs=pl.BlockSpec((2, None, tm, w), lambda j, i: (0, j, i, 0)),
        out_shape=jax.ShapeDtypeStruct((2, half, t, w), BF16),
        compiler_params=_cparams(("parallel", "parallel")),
    )(dfb, wout, gu)
    return out.reshape(2 * half, t, w)


def _ln_stats(r):
    mu = jnp.mean(r, axis=-1, keepdims=True)
    xc = r - mu
    var = jnp.mean(xc * xc, axis=-1, keepdims=True)
    rstd = lax.rsqrt(var + LN_EPS)
    return xc * rstd, rstd


def _ln_fwd(xin, fparts, gamma, beta, *, name, fs=1.0, ple=False, tm=256):
    t, d = xin.shape
    nf = len(fparts)

    def body(*refs):
        x_ref = refs[0]
        f_refs = refs[1:1 + nf]
        g_ref, b_ref, y_ref, yb_ref = refs[1 + nf:]
        if ple:
            f = f_refs[0][...] * _sigmoid(f_refs[1][...])
        else:
            f = fs * f_refs[0][...]
        xh, _ = _ln_stats(ALPHA * x_ref[...] + f)
        y = xh * g_ref[...] + b_ref[...]
        y_ref[...] = y
        yb_ref[...] = y.astype(BF16)

    return pl.pallas_call(
        body, name=name, grid=(t // tm,),
        in_specs=[_row_spec(tm, d)] * (1 + nf) + [_vec_spec(d), _vec_spec(d)],
        out_specs=[_row_spec(tm, d), _row_spec(tm, d)],
        out_shape=[jax.ShapeDtypeStruct((t, d), F32), jax.ShapeDtypeStruct((t, d), BF16)],
        compiler_params=_cparams(("parallel",)),
    )(xin, *fparts, gamma, beta)


def _ln_bwd(xin, fparts, gamma, beta, grads, *, name, fs=1.0, ple=False, target=None, tm=256):
    t, d = xin.shape
    nf = len(fparts)
    ng = len(grads)
    coefs = [c for _, c in grads]
    use_t = target is not None
    n_fout = 2 if ple else 1

    def body(*refs):
        pos = 0
        x_ref = refs[pos]; pos += 1
        f_refs = refs[pos:pos + nf]; pos += nf
        g_ref, b_ref = refs[pos:pos + 2]; pos += 2
        gr_refs = refs[pos:pos + ng]; pos += ng
        if use_t:
            t_ref = refs[pos]; pos += 1
        dr_ref = refs[pos]; pos += 1
        fo_refs = refs[pos:pos + n_fout]; pos += n_fout
        dg_ref, db_ref = refs[pos:pos + 2]; pos += 2
        if use_t:
            loss_ref = refs[pos]; pos += 1
        i = pl.program_id(0)

        if ple:
            pe = f_refs[0][...]
            sg = _sigmoid(f_refs[1][...])
            f = pe * sg
        else:
            f = fs * f_refs[0][...]
        xh, rstd = _ln_stats(ALPHA * x_ref[...] + f)
        gam = g_ref[...]
        if use_t:
            diff = xh * gam + b_ref[...] - t_ref[...]
            dy = diff * (1.0 / d)
            lpart = 0.5 * jnp.sum(jnp.sum(diff * diff, axis=-1, keepdims=True), axis=0, keepdims=True) * (1.0 / d)
        else:
            dy = coefs[0] * gr_refs[0][...]
            for c, r in zip(coefs[1:], gr_refs[1:]):
                dy = dy + c * r[...]
        dxh = dy * gam
        m1 = jnp.mean(dxh, axis=-1, keepdims=True)
        m2 = jnp.mean(dxh * xh, axis=-1, keepdims=True)
        dr = rstd * (dxh - m1 - xh * m2)
        dr_ref[...] = dr
        if ple:
            fo_refs[0][...] = (dr * sg).astype(BF16)
            fo_refs[1][...] = (dr * pe * (sg * (1.0 - sg))).astype(BF16)
        else:
            fo_refs[0][...] = (fs * dr).astype(BF16)
        dgp = jnp.sum(dy * xh, axis=0, keepdims=True)
        dbp = jnp.sum(dy, axis=0, keepdims=True)

        @pl.when(i == 0)
        def _():
            dg_ref[...] = dgp
            db_ref[...] = dbp
            if use_t:
                loss_ref[...] = jnp.broadcast_to(lpart, loss_ref.shape)

        @pl.when(i > 0)
        def _():
            dg_ref[...] += dgp
            db_ref[...] += dbp
            if use_t:
                loss_ref[...] += jnp.broadcast_to(lpart, loss_ref.shape)

    ins = [xin, *fparts, gamma, beta, *[g for g, _ in grads]] + ([target] if use_t else [])
    in_specs = ([_row_spec(tm, d)] * (1 + nf) + [_vec_spec(d), _vec_spec(d)] + [_row_spec(tm, d)] * ng
                + ([_row_spec(tm, d)] if use_t else []))
    out_specs = [_row_spec(tm, d)] * (1 + n_fout) + [_vec_spec(d), _vec_spec(d)] + ([_vec_spec(128)] if use_t else [])
    out_shape = ([jax.ShapeDtypeStruct((t, d), F32)] + [jax.ShapeDtypeStruct((t, d), BF16)] * n_fout
                 + [jax.ShapeDtypeStruct((1, d), F32)] * 2 + ([jax.ShapeDtypeStruct((1, 128), F32)] if use_t else []))
    return pl.pallas_call(
        body, name=name, grid=(t // tm,), in_specs=in_specs, out_specs=out_specs, out_shape=out_shape,
        compiler_params=_cparams(("arbitrary",)),
    )(*ins)


def _axpy(a, ca, b, *, name, tm=256):
    t, d = a.shape

    def body(a_ref, b_ref, o_ref):
        o_ref[...] = ca * a_ref[...] + b_ref[...]

    return pl.pallas_call(
        body, name=name, grid=(t // tm,), in_specs=[_row_spec(tm, d)] * 2, out_specs=_row_spec(tm, d),
        out_shape=jax.ShapeDtypeStruct((t, d), F32), compiler_params=_cparams(("parallel",)),
    )(a, b)


def _conv_fwd(proj, cw, cb, *, name, tm=256):
    t = proj.shape[0]
    c = CONV_CH
    hb = tm // 8

    def body(b_ref, c_ref, h_ref, cp_ref, hp_ref, w_ref, bias_ref, o_ref, q_scr):
        i = pl.program_id(0)
        q = c_ref[...] * h_ref[...]
        halo = jnp.where(i > 0, cp_ref[...] * hp_ref[...], 0.0)
        q_scr[0:8, :] = halo
        q_scr[8:, :] = q
        z = (w_ref[2:3, :] * q + w_ref[1:2, :] * q_scr[pl.ds(7, tm), :] + w_ref[0:1, :] * q_scr[pl.ds(6, tm), :]
             + bias_ref[...])
        o_ref[...] = (b_ref[...] * z).astype(BF16)

    prev = lambda blk: pl.BlockSpec((8, c), lambda i: (jnp.maximum(i * hb - 1, 0), blk))
    return pl.pallas_call(
        body, name=name, grid=(t // tm,),
        in_specs=[_row_spec(tm, c, 0), _row_spec(tm, c, 1), _row_spec(tm, c, 2), prev(1), prev(2),
                  pl.BlockSpec((8, c), lambda i: (0, 0)), _vec_spec(c)],
        out_specs=_row_spec(tm, c),
        out_shape=jax.ShapeDtypeStruct((t, c), BF16),
        scratch_shapes=[pltpu.VMEM((tm + 8, c), F32)],
        compiler_params=_cparams(("parallel",)),
    )(proj, proj, proj, proj, proj, cw, cb)


def _conv_bwd(dyc, proj, cw, cb, *, name, tm=256):
    t = proj.shape[0]
    c = CONV_CH
    hb = tm // 8
    nblk = t // 8

    def body(d_ref, b_ref, c_ref, h_ref, cp_ref, hp_ref, dn_ref, bn_ref, w_ref, bias_ref,
             db_ref, dc_ref, dh_ref, dw_ref, q_scr, z_scr):
        i = pl.program_id(0)
        last = pl.num_programs(0) - 1
        cc = c_ref[...]
        ch = h_ref[...]
        q = cc * ch
        halo = jnp.where(i > 0, cp_ref[...] * hp_ref[...], 0.0)
        q_scr[0:8, :] = halo
        q_scr[8:, :] = q
        w0, w1, w2 = w_ref[0:1, :], w_ref[1:2, :], w_ref[2:3, :]
        qm1 = q_scr[pl.ds(7, tm), :]
        qm2 = q_scr[pl.ds(6, tm), :]
        z = w2 * q + w1 * qm1 + w0 * qm2 + bias_ref[...]
        d = d_ref[...]
        bb = b_ref[...]
        db_ref[...] = (d * z).astype(BF16)
        dz = d * bb
        z_scr[0:tm, :] = dz
        z_scr[tm:, :] = jnp.where(i < last, dn_ref[...] * bn_ref[...], 0.0)
        dq = w2 * dz + w1 * z_scr[pl.ds(1, tm), :] + w0 * z_scr[pl.ds(2, tm), :]
        dc_ref[...] = (dq * ch).astype(BF16)
        dh_ref[...] = (dq * cc).astype(BF16)
        row = lax.broadcasted_iota(jnp.int32, (8, c), 0)
        part = jnp.zeros((8, c), F32)
        for k, term in enumerate((dz * qm2, dz * qm1, dz * q, dz)):
            part = jnp.where(row == k, jnp.sum(term, axis=0, keepdims=True), part)

        @pl.when(i == 0)
        def _():
            dw_ref[...] = part

        @pl.when(i > 0)
        def _():
            dw_ref[...] += part

    prev = lambda blk: pl.BlockSpec((8, c), lambda i: (jnp.maximum(i * hb - 1, 0), blk))
    nxt_p = pl.BlockSpec((8, c), lambda i: (jnp.minimum((i + 1) * hb, nblk - 1), 0))
    nxt_d = pl.BlockSpec((8, c), lambda i: (jnp.minimum((i + 1) * hb, nblk - 1), 0))
    return pl.pallas_call(
        body, name=name, grid=(t // tm,),
        in_specs=[_row_spec(tm, c), _row_spec(tm, c, 0), _row_spec(tm, c, 1), _row_spec(tm, c, 2),
                  prev(1), prev(2), nxt_d, nxt_p, pl.BlockSpec((8, c), lambda i: (0, 0)), _vec_spec(c)],
        out_specs=[_row_spec(tm, c)] * 3 + [pl.BlockSpec((8, c), lambda i: (0, 0))],
        out_shape=[jax.ShapeDtypeStruct((t, c), BF16)] * 3 + [jax.ShapeDtypeStruct((8, c), F32)],
        scratch_shapes=[pltpu.VMEM((tm + 8, c), F32), pltpu.VMEM((tm + 8, c), F32)],
        compiler_params=_cparams(("arbitrary",)),
    )(dyc, proj, proj, proj, proj, proj, dyc, proj, cw, cb)


def _gate_fwd(yconv, glu, proj, *, name, tm=256):
    t, d = yconv.shape

    def body(yc_ref, ga_ref, gb_ref, gc_ref, gs_ref, o_ref):
        yssm = ga_ref[...] * _sigmoid(gb_ref[...])
        o_ref[...] = (_sigmoid(gc_ref[...]) * yc_ref[...] + _sigmoid(gs_ref[...]) * yssm).astype(BF16)

    return pl.pallas_call(
        body, name=name, grid=(t // tm,),
        in_specs=[_row_spec(tm, d), _row_spec(tm, d, 0), _row_spec(tm, d, 1), _row_spec(tm, d, 2), _row_spec(tm, d, 3)],
        out_specs=_row_spec(tm, d), out_shape=jax.ShapeDtypeStruct((t, d), BF16),
        compiler_params=_cparams(("parallel",)),
    )(yconv, glu, glu, proj, proj)


def _gate_bwd(dm, yconv, glu, proj, *, name, tm=256):
    t, d = yconv.shape

    def body(dm_ref, yc_ref, ga_ref, gb_ref, gc_ref, gs_ref, dyc_ref, dglu_ref, dgate_ref):
        dmv = dm_ref[...]
        sc = _sigmoid(gc_ref[...])
        ss = _sigmoid(gs_ref[...])
        sb = _sigmoid(gb_ref[...])
        ga = ga_ref[...]
        yssm = ga * sb
        dyc_ref[...] = (dmv * sc).astype(BF16)
        dgate_ref[:, 0:d] = (dmv * yc_ref[...] * (sc * (1.0 - sc))).astype(BF16)
        dys = dmv * ss
        dgate_ref[:, d:2 * d] = (dmv * yssm * (ss * (1.0 - ss))).astype(BF16)
        dglu_ref[:, 0:d] = (dys * sb).astype(BF16)
        dglu_ref[:, d:2 * d] = (dys * ga * (sb * (1.0 - sb))).astype(BF16)

    return pl.pallas_call(
        body, name=name, grid=(t // tm,),
        in_specs=[_row_spec(tm, d), _row_spec(tm, d), _row_spec(tm, d, 0), _row_spec(tm, d, 1),
                  _row_spec(tm, d, 2), _row_spec(tm, d, 3)],
        out_specs=[_row_spec(tm, d), _row_spec(tm, 2 * d), _row_spec(tm, 2 * d)],
        out_shape=[jax.ShapeDtypeStruct((t, d), BF16), jax.ShapeDtypeStruct((t, 2 * d), BF16),
                   jax.ShapeDtypeStruct((t, 2 * d), BF16)],
        compiler_params=_cparams(("parallel",)),
    )(dm, yconv, glu, glu, proj, proj)


_GELU_C = math.sqrt(2.0 / math.pi)


def _gelu(v):
    return 0.5 * v * (1.0 + jnp.tanh(_GELU_C * (v + 0.044715 * v * v * v)))


def _gelu_grad(v):
    th = jnp.tanh(_GELU_C * (v + 0.044715 * v * v * v))
    return 0.5 * (1.0 + th) + 0.5 * v * (1.0 - th * th) * (_GELU_C * (1.0 + 3.0 * 0.044715 * v * v))


def _cmul(ar, ai, br, bi):
    return ar * br - ai * bi, ar * bi + ai * br


def _scan_fwd(proj, bmat, tab, *, name):
    t = proj.shape[0]
    tt, cbw = SCAN_TT, SCAN_CB
    w2 = 2 * cbw

    def body(u_ref, b_ref, tab_ref, s_ref, sb_ref, bu_scr, carry):
        ti = pl.program_id(1)

        @pl.when(ti == 0)
        def _():
            carry[...] = jnp.zeros_like(carry)

        bu_scr[...] = jnp.dot(u_ref[...].astype(BF16), b_ref[...], preferred_element_type=F32)
        row = lax.broadcasted_iota(jnp.int32, (8, cbw), 0)

        def blk(bi, c):
            cr, ci = c
            r0 = pl.multiple_of(bi * 8, 8)
            xr = bu_scr[pl.ds(r0, 8), 0:cbw]
            xi = bu_scr[pl.ds(r0, 8), cbw:w2]
            for k, sh in enumerate((1, 2, 4)):
                kr = tab_ref[k:k + 1, 0:cbw]
                ki = tab_ref[k:k + 1, cbw:w2]
                sr = jnp.where(row >= sh, pltpu.roll(xr, sh, 0), 0.0)
                si = jnp.where(row >= sh, pltpu.roll(xi, sh, 0), 0.0)
                pr, pi = _cmul(kr, ki, sr, si)
                xr = xr + pr
                xi = xi + pi
            pr, pi = _cmul(tab_ref[8:16, 0:cbw], tab_ref[8:16, cbw:w2], cr, ci)
            xr = xr + pr
            xi = xi + pi
            s_ref[pl.ds(r0, 8), 0:cbw] = xr
            s_ref[pl.ds(r0, 8), cbw:w2] = xi
            return (jnp.broadcast_to(xr[7:8, :], (8, cbw)), jnp.broadcast_to(xi[7:8, :], (8, cbw)))

        cr, ci = lax.fori_loop(0, tt // 8, blk, (carry[:, 0:cbw], carry[:, cbw:w2]))
        carry[:, 0:cbw] = cr
        carry[:, cbw:w2] = ci
        sb_ref[...] = s_ref[...].astype(BF16)

    return pl.pallas_call(
        body, name=name, grid=(SCAN_NCB, t // tt),
        in_specs=[pl.BlockSpec((tt, SCAN_UW), lambda cb, ti: (ti, 3 * SCAN_NCB + cb)),
                  pl.BlockSpec((None, SCAN_UW, w2), lambda cb, ti: (cb, 0, 0)),
                  pl.BlockSpec((16, w2), lambda cb, ti: (0, cb))],
        out_specs=[pl.BlockSpec((tt, w2), lambda cb, ti: (ti, cb))] * 2,
        out_shape=[jax.ShapeDtypeStruct((t, 2 * SSM_CH), F32), jax.ShapeDtypeStruct((t, 2 * SSM_CH), BF16)],
        scratch_shapes=[pltpu.VMEM((tt, w2), F32), pltpu.VMEM((8, w2), F32)],
        compiler_params=_cparams(("parallel", "arbitrary")),
    )(proj, bmat, tab)


def _scan_bwd(dyb, cmat_t, s, tabb, *, name):
    t = s.shape[0]
    tt, cbw = SCAN_TT, SCAN_CB
    w2 = 2 * cbw
    nt = t // tt
    hb = tt // 8

    def body(dy_ref, c_ref, s_ref, sp_ref, tab_ref, h_ref, da_ref, g_scr, s_scr, carry):
        ti = pl.program_id(1)

        @pl.when(ti == 0)
        def _():
            carry[...] = jnp.zeros_like(carry)
            da_ref[...] = jnp.zeros_like(da_ref)

        g_scr[...] = jnp.dot(dy_ref[...], c_ref[...], preferred_element_type=F32)
        s_scr[0:8, :] = jnp.where(ti < nt - 1, sp_ref[...], 0.0)
        s_scr[8:, :] = s_ref[...]
        row = lax.broadcasted_iota(jnp.int32, (8, cbw), 0)

        def blk(k, c):
            cr, ci, ar, ai = c
            bi = hb - 1 - k
            r0 = pl.multiple_of(bi * 8, 8)
            xr = g_scr[pl.ds(r0, 8), 0:cbw]
            xi = g_scr[pl.ds(r0, 8), cbw:w2]
            for j, sh in enumerate((1, 2, 4)):
                kr = tab_ref[j:j + 1, 0:cbw]
                ki = tab_ref[j:j + 1, cbw:w2]
                sr = jnp.where(row < 8 - sh, pltpu.roll(xr, 8 - sh, 0), 0.0)
                si = jnp.where(row < 8 - sh, pltpu.roll(xi, 8 - sh, 0), 0.0)
                pr, pi = _cmul(kr, ki, sr, si)
                xr = xr + pr
                xi = xi + pi
            pr, pi = _cmul(tab_ref[8:16, 0:cbw], tab_ref[8:16, cbw:w2], cr, ci)
            xr = xr + pr
            xi = xi + pi
            h_ref[pl.ds(r0, 8), 0:cbw] = xr.astype(BF16)
            h_ref[pl.ds(r0, 8), cbw:w2] = xi.astype(BF16)
            pvr = s_scr[pl.ds(r0, 8), 0:cbw]
            pvi = s_scr[pl.ds(r0, 8), cbw:w2]
            cur_r = s_scr[pl.ds(r0 + 8, 8), 0:cbw]
            cur_i = s_scr[pl.ds(r0 + 8, 8), cbw:w2]
            spr = jnp.where(row == 0, jnp.broadcast_to(pvr[7:8, :], (8, cbw)), pltpu.roll(cur_r, 1, 0))
            spi = jnp.where(row == 0, jnp.broadcast_to(pvi[7:8, :], (8, cbw)), pltpu.roll(cur_i, 1, 0))
            ar = ar + spr * xr + spi * xi
            ai = ai + spr * xi - spi * xr
            return (jnp.broadcast_to(xr[0:1, :], (8, cbw)), jnp.broadcast_to(xi[0:1, :], (8, cbw)), ar, ai)

        z = jnp.zeros((8, cbw), F32)
        cr, ci, ar, ai = lax.fori_loop(0, hb, blk, (carry[:, 0:cbw], carry[:, cbw:w2], z, z))
        carry[:, 0:cbw] = cr
        carry[:, cbw:w2] = ci
        da_ref[:, 0:cbw] += ar
        da_ref[:, cbw:w2] += ai

    rt = lambda ti: nt - 1 - ti
    return pl.pallas_call(
        body, name=name, grid=(SCAN_NCB, nt),
        in_specs=[pl.BlockSpec((tt, SCAN_UW), lambda cb, ti: (rt(ti), cb)),
                  pl.BlockSpec((None, SCAN_UW, w2), lambda cb, ti: (cb, 0, 0)),
                  pl.BlockSpec((tt, w2), lambda cb, ti: (rt(ti), cb)),
                  pl.BlockSpec((8, w2), lambda cb, ti: (jnp.maximum(rt(ti) * hb - 1, 0), cb)),
                  pl.BlockSpec((16, w2), lambda cb, ti: (0, cb))],
        out_specs=[pl.BlockSpec((tt, w2), lambda cb, ti: (rt(ti), cb)),
                   pl.BlockSpec((8, w2), lambda cb, ti: (0, cb))],
        out_shape=[jax.ShapeDtypeStruct((t, 2 * SSM_CH), BF16), jax.ShapeDtypeStruct((8, 2 * SSM_CH), F32)],
        scratch_shapes=[pltpu.VMEM((tt, w2), F32), pltpu.VMEM((tt + 8, w2), F32), pltpu.VMEM((8, w2), F32)],
        compiler_params=_cparams(("parallel", "arbitrary")),
    )(dyb, cmat_t, s, s, tabb)


def _s5_out(ymm, proj, dvec, *, name, tm=256):
    t, w = ymm.shape

    def body(y_ref, u_ref, d_ref, yo_ref, sg_ref, ub_ref):
        u = u_ref[...]
        y = y_ref[...] + d_ref[...] * u
        yo_ref[...] = y
        sg_ref[...] = _gelu(y).astype(BF16)
        ub_ref[...] = u.astype(BF16)

    return pl.pallas_call(
        body, name=name, grid=(t // tm,),
        in_specs=[_row_spec(tm, w), _row_spec(tm, w, 3), _vec_spec(w)],
        out_specs=[_row_spec(tm, w)] * 3,
        out_shape=[jax.ShapeDtypeStruct((t, w), F32), jax.ShapeDtypeStruct((t, w), BF16), jax.ShapeDtypeStruct((t, w), BF16)],
        compiler_params=_cparams(("parallel",)),
    )(ymm, proj, dvec)


def _s5_bwd_in(dsg, y, proj, *, name, tm=256):
    t, w = y.shape

    def body(d_ref, y_ref, u_ref, dy_ref, dyb_ref, dd_ref):
        i = pl.program_id(0)
        dy = d_ref[...] * _gelu_grad(y_ref[...])
        dy_ref[...] = dy
        dyb_ref[...] = dy.astype(BF16)
        part = jnp.sum(dy * u_ref[...], axis=0, keepdims=True)

        @pl.when(i == 0)
        def _():
            dd_ref[...] = part

        @pl.when(i > 0)
        def _():
            dd_ref[...] += part

    return pl.pallas_call(
        body, name=name, grid=(t // tm,),
        in_specs=[_row_spec(tm, w), _row_spec(tm, w), _row_spec(tm, w, 3)],
        out_specs=[_row_spec(tm, w), _row_spec(tm, w), _vec_spec(w)],
        out_shape=[jax.ShapeDtypeStruct((t, w), F32), jax.ShapeDtypeStruct((t, w), BF16), jax.ShapeDtypeStruct((1, w), F32)],
        compiler_params=_cparams(("arbitrary",)),
    )(dsg, y, proj)


def _s5_du(dumm, dy, dvec, *, name, tm=256):
    t, w = dy.shape

    def body(a_ref, dy_ref, d_ref, o_ref):
        o_ref[...] = (a_ref[...] + d_ref[...] * dy_ref[...]).astype(BF16)

    return pl.pallas_call(
        body, name=name, grid=(t // tm,), in_specs=[_row_spec(tm, w), _row_spec(tm, w), _vec_spec(w)],
        out_specs=_row_spec(tm, w), out_shape=jax.ShapeDtypeStruct((t, w), BF16),
        compiler_params=_cparams(("parallel",)),
    )(dumm, dy, dvec)


def _s5_discretise(lam_re, lam_im, log_step, b_re, b_im):
    lam = lax.complex(lam_re, lam_im)
    dt = jnp.exp(log_step)[:, None]
    a = jnp.exp(lam * dt)
    bbar = ((a - 1.0) / lam)[..., None] * lax.complex(b_re, b_im)
    return jnp.real(a), jnp.imag(a), jnp.real(bbar), jnp.imag(bbar)


def _perm_cols(re, im):
    lead = re.shape[:-1]
    r = re.reshape(lead + (SCAN_NCB, 1, SCAN_CB))
    i = im.reshape(lead + (SCAN_NCB, 1, SCAN_CB))
    return jnp.concatenate([r, i], axis=-2).reshape(lead + (2 * SSM_CH,))


def _unperm_cols(x):
    lead = x.shape[:-1]
    y = x.reshape(lead + (SCAN_NCB, 2, SCAN_CB))
    return y[..., 0, :].reshape(lead + (SSM_CH,)), y[..., 1, :].reshape(lead + (SSM_CH,))


def _compact(re, im):
    _, r, c = re.shape
    eye = jnp.eye(SCAN_GPB, dtype=re.dtype)

    def half(x):
        x = x.reshape(SCAN_NCB, SCAN_GPB, r, c)
        return (eye[None, :, None, :, None] * x[:, :, :, None, :]).reshape(SCAN_NCB, SCAN_GPB * r, SCAN_GPB * c)

    return jnp.concatenate([half(re), half(im)], axis=-1)


def _compact_extract(x, r):
    c = SSM_STATE
    eye = jnp.eye(SCAN_GPB, dtype=x.dtype)
    y = x.reshape(SCAN_NCB, SCAN_GPB, r, 2, SCAN_GPB, c)
    dg = jnp.sum(y * eye[None, :, None, None, :, None], axis=4).reshape(SSM_GROUPS, r, 2, c)
    return dg[:, :, 0, :], dg[:, :, 1, :]


def _pow_table(ar, ai, descending=False):
    ar = ar.reshape(1, SSM_CH)
    ai = ai.reshape(1, SSM_CH)
    pw = [(ar, ai)]
    for _ in range(7):
        pw.append(_cmul(pw[-1][0], pw[-1][1], ar, ai))
    zero = (jnp.zeros_like(ar), jnp.zeros_like(ar))
    rows = [pw[0], pw[1], pw[3]] + [zero] * 5 + (pw[::-1] if descending else pw)
    re = jnp.concatenate([r for r, _ in rows], axis=0)
    im = jnp.concatenate([i for _, i in rows], axis=0)
    return _perm_cols(re, im)


def _place():
    x, y, c = lax.axis_index("x"), lax.axis_index("y"), lax.axis_index("c")
    chips = [(1 - x, y), (x, 1 - y), (1 - x, 1 - y)]
    return x, y, c, chips


def _dev(px, py, pc):
    return 4 * px + 2 * py + pc


def _all_gather(shards, *, name):
    n = len(shards)
    any_spec = pl.BlockSpec(memory_space=pl.ANY)

    def body(*refs):
        ins, outs = refs[:n], refs[n:2 * n]
        send, recv, lsem = refs[2 * n:]
        x, y, c, chips = _place()
        me, sib = (x, y, c), (x, y, 1 - c)

        def copy(w, k, block, to, src=None):
            dst = outs[w].at[_dev(*block)]
            return pltpu.make_async_remote_copy(
                src_ref=dst if src is None else src, dst_ref=dst,
                send_sem=send.at[w * 7 + k], recv_sem=recv.at[w * 7 + k], device_id=to, device_id_type=MESH)

        mine = [pltpu.make_async_copy(ins[w], outs[w].at[_dev(*me)], lsem.at[w]) for w in range(n)]
        for cp in mine:
            cp.start()
        first = []
        for w in range(n):
            first.append(copy(w, 0, me, sib, src=ins[w]))
            first += [copy(w, 1 + j, me, (*chip, c), src=ins[w]) for j, chip in enumerate(chips)]
        for cp in first:
            cp.start()
        passed = []
        for j, chip in enumerate(chips):
            for w in range(n):
                copy(w, 1 + j, (*chip, c), me).wait_recv()
                fwd = copy(w, 4 + j, (*chip, c), sib)
                fwd.start()
                passed.append(fwd)
        for w in range(n):
            copy(w, 0, sib, me).wait_recv()
        for j, chip in enumerate(chips):
            for w in range(n):
                copy(w, 4 + j, (*chip, 1 - c), me).wait_recv()
        for cp in first + passed:
            cp.wait_send()
        for cp in mine:
            cp.wait()

    return pl.pallas_call(
        body, name=name, in_specs=[any_spec] * n, out_specs=[any_spec] * n,
        out_shape=[jax.ShapeDtypeStruct((N_DEV,) + s.shape, s.dtype) for s in shards],
        scratch_shapes=[pltpu.SemaphoreType.DMA((7 * n,)), pltpu.SemaphoreType.DMA((7 * n,)),
                        pltpu.SemaphoreType.DMA((n,))],
    )(*shards)


def _sibling_exchange(grads, *, name):
    n = len(grads)
    any_spec = pl.BlockSpec(memory_space=pl.ANY)

    def body(*refs):
        ins, outs = refs[:n], refs[n:2 * n]
        send, recv = refs[2 * n:]
        x, y, c, chips = _place()
        sib = (x, y, 1 - c)
        owners = [(x, y)] + chips
        cps = []
        for w in range(n):
            for k, chip in enumerate(owners):
                cps.append(pltpu.make_async_remote_copy(
                    src_ref=ins[w].at[_dev(*chip, 1 - c)], dst_ref=outs[w].at[k],
                    send_sem=send.at[w * 4 + k], recv_sem=recv.at[w * 4 + k], device_id=sib, device_id_type=MESH))
        for cp in cps:
            cp.start()
        for cp in cps:
            cp.wait_recv()
        for cp in cps:
            cp.wait_send()

    return pl.pallas_call(
        body, name=name, in_specs=[any_spec] * n, out_specs=[any_spec] * n,
        out_shape=[jax.ShapeDtypeStruct((4,) + g.shape[1:], g.dtype) for g in grads],
        scratch_shapes=[pltpu.SemaphoreType.DMA((4 * n,)), pltpu.SemaphoreType.DMA((4 * n,))],
    )(*grads)


def _chip_exchange(parts, *, name):
    n = len(parts)
    any_spec = pl.BlockSpec(memory_space=pl.ANY)

    def body(*refs):
        ins, outs = refs[:n], refs[n:2 * n]
        send, recv = refs[2 * n:]
        x, y, c, chips = _place()
        cps = []
        for w in range(n):
            for j, chip in enumerate(chips):
                cps.append(pltpu.make_async_remote_copy(
                    src_ref=ins[w].at[j], dst_ref=outs[w].at[j],
                    send_sem=send.at[w * 3 + j], recv_sem=recv.at[w * 3 + j], device_id=(*chip, c), device_id_type=MESH))
        for cp in cps:
            cp.start()
        for cp in cps:
            cp.wait_recv()
        for cp in cps:
            cp.wait_send()

    return pl.pallas_call(
        body, name=name, in_specs=[any_spec] * n, out_specs=[any_spec] * n,
        out_shape=[jax.ShapeDtypeStruct(p.shape, p.dtype) for p in parts],
        scratch_shapes=[pltpu.SemaphoreType.DMA((3 * n,)), pltpu.SemaphoreType.DMA((3 * n,))],
    )(*parts)


UPDATE_TILE_BYTES = 768 * 1024


def _row_tile(r, c):
    best = 8
    for t in range(8, r + 1, 8):
        if r % t == 0 and t * c * 4 <= UPDATE_TILE_BYTES:
            best = t
    return best


def _chip_partial(g, sib, ids, *, name):
    _, r, c = g.shape
    tr = _row_tile(r, c)

    def body(ids_ref, g_ref, s_ref, o_ref):
        o_ref[...] = (g_ref[...] + s_ref[...]).astype(BF16)

    return pl.pallas_call(
        body, name=name,
        grid_spec=pltpu.PrefetchScalarGridSpec(
            num_scalar_prefetch=1, grid=(3, r // tr),
            in_specs=[pl.BlockSpec((None, tr, c), lambda j, i, ids_ref: (ids_ref[j], i, 0)),
                      pl.BlockSpec((None, tr, c), lambda j, i, ids_ref: (j + 1, i, 0))],
            out_specs=pl.BlockSpec((None, tr, c), lambda j, i, ids_ref: (j, i, 0))),
        out_shape=jax.ShapeDtypeStruct((3, r, c), BF16),
        compiler_params=_cparams(("parallel", "parallel")),
    )(ids, g, sib)


def _adamw_math(w, g, m, v):
    m = ADAM_B1 * m + (1.0 - ADAM_B1) * g
    v = ADAM_B2 * v + (1.0 - ADAM_B2) * (g * g)
    m_hat = m / (1.0 - ADAM_B1 ** ADAM_STEP)
    v_hat = v / (1.0 - ADAM_B2 ** ADAM_STEP)
    delta = -ADAM_LR * (m_hat / (jnp.sqrt(v_hat) + ADAM_EPS) + ADAM_WD * w)
    return delta, m, v


def _shard_update(g, sib, rem, me, w, m, v, *, name):
    r, c = w.shape
    tr = _row_tile(r, c)

    def body(me_ref, g_ref, s_ref, r0_ref, r1_ref, r2_ref, w_ref, m_ref, v_ref, go_ref, d_ref, mo_ref, vo_ref):
        gt = g_ref[...] + s_ref[...]
        gt = gt + r0_ref[...].astype(F32)
        gt = gt + r1_ref[...].astype(F32)
        gt = gt + r2_ref[...].astype(F32)
        go_ref[...] = gt
        d, mn, vn = _adamw_math(w_ref[...], gt, m_ref[...], v_ref[...])
        d_ref[...] = d
        mo_ref[...] = mn
        vo_ref[...] = vn

    blk = lambda k: pl.BlockSpec((None, tr, c), lambda i, me_ref: (k, i, 0))
    plain = pl.BlockSpec((tr, c), lambda i, me_ref: (i, 0))
    return pl.pallas_call(
        body, name=name,
        grid_spec=pltpu.PrefetchScalarGridSpec(
            num_scalar_prefetch=1, grid=(r // tr,),
            in_specs=[pl.BlockSpec((None, tr, c), lambda i, me_ref: (me_ref[0], i, 0)), blk(0), blk(0), blk(1), blk(2),
                      plain, plain, plain],
            out_specs=[plain] * 4),
        out_shape=[jax.ShapeDtypeStruct((r, c), F32)] * 4,
        compiler_params=_cparams(("parallel",)),
    )(me, g, sib, rem, rem, rem, w, m, v)


def _small_update(gathered, w, m, v, *, name):
    _, r, c = gathered.shape

    def body(g_ref, w_ref, m_ref, v_ref, go_ref, d_ref, mo_ref, vo_ref):
        gt = g_ref[0]
        for k in range(1, N_DEV):
            gt = gt + g_ref[k]
        go_ref[...] = gt
        d, mn, vn = _adamw_math(w_ref[...], gt, m_ref[...], v_ref[...])
        d_ref[...] = d
        mo_ref[...] = mn
        vo_ref[...] = vn

    return pl.pallas_call(
        body, name=name, out_shape=[jax.ShapeDtypeStruct((r, c), F32)] * 4,
        compiler_params=pltpu.CompilerParams(vmem_limit_bytes=VMEM_LIMIT),
    )(gathered, w, m, v)


SMALL_UNIT = 1024


def _pack(parts):
    flat = []
    for p in parts:
        f = p.reshape(-1).astype(F32)
        pad = (-f.shape[0]) % SMALL_UNIT
        flat.append(jnp.pad(f, (0, pad)))
    return jnp.concatenate(flat).reshape(-1, 128)


def _unpack(buf, shapes):
    flat = buf.reshape(-1)
    out, off = [], 0
    for s in shapes:
        nel = math.prod(s)
        out.append(flat[off:off + nel].reshape(s))
        off += nel + ((-nel) % SMALL_UNIT)
    return out


def kernel(x, p, ffn1_w_in, ffn1_w_out, ln1_g, ln1_b, mix_w_in, conv_w, conv_b, conv_w_out, ssm_lam_re, ssm_lam_im, ssm_log_step, ssm_b_re, ssm_b_im, ssm_c_re, ssm_c_im, ssm_d, ssm_w_glu, mix_w_out, ln2_g, ln2_b, ffn2_w_in, ffn2_w_out, ln3_g, ln3_b, ple_w_in, ple_w_gate, ln4_g, ln4_b, loss_target, m_ffn1_w_in, m_ffn1_w_out, m_ln1_g, m_ln1_b, m_mix_w_in, m_conv_w, m_conv_b, m_conv_w_out, m_ssm_lam_re, m_ssm_lam_im, m_ssm_log_step, m_ssm_b_re, m_ssm_b_im, m_ssm_c_re, m_ssm_c_im, m_ssm_d, m_ssm_w_glu, m_mix_w_out, m_ln2_g, m_ln2_b, m_ffn2_w_in, m_ffn2_w_out, m_ln3_g, m_ln3_b, m_ple_w_in, m_ple_w_gate, m_ln4_g, m_ln4_b, v_ffn1_w_in, v_ffn1_w_out, v_ln1_g, v_ln1_b, v_mix_w_in, v_conv_w, v_conv_b, v_conv_w_out, v_ssm_lam_re, v_ssm_lam_im, v_ssm_log_step, v_ssm_b_re, v_ssm_b_im, v_ssm_c_re, v_ssm_c_im, v_ssm_d, v_ssm_w_glu, v_mix_w_out, v_ln2_g, v_ln2_b, v_ffn2_w_in, v_ffn2_w_out, v_ln3_g, v_ln3_b, v_ple_w_in, v_ple_w_gate, v_ln4_g, v_ln4_b):
    args = dict(locals())
    big = ['ffn1_w_in', 'ffn1_w_out', 'mix_w_in', 'conv_w_out', 'ssm_w_glu', 'mix_w_out',
           'ffn2_w_in', 'ffn2_w_out', 'ple_w_in', 'ple_w_gate']
    small = ['ln1_g', 'ln1_b', 'conv_b', 'ssm_lam_re', 'ssm_lam_im', 'ssm_log_step', 'ssm_b_re', 'ssm_b_im',
             'ssm_c_re', 'ssm_c_im', 'ssm_d', 'ln2_g', 'ln2_b', 'ln3_g', 'ln3_b', 'ln4_g', 'ln4_b']
    order = ['ffn1_w_in', 'ffn1_w_out', 'ln1_g', 'ln1_b', 'mix_w_in', 'conv_w', 'conv_b', 'conv_w_out',
             'ssm_lam_re', 'ssm_lam_im', 'ssm_log_step', 'ssm_b_re', 'ssm_b_im', 'ssm_c_re', 'ssm_c_im', 'ssm_d',
             'ssm_w_glu', 'mix_w_out', 'ln2_g', 'ln2_b', 'ffn2_w_in', 'ffn2_w_out', 'ln3_g', 'ln3_b',
             'ple_w_in', 'ple_w_gate', 'ln4_g', 'ln4_b']

    t = x.shape[1]
    d = x.shape[2]
    xc_, yc_, cc_ = lax.axis_index("x"), lax.axis_index("y"), lax.axis_index("c")
    me = (4 * xc_ + 2 * yc_ + cc_).astype(jnp.int32)
    cw_cols = conv_w.shape[2]

    shards = [args[nm][0].astype(BF16) for nm in big]
    cw_pad = jnp.zeros((8, 128), F32).at[0:3, 0:cw_cols].set(conv_w[0])
    gathered = _all_gather(shards + [cw_pad], name="gather_weights")
    wg = dict(zip(big, gathered[:-1]))
    cw_full = jnp.transpose(gathered[-1][:, 0:3, 0:cw_cols], (1, 0, 2)).reshape(3, N_DEV * cw_cols)
    cw8 = jnp.zeros((8, CONV_CH), F32).at[0:3, :].set(cw_full)
    w1in, w2in = wg['ffn1_w_in'], wg['ffn2_w_in']
    wf = w1in.shape[2]
    w1out = wg['ffn1_w_out'].reshape(4, wf, d)
    w2out = wg['ffn2_w_out'].reshape(4, wf, d)
    wmix = wg['mix_w_in']
    wco = wg['conv_w_out']
    wglu = wg['ssm_w_glu']
    wmo = wg['mix_w_out'].reshape(d, d)
    wpin = wg['ple_w_in']
    wgate = wg['ple_w_gate'].reshape(d, d)

    s5_in = (ssm_lam_re[0], ssm_lam_im[0], ssm_log_step[0], ssm_b_re[0], ssm_b_im[0])
    (a_re, a_im, bb_re, bb_im), s5_vjp = jax.vjp(_s5_discretise, *s5_in)
    tab_f = _pow_table(a_re, a_im)
    tab_b = _pow_table(a_re, -a_im, descending=True)
    bmat_b = _compact(jnp.transpose(bb_re, (0, 2, 1)), jnp.transpose(bb_im, (0, 2, 1))).astype(BF16)
    cmat_tb = _compact(ssm_c_re[0], -ssm_c_im[0]).astype(BF16)
    dvec = ssm_d[0].reshape(1, SSM_W)

    xf = x[0]
    x_b = xf.astype(BF16)
    p_b = p[0, 0].astype(BF16)
    tgt = loss_target[0]
    tq = min(512, t)

    def ffn_fwd(xin_f, xin_b, win, wout, gam, bet, tag):
        a, h = _ffn_in(xin_b, win, name=f"{tag}_in", tm=tq)
        f = _mm(a, wout, name=f"{tag}_out", ja='c', jb='c', nj=4, tm=tq, tn=d, tk=wf)
        y, yb = _ln_fwd(xin_f, [f], gam, bet, name=f"{tag}_ln", fs=0.5)
        return h, a, f, y, yb

    h1, a1, f1, x1, x1b = ffn_fwd(xf, x_b, w1in, w1out, ln1_g, ln1_b, "ffn1")
    proj = _mm(x1b, wmix, name="mix_in", jb='b', jo='b', o_flat=True, nj=8, tm=tq, tn=512, tk=d)
    ycin = _conv_fwd(proj, cw8, conv_b, name="conv_fwd")
    yconv = _mm(ycin, wco, name="conv_out", jb='b', jo='b', o_flat=True, nj=8, tm=tq, tn=128, tk=CONV_CH)
    s_f, s_b = _scan_fwd(proj, bmat_b, tab_f, name="scan_fwd")
    blk = dict(ja='b', jb='b', jo='b', nj=SCAN_NCB)
    ymm = _mm(s_b, cmat_tb, name="ssm_read", a_flat=True, o_flat=True, tb=True, tm=tq, tn=SCAN_UW, tk=2 * SCAN_CB, **blk)
    ys, sg, u_b = _s5_out(ymm, proj, dvec, name="ssm_out")
    glu = _mm(sg, wglu, name="glu_in", jb='b', jo='b', o_flat=True, nj=8, tm=tq, tn=256, tk=SSM_W)
    merged = _gate_fwd(yconv, glu, proj, name="gate_fwd")
    mix = _mm(merged, wmo, name="mix_out", tm=tq, tn=d, tk=d)
    x2, x2b = _ln_fwd(x1, [mix], ln2_g, ln2_b, name="mix_ln")
    h2, a2, f2, x3, x3b = ffn_fwd(x2, x2b, w2in, w2out, ln3_g, ln3_b, "ffn2")
    pe = _mm(p_b, wpin, name="ple_in", jb='b', jo='b', o_flat=True, nj=8, tm=tq, tn=128, tk=p_b.shape[1])
    gp = _mm(x3b, wgate, name="ple_gate", tm=tq, tn=d, tk=d)

    dr4, dpe_b, dgp_b, dg4, db4, loss_part = _ln_bwd(x3, [pe, gp], ln4_g, ln4_b, [], name="ple_ln_bwd",
                                                     ple=True, target=tgt)
    g_ple_in = _mm(p_b, dpe_b, name="ple_in_wg", jb='b', jo='b', b_flat=True, ta=True, nj=8,
                   tm=p_b.shape[1], tn=128, tk=tq)
    g_ple_gate = _mm(x3b, dgp_b, name="ple_gate_wg", ta=True, tm=d, tn=d, tk=tq)
    dx3_g = _mm(dgp_b, wgate, name="ple_gate_dg", tb=True, tm=tq, tn=d, tk=d)

    def ffn_bwd(xin_f, xin_b, h, a, f, win, wout, gam, bet, grads, tag):
        dr, df_b, dgam, dbet = _ln_bwd(xin_f, [f], gam, bet, grads, name=f"{tag}_ln_bwd", fs=0.5)
        dh = _ffn_out_dg(df_b, wout, h, name=f"{tag}_out_dg", tm=tq)
        g_out = _mm(a, df_b, name=f"{tag}_out_wg", ja='b', jo='b', ta=True, nj=4, tm=wf, tn=d, tk=tq)
        dx = _mm(dh, win, name=f"{tag}_in_dg", ja='c', jb='c', tb=True, nj=8, tm=tq, tn=d, tk=wf)
        g_in = _mm(xin_b, dh, name=f"{tag}_in_wg", jb='b', jo='b', ta=True, nj=8, tm=d, tn=wf, tk=tq)
        return dr, dx, g_in, g_out, dgam, dbet

    dr3, dx2_f, g_ffn2_in, g_ffn2_out, dg3, db3 = ffn_bwd(
        x2, x2b, h2, a2, f2, w2in, w2out, ln3_g, ln3_b, [(dr4, ALPHA), (dx3_g, 1.0)], "ffn2")

    dr2, dmix_b, dg2, db2 = _ln_bwd(x1, [mix], ln2_g, ln2_b, [(dr3, ALPHA), (dx2_f, 1.0)], name="mix_ln_bwd")
    dmerged = _mm(dmix_b, wmo, name="mix_out_dg", tb=True, tm=tq, tn=d, tk=d)
    g_mix_out = _mm(merged, dmix_b, name="mix_out_wg", ta=True, tm=d, tn=d, tk=tq)
    dyconv_b, dglu_b, dgate_b = _gate_bwd(dmerged, yconv, glu, proj, name="gate_bwd")
    g_conv_out = _mm(ycin, dyconv_b, name="conv_out_wg", jb='b', jo='b', b_flat=True, ta=True, nj=8,
                     tm=CONV_CH, tn=128, tk=tq)
    dycin = _mm(dyconv_b, wco, name="conv_out_dg", ja='c', jb='c', a_flat=True, tb=True, nj=8,
                tm=tq, tn=CONV_CH, tk=128)
    g_glu = _mm(sg, dglu_b, name="glu_in_wg", jb='b', jo='b', b_flat=True, ta=True, nj=8,
                tm=SSM_W, tn=256, tk=tq)
    dsg = _mm(dglu_b, wglu, name="glu_in_dg", ja='c', jb='c', a_flat=True, tb=True, nj=8,
              tm=tq, tn=SSM_W, tk=256)
    dys, dys_b, dd = _s5_bwd_in(dsg, ys, proj, name="ssm_out_bwd")
    h_b, da_acc = _scan_bwd(dys_b, cmat_tb, s_f, tab_b, name="scan_bwd")
    dumm = _mm(h_b, bmat_b, name="ssm_write_dg", a_flat=True, o_flat=True, tb=True, tm=tq, tn=SCAN_UW,
               tk=2 * SCAN_CB, **blk)
    du_b = _s5_du(dumm, dys, dvec, name="ssm_du")
    g_bmat = _mm(u_b, h_b, name="ssm_write_wg", a_flat=True, b_flat=True, ta=True, tm=SCAN_UW, tn=2 * SCAN_CB,
                 tk=tq, **blk)
    g_cmat = _mm(dys_b, s_b, name="ssm_read_wg", a_flat=True, b_flat=True, ta=True, tm=SCAN_UW, tn=2 * SCAN_CB,
                 tk=tq, **blk)
    dcb_b, dcc_b, dch_b, dconv = _conv_bwd(dycin, proj, cw8, conv_b, name="conv_bwd")
    dproj = jnp.concatenate([dcb_b, dcc_b, dch_b, du_b, dgate_b], axis=1)
    g_mix_in = _mm(x1b, dproj, name="mix_in_wg", jb='b', jo='b', b_flat=True, ta=True, nj=8,
                   tm=d, tn=512, tk=tq)
    dx1_m = _mm(dproj, wmix, name="mix_in_dg", ja='c', jb='c', a_flat=True, tb=True, nj=8,
                tm=tq, tn=d, tk=512)

    dr1, dx_f, g_ffn1_in, g_ffn1_out, dg1, db1 = ffn_bwd(
        xf, x_b, h1, a1, f1, w1in, w1out, ln1_g, ln1_b, [(dr2, ALPHA), (dx1_m, 1.0)], "ffn1")
    grad_x = _axpy(dr1, ALPHA, dx_f, name="grad_x")

    da_sum = jnp.sum(da_acc, axis=0)
    da_re, da_im = _unperm_cols(da_sum)
    gbb_re, gbb_im = [jnp.transpose(v, (0, 2, 1)) for v in _compact_extract(g_bmat, SSM_GROUP)]
    g_c_re, g_c_im_neg = _compact_extract(g_cmat, SSM_GROUP)
    g_c_im = -g_c_im_neg
    g_lam_re, g_lam_im, g_log_step, g_b_re, g_b_im = s5_vjp(
        (da_re.reshape(SSM_GROUPS, SSM_STATE), da_im.reshape(SSM_GROUPS, SSM_STATE), gbb_re, gbb_im))
    g_d = dd.reshape(SSM_GROUPS, SSM_GROUP)

    small_g = {'ln1_g': dg1, 'ln1_b': db1, 'conv_b': dconv[3:4], 'ssm_lam_re': g_lam_re, 'ssm_lam_im': g_lam_im,
               'ssm_log_step': g_log_step, 'ssm_b_re': g_b_re, 'ssm_b_im': g_b_im, 'ssm_c_re': g_c_re,
               'ssm_c_im': g_c_im, 'ssm_d': g_d, 'ln2_g': dg2, 'ln2_b': db2, 'ln3_g': dg3, 'ln3_b': db3,
               'ln4_g': dg4, 'ln4_b': db4}
    small_shapes = [args[nm].shape for nm in small] + [(3, CONV_CH), (1,)]
    g_pack = _pack([small_g[nm] for nm in small] + [dconv[0:3], loss_part[0:1, 0:1]])
    (g_all,) = _all_gather([g_pack], name="gather_small")

    def full_cw(a):
        return lax.dynamic_update_slice(jnp.zeros((3, CONV_CH), F32), a[0], (0, me * cw_cols))

    zero1 = jnp.zeros((1,), F32)
    w_pack = _pack([args[nm] for nm in small] + [full_cw(conv_w), zero1])
    m_pack = _pack([args['m_' + nm] for nm in small] + [full_cw(m_conv_w), zero1])
    v_pack = _pack([args['v_' + nm] for nm in small] + [full_cw(v_conv_w), zero1])
    sg_sum, sd, sm, sv = _small_update(g_all, w_pack, m_pack, v_pack, name="small_update")
    res = {}
    for key, buf in (('grad_', sg_sum), ('delta_', sd), ('new_m_', sm), ('new_v_', sv)):
        parts = _unpack(buf, small_shapes)
        for nm, val in zip(small, parts[:len(small)]):
            res[key + nm] = val
        res[key + 'conv_w'] = lax.dynamic_slice(parts[len(small)], (0, me * cw_cols), (3, cw_cols))[None]
        if key == 'grad_':
            loss = parts[-1][0]

    big_g = {'ffn1_w_in': g_ffn1_in, 'ffn1_w_out': g_ffn1_out, 'mix_w_in': g_mix_in, 'conv_w_out': g_conv_out,
             'ssm_w_glu': g_glu, 'mix_w_out': g_mix_out, 'ffn2_w_in': g_ffn2_in, 'ffn2_w_out': g_ffn2_out,
             'ple_w_in': g_ple_in, 'ple_w_gate': g_ple_gate}
    blocked = [big_g[nm].reshape((N_DEV,) + args[nm].shape[1:]) for nm in big]
    sib_parts = _sibling_exchange(blocked, name="grad_sibling")
    ids = jnp.stack([_dev(1 - xc_, yc_, cc_), _dev(xc_, 1 - yc_, cc_), _dev(1 - xc_, 1 - yc_, cc_)]).astype(jnp.int32)
    chip_parts = [_chip_partial(g, s, ids, name=f"chip_sum_{nm}") for nm, g, s in zip(big, blocked, sib_parts)]
    rem_parts = _chip_exchange(chip_parts, name="grad_chips")
    me1 = me.reshape(1)
    for nm, g, s, r in zip(big, blocked, sib_parts, rem_parts):
        gt, dl, mn, vn = _shard_update(g, s, r, me1, args[nm][0], args['m_' + nm][0], args['v_' + nm][0],
                                       name=f"update_{nm}")
        res['grad_' + nm], res['delta_' + nm], res['new_m_' + nm], res['new_v_' + nm] = gt[None], dl[None], mn[None], vn[None]

    outs = [loss, grad_x[None]]
    for key in ('grad_', 'delta_', 'new_m_', 'new_v_'):
        outs += [res[key + nm] for nm in order]
    return tuple(outs)
```

```python
import functools
import math

import jax
import jax.numpy as jnp
from jax import lax
from jax.experimental import pallas as pl
from jax.experimental.pallas import tpu as pltpu

F32 = jnp.float32
BF16 = jnp.bfloat16
MESH = pl.DeviceIdType.MESH

N_DEV = 8
ALPHA = 2.0 ** 0.25
LN_EPS = 1e-5
CONV_CH = 512
SSM_W = 512
SSM_GROUPS = 32
SSM_GROUP = 16
SSM_STATE = 64
SSM_CH = SSM_GROUPS * SSM_STATE
SCAN_CB = 512
SCAN_NCB = SSM_CH // SCAN_CB
SCAN_GPB = SSM_GROUPS // SCAN_NCB
SCAN_UW = SCAN_GPB * SSM_GROUP
SCAN_TT = 256
ADAM_LR = 0.001
ADAM_B1 = 0.9
ADAM_B2 = 0.999
ADAM_EPS = 1e-08
ADAM_WD = 0.01
ADAM_STEP = 10
VMEM_LIMIT = 56 * 1024 * 1024


def _cparams(sem=None, **kw):
    return pltpu.CompilerParams(dimension_semantics=sem, vmem_limit_bytes=VMEM_LIMIT, **kw)


def _mm(a, b, *, name, ja=None, jb=None, jo=None, a_flat=False, b_flat=False, o_flat=False,
        ta=False, tb=False, tm, tn, tk, nj=1, out_dtype=F32, plans=()):
    def dims(arr, j, flat):
        if j is None:
            return arr.shape
        if flat:
            return (arr.shape[0], arr.shape[1] // nj)
        assert arr.shape[0] == nj, (name, arr.shape, nj)
        return arr.shape[1:]

    ar, ac = dims(a, ja, a_flat)
    br, bc = dims(b, jb, b_flat)
    m, k = (ac, ar) if ta else (ar, ac)
    k2, n = (bc, br) if tb else (br, bc)
    assert k == k2, (name, a.shape, b.shape)
    assert m % tm == 0 and n % tn == 0 and k % tk == 0, (name, m, n, k, tm, tn, tk)
    njb = nj if 'b' in (ja, jb) else 1
    njc = nj if 'c' in (ja, jb) else 1
    nk = k // tk
    nred = njc * nk
    grid = (njb, m // tm, n // tn, njc, nk)

    def make_spec(j, flat, blk, rfn, cfn, cols_per_j):
        def jsel(g, c):
            return g if j == 'b' else c
        if j is None:
            return pl.BlockSpec(blk, lambda g, i, jn, c, kk: (rfn(i, jn, kk), cfn(i, jn, kk)))
        if flat:
            nb = cols_per_j // blk[1]
            return pl.BlockSpec(blk, lambda g, i, jn, c, kk: (rfn(i, jn, kk), jsel(g, c) * nb + cfn(i, jn, kk)))
        return pl.BlockSpec((None,) + blk,
                            lambda g, i, jn, c, kk: (jsel(g, c), rfn(i, jn, kk), cfn(i, jn, kk)))

    if ta:
        a_spec = make_spec(ja, a_flat, (tk, tm), lambda i, jn, kk: kk, lambda i, jn, kk: i, ac)
    else:
        a_spec = make_spec(ja, a_flat, (tm, tk), lambda i, jn, kk: i, lambda i, jn, kk: kk, ac)
    if tb:
        b_spec = make_spec(jb, b_flat, (tn, tk), lambda i, jn, kk: jn, lambda i, jn, kk: kk, bc)
    else:
        b_spec = make_spec(jb, b_flat, (tk, tn), lambda i, jn, kk: kk, lambda i, jn, kk: jn, bc)
    o_spec = make_spec(jo, o_flat, (tm, tn), lambda i, jn, kk: i, lambda i, jn, kk: jn, n)
    if jo is None:
        out_shape = (m, n)
    elif o_flat:
        out_shape = (m, nj * n)
    else:
        out_shape = (nj, m, n)

    dn = (((0 if ta else 1,), (1 if tb else 0,)), ((), ()))

    def body(a_ref, b_ref, o_ref, *scratch):
        p = lax.dot_general(a_ref[...], b_ref[...], dn, preferred_element_type=F32)
        if nred == 1:
            o_ref[...] = p.astype(o_ref.dtype)
        else:
            acc = scratch[0]
            r = pl.program_id(3) * nk + pl.program_id(4)

            @pl.when(r == 0)
            def _():
                acc[...] = p

            @pl.when(r > 0)
            def _():
                acc[...] += p

            @pl.when(r == nred - 1)
            def _():
                o_ref[...] = acc[...].astype(o_ref.dtype)

    res = _call_with_plans(
        body, plans, name=name, grid=grid, in_specs=[a_spec, b_spec], out_specs=[o_spec],
        out_shape=[jax.ShapeDtypeStruct(out_shape, out_dtype)],
        scratch_shapes=[] if nred == 1 else [pltpu.VMEM((tm, tn), F32)],
        semantics=("parallel", "parallel", "parallel", "arbitrary", "arbitrary"), operands=(a, b))
    return (res[0][0], res[1]) if plans else res[0][0]


def _sigmoid(v):
    return 1.0 / (1.0 + jnp.exp(-v))


def _row_spec(tm, cols, colblk=0):
    return pl.BlockSpec((tm, cols), lambda i: (i, colblk))


def _vec_spec(cols):
    return pl.BlockSpec((1, cols), lambda i: (0, 0))


def _ffn_in(xb, win, *, name, tm, plans=()):
    t, d = xb.shape
    nj, _, w = win.shape
    half = nj // 2

    def body(x_ref, wg_ref, wu_ref, a_ref, gu_ref):
        xv = x_ref[...]
        g = jnp.dot(xv, wg_ref[...], preferred_element_type=F32)
        u = jnp.dot(xv, wu_ref[...], preferred_element_type=F32)
        a_ref[...] = (g * _sigmoid(g) * u).astype(BF16)
        gu_ref[0] = g.astype(BF16)
        gu_ref[1] = u.astype(BF16)

    (a, gu), riders = _call_with_plans(
        body, plans, name=name, grid=(half, t // tm),
        in_specs=[pl.BlockSpec((tm, d), lambda j, i: (i, 0)),
                  pl.BlockSpec((None, d, w), lambda j, i: (j, 0, 0)),
                  pl.BlockSpec((None, d, w), lambda j, i: (j + half, 0, 0))],
        out_specs=[pl.BlockSpec((None, tm, w), lambda j, i: (j, i, 0)),
                   pl.BlockSpec((2, None, tm, w), lambda j, i: (0, j, i, 0))],
        out_shape=[jax.ShapeDtypeStruct((half, t, w), BF16), jax.ShapeDtypeStruct((2, half, t, w), BF16)],
        scratch_shapes=[], semantics=("parallel", "parallel"), operands=(xb, win, win))
    return a, gu, riders


def _ffn_out_dg(dfb, wout, gu, *, name, tm, plans=()):
    t, d = dfb.shape
    half, w, _ = wout.shape
    dn = (((1,), (1,)), ((), ()))

    def body(df_ref, w_ref, gu_ref, dh_ref):
        da = lax.dot_general(df_ref[...], w_ref[...], dn, preferred_element_type=F32)
        g = gu_ref[0].astype(F32)
        u = gu_ref[1].astype(F32)
        sg = _sigmoid(g)
        dh_ref[0] = (da * u * (sg * (1.0 + g * (1.0 - sg)))).astype(BF16)
        dh_ref[1] = (da * (g * sg)).astype(BF16)

    (out,), riders = _call_with_plans(
        body, plans, name=name, grid=(half, t // tm),
        in_specs=[pl.BlockSpec((tm, d), lambda j, i: (i, 0)),
                  pl.BlockSpec((None, w, d), lambda j, i: (j, 0, 0)),
                  pl.BlockSpec((2, None, tm, w), lambda j, i: (0, j, i, 0))],
        out_specs=[pl.BlockSpec((2, None, tm, w), lambda j, i: (0, j, i, 0))],
        out_shape=[jax.ShapeDtypeStruct((2, half, t, w), BF16)],
        scratch_shapes=[], semantics=("parallel", "parallel"), operands=(dfb, wout, gu))
    return out.reshape(2 * half, t, w), riders


def _ln_stats(r):
    mu = jnp.mean(r, axis=-1, keepdims=True)
    xc = r - mu
    var = jnp.mean(xc * xc, axis=-1, keepdims=True)
    rstd = lax.rsqrt(var + LN_EPS)
    return xc * rstd, rstd


def _ln_fwd(xin, fparts, gamma, beta, *, name, fs=1.0, ple=False, tm=256):
    t, d = xin.shape
    nf = len(fparts)

    def body(*refs):
        x_ref = refs[0]
        f_refs = refs[1:1 + nf]
        g_ref, b_ref, y_ref, yb_ref = refs[1 + nf:]
        if ple:
            f = f_refs[0][...] * _sigmoid(f_refs[1][...])
        else:
            f = fs * f_refs[0][...]
        xh, _ = _ln_stats(ALPHA * x_ref[...] + f)
        y = xh * g_ref[...] + b_ref[...]
        y_ref[...] = y
        yb_ref[...] = y.astype(BF16)

    return pl.pallas_call(
        body, name=name, grid=(t // tm,),
        in_specs=[_row_spec(tm, d)] * (1 + nf) + [_vec_spec(d), _vec_spec(d)],
        out_specs=[_row_spec(tm, d), _row_spec(tm, d)],
        out_shape=[jax.ShapeDtypeStruct((t, d), F32), jax.ShapeDtypeStruct((t, d), BF16)],
        compiler_params=_cparams(("parallel",)),
    )(xin, *fparts, gamma, beta)


def _ln_bwd(xin, fparts, gamma, beta, grads, *, name, fs=1.0, ple=False, target=None, tm=256):
    t, d = xin.shape
    nf = len(fparts)
    ng = len(grads)
    coefs = [c for _, c in grads]
    use_t = target is not None
    n_fout = 2 if ple else 1

    def body(*refs):
        pos = 0
        x_ref = refs[pos]; pos += 1
        f_refs = refs[pos:pos + nf]; pos += nf
        g_ref, b_ref = refs[pos:pos + 2]; pos += 2
        gr_refs = refs[pos:pos + ng]; pos += ng
        if use_t:
            t_ref = refs[pos]; pos += 1
        dr_ref = refs[pos]; pos += 1
        fo_refs = refs[pos:pos + n_fout]; pos += n_fout
        dg_ref, db_ref = refs[pos:pos + 2]; pos += 2
        if use_t:
            loss_ref = refs[pos]; pos += 1
        i = pl.program_id(0)

        if ple:
            pe = f_refs[0][...]
            sg = _sigmoid(f_refs[1][...])
            f = pe * sg
        else:
            f = fs * f_refs[0][...]
        xh, rstd = _ln_stats(ALPHA * x_ref[...] + f)
        gam = g_ref[...]
        if use_t:
            diff = xh * gam + b_ref[...] - t_ref[...]
            dy = diff * (1.0 / d)
            lpart = 0.5 * jnp.sum(jnp.sum(diff * diff, axis=-1, keepdims=True), axis=0, keepdims=True) * (1.0 / d)
        else:
            dy = coefs[0] * gr_refs[0][...]
            for c, r in zip(coefs[1:], gr_refs[1:]):
                dy = dy + c * r[...]
        dxh = dy * gam
        m1 = jnp.mean(dxh, axis=-1, keepdims=True)
        m2 = jnp.mean(dxh * xh, axis=-1, keepdims=True)
        dr = rstd * (dxh - m1 - xh * m2)
        dr_ref[...] = dr
        if ple:
            fo_refs[0][...] = (dr * sg).astype(BF16)
            fo_refs[1][...] = (dr * pe * (sg * (1.0 - sg))).astype(BF16)
        else:
            fo_refs[0][...] = (fs * dr).astype(BF16)
        dgp = jnp.sum(dy * xh, axis=0, keepdims=True)
        dbp = jnp.sum(dy, axis=0, keepdims=True)

        @pl.when(i == 0)
        def _():
            dg_ref[...] = dgp
            db_ref[...] = dbp
            if use_t:
                loss_ref[...] = jnp.broadcast_to(lpart, loss_ref.shape)

        @pl.when(i > 0)
        def _():
            dg_ref[...] += dgp
            db_ref[...] += dbp
            if use_t:
                loss_ref[...] += jnp.broadcast_to(lpart, loss_ref.shape)

    ins = [xin, *fparts, gamma, beta, *[g for g, _ in grads]] + ([target] if use_t else [])
    in_specs = ([_row_spec(tm, d)] * (1 + nf) + [_vec_spec(d), _vec_spec(d)] + [_row_spec(tm, d)] * ng
                + ([_row_spec(tm, d)] if use_t else []))
    out_specs = [_row_spec(tm, d)] * (1 + n_fout) + [_vec_spec(d), _vec_spec(d)] + ([_vec_spec(128)] if use_t else [])
    out_shape = ([jax.ShapeDtypeStruct((t, d), F32)] + [jax.ShapeDtypeStruct((t, d), BF16)] * n_fout
                 + [jax.ShapeDtypeStruct((1, d), F32)] * 2 + ([jax.ShapeDtypeStruct((1, 128), F32)] if use_t else []))
    return pl.pallas_call(
        body, name=name, grid=(t // tm,), in_specs=in_specs, out_specs=out_specs, out_shape=out_shape,
        compiler_params=_cparams(("arbitrary",)),
    )(*ins)


def _axpy(a, ca, b, *, name, tm=256):
    t, d = a.shape

    def body(a_ref, b_ref, o_ref):
        o_ref[...] = ca * a_ref[...] + b_ref[...]

    return pl.pallas_call(
        body, name=name, grid=(t // tm,), in_specs=[_row_spec(tm, d)] * 2, out_specs=_row_spec(tm, d),
        out_shape=jax.ShapeDtypeStruct((t, d), F32), compiler_params=_cparams(("parallel",)),
    )(a, b)


def _conv_fwd(proj, cw, cb, *, name, tm=256):
    t = proj.shape[0]
    c = CONV_CH
    hb = tm // 8

    def body(b_ref, c_ref, h_ref, cp_ref, hp_ref, w_ref, bias_ref, o_ref, q_scr):
        i = pl.program_id(0)
        q = c_ref[...] * h_ref[...]
        halo = jnp.where(i > 0, cp_ref[...] * hp_ref[...], 0.0)
        q_scr[0:8, :] = halo
        q_scr[8:, :] = q
        z = (w_ref[2:3, :] * q + w_ref[1:2, :] * q_scr[pl.ds(7, tm), :] + w_ref[0:1, :] * q_scr[pl.ds(6, tm), :]
             + bias_ref[...])
        o_ref[...] = (b_ref[...] * z).astype(BF16)

    prev = lambda blk: pl.BlockSpec((8, c), lambda i: (jnp.maximum(i * hb - 1, 0), blk))
    return pl.pallas_call(
        body, name=name, grid=(t // tm,),
        in_specs=[_row_spec(tm, c, 0), _row_spec(tm, c, 1), _row_spec(tm, c, 2), prev(1), prev(2),
                  pl.BlockSpec((8, c), lambda i: (0, 0)), _vec_spec(c)],
        out_specs=_row_spec(tm, c),
        out_shape=jax.ShapeDtypeStruct((t, c), BF16),
        scratch_shapes=[pltpu.VMEM((tm + 8, c), F32)],
        compiler_params=_cparams(("parallel",)),
    )(proj, proj, proj, proj, proj, cw, cb)


def _conv_bwd(dyc, proj, cw, cb, *, name, tm=256):
    t = proj.shape[0]
    c = CONV_CH
    hb = tm // 8
    nblk = t // 8

    def body(d_ref, b_ref, c_ref, h_ref, cp_ref, hp_ref, dn_ref, bn_ref, w_ref, bias_ref,
             db_ref, dc_ref, dh_ref, dw_ref, q_scr, z_scr):
        i = pl.program_id(0)
        last = pl.num_programs(0) - 1
        cc = c_ref[...]
        ch = h_ref[...]
        q = cc * ch
        halo = jnp.where(i > 0, cp_ref[...] * hp_ref[...], 0.0)
        q_scr[0:8, :] = halo
        q_scr[8:, :] = q
        w0, w1, w2 = w_ref[0:1, :], w_ref[1:2, :], w_ref[2:3, :]
        qm1 = q_scr[pl.ds(7, tm), :]
        qm2 = q_scr[pl.ds(6, tm), :]
        z = w2 * q + w1 * qm1 + w0 * qm2 + bias_ref[...]
        d = d_ref[...]
        bb = b_ref[...]
        db_ref[...] = (d * z).astype(BF16)
        dz = d * bb
        z_scr[0:tm, :] = dz
        z_scr[tm:, :] = jnp.where(i < last, dn_ref[...] * bn_ref[...], 0.0)
        dq = w2 * dz + w1 * z_scr[pl.ds(1, tm), :] + w0 * z_scr[pl.ds(2, tm), :]
        dc_ref[...] = (dq * ch).astype(BF16)
        dh_ref[...] = (dq * cc).astype(BF16)
        row = lax.broadcasted_iota(jnp.int32, (8, c), 0)
        part = jnp.zeros((8, c), F32)
        for k, term in enumerate((dz * qm2, dz * qm1, dz * q, dz)):
            part = jnp.where(row == k, jnp.sum(term, axis=0, keepdims=True), part)

        @pl.when(i == 0)
        def _():
            dw_ref[...] = part

        @pl.when(i > 0)
        def _():
            dw_ref[...] += part

    prev = lambda blk: pl.BlockSpec((8, c), lambda i: (jnp.maximum(i * hb - 1, 0), blk))
    nxt_p = pl.BlockSpec((8, c), lambda i: (jnp.minimum((i + 1) * hb, nblk - 1), 0))
    nxt_d = pl.BlockSpec((8, c), lambda i: (jnp.minimum((i + 1) * hb, nblk - 1), 0))
    return pl.pallas_call(
        body, name=name, grid=(t // tm,),
        in_specs=[_row_spec(tm, c), _row_spec(tm, c, 0), _row_spec(tm, c, 1), _row_spec(tm, c, 2),
                  prev(1), prev(2), nxt_d, nxt_p, pl.BlockSpec((8, c), lambda i: (0, 0)), _vec_spec(c)],
        out_specs=[_row_spec(tm, c)] * 3 + [pl.BlockSpec((8, c), lambda i: (0, 0))],
        out_shape=[jax.ShapeDtypeStruct((t, c), BF16)] * 3 + [jax.ShapeDtypeStruct((8, c), F32)],
        scratch_shapes=[pltpu.VMEM((tm + 8, c), F32), pltpu.VMEM((tm + 8, c), F32)],
        compiler_params=_cparams(("arbitrary",)),
    )(dyc, proj, proj, proj, proj, proj, dyc, proj, cw, cb)


def _gate_fwd(yconv, glu, proj, *, name, tm=256):
    t, d = yconv.shape

    def body(yc_ref, ga_ref, gb_ref, gc_ref, gs_ref, o_ref):
        yssm = ga_ref[...] * _sigmoid(gb_ref[...])
        o_ref[...] = (_sigmoid(gc_ref[...]) * yc_ref[...] + _sigmoid(gs_ref[...]) * yssm).astype(BF16)

    return pl.pallas_call(
        body, name=name, grid=(t // tm,),
        in_specs=[_row_spec(tm, d), _row_spec(tm, d, 0), _row_spec(tm, d, 1), _row_spec(tm, d, 2), _row_spec(tm, d, 3)],
        out_specs=_row_spec(tm, d), out_shape=jax.ShapeDtypeStruct((t, d), BF16),
        compiler_params=_cparams(("parallel",)),
    )(yconv, glu, glu, proj, proj)


def _gate_bwd(dm, yconv, glu, proj, *, name, tm=256):
    t, d = yconv.shape

    def body(dm_ref, yc_ref, ga_ref, gb_ref, gc_ref, gs_ref, dyc_ref, dglu_ref, dgate_ref):
        dmv = dm_ref[...]
        sc = _sigmoid(gc_ref[...])
        ss = _sigmoid(gs_ref[...])
        sb = _sigmoid(gb_ref[...])
        ga = ga_ref[...]
        yssm = ga * sb
        dyc_ref[...] = (dmv * sc).astype(BF16)
        dgate_ref[:, 0:d] = (dmv * yc_ref[...] * (sc * (1.0 - sc))).astype(BF16)
        dys = dmv * ss
        dgate_ref[:, d:2 * d] = (dmv * yssm * (ss * (1.0 - ss))).astype(BF16)
        dglu_ref[:, 0:d] = (dys * sb).astype(BF16)
        dglu_ref[:, d:2 * d] = (dys * ga * (sb * (1.0 - sb))).astype(BF16)

    return pl.pallas_call(
        body, name=name, grid=(t // tm,),
        in_specs=[_row_spec(tm, d), _row_spec(tm, d), _row_spec(tm, d, 0), _row_spec(tm, d, 1),
                  _row_spec(tm, d, 2), _row_spec(tm, d, 3)],
        out_specs=[_row_spec(tm, d), _row_spec(tm, 2 * d), _row_spec(tm, 2 * d)],
        out_shape=[jax.ShapeDtypeStruct((t, d), BF16), jax.ShapeDtypeStruct((t, 2 * d), BF16),
                   jax.ShapeDtypeStruct((t, 2 * d), BF16)],
        compiler_params=_cparams(("parallel",)),
    )(dm, yconv, glu, glu, proj, proj)


_GELU_C = math.sqrt(2.0 / math.pi)


def _gelu(v):
    return 0.5 * v * (1.0 + jnp.tanh(_GELU_C * (v + 0.044715 * v * v * v)))


def _gelu_grad(v):
    th = jnp.tanh(_GELU_C * (v + 0.044715 * v * v * v))
    return 0.5 * (1.0 + th) + 0.5 * v * (1.0 - th * th) * (_GELU_C * (1.0 + 3.0 * 0.044715 * v * v))


def _cmul(ar, ai, br, bi):
    return ar * br - ai * bi, ar * bi + ai * br


def _scan_fwd(proj, bmat, tab, *, name):
    t = proj.shape[0]
    tt, cbw = SCAN_TT, SCAN_CB
    w2 = 2 * cbw

    def body(u_ref, b_ref, tab_ref, s_ref, sb_ref, bu_scr, carry):
        ti = pl.program_id(1)

        @pl.when(ti == 0)
        def _():
            carry[...] = jnp.zeros_like(carry)

        bu_scr[...] = jnp.dot(u_ref[...].astype(BF16), b_ref[...], preferred_element_type=F32)
        row = lax.broadcasted_iota(jnp.int32, (8, cbw), 0)

        def blk(bi, c):
            cr, ci = c
            r0 = pl.multiple_of(bi * 8, 8)
            xr = bu_scr[pl.ds(r0, 8), 0:cbw]
            xi = bu_scr[pl.ds(r0, 8), cbw:w2]
            for k, sh in enumerate((1, 2, 4)):
                kr = tab_ref[k:k + 1, 0:cbw]
                ki = tab_ref[k:k + 1, cbw:w2]
                sr = jnp.where(row >= sh, pltpu.roll(xr, sh, 0), 0.0)
                si = jnp.where(row >= sh, pltpu.roll(xi, sh, 0), 0.0)
                pr, pi = _cmul(kr, ki, sr, si)
                xr = xr + pr
                xi = xi + pi
            pr, pi = _cmul(tab_ref[8:16, 0:cbw], tab_ref[8:16, cbw:w2], cr, ci)
            xr = xr + pr
            xi = xi + pi
            s_ref[pl.ds(r0, 8), 0:cbw] = xr
            s_ref[pl.ds(r0, 8), cbw:w2] = xi
            return (jnp.broadcast_to(xr[7:8, :], (8, cbw)), jnp.broadcast_to(xi[7:8, :], (8, cbw)))

        cr, ci = lax.fori_loop(0, tt // 8, blk, (carry[:, 0:cbw], carry[:, cbw:w2]))
        carry[:, 0:cbw] = cr
        carry[:, cbw:w2] = ci
        sb_ref[...] = s_ref[...].astype(BF16)

    return pl.pallas_call(
        body, name=name, grid=(SCAN_NCB, t // tt),
        in_specs=[pl.BlockSpec((tt, SCAN_UW), lambda cb, ti: (ti, 3 * SCAN_NCB + cb)),
                  pl.BlockSpec((None, SCAN_UW, w2), lambda cb, ti: (cb, 0, 0)),
                  pl.BlockSpec((16, w2), lambda cb, ti: (0, cb))],
        out_specs=[pl.BlockSpec((tt, w2), lambda cb, ti: (ti, cb))] * 2,
        out_shape=[jax.ShapeDtypeStruct((t, 2 * SSM_CH), F32), jax.ShapeDtypeStruct((t, 2 * SSM_CH), BF16)],
        scratch_shapes=[pltpu.VMEM((tt, w2), F32), pltpu.VMEM((8, w2), F32)],
        compiler_params=_cparams(("parallel", "arbitrary")),
    )(proj, bmat, tab)


def _scan_bwd(dyb, cmat_t, s, tabb, *, name):
    t = s.shape[0]
    tt, cbw = SCAN_TT, SCAN_CB
    w2 = 2 * cbw
    nt = t // tt
    hb = tt // 8

    def body(dy_ref, c_ref, s_ref, sp_ref, tab_ref, h_ref, da_ref, g_scr, s_scr, carry):
        ti = pl.program_id(1)

        @pl.when(ti == 0)
        def _():
            carry[...] = jnp.zeros_like(carry)
            da_ref[...] = jnp.zeros_like(da_ref)

        g_scr[...] = jnp.dot(dy_ref[...], c_ref[...], preferred_element_type=F32)
        s_scr[0:8, :] = jnp.where(ti < nt - 1, sp_ref[...], 0.0)
        s_scr[8:, :] = s_ref[...]
        row = lax.broadcasted_iota(jnp.int32, (8, cbw), 0)

        def blk(k, c):
            cr, ci, ar, ai = c
            bi = hb - 1 - k
            r0 = pl.multiple_of(bi * 8, 8)
            xr = g_scr[pl.ds(r0, 8), 0:cbw]
            xi = g_scr[pl.ds(r0, 8), cbw:w2]
            for j, sh in enumerate((1, 2, 4)):
                kr = tab_ref[j:j + 1, 0:cbw]
                ki = tab_ref[j:j + 1, cbw:w2]
                sr = jnp.where(row < 8 - sh, pltpu.roll(xr, 8 - sh, 0), 0.0)
                si = jnp.where(row < 8 - sh, pltpu.roll(xi, 8 - sh, 0), 0.0)
                pr, pi = _cmul(kr, ki, sr, si)
                xr = xr + pr
                xi = xi + pi
            pr, pi = _cmul(tab_ref[8:16, 0:cbw], tab_ref[8:16, cbw:w2], cr, ci)
            xr = xr + pr
            xi = xi + pi
            h_ref[pl.ds(r0, 8), 0:cbw] = xr.astype(BF16)
            h_ref[pl.ds(r0, 8), cbw:w2] = xi.astype(BF16)
            pvr = s_scr[pl.ds(r0, 8), 0:cbw]
            pvi = s_scr[pl.ds(r0, 8), cbw:w2]
            cur_r = s_scr[pl.ds(r0 + 8, 8), 0:cbw]
            cur_i = s_scr[pl.ds(r0 + 8, 8), cbw:w2]
            spr = jnp.where(row == 0, jnp.broadcast_to(pvr[7:8, :], (8, cbw)), pltpu.roll(cur_r, 1, 0))
            spi = jnp.where(row == 0, jnp.broadcast_to(pvi[7:8, :], (8, cbw)), pltpu.roll(cur_i, 1, 0))
            ar = ar + spr * xr + spi * xi
            ai = ai + spr * xi - spi * xr
            return (jnp.broadcast_to(xr[0:1, :], (8, cbw)), jnp.broadcast_to(xi[0:1, :], (8, cbw)), ar, ai)

        z = jnp.zeros((8, cbw), F32)
        cr, ci, ar, ai = lax.fori_loop(0, hb, blk, (carry[:, 0:cbw], carry[:, cbw:w2], z, z))
        carry[:, 0:cbw] = cr
        carry[:, cbw:w2] = ci
        da_ref[:, 0:cbw] += ar
        da_ref[:, cbw:w2] += ai

    rt = lambda ti: nt - 1 - ti
    return pl.pallas_call(
        body, name=name, grid=(SCAN_NCB, nt),
        in_specs=[pl.BlockSpec((tt, SCAN_UW), lambda cb, ti: (rt(ti), cb)),
                  pl.BlockSpec((None, SCAN_UW, w2), lambda cb, ti: (cb, 0, 0)),
                  pl.BlockSpec((tt, w2), lambda cb, ti: (rt(ti), cb)),
                  pl.BlockSpec((8, w2), lambda cb, ti: (jnp.maximum(rt(ti) * hb - 1, 0), cb)),
                  pl.BlockSpec((16, w2), lambda cb, ti: (0, cb))],
        out_specs=[pl.BlockSpec((tt, w2), lambda cb, ti: (rt(ti), cb)),
                   pl.BlockSpec((8, w2), lambda cb, ti: (0, cb))],
        out_shape=[jax.ShapeDtypeStruct((t, 2 * SSM_CH), BF16), jax.ShapeDtypeStruct((8, 2 * SSM_CH), F32)],
        scratch_shapes=[pltpu.VMEM((tt, w2), F32), pltpu.VMEM((tt + 8, w2), F32), pltpu.VMEM((8, w2), F32)],
        compiler_params=_cparams(("parallel", "arbitrary")),
    )(dyb, cmat_t, s, s, tabb)


def _s5_out(ymm, proj, dvec, *, name, tm=256):
    t, w = ymm.shape

    def body(y_ref, u_ref, d_ref, yo_ref, sg_ref, ub_ref):
        u = u_ref[...]
        y = y_ref[...] + d_ref[...] * u
        yo_ref[...] = y
        sg_ref[...] = _gelu(y).astype(BF16)
        ub_ref[...] = u.astype(BF16)

    return pl.pallas_call(
        body, name=name, grid=(t // tm,),
        in_specs=[_row_spec(tm, w), _row_spec(tm, w, 3), _vec_spec(w)],
        out_specs=[_row_spec(tm, w)] * 3,
        out_shape=[jax.ShapeDtypeStruct((t, w), F32), jax.ShapeDtypeStruct((t, w), BF16), jax.ShapeDtypeStruct((t, w), BF16)],
        compiler_params=_cparams(("parallel",)),
    )(ymm, proj, dvec)


def _s5_bwd_in(dsg, y, proj, *, name, tm=256):
    t, w = y.shape

    def body(d_ref, y_ref, u_ref, dy_ref, dyb_ref, dd_ref):
        i = pl.program_id(0)
        dy = d_ref[...] * _gelu_grad(y_ref[...])
        dy_ref[...] = dy
        dyb_ref[...] = dy.astype(BF16)
        part = jnp.sum(dy * u_ref[...], axis=0, keepdims=True)

        @pl.when(i == 0)
        def _():
            dd_ref[...] = part

        @pl.when(i > 0)
        def _():
            dd_ref[...] += part

    return pl.pallas_call(
        body, name=name, grid=(t // tm,),
        in_specs=[_row_spec(tm, w), _row_spec(tm, w), _row_spec(tm, w, 3)],
        out_specs=[_row_spec(tm, w), _row_spec(tm, w), _vec_spec(w)],
        out_shape=[jax.ShapeDtypeStruct((t, w), F32), jax.ShapeDtypeStruct((t, w), BF16), jax.ShapeDtypeStruct((1, w), F32)],
        compiler_params=_cparams(("arbitrary",)),
    )(dsg, y, proj)


def _s5_du(dumm, dy, dvec, *, name, tm=256):
    t, w = dy.shape

    def body(a_ref, dy_ref, d_ref, o_ref):
        o_ref[...] = (a_ref[...] + d_ref[...] * dy_ref[...]).astype(BF16)

    return pl.pallas_call(
        body, name=name, grid=(t // tm,), in_specs=[_row_spec(tm, w), _row_spec(tm, w), _vec_spec(w)],
        out_specs=_row_spec(tm, w), out_shape=jax.ShapeDtypeStruct((t, w), BF16),
        compiler_params=_cparams(("parallel",)),
    )(dumm, dy, dvec)


def _s5_discretise(lam_re, lam_im, log_step, b_re, b_im):
    lam = lax.complex(lam_re, lam_im)
    dt = jnp.exp(log_step)[:, None]
    a = jnp.exp(lam * dt)
    bbar = ((a - 1.0) / lam)[..., None] * lax.complex(b_re, b_im)
    return jnp.real(a), jnp.imag(a), jnp.real(bbar), jnp.imag(bbar)


def _perm_cols(re, im):
    lead = re.shape[:-1]
    r = re.reshape(lead + (SCAN_NCB, 1, SCAN_CB))
    i = im.reshape(lead + (SCAN_NCB, 1, SCAN_CB))
    return jnp.concatenate([r, i], axis=-2).reshape(lead + (2 * SSM_CH,))


def _unperm_cols(x):
    lead = x.shape[:-1]
    y = x.reshape(lead + (SCAN_NCB, 2, SCAN_CB))
    return y[..., 0, :].reshape(lead + (SSM_CH,)), y[..., 1, :].reshape(lead + (SSM_CH,))


def _compact(re, im):
    _, r, c = re.shape
    eye = jnp.eye(SCAN_GPB, dtype=re.dtype)

    def half(x):
        x = x.reshape(SCAN_NCB, SCAN_GPB, r, c)
        return (eye[None, :, None, :, None] * x[:, :, :, None, :]).reshape(SCAN_NCB, SCAN_GPB * r, SCAN_GPB * c)

    return jnp.concatenate([half(re), half(im)], axis=-1)


def _compact_extract(x, r):
    c = SSM_STATE
    eye = jnp.eye(SCAN_GPB, dtype=x.dtype)
    y = x.reshape(SCAN_NCB, SCAN_GPB, r, 2, SCAN_GPB, c)
    dg = jnp.sum(y * eye[None, :, None, None, :, None], axis=4).reshape(SSM_GROUPS, r, 2, c)
    return dg[:, :, 0, :], dg[:, :, 1, :]


def _pow_table(ar, ai, descending=False):
    ar = ar.reshape(1, SSM_CH)
    ai = ai.reshape(1, SSM_CH)
    pw = [(ar, ai)]
    for _ in range(7):
        pw.append(_cmul(pw[-1][0], pw[-1][1], ar, ai))
    zero = (jnp.zeros_like(ar), jnp.zeros_like(ar))
    rows = [pw[0], pw[1], pw[3]] + [zero] * 5 + (pw[::-1] if descending else pw)
    re = jnp.concatenate([r for r, _ in rows], axis=0)
    im = jnp.concatenate([i for _, i in rows], axis=0)
    return _perm_cols(re, im)


def _place():
    x, y, c = lax.axis_index("x"), lax.axis_index("y"), lax.axis_index("c")
    chips = [(1 - x, y), (x, 1 - y), (1 - x, 1 - y)]
    return x, y, c, chips


def _dev(px, py, pc):
    return 4 * px + 2 * py + pc


class _Plan:
    def __init__(self, ins, out_shapes, sem_shapes, start, finish):
        self.ins, self.out_shapes, self.sem_shapes = list(ins), list(out_shapes), list(sem_shapes)
        self.start, self.finish = start, finish


def _split_plan_refs(plans, in_refs, out_refs, sem_refs):
    res, i, o, s = [], 0, 0, 0
    for p in plans:
        ni, no, ns = len(p.ins), len(p.out_shapes), len(p.sem_shapes)
        res.append((in_refs[i:i + ni], out_refs[o:o + no], sem_refs[s:s + ns]))
        i, o, s = i + ni, o + no, s + ns
    return res


def _run_plans(plans, *, name):
    ins = [a for p in plans for a in p.ins]
    outs = [o for p in plans for o in p.out_shapes]
    sems = [s for p in plans for s in p.sem_shapes]
    any_spec = pl.BlockSpec(memory_space=pl.ANY)

    def body(*refs):
        parts = _split_plan_refs(plans, refs[:len(ins)], refs[len(ins):len(ins) + len(outs)], refs[len(ins) + len(outs):])
        for p, r in zip(plans, parts):
            p.start(*r)
        for p, r in zip(plans, parts):
            p.finish(*r)

    res = pl.pallas_call(body, name=name, in_specs=[any_spec] * len(ins), out_specs=[any_spec] * len(outs),
                         out_shape=outs, scratch_shapes=sems)(*ins)
    return _split_plan_refs(plans, [], res, [])


def _call_with_plans(body, plans, *, name, grid, in_specs, out_specs, out_shape, scratch_shapes, semantics, operands):
    plans = list(plans)
    if not plans:
        res = pl.pallas_call(body, name=name, grid=grid, in_specs=in_specs, out_specs=out_specs, out_shape=out_shape,
                             scratch_shapes=scratch_shapes, compiler_params=_cparams(semantics))(*operands)
        return list(res), []
    n_in, n_out, n_scr = len(in_specs), len(out_specs), len(scratch_shapes)
    p_ins = [a for p in plans for a in p.ins]
    p_outs = [o for p in plans for o in p.out_shapes]
    p_sems = [s for p in plans for s in p.sem_shapes]
    nsteps = math.prod(grid)
    any_spec = pl.BlockSpec(memory_space=pl.ANY)

    def wrapped(*refs):
        bounds = [n_in, len(p_ins), n_out, len(p_outs), n_scr]
        parts, pos = [], 0
        for b in bounds:
            parts.append(refs[pos:pos + b])
            pos += b
        ins, p_in, outs, p_out, scr = parts
        step = pl.program_id(0)
        for ax in range(1, len(grid)):
            step = step * grid[ax] + pl.program_id(ax)
        riders = _split_plan_refs(plans, p_in, p_out, refs[pos:])

        @pl.when(step == 0)
        def _():
            for p, r in zip(plans, riders):
                p.start(*r)

        body(*ins, *outs, *scr)

        @pl.when(step == nsteps - 1)
        def _():
            for p, r in zip(plans, riders):
                p.finish(*r)

    res = pl.pallas_call(
        wrapped, name=name, grid=grid, in_specs=list(in_specs) + [any_spec] * len(p_ins),
        out_specs=list(out_specs) + [any_spec] * len(p_outs), out_shape=list(out_shape) + p_outs,
        scratch_shapes=list(scratch_shapes) + p_sems, compiler_params=_cparams(("arbitrary",) * len(grid)),
    )(*operands, *p_ins)
    return list(res[:n_out]), [r[1] for r in _split_plan_refs(plans, [], res[n_out:], [])]


def _gather_plan(shards):
    n = len(shards)

    def make(ins, outs, sems):
        send, recv, lsem = sems
        x, y, c, chips = _place()
        me, sib = (x, y, c), (x, y, 1 - c)

        def copy(w, k, block, to, src=None):
            dst = outs[w].at[_dev(*block)]
            return pltpu.make_async_remote_copy(
                src_ref=dst if src is None else src, dst_ref=dst,
                send_sem=send.at[w * 7 + k], recv_sem=recv.at[w * 7 + k], device_id=to, device_id_type=MESH)

        mine = [pltpu.make_async_copy(ins[w], outs[w].at[_dev(*me)], lsem.at[w]) for w in range(n)]
        first = []
        for w in range(n):
            first.append(copy(w, 0, me, sib, src=ins[w]))
            first += [copy(w, 1 + j, me, (*chip, c), src=ins[w]) for j, chip in enumerate(chips)]
        return copy, mine, first, me, sib, c, chips

    def start(ins, outs, sems):
        _, mine, first, *_ = make(ins, outs, sems)
        for cp in mine + first:
            cp.start()

    def finish(ins, outs, sems):
        copy, mine, first, me, sib, c, chips = make(ins, outs, sems)
        passed = []
        for j, chip in enumerate(chips):
            for w in range(n):
                copy(w, 1 + j, (*chip, c), me).wait_recv()
                fwd = copy(w, 4 + j, (*chip, c), sib)
                fwd.start()
                passed.append(fwd)
        for w in range(n):
            copy(w, 0, sib, me).wait_recv()
        for j, chip in enumerate(chips):
            for w in range(n):
                copy(w, 4 + j, (*chip, 1 - c), me).wait_recv()
        for cp in first + passed:
            cp.wait_send()
        for cp in mine:
            cp.wait()

    return _Plan(shards, [jax.ShapeDtypeStruct((N_DEV,) + s.shape, s.dtype) for s in shards],
                 [pltpu.SemaphoreType.DMA((7 * n,)), pltpu.SemaphoreType.DMA((7 * n,)), pltpu.SemaphoreType.DMA((n,))],
                 start, finish)


def _swap_plan(copies_of, n_copies, ins, out_shapes):
    def cps(in_refs, out_refs, sems):
        return copies_of(in_refs, out_refs, sems[0], sems[1])

    def start(in_refs, out_refs, sems):
        for cp in cps(in_refs, out_refs, sems):
            cp.start()

    def finish(in_refs, out_refs, sems):
        all_cps = cps(in_refs, out_refs, sems)
        for cp in all_cps:
            cp.wait_recv()
        for cp in all_cps:
            cp.wait_send()

    return _Plan(ins, out_shapes, [pltpu.SemaphoreType.DMA((n_copies,)), pltpu.SemaphoreType.DMA((n_copies,))],
                 start, finish)


def _sibling_plan(grads):
    n = len(grads)

    def copies(ins, outs, send, recv):
        x, y, c, chips = _place()
        owners = [(x, y)] + chips
        return [pltpu.make_async_remote_copy(
            src_ref=ins[w].at[_dev(*chip, 1 - c)], dst_ref=outs[w].at[k], send_sem=send.at[w * 4 + k],
            recv_sem=recv.at[w * 4 + k], device_id=(x, y, 1 - c), device_id_type=MESH)
            for w in range(n) for k, chip in enumerate(owners)]

    return _swap_plan(copies, 4 * n, grads, [jax.ShapeDtypeStruct((4,) + g.shape[1:], g.dtype) for g in grads])


def _chip_plan(parts, js=(0, 1, 2)):
    n, nj = len(parts), len(js)

    def copies(ins, outs, send, recv):
        x, y, c, chips = _place()
        return [pltpu.make_async_remote_copy(
            src_ref=ins[w].at[j], dst_ref=outs[w * nj + k], send_sem=send.at[w * nj + k],
            recv_sem=recv.at[w * nj + k], device_id=(*chips[j], c), device_id_type=MESH)
            for w in range(n) for k, j in enumerate(js)]

    return _swap_plan(copies, n * nj, parts,
                      [jax.ShapeDtypeStruct(p.shape[1:], p.dtype) for p in parts for _ in js])


UPDATE_TILE_BYTES = 768 * 1024


def _row_tile(r, c):
    best = 8
    for t in range(8, r + 1, 8):
        if r % t == 0 and t * c * 4 <= UPDATE_TILE_BYTES:
            best = t
    return best


def _chip_partial(g, sib, ids, *, name):
    _, r, c = g.shape
    tr = _row_tile(r, c)

    def body(ids_ref, g_ref, s_ref, o_ref):
        o_ref[...] = (g_ref[...] + s_ref[...]).astype(BF16)

    return pl.pallas_call(
        body, name=name,
        grid_spec=pltpu.PrefetchScalarGridSpec(
            num_scalar_prefetch=1, grid=(3, r // tr),
            in_specs=[pl.BlockSpec((None, tr, c), lambda j, i, ids_ref: (ids_ref[j], i, 0)),
                      pl.BlockSpec((None, tr, c), lambda j, i, ids_ref: (j + 1, i, 0))],
            out_specs=pl.BlockSpec((None, tr, c), lambda j, i, ids_ref: (j, i, 0))),
        out_shape=jax.ShapeDtypeStruct((3, r, c), BF16),
        compiler_params=_cparams(("parallel", "parallel")),
    )(ids, g, sib)


def _adamw_math(w, g, m, v):
    m = ADAM_B1 * m + (1.0 - ADAM_B1) * g
    v = ADAM_B2 * v + (1.0 - ADAM_B2) * (g * g)
    m_hat = m / (1.0 - ADAM_B1 ** ADAM_STEP)
    v_hat = v / (1.0 - ADAM_B2 ** ADAM_STEP)
    delta = -ADAM_LR * (m_hat / (jnp.sqrt(v_hat) + ADAM_EPS) + ADAM_WD * w)
    return delta, m, v


def _shard_update(g, sib, rem, me, w, m, v, *, name):
    r, c = w.shape
    tr = _row_tile(r, c)

    def body(me_ref, g_ref, s_ref, r0_ref, r1_ref, r2_ref, w_ref, m_ref, v_ref, go_ref, d_ref, mo_ref, vo_ref):
        gt = g_ref[...] + s_ref[...]
        gt = gt + r0_ref[...].astype(F32)
        gt = gt + r1_ref[...].astype(F32)
        gt = gt + r2_ref[...].astype(F32)
        go_ref[...] = gt
        d, mn, vn = _adamw_math(w_ref[...], gt, m_ref[...], v_ref[...])
        d_ref[...] = d
        mo_ref[...] = mn
        vo_ref[...] = vn

    blk = lambda k: pl.BlockSpec((None, tr, c), lambda i, me_ref: (k, i, 0))
    plain = pl.BlockSpec((tr, c), lambda i, me_ref: (i, 0))
    return pl.pallas_call(
        body, name=name,
        grid_spec=pltpu.PrefetchScalarGridSpec(
            num_scalar_prefetch=1, grid=(r // tr,),
            in_specs=[pl.BlockSpec((None, tr, c), lambda i, me_ref: (me_ref[0], i, 0)), blk(0), plain, plain, plain,
                      plain, plain, plain],
            out_specs=[plain] * 4),
        out_shape=[jax.ShapeDtypeStruct((r, c), F32)] * 4,
        compiler_params=_cparams(("parallel",)),
    )(me, g, sib, *rem, w, m, v)


def _small_update(gathered, w, m, v, *, name):
    _, r, c = gathered.shape

    def body(g_ref, w_ref, m_ref, v_ref, go_ref, d_ref, mo_ref, vo_ref):
        gt = g_ref[0]
        for k in range(1, N_DEV):
            gt = gt + g_ref[k]
        go_ref[...] = gt
        d, mn, vn = _adamw_math(w_ref[...], gt, m_ref[...], v_ref[...])
        d_ref[...] = d
        mo_ref[...] = mn
        vo_ref[...] = vn

    return pl.pallas_call(
        body, name=name, out_shape=[jax.ShapeDtypeStruct((r, c), F32)] * 4,
        compiler_params=pltpu.CompilerParams(vmem_limit_bytes=VMEM_LIMIT),
    )(gathered, w, m, v)


SMALL_UNIT = 1024


def _pack(parts):
    flat = []
    for p in parts:
        f = p.reshape(-1).astype(F32)
        pad = (-f.shape[0]) % SMALL_UNIT
        flat.append(jnp.pad(f, (0, pad)))
    return jnp.concatenate(flat).reshape(-1, 128)


def _unpack(buf, shapes):
    flat = buf.reshape(-1)
    out, off = [], 0
    for s in shapes:
        nel = math.prod(s)
        out.append(flat[off:off + nel].reshape(s))
        off += nel + ((-nel) % SMALL_UNIT)
    return out


def kernel(x, p, ffn1_w_in, ffn1_w_out, ln1_g, ln1_b, mix_w_in, conv_w, conv_b, conv_w_out, ssm_lam_re, ssm_lam_im, ssm_log_step, ssm_b_re, ssm_b_im, ssm_c_re, ssm_c_im, ssm_d, ssm_w_glu, mix_w_out, ln2_g, ln2_b, ffn2_w_in, ffn2_w_out, ln3_g, ln3_b, ple_w_in, ple_w_gate, ln4_g, ln4_b, loss_target, m_ffn1_w_in, m_ffn1_w_out, m_ln1_g, m_ln1_b, m_mix_w_in, m_conv_w, m_conv_b, m_conv_w_out, m_ssm_lam_re, m_ssm_lam_im, m_ssm_log_step, m_ssm_b_re, m_ssm_b_im, m_ssm_c_re, m_ssm_c_im, m_ssm_d, m_ssm_w_glu, m_mix_w_out, m_ln2_g, m_ln2_b, m_ffn2_w_in, m_ffn2_w_out, m_ln3_g, m_ln3_b, m_ple_w_in, m_ple_w_gate, m_ln4_g, m_ln4_b, v_ffn1_w_in, v_ffn1_w_out, v_ln1_g, v_ln1_b, v_mix_w_in, v_conv_w, v_conv_b, v_conv_w_out, v_ssm_lam_re, v_ssm_lam_im, v_ssm_log_step, v_ssm_b_re, v_ssm_b_im, v_ssm_c_re, v_ssm_c_im, v_ssm_d, v_ssm_w_glu, v_mix_w_out, v_ln2_g, v_ln2_b, v_ffn2_w_in, v_ffn2_w_out, v_ln3_g, v_ln3_b, v_ple_w_in, v_ple_w_gate, v_ln4_g, v_ln4_b):
    args = dict(locals())
    big = ['ffn1_w_in', 'ffn1_w_out', 'mix_w_in', 'conv_w_out', 'ssm_w_glu', 'mix_w_out',
           'ffn2_w_in', 'ffn2_w_out', 'ple_w_in', 'ple_w_gate']
    small = ['ln1_g', 'ln1_b', 'conv_b', 'ssm_lam_re', 'ssm_lam_im', 'ssm_log_step', 'ssm_b_re', 'ssm_b_im',
             'ssm_c_re', 'ssm_c_im', 'ssm_d', 'ln2_g', 'ln2_b', 'ln3_g', 'ln3_b', 'ln4_g', 'ln4_b']
    order = ['ffn1_w_in', 'ffn1_w_out', 'ln1_g', 'ln1_b', 'mix_w_in', 'conv_w', 'conv_b', 'conv_w_out',
             'ssm_lam_re', 'ssm_lam_im', 'ssm_log_step', 'ssm_b_re', 'ssm_b_im', 'ssm_c_re', 'ssm_c_im', 'ssm_d',
             'ssm_w_glu', 'mix_w_out', 'ln2_g', 'ln2_b', 'ffn2_w_in', 'ffn2_w_out', 'ln3_g', 'ln3_b',
             'ple_w_in', 'ple_w_gate', 'ln4_g', 'ln4_b']

    t = x.shape[1]
    d = x.shape[2]
    xc_, yc_, cc_ = lax.axis_index("x"), lax.axis_index("y"), lax.axis_index("c")
    me = (4 * xc_ + 2 * yc_ + cc_).astype(jnp.int32)
    cw_cols = conv_w.shape[2]

    shard = {nm: args[nm][0].astype(BF16) for nm in big}
    cw_pad = jnp.zeros((8, 128), F32).at[0:3, 0:cw_cols].set(conv_w[0])
    wf = shard['ffn1_w_in'].shape[1]

    def gather(*names):
        return _gather_plan([shard[nm] for nm in names])

    ((_, (w1in, w1out_g, cw_g), _),) = _run_plans(
        [_gather_plan([shard['ffn1_w_in'], shard['ffn1_w_out'], cw_pad])], name="gather_ffn1")
    w1out = w1out_g.reshape(4, wf, d)
    cw_full = jnp.transpose(cw_g[:, 0:3, 0:cw_cols], (1, 0, 2)).reshape(3, N_DEV * cw_cols)
    cw8 = jnp.zeros((8, CONV_CH), F32).at[0:3, :].set(cw_full)

    s5_in = (ssm_lam_re[0], ssm_lam_im[0], ssm_log_step[0], ssm_b_re[0], ssm_b_im[0])
    (a_re, a_im, bb_re, bb_im), s5_vjp = jax.vjp(_s5_discretise, *s5_in)
    tab_f = _pow_table(a_re, a_im)
    tab_b = _pow_table(a_re, -a_im, descending=True)
    bmat_b = _compact(jnp.transpose(bb_re, (0, 2, 1)), jnp.transpose(bb_im, (0, 2, 1))).astype(BF16)
    cmat_tb = _compact(ssm_c_re[0], -ssm_c_im[0]).astype(BF16)
    dvec = ssm_d[0].reshape(1, SSM_W)

    xf = x[0]
    x_b = xf.astype(BF16)
    p_b = p[0, 0].astype(BF16)
    tgt = loss_target[0]
    tq = min(512, t)

    ffn_out = dict(ja='c', jb='c', nj=4, tm=tq, tn=d, tk=wf)
    a1, h1, ((wmix,),) = _ffn_in(x_b, w1in, name="ffn1_in", tm=tq, plans=[gather('mix_w_in')])
    f1, ((wco, wglu, wmo_g),) = _mm(a1, w1out, name="ffn1_out", **ffn_out,
                                    plans=[gather('conv_w_out', 'ssm_w_glu', 'mix_w_out')])
    wmo = wmo_g.reshape(d, d)
    x1, x1b = _ln_fwd(xf, [f1], ln1_g, ln1_b, name="ffn1_ln", fs=0.5)
    proj, ((w2in,),) = _mm(x1b, wmix, name="mix_in", jb='b', jo='b', o_flat=True, nj=8, tm=tq, tn=512, tk=d,
                           plans=[gather('ffn2_w_in')])
    ycin = _conv_fwd(proj, cw8, conv_b, name="conv_fwd")
    yconv = _mm(ycin, wco, name="conv_out", jb='b', jo='b', o_flat=True, nj=8, tm=tq, tn=128, tk=CONV_CH)
    s_f, s_b = _scan_fwd(proj, bmat_b, tab_f, name="scan_fwd")
    blk = dict(ja='b', jb='b', jo='b', nj=SCAN_NCB)
    ymm = _mm(s_b, cmat_tb, name="ssm_read", a_flat=True, o_flat=True, tb=True, tm=tq, tn=SCAN_UW, tk=2 * SCAN_CB, **blk)
    ys, sg, u_b = _s5_out(ymm, proj, dvec, name="ssm_out")
    glu, ((w2out_g,),) = _mm(sg, wglu, name="glu_in", jb='b', jo='b', o_flat=True, nj=8, tm=tq, tn=256, tk=SSM_W,
                             plans=[gather('ffn2_w_out')])
    w2out = w2out_g.reshape(4, wf, d)
    merged = _gate_fwd(yconv, glu, proj, name="gate_fwd")
    mix = _mm(merged, wmo, name="mix_out", tm=tq, tn=d, tk=d)
    x2, x2b = _ln_fwd(x1, [mix], ln2_g, ln2_b, name="mix_ln")
    a2, h2, ((wpin, wgate_g),) = _ffn_in(x2b, w2in, name="ffn2_in", tm=tq, plans=[gather('ple_w_in', 'ple_w_gate')])
    wgate = wgate_g.reshape(d, d)
    f2 = _mm(a2, w2out, name="ffn2_out", **ffn_out)
    x3, x3b = _ln_fwd(x2, [f2], ln3_g, ln3_b, name="ffn2_ln", fs=0.5)
    pe = _mm(p_b, wpin, name="ple_in", jb='b', jo='b', o_flat=True, nj=8, tm=tq, tn=128, tk=p_b.shape[1])
    gp = _mm(x3b, wgate, name="ple_gate", tm=tq, tn=d, tk=d)

    dr4, dpe_b, dgp_b, dg4, db4, loss_part = _ln_bwd(x3, [pe, gp], ln4_g, ln4_b, [], name="ple_ln_bwd",
                                                     ple=True, target=tgt)
    gb, sib, rem = {}, {}, {}
    ids = jnp.stack([_dev(1 - xc_, yc_, cc_), _dev(xc_, 1 - yc_, cc_), _dev(1 - xc_, 1 - yc_, cc_)]).astype(jnp.int32)

    def blocked(nm, g):
        return g.reshape((N_DEV,) + args[nm].shape[1:])

    def to_sibling(*names):
        return _sibling_plan([gb[nm] for nm in names])

    def chip_sums(names, sibs):
        for nm, s in zip(names, sibs):
            sib[nm] = s
        return [_chip_partial(gb[nm], sib[nm], ids, name=f"chip_sum_{nm}") for nm in names]

    def set_rem(names, per_j):
        for i, nm in enumerate(names):
            rem[nm] = [per_j[j][i] for j in range(3)]

    ffn_in_dg = dict(ja='c', jb='c', tb=True, nj=8, tm=tq, tn=d, tk=wf)
    ffn_in_wg = dict(jb='b', jo='b', ta=True, nj=8, tm=d, tn=wf, tk=tq)
    ffn_out_wg = dict(ja='b', jo='b', ta=True, nj=4, tm=wf, tn=d, tk=tq)

    gb['ple_w_in'] = _mm(p_b, dpe_b, name="ple_in_wg", jb='b', jo='b', b_flat=True, ta=True, nj=8,
                         tm=p_b.shape[1], tn=128, tk=tq)
    gb['ple_w_gate'] = blocked('ple_w_gate', _mm(x3b, dgp_b, name="ple_gate_wg", ta=True, tm=d, tn=d, tk=tq))
    g_ple = ['ple_w_in', 'ple_w_gate']
    dx3_g, (s_,) = _mm(dgp_b, wgate, name="ple_gate_dg", tb=True, tm=tq, tn=d, tk=d, plans=[to_sibling(*g_ple)])
    part = chip_sums(g_ple, s_)

    dr3, df2_b, dg3, db3 = _ln_bwd(x2, [f2], ln3_g, ln3_b, [(dr4, ALPHA), (dx3_g, 1.0)], name="ffn2_ln_bwd", fs=0.5)
    dh2, _ = _ffn_out_dg(df2_b, w2out, h2, name="ffn2_out_dg", tm=tq)
    g_, (r_,) = _mm(a2, df2_b, name="ffn2_out_wg", **ffn_out_wg, plans=[_chip_plan(part)])
    gb['ffn2_w_out'] = blocked('ffn2_w_out', g_)
    rem['ple_w_in'], rem['ple_w_gate'] = r_[0:3], r_[3:6]
    dx2_f, (s_,) = _mm(dh2, w2in, name="ffn2_in_dg", **ffn_in_dg, plans=[to_sibling('ffn2_w_out')])
    part = chip_sums(['ffn2_w_out'], s_)
    gb['ffn2_w_in'], (r_,) = _mm(x2b, dh2, name="ffn2_in_wg", **ffn_in_wg, plans=[_chip_plan(part)])
    rem['ffn2_w_out'] = r_

    dr2, dmix_b, dg2, db2 = _ln_bwd(x1, [mix], ln2_g, ln2_b, [(dr3, ALPHA), (dx2_f, 1.0)], name="mix_ln_bwd")
    dmerged, (s_,) = _mm(dmix_b, wmo, name="mix_out_dg", tb=True, tm=tq, tn=d, tk=d, plans=[to_sibling('ffn2_w_in')])
    part = chip_sums(['ffn2_w_in'], s_)
    gb['mix_w_out'] = blocked('mix_w_out', _mm(merged, dmix_b, name="mix_out_wg", ta=True, tm=d, tn=d, tk=tq))
    dyconv_b, dglu_b, dgate_b = _gate_bwd(dmerged, yconv, glu, proj, name="gate_bwd")
    gb['conv_w_out'], (r0,) = _mm(ycin, dyconv_b, name="conv_out_wg", jb='b', jo='b', b_flat=True, ta=True, nj=8,
                                  tm=CONV_CH, tn=128, tk=tq, plans=[_chip_plan(part, js=(0,))])
    dycin, (r1,) = _mm(dyconv_b, wco, name="conv_out_dg", ja='c', jb='c', a_flat=True, tb=True, nj=8,
                       tm=tq, tn=CONV_CH, tk=128, plans=[_chip_plan(part, js=(1,))])
    gb['ssm_w_glu'], (r2,) = _mm(sg, dglu_b, name="glu_in_wg", jb='b', jo='b', b_flat=True, ta=True, nj=8,
                                 tm=SSM_W, tn=256, tk=tq, plans=[_chip_plan(part, js=(2,))])
    set_rem(['ffn2_w_in'], [r0, r1, r2])
    g_mix = ['mix_w_out', 'conv_w_out', 'ssm_w_glu']
    dsg, (s_,) = _mm(dglu_b, wglu, name="glu_in_dg", ja='c', jb='c', a_flat=True, tb=True, nj=8,
                     tm=tq, tn=SSM_W, tk=256, plans=[to_sibling(*g_mix)])
    part = chip_sums(g_mix, s_)
    dys, dys_b, dd = _s5_bwd_in(dsg, ys, proj, name="ssm_out_bwd")
    h_b, da_acc = _scan_bwd(dys_b, cmat_tb, s_f, tab_b, name="scan_bwd")
    dumm, (r0,) = _mm(h_b, bmat_b, name="ssm_write_dg", a_flat=True, o_flat=True, tb=True, tm=tq, tn=SCAN_UW,
                      tk=2 * SCAN_CB, **blk, plans=[_chip_plan(part, js=(0,))])
    du_b = _s5_du(dumm, dys, dvec, name="ssm_du")
    g_bmat, (r1,) = _mm(u_b, h_b, name="ssm_write_wg", a_flat=True, b_flat=True, ta=True, tm=SCAN_UW,
                        tn=2 * SCAN_CB, tk=tq, **blk, plans=[_chip_plan(part, js=(1,))])
    g_cmat, (r2,) = _mm(dys_b, s_b, name="ssm_read_wg", a_flat=True, b_flat=True, ta=True, tm=SCAN_UW,
                        tn=2 * SCAN_CB, tk=tq, **blk, plans=[_chip_plan(part, js=(2,))])
    set_rem(g_mix, [r0, r1, r2])
    dcb_b, dcc_b, dch_b, dconv = _conv_bwd(dycin, proj, cw8, conv_b, name="conv_bwd")
    dproj = jnp.concatenate([dcb_b, dcc_b, dch_b, du_b, dgate_b], axis=1)
    gb['mix_w_in'] = _mm(x1b, dproj, name="mix_in_wg", jb='b', jo='b', b_flat=True, ta=True, nj=8,
                         tm=d, tn=512, tk=tq)
    dx1_m, (s_,) = _mm(dproj, wmix, name="mix_in_dg", ja='c', jb='c', a_flat=True, tb=True, nj=8,
                       tm=tq, tn=d, tk=512, plans=[to_sibling('mix_w_in')])
    part = chip_sums(['mix_w_in'], s_)

    dr1, df1_b, dg1, db1 = _ln_bwd(xf, [f1], ln1_g, ln1_b, [(dr2, ALPHA), (dx1_m, 1.0)], name="ffn1_ln_bwd", fs=0.5)
    dh1, (r0,) = _ffn_out_dg(df1_b, w1out, h1, name="ffn1_out_dg", tm=tq, plans=[_chip_plan(part, js=(0,))])
    g_, (r1,) = _mm(a1, df1_b, name="ffn1_out_wg", **ffn_out_wg, plans=[_chip_plan(part, js=(1,))])
    gb['ffn1_w_out'] = blocked('ffn1_w_out', g_)

    da_sum = jnp.sum(da_acc, axis=0)
    da_re, da_im = _unperm_cols(da_sum)
    gbb_re, gbb_im = [jnp.transpose(v, (0, 2, 1)) for v in _compact_extract(g_bmat, SSM_GROUP)]
    g_c_re, g_c_im_neg = _compact_extract(g_cmat, SSM_GROUP)
    g_c_im = -g_c_im_neg
    g_lam_re, g_lam_im, g_log_step, g_b_re, g_b_im = s5_vjp(
        (da_re.reshape(SSM_GROUPS, SSM_STATE), da_im.reshape(SSM_GROUPS, SSM_STATE), gbb_re, gbb_im))
    g_d = dd.reshape(SSM_GROUPS, SSM_GROUP)

    small_g = {'ln1_g': dg1, 'ln1_b': db1, 'conv_b': dconv[3:4], 'ssm_lam_re': g_lam_re, 'ssm_lam_im': g_lam_im,
               'ssm_log_step': g_log_step, 'ssm_b_re': g_b_re, 'ssm_b_im': g_b_im, 'ssm_c_re': g_c_re,
               'ssm_c_im': g_c_im, 'ssm_d': g_d, 'ln2_g': dg2, 'ln2_b': db2, 'ln3_g': dg3, 'ln3_b': db3,
               'ln4_g': dg4, 'ln4_b': db4}
    small_shapes = [args[nm].shape for nm in small] + [(3, CONV_CH), (1,)]
    g_pack = _pack([small_g[nm] for nm in small] + [dconv[0:3], loss_part[0:1, 0:1]])

    gb['ffn1_w_in'], (r2, s_, (g_all,)) = _mm(
        x_b, dh1, name="ffn1_in_wg", **ffn_in_wg,
        plans=[_chip_plan(part, js=(2,)), to_sibling('ffn1_w_out'), _gather_plan([g_pack])])
    set_rem(['mix_w_in'], [r0, r1, r2])
    part = chip_sums(['ffn1_w_out'], s_)
    dx_f, (r_, s_) = _mm(dh1, w1in, name="ffn1_in_dg", **ffn_in_dg, plans=[_chip_plan(part), to_sibling('ffn1_w_in')])
    rem['ffn1_w_out'] = r_
    part = chip_sums(['ffn1_w_in'], s_)
    grad_x = _axpy(dr1, ALPHA, dx_f, name="grad_x")
    ((_, rem['ffn1_w_in'], _),) = _run_plans([_chip_plan(part)], name="grad_chips_ffn1_in")

    def full_cw(a):
        return lax.dynamic_update_slice(jnp.zeros((3, CONV_CH), F32), a[0], (0, me * cw_cols))

    zero1 = jnp.zeros((1,), F32)
    w_pack = _pack([args[nm] for nm in small] + [full_cw(conv_w), zero1])
    m_pack = _pack([args['m_' + nm] for nm in small] + [full_cw(m_conv_w), zero1])
    v_pack = _pack([args['v_' + nm] for nm in small] + [full_cw(v_conv_w), zero1])
    sg_sum, sd, sm, sv = _small_update(g_all, w_pack, m_pack, v_pack, name="small_update")
    res = {}
    for key, buf in (('grad_', sg_sum), ('delta_', sd), ('new_m_', sm), ('new_v_', sv)):
        parts = _unpack(buf, small_shapes)
        for nm, val in zip(small, parts[:len(small)]):
            res[key + nm] = val
        res[key + 'conv_w'] = lax.dynamic_slice(parts[len(small)], (0, me * cw_cols), (3, cw_cols))[None]
        if key == 'grad_':
            loss = parts[-1][0]

    me1 = me.reshape(1)
    for nm in big:
        gt, dl, mn, vn = _shard_update(gb[nm], sib[nm], rem[nm], me1, args[nm][0], args['m_' + nm][0],
                                       args['v_' + nm][0], name=f"update_{nm}")
        res['grad_' + nm], res['delta_' + nm], res['new_m_' + nm], res['new_v_' + nm] = gt[None], dl[None], mn[None], vn[None]

    outs = [loss, grad_x[None]]
    for key in ('grad_', 'delta_', 'new_m_', 'new_v_'):
        outs += [res[key + nm] for nm in order]
    return tuple(outs)
```

```python
import functools
import math

import jax
import jax.numpy as jnp
from jax import lax
from jax.experimental import pallas as pl
from jax.experimental.pallas import tpu as pltpu

F32 = jnp.float32
BF16 = jnp.bfloat16
MESH = pl.DeviceIdType.MESH

N_DEV = 8
ALPHA = 2.0 ** 0.25
LN_EPS = 1e-5
CONV_CH = 512
SSM_W = 512
SSM_GROUPS = 32
SSM_GROUP = 16
SSM_STATE = 64
SSM_CH = SSM_GROUPS * SSM_STATE
SCAN_CB = 512
SCAN_NCB = SSM_CH // SCAN_CB
SCAN_GPB = SSM_GROUPS // SCAN_NCB
SCAN_UW = SCAN_GPB * SSM_GROUP
SCAN_TT = 256
ADAM_LR = 0.001
ADAM_B1 = 0.9
ADAM_B2 = 0.999
ADAM_EPS = 1e-08
ADAM_WD = 0.01
ADAM_STEP = 10
VMEM_LIMIT = 56 * 1024 * 1024


def _cparams(sem=None, **kw):
    return pltpu.CompilerParams(dimension_semantics=sem, vmem_limit_bytes=VMEM_LIMIT, **kw)


def _mm(a, b, *, name, ja=None, jb=None, jo=None, a_flat=False, b_flat=False, o_flat=False,
        ta=False, tb=False, tm, tn, tk, nj=1, out_dtype=F32, plans=()):
    def dims(arr, j, flat):
        if j is None:
            return arr.shape
        if flat:
            return (arr.shape[0], arr.shape[1] // nj)
        assert arr.shape[0] == nj, (name, arr.shape, nj)
        return arr.shape[1:]

    ar, ac = dims(a, ja, a_flat)
    br, bc = dims(b, jb, b_flat)
    m, k = (ac, ar) if ta else (ar, ac)
    k2, n = (bc, br) if tb else (br, bc)
    assert k == k2, (name, a.shape, b.shape)
    assert m % tm == 0 and n % tn == 0 and k % tk == 0, (name, m, n, k, tm, tn, tk)
    njb = nj if 'b' in (ja, jb) else 1
    njc = nj if 'c' in (ja, jb) else 1
    nk = k // tk
    j_inside = njc > 1 and nk == 1 and not ta
    n_in = njc if j_inside else 1
    nred = nk if j_inside else njc * nk
    grid = (njb, m // tm, n // tn, 1 if j_inside else njc, nk)

    def make_spec(j, flat, blk, rfn, cfn, cols_per_j):
        def jsel(g, c):
            return g if j == 'b' else c
        if j is None:
            return pl.BlockSpec(blk, lambda g, i, jn, c, kk: (rfn(i, jn, kk), cfn(i, jn, kk)))
        if j == 'c' and j_inside:
            if flat:
                return pl.BlockSpec((blk[0], nj * blk[1]), lambda g, i, jn, c, kk: (rfn(i, jn, kk), 0))
            return pl.BlockSpec((nj,) + blk, lambda g, i, jn, c, kk: (0, rfn(i, jn, kk), cfn(i, jn, kk)))
        if flat:
            nb = cols_per_j // blk[1]
            return pl.BlockSpec(blk, lambda g, i, jn, c, kk: (rfn(i, jn, kk), jsel(g, c) * nb + cfn(i, jn, kk)))
        return pl.BlockSpec((None,) + blk,
                            lambda g, i, jn, c, kk: (jsel(g, c), rfn(i, jn, kk), cfn(i, jn, kk)))

    if ta:
        a_spec = make_spec(ja, a_flat, (tk, tm), lambda i, jn, kk: kk, lambda i, jn, kk: i, ac)
    else:
        a_spec = make_spec(ja, a_flat, (tm, tk), lambda i, jn, kk: i, lambda i, jn, kk: kk, ac)
    if tb:
        b_spec = make_spec(jb, b_flat, (tn, tk), lambda i, jn, kk: jn, lambda i, jn, kk: kk, bc)
    else:
        b_spec = make_spec(jb, b_flat, (tk, tn), lambda i, jn, kk: kk, lambda i, jn, kk: jn, bc)
    o_spec = make_spec(jo, o_flat, (tm, tn), lambda i, jn, kk: i, lambda i, jn, kk: jn, n)
    if jo is None:
        out_shape = (m, n)
    elif o_flat:
        out_shape = (m, nj * n)
    else:
        out_shape = (nj, m, n)

    dn = (((0 if ta else 1,), (1 if tb else 0,)), ((), ()))

    def operand(ref, j, flat, jj, width):
        if not (j == 'c' and j_inside):
            return ref[...]
        return ref[:, jj * width:(jj + 1) * width] if flat else ref[jj]

    def body(a_ref, b_ref, o_ref, *scratch):
        p = None
        for jj in range(n_in):
            q = lax.dot_general(operand(a_ref, ja, a_flat, jj, tk), operand(b_ref, jb, b_flat, jj, tk if tb else tn),
                                dn, preferred_element_type=F32)
            p = q if p is None else p + q
        if nred == 1:
            o_ref[...] = p.astype(o_ref.dtype)
        else:
            acc = scratch[0]
            r = pl.program_id(3) * nk + pl.program_id(4)

            @pl.when(r == 0)
            def _():
                acc[...] = p

            @pl.when(r > 0)
            def _():
                acc[...] += p

            @pl.when(r == nred - 1)
            def _():
                o_ref[...] = acc[...].astype(o_ref.dtype)

    res = _call_with_plans(
        body, plans, name=name, grid=grid, in_specs=[a_spec, b_spec], out_specs=[o_spec],
        out_shape=[jax.ShapeDtypeStruct(out_shape, out_dtype)],
        scratch_shapes=[] if nred == 1 else [pltpu.VMEM((tm, tn), F32)],
        semantics=("parallel", "parallel", "parallel", "arbitrary", "arbitrary"), operands=(a, b))
    return (res[0][0], res[1]) if plans else res[0][0]


def _sigmoid(v):
    return 1.0 / (1.0 + jnp.exp(-v))


def _row_spec(tm, cols, colblk=0):
    return pl.BlockSpec((tm, cols), lambda i: (i, colblk))


def _vec_spec(cols):
    return pl.BlockSpec((1, cols), lambda i: (0, 0))


def _ffn_in(xb, win, *, name, tm, plans=()):
    t, d = xb.shape
    nj, _, w = win.shape
    half = nj // 2

    def body(x_ref, wg_ref, wu_ref, a_ref, gu_ref):
        xv = x_ref[...]
        g = jnp.dot(xv, wg_ref[...], preferred_element_type=F32)
        u = jnp.dot(xv, wu_ref[...], preferred_element_type=F32)
        a_ref[...] = (g * _sigmoid(g) * u).astype(BF16)
        gu_ref[0] = g.astype(BF16)
        gu_ref[1] = u.astype(BF16)

    (a, gu), riders = _call_with_plans(
        body, plans, name=name, grid=(half, t // tm),
        in_specs=[pl.BlockSpec((tm, d), lambda j, i: (i, 0)),
                  pl.BlockSpec((None, d, w), lambda j, i: (j, 0, 0)),
                  pl.BlockSpec((None, d, w), lambda j, i: (j + half, 0, 0))],
        out_specs=[pl.BlockSpec((None, tm, w), lambda j, i: (j, i, 0)),
                   pl.BlockSpec((2, None, tm, w), lambda j, i: (0, j, i, 0))],
        out_shape=[jax.ShapeDtypeStruct((half, t, w), BF16), jax.ShapeDtypeStruct((2, half, t, w), BF16)],
        scratch_shapes=[], semantics=("parallel", "parallel"), operands=(xb, win, win))
    return a, gu, riders


def _ffn_out_dg(dfb, wout, gu, *, name, tm, plans=()):
    t, d = dfb.shape
    half, w, _ = wout.shape
    dn = (((1,), (1,)), ((), ()))

    def body(df_ref, w_ref, gu_ref, dh_ref):
        da = lax.dot_general(df_ref[...], w_ref[...], dn, preferred_element_type=F32)
        g = gu_ref[0].astype(F32)
        u = gu_ref[1].astype(F32)
        sg = _sigmoid(g)
        dh_ref[0] = (da * u * (sg * (1.0 + g * (1.0 - sg)))).astype(BF16)
        dh_ref[1] = (da * (g * sg)).astype(BF16)

    (out,), riders = _call_with_plans(
        body, plans, name=name, grid=(half, t // tm),
        in_specs=[pl.BlockSpec((tm, d), lambda j, i: (i, 0)),
                  pl.BlockSpec((None, w, d), lambda j, i: (j, 0, 0)),
                  pl.BlockSpec((2, None, tm, w), lambda j, i: (0, j, i, 0))],
        out_specs=[pl.BlockSpec((2, None, tm, w), lambda j, i: (0, j, i, 0))],
        out_shape=[jax.ShapeDtypeStruct((2, half, t, w), BF16)],
        scratch_shapes=[], semantics=("parallel", "parallel"), operands=(dfb, wout, gu))
    return out.reshape(2 * half, t, w), riders


def _ln_stats(r):
    mu = jnp.mean(r, axis=-1, keepdims=True)
    xc = r - mu
    var = jnp.mean(xc * xc, axis=-1, keepdims=True)
    rstd = lax.rsqrt(var + LN_EPS)
    return xc * rstd, rstd


def _ln_fwd(xin, fparts, gamma, beta, *, name, fs=1.0, ple=False, tm=256):
    t, d = xin.shape
    nf = len(fparts)

    def body(*refs):
        x_ref = refs[0]
        f_refs = refs[1:1 + nf]
        g_ref, b_ref, y_ref, yb_ref = refs[1 + nf:]
        if ple:
            f = f_refs[0][...] * _sigmoid(f_refs[1][...])
        else:
            f = fs * f_refs[0][...]
        xh, _ = _ln_stats(ALPHA * x_ref[...] + f)
        y = xh * g_ref[...] + b_ref[...]
        y_ref[...] = y
        yb_ref[...] = y.astype(BF16)

    return pl.pallas_call(
        body, name=name, grid=(t // tm,),
        in_specs=[_row_spec(tm, d)] * (1 + nf) + [_vec_spec(d), _vec_spec(d)],
        out_specs=[_row_spec(tm, d), _row_spec(tm, d)],
        out_shape=[jax.ShapeDtypeStruct((t, d), F32), jax.ShapeDtypeStruct((t, d), BF16)],
        compiler_params=_cparams(("parallel",)),
    )(xin, *fparts, gamma, beta)


def _ln_bwd(xin, fparts, gamma, beta, grads, *, name, fs=1.0, ple=False, target=None, tm=256):
    t, d = xin.shape
    nf = len(fparts)
    ng = len(grads)
    coefs = [c for _, c in grads]
    use_t = target is not None
    n_fout = 2 if ple else 1

    def body(*refs):
        pos = 0
        x_ref = refs[pos]; pos += 1
        f_refs = refs[pos:pos + nf]; pos += nf
        g_ref, b_ref = refs[pos:pos + 2]; pos += 2
        gr_refs = refs[pos:pos + ng]; pos += ng
        if use_t:
            t_ref = refs[pos]; pos += 1
        dr_ref = refs[pos]; pos += 1
        fo_refs = refs[pos:pos + n_fout]; pos += n_fout
        dg_ref, db_ref = refs[pos:pos + 2]; pos += 2
        if use_t:
            loss_ref = refs[pos]; pos += 1
        i = pl.program_id(0)

        if ple:
            pe = f_refs[0][...]
            sg = _sigmoid(f_refs[1][...])
            f = pe * sg
        else:
            f = fs * f_refs[0][...]
        xh, rstd = _ln_stats(ALPHA * x_ref[...] + f)
        gam = g_ref[...]
        if use_t:
            diff = xh * gam + b_ref[...] - t_ref[...]
            dy = diff * (1.0 / d)
            lpart = 0.5 * jnp.sum(jnp.sum(diff * diff, axis=-1, keepdims=True), axis=0, keepdims=True) * (1.0 / d)
        else:
            dy = coefs[0] * gr_refs[0][...]
            for c, r in zip(coefs[1:], gr_refs[1:]):
                dy = dy + c * r[...]
        dxh = dy * gam
        m1 = jnp.mean(dxh, axis=-1, keepdims=True)
        m2 = jnp.mean(dxh * xh, axis=-1, keepdims=True)
        dr = rstd * (dxh - m1 - xh * m2)
        dr_ref[...] = dr
        if ple:
            fo_refs[0][...] = (dr * sg).astype(BF16)
            fo_refs[1][...] = (dr * pe * (sg * (1.0 - sg))).astype(BF16)
        else:
            fo_refs[0][...] = (fs * dr).astype(BF16)
        dgp = jnp.sum(dy * xh, axis=0, keepdims=True)
        dbp = jnp.sum(dy, axis=0, keepdims=True)

        @pl.when(i == 0)
        def _():
            dg_ref[...] = dgp
            db_ref[...] = dbp
            if use_t:
                loss_ref[...] = jnp.broadcast_to(lpart, loss_ref.shape)

        @pl.when(i > 0)
        def _():
            dg_ref[...] += dgp
            db_ref[...] += dbp
            if use_t:
                loss_ref[...] += jnp.broadcast_to(lpart, loss_ref.shape)

    ins = [xin, *fparts, gamma, beta, *[g for g, _ in grads]] + ([target] if use_t else [])
    in_specs = ([_row_spec(tm, d)] * (1 + nf) + [_vec_spec(d), _vec_spec(d)] + [_row_spec(tm, d)] * ng
                + ([_row_spec(tm, d)] if use_t else []))
    out_specs = [_row_spec(tm, d)] * (1 + n_fout) + [_vec_spec(d), _vec_spec(d)] + ([_vec_spec(128)] if use_t else [])
    out_shape = ([jax.ShapeDtypeStruct((t, d), F32)] + [jax.ShapeDtypeStruct((t, d), BF16)] * n_fout
                 + [jax.ShapeDtypeStruct((1, d), F32)] * 2 + ([jax.ShapeDtypeStruct((1, 128), F32)] if use_t else []))
    return pl.pallas_call(
        body, name=name, grid=(t // tm,), in_specs=in_specs, out_specs=out_specs, out_shape=out_shape,
        compiler_params=_cparams(("arbitrary",)),
    )(*ins)


def _axpy(a, ca, b, *, name, tm=256):
    t, d = a.shape

    def body(a_ref, b_ref, o_ref):
        o_ref[...] = ca * a_ref[...] + b_ref[...]

    return pl.pallas_call(
        body, name=name, grid=(t // tm,), in_specs=[_row_spec(tm, d)] * 2, out_specs=_row_spec(tm, d),
        out_shape=jax.ShapeDtypeStruct((t, d), F32), compiler_params=_cparams(("parallel",)),
    )(a, b)


def _conv_fwd(proj, cw, cb, *, name, tm=256):
    t = proj.shape[0]
    c = CONV_CH
    hb = tm // 8

    def body(b_ref, c_ref, h_ref, cp_ref, hp_ref, w_ref, bias_ref, o_ref, q_scr):
        i = pl.program_id(0)
        q = c_ref[...] * h_ref[...]
        halo = jnp.where(i > 0, cp_ref[...] * hp_ref[...], 0.0)
        q_scr[0:8, :] = halo
        q_scr[8:, :] = q
        z = (w_ref[2:3, :] * q + w_ref[1:2, :] * q_scr[pl.ds(7, tm), :] + w_ref[0:1, :] * q_scr[pl.ds(6, tm), :]
             + bias_ref[...])
        o_ref[...] = (b_ref[...] * z).astype(BF16)

    prev = lambda blk: pl.BlockSpec((8, c), lambda i: (jnp.maximum(i * hb - 1, 0), blk))
    return pl.pallas_call(
        body, name=name, grid=(t // tm,),
        in_specs=[_row_spec(tm, c, 0), _row_spec(tm, c, 1), _row_spec(tm, c, 2), prev(1), prev(2),
                  pl.BlockSpec((8, c), lambda i: (0, 0)), _vec_spec(c)],
        out_specs=_row_spec(tm, c),
        out_shape=jax.ShapeDtypeStruct((t, c), BF16),
        scratch_shapes=[pltpu.VMEM((tm + 8, c), F32)],
        compiler_params=_cparams(("parallel",)),
    )(proj, proj, proj, proj, proj, cw, cb)


def _conv_bwd(dyc, proj, cw, cb, *, name, tm=256):
    t = proj.shape[0]
    c = CONV_CH
    hb = tm // 8
    nblk = t // 8

    def body(d_ref, b_ref, c_ref, h_ref, cp_ref, hp_ref, dn_ref, bn_ref, w_ref, bias_ref,
             db_ref, dc_ref, dh_ref, dw_ref, q_scr, z_scr):
        i = pl.program_id(0)
        last = pl.num_programs(0) - 1
        cc = c_ref[...]
        ch = h_ref[...]
        q = cc * ch
        halo = jnp.where(i > 0, cp_ref[...] * hp_ref[...], 0.0)
        q_scr[0:8, :] = halo
        q_scr[8:, :] = q
        w0, w1, w2 = w_ref[0:1, :], w_ref[1:2, :], w_ref[2:3, :]
        qm1 = q_scr[pl.ds(7, tm), :]
        qm2 = q_scr[pl.ds(6, tm), :]
        z = w2 * q + w1 * qm1 + w0 * qm2 + bias_ref[...]
        d = d_ref[...]
        bb = b_ref[...]
        db_ref[...] = (d * z).astype(BF16)
        dz = d * bb
        z_scr[0:tm, :] = dz
        z_scr[tm:, :] = jnp.where(i < last, dn_ref[...] * bn_ref[...], 0.0)
        dq = w2 * dz + w1 * z_scr[pl.ds(1, tm), :] + w0 * z_scr[pl.ds(2, tm), :]
        dc_ref[...] = (dq * ch).astype(BF16)
        dh_ref[...] = (dq * cc).astype(BF16)
        row = lax.broadcasted_iota(jnp.int32, (8, c), 0)
        part = jnp.zeros((8, c), F32)
        for k, term in enumerate((dz * qm2, dz * qm1, dz * q, dz)):
            part = jnp.where(row == k, jnp.sum(term, axis=0, keepdims=True), part)

        @pl.when(i == 0)
        def _():
            dw_ref[...] = part

        @pl.when(i > 0)
        def _():
            dw_ref[...] += part

    prev = lambda blk: pl.BlockSpec((8, c), lambda i: (jnp.maximum(i * hb - 1, 0), blk))
    nxt_p = pl.BlockSpec((8, c), lambda i: (jnp.minimum((i + 1) * hb, nblk - 1), 0))
    nxt_d = pl.BlockSpec((8, c), lambda i: (jnp.minimum((i + 1) * hb, nblk - 1), 0))
    return pl.pallas_call(
        body, name=name, grid=(t // tm,),
        in_specs=[_row_spec(tm, c), _row_spec(tm, c, 0), _row_spec(tm, c, 1), _row_spec(tm, c, 2),
                  prev(1), prev(2), nxt_d, nxt_p, pl.BlockSpec((8, c), lambda i: (0, 0)), _vec_spec(c)],
        out_specs=[_row_spec(tm, c)] * 3 + [pl.BlockSpec((8, c), lambda i: (0, 0))],
        out_shape=[jax.ShapeDtypeStruct((t, c), BF16)] * 3 + [jax.ShapeDtypeStruct((8, c), F32)],
        scratch_shapes=[pltpu.VMEM((tm + 8, c), F32), pltpu.VMEM((tm + 8, c), F32)],
        compiler_params=_cparams(("arbitrary",)),
    )(dyc, proj, proj, proj, proj, proj, dyc, proj, cw, cb)


def _gate_fwd(yconv, glu, proj, *, name, tm=256):
    t, d = yconv.shape

    def body(yc_ref, ga_ref, gb_ref, gc_ref, gs_ref, o_ref):
        yssm = ga_ref[...] * _sigmoid(gb_ref[...])
        o_ref[...] = (_sigmoid(gc_ref[...]) * yc_ref[...] + _sigmoid(gs_ref[...]) * yssm).astype(BF16)

    return pl.pallas_call(
        body, name=name, grid=(t // tm,),
        in_specs=[_row_spec(tm, d), _row_spec(tm, d, 0), _row_spec(tm, d, 1), _row_spec(tm, d, 2), _row_spec(tm, d, 3)],
        out_specs=_row_spec(tm, d), out_shape=jax.ShapeDtypeStruct((t, d), BF16),
        compiler_params=_cparams(("parallel",)),
    )(yconv, glu, glu, proj, proj)


def _gate_bwd(dm, yconv, glu, proj, *, name, tm=256):
    t, d = yconv.shape

    def body(dm_ref, yc_ref, ga_ref, gb_ref, gc_ref, gs_ref, dyc_ref, dglu_ref, dgate_ref):
        dmv = dm_ref[...]
        sc = _sigmoid(gc_ref[...])
        ss = _sigmoid(gs_ref[...])
        sb = _sigmoid(gb_ref[...])
        ga = ga_ref[...]
        yssm = ga * sb
        dyc_ref[...] = (dmv * sc).astype(BF16)
        dgate_ref[:, 0:d] = (dmv * yc_ref[...] * (sc * (1.0 - sc))).astype(BF16)
        dys = dmv * ss
        dgate_ref[:, d:2 * d] = (dmv * yssm * (ss * (1.0 - ss))).astype(BF16)
        dglu_ref[:, 0:d] = (dys * sb).astype(BF16)
        dglu_ref[:, d:2 * d] = (dys * ga * (sb * (1.0 - sb))).astype(BF16)

    return pl.pallas_call(
        body, name=name, grid=(t // tm,),
        in_specs=[_row_spec(tm, d), _row_spec(tm, d), _row_spec(tm, d, 0), _row_spec(tm, d, 1),
                  _row_spec(tm, d, 2), _row_spec(tm, d, 3)],
        out_specs=[_row_spec(tm, d), _row_spec(tm, 2 * d), _row_spec(tm, 2 * d)],
        out_shape=[jax.ShapeDtypeStruct((t, d), BF16), jax.ShapeDtypeStruct((t, 2 * d), BF16),
                   jax.ShapeDtypeStruct((t, 2 * d), BF16)],
        compiler_params=_cparams(("parallel",)),
    )(dm, yconv, glu, glu, proj, proj)


_GELU_C = math.sqrt(2.0 / math.pi)


def _gelu(v):
    return 0.5 * v * (1.0 + jnp.tanh(_GELU_C * (v + 0.044715 * v * v * v)))


def _gelu_grad(v):
    th = jnp.tanh(_GELU_C * (v + 0.044715 * v * v * v))
    return 0.5 * (1.0 + th) + 0.5 * v * (1.0 - th * th) * (_GELU_C * (1.0 + 3.0 * 0.044715 * v * v))


def _cmul(ar, ai, br, bi):
    return ar * br - ai * bi, ar * bi + ai * br


def _scan_fwd(proj, bmat, tab, *, name, plans=()):
    t = proj.shape[0]
    tt, cbw = SCAN_TT, SCAN_CB
    w2 = 2 * cbw

    def body(u_ref, b_ref, tab_ref, s_ref, sb_ref, bu_scr, carry):
        ti = pl.program_id(1)

        @pl.when(ti == 0)
        def _():
            carry[...] = jnp.zeros_like(carry)

        bu_scr[...] = jnp.dot(u_ref[...].astype(BF16), b_ref[...], preferred_element_type=F32)
        row = lax.broadcasted_iota(jnp.int32, (8, cbw), 0)

        def blk(bi, c):
            cr, ci = c
            r0 = pl.multiple_of(bi * 8, 8)
            xr = bu_scr[pl.ds(r0, 8), 0:cbw]
            xi = bu_scr[pl.ds(r0, 8), cbw:w2]
            for k, sh in enumerate((1, 2, 4)):
                kr = tab_ref[k:k + 1, 0:cbw]
                ki = tab_ref[k:k + 1, cbw:w2]
                sr = jnp.where(row >= sh, pltpu.roll(xr, sh, 0), 0.0)
                si = jnp.where(row >= sh, pltpu.roll(xi, sh, 0), 0.0)
                pr, pi = _cmul(kr, ki, sr, si)
                xr = xr + pr
                xi = xi + pi
            pr, pi = _cmul(tab_ref[8:16, 0:cbw], tab_ref[8:16, cbw:w2], cr, ci)
            xr = xr + pr
            xi = xi + pi
            s_ref[pl.ds(r0, 8), 0:cbw] = xr
            s_ref[pl.ds(r0, 8), cbw:w2] = xi
            return (jnp.broadcast_to(xr[7:8, :], (8, cbw)), jnp.broadcast_to(xi[7:8, :], (8, cbw)))

        cr, ci = lax.fori_loop(0, tt // 8, blk, (carry[:, 0:cbw], carry[:, cbw:w2]))
        carry[:, 0:cbw] = cr
        carry[:, cbw:w2] = ci
        sb_ref[...] = s_ref[...].astype(BF16)

    (s, sb), riders = _call_with_plans(
        body, plans, name=name, grid=(SCAN_NCB, t // tt),
        in_specs=[pl.BlockSpec((tt, SCAN_UW), lambda cb, ti: (ti, 3 * SCAN_NCB + cb)),
                  pl.BlockSpec((None, SCAN_UW, w2), lambda cb, ti: (cb, 0, 0)),
                  pl.BlockSpec((16, w2), lambda cb, ti: (0, cb))],
        out_specs=[pl.BlockSpec((tt, w2), lambda cb, ti: (ti, cb))] * 2,
        out_shape=[jax.ShapeDtypeStruct((t, 2 * SSM_CH), F32), jax.ShapeDtypeStruct((t, 2 * SSM_CH), BF16)],
        scratch_shapes=[pltpu.VMEM((tt, w2), F32), pltpu.VMEM((8, w2), F32)],
        semantics=("parallel", "arbitrary"), operands=(proj, bmat, tab))
    return s, sb, riders


def _scan_bwd(dyb, cmat_t, s, tabb, *, name):
    t = s.shape[0]
    tt, cbw = SCAN_TT, SCAN_CB
    w2 = 2 * cbw
    nt = t // tt
    hb = tt // 8

    def body(dy_ref, c_ref, s_ref, sp_ref, tab_ref, h_ref, da_ref, g_scr, s_scr, carry):
        ti = pl.program_id(1)

        @pl.when(ti == 0)
        def _():
            carry[...] = jnp.zeros_like(carry)
            da_ref[...] = jnp.zeros_like(da_ref)

        g_scr[...] = jnp.dot(dy_ref[...], c_ref[...], preferred_element_type=F32)
        s_scr[0:8, :] = jnp.where(ti < nt - 1, sp_ref[...], 0.0)
        s_scr[8:, :] = s_ref[...]
        row = lax.broadcasted_iota(jnp.int32, (8, cbw), 0)

        def blk(k, c):
            cr, ci, ar, ai = c
            bi = hb - 1 - k
            r0 = pl.multiple_of(bi * 8, 8)
            xr = g_scr[pl.ds(r0, 8), 0:cbw]
            xi = g_scr[pl.ds(r0, 8), cbw:w2]
            for j, sh in enumerate((1, 2, 4)):
                kr = tab_ref[j:j + 1, 0:cbw]
                ki = tab_ref[j:j + 1, cbw:w2]
                sr = jnp.where(row < 8 - sh, pltpu.roll(xr, 8 - sh, 0), 0.0)
                si = jnp.where(row < 8 - sh, pltpu.roll(xi, 8 - sh, 0), 0.0)
                pr, pi = _cmul(kr, ki, sr, si)
                xr = xr + pr
                xi = xi + pi
            pr, pi = _cmul(tab_ref[8:16, 0:cbw], tab_ref[8:16, cbw:w2], cr, ci)
            xr = xr + pr
            xi = xi + pi
            h_ref[pl.ds(r0, 8), 0:cbw] = xr.astype(BF16)
            h_ref[pl.ds(r0, 8), cbw:w2] = xi.astype(BF16)
            pvr = s_scr[pl.ds(r0, 8), 0:cbw]
            pvi = s_scr[pl.ds(r0, 8), cbw:w2]
            cur_r = s_scr[pl.ds(r0 + 8, 8), 0:cbw]
            cur_i = s_scr[pl.ds(r0 + 8, 8), cbw:w2]
            spr = jnp.where(row == 0, jnp.broadcast_to(pvr[7:8, :], (8, cbw)), pltpu.roll(cur_r, 1, 0))
            spi = jnp.where(row == 0, jnp.broadcast_to(pvi[7:8, :], (8, cbw)), pltpu.roll(cur_i, 1, 0))
            ar = ar + spr * xr + spi * xi
            ai = ai + spr * xi - spi * xr
            return (jnp.broadcast_to(xr[0:1, :], (8, cbw)), jnp.broadcast_to(xi[0:1, :], (8, cbw)), ar, ai)

        z = jnp.zeros((8, cbw), F32)
        cr, ci, ar, ai = lax.fori_loop(0, hb, blk, (carry[:, 0:cbw], carry[:, cbw:w2], z, z))
        carry[:, 0:cbw] = cr
        carry[:, cbw:w2] = ci
        da_ref[:, 0:cbw] += ar
        da_ref[:, cbw:w2] += ai

    rt = lambda ti: nt - 1 - ti
    return pl.pallas_call(
        body, name=name, grid=(SCAN_NCB, nt),
        in_specs=[pl.BlockSpec((tt, SCAN_UW), lambda cb, ti: (rt(ti), cb)),
                  pl.BlockSpec((None, SCAN_UW, w2), lambda cb, ti: (cb, 0, 0)),
                  pl.BlockSpec((tt, w2), lambda cb, ti: (rt(ti), cb)),
                  pl.BlockSpec((8, w2), lambda cb, ti: (jnp.maximum(rt(ti) * hb - 1, 0), cb)),
                  pl.BlockSpec((16, w2), lambda cb, ti: (0, cb))],
        out_specs=[pl.BlockSpec((tt, w2), lambda cb, ti: (rt(ti), cb)),
                   pl.BlockSpec((8, w2), lambda cb, ti: (0, cb))],
        out_shape=[jax.ShapeDtypeStruct((t, 2 * SSM_CH), BF16), jax.ShapeDtypeStruct((8, 2 * SSM_CH), F32)],
        scratch_shapes=[pltpu.VMEM((tt, w2), F32), pltpu.VMEM((tt + 8, w2), F32), pltpu.VMEM((8, w2), F32)],
        compiler_params=_cparams(("parallel", "arbitrary")),
    )(dyb, cmat_t, s, s, tabb)


def _s5_out(ymm, proj, dvec, *, name, tm=256):
    t, w = ymm.shape

    def body(y_ref, u_ref, d_ref, yo_ref, sg_ref, ub_ref):
        u = u_ref[...]
        y = y_ref[...] + d_ref[...] * u
        yo_ref[...] = y
        sg_ref[...] = _gelu(y).astype(BF16)
        ub_ref[...] = u.astype(BF16)

    return pl.pallas_call(
        body, name=name, grid=(t // tm,),
        in_specs=[_row_spec(tm, w), _row_spec(tm, w, 3), _vec_spec(w)],
        out_specs=[_row_spec(tm, w)] * 3,
        out_shape=[jax.ShapeDtypeStruct((t, w), F32), jax.ShapeDtypeStruct((t, w), BF16), jax.ShapeDtypeStruct((t, w), BF16)],
        compiler_params=_cparams(("parallel",)),
    )(ymm, proj, dvec)


def _s5_bwd_in(dsg, y, proj, *, name, tm=256):
    t, w = y.shape

    def body(d_ref, y_ref, u_ref, dy_ref, dyb_ref, dd_ref):
        i = pl.program_id(0)
        dy = d_ref[...] * _gelu_grad(y_ref[...])
        dy_ref[...] = dy
        dyb_ref[...] = dy.astype(BF16)
        part = jnp.sum(dy * u_ref[...], axis=0, keepdims=True)

        @pl.when(i == 0)
        def _():
            dd_ref[...] = part

        @pl.when(i > 0)
        def _():
            dd_ref[...] += part

    return pl.pallas_call(
        body, name=name, grid=(t // tm,),
        in_specs=[_row_spec(tm, w), _row_spec(tm, w), _row_spec(tm, w, 3)],
        out_specs=[_row_spec(tm, w), _row_spec(tm, w), _vec_spec(w)],
        out_shape=[jax.ShapeDtypeStruct((t, w), F32), jax.ShapeDtypeStruct((t, w), BF16), jax.ShapeDtypeStruct((1, w), F32)],
        compiler_params=_cparams(("arbitrary",)),
    )(dsg, y, proj)


def _s5_du(dumm, dy, dvec, *, name, tm=256):
    t, w = dy.shape

    def body(a_ref, dy_ref, d_ref, o_ref):
        o_ref[...] = (a_ref[...] + d_ref[...] * dy_ref[...]).astype(BF16)

    return pl.pallas_call(
        body, name=name, grid=(t // tm,), in_specs=[_row_spec(tm, w), _row_spec(tm, w), _vec_spec(w)],
        out_specs=_row_spec(tm, w), out_shape=jax.ShapeDtypeStruct((t, w), BF16),
        compiler_params=_cparams(("parallel",)),
    )(dumm, dy, dvec)


def _s5_discretise(lam_re, lam_im, log_step, b_re, b_im):
    lam = lax.complex(lam_re, lam_im)
    dt = jnp.exp(log_step)[:, None]
    a = jnp.exp(lam * dt)
    bbar = ((a - 1.0) / lam)[..., None] * lax.complex(b_re, b_im)
    return jnp.real(a), jnp.imag(a), jnp.real(bbar), jnp.imag(bbar)


def _perm_cols(re, im):
    lead = re.shape[:-1]
    r = re.reshape(lead + (SCAN_NCB, 1, SCAN_CB))
    i = im.reshape(lead + (SCAN_NCB, 1, SCAN_CB))
    return jnp.concatenate([r, i], axis=-2).reshape(lead + (2 * SSM_CH,))


def _unperm_cols(x):
    lead = x.shape[:-1]
    y = x.reshape(lead + (SCAN_NCB, 2, SCAN_CB))
    return y[..., 0, :].reshape(lead + (SSM_CH,)), y[..., 1, :].reshape(lead + (SSM_CH,))


def _compact(re, im):
    _, r, c = re.shape
    eye = jnp.eye(SCAN_GPB, dtype=re.dtype)

    def half(x):
        x = x.reshape(SCAN_NCB, SCAN_GPB, r, c)
        return (eye[None, :, None, :, None] * x[:, :, :, None, :]).reshape(SCAN_NCB, SCAN_GPB * r, SCAN_GPB * c)

    return jnp.concatenate([half(re), half(im)], axis=-1)


def _compact_extract(x, r):
    c = SSM_STATE
    eye = jnp.eye(SCAN_GPB, dtype=x.dtype)
    y = x.reshape(SCAN_NCB, SCAN_GPB, r, 2, SCAN_GPB, c)
    dg = jnp.sum(y * eye[None, :, None, None, :, None], axis=4).reshape(SSM_GROUPS, r, 2, c)
    return dg[:, :, 0, :], dg[:, :, 1, :]


def _pow_table(ar, ai, descending=False):
    ar = ar.reshape(1, SSM_CH)
    ai = ai.reshape(1, SSM_CH)
    pw = [(ar, ai)]
    for _ in range(7):
        pw.append(_cmul(pw[-1][0], pw[-1][1], ar, ai))
    zero = (jnp.zeros_like(ar), jnp.zeros_like(ar))
    rows = [pw[0], pw[1], pw[3]] + [zero] * 5 + (pw[::-1] if descending else pw)
    re = jnp.concatenate([r for r, _ in rows], axis=0)
    im = jnp.concatenate([i for _, i in rows], axis=0)
    return _perm_cols(re, im)


def _place():
    x, y, c = lax.axis_index("x"), lax.axis_index("y"), lax.axis_index("c")
    chips = [(1 - x, y), (x, 1 - y), (1 - x, 1 - y)]
    return x, y, c, chips


def _dev(px, py, pc):
    return 4 * px + 2 * py + pc


class _Plan:
    def __init__(self, ins, out_shapes, sem_shapes, start, finish):
        self.ins, self.out_shapes, self.sem_shapes = list(ins), list(out_shapes), list(sem_shapes)
        self.start, self.finish = start, finish


def _split_plan_refs(plans, in_refs, out_refs, sem_refs):
    res, i, o, s = [], 0, 0, 0
    for p in plans:
        ni, no, ns = len(p.ins), len(p.out_shapes), len(p.sem_shapes)
        res.append((in_refs[i:i + ni], out_refs[o:o + no], sem_refs[s:s + ns]))
        i, o, s = i + ni, o + no, s + ns
    return res


def _run_plans(plans, *, name):
    ins = [a for p in plans for a in p.ins]
    outs = [o for p in plans for o in p.out_shapes]
    sems = [s for p in plans for s in p.sem_shapes]
    any_spec = pl.BlockSpec(memory_space=pl.ANY)

    def body(*refs):
        parts = _split_plan_refs(plans, refs[:len(ins)], refs[len(ins):len(ins) + len(outs)], refs[len(ins) + len(outs):])
        for p, r in zip(plans, parts):
            p.start(*r)
        for p, r in zip(plans, parts):
            p.finish(*r)

    res = pl.pallas_call(body, name=name, in_specs=[any_spec] * len(ins), out_specs=[any_spec] * len(outs),
                         out_shape=outs, scratch_shapes=sems)(*ins)
    return _split_plan_refs(plans, [], res, [])


def _call_with_plans(body, plans, *, name, grid, in_specs, out_specs, out_shape, scratch_shapes, semantics, operands):
    plans = list(plans)
    if not plans:
        res = pl.pallas_call(body, name=name, grid=grid, in_specs=in_specs, out_specs=out_specs, out_shape=out_shape,
                             scratch_shapes=scratch_shapes, compiler_params=_cparams(semantics))(*operands)
        return list(res), []
    n_in, n_out, n_scr = len(in_specs), len(out_specs), len(scratch_shapes)
    p_ins = [a for p in plans for a in p.ins]
    p_outs = [o for p in plans for o in p.out_shapes]
    p_sems = [s for p in plans for s in p.sem_shapes]
    nsteps = math.prod(grid)
    any_spec = pl.BlockSpec(memory_space=pl.ANY)

    def wrapped(*refs):
        bounds = [n_in, len(p_ins), n_out, len(p_outs), n_scr]
        parts, pos = [], 0
        for b in bounds:
            parts.append(refs[pos:pos + b])
            pos += b
        ins, p_in, outs, p_out, scr = parts
        step = pl.program_id(0)
        for ax in range(1, len(grid)):
            step = step * grid[ax] + pl.program_id(ax)
        riders = _split_plan_refs(plans, p_in, p_out, refs[pos:])

        @pl.when(step == 0)
        def _():
            for p, r in zip(plans, riders):
                p.start(*r)

        body(*ins, *outs, *scr)

        @pl.when(step == nsteps - 1)
        def _():
            for p, r in zip(plans, riders):
                p.finish(*r)

    res = pl.pallas_call(
        wrapped, name=name, grid=grid, in_specs=list(in_specs) + [any_spec] * len(p_ins),
        out_specs=list(out_specs) + [any_spec] * len(p_outs), out_shape=list(out_shape) + p_outs,
        scratch_shapes=list(scratch_shapes) + p_sems, compiler_params=_cparams(("arbitrary",) * len(grid)),
    )(*operands, *p_ins)
    return list(res[:n_out]), [r[1] for r in _split_plan_refs(plans, [], res[n_out:], [])]


def _gather_plan(shards):
    n = len(shards)

    def make(ins, outs, sems):
        send, recv, lsem = sems
        x, y, c, chips = _place()
        me, sib = (x, y, c), (x, y, 1 - c)

        def copy(w, k, block, to, src=None):
            dst = outs[w].at[_dev(*block)]
            return pltpu.make_async_remote_copy(
                src_ref=dst if src is None else src, dst_ref=dst,
                send_sem=send.at[w * 7 + k], recv_sem=recv.at[w * 7 + k], device_id=to, device_id_type=MESH)

        mine = [pltpu.make_async_copy(ins[w], outs[w].at[_dev(*me)], lsem.at[w]) for w in range(n)]
        first = []
        for w in range(n):
            first.append(copy(w, 0, me, sib, src=ins[w]))
            first += [copy(w, 1 + j, me, (*chip, c), src=ins[w]) for j, chip in enumerate(chips)]
        return copy, mine, first, me, sib, c, chips

    def start(ins, outs, sems):
        _, mine, first, *_ = make(ins, outs, sems)
        for cp in mine + first:
            cp.start()

    def finish(ins, outs, sems):
        copy, mine, first, me, sib, c, chips = make(ins, outs, sems)
        passed = []
        for j, chip in enumerate(chips):
            for w in range(n):
                copy(w, 1 + j, (*chip, c), me).wait_recv()
                fwd = copy(w, 4 + j, (*chip, c), sib)
                fwd.start()
                passed.append(fwd)
        for w in range(n):
            copy(w, 0, sib, me).wait_recv()
        for j, chip in enumerate(chips):
            for w in range(n):
                copy(w, 4 + j, (*chip, 1 - c), me).wait_recv()
        for cp in first + passed:
            cp.wait_send()
        for cp in mine:
            cp.wait()

    return _Plan(shards, [jax.ShapeDtypeStruct((N_DEV,) + s.shape, s.dtype) for s in shards],
                 [pltpu.SemaphoreType.DMA((7 * n,)), pltpu.SemaphoreType.DMA((7 * n,)), pltpu.SemaphoreType.DMA((n,))],
                 start, finish)


def _swap_plan(copies_of, n_copies, ins, out_shapes):
    def cps(in_refs, out_refs, sems):
        return copies_of(in_refs, out_refs, sems[0], sems[1])

    def start(in_refs, out_refs, sems):
        for cp in cps(in_refs, out_refs, sems):
            cp.start()

    def finish(in_refs, out_refs, sems):
        all_cps = cps(in_refs, out_refs, sems)
        for cp in all_cps:
            cp.wait_recv()
        for cp in all_cps:
            cp.wait_send()

    return _Plan(ins, out_shapes, [pltpu.SemaphoreType.DMA((n_copies,)), pltpu.SemaphoreType.DMA((n_copies,))],
                 start, finish)


def _sibling_plan(grads):
    n = len(grads)

    def copies(ins, outs, send, recv):
        x, y, c, chips = _place()
        owners = [(x, y)] + chips
        return [pltpu.make_async_remote_copy(
            src_ref=ins[w].at[_dev(*chip, 1 - c)], dst_ref=outs[w].at[k], send_sem=send.at[w * 4 + k],
            recv_sem=recv.at[w * 4 + k], device_id=(x, y, 1 - c), device_id_type=MESH)
            for w in range(n) for k, chip in enumerate(owners)]

    return _swap_plan(copies, 4 * n, grads, [jax.ShapeDtypeStruct((4,) + g.shape[1:], g.dtype) for g in grads])


def _chip_plan(parts, js=(0, 1, 2)):
    n, nj = len(parts), len(js)

    def copies(ins, outs, send, recv):
        x, y, c, chips = _place()
        return [pltpu.make_async_remote_copy(
            src_ref=ins[w].at[j], dst_ref=outs[w * nj + k], send_sem=send.at[w * nj + k],
            recv_sem=recv.at[w * nj + k], device_id=(*chips[j], c), device_id_type=MESH)
            for w in range(n) for k, j in enumerate(js)]

    return _swap_plan(copies, n * nj, parts,
                      [jax.ShapeDtypeStruct(p.shape[1:], p.dtype) for p in parts for _ in js])


UPDATE_TILE_BYTES = 768 * 1024


def _row_tile(r, c):
    best = 8
    for t in range(8, r + 1, 8):
        if r % t == 0 and t * c * 4 <= UPDATE_TILE_BYTES:
            best = t
    return best


def _chip_partial(g, sib, ids, *, name):
    _, r, c = g.shape
    tr = _row_tile(r, c)

    def body(ids_ref, g_ref, s_ref, o_ref):
        o_ref[...] = (g_ref[...] + s_ref[...]).astype(BF16)

    return pl.pallas_call(
        body, name=name,
        grid_spec=pltpu.PrefetchScalarGridSpec(
            num_scalar_prefetch=1, grid=(3, r // tr),
            in_specs=[pl.BlockSpec((None, tr, c), lambda j, i, ids_ref: (ids_ref[j], i, 0)),
                      pl.BlockSpec((None, tr, c), lambda j, i, ids_ref: (j + 1, i, 0))],
            out_specs=pl.BlockSpec((None, tr, c), lambda j, i, ids_ref: (j, i, 0))),
        out_shape=jax.ShapeDtypeStruct((3, r, c), BF16),
        compiler_params=_cparams(("parallel", "parallel")),
    )(ids, g, sib)


def _adamw_math(w, g, m, v):
    m = ADAM_B1 * m + (1.0 - ADAM_B1) * g
    v = ADAM_B2 * v + (1.0 - ADAM_B2) * (g * g)
    m_hat = m / (1.0 - ADAM_B1 ** ADAM_STEP)
    v_hat = v / (1.0 - ADAM_B2 ** ADAM_STEP)
    delta = -ADAM_LR * (m_hat / (jnp.sqrt(v_hat) + ADAM_EPS) + ADAM_WD * w)
    return delta, m, v


def _shard_update(g, sib, rem, me, w, m, v, *, name):
    r, c = w.shape
    tr = _row_tile(r, c)

    def body(me_ref, g_ref, s_ref, r0_ref, r1_ref, r2_ref, w_ref, m_ref, v_ref, go_ref, d_ref, mo_ref, vo_ref):
        gt = g_ref[...] + s_ref[...]
        gt = gt + r0_ref[...].astype(F32)
        gt = gt + r1_ref[...].astype(F32)
        gt = gt + r2_ref[...].astype(F32)
        go_ref[...] = gt
        d, mn, vn = _adamw_math(w_ref[...], gt, m_ref[...], v_ref[...])
        d_ref[...] = d
        mo_ref[...] = mn
        vo_ref[...] = vn

    blk = lambda k: pl.BlockSpec((None, tr, c), lambda i, me_ref: (k, i, 0))
    plain = pl.BlockSpec((tr, c), lambda i, me_ref: (i, 0))
    return pl.pallas_call(
        body, name=name,
        grid_spec=pltpu.PrefetchScalarGridSpec(
            num_scalar_prefetch=1, grid=(r // tr,),
            in_specs=[pl.BlockSpec((None, tr, c), lambda i, me_ref: (me_ref[0], i, 0)), blk(0), plain, plain, plain,
                      plain, plain, plain],
            out_specs=[plain] * 4),
        out_shape=[jax.ShapeDtypeStruct((r, c), F32)] * 4,
        compiler_params=_cparams(("parallel",)),
    )(me, g, sib, *rem, w, m, v)


def _small_update(gathered, w, m, v, *, name):
    _, r, c = gathered.shape

    def body(g_ref, w_ref, m_ref, v_ref, go_ref, d_ref, mo_ref, vo_ref):
        gt = g_ref[0]
        for k in range(1, N_DEV):
            gt = gt + g_ref[k]
        go_ref[...] = gt
        d, mn, vn = _adamw_math(w_ref[...], gt, m_ref[...], v_ref[...])
        d_ref[...] = d
        mo_ref[...] = mn
        vo_ref[...] = vn

    return pl.pallas_call(
        body, name=name, out_shape=[jax.ShapeDtypeStruct((r, c), F32)] * 4,
        compiler_params=pltpu.CompilerParams(vmem_limit_bytes=VMEM_LIMIT),
    )(gathered, w, m, v)


SMALL_UNIT = 1024


def _pack(parts):
    flat = []
    for p in parts:
        f = p.reshape(-1).astype(F32)
        pad = (-f.shape[0]) % SMALL_UNIT
        flat.append(jnp.pad(f, (0, pad)))
    return jnp.concatenate(flat).reshape(-1, 128)


def _unpack(buf, shapes):
    flat = buf.reshape(-1)
    out, off = [], 0
    for s in shapes:
        nel = math.prod(s)
        out.append(flat[off:off + nel].reshape(s))
        off += nel + ((-nel) % SMALL_UNIT)
    return out


def kernel(x, p, ffn1_w_in, ffn1_w_out, ln1_g, ln1_b, mix_w_in, conv_w, conv_b, conv_w_out, ssm_lam_re, ssm_lam_im, ssm_log_step, ssm_b_re, ssm_b_im, ssm_c_re, ssm_c_im, ssm_d, ssm_w_glu, mix_w_out, ln2_g, ln2_b, ffn2_w_in, ffn2_w_out, ln3_g, ln3_b, ple_w_in, ple_w_gate, ln4_g, ln4_b, loss_target, m_ffn1_w_in, m_ffn1_w_out, m_ln1_g, m_ln1_b, m_mix_w_in, m_conv_w, m_conv_b, m_conv_w_out, m_ssm_lam_re, m_ssm_lam_im, m_ssm_log_step, m_ssm_b_re, m_ssm_b_im, m_ssm_c_re, m_ssm_c_im, m_ssm_d, m_ssm_w_glu, m_mix_w_out, m_ln2_g, m_ln2_b, m_ffn2_w_in, m_ffn2_w_out, m_ln3_g, m_ln3_b, m_ple_w_in, m_ple_w_gate, m_ln4_g, m_ln4_b, v_ffn1_w_in, v_ffn1_w_out, v_ln1_g, v_ln1_b, v_mix_w_in, v_conv_w, v_conv_b, v_conv_w_out, v_ssm_lam_re, v_ssm_lam_im, v_ssm_log_step, v_ssm_b_re, v_ssm_b_im, v_ssm_c_re, v_ssm_c_im, v_ssm_d, v_ssm_w_glu, v_mix_w_out, v_ln2_g, v_ln2_b, v_ffn2_w_in, v_ffn2_w_out, v_ln3_g, v_ln3_b, v_ple_w_in, v_ple_w_gate, v_ln4_g, v_ln4_b):
    args = dict(locals())
    big = ['ffn1_w_in', 'ffn1_w_out', 'mix_w_in', 'conv_w_out', 'ssm_w_glu', 'mix_w_out',
           'ffn2_w_in', 'ffn2_w_out', 'ple_w_in', 'ple_w_gate']
    small = ['ln1_g', 'ln1_b', 'conv_b', 'ssm_lam_re', 'ssm_lam_im', 'ssm_log_step', 'ssm_b_re', 'ssm_b_im',
             'ssm_c_re', 'ssm_c_im', 'ssm_d', 'ln2_g', 'ln2_b', 'ln3_g', 'ln3_b', 'ln4_g', 'ln4_b']
    order = ['ffn1_w_in', 'ffn1_w_out', 'ln1_g', 'ln1_b', 'mix_w_in', 'conv_w', 'conv_b', 'conv_w_out',
             'ssm_lam_re', 'ssm_lam_im', 'ssm_log_step', 'ssm_b_re', 'ssm_b_im', 'ssm_c_re', 'ssm_c_im', 'ssm_d',
             'ssm_w_glu', 'mix_w_out', 'ln2_g', 'ln2_b', 'ffn2_w_in', 'ffn2_w_out', 'ln3_g', 'ln3_b',
             'ple_w_in', 'ple_w_gate', 'ln4_g', 'ln4_b']

    t = x.shape[1]
    d = x.shape[2]
    xc_, yc_, cc_ = lax.axis_index("x"), lax.axis_index("y"), lax.axis_index("c")
    me = (4 * xc_ + 2 * yc_ + cc_).astype(jnp.int32)
    cw_cols = conv_w.shape[2]

    shard = {nm: args[nm][0].astype(BF16) for nm in big}
    cw_pad = jnp.zeros((8, 128), F32).at[0:3, 0:cw_cols].set(conv_w[0])
    wf = shard['ffn1_w_in'].shape[1]

    def gather(*names):
        return _gather_plan([shard[nm] for nm in names])

    ((_, (w1in, cw_g), _),) = _run_plans([_gather_plan([shard['ffn1_w_in'], cw_pad])], name="gather_ffn1_in")
    cw_full = jnp.transpose(cw_g[:, 0:3, 0:cw_cols], (1, 0, 2)).reshape(3, N_DEV * cw_cols)
    cw8 = jnp.zeros((8, CONV_CH), F32).at[0:3, :].set(cw_full)

    s5_in = (ssm_lam_re[0], ssm_lam_im[0], ssm_log_step[0], ssm_b_re[0], ssm_b_im[0])
    (a_re, a_im, bb_re, bb_im), s5_vjp = jax.vjp(_s5_discretise, *s5_in)
    tab_f = _pow_table(a_re, a_im)
    tab_b = _pow_table(a_re, -a_im, descending=True)
    bmat_b = _compact(jnp.transpose(bb_re, (0, 2, 1)), jnp.transpose(bb_im, (0, 2, 1))).astype(BF16)
    cmat_tb = _compact(ssm_c_re[0], -ssm_c_im[0]).astype(BF16)
    dvec = ssm_d[0].reshape(1, SSM_W)

    xf = x[0]
    x_b = xf.astype(BF16)
    p_b = p[0, 0].astype(BF16)
    tgt = loss_target[0]
    tq = min(512, t)

    ffn_out = dict(ja='c', jb='c', nj=4, tm=tq, tn=d, tk=wf)
    a1, h1, ((w1out_g,),) = _ffn_in(x_b, w1in, name="ffn1_in", tm=tq, plans=[gather('ffn1_w_out')])
    w1out = w1out_g.reshape(4, wf, d)
    f1, ((wmix,),) = _mm(a1, w1out, name="ffn1_out", **ffn_out, plans=[gather('mix_w_in')])
    x1, x1b = _ln_fwd(xf, [f1], ln1_g, ln1_b, name="ffn1_ln", fs=0.5)
    proj, ((wco, wglu, wmo_g),) = _mm(x1b, wmix, name="mix_in", jb='b', jo='b', o_flat=True, nj=8, tm=tq, tn=512,
                                      tk=d, plans=[gather('conv_w_out', 'ssm_w_glu', 'mix_w_out')])
    wmo = wmo_g.reshape(d, d)
    ycin = _conv_fwd(proj, cw8, conv_b, name="conv_fwd")
    yconv = _mm(ycin, wco, name="conv_out", jb='b', jo='b', o_flat=True, nj=8, tm=tq, tn=128, tk=CONV_CH)
    s_f, s_b, ((w2in,),) = _scan_fwd(proj, bmat_b, tab_f, name="scan_fwd", plans=[gather('ffn2_w_in')])
    blk = dict(ja='b', jb='b', jo='b', nj=SCAN_NCB)
    ymm = _mm(s_b, cmat_tb, name="ssm_read", a_flat=True, o_flat=True, tb=True, tm=tq, tn=SCAN_UW, tk=2 * SCAN_CB, **blk)
    ys, sg, u_b = _s5_out(ymm, proj, dvec, name="ssm_out")
    glu, ((w2out_g,),) = _mm(sg, wglu, name="glu_in", jb='b', jo='b', o_flat=True, nj=8, tm=tq, tn=256, tk=SSM_W,
                             plans=[gather('ffn2_w_out')])
    w2out = w2out_g.reshape(4, wf, d)
    merged = _gate_fwd(yconv, glu, proj, name="gate_fwd")
    mix = _mm(merged, wmo, name="mix_out", tm=tq, tn=d, tk=d)
    x2, x2b = _ln_fwd(x1, [mix], ln2_g, ln2_b, name="mix_ln")
    a2, h2, ((wpin, wgate_g),) = _ffn_in(x2b, w2in, name="ffn2_in", tm=tq, plans=[gather('ple_w_in', 'ple_w_gate')])
    wgate = wgate_g.reshape(d, d)
    f2 = _mm(a2, w2out, name="ffn2_out", **ffn_out)
    x3, x3b = _ln_fwd(x2, [f2], ln3_g, ln3_b, name="ffn2_ln", fs=0.5)
    pe = _mm(p_b, wpin, name="ple_in", jb='b', jo='b', o_flat=True, nj=8, tm=tq, tn=128, tk=p_b.shape[1])
    gp = _mm(x3b, wgate, name="ple_gate", tm=tq, tn=d, tk=d)

    dr4, dpe_b, dgp_b, dg4, db4, loss_part = _ln_bwd(x3, [pe, gp], ln4_g, ln4_b, [], name="ple_ln_bwd",
                                                     ple=True, target=tgt)
    gb, sib, rem = {}, {}, {}
    ids = jnp.stack([_dev(1 - xc_, yc_, cc_), _dev(xc_, 1 - yc_, cc_), _dev(1 - xc_, 1 - yc_, cc_)]).astype(jnp.int32)

    def blocked(nm, g):
        return g.reshape((N_DEV,) + args[nm].shape[1:])

    def to_sibling(*names):
        return _sibling_plan([gb[nm] for nm in names])

    def chip_sums(names, sibs):
        for nm, s in zip(names, sibs):
            sib[nm] = s
        return [_chip_partial(gb[nm], sib[nm], ids, name=f"chip_sum_{nm}") for nm in names]

    def set_rem(names, per_j):
        for i, nm in enumerate(names):
            rem[nm] = [per_j[j][i] for j in range(3)]

    ffn_in_dg = dict(ja='c', jb='c', tb=True, nj=8, tm=tq, tn=d, tk=wf)
    ffn_in_wg = dict(jb='b', jo='b', ta=True, nj=8, tm=d, tn=wf, tk=t)
    ffn_out_wg = dict(ja='b', jo='b', ta=True, nj=4, tm=wf, tn=d, tk=t)

    gb['ple_w_in'] = _mm(p_b, dpe_b, name="ple_in_wg", jb='b', jo='b', b_flat=True, ta=True, nj=8,
                         tm=p_b.shape[1], tn=128, tk=t)
    gb['ple_w_gate'] = blocked('ple_w_gate', _mm(x3b, dgp_b, name="ple_gate_wg", ta=True, tm=d, tn=d, tk=t))
    g_ple = ['ple_w_in', 'ple_w_gate']
    dx3_g, (s_,) = _mm(dgp_b, wgate, name="ple_gate_dg", tb=True, tm=tq, tn=d, tk=d, plans=[to_sibling(*g_ple)])
    part = chip_sums(g_ple, s_)

    dr3, df2_b, dg3, db3 = _ln_bwd(x2, [f2], ln3_g, ln3_b, [(dr4, ALPHA), (dx3_g, 1.0)], name="ffn2_ln_bwd", fs=0.5)
    dh2, _ = _ffn_out_dg(df2_b, w2out, h2, name="ffn2_out_dg", tm=tq)
    g_, (r_,) = _mm(a2, df2_b, name="ffn2_out_wg", **ffn_out_wg, plans=[_chip_plan(part)])
    gb['ffn2_w_out'] = blocked('ffn2_w_out', g_)
    rem['ple_w_in'], rem['ple_w_gate'] = r_[0:3], r_[3:6]
    dx2_f, (s_,) = _mm(dh2, w2in, name="ffn2_in_dg", **ffn_in_dg, plans=[to_sibling('ffn2_w_out')])
    part = chip_sums(['ffn2_w_out'], s_)
    gb['ffn2_w_in'], (r_,) = _mm(x2b, dh2, name="ffn2_in_wg", **ffn_in_wg, plans=[_chip_plan(part)])
    rem['ffn2_w_out'] = r_

    dr2, dmix_b, dg2, db2 = _ln_bwd(x1, [mix], ln2_g, ln2_b, [(dr3, ALPHA), (dx2_f, 1.0)], name="mix_ln_bwd")
    dmerged, (s_,) = _mm(dmix_b, wmo, name="mix_out_dg", tb=True, tm=tq, tn=d, tk=d, plans=[to_sibling('ffn2_w_in')])
    part = chip_sums(['ffn2_w_in'], s_)
    gb['mix_w_out'] = blocked('mix_w_out', _mm(merged, dmix_b, name="mix_out_wg", ta=True, tm=d, tn=d, tk=t))
    dyconv_b, dglu_b, dgate_b = _gate_bwd(dmerged, yconv, glu, proj, name="gate_bwd")
    gb['conv_w_out'], (r0,) = _mm(ycin, dyconv_b, name="conv_out_wg", jb='b', jo='b', b_flat=True, ta=True, nj=8,
                                  tm=CONV_CH, tn=128, tk=t, plans=[_chip_plan(part, js=(0,))])
    dycin, (r1,) = _mm(dyconv_b, wco, name="conv_out_dg", ja='c', jb='c', a_flat=True, tb=True, nj=8,
                       tm=tq, tn=CONV_CH, tk=128, plans=[_chip_plan(part, js=(1,))])
    gb['ssm_w_glu'], (r2,) = _mm(sg, dglu_b, name="glu_in_wg", jb='b', jo='b', b_flat=True, ta=True, nj=8,
                                 tm=SSM_W, tn=256, tk=t, plans=[_chip_plan(part, js=(2,))])
    set_rem(['ffn2_w_in'], [r0, r1, r2])
    g_mix = ['mix_w_out', 'conv_w_out', 'ssm_w_glu']
    dsg, (s_,) = _mm(dglu_b, wglu, name="glu_in_dg", ja='c', jb='c', a_flat=True, tb=True, nj=8,
                     tm=tq, tn=SSM_W, tk=256, plans=[to_sibling(*g_mix)])
    part = chip_sums(g_mix, s_)
    dys, dys_b, dd = _s5_bwd_in(dsg, ys, proj, name="ssm_out_bwd")
    h_b, da_acc = _scan_bwd(dys_b, cmat_tb, s_f, tab_b, name="scan_bwd")
    dumm, (r0,) = _mm(h_b, bmat_b, name="ssm_write_dg", a_flat=True, o_flat=True, tb=True, tm=tq, tn=SCAN_UW,
                      tk=2 * SCAN_CB, **blk, plans=[_chip_plan(part, js=(0,))])
    du_b = _s5_du(dumm, dys, dvec, name="ssm_du")
    g_bmat, (r1,) = _mm(u_b, h_b, name="ssm_write_wg", a_flat=True, b_flat=True, ta=True, tm=SCAN_UW,
                        tn=2 * SCAN_CB, tk=t, **blk, plans=[_chip_plan(part, js=(1,))])
    g_cmat, (r2,) = _mm(dys_b, s_b, name="ssm_read_wg", a_flat=True, b_flat=True, ta=True, tm=SCAN_UW,
                        tn=2 * SCAN_CB, tk=t, **blk, plans=[_chip_plan(part, js=(2,))])
    set_rem(g_mix, [r0, r1, r2])
    dcb_b, dcc_b, dch_b, dconv = _conv_bwd(dycin, proj, cw8, conv_b, name="conv_bwd")
    dproj = jnp.concatenate([dcb_b, dcc_b, dch_b, du_b, dgate_b], axis=1)
    gb['mix_w_in'] = _mm(x1b, dproj, name="mix_in_wg", jb='b', jo='b', b_flat=True, ta=True, nj=8,
                         tm=d, tn=512, tk=t)
    dx1_m, (s_,) = _mm(dproj, wmix, name="mix_in_dg", ja='c', jb='c', a_flat=True, tb=True, nj=8,
                       tm=tq, tn=d, tk=512, plans=[to_sibling('mix_w_in')])
    part = chip_sums(['mix_w_in'], s_)

    dr1, df1_b, dg1, db1 = _ln_bwd(xf, [f1], ln1_g, ln1_b, [(dr2, ALPHA), (dx1_m, 1.0)], name="ffn1_ln_bwd", fs=0.5)
    dh1, (r0,) = _ffn_out_dg(df1_b, w1out, h1, name="ffn1_out_dg", tm=tq, plans=[_chip_plan(part, js=(0,))])

    da_sum = jnp.sum(da_acc, axis=0)
    da_re, da_im = _unperm_cols(da_sum)
    gbb_re, gbb_im = [jnp.transpose(v, (0, 2, 1)) for v in _compact_extract(g_bmat, SSM_GROUP)]
    g_c_re, g_c_im_neg = _compact_extract(g_cmat, SSM_GROUP)
    g_c_im = -g_c_im_neg
    g_lam_re, g_lam_im, g_log_step, g_b_re, g_b_im = s5_vjp(
        (da_re.reshape(SSM_GROUPS, SSM_STATE), da_im.reshape(SSM_GROUPS, SSM_STATE), gbb_re, gbb_im))
    g_d = dd.reshape(SSM_GROUPS, SSM_GROUP)

    small_g = {'ln1_g': dg1, 'ln1_b': db1, 'conv_b': dconv[3:4], 'ssm_lam_re': g_lam_re, 'ssm_lam_im': g_lam_im,
               'ssm_log_step': g_log_step, 'ssm_b_re': g_b_re, 'ssm_b_im': g_b_im, 'ssm_c_re': g_c_re,
               'ssm_c_im': g_c_im, 'ssm_d': g_d, 'ln2_g': dg2, 'ln2_b': db2, 'ln3_g': dg3, 'ln3_b': db3,
               'ln4_g': dg4, 'ln4_b': db4}
    small_shapes = [args[nm].shape for nm in small] + [(3, CONV_CH), (1,)]
    g_pack = _pack([small_g[nm] for nm in small] + [dconv[0:3], loss_part[0:1, 0:1]])

    gb['ffn1_w_in'], (r1, (g_all,)) = _mm(x_b, dh1, name="ffn1_in_wg", **ffn_in_wg,
                                          plans=[_chip_plan(part, js=(1,)), _gather_plan([g_pack])])
    g_, (r2, s_) = _mm(a1, df1_b, name="ffn1_out_wg", **ffn_out_wg,
                       plans=[_chip_plan(part, js=(2,)), to_sibling('ffn1_w_in')])
    gb['ffn1_w_out'] = blocked('ffn1_w_out', g_)
    set_rem(['mix_w_in'], [r0, r1, r2])
    part = chip_sums(['ffn1_w_in'], s_)
    dx_f, (r_, s_) = _mm(dh1, w1in, name="ffn1_in_dg", **ffn_in_dg, plans=[_chip_plan(part), to_sibling('ffn1_w_out')])
    rem['ffn1_w_in'] = r_
    part = chip_sums(['ffn1_w_out'], s_)
    grad_x = _axpy(dr1, ALPHA, dx_f, name="grad_x")
    ((_, rem['ffn1_w_out'], _),) = _run_plans([_chip_plan(part)], name="grad_chips_ffn1_out")

    def full_cw(a):
        return lax.dynamic_update_slice(jnp.zeros((3, CONV_CH), F32), a[0], (0, me * cw_cols))

    zero1 = jnp.zeros((1,), F32)
    w_pack = _pack([args[nm] for nm in small] + [full_cw(conv_w), zero1])
    m_pack = _pack([args['m_' + nm] for nm in small] + [full_cw(m_conv_w), zero1])
    v_pack = _pack([args['v_' + nm] for nm in small] + [full_cw(v_conv_w), zero1])
    sg_sum, sd, sm, sv = _small_update(g_all, w_pack, m_pack, v_pack, name="small_update")
    res = {}
    for key, buf in (('grad_', sg_sum), ('delta_', sd), ('new_m_', sm), ('new_v_', sv)):
        parts = _unpack(buf, small_shapes)
        for nm, val in zip(small, parts[:len(small)]):
            res[key + nm] = val
        res[key + 'conv_w'] = lax.dynamic_slice(parts[len(small)], (0, me * cw_cols), (3, cw_cols))[None]
        if key == 'grad_':
            loss = parts[-1][0]

    me1 = me.reshape(1)
    for nm in big:
        gt, dl, mn, vn = _shard_update(gb[nm], sib[nm], rem[nm], me1, args[nm][0], args['m_' + nm][0],
                                       args['v_' + nm][0], name=f"update_{nm}")
        res['grad_' + nm], res['delta_' + nm], res['new_m_' + nm], res['new_v_' + nm] = gt[None], dl[None], mn[None], vn[None]

    outs = [loss, grad_x[None]]
    for key in ('grad_', 'delta_', 'new_m_', 'new_v_'):
        outs += [res[key + nm] for nm in order]
    return tuple(outs)
```

```python
import functools
import math

import jax
import jax.numpy as jnp
from jax import lax
from jax.experimental import pallas as pl
from jax.experimental.pallas import tpu as pltpu

F32 = jnp.float32
BF16 = jnp.bfloat16
MESH = pl.DeviceIdType.MESH

N_DEV = 8
ALPHA = 2.0 ** 0.25
LN_EPS = 1e-5
CONV_CH = 512
SSM_W = 512
SSM_GROUPS = 32
SSM_GROUP = 16
SSM_STATE = 64
SSM_CH = SSM_GROUPS * SSM_STATE
SCAN_CB = 512
SCAN_NCB = SSM_CH // SCAN_CB
SCAN_GPB = SSM_GROUPS // SCAN_NCB
SCAN_UW = SCAN_GPB * SSM_GROUP
SCAN_TT = 256
ADAM_LR = 0.001
ADAM_B1 = 0.9
ADAM_B2 = 0.999
ADAM_EPS = 1e-08
ADAM_WD = 0.01
ADAM_STEP = 10
VMEM_LIMIT = 56 * 1024 * 1024


def _cparams(sem=None, **kw):
    return pltpu.CompilerParams(dimension_semantics=sem, vmem_limit_bytes=VMEM_LIMIT, **kw)


def _mm(a, b, *, name, ja=None, jb=None, jo=None, a_flat=False, b_flat=False, o_flat=False,
        ta=False, tb=False, tm, tn, tk, nj=1, out_dtype=F32, plans=()):
    def dims(arr, j, flat):
        if j is None:
            return arr.shape
        if flat:
            return (arr.shape[0], arr.shape[1] // nj)
        assert arr.shape[0] == nj, (name, arr.shape, nj)
        return arr.shape[1:]

    ar, ac = dims(a, ja, a_flat)
    br, bc = dims(b, jb, b_flat)
    m, k = (ac, ar) if ta else (ar, ac)
    k2, n = (bc, br) if tb else (br, bc)
    assert k == k2, (name, a.shape, b.shape)
    assert m % tm == 0 and n % tn == 0 and k % tk == 0, (name, m, n, k, tm, tn, tk)
    njb = nj if 'b' in (ja, jb) else 1
    njc = nj if 'c' in (ja, jb) else 1
    nk = k // tk
    j_inside = njc > 1 and nk == 1 and not ta
    n_in = njc if j_inside else 1
    nred = nk if j_inside else njc * nk
    grid = (njb, m // tm, n // tn, 1 if j_inside else njc, nk)

    def make_spec(j, flat, blk, rfn, cfn, cols_per_j):
        def jsel(g, c):
            return g if j == 'b' else c
        if j is None:
            return pl.BlockSpec(blk, lambda g, i, jn, c, kk: (rfn(i, jn, kk), cfn(i, jn, kk)))
        if j == 'c' and j_inside:
            if flat:
                return pl.BlockSpec((blk[0], nj * blk[1]), lambda g, i, jn, c, kk: (rfn(i, jn, kk), 0))
            return pl.BlockSpec((nj,) + blk, lambda g, i, jn, c, kk: (0, rfn(i, jn, kk), cfn(i, jn, kk)))
        if flat:
            nb = cols_per_j // blk[1]
            return pl.BlockSpec(blk, lambda g, i, jn, c, kk: (rfn(i, jn, kk), jsel(g, c) * nb + cfn(i, jn, kk)))
        return pl.BlockSpec((None,) + blk,
                            lambda g, i, jn, c, kk: (jsel(g, c), rfn(i, jn, kk), cfn(i, jn, kk)))

    if ta:
        a_spec = make_spec(ja, a_flat, (tk, tm), lambda i, jn, kk: kk, lambda i, jn, kk: i, ac)
    else:
        a_spec = make_spec(ja, a_flat, (tm, tk), lambda i, jn, kk: i, lambda i, jn, kk: kk, ac)
    if tb:
        b_spec = make_spec(jb, b_flat, (tn, tk), lambda i, jn, kk: jn, lambda i, jn, kk: kk, bc)
    else:
        b_spec = make_spec(jb, b_flat, (tk, tn), lambda i, jn, kk: kk, lambda i, jn, kk: jn, bc)
    o_spec = make_spec(jo, o_flat, (tm, tn), lambda i, jn, kk: i, lambda i, jn, kk: jn, n)
    if jo is None:
        out_shape = (m, n)
    elif o_flat:
        out_shape = (m, nj * n)
    else:
        out_shape = (nj, m, n)

    dn = (((0 if ta else 1,), (1 if tb else 0,)), ((), ()))

    def operand(ref, j, flat, jj, width):
        if not (j == 'c' and j_inside):
            return ref[...]
        return ref[:, jj * width:(jj + 1) * width] if flat else ref[jj]

    def body(a_ref, b_ref, o_ref, *scratch):
        p = None
        for jj in range(n_in):
            q = lax.dot_general(operand(a_ref, ja, a_flat, jj, tk), operand(b_ref, jb, b_flat, jj, tk if tb else tn),
                                dn, preferred_element_type=F32)
            p = q if p is None else p + q
        if nred == 1:
            o_ref[...] = p.astype(o_ref.dtype)
        else:
            acc = scratch[0]
            r = pl.program_id(3) * nk + pl.program_id(4)

            @pl.when(r == 0)
            def _():
                acc[...] = p

            @pl.when(r > 0)
            def _():
                acc[...] += p

            @pl.when(r == nred - 1)
            def _():
                o_ref[...] = acc[...].astype(o_ref.dtype)

    res = _call_with_plans(
        body, plans, name=name, grid=grid, in_specs=[a_spec, b_spec], out_specs=[o_spec],
        out_shape=[jax.ShapeDtypeStruct(out_shape, out_dtype)],
        scratch_shapes=[] if nred == 1 else [pltpu.VMEM((tm, tn), F32)],
        semantics=("parallel", "parallel", "parallel", "arbitrary", "arbitrary"), operands=(a, b))
    return (res[0][0], res[1]) if plans else res[0][0]


def _sigmoid(v):
    return 1.0 / (1.0 + jnp.exp(-v))


def _row_spec(tm, cols, colblk=0):
    return pl.BlockSpec((tm, cols), lambda i: (i, colblk))


def _vec_spec(cols):
    return pl.BlockSpec((1, cols), lambda i: (0, 0))


def _ffn_in(xb, win, *, name, tm, plans=()):
    t, d = xb.shape
    nj, w, _ = win.shape
    half = nj // 2
    dn = (((1,), (1,)), ((), ()))

    def body(x_ref, wg_ref, wu_ref, a_ref, gu_ref):
        xv = x_ref[...]
        g = lax.dot_general(xv, wg_ref[...], dn, preferred_element_type=F32)
        u = lax.dot_general(xv, wu_ref[...], dn, preferred_element_type=F32)
        a_ref[...] = (g * _sigmoid(g) * u).astype(BF16)
        gu_ref[0] = g.astype(BF16)
        gu_ref[1] = u.astype(BF16)

    (a, gu), riders = _call_with_plans(
        body, plans, name=name, grid=(half, t // tm),
        in_specs=[pl.BlockSpec((tm, d), lambda j, i: (i, 0)),
                  pl.BlockSpec((None, w, d), lambda j, i: (j, 0, 0)),
                  pl.BlockSpec((None, w, d), lambda j, i: (j + half, 0, 0))],
        out_specs=[pl.BlockSpec((None, tm, w), lambda j, i: (j, i, 0)),
                   pl.BlockSpec((2, None, tm, w), lambda j, i: (0, j, i, 0))],
        out_shape=[jax.ShapeDtypeStruct((half, t, w), BF16), jax.ShapeDtypeStruct((2, half, t, w), BF16)],
        scratch_shapes=[], semantics=("parallel", "parallel"), operands=(xb, win, win))
    return a, gu, riders


def _ffn_out_dg(dfb, wout, gu, *, name, tm, plans=()):
    t, d = dfb.shape
    half, w, _ = wout.shape
    dn = (((1,), (1,)), ((), ()))

    def body(df_ref, w_ref, gu_ref, dh_ref):
        da = lax.dot_general(df_ref[...], w_ref[...], dn, preferred_element_type=F32)
        g = gu_ref[0].astype(F32)
        u = gu_ref[1].astype(F32)
        sg = _sigmoid(g)
        dh_ref[0] = (da * u * (sg * (1.0 + g * (1.0 - sg)))).astype(BF16)
        dh_ref[1] = (da * (g * sg)).astype(BF16)

    (out,), riders = _call_with_plans(
        body, plans, name=name, grid=(half, t // tm),
        in_specs=[pl.BlockSpec((tm, d), lambda j, i: (i, 0)),
                  pl.BlockSpec((None, w, d), lambda j, i: (j, 0, 0)),
                  pl.BlockSpec((2, None, tm, w), lambda j, i: (0, j, i, 0))],
        out_specs=[pl.BlockSpec((2, None, tm, w), lambda j, i: (0, j, i, 0))],
        out_shape=[jax.ShapeDtypeStruct((2, half, t, w), BF16)],
        scratch_shapes=[], semantics=("parallel", "parallel"), operands=(dfb, wout, gu))
    return out.reshape(2 * half, t, w), riders


def _ln_stats(r):
    mu = jnp.mean(r, axis=-1, keepdims=True)
    xc = r - mu
    var = jnp.mean(xc * xc, axis=-1, keepdims=True)
    rstd = lax.rsqrt(var + LN_EPS)
    return xc * rstd, rstd


def _ln_fwd(xin, fparts, gamma, beta, *, name, fs=1.0, ple=False, tm=256):
    t, d = xin.shape
    nf = len(fparts)

    def body(*refs):
        x_ref = refs[0]
        f_refs = refs[1:1 + nf]
        g_ref, b_ref, y_ref, yb_ref = refs[1 + nf:]
        if ple:
            f = f_refs[0][...] * _sigmoid(f_refs[1][...])
        else:
            f = fs * f_refs[0][...]
        xh, _ = _ln_stats(ALPHA * x_ref[...] + f)
        y = xh * g_ref[...] + b_ref[...]
        y_ref[...] = y
        yb_ref[...] = y.astype(BF16)

    return pl.pallas_call(
        body, name=name, grid=(t // tm,),
        in_specs=[_row_spec(tm, d)] * (1 + nf) + [_vec_spec(d), _vec_spec(d)],
        out_specs=[_row_spec(tm, d), _row_spec(tm, d)],
        out_shape=[jax.ShapeDtypeStruct((t, d), F32), jax.ShapeDtypeStruct((t, d), BF16)],
        compiler_params=_cparams(("parallel",)),
    )(xin, *fparts, gamma, beta)


def _ln_bwd(xin, fparts, gamma, beta, grads, *, name, fs=1.0, ple=False, target=None, tm=256):
    t, d = xin.shape
    nf = len(fparts)
    ng = len(grads)
    coefs = [c for _, c in grads]
    use_t = target is not None
    n_fout = 2 if ple else 1

    def body(*refs):
        pos = 0
        x_ref = refs[pos]; pos += 1
        f_refs = refs[pos:pos + nf]; pos += nf
        g_ref, b_ref = refs[pos:pos + 2]; pos += 2
        gr_refs = refs[pos:pos + ng]; pos += ng
        if use_t:
            t_ref = refs[pos]; pos += 1
        dr_ref = refs[pos]; pos += 1
        fo_refs = refs[pos:pos + n_fout]; pos += n_fout
        dg_ref, db_ref = refs[pos:pos + 2]; pos += 2
        if use_t:
            loss_ref = refs[pos]; pos += 1
        i = pl.program_id(0)

        if ple:
            pe = f_refs[0][...]
            sg = _sigmoid(f_refs[1][...])
            f = pe * sg
        else:
            f = fs * f_refs[0][...]
        xh, rstd = _ln_stats(ALPHA * x_ref[...] + f)
        gam = g_ref[...]
        if use_t:
            diff = xh * gam + b_ref[...] - t_ref[...]
            dy = diff * (1.0 / d)
            lpart = 0.5 * jnp.sum(jnp.sum(diff * diff, axis=-1, keepdims=True), axis=0, keepdims=True) * (1.0 / d)
        else:
            dy = coefs[0] * gr_refs[0][...]
            for c, r in zip(coefs[1:], gr_refs[1:]):
                dy = dy + c * r[...]
        dxh = dy * gam
        m1 = jnp.mean(dxh, axis=-1, keepdims=True)
        m2 = jnp.mean(dxh * xh, axis=-1, keepdims=True)
        dr = rstd * (dxh - m1 - xh * m2)
        dr_ref[...] = dr
        if ple:
            fo_refs[0][...] = (dr * sg).astype(BF16)
            fo_refs[1][...] = (dr * pe * (sg * (1.0 - sg))).astype(BF16)
        else:
            fo_refs[0][...] = (fs * dr).astype(BF16)
        dgp = jnp.sum(dy * xh, axis=0, keepdims=True)
        dbp = jnp.sum(dy, axis=0, keepdims=True)

        @pl.when(i == 0)
        def _():
            dg_ref[...] = dgp
            db_ref[...] = dbp
            if use_t:
                loss_ref[...] = jnp.broadcast_to(lpart, loss_ref.shape)

        @pl.when(i > 0)
        def _():
            dg_ref[...] += dgp
            db_ref[...] += dbp
            if use_t:
                loss_ref[...] += jnp.broadcast_to(lpart, loss_ref.shape)

    ins = [xin, *fparts, gamma, beta, *[g for g, _ in grads]] + ([target] if use_t else [])
    in_specs = ([_row_spec(tm, d)] * (1 + nf) + [_vec_spec(d), _vec_spec(d)] + [_row_spec(tm, d)] * ng
                + ([_row_spec(tm, d)] if use_t else []))
    out_specs = [_row_spec(tm, d)] * (1 + n_fout) + [_vec_spec(d), _vec_spec(d)] + ([_vec_spec(128)] if use_t else [])
    out_shape = ([jax.ShapeDtypeStruct((t, d), F32)] + [jax.ShapeDtypeStruct((t, d), BF16)] * n_fout
                 + [jax.ShapeDtypeStruct((1, d), F32)] * 2 + ([jax.ShapeDtypeStruct((1, 128), F32)] if use_t else []))
    return pl.pallas_call(
        body, name=name, grid=(t // tm,), in_specs=in_specs, out_specs=out_specs, out_shape=out_shape,
        compiler_params=_cparams(("arbitrary",)),
    )(*ins)


def _axpy(a, ca, b, *, name, tm=256):
    t, d = a.shape

    def body(a_ref, b_ref, o_ref):
        o_ref[...] = ca * a_ref[...] + b_ref[...]

    return pl.pallas_call(
        body, name=name, grid=(t // tm,), in_specs=[_row_spec(tm, d)] * 2, out_specs=_row_spec(tm, d),
        out_shape=jax.ShapeDtypeStruct((t, d), F32), compiler_params=_cparams(("parallel",)),
    )(a, b)


def _conv_fwd(proj, cw, cb, *, name, tm=256):
    t = proj.shape[0]
    c = CONV_CH
    hb = tm // 8

    def body(b_ref, c_ref, h_ref, cp_ref, hp_ref, w_ref, bias_ref, o_ref, q_scr):
        i = pl.program_id(0)
        q = c_ref[...] * h_ref[...]
        halo = jnp.where(i > 0, cp_ref[...] * hp_ref[...], 0.0)
        q_scr[0:8, :] = halo
        q_scr[8:, :] = q
        z = (w_ref[2:3, :] * q + w_ref[1:2, :] * q_scr[pl.ds(7, tm), :] + w_ref[0:1, :] * q_scr[pl.ds(6, tm), :]
             + bias_ref[...])
        o_ref[...] = (b_ref[...] * z).astype(BF16)

    prev = lambda blk: pl.BlockSpec((8, c), lambda i: (jnp.maximum(i * hb - 1, 0), blk))
    return pl.pallas_call(
        body, name=name, grid=(t // tm,),
        in_specs=[_row_spec(tm, c, 0), _row_spec(tm, c, 1), _row_spec(tm, c, 2), prev(1), prev(2),
                  pl.BlockSpec((8, c), lambda i: (0, 0)), _vec_spec(c)],
        out_specs=_row_spec(tm, c),
        out_shape=jax.ShapeDtypeStruct((t, c), BF16),
        scratch_shapes=[pltpu.VMEM((tm + 8, c), F32)],
        compiler_params=_cparams(("parallel",)),
    )(proj, proj, proj, proj, proj, cw, cb)


def _conv_bwd(dyc, proj, cw, cb, *, name, tm=256):
    t = proj.shape[0]
    c = CONV_CH
    hb = tm // 8
    nblk = t // 8

    def body(d_ref, b_ref, c_ref, h_ref, cp_ref, hp_ref, dn_ref, bn_ref, w_ref, bias_ref,
             db_ref, dc_ref, dh_ref, dw_ref, q_scr, z_scr):
        i = pl.program_id(0)
        last = pl.num_programs(0) - 1
        cc = c_ref[...]
        ch = h_ref[...]
        q = cc * ch
        halo = jnp.where(i > 0, cp_ref[...] * hp_ref[...], 0.0)
        q_scr[0:8, :] = halo
        q_scr[8:, :] = q
        w0, w1, w2 = w_ref[0:1, :], w_ref[1:2, :], w_ref[2:3, :]
        qm1 = q_scr[pl.ds(7, tm), :]
        qm2 = q_scr[pl.ds(6, tm), :]
        z = w2 * q + w1 * qm1 + w0 * qm2 + bias_ref[...]
        d = d_ref[...]
        bb = b_ref[...]
        db_ref[...] = (d * z).astype(BF16)
        dz = d * bb
        z_scr[0:tm, :] = dz
        z_scr[tm:, :] = jnp.where(i < last, dn_ref[...] * bn_ref[...], 0.0)
        dq = w2 * dz + w1 * z_scr[pl.ds(1, tm), :] + w0 * z_scr[pl.ds(2, tm), :]
        dc_ref[...] = (dq * ch).astype(BF16)
        dh_ref[...] = (dq * cc).astype(BF16)
        row = lax.broadcasted_iota(jnp.int32, (8, c), 0)
        part = jnp.zeros((8, c), F32)
        for k, term in enumerate((dz * qm2, dz * qm1, dz * q, dz)):
            part = jnp.where(row == k, jnp.sum(term, axis=0, keepdims=True), part)

        @pl.when(i == 0)
        def _():
            dw_ref[...] = part

        @pl.when(i > 0)
        def _():
            dw_ref[...] += part

    prev = lambda blk: pl.BlockSpec((8, c), lambda i: (jnp.maximum(i * hb - 1, 0), blk))
    nxt_p = pl.BlockSpec((8, c), lambda i: (jnp.minimum((i + 1) * hb, nblk - 1), 0))
    nxt_d = pl.BlockSpec((8, c), lambda i: (jnp.minimum((i + 1) * hb, nblk - 1), 0))
    return pl.pallas_call(
        body, name=name, grid=(t // tm,),
        in_specs=[_row_spec(tm, c), _row_spec(tm, c, 0), _row_spec(tm, c, 1), _row_spec(tm, c, 2),
                  prev(1), prev(2), nxt_d, nxt_p, pl.BlockSpec((8, c), lambda i: (0, 0)), _vec_spec(c)],
        out_specs=[_row_spec(tm, c)] * 3 + [pl.BlockSpec((8, c), lambda i: (0, 0))],
        out_shape=[jax.ShapeDtypeStruct((t, c), BF16)] * 3 + [jax.ShapeDtypeStruct((8, c), F32)],
        scratch_shapes=[pltpu.VMEM((tm + 8, c), F32), pltpu.VMEM((tm + 8, c), F32)],
        compiler_params=_cparams(("arbitrary",)),
    )(dyc, proj, proj, proj, proj, proj, dyc, proj, cw, cb)


def _gate_fwd(yconv, glu, proj, *, name, tm=256):
    t, d = yconv.shape

    def body(yc_ref, ga_ref, gb_ref, gc_ref, gs_ref, o_ref):
        yssm = ga_ref[...] * _sigmoid(gb_ref[...])
        o_ref[...] = (_sigmoid(gc_ref[...]) * yc_ref[...] + _sigmoid(gs_ref[...]) * yssm).astype(BF16)

    return pl.pallas_call(
        body, name=name, grid=(t // tm,),
        in_specs=[_row_spec(tm, d), _row_spec(tm, d, 0), _row_spec(tm, d, 1), _row_spec(tm, d, 2), _row_spec(tm, d, 3)],
        out_specs=_row_spec(tm, d), out_shape=jax.ShapeDtypeStruct((t, d), BF16),
        compiler_params=_cparams(("parallel",)),
    )(yconv, glu, glu, proj, proj)


def _gate_bwd(dm, yconv, glu, proj, *, name, tm=256):
    t, d = yconv.shape

    def body(dm_ref, yc_ref, ga_ref, gb_ref, gc_ref, gs_ref, dyc_ref, dglu_ref, dgate_ref):
        dmv = dm_ref[...]
        sc = _sigmoid(gc_ref[...])
        ss = _sigmoid(gs_ref[...])
        sb = _sigmoid(gb_ref[...])
        ga = ga_ref[...]
        yssm = ga * sb
        dyc_ref[...] = (dmv * sc).astype(BF16)
        dgate_ref[:, 0:d] = (dmv * yc_ref[...] * (sc * (1.0 - sc))).astype(BF16)
        dys = dmv * ss
        dgate_ref[:, d:2 * d] = (dmv * yssm * (ss * (1.0 - ss))).astype(BF16)
        dglu_ref[:, 0:d] = (dys * sb).astype(BF16)
        dglu_ref[:, d:2 * d] = (dys * ga * (sb * (1.0 - sb))).astype(BF16)

    return pl.pallas_call(
        body, name=name, grid=(t // tm,),
        in_specs=[_row_spec(tm, d), _row_spec(tm, d), _row_spec(tm, d, 0), _row_spec(tm, d, 1),
                  _row_spec(tm, d, 2), _row_spec(tm, d, 3)],
        out_specs=[_row_spec(tm, d), _row_spec(tm, 2 * d), _row_spec(tm, 2 * d)],
        out_shape=[jax.ShapeDtypeStruct((t, d), BF16), jax.ShapeDtypeStruct((t, 2 * d), BF16),
                   jax.ShapeDtypeStruct((t, 2 * d), BF16)],
        compiler_params=_cparams(("parallel",)),
    )(dm, yconv, glu, glu, proj, proj)


_GELU_C = math.sqrt(2.0 / math.pi)


def _gelu(v):
    return 0.5 * v * (1.0 + jnp.tanh(_GELU_C * (v + 0.044715 * v * v * v)))


def _gelu_grad(v):
    th = jnp.tanh(_GELU_C * (v + 0.044715 * v * v * v))
    return 0.5 * (1.0 + th) + 0.5 * v * (1.0 - th * th) * (_GELU_C * (1.0 + 3.0 * 0.044715 * v * v))


def _cmul(ar, ai, br, bi):
    return ar * br - ai * bi, ar * bi + ai * br


def _scan_fwd(proj, bmat, tab, *, name, plans=()):
    t = proj.shape[0]
    tt, cbw = SCAN_TT, SCAN_CB
    w2 = 2 * cbw

    def body(u_ref, b_ref, tab_ref, s_ref, sb_ref, bu_scr, carry):
        ti = pl.program_id(1)

        @pl.when(ti == 0)
        def _():
            carry[...] = jnp.zeros_like(carry)

        bu_scr[...] = jnp.dot(u_ref[...].astype(BF16), b_ref[...], preferred_element_type=F32)
        row = lax.broadcasted_iota(jnp.int32, (8, cbw), 0)

        def blk(bi, c):
            cr, ci = c
            r0 = pl.multiple_of(bi * 8, 8)
            xr = bu_scr[pl.ds(r0, 8), 0:cbw]
            xi = bu_scr[pl.ds(r0, 8), cbw:w2]
            for k, sh in enumerate((1, 2, 4)):
                kr = tab_ref[k:k + 1, 0:cbw]
                ki = tab_ref[k:k + 1, cbw:w2]
                sr = jnp.where(row >= sh, pltpu.roll(xr, sh, 0), 0.0)
                si = jnp.where(row >= sh, pltpu.roll(xi, sh, 0), 0.0)
                pr, pi = _cmul(kr, ki, sr, si)
                xr = xr + pr
                xi = xi + pi
            pr, pi = _cmul(tab_ref[8:16, 0:cbw], tab_ref[8:16, cbw:w2], cr, ci)
            xr = xr + pr
            xi = xi + pi
            s_ref[pl.ds(r0, 8), 0:cbw] = xr
            s_ref[pl.ds(r0, 8), cbw:w2] = xi
            return (jnp.broadcast_to(xr[7:8, :], (8, cbw)), jnp.broadcast_to(xi[7:8, :], (8, cbw)))

        cr, ci = lax.fori_loop(0, tt // 8, blk, (carry[:, 0:cbw], carry[:, cbw:w2]))
        carry[:, 0:cbw] = cr
        carry[:, cbw:w2] = ci
        sb_ref[...] = s_ref[...].astype(BF16)

    (s, sb), riders = _call_with_plans(
        body, plans, name=name, grid=(SCAN_NCB, t // tt),
        in_specs=[pl.BlockSpec((tt, SCAN_UW), lambda cb, ti: (ti, 3 * SCAN_NCB + cb)),
                  pl.BlockSpec((None, SCAN_UW, w2), lambda cb, ti: (cb, 0, 0)),
                  pl.BlockSpec((16, w2), lambda cb, ti: (0, cb))],
        out_specs=[pl.BlockSpec((tt, w2), lambda cb, ti: (ti, cb))] * 2,
        out_shape=[jax.ShapeDtypeStruct((t, 2 * SSM_CH), F32), jax.ShapeDtypeStruct((t, 2 * SSM_CH), BF16)],
        scratch_shapes=[pltpu.VMEM((tt, w2), F32), pltpu.VMEM((8, w2), F32)],
        semantics=("parallel", "arbitrary"), operands=(proj, bmat, tab))
    return s, sb, riders


def _scan_bwd(dyb, cmat_t, s, tabb, *, name, plans=()):
    t = s.shape[0]
    tt, cbw = SCAN_TT, SCAN_CB
    w2 = 2 * cbw
    nt = t // tt
    hb = tt // 8

    def body(dy_ref, c_ref, s_ref, sp_ref, tab_ref, h_ref, da_ref, g_scr, s_scr, carry):
        ti = pl.program_id(1)

        @pl.when(ti == 0)
        def _():
            carry[...] = jnp.zeros_like(carry)
            da_ref[...] = jnp.zeros_like(da_ref)

        g_scr[...] = jnp.dot(dy_ref[...], c_ref[...], preferred_element_type=F32)
        s_scr[0:8, :] = jnp.where(ti < nt - 1, sp_ref[...], 0.0)
        s_scr[8:, :] = s_ref[...]
        row = lax.broadcasted_iota(jnp.int32, (8, cbw), 0)

        def blk(k, c):
            cr, ci, ar, ai = c
            bi = hb - 1 - k
            r0 = pl.multiple_of(bi * 8, 8)
            xr = g_scr[pl.ds(r0, 8), 0:cbw]
            xi = g_scr[pl.ds(r0, 8), cbw:w2]
            for j, sh in enumerate((1, 2, 4)):
                kr = tab_ref[j:j + 1, 0:cbw]
                ki = tab_ref[j:j + 1, cbw:w2]
                sr = jnp.where(row < 8 - sh, pltpu.roll(xr, 8 - sh, 0), 0.0)
                si = jnp.where(row < 8 - sh, pltpu.roll(xi, 8 - sh, 0), 0.0)
                pr, pi = _cmul(kr, ki, sr, si)
                xr = xr + pr
                xi = xi + pi
            pr, pi = _cmul(tab_ref[8:16, 0:cbw], tab_ref[8:16, cbw:w2], cr, ci)
            xr = xr + pr
            xi = xi + pi
            h_ref[pl.ds(r0, 8), 0:cbw] = xr.astype(BF16)
            h_ref[pl.ds(r0, 8), cbw:w2] = xi.astype(BF16)
            pvr = s_scr[pl.ds(r0, 8), 0:cbw]
            pvi = s_scr[pl.ds(r0, 8), cbw:w2]
            cur_r = s_scr[pl.ds(r0 + 8, 8), 0:cbw]
            cur_i = s_scr[pl.ds(r0 + 8, 8), cbw:w2]
            spr = jnp.where(row == 0, jnp.broadcast_to(pvr[7:8, :], (8, cbw)), pltpu.roll(cur_r, 1, 0))
            spi = jnp.where(row == 0, jnp.broadcast_to(pvi[7:8, :], (8, cbw)), pltpu.roll(cur_i, 1, 0))
            ar = ar + spr * xr + spi * xi
            ai = ai + spr * xi - spi * xr
            return (jnp.broadcast_to(xr[0:1, :], (8, cbw)), jnp.broadcast_to(xi[0:1, :], (8, cbw)), ar, ai)

        z = jnp.zeros((8, cbw), F32)
        cr, ci, ar, ai = lax.fori_loop(0, hb, blk, (carry[:, 0:cbw], carry[:, cbw:w2], z, z))
        carry[:, 0:cbw] = cr
        carry[:, cbw:w2] = ci
        da_ref[:, 0:cbw] += ar
        da_ref[:, cbw:w2] += ai

    rt = lambda ti: nt - 1 - ti
    (h, da), riders = _call_with_plans(
        body, plans, name=name, grid=(SCAN_NCB, nt),
        in_specs=[pl.BlockSpec((tt, SCAN_UW), lambda cb, ti: (rt(ti), cb)),
                  pl.BlockSpec((None, SCAN_UW, w2), lambda cb, ti: (cb, 0, 0)),
                  pl.BlockSpec((tt, w2), lambda cb, ti: (rt(ti), cb)),
                  pl.BlockSpec((8, w2), lambda cb, ti: (jnp.maximum(rt(ti) * hb - 1, 0), cb)),
                  pl.BlockSpec((16, w2), lambda cb, ti: (0, cb))],
        out_specs=[pl.BlockSpec((tt, w2), lambda cb, ti: (rt(ti), cb)),
                   pl.BlockSpec((8, w2), lambda cb, ti: (0, cb))],
        out_shape=[jax.ShapeDtypeStruct((t, 2 * SSM_CH), BF16), jax.ShapeDtypeStruct((8, 2 * SSM_CH), F32)],
        scratch_shapes=[pltpu.VMEM((tt, w2), F32), pltpu.VMEM((tt + 8, w2), F32), pltpu.VMEM((8, w2), F32)],
        semantics=("parallel", "arbitrary"), operands=(dyb, cmat_t, s, s, tabb))
    return h, da, riders


def _s5_out(ymm, proj, dvec, *, name, tm=256):
    t, w = ymm.shape

    def body(y_ref, u_ref, d_ref, yo_ref, sg_ref, ub_ref):
        u = u_ref[...]
        y = y_ref[...] + d_ref[...] * u
        yo_ref[...] = y
        sg_ref[...] = _gelu(y).astype(BF16)
        ub_ref[...] = u.astype(BF16)

    return pl.pallas_call(
        body, name=name, grid=(t // tm,),
        in_specs=[_row_spec(tm, w), _row_spec(tm, w, 3), _vec_spec(w)],
        out_specs=[_row_spec(tm, w)] * 3,
        out_shape=[jax.ShapeDtypeStruct((t, w), F32), jax.ShapeDtypeStruct((t, w), BF16), jax.ShapeDtypeStruct((t, w), BF16)],
        compiler_params=_cparams(("parallel",)),
    )(ymm, proj, dvec)


def _s5_bwd_in(dsg, y, proj, *, name, tm=256):
    t, w = y.shape

    def body(d_ref, y_ref, u_ref, dy_ref, dyb_ref, dd_ref):
        i = pl.program_id(0)
        dy = d_ref[...] * _gelu_grad(y_ref[...])
        dy_ref[...] = dy
        dyb_ref[...] = dy.astype(BF16)
        part = jnp.sum(dy * u_ref[...], axis=0, keepdims=True)

        @pl.when(i == 0)
        def _():
            dd_ref[...] = part

        @pl.when(i > 0)
        def _():
            dd_ref[...] += part

    return pl.pallas_call(
        body, name=name, grid=(t // tm,),
        in_specs=[_row_spec(tm, w), _row_spec(tm, w), _row_spec(tm, w, 3)],
        out_specs=[_row_spec(tm, w), _row_spec(tm, w), _vec_spec(w)],
        out_shape=[jax.ShapeDtypeStruct((t, w), F32), jax.ShapeDtypeStruct((t, w), BF16), jax.ShapeDtypeStruct((1, w), F32)],
        compiler_params=_cparams(("arbitrary",)),
    )(dsg, y, proj)


def _s5_du(dumm, dy, dvec, *, name, tm=256):
    t, w = dy.shape

    def body(a_ref, dy_ref, d_ref, o_ref):
        o_ref[...] = (a_ref[...] + d_ref[...] * dy_ref[...]).astype(BF16)

    return pl.pallas_call(
        body, name=name, grid=(t // tm,), in_specs=[_row_spec(tm, w), _row_spec(tm, w), _vec_spec(w)],
        out_specs=_row_spec(tm, w), out_shape=jax.ShapeDtypeStruct((t, w), BF16),
        compiler_params=_cparams(("parallel",)),
    )(dumm, dy, dvec)


def _s5_discretise(lam_re, lam_im, log_step, b_re, b_im):
    lam = lax.complex(lam_re, lam_im)
    dt = jnp.exp(log_step)[:, None]
    a = jnp.exp(lam * dt)
    bbar = ((a - 1.0) / lam)[..., None] * lax.complex(b_re, b_im)
    return jnp.real(a), jnp.imag(a), jnp.real(bbar), jnp.imag(bbar)


def _perm_cols(re, im):
    lead = re.shape[:-1]
    r = re.reshape(lead + (SCAN_NCB, 1, SCAN_CB))
    i = im.reshape(lead + (SCAN_NCB, 1, SCAN_CB))
    return jnp.concatenate([r, i], axis=-2).reshape(lead + (2 * SSM_CH,))


def _unperm_cols(x):
    lead = x.shape[:-1]
    y = x.reshape(lead + (SCAN_NCB, 2, SCAN_CB))
    return y[..., 0, :].reshape(lead + (SSM_CH,)), y[..., 1, :].reshape(lead + (SSM_CH,))


def _compact(re, im):
    _, r, c = re.shape
    eye = jnp.eye(SCAN_GPB, dtype=re.dtype)

    def half(x):
        x = x.reshape(SCAN_NCB, SCAN_GPB, r, c)
        return (eye[None, :, None, :, None] * x[:, :, :, None, :]).reshape(SCAN_NCB, SCAN_GPB * r, SCAN_GPB * c)

    return jnp.concatenate([half(re), half(im)], axis=-1)


def _compact_extract(x, r):
    c = SSM_STATE
    eye = jnp.eye(SCAN_GPB, dtype=x.dtype)
    y = x.reshape(SCAN_NCB, SCAN_GPB, r, 2, SCAN_GPB, c)
    dg = jnp.sum(y * eye[None, :, None, None, :, None], axis=4).reshape(SSM_GROUPS, r, 2, c)
    return dg[:, :, 0, :], dg[:, :, 1, :]


def _pow_table(ar, ai, descending=False):
    ar = ar.reshape(1, SSM_CH)
    ai = ai.reshape(1, SSM_CH)
    pw = [(ar, ai)]
    for _ in range(7):
        pw.append(_cmul(pw[-1][0], pw[-1][1], ar, ai))
    zero = (jnp.zeros_like(ar), jnp.zeros_like(ar))
    rows = [pw[0], pw[1], pw[3]] + [zero] * 5 + (pw[::-1] if descending else pw)
    re = jnp.concatenate([r for r, _ in rows], axis=0)
    im = jnp.concatenate([i for _, i in rows], axis=0)
    return _perm_cols(re, im)


def _place():
    x, y, c = lax.axis_index("x"), lax.axis_index("y"), lax.axis_index("c")
    chips = [(1 - x, y), (x, 1 - y), (1 - x, 1 - y)]
    return x, y, c, chips


def _dev(px, py, pc):
    return 4 * px + 2 * py + pc


class _Plan:
    def __init__(self, ins, out_shapes, sem_shapes, start, finish):
        self.ins, self.out_shapes, self.sem_shapes = list(ins), list(out_shapes), list(sem_shapes)
        self.start, self.finish = start, finish


def _split_plan_refs(plans, in_refs, out_refs, sem_refs):
    res, i, o, s = [], 0, 0, 0
    for p in plans:
        ni, no, ns = len(p.ins), len(p.out_shapes), len(p.sem_shapes)
        res.append((in_refs[i:i + ni], out_refs[o:o + no], sem_refs[s:s + ns]))
        i, o, s = i + ni, o + no, s + ns
    return res


def _run_plans(plans, *, name):
    ins = [a for p in plans for a in p.ins]
    outs = [o for p in plans for o in p.out_shapes]
    sems = [s for p in plans for s in p.sem_shapes]
    any_spec = pl.BlockSpec(memory_space=pl.ANY)

    def body(*refs):
        parts = _split_plan_refs(plans, refs[:len(ins)], refs[len(ins):len(ins) + len(outs)], refs[len(ins) + len(outs):])
        for p, r in zip(plans, parts):
            p.start(*r)
        for p, r in zip(plans, parts):
            p.finish(*r)

    res = pl.pallas_call(body, name=name, in_specs=[any_spec] * len(ins), out_specs=[any_spec] * len(outs),
                         out_shape=outs, scratch_shapes=sems)(*ins)
    return _split_plan_refs(plans, [], res, [])


def _call_with_plans(body, plans, *, name, grid, in_specs, out_specs, out_shape, scratch_shapes, semantics, operands):
    plans = list(plans)
    if not plans:
        res = pl.pallas_call(body, name=name, grid=grid, in_specs=in_specs, out_specs=out_specs, out_shape=out_shape,
                             scratch_shapes=scratch_shapes, compiler_params=_cparams(semantics))(*operands)
        return list(res), []
    n_in, n_out, n_scr = len(in_specs), len(out_specs), len(scratch_shapes)
    p_ins = [a for p in plans for a in p.ins]
    p_outs = [o for p in plans for o in p.out_shapes]
    p_sems = [s for p in plans for s in p.sem_shapes]
    nsteps = math.prod(grid)
    any_spec = pl.BlockSpec(memory_space=pl.ANY)

    def wrapped(*refs):
        bounds = [n_in, len(p_ins), n_out, len(p_outs), n_scr]
        parts, pos = [], 0
        for b in bounds:
            parts.append(refs[pos:pos + b])
            pos += b
        ins, p_in, outs, p_out, scr = parts
        step = pl.program_id(0)
        for ax in range(1, len(grid)):
            step = step * grid[ax] + pl.program_id(ax)
        riders = _split_plan_refs(plans, p_in, p_out, refs[pos:])

        @pl.when(step == 0)
        def _():
            for p, r in zip(plans, riders):
                p.start(*r)

        body(*ins, *outs, *scr)

        @pl.when(step == nsteps - 1)
        def _():
            for p, r in zip(plans, riders):
                p.finish(*r)

    res = pl.pallas_call(
        wrapped, name=name, grid=grid, in_specs=list(in_specs) + [any_spec] * len(p_ins),
        out_specs=list(out_specs) + [any_spec] * len(p_outs), out_shape=list(out_shape) + p_outs,
        scratch_shapes=list(scratch_shapes) + p_sems, compiler_params=_cparams(("arbitrary",) * len(grid)),
    )(*operands, *p_ins)
    return list(res[:n_out]), [r[1] for r in _split_plan_refs(plans, [], res[n_out:], [])]


def _gather_plan(shards):
    n = len(shards)

    def make(ins, outs, sems):
        send, recv, lsem = sems
        x, y, c, chips = _place()
        me, sib = (x, y, c), (x, y, 1 - c)

        def copy(w, k, block, to, src=None):
            dst = outs[w].at[_dev(*block)]
            return pltpu.make_async_remote_copy(
                src_ref=dst if src is None else src, dst_ref=dst,
                send_sem=send.at[w * 7 + k], recv_sem=recv.at[w * 7 + k], device_id=to, device_id_type=MESH)

        mine = [pltpu.make_async_copy(ins[w], outs[w].at[_dev(*me)], lsem.at[w]) for w in range(n)]
        first = []
        for w in range(n):
            first.append(copy(w, 0, me, sib, src=ins[w]))
            first += [copy(w, 1 + j, me, (*chip, c), src=ins[w]) for j, chip in enumerate(chips)]
        return copy, mine, first, me, sib, c, chips

    def start(ins, outs, sems):
        _, mine, first, *_ = make(ins, outs, sems)
        for cp in mine + first:
            cp.start()

    def finish(ins, outs, sems):
        copy, mine, first, me, sib, c, chips = make(ins, outs, sems)
        passed = []
        for j, chip in enumerate(chips):
            for w in range(n):
                copy(w, 1 + j, (*chip, c), me).wait_recv()
                fwd = copy(w, 4 + j, (*chip, c), sib)
                fwd.start()
                passed.append(fwd)
        for w in range(n):
            copy(w, 0, sib, me).wait_recv()
        for j, chip in enumerate(chips):
            for w in range(n):
                copy(w, 4 + j, (*chip, 1 - c), me).wait_recv()
        for cp in first + passed:
            cp.wait_send()
        for cp in mine:
            cp.wait()

    return _Plan(shards, [jax.ShapeDtypeStruct((N_DEV,) + s.shape, s.dtype) for s in shards],
                 [pltpu.SemaphoreType.DMA((7 * n,)), pltpu.SemaphoreType.DMA((7 * n,)), pltpu.SemaphoreType.DMA((n,))],
                 start, finish)


def _swap_plan(copies_of, n_copies, ins, out_shapes):
    def cps(in_refs, out_refs, sems):
        return copies_of(in_refs, out_refs, sems[0], sems[1])

    def start(in_refs, out_refs, sems):
        for cp in cps(in_refs, out_refs, sems):
            cp.start()

    def finish(in_refs, out_refs, sems):
        all_cps = cps(in_refs, out_refs, sems)
        for cp in all_cps:
            cp.wait_recv()
        for cp in all_cps:
            cp.wait_send()

    return _Plan(ins, out_shapes, [pltpu.SemaphoreType.DMA((n_copies,)), pltpu.SemaphoreType.DMA((n_copies,))],
                 start, finish)


def _sibling_plan(grads):
    n = len(grads)

    def copies(ins, outs, send, recv):
        x, y, c, chips = _place()
        owners = [(x, y)] + chips
        return [pltpu.make_async_remote_copy(
            src_ref=ins[w].at[_dev(*chip, 1 - c)], dst_ref=outs[w].at[k], send_sem=send.at[w * 4 + k],
            recv_sem=recv.at[w * 4 + k], device_id=(x, y, 1 - c), device_id_type=MESH)
            for w in range(n) for k, chip in enumerate(owners)]

    return _swap_plan(copies, 4 * n, grads, [jax.ShapeDtypeStruct((4,) + g.shape[1:], g.dtype) for g in grads])


def _chip_plan(parts, js=(0, 1, 2)):
    n, nj = len(parts), len(js)

    def copies(ins, outs, send, recv):
        x, y, c, chips = _place()
        return [pltpu.make_async_remote_copy(
            src_ref=ins[w].at[j], dst_ref=outs[w * nj + k], send_sem=send.at[w * nj + k],
            recv_sem=recv.at[w * nj + k], device_id=(*chips[j], c), device_id_type=MESH)
            for w in range(n) for k, j in enumerate(js)]

    return _swap_plan(copies, n * nj, parts,
                      [jax.ShapeDtypeStruct(p.shape[1:], p.dtype) for p in parts for _ in js])


UPDATE_TILE_BYTES = 768 * 1024


def _row_tile(r, c):
    best = 8
    for t in range(8, r + 1, 8):
        if r % t == 0 and t * c * 4 <= UPDATE_TILE_BYTES:
            best = t
    return best


def _chip_partial(g, sib, ids, *, name):
    _, r, c = g.shape
    tr = _row_tile(r, c)

    def body(ids_ref, g_ref, s_ref, o_ref):
        o_ref[...] = (g_ref[...] + s_ref[...]).astype(BF16)

    return pl.pallas_call(
        body, name=name,
        grid_spec=pltpu.PrefetchScalarGridSpec(
            num_scalar_prefetch=1, grid=(3, r // tr),
            in_specs=[pl.BlockSpec((None, tr, c), lambda j, i, ids_ref: (ids_ref[j], i, 0)),
                      pl.BlockSpec((None, tr, c), lambda j, i, ids_ref: (j + 1, i, 0))],
            out_specs=pl.BlockSpec((None, tr, c), lambda j, i, ids_ref: (j, i, 0))),
        out_shape=jax.ShapeDtypeStruct((3, r, c), BF16),
        compiler_params=_cparams(("parallel", "parallel")),
    )(ids, g, sib)


def _adamw_math(w, g, m, v):
    m = ADAM_B1 * m + (1.0 - ADAM_B1) * g
    v = ADAM_B2 * v + (1.0 - ADAM_B2) * (g * g)
    m_hat = m / (1.0 - ADAM_B1 ** ADAM_STEP)
    v_hat = v / (1.0 - ADAM_B2 ** ADAM_STEP)
    delta = -ADAM_LR * (m_hat / (jnp.sqrt(v_hat) + ADAM_EPS) + ADAM_WD * w)
    return delta, m, v


def _shard_update(g, sib, rem, me, w, m, v, *, name):
    r, c = w.shape
    tr = _row_tile(r, c)

    def body(me_ref, g_ref, s_ref, r0_ref, r1_ref, r2_ref, w_ref, m_ref, v_ref, go_ref, d_ref, mo_ref, vo_ref):
        gt = g_ref[...] + s_ref[...]
        gt = gt + r0_ref[...].astype(F32)
        gt = gt + r1_ref[...].astype(F32)
        gt = gt + r2_ref[...].astype(F32)
        go_ref[...] = gt
        d, mn, vn = _adamw_math(w_ref[...], gt, m_ref[...], v_ref[...])
        d_ref[...] = d
        mo_ref[...] = mn
        vo_ref[...] = vn

    blk = lambda k: pl.BlockSpec((None, tr, c), lambda i, me_ref: (k, i, 0))
    plain = pl.BlockSpec((tr, c), lambda i, me_ref: (i, 0))
    return pl.pallas_call(
        body, name=name,
        grid_spec=pltpu.PrefetchScalarGridSpec(
            num_scalar_prefetch=1, grid=(r // tr,),
            in_specs=[pl.BlockSpec((None, tr, c), lambda i, me_ref: (me_ref[0], i, 0)), blk(0), plain, plain, plain,
                      plain, plain, plain],
            out_specs=[plain] * 4),
        out_shape=[jax.ShapeDtypeStruct((r, c), F32)] * 4,
        compiler_params=_cparams(("parallel",)),
    )(me, g, sib, *rem, w, m, v)


def _small_update(gathered, w, m, v, *, name):
    _, r, c = gathered.shape

    def body(g_ref, w_ref, m_ref, v_ref, go_ref, d_ref, mo_ref, vo_ref):
        gt = g_ref[0]
        for k in range(1, N_DEV):
            gt = gt + g_ref[k]
        go_ref[...] = gt
        d, mn, vn = _adamw_math(w_ref[...], gt, m_ref[...], v_ref[...])
        d_ref[...] = d
        mo_ref[...] = mn
        vo_ref[...] = vn

    return pl.pallas_call(
        body, name=name, out_shape=[jax.ShapeDtypeStruct((r, c), F32)] * 4,
        compiler_params=pltpu.CompilerParams(vmem_limit_bytes=VMEM_LIMIT),
    )(gathered, w, m, v)


SMALL_UNIT = 1024


def _pack(parts):
    flat = []
    for p in parts:
        f = p.reshape(-1).astype(F32)
        pad = (-f.shape[0]) % SMALL_UNIT
        flat.append(jnp.pad(f, (0, pad)))
    return jnp.concatenate(flat).reshape(-1, 128)


def _unpack(buf, shapes):
    flat = buf.reshape(-1)
    out, off = [], 0
    for s in shapes:
        nel = math.prod(s)
        out.append(flat[off:off + nel].reshape(s))
        off += nel + ((-nel) % SMALL_UNIT)
    return out


def kernel(x, p, ffn1_w_in, ffn1_w_out, ln1_g, ln1_b, mix_w_in, conv_w, conv_b, conv_w_out, ssm_lam_re, ssm_lam_im, ssm_log_step, ssm_b_re, ssm_b_im, ssm_c_re, ssm_c_im, ssm_d, ssm_w_glu, mix_w_out, ln2_g, ln2_b, ffn2_w_in, ffn2_w_out, ln3_g, ln3_b, ple_w_in, ple_w_gate, ln4_g, ln4_b, loss_target, m_ffn1_w_in, m_ffn1_w_out, m_ln1_g, m_ln1_b, m_mix_w_in, m_conv_w, m_conv_b, m_conv_w_out, m_ssm_lam_re, m_ssm_lam_im, m_ssm_log_step, m_ssm_b_re, m_ssm_b_im, m_ssm_c_re, m_ssm_c_im, m_ssm_d, m_ssm_w_glu, m_mix_w_out, m_ln2_g, m_ln2_b, m_ffn2_w_in, m_ffn2_w_out, m_ln3_g, m_ln3_b, m_ple_w_in, m_ple_w_gate, m_ln4_g, m_ln4_b, v_ffn1_w_in, v_ffn1_w_out, v_ln1_g, v_ln1_b, v_mix_w_in, v_conv_w, v_conv_b, v_conv_w_out, v_ssm_lam_re, v_ssm_lam_im, v_ssm_log_step, v_ssm_b_re, v_ssm_b_im, v_ssm_c_re, v_ssm_c_im, v_ssm_d, v_ssm_w_glu, v_mix_w_out, v_ln2_g, v_ln2_b, v_ffn2_w_in, v_ffn2_w_out, v_ln3_g, v_ln3_b, v_ple_w_in, v_ple_w_gate, v_ln4_g, v_ln4_b):
    args = dict(locals())
    big = ['ffn1_w_in', 'ffn1_w_out', 'mix_w_in', 'conv_w_out', 'ssm_w_glu', 'mix_w_out',
           'ffn2_w_in', 'ffn2_w_out', 'ple_w_in', 'ple_w_gate']
    small = ['ln1_g', 'ln1_b', 'conv_b', 'ssm_lam_re', 'ssm_lam_im', 'ssm_log_step', 'ssm_b_re', 'ssm_b_im',
             'ssm_c_re', 'ssm_c_im', 'ssm_d', 'ln2_g', 'ln2_b', 'ln3_g', 'ln3_b', 'ln4_g', 'ln4_b']
    order = ['ffn1_w_in', 'ffn1_w_out', 'ln1_g', 'ln1_b', 'mix_w_in', 'conv_w', 'conv_b', 'conv_w_out',
             'ssm_lam_re', 'ssm_lam_im', 'ssm_log_step', 'ssm_b_re', 'ssm_b_im', 'ssm_c_re', 'ssm_c_im', 'ssm_d',
             'ssm_w_glu', 'mix_w_out', 'ln2_g', 'ln2_b', 'ffn2_w_in', 'ffn2_w_out', 'ln3_g', 'ln3_b',
             'ple_w_in', 'ple_w_gate', 'ln4_g', 'ln4_b']

    t = x.shape[1]
    d = x.shape[2]
    xc_, yc_, cc_ = lax.axis_index("x"), lax.axis_index("y"), lax.axis_index("c")
    me = (4 * xc_ + 2 * yc_ + cc_).astype(jnp.int32)
    cw_cols = conv_w.shape[2]

    turned = ('ffn1_w_in', 'ffn2_w_in')

    def local(a, nm):
        return jnp.swapaxes(a[0], 0, 1) if nm in turned else a[0]

    shard = {nm: local(args[nm], nm).astype(BF16) for nm in big}
    cw_pad = jnp.zeros((8, 128), F32).at[0:3, 0:cw_cols].set(conv_w[0])
    wf = shard['ffn1_w_in'].shape[0]

    def gather(*names):
        return _gather_plan([shard[nm] for nm in names])

    ((_, (w1in, cw_g), _),) = _run_plans([_gather_plan([shard['ffn1_w_in'], cw_pad])], name="gather_ffn1_in")
    cw_full = jnp.transpose(cw_g[:, 0:3, 0:cw_cols], (1, 0, 2)).reshape(3, N_DEV * cw_cols)
    cw8 = jnp.zeros((8, CONV_CH), F32).at[0:3, :].set(cw_full)

    s5_in = (ssm_lam_re[0], ssm_lam_im[0], ssm_log_step[0], ssm_b_re[0], ssm_b_im[0])
    (a_re, a_im, bb_re, bb_im), s5_vjp = jax.vjp(_s5_discretise, *s5_in)
    tab_f = _pow_table(a_re, a_im)
    tab_b = _pow_table(a_re, -a_im, descending=True)
    bmat_b = _compact(jnp.transpose(bb_re, (0, 2, 1)), jnp.transpose(bb_im, (0, 2, 1))).astype(BF16)
    cmat_tb = _compact(ssm_c_re[0], -ssm_c_im[0]).astype(BF16)
    dvec = ssm_d[0].reshape(1, SSM_W)

    xf = x[0]
    x_b = xf.astype(BF16)
    p_b = p[0, 0].astype(BF16)
    tgt = loss_target[0]
    tq = min(512, t)

    ffn_out = dict(ja='c', jb='c', nj=4, tm=tq, tn=d, tk=wf)
    a1, h1, ((w1out_g,),) = _ffn_in(x_b, w1in, name="ffn1_in", tm=tq, plans=[gather('ffn1_w_out')])
    w1out = w1out_g.reshape(4, wf, d)
    f1, ((wmix,),) = _mm(a1, w1out, name="ffn1_out", **ffn_out, plans=[gather('mix_w_in')])
    x1, x1b = _ln_fwd(xf, [f1], ln1_g, ln1_b, name="ffn1_ln", fs=0.5)
    proj, ((wco, wglu, wmo_g),) = _mm(x1b, wmix, name="mix_in", jb='b', jo='b', o_flat=True, nj=8, tm=tq, tn=512,
                                      tk=d, plans=[gather('conv_w_out', 'ssm_w_glu', 'mix_w_out')])
    wmo = wmo_g.reshape(d, d)
    ycin = _conv_fwd(proj, cw8, conv_b, name="conv_fwd")
    yconv = _mm(ycin, wco, name="conv_out", jb='b', jo='b', o_flat=True, nj=8, tm=tq, tn=128, tk=CONV_CH)
    s_f, s_b, ((w2in,),) = _scan_fwd(proj, bmat_b, tab_f, name="scan_fwd", plans=[gather('ffn2_w_in')])
    blk = dict(ja='b', jb='b', jo='b', nj=SCAN_NCB)
    ymm = _mm(s_b, cmat_tb, name="ssm_read", a_flat=True, o_flat=True, tb=True, tm=tq, tn=SCAN_UW, tk=2 * SCAN_CB, **blk)
    ys, sg, u_b = _s5_out(ymm, proj, dvec, name="ssm_out")
    glu, ((w2out_g,),) = _mm(sg, wglu, name="glu_in", jb='b', jo='b', o_flat=True, nj=8, tm=tq, tn=256, tk=SSM_W,
                             plans=[gather('ffn2_w_out')])
    w2out = w2out_g.reshape(4, wf, d)
    merged = _gate_fwd(yconv, glu, proj, name="gate_fwd")
    mix = _mm(merged, wmo, name="mix_out", tm=tq, tn=d, tk=d)
    x2, x2b = _ln_fwd(x1, [mix], ln2_g, ln2_b, name="mix_ln")
    a2, h2, ((wpin, wgate_g),) = _ffn_in(x2b, w2in, name="ffn2_in", tm=tq, plans=[gather('ple_w_in', 'ple_w_gate')])
    wgate = wgate_g.reshape(d, d)
    f2 = _mm(a2, w2out, name="ffn2_out", **ffn_out)
    x3, x3b = _ln_fwd(x2, [f2], ln3_g, ln3_b, name="ffn2_ln", fs=0.5)
    pe = _mm(p_b, wpin, name="ple_in", jb='b', jo='b', o_flat=True, nj=8, tm=tq, tn=128, tk=p_b.shape[1])
    gp = _mm(x3b, wgate, name="ple_gate", tm=tq, tn=d, tk=d)

    dr4, dpe_b, dgp_b, dg4, db4, loss_part = _ln_bwd(x3, [pe, gp], ln4_g, ln4_b, [], name="ple_ln_bwd",
                                                     ple=True, target=tgt)
    gb, sib, rem = {}, {}, {}
    ids = jnp.stack([_dev(1 - xc_, yc_, cc_), _dev(xc_, 1 - yc_, cc_), _dev(1 - xc_, 1 - yc_, cc_)]).astype(jnp.int32)

    def blocked(nm, g):
        return g.reshape((N_DEV,) + args[nm].shape[1:])

    def to_sibling(*names):
        return _sibling_plan([gb[nm] for nm in names])

    def chip_sums(names, sibs):
        for nm, s in zip(names, sibs):
            sib[nm] = s
        return [_chip_partial(gb[nm], sib[nm], ids, name=f"chip_sum_{nm}") for nm in names]

    ffn_in_dg = dict(ja='c', jb='c', nj=8, tm=tq, tn=d, tk=wf)
    ffn_in_wg = dict(ja='b', jo='b', ta=True, nj=8, tm=wf, tn=d, tk=t)
    ffn_out_wg = dict(ja='b', jo='b', ta=True, nj=4, tm=wf, tn=d, tk=t)

    gb['ple_w_in'] = _mm(p_b, dpe_b, name="ple_in_wg", jb='b', jo='b', b_flat=True, ta=True, nj=8,
                         tm=p_b.shape[1], tn=128, tk=t)
    gb['ple_w_gate'] = blocked('ple_w_gate', _mm(x3b, dgp_b, name="ple_gate_wg", ta=True, tm=d, tn=d, tk=t))
    g_ple = ['ple_w_in', 'ple_w_gate']
    dx3_g, (s_,) = _mm(dgp_b, wgate, name="ple_gate_dg", tb=True, tm=tq, tn=d, tk=d, plans=[to_sibling(*g_ple)])
    part = chip_sums(g_ple, s_)

    dr3, df2_b, dg3, db3 = _ln_bwd(x2, [f2], ln3_g, ln3_b, [(dr4, ALPHA), (dx3_g, 1.0)], name="ffn2_ln_bwd", fs=0.5)
    dh2, _ = _ffn_out_dg(df2_b, w2out, h2, name="ffn2_out_dg", tm=tq)
    g_, (r_,) = _mm(a2, df2_b, name="ffn2_out_wg", **ffn_out_wg, plans=[_chip_plan(part)])
    gb['ffn2_w_out'] = blocked('ffn2_w_out', g_)
    rem['ple_w_in'], rem['ple_w_gate'] = r_[0:3], r_[3:6]
    dx2_f, (s_,) = _mm(dh2, w2in, name="ffn2_in_dg", **ffn_in_dg, plans=[to_sibling('ffn2_w_out')])
    part = chip_sums(['ffn2_w_out'], s_)
    gb['ffn2_w_in'], (r_,) = _mm(dh2, x2b, name="ffn2_in_wg", **ffn_in_wg, plans=[_chip_plan(part)])
    rem['ffn2_w_out'] = r_

    dr2, dmix_b, dg2, db2 = _ln_bwd(x1, [mix], ln2_g, ln2_b, [(dr3, ALPHA), (dx2_f, 1.0)], name="mix_ln_bwd")
    dmerged, (s_,) = _mm(dmix_b, wmo, name="mix_out_dg", tb=True, tm=tq, tn=d, tk=d, plans=[to_sibling('ffn2_w_in')])
    part = chip_sums(['ffn2_w_in'], s_)
    gb['mix_w_out'] = blocked('mix_w_out', _mm(merged, dmix_b, name="mix_out_wg", ta=True, tm=d, tn=d, tk=t))
    dyconv_b, dglu_b, dgate_b = _gate_bwd(dmerged, yconv, glu, proj, name="gate_bwd")
    gb['conv_w_out'] = _mm(ycin, dyconv_b, name="conv_out_wg", jb='b', jo='b', b_flat=True, ta=True, nj=8,
                           tm=CONV_CH, tn=128, tk=t)
    dycin = _mm(dyconv_b, wco, name="conv_out_dg", ja='c', jb='c', a_flat=True, tb=True, nj=8,
                tm=tq, tn=CONV_CH, tk=128)
    gb['ssm_w_glu'] = _mm(sg, dglu_b, name="glu_in_wg", jb='b', jo='b', b_flat=True, ta=True, nj=8,
                          tm=SSM_W, tn=256, tk=t)
    g_mix = ['mix_w_out', 'conv_w_out', 'ssm_w_glu']
    dsg, (s_,) = _mm(dglu_b, wglu, name="glu_in_dg", ja='c', jb='c', a_flat=True, tb=True, nj=8,
                     tm=tq, tn=SSM_W, tk=256, plans=[to_sibling(*g_mix)])
    part_mix = chip_sums(g_mix, s_)
    dys, dys_b, dd = _s5_bwd_in(dsg, ys, proj, name="ssm_out_bwd")
    h_b, da_acc, (rem['ffn2_w_in'],) = _scan_bwd(dys_b, cmat_tb, s_f, tab_b, name="scan_bwd", plans=[_chip_plan(part)])
    dumm = _mm(h_b, bmat_b, name="ssm_write_dg", a_flat=True, o_flat=True, tb=True, tm=tq, tn=SCAN_UW,
               tk=2 * SCAN_CB, **blk)
    du_b = _s5_du(dumm, dys, dvec, name="ssm_du")
    g_bmat = _mm(u_b, h_b, name="ssm_write_wg", a_flat=True, b_flat=True, ta=True, tm=SCAN_UW,
                 tn=2 * SCAN_CB, tk=t, **blk)
    g_cmat = _mm(dys_b, s_b, name="ssm_read_wg", a_flat=True, b_flat=True, ta=True, tm=SCAN_UW,
                 tn=2 * SCAN_CB, tk=t, **blk)
    dcb_b, dcc_b, dch_b, dconv = _conv_bwd(dycin, proj, cw8, conv_b, name="conv_bwd")
    dproj = jnp.concatenate([dcb_b, dcc_b, dch_b, du_b, dgate_b], axis=1)
    gb['mix_w_in'], (r_,) = _mm(x1b, dproj, name="mix_in_wg", jb='b', jo='b', b_flat=True, ta=True, nj=8,
                                tm=d, tn=512, tk=t, plans=[_chip_plan(part_mix)])
    for i, nm in enumerate(g_mix):
        rem[nm] = r_[3 * i:3 * i + 3]
    dx1_m, (s_,) = _mm(dproj, wmix, name="mix_in_dg", ja='c', jb='c', a_flat=True, tb=True, nj=8,
                       tm=tq, tn=d, tk=512, plans=[to_sibling('mix_w_in')])
    part = chip_sums(['mix_w_in'], s_)

    dr1, df1_b, dg1, db1 = _ln_bwd(xf, [f1], ln1_g, ln1_b, [(dr2, ALPHA), (dx1_m, 1.0)], name="ffn1_ln_bwd", fs=0.5)
    dh1, (rem['mix_w_in'],) = _ffn_out_dg(df1_b, w1out, h1, name="ffn1_out_dg", tm=tq, plans=[_chip_plan(part)])

    da_sum = jnp.sum(da_acc, axis=0)
    da_re, da_im = _unperm_cols(da_sum)
    gbb_re, gbb_im = [jnp.transpose(v, (0, 2, 1)) for v in _compact_extract(g_bmat, SSM_GROUP)]
    g_c_re, g_c_im_neg = _compact_extract(g_cmat, SSM_GROUP)
    g_c_im = -g_c_im_neg
    g_lam_re, g_lam_im, g_log_step, g_b_re, g_b_im = s5_vjp(
        (da_re.reshape(SSM_GROUPS, SSM_STATE), da_im.reshape(SSM_GROUPS, SSM_STATE), gbb_re, gbb_im))
    g_d = dd.reshape(SSM_GROUPS, SSM_GROUP)

    small_g = {'ln1_g': dg1, 'ln1_b': db1, 'conv_b': dconv[3:4], 'ssm_lam_re': g_lam_re, 'ssm_lam_im': g_lam_im,
               'ssm_log_step': g_log_step, 'ssm_b_re': g_b_re, 'ssm_b_im': g_b_im, 'ssm_c_re': g_c_re,
               'ssm_c_im': g_c_im, 'ssm_d': g_d, 'ln2_g': dg2, 'ln2_b': db2, 'ln3_g': dg3, 'ln3_b': db3,
               'ln4_g': dg4, 'ln4_b': db4}
    small_shapes = [args[nm].shape for nm in small] + [(3, CONV_CH), (1,)]
    g_pack = _pack([small_g[nm] for nm in small] + [dconv[0:3], loss_part[0:1, 0:1]])

    gb['ffn1_w_in'], ((g_all,),) = _mm(dh1, x_b, name="ffn1_in_wg", **ffn_in_wg, plans=[_gather_plan([g_pack])])
    g_, (s_,) = _mm(a1, df1_b, name="ffn1_out_wg", **ffn_out_wg, plans=[to_sibling('ffn1_w_in')])
    gb['ffn1_w_out'] = blocked('ffn1_w_out', g_)
    part = chip_sums(['ffn1_w_in'], s_)
    dx_f, (r_, s_) = _mm(dh1, w1in, name="ffn1_in_dg", **ffn_in_dg, plans=[_chip_plan(part), to_sibling('ffn1_w_out')])
    rem['ffn1_w_in'] = r_
    part = chip_sums(['ffn1_w_out'], s_)
    grad_x = _axpy(dr1, ALPHA, dx_f, name="grad_x")
    ((_, rem['ffn1_w_out'], _),) = _run_plans([_chip_plan(part)], name="grad_chips_ffn1_out")

    def full_cw(a):
        return lax.dynamic_update_slice(jnp.zeros((3, CONV_CH), F32), a[0], (0, me * cw_cols))

    zero1 = jnp.zeros((1,), F32)
    w_pack = _pack([args[nm] for nm in small] + [full_cw(conv_w), zero1])
    m_pack = _pack([args['m_' + nm] for nm in small] + [full_cw(m_conv_w), zero1])
    v_pack = _pack([args['v_' + nm] for nm in small] + [full_cw(v_conv_w), zero1])
    sg_sum, sd, sm, sv = _small_update(g_all, w_pack, m_pack, v_pack, name="small_update")
    res = {}
    for key, buf in (('grad_', sg_sum), ('delta_', sd), ('new_m_', sm), ('new_v_', sv)):
        parts = _unpack(buf, small_shapes)
        for nm, val in zip(small, parts[:len(small)]):
            res[key + nm] = val
        res[key + 'conv_w'] = lax.dynamic_slice(parts[len(small)], (0, me * cw_cols), (3, cw_cols))[None]
        if key == 'grad_':
            loss = parts[-1][0]

    me1 = me.reshape(1)
    for nm in big:
        upd = _shard_update(gb[nm], sib[nm], rem[nm], me1, local(args[nm], nm), local(args['m_' + nm], nm),
                            local(args['v_' + nm], nm), name=f"update_{nm}")
        for key, val in zip(('grad_', 'delta_', 'new_m_', 'new_v_'), upd):
            res[key + nm] = (jnp.swapaxes(val, 0, 1) if nm in turned else val)[None]

    outs = [loss, grad_x[None]]
    for key in ('grad_', 'delta_', 'new_m_', 'new_v_'):
        outs += [res[key + nm] for nm in order]
    return tuple(outs)
```

```python
import functools
import math

import jax
import jax.numpy as jnp
from jax import lax
from jax.experimental import pallas as pl
from jax.experimental.pallas import tpu as pltpu

F32 = jnp.float32
BF16 = jnp.bfloat16
MESH = pl.DeviceIdType.MESH

N_DEV = 8
ALPHA = 2.0 ** 0.25
LN_EPS = 1e-5
CONV_CH = 512
SSM_W = 512
SSM_GROUPS = 32
SSM_GROUP = 16
SSM_STATE = 64
SSM_CH = SSM_GROUPS * SSM_STATE
SCAN_CB = 512
SCAN_NCB = SSM_CH // SCAN_CB
SCAN_GPB = SSM_GROUPS // SCAN_NCB
SCAN_UW = SCAN_GPB * SSM_GROUP
SCAN_TT = 256
ADAM_LR = 0.001
ADAM_B1 = 0.9
ADAM_B2 = 0.999
ADAM_EPS = 1e-08
ADAM_WD = 0.01
ADAM_STEP = 10
VMEM_LIMIT = 56 * 1024 * 1024


def _cparams(sem=None, **kw):
    return pltpu.CompilerParams(dimension_semantics=sem, vmem_limit_bytes=VMEM_LIMIT, **kw)


def _mm(a, b, *, name, ja=None, jb=None, jo=None, a_flat=False, b_flat=False, o_flat=False,
        ta=False, tb=False, tm, tn, tk, nj=1, out_dtype=F32, plans=(), epilogue=None):
    def dims(arr, j, flat):
        if j is None:
            return arr.shape
        if flat:
            return (arr.shape[0], arr.shape[1] // nj)
        assert arr.shape[0] == nj, (name, arr.shape, nj)
        return arr.shape[1:]

    ar, ac = dims(a, ja, a_flat)
    br, bc = dims(b, jb, b_flat)
    m, k = (ac, ar) if ta else (ar, ac)
    k2, n = (bc, br) if tb else (br, bc)
    assert k == k2, (name, a.shape, b.shape)
    assert m % tm == 0 and n % tn == 0 and k % tk == 0, (name, m, n, k, tm, tn, tk)
    njb = nj if 'b' in (ja, jb) else 1
    njc = nj if 'c' in (ja, jb) else 1
    nk = k // tk
    j_inside = njc > 1 and nk == 1 and not ta
    n_in = njc if j_inside else 1
    nred = nk if j_inside else njc * nk
    grid = (njb, m // tm, n // tn, 1 if j_inside else njc, nk)

    def make_spec(j, flat, blk, rfn, cfn, cols_per_j):
        def jsel(g, c):
            return g if j == 'b' else c
        if j is None:
            return pl.BlockSpec(blk, lambda g, i, jn, c, kk: (rfn(i, jn, kk), cfn(i, jn, kk)))
        if j == 'c' and j_inside:
            if flat:
                return pl.BlockSpec((blk[0], nj * blk[1]), lambda g, i, jn, c, kk: (rfn(i, jn, kk), 0))
            return pl.BlockSpec((nj,) + blk, lambda g, i, jn, c, kk: (0, rfn(i, jn, kk), cfn(i, jn, kk)))
        if flat:
            nb = cols_per_j // blk[1]
            return pl.BlockSpec(blk, lambda g, i, jn, c, kk: (rfn(i, jn, kk), jsel(g, c) * nb + cfn(i, jn, kk)))
        return pl.BlockSpec((None,) + blk,
                            lambda g, i, jn, c, kk: (jsel(g, c), rfn(i, jn, kk), cfn(i, jn, kk)))

    if ta:
        a_spec = make_spec(ja, a_flat, (tk, tm), lambda i, jn, kk: kk, lambda i, jn, kk: i, ac)
    else:
        a_spec = make_spec(ja, a_flat, (tm, tk), lambda i, jn, kk: i, lambda i, jn, kk: kk, ac)
    if tb:
        b_spec = make_spec(jb, b_flat, (tn, tk), lambda i, jn, kk: jn, lambda i, jn, kk: kk, bc)
    else:
        b_spec = make_spec(jb, b_flat, (tk, tn), lambda i, jn, kk: kk, lambda i, jn, kk: jn, bc)
    o_spec = make_spec(jo, o_flat, (tm, tn), lambda i, jn, kk: i, lambda i, jn, kk: jn, n)
    if jo is None:
        out_shape = (m, n)
    elif o_flat:
        out_shape = (m, nj * n)
    else:
        out_shape = (nj, m, n)

    dn = (((0 if ta else 1,), (1 if tb else 0,)), ((), ()))

    def operand(ref, j, flat, jj, width):
        if not (j == 'c' and j_inside):
            return ref[...]
        return ref[:, jj * width:(jj + 1) * width] if flat else ref[jj]

    e_fn, e_rows, e_vecs, e_dtypes = epilogue if epilogue else (None, (), (), (out_dtype,))
    assert not epilogue or (nred == 1 and jo is None), name
    n_e = len(e_rows) + len(e_vecs)
    n_o = len(e_dtypes)

    def body(a_ref, b_ref, *rest):
        e_refs, o_refs, scratch = rest[:n_e], rest[n_e:n_e + n_o], rest[n_e + n_o:]
        o_ref = o_refs[0]
        p = None
        for jj in range(n_in):
            q = lax.dot_general(operand(a_ref, ja, a_flat, jj, tk), operand(b_ref, jb, b_flat, jj, tk if tb else tn),
                                dn, preferred_element_type=F32)
            p = q if p is None else p + q
        if epilogue:
            for ref, val in zip(o_refs, e_fn(p, *[r[...] for r in e_refs])):
                ref[...] = val.astype(ref.dtype)
        elif nred == 1:
            o_ref[...] = p.astype(o_ref.dtype)
        else:
            acc = scratch[0]
            r = pl.program_id(3) * nk + pl.program_id(4)

            @pl.when(r == 0)
            def _():
                acc[...] = p

            @pl.when(r > 0)
            def _():
                acc[...] += p

            @pl.when(r == nred - 1)
            def _():
                o_ref[...] = acc[...].astype(o_ref.dtype)

    vec_spec = pl.BlockSpec((1, tn), lambda g, i, jn, c, kk: (0, jn))
    res = _call_with_plans(
        body, plans, name=name, grid=grid,
        in_specs=[a_spec, b_spec] + [o_spec] * len(e_rows) + [vec_spec] * len(e_vecs), out_specs=[o_spec] * n_o,
        out_shape=[jax.ShapeDtypeStruct(out_shape, dt) for dt in e_dtypes],
        scratch_shapes=[] if nred == 1 else [pltpu.VMEM((tm, tn), F32)],
        semantics=("parallel", "parallel", "parallel", "arbitrary", "arbitrary"), operands=(a, b, *e_rows, *e_vecs))
    outs = res[0] if epilogue else res[0][0]
    return (outs, res[1]) if plans else outs


def _sigmoid(v):
    return 1.0 / (1.0 + jnp.exp(-v))


def _row_spec(tm, cols, colblk=0):
    return pl.BlockSpec((tm, cols), lambda i: (i, colblk))


def _vec_spec(cols):
    return pl.BlockSpec((1, cols), lambda i: (0, 0))


def _ffn_in(xb, win, *, name, tm, plans=()):
    t, d = xb.shape
    nj, w, _ = win.shape
    half = nj // 2
    dn = (((1,), (1,)), ((), ()))

    def body(x_ref, wg_ref, wu_ref, a_ref, gu_ref):
        xv = x_ref[...]
        g = lax.dot_general(xv, wg_ref[...], dn, preferred_element_type=F32)
        u = lax.dot_general(xv, wu_ref[...], dn, preferred_element_type=F32)
        a_ref[...] = (g * _sigmoid(g) * u).astype(BF16)
        gu_ref[0] = g.astype(BF16)
        gu_ref[1] = u.astype(BF16)

    (a, gu), riders = _call_with_plans(
        body, plans, name=name, grid=(half, t // tm),
        in_specs=[pl.BlockSpec((tm, d), lambda j, i: (i, 0)),
                  pl.BlockSpec((None, w, d), lambda j, i: (j, 0, 0)),
                  pl.BlockSpec((None, w, d), lambda j, i: (j + half, 0, 0))],
        out_specs=[pl.BlockSpec((None, tm, w), lambda j, i: (j, i, 0)),
                   pl.BlockSpec((2, None, tm, w), lambda j, i: (0, j, i, 0))],
        out_shape=[jax.ShapeDtypeStruct((half, t, w), BF16), jax.ShapeDtypeStruct((2, half, t, w), BF16)],
        scratch_shapes=[], semantics=("parallel", "parallel"), operands=(xb, win, win))
    return a, gu, riders


def _ffn_out_dg(dfb, wout, gu, *, name, tm, plans=()):
    t, d = dfb.shape
    half, w, _ = wout.shape
    dn = (((1,), (1,)), ((), ()))

    def body(df_ref, w_ref, gu_ref, dh_ref):
        da = lax.dot_general(df_ref[...], w_ref[...], dn, preferred_element_type=F32)
        g = gu_ref[0].astype(F32)
        u = gu_ref[1].astype(F32)
        sg = _sigmoid(g)
        dh_ref[0] = (da * u * (sg * (1.0 + g * (1.0 - sg)))).astype(BF16)
        dh_ref[1] = (da * (g * sg)).astype(BF16)

    (out,), riders = _call_with_plans(
        body, plans, name=name, grid=(half, t // tm),
        in_specs=[pl.BlockSpec((tm, d), lambda j, i: (i, 0)),
                  pl.BlockSpec((None, w, d), lambda j, i: (j, 0, 0)),
                  pl.BlockSpec((2, None, tm, w), lambda j, i: (0, j, i, 0))],
        out_specs=[pl.BlockSpec((2, None, tm, w), lambda j, i: (0, j, i, 0))],
        out_shape=[jax.ShapeDtypeStruct((2, half, t, w), BF16)],
        scratch_shapes=[], semantics=("parallel", "parallel"), operands=(dfb, wout, gu))
    return out.reshape(2 * half, t, w), riders


def _ln_stats(r):
    mu = jnp.mean(r, axis=-1, keepdims=True)
    xc = r - mu
    var = jnp.mean(xc * xc, axis=-1, keepdims=True)
    rstd = lax.rsqrt(var + LN_EPS)
    return xc * rstd, rstd


def _ln_epilogue(xin, gamma, beta, fs):
    def fn(p, xv, g, b):
        r = ALPHA * xv + fs * p
        xh, _ = _ln_stats(r)
        y = xh * g + b
        return r, y, y

    return fn, (xin,), (gamma, beta), (F32, F32, BF16)


def _ln_bwd(xin, fparts, gamma, beta, grads, *, name, fs=1.0, ple=False, target=None, tm=256):
    t, d = xin.shape
    nf = len(fparts)
    ng = len(grads)
    coefs = [c for _, c in grads]
    use_t = target is not None
    n_fout = 2 if ple else 1

    def body(*refs):
        pos = 0
        x_ref = refs[pos]; pos += 1
        f_refs = refs[pos:pos + nf]; pos += nf
        g_ref, b_ref = refs[pos:pos + 2]; pos += 2
        gr_refs = refs[pos:pos + ng]; pos += ng
        if use_t:
            t_ref = refs[pos]; pos += 1
        dr_ref = refs[pos]; pos += 1
        fo_refs = refs[pos:pos + n_fout]; pos += n_fout
        dg_ref, db_ref = refs[pos:pos + 2]; pos += 2
        if use_t:
            loss_ref = refs[pos]; pos += 1
        i = pl.program_id(0)

        if ple:
            pe = f_refs[0][...]
            sg = _sigmoid(f_refs[1][...])
            resid = ALPHA * x_ref[...] + pe * sg
        else:
            resid = x_ref[...]
        xh, rstd = _ln_stats(resid)
        gam = g_ref[...]
        if use_t:
            diff = xh * gam + b_ref[...] - t_ref[...]
            dy = diff * (1.0 / d)
            lpart = 0.5 * jnp.sum(jnp.sum(diff * diff, axis=-1, keepdims=True), axis=0, keepdims=True) * (1.0 / d)
        else:
            dy = coefs[0] * gr_refs[0][...]
            for c, r in zip(coefs[1:], gr_refs[1:]):
                dy = dy + c * r[...]
        dxh = dy * gam
        m1 = jnp.mean(dxh, axis=-1, keepdims=True)
        m2 = jnp.mean(dxh * xh, axis=-1, keepdims=True)
        dr = rstd * (dxh - m1 - xh * m2)
        dr_ref[...] = dr
        if ple:
            fo_refs[0][...] = (dr * sg).astype(BF16)
            fo_refs[1][...] = (dr * pe * (sg * (1.0 - sg))).astype(BF16)
        else:
            fo_refs[0][...] = (fs * dr).astype(BF16)
        dgp = jnp.sum(dy * xh, axis=0, keepdims=True)
        dbp = jnp.sum(dy, axis=0, keepdims=True)

        @pl.when(i == 0)
        def _():
            dg_ref[...] = dgp
            db_ref[...] = dbp
            if use_t:
                loss_ref[...] = jnp.broadcast_to(lpart, loss_ref.shape)

        @pl.when(i > 0)
        def _():
            dg_ref[...] += dgp
            db_ref[...] += dbp
            if use_t:
                loss_ref[...] += jnp.broadcast_to(lpart, loss_ref.shape)

    ins = [xin, *fparts, gamma, beta, *[g for g, _ in grads]] + ([target] if use_t else [])
    in_specs = ([_row_spec(tm, d)] * (1 + nf) + [_vec_spec(d), _vec_spec(d)] + [_row_spec(tm, d)] * ng
                + ([_row_spec(tm, d)] if use_t else []))
    out_specs = [_row_spec(tm, d)] * (1 + n_fout) + [_vec_spec(d), _vec_spec(d)] + ([_vec_spec(128)] if use_t else [])
    out_shape = ([jax.ShapeDtypeStruct((t, d), F32)] + [jax.ShapeDtypeStruct((t, d), BF16)] * n_fout
                 + [jax.ShapeDtypeStruct((1, d), F32)] * 2 + ([jax.ShapeDtypeStruct((1, 128), F32)] if use_t else []))
    return pl.pallas_call(
        body, name=name, grid=(t // tm,), in_specs=in_specs, out_specs=out_specs, out_shape=out_shape,
        compiler_params=_cparams(("arbitrary",)),
    )(*ins)


def _conv_fwd(proj, cw, cb, *, name, tm=256):
    t = proj.shape[0]
    c = CONV_CH
    hb = tm // 8

    def body(b_ref, c_ref, h_ref, cp_ref, hp_ref, w_ref, bias_ref, o_ref, q_scr):
        i = pl.program_id(0)
        q = c_ref[...] * h_ref[...]
        halo = jnp.where(i > 0, cp_ref[...] * hp_ref[...], 0.0)
        q_scr[0:8, :] = halo
        q_scr[8:, :] = q
        z = (w_ref[2:3, :] * q + w_ref[1:2, :] * q_scr[pl.ds(7, tm), :] + w_ref[0:1, :] * q_scr[pl.ds(6, tm), :]
             + bias_ref[...])
        o_ref[...] = (b_ref[...] * z).astype(BF16)

    prev = lambda blk: pl.BlockSpec((8, c), lambda i: (jnp.maximum(i * hb - 1, 0), blk))
    return pl.pallas_call(
        body, name=name, grid=(t // tm,),
        in_specs=[_row_spec(tm, c, 0), _row_spec(tm, c, 1), _row_spec(tm, c, 2), prev(1), prev(2),
                  pl.BlockSpec((8, c), lambda i: (0, 0)), _vec_spec(c)],
        out_specs=_row_spec(tm, c),
        out_shape=jax.ShapeDtypeStruct((t, c), BF16),
        scratch_shapes=[pltpu.VMEM((tm + 8, c), F32)],
        compiler_params=_cparams(("parallel",)),
    )(proj, proj, proj, proj, proj, cw, cb)


def _conv_bwd(dyc, proj, cw, cb, *, name, tm=256):
    t = proj.shape[0]
    c = CONV_CH
    hb = tm // 8
    nblk = t // 8

    def body(d_ref, b_ref, c_ref, h_ref, cp_ref, hp_ref, dn_ref, bn_ref, w_ref, bias_ref,
             db_ref, dc_ref, dh_ref, dw_ref, q_scr, z_scr):
        i = pl.program_id(0)
        last = pl.num_programs(0) - 1
        cc = c_ref[...]
        ch = h_ref[...]
        q = cc * ch
        halo = jnp.where(i > 0, cp_ref[...] * hp_ref[...], 0.0)
        q_scr[0:8, :] = halo
        q_scr[8:, :] = q
        w0, w1, w2 = w_ref[0:1, :], w_ref[1:2, :], w_ref[2:3, :]
        qm1 = q_scr[pl.ds(7, tm), :]
        qm2 = q_scr[pl.ds(6, tm), :]
        z = w2 * q + w1 * qm1 + w0 * qm2 + bias_ref[...]
        d = d_ref[...]
        bb = b_ref[...]
        db_ref[...] = (d * z).astype(BF16)
        dz = d * bb
        z_scr[0:tm, :] = dz
        z_scr[tm:, :] = jnp.where(i < last, dn_ref[...] * bn_ref[...], 0.0)
        dq = w2 * dz + w1 * z_scr[pl.ds(1, tm), :] + w0 * z_scr[pl.ds(2, tm), :]
        dc_ref[...] = (dq * ch).astype(BF16)
        dh_ref[...] = (dq * cc).astype(BF16)
        row = lax.broadcasted_iota(jnp.int32, (8, c), 0)
        part = jnp.zeros((8, c), F32)
        for k, term in enumerate((dz * qm2, dz * qm1, dz * q, dz)):
            part = jnp.where(row == k, jnp.sum(term, axis=0, keepdims=True), part)

        @pl.when(i == 0)
        def _():
            dw_ref[...] = part

        @pl.when(i > 0)
        def _():
            dw_ref[...] += part

    prev = lambda blk: pl.BlockSpec((8, c), lambda i: (jnp.maximum(i * hb - 1, 0), blk))
    nxt_p = pl.BlockSpec((8, c), lambda i: (jnp.minimum((i + 1) * hb, nblk - 1), 0))
    nxt_d = pl.BlockSpec((8, c), lambda i: (jnp.minimum((i + 1) * hb, nblk - 1), 0))
    return pl.pallas_call(
        body, name=name, grid=(t // tm,),
        in_specs=[_row_spec(tm, c), _row_spec(tm, c, 0), _row_spec(tm, c, 1), _row_spec(tm, c, 2),
                  prev(1), prev(2), nxt_d, nxt_p, pl.BlockSpec((8, c), lambda i: (0, 0)), _vec_spec(c)],
        out_specs=[_row_spec(tm, c)] * 3 + [pl.BlockSpec((8, c), lambda i: (0, 0))],
        out_shape=[jax.ShapeDtypeStruct((t, c), BF16)] * 3 + [jax.ShapeDtypeStruct((8, c), F32)],
        scratch_shapes=[pltpu.VMEM((tm + 8, c), F32), pltpu.VMEM((tm + 8, c), F32)],
        compiler_params=_cparams(("arbitrary",)),
    )(dyc, proj, proj, proj, proj, proj, dyc, proj, cw, cb)


def _gate_fwd(yconv, glu, proj, *, name, tm=256):
    t, d = yconv.shape

    def body(yc_ref, ga_ref, gb_ref, gc_ref, gs_ref, o_ref):
        yssm = ga_ref[...] * _sigmoid(gb_ref[...])
        o_ref[...] = (_sigmoid(gc_ref[...]) * yc_ref[...] + _sigmoid(gs_ref[...]) * yssm).astype(BF16)

    return pl.pallas_call(
        body, name=name, grid=(t // tm,),
        in_specs=[_row_spec(tm, d), _row_spec(tm, d, 0), _row_spec(tm, d, 1), _row_spec(tm, d, 2), _row_spec(tm, d, 3)],
        out_specs=_row_spec(tm, d), out_shape=jax.ShapeDtypeStruct((t, d), BF16),
        compiler_params=_cparams(("parallel",)),
    )(yconv, glu, glu, proj, proj)


def _gate_bwd(dm, yconv, glu, proj, *, name, tm=256):
    t, d = yconv.shape

    def body(dm_ref, yc_ref, ga_ref, gb_ref, gc_ref, gs_ref, dyc_ref, dglu_ref, dgate_ref):
        dmv = dm_ref[...]
        sc = _sigmoid(gc_ref[...])
        ss = _sigmoid(gs_ref[...])
        sb = _sigmoid(gb_ref[...])
        ga = ga_ref[...]
        yssm = ga * sb
        dyc_ref[...] = (dmv * sc).astype(BF16)
        dgate_ref[:, 0:d] = (dmv * yc_ref[...] * (sc * (1.0 - sc))).astype(BF16)
        dys = dmv * ss
        dgate_ref[:, d:2 * d] = (dmv * yssm * (ss * (1.0 - ss))).astype(BF16)
        dglu_ref[:, 0:d] = (dys * sb).astype(BF16)
        dglu_ref[:, d:2 * d] = (dys * ga * (sb * (1.0 - sb))).astype(BF16)

    return pl.pallas_call(
        body, name=name, grid=(t // tm,),
        in_specs=[_row_spec(tm, d), _row_spec(tm, d), _row_spec(tm, d, 0), _row_spec(tm, d, 1),
                  _row_spec(tm, d, 2), _row_spec(tm, d, 3)],
        out_specs=[_row_spec(tm, d), _row_spec(tm, 2 * d), _row_spec(tm, 2 * d)],
        out_shape=[jax.ShapeDtypeStruct((t, d), BF16), jax.ShapeDtypeStruct((t, 2 * d), BF16),
                   jax.ShapeDtypeStruct((t, 2 * d), BF16)],
        compiler_params=_cparams(("parallel",)),
    )(dm, yconv, glu, glu, proj, proj)


_GELU_C = math.sqrt(2.0 / math.pi)


def _gelu(v):
    return 0.5 * v * (1.0 + jnp.tanh(_GELU_C * (v + 0.044715 * v * v * v)))


def _gelu_grad(v):
    th = jnp.tanh(_GELU_C * (v + 0.044715 * v * v * v))
    return 0.5 * (1.0 + th) + 0.5 * v * (1.0 - th * th) * (_GELU_C * (1.0 + 3.0 * 0.044715 * v * v))


def _cmul(ar, ai, br, bi):
    return ar * br - ai * bi, ar * bi + ai * br


def _scan_fwd(proj, bmat, tab, *, name, plans=()):
    t = proj.shape[0]
    tt, cbw = SCAN_TT, SCAN_CB
    w2 = 2 * cbw

    def body(u_ref, b_ref, tab_ref, s_ref, sb_ref, bu_scr, carry):
        ti = pl.program_id(1)

        @pl.when(ti == 0)
        def _():
            carry[...] = jnp.zeros_like(carry)

        bu_scr[...] = jnp.dot(u_ref[...].astype(BF16), b_ref[...], preferred_element_type=F32)
        row = lax.broadcasted_iota(jnp.int32, (8, cbw), 0)

        def blk(bi, c):
            cr, ci = c
            r0 = pl.multiple_of(bi * 8, 8)
            xr = bu_scr[pl.ds(r0, 8), 0:cbw]
            xi = bu_scr[pl.ds(r0, 8), cbw:w2]
            for k, sh in enumerate((1, 2, 4)):
                kr = tab_ref[k:k + 1, 0:cbw]
                ki = tab_ref[k:k + 1, cbw:w2]
                sr = jnp.where(row >= sh, pltpu.roll(xr, sh, 0), 0.0)
                si = jnp.where(row >= sh, pltpu.roll(xi, sh, 0), 0.0)
                pr, pi = _cmul(kr, ki, sr, si)
                xr = xr + pr
                xi = xi + pi
            pr, pi = _cmul(tab_ref[8:16, 0:cbw], tab_ref[8:16, cbw:w2], cr, ci)
            xr = xr + pr
            xi = xi + pi
            s_ref[pl.ds(r0, 8), 0:cbw] = xr
            s_ref[pl.ds(r0, 8), cbw:w2] = xi
            return (jnp.broadcast_to(xr[7:8, :], (8, cbw)), jnp.broadcast_to(xi[7:8, :], (8, cbw)))

        cr, ci = lax.fori_loop(0, tt // 8, blk, (carry[:, 0:cbw], carry[:, cbw:w2]))
        carry[:, 0:cbw] = cr
        carry[:, cbw:w2] = ci
        sb_ref[...] = s_ref[...].astype(BF16)

    (s, sb), riders = _call_with_plans(
        body, plans, name=name, grid=(SCAN_NCB, t // tt),
        in_specs=[pl.BlockSpec((tt, SCAN_UW), lambda cb, ti: (ti, 3 * SCAN_NCB + cb)),
                  pl.BlockSpec((None, SCAN_UW, w2), lambda cb, ti: (cb, 0, 0)),
                  pl.BlockSpec((16, w2), lambda cb, ti: (0, cb))],
        out_specs=[pl.BlockSpec((tt, w2), lambda cb, ti: (ti, cb))] * 2,
        out_shape=[jax.ShapeDtypeStruct((t, 2 * SSM_CH), F32), jax.ShapeDtypeStruct((t, 2 * SSM_CH), BF16)],
        scratch_shapes=[pltpu.VMEM((tt, w2), F32), pltpu.VMEM((8, w2), F32)],
        semantics=("parallel", "arbitrary"), operands=(proj, bmat, tab))
    return s, sb, riders


def _scan_bwd(dyb, cmat_t, s, tabb, *, name, plans=()):
    t = s.shape[0]
    tt, cbw = SCAN_TT, SCAN_CB
    w2 = 2 * cbw
    nt = t // tt
    hb = tt // 8

    def body(dy_ref, c_ref, s_ref, sp_ref, tab_ref, h_ref, da_ref, g_scr, s_scr, carry):
        ti = pl.program_id(1)

        @pl.when(ti == 0)
        def _():
            carry[...] = jnp.zeros_like(carry)
            da_ref[...] = jnp.zeros_like(da_ref)

        g_scr[...] = jnp.dot(dy_ref[...], c_ref[...], preferred_element_type=F32)
        s_scr[0:8, :] = jnp.where(ti < nt - 1, sp_ref[...], 0.0)
        s_scr[8:, :] = s_ref[...]
        row = lax.broadcasted_iota(jnp.int32, (8, cbw), 0)

        def blk(k, c):
            cr, ci, ar, ai = c
            bi = hb - 1 - k
            r0 = pl.multiple_of(bi * 8, 8)
            xr = g_scr[pl.ds(r0, 8), 0:cbw]
            xi = g_scr[pl.ds(r0, 8), cbw:w2]
            for j, sh in enumerate((1, 2, 4)):
                kr = tab_ref[j:j + 1, 0:cbw]
                ki = tab_ref[j:j + 1, cbw:w2]
                sr = jnp.where(row < 8 - sh, pltpu.roll(xr, 8 - sh, 0), 0.0)
                si = jnp.where(row < 8 - sh, pltpu.roll(xi, 8 - sh, 0), 0.0)
                pr, pi = _cmul(kr, ki, sr, si)
                xr = xr + pr
                xi = xi + pi
            pr, pi = _cmul(tab_ref[8:16, 0:cbw], tab_ref[8:16, cbw:w2], cr, ci)
            xr = xr + pr
            xi = xi + pi
            h_ref[pl.ds(r0, 8), 0:cbw] = xr.astype(BF16)
            h_ref[pl.ds(r0, 8), cbw:w2] = xi.astype(BF16)
            pvr = s_scr[pl.ds(r0, 8), 0:cbw]
            pvi = s_scr[pl.ds(r0, 8), cbw:w2]
            cur_r = s_scr[pl.ds(r0 + 8, 8), 0:cbw]
            cur_i = s_scr[pl.ds(r0 + 8, 8), cbw:w2]
            spr = jnp.where(row == 0, jnp.broadcast_to(pvr[7:8, :], (8, cbw)), pltpu.roll(cur_r, 1, 0))
            spi = jnp.where(row == 0, jnp.broadcast_to(pvi[7:8, :], (8, cbw)), pltpu.roll(cur_i, 1, 0))
            ar = ar + spr * xr + spi * xi
            ai = ai + spr * xi - spi * xr
            return (jnp.broadcast_to(xr[0:1, :], (8, cbw)), jnp.broadcast_to(xi[0:1, :], (8, cbw)), ar, ai)

        z = jnp.zeros((8, cbw), F32)
        cr, ci, ar, ai = lax.fori_loop(0, hb, blk, (carry[:, 0:cbw], carry[:, cbw:w2], z, z))
        carry[:, 0:cbw] = cr
        carry[:, cbw:w2] = ci
        da_ref[:, 0:cbw] += ar
        da_ref[:, cbw:w2] += ai

    rt = lambda ti: nt - 1 - ti
    (h, da), riders = _call_with_plans(
        body, plans, name=name, grid=(SCAN_NCB, nt),
        in_specs=[pl.BlockSpec((tt, SCAN_UW), lambda cb, ti: (rt(ti), cb)),
                  pl.BlockSpec((None, SCAN_UW, w2), lambda cb, ti: (cb, 0, 0)),
                  pl.BlockSpec((tt, w2), lambda cb, ti: (rt(ti), cb)),
                  pl.BlockSpec((8, w2), lambda cb, ti: (jnp.maximum(rt(ti) * hb - 1, 0), cb)),
                  pl.BlockSpec((16, w2), lambda cb, ti: (0, cb))],
        out_specs=[pl.BlockSpec((tt, w2), lambda cb, ti: (rt(ti), cb)),
                   pl.BlockSpec((8, w2), lambda cb, ti: (0, cb))],
        out_shape=[jax.ShapeDtypeStruct((t, 2 * SSM_CH), BF16), jax.ShapeDtypeStruct((8, 2 * SSM_CH), F32)],
        scratch_shapes=[pltpu.VMEM((tt, w2), F32), pltpu.VMEM((tt + 8, w2), F32), pltpu.VMEM((8, w2), F32)],
        semantics=("parallel", "arbitrary"), operands=(dyb, cmat_t, s, s, tabb))
    return h, da, riders


def _s5_out(ymm, proj, dvec, *, name, tm=256):
    t, w = ymm.shape

    def body(y_ref, u_ref, d_ref, yo_ref, sg_ref, ub_ref):
        u = u_ref[...]
        y = y_ref[...] + d_ref[...] * u
        yo_ref[...] = y
        sg_ref[...] = _gelu(y).astype(BF16)
        ub_ref[...] = u.astype(BF16)

    return pl.pallas_call(
        body, name=name, grid=(t // tm,),
        in_specs=[_row_spec(tm, w), _row_spec(tm, w, 3), _vec_spec(w)],
        out_specs=[_row_spec(tm, w)] * 3,
        out_shape=[jax.ShapeDtypeStruct((t, w), F32), jax.ShapeDtypeStruct((t, w), BF16), jax.ShapeDtypeStruct((t, w), BF16)],
        compiler_params=_cparams(("parallel",)),
    )(ymm, proj, dvec)


def _s5_bwd_in(dsg, y, proj, *, name, tm=256):
    t, w = y.shape

    def body(d_ref, y_ref, u_ref, dy_ref, dyb_ref, dd_ref):
        i = pl.program_id(0)
        dy = d_ref[...] * _gelu_grad(y_ref[...])
        dy_ref[...] = dy
        dyb_ref[...] = dy.astype(BF16)
        part = jnp.sum(dy * u_ref[...], axis=0, keepdims=True)

        @pl.when(i == 0)
        def _():
            dd_ref[...] = part

        @pl.when(i > 0)
        def _():
            dd_ref[...] += part

    return pl.pallas_call(
        body, name=name, grid=(t // tm,),
        in_specs=[_row_spec(tm, w), _row_spec(tm, w), _row_spec(tm, w, 3)],
        out_specs=[_row_spec(tm, w), _row_spec(tm, w), _vec_spec(w)],
        out_shape=[jax.ShapeDtypeStruct((t, w), F32), jax.ShapeDtypeStruct((t, w), BF16), jax.ShapeDtypeStruct((1, w), F32)],
        compiler_params=_cparams(("arbitrary",)),
    )(dsg, y, proj)


def _s5_du(dumm, dy, dvec, *, name, tm=256):
    t, w = dy.shape

    def body(a_ref, dy_ref, d_ref, o_ref):
        o_ref[...] = (a_ref[...] + d_ref[...] * dy_ref[...]).astype(BF16)

    return pl.pallas_call(
        body, name=name, grid=(t // tm,), in_specs=[_row_spec(tm, w), _row_spec(tm, w), _vec_spec(w)],
        out_specs=_row_spec(tm, w), out_shape=jax.ShapeDtypeStruct((t, w), BF16),
        compiler_params=_cparams(("parallel",)),
    )(dumm, dy, dvec)


def _s5_discretise(lam_re, lam_im, log_step, b_re, b_im):
    lam = lax.complex(lam_re, lam_im)
    dt = jnp.exp(log_step)[:, None]
    a = jnp.exp(lam * dt)
    bbar = ((a - 1.0) / lam)[..., None] * lax.complex(b_re, b_im)
    return jnp.real(a), jnp.imag(a), jnp.real(bbar), jnp.imag(bbar)


def _perm_cols(re, im):
    lead = re.shape[:-1]
    r = re.reshape(lead + (SCAN_NCB, 1, SCAN_CB))
    i = im.reshape(lead + (SCAN_NCB, 1, SCAN_CB))
    return jnp.concatenate([r, i], axis=-2).reshape(lead + (2 * SSM_CH,))


def _unperm_cols(x):
    lead = x.shape[:-1]
    y = x.reshape(lead + (SCAN_NCB, 2, SCAN_CB))
    return y[..., 0, :].reshape(lead + (SSM_CH,)), y[..., 1, :].reshape(lead + (SSM_CH,))


def _compact(re, im):
    _, r, c = re.shape
    eye = jnp.eye(SCAN_GPB, dtype=re.dtype)

    def half(x):
        x = x.reshape(SCAN_NCB, SCAN_GPB, r, c)
        return (eye[None, :, None, :, None] * x[:, :, :, None, :]).reshape(SCAN_NCB, SCAN_GPB * r, SCAN_GPB * c)

    return jnp.concatenate([half(re), half(im)], axis=-1)


def _compact_extract(x, r):
    c = SSM_STATE
    eye = jnp.eye(SCAN_GPB, dtype=x.dtype)
    y = x.reshape(SCAN_NCB, SCAN_GPB, r, 2, SCAN_GPB, c)
    dg = jnp.sum(y * eye[None, :, None, None, :, None], axis=4).reshape(SSM_GROUPS, r, 2, c)
    return dg[:, :, 0, :], dg[:, :, 1, :]


def _pow_table(ar, ai, descending=False):
    ar = ar.reshape(1, SSM_CH)
    ai = ai.reshape(1, SSM_CH)
    pw = [(ar, ai)]
    for _ in range(7):
        pw.append(_cmul(pw[-1][0], pw[-1][1], ar, ai))
    zero = (jnp.zeros_like(ar), jnp.zeros_like(ar))
    rows = [pw[0], pw[1], pw[3]] + [zero] * 5 + (pw[::-1] if descending else pw)
    re = jnp.concatenate([r for r, _ in rows], axis=0)
    im = jnp.concatenate([i for _, i in rows], axis=0)
    return _perm_cols(re, im)


def _place():
    x, y, c = lax.axis_index("x"), lax.axis_index("y"), lax.axis_index("c")
    chips = [(1 - x, y), (x, 1 - y), (1 - x, 1 - y)]
    return x, y, c, chips


def _dev(px, py, pc):
    return 4 * px + 2 * py + pc


class _Plan:
    def __init__(self, ins, out_shapes, sem_shapes, start, finish):
        self.ins, self.out_shapes, self.sem_shapes = list(ins), list(out_shapes), list(sem_shapes)
        self.start, self.finish = start, finish


def _split_plan_refs(plans, in_refs, out_refs, sem_refs):
    res, i, o, s = [], 0, 0, 0
    for p in plans:
        ni, no, ns = len(p.ins), len(p.out_shapes), len(p.sem_shapes)
        res.append((in_refs[i:i + ni], out_refs[o:o + no], sem_refs[s:s + ns]))
        i, o, s = i + ni, o + no, s + ns
    return res


def _run_plans(plans, *, name):
    ins = [a for p in plans for a in p.ins]
    outs = [o for p in plans for o in p.out_shapes]
    sems = [s for p in plans for s in p.sem_shapes]
    any_spec = pl.BlockSpec(memory_space=pl.ANY)

    def body(*refs):
        parts = _split_plan_refs(plans, refs[:len(ins)], refs[len(ins):len(ins) + len(outs)], refs[len(ins) + len(outs):])
        for p, r in zip(plans, parts):
            p.start(*r)
        for p, r in zip(plans, parts):
            p.finish(*r)

    res = pl.pallas_call(body, name=name, in_specs=[any_spec] * len(ins), out_specs=[any_spec] * len(outs),
                         out_shape=outs, scratch_shapes=sems)(*ins)
    return _split_plan_refs(plans, [], res, [])


def _call_with_plans(body, plans, *, name, grid, in_specs, out_specs, out_shape, scratch_shapes, semantics, operands):
    plans = list(plans)
    if not plans:
        res = pl.pallas_call(body, name=name, grid=grid, in_specs=in_specs, out_specs=out_specs, out_shape=out_shape,
                             scratch_shapes=scratch_shapes, compiler_params=_cparams(semantics))(*operands)
        return list(res), []
    n_in, n_out, n_scr = len(in_specs), len(out_specs), len(scratch_shapes)
    p_ins = [a for p in plans for a in p.ins]
    p_outs = [o for p in plans for o in p.out_shapes]
    p_sems = [s for p in plans for s in p.sem_shapes]
    nsteps = math.prod(grid)
    any_spec = pl.BlockSpec(memory_space=pl.ANY)

    def wrapped(*refs):
        bounds = [n_in, len(p_ins), n_out, len(p_outs), n_scr]
        parts, pos = [], 0
        for b in bounds:
            parts.append(refs[pos:pos + b])
            pos += b
        ins, p_in, outs, p_out, scr = parts
        step = pl.program_id(0)
        for ax in range(1, len(grid)):
            step = step * grid[ax] + pl.program_id(ax)
        riders = _split_plan_refs(plans, p_in, p_out, refs[pos:])

        @pl.when(step == 0)
        def _():
            for p, r in zip(plans, riders):
                p.start(*r)

        body(*ins, *outs, *scr)

        @pl.when(step == nsteps - 1)
        def _():
            for p, r in zip(plans, riders):
                p.finish(*r)

    res = pl.pallas_call(
        wrapped, name=name, grid=grid, in_specs=list(in_specs) + [any_spec] * len(p_ins),
        out_specs=list(out_specs) + [any_spec] * len(p_outs), out_shape=list(out_shape) + p_outs,
        scratch_shapes=list(scratch_shapes) + p_sems, compiler_params=_cparams(("arbitrary",) * len(grid)),
    )(*operands, *p_ins)
    return list(res[:n_out]), [r[1] for r in _split_plan_refs(plans, [], res[n_out:], [])]


def _gather_plan(shards):
    n = len(shards)

    def make(ins, outs, sems):
        send, recv, lsem = sems
        x, y, c, chips = _place()
        me, sib = (x, y, c), (x, y, 1 - c)

        def copy(w, k, block, to, src=None):
            dst = outs[w].at[_dev(*block)]
            return pltpu.make_async_remote_copy(
                src_ref=dst if src is None else src, dst_ref=dst,
                send_sem=send.at[w * 7 + k], recv_sem=recv.at[w * 7 + k], device_id=to, device_id_type=MESH)

        mine = [pltpu.make_async_copy(ins[w], outs[w].at[_dev(*me)], lsem.at[w]) for w in range(n)]
        first = []
        for w in range(n):
            first.append(copy(w, 0, me, sib, src=ins[w]))
            first += [copy(w, 1 + j, me, (*chip, c), src=ins[w]) for j, chip in enumerate(chips)]
        return copy, mine, first, me, sib, c, chips

    def start(ins, outs, sems):
        _, mine, first, *_ = make(ins, outs, sems)
        for cp in mine + first:
            cp.start()

    def finish(ins, outs, sems):
        copy, mine, first, me, sib, c, chips = make(ins, outs, sems)
        passed = []
        for j, chip in enumerate(chips):
            for w in range(n):
                copy(w, 1 + j, (*chip, c), me).wait_recv()
                fwd = copy(w, 4 + j, (*chip, c), sib)
                fwd.start()
                passed.append(fwd)
        for w in range(n):
            copy(w, 0, sib, me).wait_recv()
        for j, chip in enumerate(chips):
            for w in range(n):
                copy(w, 4 + j, (*chip, 1 - c), me).wait_recv()
        for cp in first + passed:
            cp.wait_send()
        for cp in mine:
            cp.wait()

    return _Plan(shards, [jax.ShapeDtypeStruct((N_DEV,) + s.shape, s.dtype) for s in shards],
                 [pltpu.SemaphoreType.DMA((7 * n,)), pltpu.SemaphoreType.DMA((7 * n,)), pltpu.SemaphoreType.DMA((n,))],
                 start, finish)


def _swap_plan(copies_of, n_copies, ins, out_shapes):
    def cps(in_refs, out_refs, sems):
        return copies_of(in_refs, out_refs, sems[0], sems[1])

    def start(in_refs, out_refs, sems):
        for cp in cps(in_refs, out_refs, sems):
            cp.start()

    def finish(in_refs, out_refs, sems):
        all_cps = cps(in_refs, out_refs, sems)
        for cp in all_cps:
            cp.wait_recv()
        for cp in all_cps:
            cp.wait_send()

    return _Plan(ins, out_shapes, [pltpu.SemaphoreType.DMA((n_copies,)), pltpu.SemaphoreType.DMA((n_copies,))],
                 start, finish)


def _sibling_plan(grads):
    n = len(grads)

    def copies(ins, outs, send, recv):
        x, y, c, chips = _place()
        owners = [(x, y)] + chips
        return [pltpu.make_async_remote_copy(
            src_ref=ins[w].at[_dev(*chip, 1 - c)], dst_ref=outs[w].at[k], send_sem=send.at[w * 4 + k],
            recv_sem=recv.at[w * 4 + k], device_id=(x, y, 1 - c), device_id_type=MESH)
            for w in range(n) for k, chip in enumerate(owners)]

    return _swap_plan(copies, 4 * n, grads, [jax.ShapeDtypeStruct((4,) + g.shape[1:], g.dtype) for g in grads])


def _chip_plan(parts, js=(0, 1, 2)):
    n, nj = len(parts), len(js)

    def copies(ins, outs, send, recv):
        x, y, c, chips = _place()
        return [pltpu.make_async_remote_copy(
            src_ref=ins[w].at[j], dst_ref=outs[w * nj + k], send_sem=send.at[w * nj + k],
            recv_sem=recv.at[w * nj + k], device_id=(*chips[j], c), device_id_type=MESH)
            for w in range(n) for k, j in enumerate(js)]

    return _swap_plan(copies, n * nj, parts,
                      [jax.ShapeDtypeStruct(p.shape[1:], p.dtype) for p in parts for _ in js])


UPDATE_TILE_BYTES = 768 * 1024


def _row_tile(r, c):
    best = 8
    for t in range(8, r + 1, 8):
        if r % t == 0 and t * c * 4 <= UPDATE_TILE_BYTES:
            best = t
    return best


def _chip_partial(g, sib, ids, *, name):
    _, r, c = g.shape
    tr = _row_tile(r, c)

    def body(ids_ref, g_ref, s_ref, o_ref):
        o_ref[...] = (g_ref[...] + s_ref[...]).astype(BF16)

    return pl.pallas_call(
        body, name=name,
        grid_spec=pltpu.PrefetchScalarGridSpec(
            num_scalar_prefetch=1, grid=(3, r // tr),
            in_specs=[pl.BlockSpec((None, tr, c), lambda j, i, ids_ref: (ids_ref[j], i, 0)),
                      pl.BlockSpec((None, tr, c), lambda j, i, ids_ref: (j + 1, i, 0))],
            out_specs=pl.BlockSpec((None, tr, c), lambda j, i, ids_ref: (j, i, 0))),
        out_shape=jax.ShapeDtypeStruct((3, r, c), BF16),
        compiler_params=_cparams(("parallel", "parallel")),
    )(ids, g, sib)


def _adamw_math(w, g, m, v):
    m = ADAM_B1 * m + (1.0 - ADAM_B1) * g
    v = ADAM_B2 * v + (1.0 - ADAM_B2) * (g * g)
    m_hat = m / (1.0 - ADAM_B1 ** ADAM_STEP)
    v_hat = v / (1.0 - ADAM_B2 ** ADAM_STEP)
    delta = -ADAM_LR * (m_hat / (jnp.sqrt(v_hat) + ADAM_EPS) + ADAM_WD * w)
    return delta, m, v


def _shard_update(g, sib, rem, me, w, m, v, *, name):
    r, c = w.shape
    tr = _row_tile(r, c)

    def body(me_ref, g_ref, s_ref, r0_ref, r1_ref, r2_ref, w_ref, m_ref, v_ref, go_ref, d_ref, mo_ref, vo_ref):
        gt = g_ref[...] + s_ref[...]
        gt = gt + r0_ref[...].astype(F32)
        gt = gt + r1_ref[...].astype(F32)
        gt = gt + r2_ref[...].astype(F32)
        go_ref[...] = gt
        d, mn, vn = _adamw_math(w_ref[...], gt, m_ref[...], v_ref[...])
        d_ref[...] = d
        mo_ref[...] = mn
        vo_ref[...] = vn

    blk = lambda k: pl.BlockSpec((None, tr, c), lambda i, me_ref: (k, i, 0))
    plain = pl.BlockSpec((tr, c), lambda i, me_ref: (i, 0))
    return pl.pallas_call(
        body, name=name,
        grid_spec=pltpu.PrefetchScalarGridSpec(
            num_scalar_prefetch=1, grid=(r // tr,),
            in_specs=[pl.BlockSpec((None, tr, c), lambda i, me_ref: (me_ref[0], i, 0)), blk(0), plain, plain, plain,
                      plain, plain, plain],
            out_specs=[plain] * 4),
        out_shape=[jax.ShapeDtypeStruct((r, c), F32)] * 4,
        compiler_params=_cparams(("parallel",)),
    )(me, g, sib, *rem, w, m, v)


def _small_update(gathered, w, m, v, *, name):
    _, r, c = gathered.shape

    def body(g_ref, w_ref, m_ref, v_ref, go_ref, d_ref, mo_ref, vo_ref):
        gt = g_ref[0]
        for k in range(1, N_DEV):
            gt = gt + g_ref[k]
        go_ref[...] = gt
        d, mn, vn = _adamw_math(w_ref[...], gt, m_ref[...], v_ref[...])
        d_ref[...] = d
        mo_ref[...] = mn
        vo_ref[...] = vn

    return pl.pallas_call(
        body, name=name, out_shape=[jax.ShapeDtypeStruct((r, c), F32)] * 4,
        compiler_params=pltpu.CompilerParams(vmem_limit_bytes=VMEM_LIMIT),
    )(gathered, w, m, v)


SMALL_UNIT = 1024


def _pack(parts):
    flat = []
    for p in parts:
        f = p.reshape(-1).astype(F32)
        pad = (-f.shape[0]) % SMALL_UNIT
        flat.append(jnp.pad(f, (0, pad)))
    return jnp.concatenate(flat).reshape(-1, 128)


def _unpack(buf, shapes):
    flat = buf.reshape(-1)
    out, off = [], 0
    for s in shapes:
        nel = math.prod(s)
        out.append(flat[off:off + nel].reshape(s))
        off += nel + ((-nel) % SMALL_UNIT)
    return out


def kernel(x, p, ffn1_w_in, ffn1_w_out, ln1_g, ln1_b, mix_w_in, conv_w, conv_b, conv_w_out, ssm_lam_re, ssm_lam_im, ssm_log_step, ssm_b_re, ssm_b_im, ssm_c_re, ssm_c_im, ssm_d, ssm_w_glu, mix_w_out, ln2_g, ln2_b, ffn2_w_in, ffn2_w_out, ln3_g, ln3_b, ple_w_in, ple_w_gate, ln4_g, ln4_b, loss_target, m_ffn1_w_in, m_ffn1_w_out, m_ln1_g, m_ln1_b, m_mix_w_in, m_conv_w, m_conv_b, m_conv_w_out, m_ssm_lam_re, m_ssm_lam_im, m_ssm_log_step, m_ssm_b_re, m_ssm_b_im, m_ssm_c_re, m_ssm_c_im, m_ssm_d, m_ssm_w_glu, m_mix_w_out, m_ln2_g, m_ln2_b, m_ffn2_w_in, m_ffn2_w_out, m_ln3_g, m_ln3_b, m_ple_w_in, m_ple_w_gate, m_ln4_g, m_ln4_b, v_ffn1_w_in, v_ffn1_w_out, v_ln1_g, v_ln1_b, v_mix_w_in, v_conv_w, v_conv_b, v_conv_w_out, v_ssm_lam_re, v_ssm_lam_im, v_ssm_log_step, v_ssm_b_re, v_ssm_b_im, v_ssm_c_re, v_ssm_c_im, v_ssm_d, v_ssm_w_glu, v_mix_w_out, v_ln2_g, v_ln2_b, v_ffn2_w_in, v_ffn2_w_out, v_ln3_g, v_ln3_b, v_ple_w_in, v_ple_w_gate, v_ln4_g, v_ln4_b):
    args = dict(locals())
    big = ['ffn1_w_in', 'ffn1_w_out', 'mix_w_in', 'conv_w_out', 'ssm_w_glu', 'mix_w_out',
           'ffn2_w_in', 'ffn2_w_out', 'ple_w_in', 'ple_w_gate']
    small = ['ln1_g', 'ln1_b', 'conv_b', 'ssm_lam_re', 'ssm_lam_im', 'ssm_log_step', 'ssm_b_re', 'ssm_b_im',
             'ssm_c_re', 'ssm_c_im', 'ssm_d', 'ln2_g', 'ln2_b', 'ln3_g', 'ln3_b', 'ln4_g', 'ln4_b']
    order = ['ffn1_w_in', 'ffn1_w_out', 'ln1_g', 'ln1_b', 'mix_w_in', 'conv_w', 'conv_b', 'conv_w_out',
             'ssm_lam_re', 'ssm_lam_im', 'ssm_log_step', 'ssm_b_re', 'ssm_b_im', 'ssm_c_re', 'ssm_c_im', 'ssm_d',
             'ssm_w_glu', 'mix_w_out', 'ln2_g', 'ln2_b', 'ffn2_w_in', 'ffn2_w_out', 'ln3_g', 'ln3_b',
             'ple_w_in', 'ple_w_gate', 'ln4_g', 'ln4_b']

    t = x.shape[1]
    d = x.shape[2]
    xc_, yc_, cc_ = lax.axis_index("x"), lax.axis_index("y"), lax.axis_index("c")
    me = (4 * xc_ + 2 * yc_ + cc_).astype(jnp.int32)
    cw_cols = conv_w.shape[2]

    turned = ('ffn1_w_in', 'ffn2_w_in')

    def local(a, nm):
        return jnp.swapaxes(a[0], 0, 1) if nm in turned else a[0]

    shard = {nm: local(args[nm], nm).astype(BF16) for nm in big}
    cw_pad = jnp.zeros((8, 128), F32).at[0:3, 0:cw_cols].set(conv_w[0])
    wf = shard['ffn1_w_in'].shape[0]

    def gather(*names):
        return _gather_plan([shard[nm] for nm in names])

    ((_, (w1in, cw_g), _),) = _run_plans([_gather_plan([shard['ffn1_w_in'], cw_pad])], name="gather_ffn1_in")
    cw_full = jnp.transpose(cw_g[:, 0:3, 0:cw_cols], (1, 0, 2)).reshape(3, N_DEV * cw_cols)
    cw8 = jnp.zeros((8, CONV_CH), F32).at[0:3, :].set(cw_full)

    s5_in = (ssm_lam_re[0], ssm_lam_im[0], ssm_log_step[0], ssm_b_re[0], ssm_b_im[0])
    (a_re, a_im, bb_re, bb_im), s5_vjp = jax.vjp(_s5_discretise, *s5_in)
    tab_f = _pow_table(a_re, a_im)
    tab_b = _pow_table(a_re, -a_im, descending=True)
    bmat_b = _compact(jnp.transpose(bb_re, (0, 2, 1)), jnp.transpose(bb_im, (0, 2, 1))).astype(BF16)
    cmat_tb = _compact(ssm_c_re[0], -ssm_c_im[0]).astype(BF16)
    dvec = ssm_d[0].reshape(1, SSM_W)

    xf = x[0]
    x_b = xf.astype(BF16)
    p_b = p[0, 0].astype(BF16)
    tgt = loss_target[0]
    tq = min(512, t)

    ffn_out = dict(ja='c', jb='c', nj=4, tm=tq, tn=d, tk=wf)
    def side_by_side(wb):
        return jnp.transpose(wb, (1, 0, 2)).reshape(wb.shape[1], N_DEV * wb.shape[2])

    a1, h1, ((w1out_g,),) = _ffn_in(x_b, w1in, name="ffn1_in", tm=tq, plans=[gather('ffn1_w_out')])
    w1out = w1out_g.reshape(4, wf, d)
    (r1, x1, x1b), ((wmix,),) = _mm(a1, w1out, name="ffn1_out", **ffn_out, plans=[gather('mix_w_in')],
                                    epilogue=_ln_epilogue(xf, ln1_g, ln1_b, 0.5))
    proj, ((wco, wglu, wmo_g),) = _mm(x1b, wmix, name="mix_in", jb='b', jo='b', o_flat=True, nj=8, tm=tq, tn=512,
                                      tk=d, plans=[gather('conv_w_out', 'ssm_w_glu', 'mix_w_out')])
    wmo = wmo_g.reshape(d, d)
    ycin = _conv_fwd(proj, cw8, conv_b, name="conv_fwd")
    wco, wglu = side_by_side(wco), side_by_side(wglu)
    yconv = _mm(ycin, wco, name="conv_out", tm=tq, tn=d, tk=CONV_CH)
    s_f, s_b, ((w2in,),) = _scan_fwd(proj, bmat_b, tab_f, name="scan_fwd", plans=[gather('ffn2_w_in')])
    blk = dict(ja='b', jb='b', jo='b', nj=SCAN_NCB)
    ymm = _mm(s_b, cmat_tb, name="ssm_read", a_flat=True, o_flat=True, tb=True, tm=tq, tn=SCAN_UW, tk=2 * SCAN_CB, **blk)
    ys, sg, u_b = _s5_out(ymm, proj, dvec, name="ssm_out")
    glu, ((w2out_g,),) = _mm(sg, wglu, name="glu_in", tm=tq, tn=d, tk=SSM_W, plans=[gather('ffn2_w_out')])
    w2out = w2out_g.reshape(4, wf, d)
    merged = _gate_fwd(yconv, glu, proj, name="gate_fwd")
    r2, x2, x2b = _mm(merged, wmo, name="mix_out", tm=tq, tn=d, tk=d, epilogue=_ln_epilogue(x1, ln2_g, ln2_b, 1.0))
    a2, h2, ((wpin, wgate_g),) = _ffn_in(x2b, w2in, name="ffn2_in", tm=tq, plans=[gather('ple_w_in', 'ple_w_gate')])
    wgate = wgate_g.reshape(d, d)
    r3, x3, x3b = _mm(a2, w2out, name="ffn2_out", **ffn_out, epilogue=_ln_epilogue(x2, ln3_g, ln3_b, 0.5))
    pe = _mm(p_b, side_by_side(wpin), name="ple_in", tm=tq, tn=d, tk=p_b.shape[1])
    gp = _mm(x3b, wgate, name="ple_gate", tm=tq, tn=d, tk=d)

    dr4, dpe_b, dgp_b, dg4, db4, loss_part = _ln_bwd(x3, [pe, gp], ln4_g, ln4_b, [], name="ple_ln_bwd",
                                                     ple=True, target=tgt)
    gb, sib, rem = {}, {}, {}
    ids = jnp.stack([_dev(1 - xc_, yc_, cc_), _dev(xc_, 1 - yc_, cc_), _dev(1 - xc_, 1 - yc_, cc_)]).astype(jnp.int32)

    def blocked(nm, g):
        return g.reshape((N_DEV,) + args[nm].shape[1:])

    def to_sibling(*names):
        return _sibling_plan([gb[nm] for nm in names])

    def chip_sums(names, sibs):
        for nm, s in zip(names, sibs):
            sib[nm] = s
        return [_chip_partial(gb[nm], sib[nm], ids, name=f"chip_sum_{nm}") for nm in names]

    ffn_in_dg = dict(ja='c', jb='c', nj=8, tm=tq, tn=d, tk=wf)
    ffn_in_wg = dict(ja='b', jo='b', ta=True, nj=8, tm=wf, tn=d, tk=t)
    ffn_out_wg = dict(ja='b', jo='b', ta=True, nj=4, tm=wf, tn=d, tk=t)

    gb['ple_w_in'] = _mm(p_b, dpe_b, name="ple_in_wg", jb='b', jo='b', b_flat=True, ta=True, nj=8,
                         tm=p_b.shape[1], tn=128, tk=t)
    gb['ple_w_gate'] = blocked('ple_w_gate', _mm(x3b, dgp_b, name="ple_gate_wg", ta=True, tm=d, tn=d, tk=t))
    g_ple = ['ple_w_in', 'ple_w_gate']
    dx3_g, (s_,) = _mm(dgp_b, wgate, name="ple_gate_dg", tb=True, tm=tq, tn=d, tk=d, plans=[to_sibling(*g_ple)])
    part = chip_sums(g_ple, s_)

    dr3, df2_b, dg3, db3 = _ln_bwd(r3, [], ln3_g, ln3_b, [(dr4, ALPHA), (dx3_g, 1.0)], name="ffn2_ln_bwd", fs=0.5)
    dh2, _ = _ffn_out_dg(df2_b, w2out, h2, name="ffn2_out_dg", tm=tq)
    g_, (r_,) = _mm(a2, df2_b, name="ffn2_out_wg", **ffn_out_wg, plans=[_chip_plan(part)])
    gb['ffn2_w_out'] = blocked('ffn2_w_out', g_)
    rem['ple_w_in'], rem['ple_w_gate'] = r_[0:3], r_[3:6]
    dx2_f, (s_,) = _mm(dh2, w2in, name="ffn2_in_dg", **ffn_in_dg, plans=[to_sibling('ffn2_w_out')])
    part = chip_sums(['ffn2_w_out'], s_)
    gb['ffn2_w_in'], (r_,) = _mm(dh2, x2b, name="ffn2_in_wg", **ffn_in_wg, plans=[_chip_plan(part)])
    rem['ffn2_w_out'] = r_

    dr2, dmix_b, dg2, db2 = _ln_bwd(r2, [], ln2_g, ln2_b, [(dr3, ALPHA), (dx2_f, 1.0)], name="mix_ln_bwd")
    dmerged, (s_,) = _mm(dmix_b, wmo, name="mix_out_dg", tb=True, tm=tq, tn=d, tk=d, plans=[to_sibling('ffn2_w_in')])
    part = chip_sums(['ffn2_w_in'], s_)
    gb['mix_w_out'] = blocked('mix_w_out', _mm(merged, dmix_b, name="mix_out_wg", ta=True, tm=d, tn=d, tk=t))
    dyconv_b, dglu_b, dgate_b = _gate_bwd(dmerged, yconv, glu, proj, name="gate_bwd")
    gb['conv_w_out'] = _mm(ycin, dyconv_b, name="conv_out_wg", jb='b', jo='b', b_flat=True, ta=True, nj=8,
                           tm=CONV_CH, tn=128, tk=t)
    dycin = _mm(dyconv_b, wco, name="conv_out_dg", tb=True, tm=tq, tn=CONV_CH, tk=d)
    gb['ssm_w_glu'] = _mm(sg, dglu_b, name="glu_in_wg", jb='b', jo='b', b_flat=True, ta=True, nj=8,
                          tm=SSM_W, tn=256, tk=t)
    g_mix = ['mix_w_out', 'conv_w_out', 'ssm_w_glu']
    dsg, (s_,) = _mm(dglu_b, wglu, name="glu_in_dg", tb=True, tm=tq, tn=SSM_W, tk=2 * d, plans=[to_sibling(*g_mix)])
    part_mix = chip_sums(g_mix, s_)
    dys, dys_b, dd = _s5_bwd_in(dsg, ys, proj, name="ssm_out_bwd")
    h_b, da_acc, (rem['ffn2_w_in'],) = _scan_bwd(dys_b, cmat_tb, s_f, tab_b, name="scan_bwd", plans=[_chip_plan(part)])
    dumm = _mm(h_b, bmat_b, name="ssm_write_dg", a_flat=True, o_flat=True, tb=True, tm=tq, tn=SCAN_UW,
               tk=2 * SCAN_CB, **blk)
    du_b = _s5_du(dumm, dys, dvec, name="ssm_du")
    g_bmat = _mm(u_b, h_b, name="ssm_write_wg", a_flat=True, b_flat=True, ta=True, tm=SCAN_UW,
                 tn=2 * SCAN_CB, tk=t, **blk)
    g_cmat = _mm(dys_b, s_b, name="ssm_read_wg", a_flat=True, b_flat=True, ta=True, tm=SCAN_UW,
                 tn=2 * SCAN_CB, tk=t, **blk)
    dcb_b, dcc_b, dch_b, dconv = _conv_bwd(dycin, proj, cw8, conv_b, name="conv_bwd")
    dproj = jnp.concatenate([dcb_b, dcc_b, dch_b, du_b, dgate_b], axis=1)
    gb['mix_w_in'], (r_,) = _mm(x1b, dproj, name="mix_in_wg", jb='b', jo='b', b_flat=True, ta=True, nj=8,
                                tm=d, tn=512, tk=t, plans=[_chip_plan(part_mix)])
    for i, nm in enumerate(g_mix):
        rem[nm] = r_[3 * i:3 * i + 3]
    dx1_m, (s_,) = _mm(dproj, wmix, name="mix_in_dg", ja='c', jb='c', a_flat=True, tb=True, nj=8,
                       tm=tq, tn=d, tk=512, plans=[to_sibling('mix_w_in')])
    part = chip_sums(['mix_w_in'], s_)

    dr1, df1_b, dg1, db1 = _ln_bwd(r1, [], ln1_g, ln1_b, [(dr2, ALPHA), (dx1_m, 1.0)], name="ffn1_ln_bwd", fs=0.5)
    dh1, (rem['mix_w_in'],) = _ffn_out_dg(df1_b, w1out, h1, name="ffn1_out_dg", tm=tq, plans=[_chip_plan(part)])

    da_sum = jnp.sum(da_acc, axis=0)
    da_re, da_im = _unperm_cols(da_sum)
    gbb_re, gbb_im = [jnp.transpose(v, (0, 2, 1)) for v in _compact_extract(g_bmat, SSM_GROUP)]
    g_c_re, g_c_im_neg = _compact_extract(g_cmat, SSM_GROUP)
    g_c_im = -g_c_im_neg
    g_lam_re, g_lam_im, g_log_step, g_b_re, g_b_im = s5_vjp(
        (da_re.reshape(SSM_GROUPS, SSM_STATE), da_im.reshape(SSM_GROUPS, SSM_STATE), gbb_re, gbb_im))
    g_d = dd.reshape(SSM_GROUPS, SSM_GROUP)

    small_g = {'ln1_g': dg1, 'ln1_b': db1, 'conv_b': dconv[3:4], 'ssm_lam_re': g_lam_re, 'ssm_lam_im': g_lam_im,
               'ssm_log_step': g_log_step, 'ssm_b_re': g_b_re, 'ssm_b_im': g_b_im, 'ssm_c_re': g_c_re,
               'ssm_c_im': g_c_im, 'ssm_d': g_d, 'ln2_g': dg2, 'ln2_b': db2, 'ln3_g': dg3, 'ln3_b': db3,
               'ln4_g': dg4, 'ln4_b': db4}
    small_shapes = [args[nm].shape for nm in small] + [(3, CONV_CH), (1,)]
    g_pack = _pack([small_g[nm] for nm in small] + [dconv[0:3], loss_part[0:1, 0:1]])

    gb['ffn1_w_in'], ((g_all,),) = _mm(dh1, x_b, name="ffn1_in_wg", **ffn_in_wg, plans=[_gather_plan([g_pack])])
    g_, (s_,) = _mm(a1, df1_b, name="ffn1_out_wg", **ffn_out_wg, plans=[to_sibling('ffn1_w_in')])
    gb['ffn1_w_out'] = blocked('ffn1_w_out', g_)
    part = chip_sums(['ffn1_w_in'], s_)
    (grad_x,), (r_, s_) = _mm(dh1, w1in, name="ffn1_in_dg", **ffn_in_dg,
                              plans=[_chip_plan(part), to_sibling('ffn1_w_out')],
                              epilogue=(lambda pr, drv: (pr + ALPHA * drv,), (dr1,), (), (F32,)))
    rem['ffn1_w_in'] = r_
    part = chip_sums(['ffn1_w_out'], s_)
    ((_, rem['ffn1_w_out'], _),) = _run_plans([_chip_plan(part)], name="grad_chips_ffn1_out")

    def full_cw(a):
        return lax.dynamic_update_slice(jnp.zeros((3, CONV_CH), F32), a[0], (0, me * cw_cols))

    zero1 = jnp.zeros((1,), F32)
    w_pack = _pack([args[nm] for nm in small] + [full_cw(conv_w), zero1])
    m_pack = _pack([args['m_' + nm] for nm in small] + [full_cw(m_conv_w), zero1])
    v_pack = _pack([args['v_' + nm] for nm in small] + [full_cw(v_conv_w), zero1])
    sg_sum, sd, sm, sv = _small_update(g_all, w_pack, m_pack, v_pack, name="small_update")
    res = {}
    for key, buf in (('grad_', sg_sum), ('delta_', sd), ('new_m_', sm), ('new_v_', sv)):
        parts = _unpack(buf, small_shapes)
        for nm, val in zip(small, parts[:len(small)]):
            res[key + nm] = val
        res[key + 'conv_w'] = lax.dynamic_slice(parts[len(small)], (0, me * cw_cols), (3, cw_cols))[None]
        if key == 'grad_':
            loss = parts[-1][0]

    me1 = me.reshape(1)
    for nm in big:
        upd = _shard_update(gb[nm], sib[nm], rem[nm], me1, local(args[nm], nm), local(args['m_' + nm], nm),
                            local(args['v_' + nm], nm), name=f"update_{nm}")
        for key, val in zip(('grad_', 'delta_', 'new_m_', 'new_v_'), upd):
            res[key + nm] = (jnp.swapaxes(val, 0, 1) if nm in turned else val)[None]

    outs = [loss, grad_x[None]]
    for key in ('grad_', 'delta_', 'new_m_', 'new_v_'):
        outs += [res[key + nm] for nm in order]
    return tuple(outs)
```

```python
import functools
import math

import jax
import jax.numpy as jnp
from jax import lax
from jax.experimental import pallas as pl
from jax.experimental.pallas import tpu as pltpu

F32 = jnp.float32
BF16 = jnp.bfloat16
MESH = pl.DeviceIdType.MESH

N_DEV = 8
ALPHA = 2.0 ** 0.25
LN_EPS = 1e-5
CONV_CH = 512
SSM_W = 512
SSM_GROUPS = 32
SSM_GROUP = 16
SSM_STATE = 64
SSM_CH = SSM_GROUPS * SSM_STATE
SCAN_CB = 512
SCAN_NCB = SSM_CH // SCAN_CB
SCAN_GPB = SSM_GROUPS // SCAN_NCB
SCAN_UW = SCAN_GPB * SSM_GROUP
SCAN_TT = 256
ADAM_LR = 0.001
ADAM_B1 = 0.9
ADAM_B2 = 0.999
ADAM_EPS = 1e-08
ADAM_WD = 0.01
ADAM_STEP = 10
VMEM_LIMIT = 56 * 1024 * 1024


def _cparams(sem=None, **kw):
    return pltpu.CompilerParams(dimension_semantics=sem, vmem_limit_bytes=VMEM_LIMIT, **kw)


def _mm(a, b, *, name, ja=None, jb=None, jo=None, a_flat=False, b_flat=False, o_flat=False,
        ta=False, tb=False, tm, tn, tk, nj=1, out_dtype=F32, plans=(), epilogue=None):
    def dims(arr, j, flat):
        if j is None:
            return arr.shape
        if flat:
            return (arr.shape[0], arr.shape[1] // nj)
        assert arr.shape[0] == nj, (name, arr.shape, nj)
        return arr.shape[1:]

    ar, ac = dims(a, ja, a_flat)
    br, bc = dims(b, jb, b_flat)
    m, k = (ac, ar) if ta else (ar, ac)
    k2, n = (bc, br) if tb else (br, bc)
    assert k == k2, (name, a.shape, b.shape)
    assert m % tm == 0 and n % tn == 0 and k % tk == 0, (name, m, n, k, tm, tn, tk)
    njb = nj if 'b' in (ja, jb) else 1
    njc = nj if 'c' in (ja, jb) else 1
    nk = k // tk
    j_inside = njc > 1 and nk == 1 and not ta
    n_in = njc if j_inside else 1
    nred = nk if j_inside else njc * nk
    grid = (njb, m // tm, n // tn, 1 if j_inside else njc, nk)

    def make_spec(j, flat, blk, rfn, cfn, cols_per_j):
        def jsel(g, c):
            return g if j == 'b' else c
        if j is None:
            return pl.BlockSpec(blk, lambda g, i, jn, c, kk: (rfn(i, jn, kk), cfn(i, jn, kk)))
        if j == 'c' and j_inside:
            if flat:
                return pl.BlockSpec((blk[0], nj * blk[1]), lambda g, i, jn, c, kk: (rfn(i, jn, kk), 0))
            return pl.BlockSpec((nj,) + blk, lambda g, i, jn, c, kk: (0, rfn(i, jn, kk), cfn(i, jn, kk)))
        if flat:
            nb = cols_per_j // blk[1]
            return pl.BlockSpec(blk, lambda g, i, jn, c, kk: (rfn(i, jn, kk), jsel(g, c) * nb + cfn(i, jn, kk)))
        return pl.BlockSpec((None,) + blk,
                            lambda g, i, jn, c, kk: (jsel(g, c), rfn(i, jn, kk), cfn(i, jn, kk)))

    if ta:
        a_spec = make_spec(ja, a_flat, (tk, tm), lambda i, jn, kk: kk, lambda i, jn, kk: i, ac)
    else:
        a_spec = make_spec(ja, a_flat, (tm, tk), lambda i, jn, kk: i, lambda i, jn, kk: kk, ac)
    if tb:
        b_spec = make_spec(jb, b_flat, (tn, tk), lambda i, jn, kk: jn, lambda i, jn, kk: kk, bc)
    else:
        b_spec = make_spec(jb, b_flat, (tk, tn), lambda i, jn, kk: kk, lambda i, jn, kk: jn, bc)
    o_spec = make_spec(jo, o_flat, (tm, tn), lambda i, jn, kk: i, lambda i, jn, kk: jn, n)
    if jo is None:
        out_shape = (m, n)
    elif o_flat:
        out_shape = (m, nj * n)
    else:
        out_shape = (nj, m, n)

    dn = (((0 if ta else 1,), (1 if tb else 0,)), ((), ()))

    def operand(ref, j, flat, jj, width):
        if not (j == 'c' and j_inside):
            return ref[...]
        return ref[:, jj * width:(jj + 1) * width] if flat else ref[jj]

    e_fn, e_rows, e_vecs, e_dtypes = epilogue if epilogue else (None, (), (), (out_dtype,))
    assert not epilogue or (nred == 1 and jo is None), name
    n_e = len(e_rows) + len(e_vecs)
    n_o = len(e_dtypes)

    def body(a_ref, b_ref, *rest):
        e_refs, o_refs, scratch = rest[:n_e], rest[n_e:n_e + n_o], rest[n_e + n_o:]
        o_ref = o_refs[0]
        p = None
        for jj in range(n_in):
            q = lax.dot_general(operand(a_ref, ja, a_flat, jj, tk), operand(b_ref, jb, b_flat, jj, tk if tb else tn),
                                dn, preferred_element_type=F32)
            p = q if p is None else p + q
        if epilogue:
            for ref, val in zip(o_refs, e_fn(p, *[r[...] for r in e_refs])):
                ref[...] = val.astype(ref.dtype)
        elif nred == 1:
            o_ref[...] = p.astype(o_ref.dtype)
        else:
            acc = scratch[0]
            r = pl.program_id(3) * nk + pl.program_id(4)

            @pl.when(r == 0)
            def _():
                acc[...] = p

            @pl.when(r > 0)
            def _():
                acc[...] += p

            @pl.when(r == nred - 1)
            def _():
                o_ref[...] = acc[...].astype(o_ref.dtype)

    vec_spec = pl.BlockSpec((1, tn), lambda g, i, jn, c, kk: (0, jn))
    res = _call_with_plans(
        body, plans, name=name, grid=grid,
        in_specs=[a_spec, b_spec] + [o_spec] * len(e_rows) + [vec_spec] * len(e_vecs), out_specs=[o_spec] * n_o,
        out_shape=[jax.ShapeDtypeStruct(out_shape, dt) for dt in e_dtypes],
        scratch_shapes=[] if nred == 1 else [pltpu.VMEM((tm, tn), F32)],
        semantics=("parallel", "parallel", "parallel", "arbitrary", "arbitrary"), operands=(a, b, *e_rows, *e_vecs))
    outs = res[0] if epilogue else res[0][0]
    return (outs, res[1]) if plans else outs


def _sigmoid(v):
    return 1.0 / (1.0 + jnp.exp(-v))


def _row_spec(tm, cols, colblk=0):
    return pl.BlockSpec((tm, cols), lambda i: (i, colblk))


def _vec_spec(cols):
    return pl.BlockSpec((1, cols), lambda i: (0, 0))


def _ffn_in(xb, win, *, name, tm, plans=()):
    t, d = xb.shape
    nj, w, _ = win.shape
    half = nj // 2
    dn = (((1,), (1,)), ((), ()))

    def body(x_ref, wg_ref, wu_ref, a_ref, gu_ref):
        xv = x_ref[...]
        g = lax.dot_general(xv, wg_ref[...], dn, preferred_element_type=F32)
        u = lax.dot_general(xv, wu_ref[...], dn, preferred_element_type=F32)
        a_ref[...] = (g * _sigmoid(g) * u).astype(BF16)
        gu_ref[0] = g.astype(BF16)
        gu_ref[1] = u.astype(BF16)

    (a, gu), riders = _call_with_plans(
        body, plans, name=name, grid=(half, t // tm),
        in_specs=[pl.BlockSpec((tm, d), lambda j, i: (i, 0)),
                  pl.BlockSpec((None, w, d), lambda j, i: (j, 0, 0)),
                  pl.BlockSpec((None, w, d), lambda j, i: (j + half, 0, 0))],
        out_specs=[pl.BlockSpec((None, tm, w), lambda j, i: (j, i, 0)),
                   pl.BlockSpec((2, None, tm, w), lambda j, i: (0, j, i, 0))],
        out_shape=[jax.ShapeDtypeStruct((half, t, w), BF16), jax.ShapeDtypeStruct((2, half, t, w), BF16)],
        scratch_shapes=[], semantics=("parallel", "parallel"), operands=(xb, win, win))
    return a, gu, riders


def _ffn_out_dg(dfb, wout, gu, *, name, tm, plans=()):
    t, d = dfb.shape
    half, w, _ = wout.shape
    dn = (((1,), (1,)), ((), ()))

    def body(df_ref, w_ref, gu_ref, dh_ref):
        da = lax.dot_general(df_ref[...], w_ref[...], dn, preferred_element_type=F32)
        g = gu_ref[0].astype(F32)
        u = gu_ref[1].astype(F32)
        sg = _sigmoid(g)
        dh_ref[0] = (da * u * (sg * (1.0 + g * (1.0 - sg)))).astype(BF16)
        dh_ref[1] = (da * (g * sg)).astype(BF16)

    (out,), riders = _call_with_plans(
        body, plans, name=name, grid=(half, t // tm),
        in_specs=[pl.BlockSpec((tm, d), lambda j, i: (i, 0)),
                  pl.BlockSpec((None, w, d), lambda j, i: (j, 0, 0)),
                  pl.BlockSpec((2, None, tm, w), lambda j, i: (0, j, i, 0))],
        out_specs=[pl.BlockSpec((2, None, tm, w), lambda j, i: (0, j, i, 0))],
        out_shape=[jax.ShapeDtypeStruct((2, half, t, w), BF16)],
        scratch_shapes=[], semantics=("parallel", "parallel"), operands=(dfb, wout, gu))
    return out.reshape(2 * half, t, w), riders


def _ln_stats(r):
    mu = jnp.mean(r, axis=-1, keepdims=True)
    xc = r - mu
    var = jnp.mean(xc * xc, axis=-1, keepdims=True)
    rstd = lax.rsqrt(var + LN_EPS)
    return xc * rstd, rstd


def _ln_epilogue(xin, gamma, beta, fs):
    def fn(p, xv, g, b):
        r = ALPHA * xv + fs * p
        xh, _ = _ln_stats(r)
        y = xh * g + b
        return r, y, y

    return fn, (xin,), (gamma, beta), (F32, F32, BF16)


def _ln_bwd(xin, fparts, gamma, beta, grads, *, name, fs=1.0, ple=False, target=None, tm=256):
    t, d = xin.shape
    nf = len(fparts)
    ng = len(grads)
    coefs = [c for _, c in grads]
    use_t = target is not None
    n_fout = 2 if ple else 1

    def body(*refs):
        pos = 0
        x_ref = refs[pos]; pos += 1
        f_refs = refs[pos:pos + nf]; pos += nf
        g_ref, b_ref = refs[pos:pos + 2]; pos += 2
        gr_refs = refs[pos:pos + ng]; pos += ng
        if use_t:
            t_ref = refs[pos]; pos += 1
        dr_ref = refs[pos]; pos += 1
        fo_refs = refs[pos:pos + n_fout]; pos += n_fout
        dg_ref, db_ref = refs[pos:pos + 2]; pos += 2
        if use_t:
            loss_ref = refs[pos]; pos += 1
        i = pl.program_id(0)

        if ple:
            pe = f_refs[0][...]
            sg = _sigmoid(f_refs[1][...])
            resid = ALPHA * x_ref[...] + pe * sg
        else:
            resid = x_ref[...]
        xh, rstd = _ln_stats(resid)
        gam = g_ref[...]
        if use_t:
            diff = xh * gam + b_ref[...] - t_ref[...]
            dy = diff * (1.0 / d)
            lpart = 0.5 * jnp.sum(jnp.sum(diff * diff, axis=-1, keepdims=True), axis=0, keepdims=True) * (1.0 / d)
        else:
            dy = coefs[0] * gr_refs[0][...]
            for c, r in zip(coefs[1:], gr_refs[1:]):
                dy = dy + c * r[...]
        dxh = dy * gam
        m1 = jnp.mean(dxh, axis=-1, keepdims=True)
        m2 = jnp.mean(dxh * xh, axis=-1, keepdims=True)
        dr = rstd * (dxh - m1 - xh * m2)
        dr_ref[...] = dr
        if ple:
            fo_refs[0][...] = (dr * sg).astype(BF16)
            fo_refs[1][...] = (dr * pe * (sg * (1.0 - sg))).astype(BF16)
        else:
            fo_refs[0][...] = (fs * dr).astype(BF16)
        dgp = jnp.sum(dy * xh, axis=0, keepdims=True)
        dbp = jnp.sum(dy, axis=0, keepdims=True)

        @pl.when(i == 0)
        def _():
            dg_ref[...] = dgp
            db_ref[...] = dbp
            if use_t:
                loss_ref[...] = jnp.broadcast_to(lpart, loss_ref.shape)

        @pl.when(i > 0)
        def _():
            dg_ref[...] += dgp
            db_ref[...] += dbp
            if use_t:
                loss_ref[...] += jnp.broadcast_to(lpart, loss_ref.shape)

    ins = [xin, *fparts, gamma, beta, *[g for g, _ in grads]] + ([target] if use_t else [])
    in_specs = ([_row_spec(tm, d)] * (1 + nf) + [_vec_spec(d), _vec_spec(d)] + [_row_spec(tm, d)] * ng
                + ([_row_spec(tm, d)] if use_t else []))
    out_specs = [_row_spec(tm, d)] * (1 + n_fout) + [_vec_spec(d), _vec_spec(d)] + ([_vec_spec(128)] if use_t else [])
    out_shape = ([jax.ShapeDtypeStruct((t, d), F32)] + [jax.ShapeDtypeStruct((t, d), BF16)] * n_fout
                 + [jax.ShapeDtypeStruct((1, d), F32)] * 2 + ([jax.ShapeDtypeStruct((1, 128), F32)] if use_t else []))
    return pl.pallas_call(
        body, name=name, grid=(t // tm,), in_specs=in_specs, out_specs=out_specs, out_shape=out_shape,
        compiler_params=_cparams(("arbitrary",)),
    )(*ins)


def _conv_fwd(proj, cw, cb, *, name, tm=256):
    t = proj.shape[0]
    c = CONV_CH
    hb = tm // 8

    def body(b_ref, c_ref, h_ref, cp_ref, hp_ref, w_ref, bias_ref, o_ref, q_scr):
        i = pl.program_id(0)
        q = c_ref[...] * h_ref[...]
        halo = jnp.where(i > 0, cp_ref[...] * hp_ref[...], 0.0)
        q_scr[0:8, :] = halo
        q_scr[8:, :] = q
        z = (w_ref[2:3, :] * q + w_ref[1:2, :] * q_scr[pl.ds(7, tm), :] + w_ref[0:1, :] * q_scr[pl.ds(6, tm), :]
             + bias_ref[...])
        o_ref[...] = (b_ref[...] * z).astype(BF16)

    prev = lambda blk: pl.BlockSpec((8, c), lambda i: (jnp.maximum(i * hb - 1, 0), blk))
    return pl.pallas_call(
        body, name=name, grid=(t // tm,),
        in_specs=[_row_spec(tm, c, 0), _row_spec(tm, c, 1), _row_spec(tm, c, 2), prev(1), prev(2),
                  pl.BlockSpec((8, c), lambda i: (0, 0)), _vec_spec(c)],
        out_specs=_row_spec(tm, c),
        out_shape=jax.ShapeDtypeStruct((t, c), BF16),
        scratch_shapes=[pltpu.VMEM((tm + 8, c), F32)],
        compiler_params=_cparams(("parallel",)),
    )(proj, proj, proj, proj, proj, cw, cb)


def _conv_bwd(dyc, proj, cw, cb, *, name, tm=256):
    t = proj.shape[0]
    c = CONV_CH
    hb = tm // 8
    nblk = t // 8

    def body(d_ref, b_ref, c_ref, h_ref, cp_ref, hp_ref, dn_ref, bn_ref, w_ref, bias_ref,
             db_ref, dc_ref, dh_ref, dw_ref, q_scr, z_scr):
        i = pl.program_id(0)
        last = pl.num_programs(0) - 1
        cc = c_ref[...]
        ch = h_ref[...]
        q = cc * ch
        halo = jnp.where(i > 0, cp_ref[...] * hp_ref[...], 0.0)
        q_scr[0:8, :] = halo
        q_scr[8:, :] = q
        w0, w1, w2 = w_ref[0:1, :], w_ref[1:2, :], w_ref[2:3, :]
        qm1 = q_scr[pl.ds(7, tm), :]
        qm2 = q_scr[pl.ds(6, tm), :]
        z = w2 * q + w1 * qm1 + w0 * qm2 + bias_ref[...]
        d = d_ref[...]
        bb = b_ref[...]
        db_ref[...] = (d * z).astype(BF16)
        dz = d * bb
        z_scr[0:tm, :] = dz
        z_scr[tm:, :] = jnp.where(i < last, dn_ref[...] * bn_ref[...], 0.0)
        dq = w2 * dz + w1 * z_scr[pl.ds(1, tm), :] + w0 * z_scr[pl.ds(2, tm), :]
        dc_ref[...] = (dq * ch).astype(BF16)
        dh_ref[...] = (dq * cc).astype(BF16)
        row = lax.broadcasted_iota(jnp.int32, (8, c), 0)
        part = jnp.zeros((8, c), F32)
        for k, term in enumerate((dz * qm2, dz * qm1, dz * q, dz)):
            part = jnp.where(row == k, jnp.sum(term, axis=0, keepdims=True), part)

        @pl.when(i == 0)
        def _():
            dw_ref[...] = part

        @pl.when(i > 0)
        def _():
            dw_ref[...] += part

    prev = lambda blk: pl.BlockSpec((8, c), lambda i: (jnp.maximum(i * hb - 1, 0), blk))
    nxt_p = pl.BlockSpec((8, c), lambda i: (jnp.minimum((i + 1) * hb, nblk - 1), 0))
    nxt_d = pl.BlockSpec((8, c), lambda i: (jnp.minimum((i + 1) * hb, nblk - 1), 0))
    return pl.pallas_call(
        body, name=name, grid=(t // tm,),
        in_specs=[_row_spec(tm, c), _row_spec(tm, c, 0), _row_spec(tm, c, 1), _row_spec(tm, c, 2),
                  prev(1), prev(2), nxt_d, nxt_p, pl.BlockSpec((8, c), lambda i: (0, 0)), _vec_spec(c)],
        out_specs=[_row_spec(tm, c)] * 3 + [pl.BlockSpec((8, c), lambda i: (0, 0))],
        out_shape=[jax.ShapeDtypeStruct((t, c), BF16)] * 3 + [jax.ShapeDtypeStruct((8, c), F32)],
        scratch_shapes=[pltpu.VMEM((tm + 8, c), F32), pltpu.VMEM((tm + 8, c), F32)],
        compiler_params=_cparams(("arbitrary",)),
    )(dyc, proj, proj, proj, proj, proj, dyc, proj, cw, cb)


def _gate_fwd(yconv, glu, proj, *, name, tm=256):
    t, d = yconv.shape

    def body(yc_ref, ga_ref, gb_ref, gc_ref, gs_ref, o_ref):
        yssm = ga_ref[...] * _sigmoid(gb_ref[...])
        o_ref[...] = (_sigmoid(gc_ref[...]) * yc_ref[...] + _sigmoid(gs_ref[...]) * yssm).astype(BF16)

    return pl.pallas_call(
        body, name=name, grid=(t // tm,),
        in_specs=[_row_spec(tm, d), _row_spec(tm, d, 0), _row_spec(tm, d, 1), _row_spec(tm, d, 2), _row_spec(tm, d, 3)],
        out_specs=_row_spec(tm, d), out_shape=jax.ShapeDtypeStruct((t, d), BF16),
        compiler_params=_cparams(("parallel",)),
    )(yconv, glu, glu, proj, proj)


def _gate_bwd(dm, yconv, glu, proj, *, name, tm=256):
    t, d = yconv.shape

    def body(dm_ref, yc_ref, ga_ref, gb_ref, gc_ref, gs_ref, dyc_ref, dglu_ref, dgate_ref):
        dmv = dm_ref[...]
        sc = _sigmoid(gc_ref[...])
        ss = _sigmoid(gs_ref[...])
        sb = _sigmoid(gb_ref[...])
        ga = ga_ref[...]
        yssm = ga * sb
        dyc_ref[...] = (dmv * sc).astype(BF16)
        dgate_ref[:, 0:d] = (dmv * yc_ref[...] * (sc * (1.0 - sc))).astype(BF16)
        dys = dmv * ss
        dgate_ref[:, d:2 * d] = (dmv * yssm * (ss * (1.0 - ss))).astype(BF16)
        dglu_ref[:, 0:d] = (dys * sb).astype(BF16)
        dglu_ref[:, d:2 * d] = (dys * ga * (sb * (1.0 - sb))).astype(BF16)

    return pl.pallas_call(
        body, name=name, grid=(t // tm,),
        in_specs=[_row_spec(tm, d), _row_spec(tm, d), _row_spec(tm, d, 0), _row_spec(tm, d, 1),
                  _row_spec(tm, d, 2), _row_spec(tm, d, 3)],
        out_specs=[_row_spec(tm, d), _row_spec(tm, 2 * d), _row_spec(tm, 2 * d)],
        out_shape=[jax.ShapeDtypeStruct((t, d), BF16), jax.ShapeDtypeStruct((t, 2 * d), BF16),
                   jax.ShapeDtypeStruct((t, 2 * d), BF16)],
        compiler_params=_cparams(("parallel",)),
    )(dm, yconv, glu, glu, proj, proj)


_GELU_C = math.sqrt(2.0 / math.pi)


def _gelu(v):
    return 0.5 * v * (1.0 + jnp.tanh(_GELU_C * (v + 0.044715 * v * v * v)))


def _gelu_grad(v):
    th = jnp.tanh(_GELU_C * (v + 0.044715 * v * v * v))
    return 0.5 * (1.0 + th) + 0.5 * v * (1.0 - th * th) * (_GELU_C * (1.0 + 3.0 * 0.044715 * v * v))


def _cmul(ar, ai, br, bi):
    return ar * br - ai * bi, ar * bi + ai * br


def _scan_fwd(proj, bmat, tab, *, name, plans=()):
    t = proj.shape[0]
    tt, cbw = SCAN_TT, SCAN_CB
    w2 = 2 * cbw

    def body(u_ref, b_ref, tab_ref, s_ref, sb_ref, bu_scr, carry):
        ti = pl.program_id(1)

        @pl.when(ti == 0)
        def _():
            carry[...] = jnp.zeros_like(carry)

        bu_scr[...] = jnp.dot(u_ref[...].astype(BF16), b_ref[...], preferred_element_type=F32)
        row = lax.broadcasted_iota(jnp.int32, (8, cbw), 0)

        def blk(bi, c):
            cr, ci = c
            r0 = pl.multiple_of(bi * 8, 8)
            xr = bu_scr[pl.ds(r0, 8), 0:cbw]
            xi = bu_scr[pl.ds(r0, 8), cbw:w2]
            for k, sh in enumerate((1, 2, 4)):
                kr = tab_ref[k:k + 1, 0:cbw]
                ki = tab_ref[k:k + 1, cbw:w2]
                sr = jnp.where(row >= sh, pltpu.roll(xr, sh, 0), 0.0)
                si = jnp.where(row >= sh, pltpu.roll(xi, sh, 0), 0.0)
                pr, pi = _cmul(kr, ki, sr, si)
                xr = xr + pr
                xi = xi + pi
            pr, pi = _cmul(tab_ref[8:16, 0:cbw], tab_ref[8:16, cbw:w2], cr, ci)
            xr = xr + pr
            xi = xi + pi
            s_ref[pl.ds(r0, 8), 0:cbw] = xr
            s_ref[pl.ds(r0, 8), cbw:w2] = xi
            return (jnp.broadcast_to(xr[7:8, :], (8, cbw)), jnp.broadcast_to(xi[7:8, :], (8, cbw)))

        cr, ci = lax.fori_loop(0, tt // 8, blk, (carry[:, 0:cbw], carry[:, cbw:w2]))
        carry[:, 0:cbw] = cr
        carry[:, cbw:w2] = ci
        sb_ref[...] = s_ref[...].astype(BF16)

    (s, sb), riders = _call_with_plans(
        body, plans, name=name, grid=(SCAN_NCB, t // tt),
        in_specs=[pl.BlockSpec((tt, SCAN_UW), lambda cb, ti: (ti, 3 * SCAN_NCB + cb)),
                  pl.BlockSpec((None, SCAN_UW, w2), lambda cb, ti: (cb, 0, 0)),
                  pl.BlockSpec((16, w2), lambda cb, ti: (0, cb))],
        out_specs=[pl.BlockSpec((tt, w2), lambda cb, ti: (ti, cb))] * 2,
        out_shape=[jax.ShapeDtypeStruct((t, 2 * SSM_CH), F32), jax.ShapeDtypeStruct((t, 2 * SSM_CH), BF16)],
        scratch_shapes=[pltpu.VMEM((tt, w2), F32), pltpu.VMEM((8, w2), F32)],
        semantics=("parallel", "arbitrary"), operands=(proj, bmat, tab))
    return s, sb, riders


def _scan_bwd(dyb, cmat_t, s, tabb, *, name, plans=()):
    t = s.shape[0]
    tt, cbw = SCAN_TT, SCAN_CB
    w2 = 2 * cbw
    nt = t // tt
    hb = tt // 8

    def body(dy_ref, c_ref, s_ref, sp_ref, tab_ref, h_ref, da_ref, g_scr, s_scr, carry):
        ti = pl.program_id(1)

        @pl.when(ti == 0)
        def _():
            carry[...] = jnp.zeros_like(carry)
            da_ref[...] = jnp.zeros_like(da_ref)

        g_scr[...] = jnp.dot(dy_ref[...], c_ref[...], preferred_element_type=F32)
        s_scr[0:8, :] = jnp.where(ti < nt - 1, sp_ref[...], 0.0)
        s_scr[8:, :] = s_ref[...]
        row = lax.broadcasted_iota(jnp.int32, (8, cbw), 0)

        def blk(k, c):
            cr, ci, ar, ai = c
            bi = hb - 1 - k
            r0 = pl.multiple_of(bi * 8, 8)
            xr = g_scr[pl.ds(r0, 8), 0:cbw]
            xi = g_scr[pl.ds(r0, 8), cbw:w2]
            for j, sh in enumerate((1, 2, 4)):
                kr = tab_ref[j:j + 1, 0:cbw]
                ki = tab_ref[j:j + 1, cbw:w2]
                sr = jnp.where(row < 8 - sh, pltpu.roll(xr, 8 - sh, 0), 0.0)
                si = jnp.where(row < 8 - sh, pltpu.roll(xi, 8 - sh, 0), 0.0)
                pr, pi = _cmul(kr, ki, sr, si)
                xr = xr + pr
                xi = xi + pi
            pr, pi = _cmul(tab_ref[8:16, 0:cbw], tab_ref[8:16, cbw:w2], cr, ci)
            xr = xr + pr
            xi = xi + pi
            h_ref[pl.ds(r0, 8), 0:cbw] = xr.astype(BF16)
            h_ref[pl.ds(r0, 8), cbw:w2] = xi.astype(BF16)
            pvr = s_scr[pl.ds(r0, 8), 0:cbw]
            pvi = s_scr[pl.ds(r0, 8), cbw:w2]
            cur_r = s_scr[pl.ds(r0 + 8, 8), 0:cbw]
            cur_i = s_scr[pl.ds(r0 + 8, 8), cbw:w2]
            spr = jnp.where(row == 0, jnp.broadcast_to(pvr[7:8, :], (8, cbw)), pltpu.roll(cur_r, 1, 0))
            spi = jnp.where(row == 0, jnp.broadcast_to(pvi[7:8, :], (8, cbw)), pltpu.roll(cur_i, 1, 0))
            ar = ar + spr * xr + spi * xi
            ai = ai + spr * xi - spi * xr
            return (jnp.broadcast_to(xr[0:1, :], (8, cbw)), jnp.broadcast_to(xi[0:1, :], (8, cbw)), ar, ai)

        z = jnp.zeros((8, cbw), F32)
        cr, ci, ar, ai = lax.fori_loop(0, hb, blk, (carry[:, 0:cbw], carry[:, cbw:w2], z, z))
        carry[:, 0:cbw] = cr
        carry[:, cbw:w2] = ci
        da_ref[:, 0:cbw] += ar
        da_ref[:, cbw:w2] += ai

    rt = lambda ti: nt - 1 - ti
    (h, da), riders = _call_with_plans(
        body, plans, name=name, grid=(SCAN_NCB, nt),
        in_specs=[pl.BlockSpec((tt, SCAN_UW), lambda cb, ti: (rt(ti), cb)),
                  pl.BlockSpec((None, SCAN_UW, w2), lambda cb, ti: (cb, 0, 0)),
                  pl.BlockSpec((tt, w2), lambda cb, ti: (rt(ti), cb)),
                  pl.BlockSpec((8, w2), lambda cb, ti: (jnp.maximum(rt(ti) * hb - 1, 0), cb)),
                  pl.BlockSpec((16, w2), lambda cb, ti: (0, cb))],
        out_specs=[pl.BlockSpec((tt, w2), lambda cb, ti: (rt(ti), cb)),
                   pl.BlockSpec((8, w2), lambda cb, ti: (0, cb))],
        out_shape=[jax.ShapeDtypeStruct((t, 2 * SSM_CH), BF16), jax.ShapeDtypeStruct((8, 2 * SSM_CH), F32)],
        scratch_shapes=[pltpu.VMEM((tt, w2), F32), pltpu.VMEM((tt + 8, w2), F32), pltpu.VMEM((8, w2), F32)],
        semantics=("parallel", "arbitrary"), operands=(dyb, cmat_t, s, s, tabb))
    return h, da, riders


def _s5_out(ymm, proj, dvec, *, name, tm=256):
    t, w = ymm.shape

    def body(y_ref, u_ref, d_ref, yo_ref, sg_ref, ub_ref):
        u = u_ref[...]
        y = y_ref[...] + d_ref[...] * u
        yo_ref[...] = y
        sg_ref[...] = _gelu(y).astype(BF16)
        ub_ref[...] = u.astype(BF16)

    return pl.pallas_call(
        body, name=name, grid=(t // tm,),
        in_specs=[_row_spec(tm, w), _row_spec(tm, w, 3), _vec_spec(w)],
        out_specs=[_row_spec(tm, w)] * 3,
        out_shape=[jax.ShapeDtypeStruct((t, w), F32), jax.ShapeDtypeStruct((t, w), BF16), jax.ShapeDtypeStruct((t, w), BF16)],
        compiler_params=_cparams(("parallel",)),
    )(ymm, proj, dvec)


def _s5_bwd_in(dsg, y, proj, *, name, tm=256):
    t, w = y.shape

    def body(d_ref, y_ref, u_ref, dy_ref, dyb_ref, dd_ref):
        i = pl.program_id(0)
        dy = d_ref[...] * _gelu_grad(y_ref[...])
        dy_ref[...] = dy
        dyb_ref[...] = dy.astype(BF16)
        part = jnp.sum(dy * u_ref[...], axis=0, keepdims=True)

        @pl.when(i == 0)
        def _():
            dd_ref[...] = part

        @pl.when(i > 0)
        def _():
            dd_ref[...] += part

    return pl.pallas_call(
        body, name=name, grid=(t // tm,),
        in_specs=[_row_spec(tm, w), _row_spec(tm, w), _row_spec(tm, w, 3)],
        out_specs=[_row_spec(tm, w), _row_spec(tm, w), _vec_spec(w)],
        out_shape=[jax.ShapeDtypeStruct((t, w), F32), jax.ShapeDtypeStruct((t, w), BF16), jax.ShapeDtypeStruct((1, w), F32)],
        compiler_params=_cparams(("arbitrary",)),
    )(dsg, y, proj)


def _s5_du(dumm, dy, dvec, *, name, tm=256):
    t, w = dy.shape

    def body(a_ref, dy_ref, d_ref, o_ref):
        o_ref[...] = (a_ref[...] + d_ref[...] * dy_ref[...]).astype(BF16)

    return pl.pallas_call(
        body, name=name, grid=(t // tm,), in_specs=[_row_spec(tm, w), _row_spec(tm, w), _vec_spec(w)],
        out_specs=_row_spec(tm, w), out_shape=jax.ShapeDtypeStruct((t, w), BF16),
        compiler_params=_cparams(("parallel",)),
    )(dumm, dy, dvec)


def _s5_discretise(lam_re, lam_im, log_step, b_re, b_im):
    lam = lax.complex(lam_re, lam_im)
    dt = jnp.exp(log_step)[:, None]
    a = jnp.exp(lam * dt)
    bbar = ((a - 1.0) / lam)[..., None] * lax.complex(b_re, b_im)
    return jnp.real(a), jnp.imag(a), jnp.real(bbar), jnp.imag(bbar)


def _perm_cols(re, im):
    lead = re.shape[:-1]
    r = re.reshape(lead + (SCAN_NCB, 1, SCAN_CB))
    i = im.reshape(lead + (SCAN_NCB, 1, SCAN_CB))
    return jnp.concatenate([r, i], axis=-2).reshape(lead + (2 * SSM_CH,))


def _unperm_cols(x):
    lead = x.shape[:-1]
    y = x.reshape(lead + (SCAN_NCB, 2, SCAN_CB))
    return y[..., 0, :].reshape(lead + (SSM_CH,)), y[..., 1, :].reshape(lead + (SSM_CH,))


def _compact(re, im):
    _, r, c = re.shape
    eye = jnp.eye(SCAN_GPB, dtype=re.dtype)

    def half(x):
        x = x.reshape(SCAN_NCB, SCAN_GPB, r, c)
        return (eye[None, :, None, :, None] * x[:, :, :, None, :]).reshape(SCAN_NCB, SCAN_GPB * r, SCAN_GPB * c)

    return jnp.concatenate([half(re), half(im)], axis=-1)


def _compact_extract(x, r):
    c = SSM_STATE
    eye = jnp.eye(SCAN_GPB, dtype=x.dtype)
    y = x.reshape(SCAN_NCB, SCAN_GPB, r, 2, SCAN_GPB, c)
    dg = jnp.sum(y * eye[None, :, None, None, :, None], axis=4).reshape(SSM_GROUPS, r, 2, c)
    return dg[:, :, 0, :], dg[:, :, 1, :]


def _pow_table(ar, ai, descending=False):
    ar = ar.reshape(1, SSM_CH)
    ai = ai.reshape(1, SSM_CH)
    pw = [(ar, ai)]
    for _ in range(7):
        pw.append(_cmul(pw[-1][0], pw[-1][1], ar, ai))
    zero = (jnp.zeros_like(ar), jnp.zeros_like(ar))
    rows = [pw[0], pw[1], pw[3]] + [zero] * 5 + (pw[::-1] if descending else pw)
    re = jnp.concatenate([r for r, _ in rows], axis=0)
    im = jnp.concatenate([i for _, i in rows], axis=0)
    return _perm_cols(re, im)


def _place():
    x, y, c = lax.axis_index("x"), lax.axis_index("y"), lax.axis_index("c")
    chips = [(1 - x, y), (x, 1 - y), (1 - x, 1 - y)]
    return x, y, c, chips


def _dev(px, py, pc):
    return 4 * px + 2 * py + pc


class _Plan:
    def __init__(self, ins, out_shapes, sem_shapes, start, finish, middle=None):
        self.ins, self.out_shapes, self.sem_shapes = list(ins), list(out_shapes), list(sem_shapes)
        self.start, self.finish, self.middle = start, finish, middle


def _split_plan_refs(plans, in_refs, out_refs, sem_refs):
    res, i, o, s = [], 0, 0, 0
    for p in plans:
        ni, no, ns = len(p.ins), len(p.out_shapes), len(p.sem_shapes)
        res.append((in_refs[i:i + ni], out_refs[o:o + no], sem_refs[s:s + ns]))
        i, o, s = i + ni, o + no, s + ns
    return res


def _run_plans(plans, *, name):
    ins = [a for p in plans for a in p.ins]
    outs = [o for p in plans for o in p.out_shapes]
    sems = [s for p in plans for s in p.sem_shapes]
    any_spec = pl.BlockSpec(memory_space=pl.ANY)

    def body(*refs):
        parts = _split_plan_refs(plans, refs[:len(ins)], refs[len(ins):len(ins) + len(outs)], refs[len(ins) + len(outs):])
        for p, r in zip(plans, parts):
            p.start(*r)
        for p, r in zip(plans, parts):
            if p.middle:
                p.middle(*r)
        for p, r in zip(plans, parts):
            p.finish(*r)

    res = pl.pallas_call(body, name=name, in_specs=[any_spec] * len(ins), out_specs=[any_spec] * len(outs),
                         out_shape=outs, scratch_shapes=sems)(*ins)
    return _split_plan_refs(plans, [], res, [])


def _call_with_plans(body, plans, *, name, grid, in_specs, out_specs, out_shape, scratch_shapes, semantics, operands):
    plans = list(plans)
    if not plans:
        res = pl.pallas_call(body, name=name, grid=grid, in_specs=in_specs, out_specs=out_specs, out_shape=out_shape,
                             scratch_shapes=scratch_shapes, compiler_params=_cparams(semantics))(*operands)
        return list(res), []
    n_in, n_out, n_scr = len(in_specs), len(out_specs), len(scratch_shapes)
    p_ins = [a for p in plans for a in p.ins]
    p_outs = [o for p in plans for o in p.out_shapes]
    p_sems = [s for p in plans for s in p.sem_shapes]
    nsteps = math.prod(grid)
    any_spec = pl.BlockSpec(memory_space=pl.ANY)

    def wrapped(*refs):
        bounds = [n_in, len(p_ins), n_out, len(p_outs), n_scr]
        parts, pos = [], 0
        for b in bounds:
            parts.append(refs[pos:pos + b])
            pos += b
        ins, p_in, outs, p_out, scr = parts
        step = pl.program_id(0)
        for ax in range(1, len(grid)):
            step = step * grid[ax] + pl.program_id(ax)
        riders = _split_plan_refs(plans, p_in, p_out, refs[pos:])

        @pl.when(step == 0)
        def _():
            for p, r in zip(plans, riders):
                p.start(*r)

        mids = [(p, r) for p, r in zip(plans, riders) if p.middle]
        mid_step = nsteps // 2
        split_mid = mids and 0 < mid_step < nsteps - 1
        if split_mid:
            @pl.when(step == mid_step)
            def _():
                for p, r in mids:
                    p.middle(*r)

        body(*ins, *outs, *scr)

        @pl.when(step == nsteps - 1)
        def _():
            if not split_mid:
                for p, r in mids:
                    p.middle(*r)
            for p, r in zip(plans, riders):
                p.finish(*r)

    res = pl.pallas_call(
        wrapped, name=name, grid=grid, in_specs=list(in_specs) + [any_spec] * len(p_ins),
        out_specs=list(out_specs) + [any_spec] * len(p_outs), out_shape=list(out_shape) + p_outs,
        scratch_shapes=list(scratch_shapes) + p_sems, compiler_params=_cparams(("arbitrary",) * len(grid)),
    )(*operands, *p_ins)
    return list(res[:n_out]), [r[1] for r in _split_plan_refs(plans, [], res[n_out:], [])]


def _gather_plan(shards):
    n = len(shards)
    nk = 8

    def make(ins, outs, sems):
        send, recv, lsem = sems
        x, y, c, _ = _place()
        me, sib, xn, yn, dg = (x, y, c), (x, y, 1 - c), (1 - x, y, c), (x, 1 - y, c), (1 - x, 1 - y, c)

        def part(w, block, half):
            ref = outs[w].at[_dev(*block)]
            if half is None:
                return ref
            rows = shards[w].shape[0] // 2
            return ref.at[pl.ds(half * rows, rows)]

        def copy(w, k, block, to, half=None, src=None):
            dst = part(w, block, half)
            return pltpu.make_async_remote_copy(
                src_ref=dst if src is None else src, dst_ref=dst,
                send_sem=send.at[w * nk + k], recv_sem=recv.at[w * nk + k], device_id=to, device_id_type=MESH)

        mine = [pltpu.make_async_copy(ins[w], outs[w].at[_dev(*me)], lsem.at[w]) for w in range(n)]
        return copy, mine, me, sib, xn, yn, dg

    def first_copies(copy, me, sib, xn, yn, ins):
        return [copy(w, k, me, to, src=ins[w]) for w in range(n) for k, to in ((0, sib), (1, xn), (2, yn))]

    def start(ins, outs, sems):
        copy, mine, me, sib, xn, yn, _ = make(ins, outs, sems)
        for cp in mine + first_copies(copy, me, sib, xn, yn, ins):
            cp.start()

    def middle(ins, outs, sems):
        copy, _, me, sib, xn, yn, _ = make(ins, outs, sems)
        for w in range(n):
            copy(w, 1, xn, me).wait_recv()
            copy(w, 3, xn, yn, half=0).start()
            copy(w, 5, xn, sib).start()
        for w in range(n):
            copy(w, 2, yn, me).wait_recv()
            copy(w, 4, yn, xn, half=1).start()
            copy(w, 6, yn, sib).start()

    def finish(ins, outs, sems):
        copy, mine, me, sib, xn, yn, dg = make(ins, outs, sems)
        last = []
        for w in range(n):
            copy(w, 3, dg, me, half=0).wait_recv()
            copy(w, 4, dg, me, half=1).wait_recv()
            fwd = copy(w, 7, dg, sib)
            fwd.start()
            last.append(fwd)
        sx, sy, sd = (1 - me[0], me[1], 1 - me[2]), (me[0], 1 - me[1], 1 - me[2]), (1 - me[0], 1 - me[1], 1 - me[2])
        for w in range(n):
            copy(w, 0, sib, me).wait_recv()
            copy(w, 5, sx, me).wait_recv()
            copy(w, 6, sy, me).wait_recv()
            copy(w, 7, sd, me).wait_recv()
        for cp in first_copies(copy, me, sib, xn, yn, ins) + last:
            cp.wait_send()
        for w in range(n):
            copy(w, 3, xn, yn, half=0).wait_send()
            copy(w, 5, xn, sib).wait_send()
            copy(w, 4, yn, xn, half=1).wait_send()
            copy(w, 6, yn, sib).wait_send()
        for cp in mine:
            cp.wait()

    return _Plan(shards, [jax.ShapeDtypeStruct((N_DEV,) + s.shape, s.dtype) for s in shards],
                 [pltpu.SemaphoreType.DMA((nk * n,)), pltpu.SemaphoreType.DMA((nk * n,)), pltpu.SemaphoreType.DMA((n,))],
                 start, finish, middle)


def _swap_plan(copies_of, n_copies, ins, out_shapes):
    def cps(in_refs, out_refs, sems):
        return copies_of(in_refs, out_refs, sems[0], sems[1])

    def start(in_refs, out_refs, sems):
        for cp in cps(in_refs, out_refs, sems):
            cp.start()

    def finish(in_refs, out_refs, sems):
        all_cps = cps(in_refs, out_refs, sems)
        for cp in all_cps:
            cp.wait_recv()
        for cp in all_cps:
            cp.wait_send()

    return _Plan(ins, out_shapes, [pltpu.SemaphoreType.DMA((n_copies,)), pltpu.SemaphoreType.DMA((n_copies,))],
                 start, finish)


def _sibling_plan(grads):
    n = len(grads)

    def copies(ins, outs, send, recv):
        x, y, c, chips = _place()
        owners = [(x, y)] + chips
        return [pltpu.make_async_remote_copy(
            src_ref=ins[w].at[_dev(*chip, 1 - c)], dst_ref=outs[w].at[k], send_sem=send.at[w * 4 + k],
            recv_sem=recv.at[w * 4 + k], device_id=(x, y, 1 - c), device_id_type=MESH)
            for w in range(n) for k, chip in enumerate(owners)]

    return _swap_plan(copies, 4 * n, grads, [jax.ShapeDtypeStruct((4,) + g.shape[1:], g.dtype) for g in grads])


def _chip_plan(parts, js=(0, 1, 2)):
    n, nj = len(parts), len(js)

    def copies(ins, outs, send, recv):
        x, y, c, chips = _place()
        return [pltpu.make_async_remote_copy(
            src_ref=ins[w].at[j], dst_ref=outs[w * nj + k], send_sem=send.at[w * nj + k],
            recv_sem=recv.at[w * nj + k], device_id=(*chips[j], c), device_id_type=MESH)
            for w in range(n) for k, j in enumerate(js)]

    return _swap_plan(copies, n * nj, parts,
                      [jax.ShapeDtypeStruct(p.shape[1:], p.dtype) for p in parts for _ in js])


UPDATE_TILE_BYTES = 768 * 1024


def _row_tile(r, c):
    best = 8
    for t in range(8, r + 1, 8):
        if r % t == 0 and t * c * 4 <= UPDATE_TILE_BYTES:
            best = t
    return best


def _chip_partial(g, sib, ids, *, name):
    _, r, c = g.shape
    tr = _row_tile(r, c)

    def body(ids_ref, g_ref, s_ref, o_ref):
        o_ref[...] = (g_ref[...] + s_ref[...]).astype(BF16)

    return pl.pallas_call(
        body, name=name,
        grid_spec=pltpu.PrefetchScalarGridSpec(
            num_scalar_prefetch=1, grid=(3, r // tr),
            in_specs=[pl.BlockSpec((None, tr, c), lambda j, i, ids_ref: (ids_ref[j], i, 0)),
                      pl.BlockSpec((None, tr, c), lambda j, i, ids_ref: (j + 1, i, 0))],
            out_specs=pl.BlockSpec((None, tr, c), lambda j, i, ids_ref: (j, i, 0))),
        out_shape=jax.ShapeDtypeStruct((3, r, c), BF16),
        compiler_params=_cparams(("parallel", "parallel")),
    )(ids, g, sib)


def _adamw_math(w, g, m, v):
    m = ADAM_B1 * m + (1.0 - ADAM_B1) * g
    v = ADAM_B2 * v + (1.0 - ADAM_B2) * (g * g)
    m_hat = m / (1.0 - ADAM_B1 ** ADAM_STEP)
    v_hat = v / (1.0 - ADAM_B2 ** ADAM_STEP)
    delta = -ADAM_LR * (m_hat / (jnp.sqrt(v_hat) + ADAM_EPS) + ADAM_WD * w)
    return delta, m, v


def _shard_update(g, sib, rem, me, w, m, v, *, name):
    r, c = w.shape
    tr = _row_tile(r, c)

    def body(me_ref, g_ref, s_ref, r0_ref, r1_ref, r2_ref, w_ref, m_ref, v_ref, go_ref, d_ref, mo_ref, vo_ref):
        gt = g_ref[...] + s_ref[...]
        gt = gt + r0_ref[...].astype(F32)
        gt = gt + r1_ref[...].astype(F32)
        gt = gt + r2_ref[...].astype(F32)
        go_ref[...] = gt
        d, mn, vn = _adamw_math(w_ref[...], gt, m_ref[...], v_ref[...])
        d_ref[...] = d
        mo_ref[...] = mn
        vo_ref[...] = vn

    blk = lambda k: pl.BlockSpec((None, tr, c), lambda i, me_ref: (k, i, 0))
    plain = pl.BlockSpec((tr, c), lambda i, me_ref: (i, 0))
    return pl.pallas_call(
        body, name=name,
        grid_spec=pltpu.PrefetchScalarGridSpec(
            num_scalar_prefetch=1, grid=(r // tr,),
            in_specs=[pl.BlockSpec((None, tr, c), lambda i, me_ref: (me_ref[0], i, 0)), blk(0), plain, plain, plain,
                      plain, plain, plain],
            out_specs=[plain] * 4),
        out_shape=[jax.ShapeDtypeStruct((r, c), F32)] * 4,
        compiler_params=_cparams(("parallel",)),
    )(me, g, sib, *rem, w, m, v)


def _small_update(gathered, w, m, v, *, name):
    _, r, c = gathered.shape

    def body(g_ref, w_ref, m_ref, v_ref, go_ref, d_ref, mo_ref, vo_ref):
        gt = g_ref[0]
        for k in range(1, N_DEV):
            gt = gt + g_ref[k]
        go_ref[...] = gt
        d, mn, vn = _adamw_math(w_ref[...], gt, m_ref[...], v_ref[...])
        d_ref[...] = d
        mo_ref[...] = mn
        vo_ref[...] = vn

    return pl.pallas_call(
        body, name=name, out_shape=[jax.ShapeDtypeStruct((r, c), F32)] * 4,
        compiler_params=pltpu.CompilerParams(vmem_limit_bytes=VMEM_LIMIT),
    )(gathered, w, m, v)


SMALL_UNIT = 1024


def _pack(parts):
    flat = []
    for p in parts:
        f = p.reshape(-1).astype(F32)
        pad = (-f.shape[0]) % SMALL_UNIT
        flat.append(jnp.pad(f, (0, pad)))
    return jnp.concatenate(flat).reshape(-1, 128)


def _unpack(buf, shapes):
    flat = buf.reshape(-1)
    out, off = [], 0
    for s in shapes:
        nel = math.prod(s)
        out.append(flat[off:off + nel].reshape(s))
        off += nel + ((-nel) % SMALL_UNIT)
    return out


def kernel(x, p, ffn1_w_in, ffn1_w_out, ln1_g, ln1_b, mix_w_in, conv_w, conv_b, conv_w_out, ssm_lam_re, ssm_lam_im, ssm_log_step, ssm_b_re, ssm_b_im, ssm_c_re, ssm_c_im, ssm_d, ssm_w_glu, mix_w_out, ln2_g, ln2_b, ffn2_w_in, ffn2_w_out, ln3_g, ln3_b, ple_w_in, ple_w_gate, ln4_g, ln4_b, loss_target, m_ffn1_w_in, m_ffn1_w_out, m_ln1_g, m_ln1_b, m_mix_w_in, m_conv_w, m_conv_b, m_conv_w_out, m_ssm_lam_re, m_ssm_lam_im, m_ssm_log_step, m_ssm_b_re, m_ssm_b_im, m_ssm_c_re, m_ssm_c_im, m_ssm_d, m_ssm_w_glu, m_mix_w_out, m_ln2_g, m_ln2_b, m_ffn2_w_in, m_ffn2_w_out, m_ln3_g, m_ln3_b, m_ple_w_in, m_ple_w_gate, m_ln4_g, m_ln4_b, v_ffn1_w_in, v_ffn1_w_out, v_ln1_g, v_ln1_b, v_mix_w_in, v_conv_w, v_conv_b, v_conv_w_out, v_ssm_lam_re, v_ssm_lam_im, v_ssm_log_step, v_ssm_b_re, v_ssm_b_im, v_ssm_c_re, v_ssm_c_im, v_ssm_d, v_ssm_w_glu, v_mix_w_out, v_ln2_g, v_ln2_b, v_ffn2_w_in, v_ffn2_w_out, v_ln3_g, v_ln3_b, v_ple_w_in, v_ple_w_gate, v_ln4_g, v_ln4_b):
    args = dict(locals())
    big = ['ffn1_w_in', 'ffn1_w_out', 'mix_w_in', 'conv_w_out', 'ssm_w_glu', 'mix_w_out',
           'ffn2_w_in', 'ffn2_w_out', 'ple_w_in', 'ple_w_gate']
    small = ['ln1_g', 'ln1_b', 'conv_b', 'ssm_lam_re', 'ssm_lam_im', 'ssm_log_step', 'ssm_b_re', 'ssm_b_im',
             'ssm_c_re', 'ssm_c_im', 'ssm_d', 'ln2_g', 'ln2_b', 'ln3_g', 'ln3_b', 'ln4_g', 'ln4_b']
    order = ['ffn1_w_in', 'ffn1_w_out', 'ln1_g', 'ln1_b', 'mix_w_in', 'conv_w', 'conv_b', 'conv_w_out',
             'ssm_lam_re', 'ssm_lam_im', 'ssm_log_step', 'ssm_b_re', 'ssm_b_im', 'ssm_c_re', 'ssm_c_im', 'ssm_d',
             'ssm_w_glu', 'mix_w_out', 'ln2_g', 'ln2_b', 'ffn2_w_in', 'ffn2_w_out', 'ln3_g', 'ln3_b',
             'ple_w_in', 'ple_w_gate', 'ln4_g', 'ln4_b']

    t = x.shape[1]
    d = x.shape[2]
    xc_, yc_, cc_ = lax.axis_index("x"), lax.axis_index("y"), lax.axis_index("c")
    me = (4 * xc_ + 2 * yc_ + cc_).astype(jnp.int32)
    cw_cols = conv_w.shape[2]

    turned = ('ffn1_w_in', 'ffn2_w_in')

    def local(a, nm):
        return jnp.swapaxes(a[0], 0, 1) if nm in turned else a[0]

    shard = {nm: local(args[nm], nm).astype(BF16) for nm in big}
    cw_pad = jnp.zeros((16, 128), F32).at[0:3, 0:cw_cols].set(conv_w[0])
    wf = shard['ffn1_w_in'].shape[0]

    def gather(*names):
        return _gather_plan([shard[nm] for nm in names])

    ((_, (w1in, cw_g), _),) = _run_plans([_gather_plan([shard['ffn1_w_in'], cw_pad])], name="gather_ffn1_in")
    cw_full = jnp.transpose(cw_g[:, 0:3, 0:cw_cols], (1, 0, 2)).reshape(3, N_DEV * cw_cols)
    cw8 = jnp.zeros((8, CONV_CH), F32).at[0:3, :].set(cw_full)

    s5_in = (ssm_lam_re[0], ssm_lam_im[0], ssm_log_step[0], ssm_b_re[0], ssm_b_im[0])
    (a_re, a_im, bb_re, bb_im), s5_vjp = jax.vjp(_s5_discretise, *s5_in)
    tab_f = _pow_table(a_re, a_im)
    tab_b = _pow_table(a_re, -a_im, descending=True)
    bmat_b = _compact(jnp.transpose(bb_re, (0, 2, 1)), jnp.transpose(bb_im, (0, 2, 1))).astype(BF16)
    cmat_tb = _compact(ssm_c_re[0], -ssm_c_im[0]).astype(BF16)
    dvec = ssm_d[0].reshape(1, SSM_W)

    xf = x[0]
    x_b = xf.astype(BF16)
    p_b = p[0, 0].astype(BF16)
    tgt = loss_target[0]
    tq = min(512, t)

    ffn_out = dict(ja='c', jb='c', nj=4, tm=tq, tn=d, tk=wf)
    def side_by_side(wb):
        return jnp.transpose(wb, (1, 0, 2)).reshape(wb.shape[1], N_DEV * wb.shape[2])

    a1, h1, ((w1out_g,),) = _ffn_in(x_b, w1in, name="ffn1_in", tm=tq, plans=[gather('ffn1_w_out')])
    w1out = w1out_g.reshape(4, wf, d)
    (r1, x1, x1b), ((wmix,),) = _mm(a1, w1out, name="ffn1_out", **ffn_out, plans=[gather('mix_w_in')],
                                    epilogue=_ln_epilogue(xf, ln1_g, ln1_b, 0.5))
    proj, ((wco, wglu, wmo_g),) = _mm(x1b, wmix, name="mix_in", jb='b', jo='b', o_flat=True, nj=8, tm=tq, tn=512,
                                      tk=d, plans=[gather('conv_w_out', 'ssm_w_glu', 'mix_w_out')])
    wmo = wmo_g.reshape(d, d)
    ycin = _conv_fwd(proj, cw8, conv_b, name="conv_fwd")
    wco, wglu = side_by_side(wco), side_by_side(wglu)
    yconv = _mm(ycin, wco, name="conv_out", tm=tq, tn=d, tk=CONV_CH)
    s_f, s_b, ((w2in,),) = _scan_fwd(proj, bmat_b, tab_f, name="scan_fwd", plans=[gather('ffn2_w_in')])
    blk = dict(ja='b', jb='b', jo='b', nj=SCAN_NCB)
    ymm = _mm(s_b, cmat_tb, name="ssm_read", a_flat=True, o_flat=True, tb=True, tm=tq, tn=SCAN_UW, tk=2 * SCAN_CB, **blk)
    ys, sg, u_b = _s5_out(ymm, proj, dvec, name="ssm_out")
    glu = _mm(sg, wglu, name="glu_in", tm=tq, tn=d, tk=SSM_W)
    merged = _gate_fwd(yconv, glu, proj, name="gate_fwd")
    r2, x2, x2b = _mm(merged, wmo, name="mix_out", tm=tq, tn=d, tk=d, epilogue=_ln_epilogue(x1, ln2_g, ln2_b, 1.0))
    a2, h2, ((w2out_g,),) = _ffn_in(x2b, w2in, name="ffn2_in", tm=tq, plans=[gather('ffn2_w_out')])
    w2out = w2out_g.reshape(4, wf, d)
    (r3, x3, x3b), ((wpin, wgate_g),) = _mm(a2, w2out, name="ffn2_out", **ffn_out, plans=[gather('ple_w_in', 'ple_w_gate')],
                                            epilogue=_ln_epilogue(x2, ln3_g, ln3_b, 0.5))
    wgate = wgate_g.reshape(d, d)
    pe = _mm(p_b, side_by_side(wpin), name="ple_in", tm=tq, tn=d, tk=p_b.shape[1])
    gp = _mm(x3b, wgate, name="ple_gate", tm=tq, tn=d, tk=d)

    dr4, dpe_b, dgp_b, dg4, db4, loss_part = _ln_bwd(x3, [pe, gp], ln4_g, ln4_b, [], name="ple_ln_bwd",
                                                     ple=True, target=tgt)
    gb, sib, rem = {}, {}, {}
    ids = jnp.stack([_dev(1 - xc_, yc_, cc_), _dev(xc_, 1 - yc_, cc_), _dev(1 - xc_, 1 - yc_, cc_)]).astype(jnp.int32)

    def blocked(nm, g):
        return g.reshape((N_DEV,) + args[nm].shape[1:])

    def to_sibling(*names):
        return _sibling_plan([gb[nm] for nm in names])

    def chip_sums(names, sibs):
        for nm, s in zip(names, sibs):
            sib[nm] = s
        return [_chip_partial(gb[nm], sib[nm], ids, name=f"chip_sum_{nm}") for nm in names]

    ffn_in_dg = dict(ja='c', jb='c', nj=8, tm=tq, tn=d, tk=wf)
    ffn_in_wg = dict(ja='b', jo='b', ta=True, nj=8, tm=wf, tn=d, tk=t)
    ffn_out_wg = dict(ja='b', jo='b', ta=True, nj=4, tm=wf, tn=d, tk=t)

    gb['ple_w_in'] = _mm(p_b, dpe_b, name="ple_in_wg", jb='b', jo='b', b_flat=True, ta=True, nj=8,
                         tm=p_b.shape[1], tn=128, tk=t)
    gb['ple_w_gate'] = blocked('ple_w_gate', _mm(x3b, dgp_b, name="ple_gate_wg", ta=True, tm=d, tn=d, tk=t))
    g_ple = ['ple_w_in', 'ple_w_gate']
    dx3_g, (s_,) = _mm(dgp_b, wgate, name="ple_gate_dg", tb=True, tm=tq, tn=d, tk=d, plans=[to_sibling(*g_ple)])
    part = chip_sums(g_ple, s_)

    dr3, df2_b, dg3, db3 = _ln_bwd(r3, [], ln3_g, ln3_b, [(dr4, ALPHA), (dx3_g, 1.0)], name="ffn2_ln_bwd", fs=0.5)
    dh2, _ = _ffn_out_dg(df2_b, w2out, h2, name="ffn2_out_dg", tm=tq)
    g_, (r_,) = _mm(a2, df2_b, name="ffn2_out_wg", **ffn_out_wg, plans=[_chip_plan(part)])
    gb['ffn2_w_out'] = blocked('ffn2_w_out', g_)
    rem['ple_w_in'], rem['ple_w_gate'] = r_[0:3], r_[3:6]
    dx2_f, (s_,) = _mm(dh2, w2in, name="ffn2_in_dg", **ffn_in_dg, plans=[to_sibling('ffn2_w_out')])
    part = chip_sums(['ffn2_w_out'], s_)
    gb['ffn2_w_in'], (r_,) = _mm(dh2, x2b, name="ffn2_in_wg", **ffn_in_wg, plans=[_chip_plan(part)])
    rem['ffn2_w_out'] = r_

    dr2, dmix_b, dg2, db2 = _ln_bwd(r2, [], ln2_g, ln2_b, [(dr3, ALPHA), (dx2_f, 1.0)], name="mix_ln_bwd")
    dmerged, (s_,) = _mm(dmix_b, wmo, name="mix_out_dg", tb=True, tm=tq, tn=d, tk=d, plans=[to_sibling('ffn2_w_in')])
    part = chip_sums(['ffn2_w_in'], s_)
    gb['mix_w_out'] = blocked('mix_w_out', _mm(merged, dmix_b, name="mix_out_wg", ta=True, tm=d, tn=d, tk=t))
    dyconv_b, dglu_b, dgate_b = _gate_bwd(dmerged, yconv, glu, proj, name="gate_bwd")
    gb['conv_w_out'] = _mm(ycin, dyconv_b, name="conv_out_wg", jb='b', jo='b', b_flat=True, ta=True, nj=8,
                           tm=CONV_CH, tn=128, tk=t)
    dycin = _mm(dyconv_b, wco, name="conv_out_dg", tb=True, tm=tq, tn=CONV_CH, tk=d)
    gb['ssm_w_glu'] = _mm(sg, dglu_b, name="glu_in_wg", jb='b', jo='b', b_flat=True, ta=True, nj=8,
                          tm=SSM_W, tn=256, tk=t)
    g_mix = ['mix_w_out', 'conv_w_out', 'ssm_w_glu']
    dsg, (s_,) = _mm(dglu_b, wglu, name="glu_in_dg", tb=True, tm=tq, tn=SSM_W, tk=2 * d, plans=[to_sibling(*g_mix)])
    part_mix = chip_sums(g_mix, s_)
    dys, dys_b, dd = _s5_bwd_in(dsg, ys, proj, name="ssm_out_bwd")
    h_b, da_acc, (rem['ffn2_w_in'],) = _scan_bwd(dys_b, cmat_tb, s_f, tab_b, name="scan_bwd", plans=[_chip_plan(part)])
    dumm = _mm(h_b, bmat_b, name="ssm_write_dg", a_flat=True, o_flat=True, tb=True, tm=tq, tn=SCAN_UW,
               tk=2 * SCAN_CB, **blk)
    du_b = _s5_du(dumm, dys, dvec, name="ssm_du")
    g_bmat = _mm(u_b, h_b, name="ssm_write_wg", a_flat=True, b_flat=True, ta=True, tm=SCAN_UW,
                 tn=2 * SCAN_CB, tk=t, **blk)
    g_cmat = _mm(dys_b, s_b, name="ssm_read_wg", a_flat=True, b_flat=True, ta=True, tm=SCAN_UW,
                 tn=2 * SCAN_CB, tk=t, **blk)
    dcb_b, dcc_b, dch_b, dconv = _conv_bwd(dycin, proj, cw8, conv_b, name="conv_bwd")
    dproj = jnp.concatenate([dcb_b, dcc_b, dch_b, du_b, dgate_b], axis=1)
    gb['mix_w_in'], (r_,) = _mm(x1b, dproj, name="mix_in_wg", jb='b', jo='b', b_flat=True, ta=True, nj=8,
                                tm=d, tn=512, tk=t, plans=[_chip_plan(part_mix)])
    for i, nm in enumerate(g_mix):
        rem[nm] = r_[3 * i:3 * i + 3]
    dx1_m, (s_,) = _mm(dproj, wmix, name="mix_in_dg", ja='c', jb='c', a_flat=True, tb=True, nj=8,
                       tm=tq, tn=d, tk=512, plans=[to_sibling('mix_w_in')])
    part = chip_sums(['mix_w_in'], s_)

    dr1, df1_b, dg1, db1 = _ln_bwd(r1, [], ln1_g, ln1_b, [(dr2, ALPHA), (dx1_m, 1.0)], name="ffn1_ln_bwd", fs=0.5)
    dh1, (rem['mix_w_in'],) = _ffn_out_dg(df1_b, w1out, h1, name="ffn1_out_dg", tm=tq, plans=[_chip_plan(part)])

    da_sum = jnp.sum(da_acc, axis=0)
    da_re, da_im = _unperm_cols(da_sum)
    gbb_re, gbb_im = [jnp.transpose(v, (0, 2, 1)) for v in _compact_extract(g_bmat, SSM_GROUP)]
    g_c_re, g_c_im_neg = _compact_extract(g_cmat, SSM_GROUP)
    g_c_im = -g_c_im_neg
    g_lam_re, g_lam_im, g_log_step, g_b_re, g_b_im = s5_vjp(
        (da_re.reshape(SSM_GROUPS, SSM_STATE), da_im.reshape(SSM_GROUPS, SSM_STATE), gbb_re, gbb_im))
    g_d = dd.reshape(SSM_GROUPS, SSM_GROUP)

    small_g = {'ln1_g': dg1, 'ln1_b': db1, 'conv_b': dconv[3:4], 'ssm_lam_re': g_lam_re, 'ssm_lam_im': g_lam_im,
               'ssm_log_step': g_log_step, 'ssm_b_re': g_b_re, 'ssm_b_im': g_b_im, 'ssm_c_re': g_c_re,
               'ssm_c_im': g_c_im, 'ssm_d': g_d, 'ln2_g': dg2, 'ln2_b': db2, 'ln3_g': dg3, 'ln3_b': db3,
               'ln4_g': dg4, 'ln4_b': db4}
    small_shapes = [args[nm].shape for nm in small] + [(3, CONV_CH), (1,)]
    g_pack = _pack([small_g[nm] for nm in small] + [dconv[0:3], loss_part[0:1, 0:1]])

    gb['ffn1_w_in'], ((g_all,),) = _mm(dh1, x_b, name="ffn1_in_wg", **ffn_in_wg, plans=[_gather_plan([g_pack])])
    g_, (s_,) = _mm(a1, df1_b, name="ffn1_out_wg", **ffn_out_wg, plans=[to_sibling('ffn1_w_in')])
    gb['ffn1_w_out'] = blocked('ffn1_w_out', g_)
    part = chip_sums(['ffn1_w_in'], s_)
    (grad_x,), (r_, s_) = _mm(dh1, w1in, name="ffn1_in_dg", **ffn_in_dg,
                              plans=[_chip_plan(part), to_sibling('ffn1_w_out')],
                              epilogue=(lambda pr, drv: (pr + ALPHA * drv,), (dr1,), (), (F32,)))
    rem['ffn1_w_in'] = r_
    part = chip_sums(['ffn1_w_out'], s_)
    ((_, rem['ffn1_w_out'], _),) = _run_plans([_chip_plan(part)], name="grad_chips_ffn1_out")

    def full_cw(a):
        return lax.dynamic_update_slice(jnp.zeros((3, CONV_CH), F32), a[0], (0, me * cw_cols))

    zero1 = jnp.zeros((1,), F32)
    w_pack = _pack([args[nm] for nm in small] + [full_cw(conv_w), zero1])
    m_pack = _pack([args['m_' + nm] for nm in small] + [full_cw(m_conv_w), zero1])
    v_pack = _pack([args['v_' + nm] for nm in small] + [full_cw(v_conv_w), zero1])
    sg_sum, sd, sm, sv = _small_update(g_all, w_pack, m_pack, v_pack, name="small_update")
    res = {}
    for key, buf in (('grad_', sg_sum), ('delta_', sd), ('new_m_', sm), ('new_v_', sv)):
        parts = _unpack(buf, small_shapes)
        for nm, val in zip(small, parts[:len(small)]):
            res[key + nm] = val
        res[key + 'conv_w'] = lax.dynamic_slice(parts[len(small)], (0, me * cw_cols), (3, cw_cols))[None]
        if key == 'grad_':
            loss = parts[-1][0]

    me1 = me.reshape(1)
    for nm in big:
        upd = _shard_update(gb[nm], sib[nm], rem[nm], me1, local(args[nm], nm), local(args['m_' + nm], nm),
                            local(args['v_' + nm], nm), name=f"update_{nm}")
        for key, val in zip(('grad_', 'delta_', 'new_m_', 'new_v_'), upd):
            res[key + nm] = (jnp.swapaxes(val, 0, 1) if nm in turned else val)[None]

    outs = [loss, grad_x[None]]
    for key in ('grad_', 'delta_', 'new_m_', 'new_v_'):
        outs += [res[key + nm] for nm in order]
    return tuple(outs)
```

```python
import functools
import math

import jax
import jax.numpy as jnp
from jax import lax
from jax.experimental import pallas as pl
from jax.experimental.pallas import tpu as pltpu

F32 = jnp.float32
BF16 = jnp.bfloat16
MESH = pl.DeviceIdType.MESH

N_DEV = 8
ALPHA = 2.0 ** 0.25
LN_EPS = 1e-5
CONV_CH = 512
SSM_W = 512
SSM_GROUPS = 32
SSM_GROUP = 16
SSM_STATE = 64
SSM_CH = SSM_GROUPS * SSM_STATE
SCAN_CB = 512
SCAN_NCB = SSM_CH // SCAN_CB
SCAN_GPB = SSM_GROUPS // SCAN_NCB
SCAN_UW = SCAN_GPB * SSM_GROUP
SCAN_TT = 256
ADAM_LR = 0.001
ADAM_B1 = 0.9
ADAM_B2 = 0.999
ADAM_EPS = 1e-08
ADAM_WD = 0.01
ADAM_STEP = 10
VMEM_LIMIT = 56 * 1024 * 1024


def _cparams(sem=None, **kw):
    return pltpu.CompilerParams(dimension_semantics=sem, vmem_limit_bytes=VMEM_LIMIT, **kw)


def _mm(a, b, *, name, ja=None, jb=None, jo=None, a_flat=False, b_flat=False, o_flat=False,
        ta=False, tb=False, tm, tn, tk, nj=1, out_dtype=F32, plans=(), epilogue=None):
    def dims(arr, j, flat):
        if j is None:
            return arr.shape
        if flat:
            return (arr.shape[0], arr.shape[1] // nj)
        assert arr.shape[0] == nj, (name, arr.shape, nj)
        return arr.shape[1:]

    ar, ac = dims(a, ja, a_flat)
    br, bc = dims(b, jb, b_flat)
    m, k = (ac, ar) if ta else (ar, ac)
    k2, n = (bc, br) if tb else (br, bc)
    assert k == k2, (name, a.shape, b.shape)
    assert m % tm == 0 and n % tn == 0 and k % tk == 0, (name, m, n, k, tm, tn, tk)
    njb = nj if 'b' in (ja, jb) else 1
    njc = nj if 'c' in (ja, jb) else 1
    nk = k // tk
    j_inside = njc > 1 and nk == 1 and not ta
    n_in = njc if j_inside else 1
    nred = nk if j_inside else njc * nk
    grid = (njb, m // tm, n // tn, 1 if j_inside else njc, nk)

    def make_spec(j, flat, blk, rfn, cfn, cols_per_j):
        def jsel(g, c):
            return g if j == 'b' else c
        if j is None:
            return pl.BlockSpec(blk, lambda g, i, jn, c, kk: (rfn(i, jn, kk), cfn(i, jn, kk)))
        if j == 'c' and j_inside:
            if flat:
                return pl.BlockSpec((blk[0], nj * blk[1]), lambda g, i, jn, c, kk: (rfn(i, jn, kk), 0))
            return pl.BlockSpec((nj,) + blk, lambda g, i, jn, c, kk: (0, rfn(i, jn, kk), cfn(i, jn, kk)))
        if flat:
            nb = cols_per_j // blk[1]
            return pl.BlockSpec(blk, lambda g, i, jn, c, kk: (rfn(i, jn, kk), jsel(g, c) * nb + cfn(i, jn, kk)))
        return pl.BlockSpec((None,) + blk,
                            lambda g, i, jn, c, kk: (jsel(g, c), rfn(i, jn, kk), cfn(i, jn, kk)))

    if ta:
        a_spec = make_spec(ja, a_flat, (tk, tm), lambda i, jn, kk: kk, lambda i, jn, kk: i, ac)
    else:
        a_spec = make_spec(ja, a_flat, (tm, tk), lambda i, jn, kk: i, lambda i, jn, kk: kk, ac)
    if tb:
        b_spec = make_spec(jb, b_flat, (tn, tk), lambda i, jn, kk: jn, lambda i, jn, kk: kk, bc)
    else:
        b_spec = make_spec(jb, b_flat, (tk, tn), lambda i, jn, kk: kk, lambda i, jn, kk: jn, bc)
    o_spec = make_spec(jo, o_flat, (tm, tn), lambda i, jn, kk: i, lambda i, jn, kk: jn, n)
    if jo is None:
        out_shape = (m, n)
    elif o_flat:
        out_shape = (m, nj * n)
    else:
        out_shape = (nj, m, n)

    dn = (((0 if ta else 1,), (1 if tb else 0,)), ((), ()))

    def operand(ref, j, flat, jj, width):
        if not (j == 'c' and j_inside):
            return ref[...]
        return ref[:, jj * width:(jj + 1) * width] if flat else ref[jj]

    e_fn, e_rows, e_vecs, e_dtypes = epilogue if epilogue else (None, (), (), (out_dtype,))
    assert not epilogue or (nred == 1 and jo is None), name
    n_e = len(e_rows) + len(e_vecs)
    n_o = len(e_dtypes)

    def body(a_ref, b_ref, *rest):
        e_refs, o_refs, scratch = rest[:n_e], rest[n_e:n_e + n_o], rest[n_e + n_o:]
        o_ref = o_refs[0]
        p = None
        for jj in range(n_in):
            q = lax.dot_general(operand(a_ref, ja, a_flat, jj, tk), operand(b_ref, jb, b_flat, jj, tk if tb else tn),
                                dn, preferred_element_type=F32)
            p = q if p is None else p + q
        if epilogue:
            for ref, val in zip(o_refs, e_fn(p, *[r[...] for r in e_refs])):
                ref[...] = val.astype(ref.dtype)
        elif nred == 1:
            o_ref[...] = p.astype(o_ref.dtype)
        else:
            acc = scratch[0]
            r = pl.program_id(3) * nk + pl.program_id(4)

            @pl.when(r == 0)
            def _():
                acc[...] = p

            @pl.when(r > 0)
            def _():
                acc[...] += p

            @pl.when(r == nred - 1)
            def _():
                o_ref[...] = acc[...].astype(o_ref.dtype)

    vec_spec = pl.BlockSpec((1, tn), lambda g, i, jn, c, kk: (0, jn))
    res = _call_with_plans(
        body, plans, name=name, grid=grid,
        in_specs=[a_spec, b_spec] + [o_spec] * len(e_rows) + [vec_spec] * len(e_vecs), out_specs=[o_spec] * n_o,
        out_shape=[jax.ShapeDtypeStruct(out_shape, dt) for dt in e_dtypes],
        scratch_shapes=[] if nred == 1 else [pltpu.VMEM((tm, tn), F32)],
        semantics=("parallel", "parallel", "parallel", "arbitrary", "arbitrary"), operands=(a, b, *e_rows, *e_vecs))
    outs = res[0] if epilogue else res[0][0]
    return (outs, res[1]) if plans else outs


def _sigmoid(v):
    return jax.nn.sigmoid(v)


def _row_spec(tm, cols, colblk=0):
    return pl.BlockSpec((tm, cols), lambda i: (i, colblk))


def _vec_spec(cols):
    return pl.BlockSpec((1, cols), lambda i: (0, 0))


def _ffn_in(xb, win, *, name, tm, plans=()):
    t, d = xb.shape
    nj, w, _ = win.shape
    half = nj // 2
    dn = (((1,), (1,)), ((), ()))

    def body(x_ref, wg_ref, wu_ref, a_ref, gu_ref):
        xv = x_ref[...]
        g = lax.dot_general(xv, wg_ref[...], dn, preferred_element_type=F32)
        u = lax.dot_general(xv, wu_ref[...], dn, preferred_element_type=F32)
        a_ref[...] = (g * _sigmoid(g) * u).astype(BF16)
        gu_ref[0] = g.astype(BF16)
        gu_ref[1] = u.astype(BF16)

    (a, gu), riders = _call_with_plans(
        body, plans, name=name, grid=(half, t // tm),
        in_specs=[pl.BlockSpec((tm, d), lambda j, i: (i, 0)),
                  pl.BlockSpec((None, w, d), lambda j, i: (j, 0, 0)),
                  pl.BlockSpec((None, w, d), lambda j, i: (j + half, 0, 0))],
        out_specs=[pl.BlockSpec((None, tm, w), lambda j, i: (j, i, 0)),
                   pl.BlockSpec((2, None, tm, w), lambda j, i: (0, j, i, 0))],
        out_shape=[jax.ShapeDtypeStruct((half, t, w), BF16), jax.ShapeDtypeStruct((2, half, t, w), BF16)],
        scratch_shapes=[], semantics=("parallel", "parallel"), operands=(xb, win, win))
    return a, gu, riders


def _ffn_out_dg(dfb, wout, gu, *, name, tm, plans=()):
    t, d = dfb.shape
    half, w, _ = wout.shape
    dn = (((1,), (1,)), ((), ()))

    def body(df_ref, w_ref, gu_ref, dh_ref):
        da = lax.dot_general(df_ref[...], w_ref[...], dn, preferred_element_type=F32)
        g = gu_ref[0].astype(F32)
        u = gu_ref[1].astype(F32)
        sg = _sigmoid(g)
        dh_ref[0] = (da * u * (sg * (1.0 + g * (1.0 - sg)))).astype(BF16)
        dh_ref[1] = (da * (g * sg)).astype(BF16)

    (out,), riders = _call_with_plans(
        body, plans, name=name, grid=(half, t // tm),
        in_specs=[pl.BlockSpec((tm, d), lambda j, i: (i, 0)),
                  pl.BlockSpec((None, w, d), lambda j, i: (j, 0, 0)),
                  pl.BlockSpec((2, None, tm, w), lambda j, i: (0, j, i, 0))],
        out_specs=[pl.BlockSpec((2, None, tm, w), lambda j, i: (0, j, i, 0))],
        out_shape=[jax.ShapeDtypeStruct((2, half, t, w), BF16)],
        scratch_shapes=[], semantics=("parallel", "parallel"), operands=(dfb, wout, gu))
    return out.reshape(2 * half, t, w), riders


def _ln_stats(r):
    mu = jnp.mean(r, axis=-1, keepdims=True)
    xc = r - mu
    var = jnp.mean(xc * xc, axis=-1, keepdims=True)
    rstd = lax.rsqrt(var + LN_EPS)
    return xc * rstd, rstd


def _ln_epilogue(xin, gamma, beta, fs):
    def fn(p, xv, g, b):
        r = ALPHA * xv + fs * p
        xh, _ = _ln_stats(r)
        y = xh * g + b
        return r, y, y

    return fn, (xin,), (gamma, beta), (F32, F32, BF16)


def _ln_bwd(xin, fparts, gamma, beta, grads, *, name, fs=1.0, ple=False, target=None, tm=512):
    t, d = xin.shape
    nf = len(fparts)
    ng = len(grads)
    coefs = [c for _, c in grads]
    use_t = target is not None
    n_fout = 2 if ple else 1

    def body(*refs):
        pos = 0
        x_ref = refs[pos]; pos += 1
        f_refs = refs[pos:pos + nf]; pos += nf
        g_ref, b_ref = refs[pos:pos + 2]; pos += 2
        gr_refs = refs[pos:pos + ng]; pos += ng
        if use_t:
            t_ref = refs[pos]; pos += 1
        dr_ref = refs[pos]; pos += 1
        fo_refs = refs[pos:pos + n_fout]; pos += n_fout
        dg_ref, db_ref = refs[pos:pos + 2]; pos += 2
        if use_t:
            loss_ref = refs[pos]; pos += 1
        i = pl.program_id(0)

        if ple:
            pe = f_refs[0][...]
            sg = _sigmoid(f_refs[1][...])
            resid = ALPHA * x_ref[...] + pe * sg
        else:
            resid = x_ref[...]
        xh, rstd = _ln_stats(resid)
        gam = g_ref[...]
        if use_t:
            diff = xh * gam + b_ref[...] - t_ref[...]
            dy = diff * (1.0 / d)
            lpart = 0.5 * jnp.sum(jnp.sum(diff * diff, axis=-1, keepdims=True), axis=0, keepdims=True) * (1.0 / d)
        else:
            dy = coefs[0] * gr_refs[0][...]
            for c, r in zip(coefs[1:], gr_refs[1:]):
                dy = dy + c * r[...]
        dxh = dy * gam
        m1 = jnp.mean(dxh, axis=-1, keepdims=True)
        m2 = jnp.mean(dxh * xh, axis=-1, keepdims=True)
        dr = rstd * (dxh - m1 - xh * m2)
        dr_ref[...] = dr
        if ple:
            fo_refs[0][...] = (dr * sg).astype(BF16)
            fo_refs[1][...] = (dr * pe * (sg * (1.0 - sg))).astype(BF16)
        else:
            fo_refs[0][...] = (fs * dr).astype(BF16)
        dgp = jnp.sum(dy * xh, axis=0, keepdims=True)
        dbp = jnp.sum(dy, axis=0, keepdims=True)

        @pl.when(i == 0)
        def _():
            dg_ref[...] = dgp
            db_ref[...] = dbp
            if use_t:
                loss_ref[...] = jnp.broadcast_to(lpart, loss_ref.shape)

        @pl.when(i > 0)
        def _():
            dg_ref[...] += dgp
            db_ref[...] += dbp
            if use_t:
                loss_ref[...] += jnp.broadcast_to(lpart, loss_ref.shape)

    ins = [xin, *fparts, gamma, beta, *[g for g, _ in grads]] + ([target] if use_t else [])
    in_specs = ([_row_spec(tm, d)] * (1 + nf) + [_vec_spec(d), _vec_spec(d)] + [_row_spec(tm, d)] * ng
                + ([_row_spec(tm, d)] if use_t else []))
    out_specs = [_row_spec(tm, d)] * (1 + n_fout) + [_vec_spec(d), _vec_spec(d)] + ([_vec_spec(128)] if use_t else [])
    out_shape = ([jax.ShapeDtypeStruct((t, d), F32)] + [jax.ShapeDtypeStruct((t, d), BF16)] * n_fout
                 + [jax.ShapeDtypeStruct((1, d), F32)] * 2 + ([jax.ShapeDtypeStruct((1, 128), F32)] if use_t else []))
    return pl.pallas_call(
        body, name=name, grid=(t // tm,), in_specs=in_specs, out_specs=out_specs, out_shape=out_shape,
        compiler_params=_cparams(("arbitrary",)),
    )(*ins)


def _conv_fwd(proj, cw, cb, *, name, tm=512):
    t = proj.shape[0]
    c = CONV_CH
    hb = tm // 8

    def body(b_ref, c_ref, h_ref, cp_ref, hp_ref, w_ref, bias_ref, o_ref, q_scr):
        i = pl.program_id(0)
        q = c_ref[...] * h_ref[...]
        halo = jnp.where(i > 0, cp_ref[...] * hp_ref[...], 0.0)
        q_scr[0:8, :] = halo
        q_scr[8:, :] = q
        z = (w_ref[2:3, :] * q + w_ref[1:2, :] * q_scr[pl.ds(7, tm), :] + w_ref[0:1, :] * q_scr[pl.ds(6, tm), :]
             + bias_ref[...])
        o_ref[...] = (b_ref[...] * z).astype(BF16)

    prev = lambda blk: pl.BlockSpec((8, c), lambda i: (jnp.maximum(i * hb - 1, 0), blk))
    return pl.pallas_call(
        body, name=name, grid=(t // tm,),
        in_specs=[_row_spec(tm, c, 0), _row_spec(tm, c, 1), _row_spec(tm, c, 2), prev(1), prev(2),
                  pl.BlockSpec((8, c), lambda i: (0, 0)), _vec_spec(c)],
        out_specs=_row_spec(tm, c),
        out_shape=jax.ShapeDtypeStruct((t, c), BF16),
        scratch_shapes=[pltpu.VMEM((tm + 8, c), F32)],
        compiler_params=_cparams(("parallel",)),
    )(proj, proj, proj, proj, proj, cw, cb)


def _conv_bwd(dyc, proj, cw, cb, *, name, tm=512):
    t = proj.shape[0]
    c = CONV_CH
    hb = tm // 8
    nblk = t // 8

    def body(d_ref, b_ref, c_ref, h_ref, cp_ref, hp_ref, dn_ref, bn_ref, w_ref, bias_ref,
             db_ref, dc_ref, dh_ref, dw_ref, q_scr, z_scr):
        i = pl.program_id(0)
        last = pl.num_programs(0) - 1
        cc = c_ref[...]
        ch = h_ref[...]
        q = cc * ch
        halo = jnp.where(i > 0, cp_ref[...] * hp_ref[...], 0.0)
        q_scr[0:8, :] = halo
        q_scr[8:, :] = q
        w0, w1, w2 = w_ref[0:1, :], w_ref[1:2, :], w_ref[2:3, :]
        qm1 = q_scr[pl.ds(7, tm), :]
        qm2 = q_scr[pl.ds(6, tm), :]
        z = w2 * q + w1 * qm1 + w0 * qm2 + bias_ref[...]
        d = d_ref[...]
        bb = b_ref[...]
        db_ref[...] = (d * z).astype(BF16)
        dz = d * bb
        z_scr[0:tm, :] = dz
        z_scr[tm:, :] = jnp.where(i < last, dn_ref[...] * bn_ref[...], 0.0)
        dq = w2 * dz + w1 * z_scr[pl.ds(1, tm), :] + w0 * z_scr[pl.ds(2, tm), :]
        dc_ref[...] = (dq * ch).astype(BF16)
        dh_ref[...] = (dq * cc).astype(BF16)
        row = lax.broadcasted_iota(jnp.int32, (8, c), 0)
        part = jnp.zeros((8, c), F32)
        for k, term in enumerate((dz * qm2, dz * qm1, dz * q, dz)):
            part = jnp.where(row == k, jnp.sum(term, axis=0, keepdims=True), part)

        @pl.when(i == 0)
        def _():
            dw_ref[...] = part

        @pl.when(i > 0)
        def _():
            dw_ref[...] += part

    prev = lambda blk: pl.BlockSpec((8, c), lambda i: (jnp.maximum(i * hb - 1, 0), blk))
    nxt_p = pl.BlockSpec((8, c), lambda i: (jnp.minimum((i + 1) * hb, nblk - 1), 0))
    nxt_d = pl.BlockSpec((8, c), lambda i: (jnp.minimum((i + 1) * hb, nblk - 1), 0))
    return pl.pallas_call(
        body, name=name, grid=(t // tm,),
        in_specs=[_row_spec(tm, c), _row_spec(tm, c, 0), _row_spec(tm, c, 1), _row_spec(tm, c, 2),
                  prev(1), prev(2), nxt_d, nxt_p, pl.BlockSpec((8, c), lambda i: (0, 0)), _vec_spec(c)],
        out_specs=[_row_spec(tm, c)] * 3 + [pl.BlockSpec((8, c), lambda i: (0, 0))],
        out_shape=[jax.ShapeDtypeStruct((t, c), BF16)] * 3 + [jax.ShapeDtypeStruct((8, c), F32)],
        scratch_shapes=[pltpu.VMEM((tm + 8, c), F32), pltpu.VMEM((tm + 8, c), F32)],
        compiler_params=_cparams(("arbitrary",)),
    )(dyc, proj, proj, proj, proj, proj, dyc, proj, cw, cb)


def _gate_fwd(yconv, glu, proj, *, name, tm=512):
    t, d = yconv.shape

    def body(yc_ref, ga_ref, gb_ref, gc_ref, gs_ref, o_ref):
        yssm = ga_ref[...] * _sigmoid(gb_ref[...])
        o_ref[...] = (_sigmoid(gc_ref[...]) * yc_ref[...] + _sigmoid(gs_ref[...]) * yssm).astype(BF16)

    return pl.pallas_call(
        body, name=name, grid=(t // tm,),
        in_specs=[_row_spec(tm, d), _row_spec(tm, d, 0), _row_spec(tm, d, 1), _row_spec(tm, d, 2), _row_spec(tm, d, 3)],
        out_specs=_row_spec(tm, d), out_shape=jax.ShapeDtypeStruct((t, d), BF16),
        compiler_params=_cparams(("parallel",)),
    )(yconv, glu, glu, proj, proj)


def _gate_bwd(dm, yconv, glu, proj, *, name, tm=512):
    t, d = yconv.shape

    def body(dm_ref, yc_ref, ga_ref, gb_ref, gc_ref, gs_ref, dyc_ref, dglu_ref, dgate_ref):
        dmv = dm_ref[...]
        sc = _sigmoid(gc_ref[...])
        ss = _sigmoid(gs_ref[...])
        sb = _sigmoid(gb_ref[...])
        ga = ga_ref[...]
        yssm = ga * sb
        dyc_ref[...] = (dmv * sc).astype(BF16)
        dgate_ref[:, 0:d] = (dmv * yc_ref[...] * (sc * (1.0 - sc))).astype(BF16)
        dys = dmv * ss
        dgate_ref[:, d:2 * d] = (dmv * yssm * (ss * (1.0 - ss))).astype(BF16)
        dglu_ref[:, 0:d] = (dys * sb).astype(BF16)
        dglu_ref[:, d:2 * d] = (dys * ga * (sb * (1.0 - sb))).astype(BF16)

    return pl.pallas_call(
        body, name=name, grid=(t // tm,),
        in_specs=[_row_spec(tm, d), _row_spec(tm, d), _row_spec(tm, d, 0), _row_spec(tm, d, 1),
                  _row_spec(tm, d, 2), _row_spec(tm, d, 3)],
        out_specs=[_row_spec(tm, d), _row_spec(tm, 2 * d), _row_spec(tm, 2 * d)],
        out_shape=[jax.ShapeDtypeStruct((t, d), BF16), jax.ShapeDtypeStruct((t, 2 * d), BF16),
                   jax.ShapeDtypeStruct((t, 2 * d), BF16)],
        compiler_params=_cparams(("parallel",)),
    )(dm, yconv, glu, glu, proj, proj)


_GELU_C = math.sqrt(2.0 / math.pi)


def _gelu(v):
    return 0.5 * v * (1.0 + jnp.tanh(_GELU_C * (v + 0.044715 * v * v * v)))


def _gelu_grad(v):
    th = jnp.tanh(_GELU_C * (v + 0.044715 * v * v * v))
    return 0.5 * (1.0 + th) + 0.5 * v * (1.0 - th * th) * (_GELU_C * (1.0 + 3.0 * 0.044715 * v * v))


def _cmul(ar, ai, br, bi):
    return ar * br - ai * bi, ar * bi + ai * br


def _scan_fwd(proj, bmat, tab, *, name, plans=()):
    t = proj.shape[0]
    tt, cbw = SCAN_TT, SCAN_CB
    w2 = 2 * cbw

    def body(u_ref, b_ref, tab_ref, s_ref, sb_ref, bu_scr, carry):
        ti = pl.program_id(1)

        @pl.when(ti == 0)
        def _():
            carry[...] = jnp.zeros_like(carry)

        bu_scr[...] = jnp.dot(u_ref[...].astype(BF16), b_ref[...], preferred_element_type=F32)
        row = lax.broadcasted_iota(jnp.int32, (8, cbw), 0)

        def blk(bi, c):
            cr, ci = c
            r0 = pl.multiple_of(bi * 8, 8)
            xr = bu_scr[pl.ds(r0, 8), 0:cbw]
            xi = bu_scr[pl.ds(r0, 8), cbw:w2]
            for k, sh in enumerate((1, 2, 4)):
                kr = tab_ref[k:k + 1, 0:cbw]
                ki = tab_ref[k:k + 1, cbw:w2]
                sr = jnp.where(row >= sh, pltpu.roll(xr, sh, 0), 0.0)
                si = jnp.where(row >= sh, pltpu.roll(xi, sh, 0), 0.0)
                pr, pi = _cmul(kr, ki, sr, si)
                xr = xr + pr
                xi = xi + pi
            pr, pi = _cmul(tab_ref[8:16, 0:cbw], tab_ref[8:16, cbw:w2], cr, ci)
            xr = xr + pr
            xi = xi + pi
            s_ref[pl.ds(r0, 8), 0:cbw] = xr
            s_ref[pl.ds(r0, 8), cbw:w2] = xi
            return (jnp.broadcast_to(xr[7:8, :], (8, cbw)), jnp.broadcast_to(xi[7:8, :], (8, cbw)))

        cr, ci = lax.fori_loop(0, tt // 8, blk, (carry[:, 0:cbw], carry[:, cbw:w2]))
        carry[:, 0:cbw] = cr
        carry[:, cbw:w2] = ci
        sb_ref[...] = s_ref[...].astype(BF16)

    (s, sb), riders = _call_with_plans(
        body, plans, name=name, grid=(SCAN_NCB, t // tt),
        in_specs=[pl.BlockSpec((tt, SCAN_UW), lambda cb, ti: (ti, 3 * SCAN_NCB + cb)),
                  pl.BlockSpec((None, SCAN_UW, w2), lambda cb, ti: (cb, 0, 0)),
                  pl.BlockSpec((16, w2), lambda cb, ti: (0, cb))],
        out_specs=[pl.BlockSpec((tt, w2), lambda cb, ti: (ti, cb))] * 2,
        out_shape=[jax.ShapeDtypeStruct((t, 2 * SSM_CH), F32), jax.ShapeDtypeStruct((t, 2 * SSM_CH), BF16)],
        scratch_shapes=[pltpu.VMEM((tt, w2), F32), pltpu.VMEM((8, w2), F32)],
        semantics=("parallel", "arbitrary"), operands=(proj, bmat, tab))
    return s, sb, riders


def _scan_bwd(dyb, cmat_t, s, tabb, *, name, plans=()):
    t = s.shape[0]
    tt, cbw = SCAN_TT, SCAN_CB
    w2 = 2 * cbw
    nt = t // tt
    hb = tt // 8

    def body(dy_ref, c_ref, s_ref, sp_ref, tab_ref, h_ref, da_ref, g_scr, s_scr, carry):
        ti = pl.program_id(1)

        @pl.when(ti == 0)
        def _():
            carry[...] = jnp.zeros_like(carry)
            da_ref[...] = jnp.zeros_like(da_ref)

        g_scr[...] = jnp.dot(dy_ref[...], c_ref[...], preferred_element_type=F32)
        s_scr[0:8, :] = jnp.where(ti < nt - 1, sp_ref[...], 0.0)
        s_scr[8:, :] = s_ref[...]
        row = lax.broadcasted_iota(jnp.int32, (8, cbw), 0)

        def blk(k, c):
            cr, ci, ar, ai = c
            bi = hb - 1 - k
            r0 = pl.multiple_of(bi * 8, 8)
            xr = g_scr[pl.ds(r0, 8), 0:cbw]
            xi = g_scr[pl.ds(r0, 8), cbw:w2]
            for j, sh in enumerate((1, 2, 4)):
                kr = tab_ref[j:j + 1, 0:cbw]
                ki = tab_ref[j:j + 1, cbw:w2]
                sr = jnp.where(row < 8 - sh, pltpu.roll(xr, 8 - sh, 0), 0.0)
                si = jnp.where(row < 8 - sh, pltpu.roll(xi, 8 - sh, 0), 0.0)
                pr, pi = _cmul(kr, ki, sr, si)
                xr = xr + pr
                xi = xi + pi
            pr, pi = _cmul(tab_ref[8:16, 0:cbw], tab_ref[8:16, cbw:w2], cr, ci)
            xr = xr + pr
            xi = xi + pi
            h_ref[pl.ds(r0, 8), 0:cbw] = xr.astype(BF16)
            h_ref[pl.ds(r0, 8), cbw:w2] = xi.astype(BF16)
            pvr = s_scr[pl.ds(r0, 8), 0:cbw]
            pvi = s_scr[pl.ds(r0, 8), cbw:w2]
            cur_r = s_scr[pl.ds(r0 + 8, 8), 0:cbw]
            cur_i = s_scr[pl.ds(r0 + 8, 8), cbw:w2]
            spr = jnp.where(row == 0, jnp.broadcast_to(pvr[7:8, :], (8, cbw)), pltpu.roll(cur_r, 1, 0))
            spi = jnp.where(row == 0, jnp.broadcast_to(pvi[7:8, :], (8, cbw)), pltpu.roll(cur_i, 1, 0))
            ar = ar + spr * xr + spi * xi
            ai = ai + spr * xi - spi * xr
            return (jnp.broadcast_to(xr[0:1, :], (8, cbw)), jnp.broadcast_to(xi[0:1, :], (8, cbw)), ar, ai)

        z = jnp.zeros((8, cbw), F32)
        cr, ci, ar, ai = lax.fori_loop(0, hb, blk, (carry[:, 0:cbw], carry[:, cbw:w2], z, z))
        carry[:, 0:cbw] = cr
        carry[:, cbw:w2] = ci
        da_ref[:, 0:cbw] += ar
        da_ref[:, cbw:w2] += ai

    rt = lambda ti: nt - 1 - ti
    (h, da), riders = _call_with_plans(
        body, plans, name=name, grid=(SCAN_NCB, nt),
        in_specs=[pl.BlockSpec((tt, SCAN_UW), lambda cb, ti: (rt(ti), cb)),
                  pl.BlockSpec((None, SCAN_UW, w2), lambda cb, ti: (cb, 0, 0)),
                  pl.BlockSpec((tt, w2), lambda cb, ti: (rt(ti), cb)),
                  pl.BlockSpec((8, w2), lambda cb, ti: (jnp.maximum(rt(ti) * hb - 1, 0), cb)),
                  pl.BlockSpec((16, w2), lambda cb, ti: (0, cb))],
        out_specs=[pl.BlockSpec((tt, w2), lambda cb, ti: (rt(ti), cb)),
                   pl.BlockSpec((8, w2), lambda cb, ti: (0, cb))],
        out_shape=[jax.ShapeDtypeStruct((t, 2 * SSM_CH), BF16), jax.ShapeDtypeStruct((8, 2 * SSM_CH), F32)],
        scratch_shapes=[pltpu.VMEM((tt, w2), F32), pltpu.VMEM((tt + 8, w2), F32), pltpu.VMEM((8, w2), F32)],
        semantics=("parallel", "arbitrary"), operands=(dyb, cmat_t, s, s, tabb))
    return h, da, riders


def _s5_out(ymm, proj, dvec, *, name, tm=512):
    t, w = ymm.shape

    def body(y_ref, u_ref, d_ref, yo_ref, sg_ref, ub_ref):
        u = u_ref[...]
        y = y_ref[...] + d_ref[...] * u
        yo_ref[...] = y
        sg_ref[...] = _gelu(y).astype(BF16)
        ub_ref[...] = u.astype(BF16)

    return pl.pallas_call(
        body, name=name, grid=(t // tm,),
        in_specs=[_row_spec(tm, w), _row_spec(tm, w, 3), _vec_spec(w)],
        out_specs=[_row_spec(tm, w)] * 3,
        out_shape=[jax.ShapeDtypeStruct((t, w), F32), jax.ShapeDtypeStruct((t, w), BF16), jax.ShapeDtypeStruct((t, w), BF16)],
        compiler_params=_cparams(("parallel",)),
    )(ymm, proj, dvec)


def _s5_bwd_in(dsg, y, proj, *, name, tm=512):
    t, w = y.shape

    def body(d_ref, y_ref, u_ref, dy_ref, dyb_ref, dd_ref):
        i = pl.program_id(0)
        dy = d_ref[...] * _gelu_grad(y_ref[...])
        dy_ref[...] = dy
        dyb_ref[...] = dy.astype(BF16)
        part = jnp.sum(dy * u_ref[...], axis=0, keepdims=True)

        @pl.when(i == 0)
        def _():
            dd_ref[...] = part

        @pl.when(i > 0)
        def _():
            dd_ref[...] += part

    return pl.pallas_call(
        body, name=name, grid=(t // tm,),
        in_specs=[_row_spec(tm, w), _row_spec(tm, w), _row_spec(tm, w, 3)],
        out_specs=[_row_spec(tm, w), _row_spec(tm, w), _vec_spec(w)],
        out_shape=[jax.ShapeDtypeStruct((t, w), F32), jax.ShapeDtypeStruct((t, w), BF16), jax.ShapeDtypeStruct((1, w), F32)],
        compiler_params=_cparams(("arbitrary",)),
    )(dsg, y, proj)


def _s5_du(dumm, dy, dvec, *, name, tm=512):
    t, w = dy.shape

    def body(a_ref, dy_ref, d_ref, o_ref):
        o_ref[...] = (a_ref[...] + d_ref[...] * dy_ref[...]).astype(BF16)

    return pl.pallas_call(
        body, name=name, grid=(t // tm,), in_specs=[_row_spec(tm, w), _row_spec(tm, w), _vec_spec(w)],
        out_specs=_row_spec(tm, w), out_shape=jax.ShapeDtypeStruct((t, w), BF16),
        compiler_params=_cparams(("parallel",)),
    )(dumm, dy, dvec)


def _s5_discretise(lam_re, lam_im, log_step, b_re, b_im):
    lam = lax.complex(lam_re, lam_im)
    dt = jnp.exp(log_step)[:, None]
    a = jnp.exp(lam * dt)
    bbar = ((a - 1.0) / lam)[..., None] * lax.complex(b_re, b_im)
    return jnp.real(a), jnp.imag(a), jnp.real(bbar), jnp.imag(bbar)


def _perm_cols(re, im):
    lead = re.shape[:-1]
    r = re.reshape(lead + (SCAN_NCB, 1, SCAN_CB))
    i = im.reshape(lead + (SCAN_NCB, 1, SCAN_CB))
    return jnp.concatenate([r, i], axis=-2).reshape(lead + (2 * SSM_CH,))


def _unperm_cols(x):
    lead = x.shape[:-1]
    y = x.reshape(lead + (SCAN_NCB, 2, SCAN_CB))
    return y[..., 0, :].reshape(lead + (SSM_CH,)), y[..., 1, :].reshape(lead + (SSM_CH,))


def _compact(re, im):
    _, r, c = re.shape
    eye = jnp.eye(SCAN_GPB, dtype=re.dtype)

    def half(x):
        x = x.reshape(SCAN_NCB, SCAN_GPB, r, c)
        return (eye[None, :, None, :, None] * x[:, :, :, None, :]).reshape(SCAN_NCB, SCAN_GPB * r, SCAN_GPB * c)

    return jnp.concatenate([half(re), half(im)], axis=-1)


def _compact_extract(x, r):
    c = SSM_STATE
    eye = jnp.eye(SCAN_GPB, dtype=x.dtype)
    y = x.reshape(SCAN_NCB, SCAN_GPB, r, 2, SCAN_GPB, c)
    dg = jnp.sum(y * eye[None, :, None, None, :, None], axis=4).reshape(SSM_GROUPS, r, 2, c)
    return dg[:, :, 0, :], dg[:, :, 1, :]


def _pow_table(ar, ai, descending=False):
    ar = ar.reshape(1, SSM_CH)
    ai = ai.reshape(1, SSM_CH)
    pw = [(ar, ai)]
    for _ in range(7):
        pw.append(_cmul(pw[-1][0], pw[-1][1], ar, ai))
    zero = (jnp.zeros_like(ar), jnp.zeros_like(ar))
    rows = [pw[0], pw[1], pw[3]] + [zero] * 5 + (pw[::-1] if descending else pw)
    re = jnp.concatenate([r for r, _ in rows], axis=0)
    im = jnp.concatenate([i for _, i in rows], axis=0)
    return _perm_cols(re, im)


def _place():
    x, y, c = lax.axis_index("x"), lax.axis_index("y"), lax.axis_index("c")
    chips = [(1 - x, y), (x, 1 - y), (1 - x, 1 - y)]
    return x, y, c, chips


def _dev(px, py, pc):
    return 4 * px + 2 * py + pc


class _Plan:
    def __init__(self, ins, out_shapes, sem_shapes, start, finish, middle=None):
        self.ins, self.out_shapes, self.sem_shapes = list(ins), list(out_shapes), list(sem_shapes)
        self.start, self.finish, self.middle = start, finish, middle


def _split_plan_refs(plans, in_refs, out_refs, sem_refs):
    res, i, o, s = [], 0, 0, 0
    for p in plans:
        ni, no, ns = len(p.ins), len(p.out_shapes), len(p.sem_shapes)
        res.append((in_refs[i:i + ni], out_refs[o:o + no], sem_refs[s:s + ns]))
        i, o, s = i + ni, o + no, s + ns
    return res


def _run_plans(plans, *, name):
    ins = [a for p in plans for a in p.ins]
    outs = [o for p in plans for o in p.out_shapes]
    sems = [s for p in plans for s in p.sem_shapes]
    any_spec = pl.BlockSpec(memory_space=pl.ANY)

    def body(*refs):
        parts = _split_plan_refs(plans, refs[:len(ins)], refs[len(ins):len(ins) + len(outs)], refs[len(ins) + len(outs):])
        for p, r in zip(plans, parts):
            p.start(*r)
        for p, r in zip(plans, parts):
            if p.middle:
                p.middle(*r)
        for p, r in zip(plans, parts):
            p.finish(*r)

    res = pl.pallas_call(body, name=name, in_specs=[any_spec] * len(ins), out_specs=[any_spec] * len(outs),
                         out_shape=outs, scratch_shapes=sems)(*ins)
    return _split_plan_refs(plans, [], res, [])


def _call_with_plans(body, plans, *, name, grid, in_specs, out_specs, out_shape, scratch_shapes, semantics, operands):
    plans = list(plans)
    if not plans:
        res = pl.pallas_call(body, name=name, grid=grid, in_specs=in_specs, out_specs=out_specs, out_shape=out_shape,
                             scratch_shapes=scratch_shapes, compiler_params=_cparams(semantics))(*operands)
        return list(res), []
    n_in, n_out, n_scr = len(in_specs), len(out_specs), len(scratch_shapes)
    p_ins = [a for p in plans for a in p.ins]
    p_outs = [o for p in plans for o in p.out_shapes]
    p_sems = [s for p in plans for s in p.sem_shapes]
    nsteps = math.prod(grid)
    any_spec = pl.BlockSpec(memory_space=pl.ANY)

    def wrapped(*refs):
        bounds = [n_in, len(p_ins), n_out, len(p_outs), n_scr]
        parts, pos = [], 0
        for b in bounds:
            parts.append(refs[pos:pos + b])
            pos += b
        ins, p_in, outs, p_out, scr = parts
        step = pl.program_id(0)
        for ax in range(1, len(grid)):
            step = step * grid[ax] + pl.program_id(ax)
        riders = _split_plan_refs(plans, p_in, p_out, refs[pos:])

        @pl.when(step == 0)
        def _():
            for p, r in zip(plans, riders):
                p.start(*r)

        mids = [(p, r) for p, r in zip(plans, riders) if p.middle]
        mid_step = nsteps // 2
        split_mid = mids and 0 < mid_step < nsteps - 1
        if split_mid:
            @pl.when(step == mid_step)
            def _():
                for p, r in mids:
                    p.middle(*r)

        body(*ins, *outs, *scr)

        @pl.when(step == nsteps - 1)
        def _():
            if not split_mid:
                for p, r in mids:
                    p.middle(*r)
            for p, r in zip(plans, riders):
                p.finish(*r)

    res = pl.pallas_call(
        wrapped, name=name, grid=grid, in_specs=list(in_specs) + [any_spec] * len(p_ins),
        out_specs=list(out_specs) + [any_spec] * len(p_outs), out_shape=list(out_shape) + p_outs,
        scratch_shapes=list(scratch_shapes) + p_sems, compiler_params=_cparams(("arbitrary",) * len(grid)),
    )(*operands, *p_ins)
    return list(res[:n_out]), [r[1] for r in _split_plan_refs(plans, [], res[n_out:], [])]


def _gather_plan(shards):
    n = len(shards)
    nk = 8

    def make(ins, outs, sems):
        send, recv, lsem = sems
        x, y, c, _ = _place()
        me, sib, xn, yn, dg = (x, y, c), (x, y, 1 - c), (1 - x, y, c), (x, 1 - y, c), (1 - x, 1 - y, c)

        def part(w, block, half):
            ref = outs[w].at[_dev(*block)]
            if half is None:
                return ref
            rows = shards[w].shape[0] // 2
            return ref.at[pl.ds(half * rows, rows)]

        def copy(w, k, block, to, half=None, src=None):
            dst = part(w, block, half)
            return pltpu.make_async_remote_copy(
                src_ref=dst if src is None else src, dst_ref=dst,
                send_sem=send.at[w * nk + k], recv_sem=recv.at[w * nk + k], device_id=to, device_id_type=MESH)

        mine = [pltpu.make_async_copy(ins[w], outs[w].at[_dev(*me)], lsem.at[w]) for w in range(n)]
        return copy, mine, me, sib, xn, yn, dg

    def first_copies(copy, me, sib, xn, yn, ins):
        return [copy(w, k, me, to, src=ins[w]) for w in range(n) for k, to in ((0, sib), (1, xn), (2, yn))]

    def start(ins, outs, sems):
        copy, mine, me, sib, xn, yn, _ = make(ins, outs, sems)
        for cp in mine + first_copies(copy, me, sib, xn, yn, ins):
            cp.start()

    def middle(ins, outs, sems):
        copy, _, me, sib, xn, yn, _ = make(ins, outs, sems)
        for w in range(n):
            copy(w, 1, xn, me).wait_recv()
            copy(w, 3, xn, yn, half=0).start()
            copy(w, 5, xn, sib).start()
        for w in range(n):
            copy(w, 2, yn, me).wait_recv()
            copy(w, 4, yn, xn, half=1).start()
            copy(w, 6, yn, sib).start()

    def finish(ins, outs, sems):
        copy, mine, me, sib, xn, yn, dg = make(ins, outs, sems)
        last = []
        for w in range(n):
            copy(w, 3, dg, me, half=0).wait_recv()
            copy(w, 4, dg, me, half=1).wait_recv()
            fwd = copy(w, 7, dg, sib)
            fwd.start()
            last.append(fwd)
        sx, sy, sd = (1 - me[0], me[1], 1 - me[2]), (me[0], 1 - me[1], 1 - me[2]), (1 - me[0], 1 - me[1], 1 - me[2])
        for w in range(n):
            copy(w, 0, sib, me).wait_recv()
            copy(w, 5, sx, me).wait_recv()
            copy(w, 6, sy, me).wait_recv()
            copy(w, 7, sd, me).wait_recv()
        for cp in first_copies(copy, me, sib, xn, yn, ins) + last:
            cp.wait_send()
        for w in range(n):
            copy(w, 3, xn, yn, half=0).wait_send()
            copy(w, 5, xn, sib).wait_send()
            copy(w, 4, yn, xn, half=1).wait_send()
            copy(w, 6, yn, sib).wait_send()
        for cp in mine:
            cp.wait()

    return _Plan(shards, [jax.ShapeDtypeStruct((N_DEV,) + s.shape, s.dtype) for s in shards],
                 [pltpu.SemaphoreType.DMA((nk * n,)), pltpu.SemaphoreType.DMA((nk * n,)), pltpu.SemaphoreType.DMA((n,))],
                 start, finish, middle)


def _swap_plan(copies_of, n_copies, ins, out_shapes):
    def cps(in_refs, out_refs, sems):
        return copies_of(in_refs, out_refs, sems[0], sems[1])

    def start(in_refs, out_refs, sems):
        for cp in cps(in_refs, out_refs, sems):
            cp.start()

    def finish(in_refs, out_refs, sems):
        all_cps = cps(in_refs, out_refs, sems)
        for cp in all_cps:
            cp.wait_recv()
        for cp in all_cps:
            cp.wait_send()

    return _Plan(ins, out_shapes, [pltpu.SemaphoreType.DMA((n_copies,)), pltpu.SemaphoreType.DMA((n_copies,))],
                 start, finish)


def _sibling_plan(grads):
    n = len(grads)

    def copies(ins, outs, send, recv):
        x, y, c, chips = _place()
        owners = [(x, y)] + chips
        return [pltpu.make_async_remote_copy(
            src_ref=ins[w].at[_dev(*chip, 1 - c)], dst_ref=outs[w].at[k], send_sem=send.at[w * 4 + k],
            recv_sem=recv.at[w * 4 + k], device_id=(x, y, 1 - c), device_id_type=MESH)
            for w in range(n) for k, chip in enumerate(owners)]

    return _swap_plan(copies, 4 * n, grads, [jax.ShapeDtypeStruct((4,) + g.shape[1:], g.dtype) for g in grads])


def _chip_plan(parts, js=(0, 1, 2)):
    n, nj = len(parts), len(js)

    def copies(ins, outs, send, recv):
        x, y, c, chips = _place()
        return [pltpu.make_async_remote_copy(
            src_ref=ins[w].at[j], dst_ref=outs[w * nj + k], send_sem=send.at[w * nj + k],
            recv_sem=recv.at[w * nj + k], device_id=(*chips[j], c), device_id_type=MESH)
            for w in range(n) for k, j in enumerate(js)]

    return _swap_plan(copies, n * nj, parts,
                      [jax.ShapeDtypeStruct(p.shape[1:], p.dtype) for p in parts for _ in js])


UPDATE_TILE_BYTES = 768 * 1024


def _row_tile(r, c):
    best = 8
    for t in range(8, r + 1, 8):
        if r % t == 0 and t * c * 4 <= UPDATE_TILE_BYTES:
            best = t
    return best


def _chip_partial(g, sib, ids, *, name):
    _, r, c = g.shape
    tr = _row_tile(r, c)

    def body(ids_ref, g_ref, s_ref, o_ref):
        o_ref[...] = (g_ref[...] + s_ref[...]).astype(BF16)

    return pl.pallas_call(
        body, name=name,
        grid_spec=pltpu.PrefetchScalarGridSpec(
            num_scalar_prefetch=1, grid=(3, r // tr),
            in_specs=[pl.BlockSpec((None, tr, c), lambda j, i, ids_ref: (ids_ref[j], i, 0)),
                      pl.BlockSpec((None, tr, c), lambda j, i, ids_ref: (j + 1, i, 0))],
            out_specs=pl.BlockSpec((None, tr, c), lambda j, i, ids_ref: (j, i, 0))),
        out_shape=jax.ShapeDtypeStruct((3, r, c), BF16),
        compiler_params=_cparams(("parallel", "parallel")),
    )(ids, g, sib)


def _adamw_math(w, g, m, v):
    m = ADAM_B1 * m + (1.0 - ADAM_B1) * g
    v = ADAM_B2 * v + (1.0 - ADAM_B2) * (g * g)
    m_hat = m / (1.0 - ADAM_B1 ** ADAM_STEP)
    v_hat = v / (1.0 - ADAM_B2 ** ADAM_STEP)
    delta = -ADAM_LR * (m_hat / (jnp.sqrt(v_hat) + ADAM_EPS) + ADAM_WD * w)
    return delta, m, v


def _shard_update(g, sib, rem, me, w, m, v, *, name):
    r, c = w.shape
    tr = _row_tile(r, c)

    def body(me_ref, g_ref, s_ref, r0_ref, r1_ref, r2_ref, w_ref, m_ref, v_ref, go_ref, d_ref, mo_ref, vo_ref):
        gt = g_ref[...] + s_ref[...]
        gt = gt + r0_ref[...].astype(F32)
        gt = gt + r1_ref[...].astype(F32)
        gt = gt + r2_ref[...].astype(F32)
        go_ref[...] = gt
        d, mn, vn = _adamw_math(w_ref[...], gt, m_ref[...], v_ref[...])
        d_ref[...] = d
        mo_ref[...] = mn
        vo_ref[...] = vn

    blk = lambda k: pl.BlockSpec((None, tr, c), lambda i, me_ref: (k, i, 0))
    plain = pl.BlockSpec((tr, c), lambda i, me_ref: (i, 0))
    return pl.pallas_call(
        body, name=name,
        grid_spec=pltpu.PrefetchScalarGridSpec(
            num_scalar_prefetch=1, grid=(r // tr,),
            in_specs=[pl.BlockSpec((None, tr, c), lambda i, me_ref: (me_ref[0], i, 0)), blk(0), plain, plain, plain,
                      plain, plain, plain],
            out_specs=[plain] * 4),
        out_shape=[jax.ShapeDtypeStruct((r, c), F32)] * 4,
        compiler_params=_cparams(("parallel",)),
    )(me, g, sib, *rem, w, m, v)


def _small_update(gathered, w, m, v, *, name):
    _, r, c = gathered.shape

    def body(g_ref, w_ref, m_ref, v_ref, go_ref, d_ref, mo_ref, vo_ref):
        gt = g_ref[0]
        for k in range(1, N_DEV):
            gt = gt + g_ref[k]
        go_ref[...] = gt
        d, mn, vn = _adamw_math(w_ref[...], gt, m_ref[...], v_ref[...])
        d_ref[...] = d
        mo_ref[...] = mn
        vo_ref[...] = vn

    return pl.pallas_call(
        body, name=name, out_shape=[jax.ShapeDtypeStruct((r, c), F32)] * 4,
        compiler_params=pltpu.CompilerParams(vmem_limit_bytes=VMEM_LIMIT),
    )(gathered, w, m, v)


SMALL_UNIT = 1024


def _pack(parts):
    flat = []
    for p in parts:
        f = p.reshape(-1).astype(F32)
        pad = (-f.shape[0]) % SMALL_UNIT
        flat.append(jnp.pad(f, (0, pad)))
    return jnp.concatenate(flat).reshape(-1, 128)


def _unpack(buf, shapes):
    flat = buf.reshape(-1)
    out, off = [], 0
    for s in shapes:
        nel = math.prod(s)
        out.append(flat[off:off + nel].reshape(s))
        off += nel + ((-nel) % SMALL_UNIT)
    return out


def kernel(x, p, ffn1_w_in, ffn1_w_out, ln1_g, ln1_b, mix_w_in, conv_w, conv_b, conv_w_out, ssm_lam_re, ssm_lam_im, ssm_log_step, ssm_b_re, ssm_b_im, ssm_c_re, ssm_c_im, ssm_d, ssm_w_glu, mix_w_out, ln2_g, ln2_b, ffn2_w_in, ffn2_w_out, ln3_g, ln3_b, ple_w_in, ple_w_gate, ln4_g, ln4_b, loss_target, m_ffn1_w_in, m_ffn1_w_out, m_ln1_g, m_ln1_b, m_mix_w_in, m_conv_w, m_conv_b, m_conv_w_out, m_ssm_lam_re, m_ssm_lam_im, m_ssm_log_step, m_ssm_b_re, m_ssm_b_im, m_ssm_c_re, m_ssm_c_im, m_ssm_d, m_ssm_w_glu, m_mix_w_out, m_ln2_g, m_ln2_b, m_ffn2_w_in, m_ffn2_w_out, m_ln3_g, m_ln3_b, m_ple_w_in, m_ple_w_gate, m_ln4_g, m_ln4_b, v_ffn1_w_in, v_ffn1_w_out, v_ln1_g, v_ln1_b, v_mix_w_in, v_conv_w, v_conv_b, v_conv_w_out, v_ssm_lam_re, v_ssm_lam_im, v_ssm_log_step, v_ssm_b_re, v_ssm_b_im, v_ssm_c_re, v_ssm_c_im, v_ssm_d, v_ssm_w_glu, v_mix_w_out, v_ln2_g, v_ln2_b, v_ffn2_w_in, v_ffn2_w_out, v_ln3_g, v_ln3_b, v_ple_w_in, v_ple_w_gate, v_ln4_g, v_ln4_b):
    args = dict(locals())
    big = ['ffn1_w_in', 'ffn1_w_out', 'mix_w_in', 'conv_w_out', 'ssm_w_glu', 'mix_w_out',
           'ffn2_w_in', 'ffn2_w_out', 'ple_w_in', 'ple_w_gate']
    small = ['ln1_g', 'ln1_b', 'conv_b', 'ssm_lam_re', 'ssm_lam_im', 'ssm_log_step', 'ssm_b_re', 'ssm_b_im',
             'ssm_c_re', 'ssm_c_im', 'ssm_d', 'ln2_g', 'ln2_b', 'ln3_g', 'ln3_b', 'ln4_g', 'ln4_b']
    order = ['ffn1_w_in', 'ffn1_w_out', 'ln1_g', 'ln1_b', 'mix_w_in', 'conv_w', 'conv_b', 'conv_w_out',
             'ssm_lam_re', 'ssm_lam_im', 'ssm_log_step', 'ssm_b_re', 'ssm_b_im', 'ssm_c_re', 'ssm_c_im', 'ssm_d',
             'ssm_w_glu', 'mix_w_out', 'ln2_g', 'ln2_b', 'ffn2_w_in', 'ffn2_w_out', 'ln3_g', 'ln3_b',
             'ple_w_in', 'ple_w_gate', 'ln4_g', 'ln4_b']

    t = x.shape[1]
    d = x.shape[2]
    xc_, yc_, cc_ = lax.axis_index("x"), lax.axis_index("y"), lax.axis_index("c")
    me = (4 * xc_ + 2 * yc_ + cc_).astype(jnp.int32)
    cw_cols = conv_w.shape[2]

    turned = ('ffn1_w_in', 'ffn2_w_in')

    def local(a, nm):
        return jnp.swapaxes(a[0], 0, 1) if nm in turned else a[0]

    shard = {nm: local(args[nm], nm).astype(BF16) for nm in big}
    cw_pad = jnp.zeros((16, 128), F32).at[0:3, 0:cw_cols].set(conv_w[0])
    wf = shard['ffn1_w_in'].shape[0]

    def gather(*names):
        return _gather_plan([shard[nm] for nm in names])

    ((_, (w1in, cw_g), _),) = _run_plans([_gather_plan([shard['ffn1_w_in'], cw_pad])], name="gather_ffn1_in")
    cw_full = jnp.transpose(cw_g[:, 0:3, 0:cw_cols], (1, 0, 2)).reshape(3, N_DEV * cw_cols)
    cw8 = jnp.zeros((8, CONV_CH), F32).at[0:3, :].set(cw_full)

    s5_in = (ssm_lam_re[0], ssm_lam_im[0], ssm_log_step[0], ssm_b_re[0], ssm_b_im[0])
    (a_re, a_im, bb_re, bb_im), s5_vjp = jax.vjp(_s5_discretise, *s5_in)
    tab_f = _pow_table(a_re, a_im)
    tab_b = _pow_table(a_re, -a_im, descending=True)
    bmat_b = _compact(jnp.transpose(bb_re, (0, 2, 1)), jnp.transpose(bb_im, (0, 2, 1))).astype(BF16)
    cmat_tb = _compact(ssm_c_re[0], -ssm_c_im[0]).astype(BF16)
    dvec = ssm_d[0].reshape(1, SSM_W)

    xf = x[0]
    x_b = xf.astype(BF16)
    p_b = p[0, 0].astype(BF16)
    tgt = loss_target[0]
    tq = min(512, t)

    ffn_out = dict(ja='c', jb='c', nj=4, tm=tq, tn=d, tk=wf)
    def side_by_side(wb):
        return jnp.transpose(wb, (1, 0, 2)).reshape(wb.shape[1], N_DEV * wb.shape[2])

    tf = min(1024, t)
    a1, h1, ((w1out_g,),) = _ffn_in(x_b, w1in, name="ffn1_in", tm=tf, plans=[gather('ffn1_w_out')])
    w1out = w1out_g.reshape(4, wf, d)
    (r1, x1, x1b), ((wmix,),) = _mm(a1, w1out, name="ffn1_out", **ffn_out, plans=[gather('mix_w_in')],
                                    epilogue=_ln_epilogue(xf, ln1_g, ln1_b, 0.5))
    proj, ((wco, wglu, wmo_g),) = _mm(x1b, wmix, name="mix_in", jb='b', jo='b', o_flat=True, nj=8, tm=tq, tn=512,
                                      tk=d, plans=[gather('conv_w_out', 'ssm_w_glu', 'mix_w_out')])
    wmo = wmo_g.reshape(d, d)
    ycin = _conv_fwd(proj, cw8, conv_b, name="conv_fwd")
    wco, wglu = side_by_side(wco), side_by_side(wglu)
    yconv = _mm(ycin, wco, name="conv_out", tm=tq, tn=d, tk=CONV_CH)
    s_f, s_b, ((w2in,),) = _scan_fwd(proj, bmat_b, tab_f, name="scan_fwd", plans=[gather('ffn2_w_in')])
    blk = dict(ja='b', jb='b', jo='b', nj=SCAN_NCB)
    ymm = _mm(s_b, cmat_tb, name="ssm_read", a_flat=True, o_flat=True, tb=True, tm=tq, tn=SCAN_UW, tk=2 * SCAN_CB, **blk)
    ys, sg, u_b = _s5_out(ymm, proj, dvec, name="ssm_out")
    glu = _mm(sg, wglu, name="glu_in", tm=tq, tn=d, tk=SSM_W)
    merged = _gate_fwd(yconv, glu, proj, name="gate_fwd")
    r2, x2, x2b = _mm(merged, wmo, name="mix_out", tm=tq, tn=d, tk=d, epilogue=_ln_epilogue(x1, ln2_g, ln2_b, 1.0))
    a2, h2, ((w2out_g,),) = _ffn_in(x2b, w2in, name="ffn2_in", tm=tf, plans=[gather('ffn2_w_out')])
    w2out = w2out_g.reshape(4, wf, d)
    (r3, x3, x3b), ((wpin, wgate_g),) = _mm(a2, w2out, name="ffn2_out", **ffn_out, plans=[gather('ple_w_in', 'ple_w_gate')],
                                            epilogue=_ln_epilogue(x2, ln3_g, ln3_b, 0.5))
    wgate = wgate_g.reshape(d, d)
    pe = _mm(p_b, side_by_side(wpin), name="ple_in", tm=tq, tn=d, tk=p_b.shape[1])
    gp = _mm(x3b, wgate, name="ple_gate", tm=tq, tn=d, tk=d)

    dr4, dpe_b, dgp_b, dg4, db4, loss_part = _ln_bwd(x3, [pe, gp], ln4_g, ln4_b, [], name="ple_ln_bwd",
                                                     ple=True, target=tgt)
    gb, sib, rem = {}, {}, {}
    ids = jnp.stack([_dev(1 - xc_, yc_, cc_), _dev(xc_, 1 - yc_, cc_), _dev(1 - xc_, 1 - yc_, cc_)]).astype(jnp.int32)

    def blocked(nm, g):
        return g.reshape((N_DEV,) + args[nm].shape[1:])

    def to_sibling(*names):
        return _sibling_plan([gb[nm] for nm in names])

    def chip_sums(names, sibs):
        for nm, s in zip(names, sibs):
            sib[nm] = s
        return [_chip_partial(gb[nm], sib[nm], ids, name=f"chip_sum_{nm}") for nm in names]

    ffn_in_dg = dict(ja='c', jb='c', nj=8, tm=tq, tn=d, tk=wf)
    ffn_in_wg = dict(ja='b', jo='b', ta=True, nj=8, tm=wf, tn=d, tk=t)
    ffn_out_wg = dict(ja='b', jo='b', ta=True, nj=4, tm=wf, tn=d, tk=t)

    gb['ple_w_in'] = _mm(p_b, dpe_b, name="ple_in_wg", jb='b', jo='b', b_flat=True, ta=True, nj=8,
                         tm=p_b.shape[1], tn=128, tk=t)
    gb['ple_w_gate'] = blocked('ple_w_gate', _mm(x3b, dgp_b, name="ple_gate_wg", ta=True, tm=d, tn=d, tk=t))
    g_ple = ['ple_w_in', 'ple_w_gate']
    dx3_g, (s_,) = _mm(dgp_b, wgate, name="ple_gate_dg", tb=True, tm=tq, tn=d, tk=d, plans=[to_sibling(*g_ple)])
    part = chip_sums(g_ple, s_)

    dr3, df2_b, dg3, db3 = _ln_bwd(r3, [], ln3_g, ln3_b, [(dr4, ALPHA), (dx3_g, 1.0)], name="ffn2_ln_bwd", fs=0.5)
    dh2, _ = _ffn_out_dg(df2_b, w2out, h2, name="ffn2_out_dg", tm=tf)
    g_, (r_,) = _mm(a2, df2_b, name="ffn2_out_wg", **ffn_out_wg, plans=[_chip_plan(part)])
    gb['ffn2_w_out'] = blocked('ffn2_w_out', g_)
    rem['ple_w_in'], rem['ple_w_gate'] = r_[0:3], r_[3:6]
    dx2_f, (s_,) = _mm(dh2, w2in, name="ffn2_in_dg", **ffn_in_dg, plans=[to_sibling('ffn2_w_out')])
    part = chip_sums(['ffn2_w_out'], s_)
    gb['ffn2_w_in'], (r_,) = _mm(dh2, x2b, name="ffn2_in_wg", **ffn_in_wg, plans=[_chip_plan(part)])
    rem['ffn2_w_out'] = r_

    dr2, dmix_b, dg2, db2 = _ln_bwd(r2, [], ln2_g, ln2_b, [(dr3, ALPHA), (dx2_f, 1.0)], name="mix_ln_bwd")
    dmerged, (s_,) = _mm(dmix_b, wmo, name="mix_out_dg", tb=True, tm=tq, tn=d, tk=d, plans=[to_sibling('ffn2_w_in')])
    part = chip_sums(['ffn2_w_in'], s_)
    gb['mix_w_out'] = blocked('mix_w_out', _mm(merged, dmix_b, name="mix_out_wg", ta=True, tm=d, tn=d, tk=t))
    dyconv_b, dglu_b, dgate_b = _gate_bwd(dmerged, yconv, glu, proj, name="gate_bwd")
    gb['conv_w_out'] = _mm(ycin, dyconv_b, name="conv_out_wg", jb='b', jo='b', b_flat=True, ta=True, nj=8,
                           tm=CONV_CH, tn=128, tk=t)
    dycin = _mm(dyconv_b, wco, name="conv_out_dg", tb=True, tm=tq, tn=CONV_CH, tk=d)
    gb['ssm_w_glu'] = _mm(sg, dglu_b, name="glu_in_wg", jb='b', jo='b', b_flat=True, ta=True, nj=8,
                          tm=SSM_W, tn=256, tk=t)
    g_mix = ['mix_w_out', 'conv_w_out', 'ssm_w_glu']
    dsg, (s_,) = _mm(dglu_b, wglu, name="glu_in_dg", tb=True, tm=tq, tn=SSM_W, tk=2 * d, plans=[to_sibling(*g_mix)])
    part_mix = chip_sums(g_mix, s_)
    dys, dys_b, dd = _s5_bwd_in(dsg, ys, proj, name="ssm_out_bwd")
    h_b, da_acc, (rem['ffn2_w_in'],) = _scan_bwd(dys_b, cmat_tb, s_f, tab_b, name="scan_bwd", plans=[_chip_plan(part)])
    dumm = _mm(h_b, bmat_b, name="ssm_write_dg", a_flat=True, o_flat=True, tb=True, tm=tq, tn=SCAN_UW,
               tk=2 * SCAN_CB, **blk)
    du_b = _s5_du(dumm, dys, dvec, name="ssm_du")
    g_bmat = _mm(u_b, h_b, name="ssm_write_wg", a_flat=True, b_flat=True, ta=True, tm=SCAN_UW,
                 tn=2 * SCAN_CB, tk=t, **blk)
    g_cmat = _mm(dys_b, s_b, name="ssm_read_wg", a_flat=True, b_flat=True, ta=True, tm=SCAN_UW,
                 tn=2 * SCAN_CB, tk=t, **blk)
    dcb_b, dcc_b, dch_b, dconv = _conv_bwd(dycin, proj, cw8, conv_b, name="conv_bwd")
    dproj = jnp.concatenate([dcb_b, dcc_b, dch_b, du_b, dgate_b], axis=1)
    gb['mix_w_in'], (r_,) = _mm(x1b, dproj, name="mix_in_wg", jb='b', jo='b', b_flat=True, ta=True, nj=8,
                                tm=d, tn=512, tk=t, plans=[_chip_plan(part_mix)])
    for i, nm in enumerate(g_mix):
        rem[nm] = r_[3 * i:3 * i + 3]
    dx1_m, (s_,) = _mm(dproj, wmix, name="mix_in_dg", ja='c', jb='c', a_flat=True, tb=True, nj=8,
                       tm=tq, tn=d, tk=512, plans=[to_sibling('mix_w_in')])
    part = chip_sums(['mix_w_in'], s_)

    dr1, df1_b, dg1, db1 = _ln_bwd(r1, [], ln1_g, ln1_b, [(dr2, ALPHA), (dx1_m, 1.0)], name="ffn1_ln_bwd", fs=0.5)
    dh1, (r01,) = _ffn_out_dg(df1_b, w1out, h1, name="ffn1_out_dg", tm=tf, plans=[_chip_plan(part, js=(0, 1))])
    part_mix_in = part

    da_sum = jnp.sum(da_acc, axis=0)
    da_re, da_im = _unperm_cols(da_sum)
    gbb_re, gbb_im = [jnp.transpose(v, (0, 2, 1)) for v in _compact_extract(g_bmat, SSM_GROUP)]
    g_c_re, g_c_im_neg = _compact_extract(g_cmat, SSM_GROUP)
    g_c_im = -g_c_im_neg
    g_lam_re, g_lam_im, g_log_step, g_b_re, g_b_im = s5_vjp(
        (da_re.reshape(SSM_GROUPS, SSM_STATE), da_im.reshape(SSM_GROUPS, SSM_STATE), gbb_re, gbb_im))
    g_d = dd.reshape(SSM_GROUPS, SSM_GROUP)

    small_g = {'ln1_g': dg1, 'ln1_b': db1, 'conv_b': dconv[3:4], 'ssm_lam_re': g_lam_re, 'ssm_lam_im': g_lam_im,
               'ssm_log_step': g_log_step, 'ssm_b_re': g_b_re, 'ssm_b_im': g_b_im, 'ssm_c_re': g_c_re,
               'ssm_c_im': g_c_im, 'ssm_d': g_d, 'ln2_g': dg2, 'ln2_b': db2, 'ln3_g': dg3, 'ln3_b': db3,
               'ln4_g': dg4, 'ln4_b': db4}
    small_shapes = [args[nm].shape for nm in small] + [(3, CONV_CH), (1,)]
    g_pack = _pack([small_g[nm] for nm in small] + [dconv[0:3], loss_part[0:1, 0:1]])

    gb['ffn1_w_in'], ((g_all,),) = _mm(dh1, x_b, name="ffn1_in_wg", **ffn_in_wg, plans=[_gather_plan([g_pack])])
    g_, (s_, r2) = _mm(a1, df1_b, name="ffn1_out_wg", **ffn_out_wg,
                       plans=[to_sibling('ffn1_w_in'), _chip_plan(part_mix_in, js=(2,))])
    rem['mix_w_in'] = r01 + r2
    gb['ffn1_w_out'] = blocked('ffn1_w_out', g_)
    part = chip_sums(['ffn1_w_in'], s_)
    (grad_x,), (r_, s_) = _mm(dh1, w1in, name="ffn1_in_dg", **ffn_in_dg,
                              plans=[_chip_plan(part), to_sibling('ffn1_w_out')],
                              epilogue=(lambda pr, drv: (pr + ALPHA * drv,), (dr1,), (), (F32,)))
    rem['ffn1_w_in'] = r_
    part = chip_sums(['ffn1_w_out'], s_)
    ((_, rem['ffn1_w_out'], _),) = _run_plans([_chip_plan(part)], name="grad_chips_ffn1_out")

    def full_cw(a):
        return lax.dynamic_update_slice(jnp.zeros((3, CONV_CH), F32), a[0], (0, me * cw_cols))

    zero1 = jnp.zeros((1,), F32)
    w_pack = _pack([args[nm] for nm in small] + [full_cw(conv_w), zero1])
    m_pack = _pack([args['m_' + nm] for nm in small] + [full_cw(m_conv_w), zero1])
    v_pack = _pack([args['v_' + nm] for nm in small] + [full_cw(v_conv_w), zero1])
    sg_sum, sd, sm, sv = _small_update(g_all, w_pack, m_pack, v_pack, name="small_update")
    res = {}
    for key, buf in (('grad_', sg_sum), ('delta_', sd), ('new_m_', sm), ('new_v_', sv)):
        parts = _unpack(buf, small_shapes)
        for nm, val in zip(small, parts[:len(small)]):
            res[key + nm] = val
        res[key + 'conv_w'] = lax.dynamic_slice(parts[len(small)], (0, me * cw_cols), (3, cw_cols))[None]
        if key == 'grad_':
            loss = parts[-1][0]

    me1 = me.reshape(1)
    for nm in big:
        upd = _shard_update(gb[nm], sib[nm], rem[nm], me1, local(args[nm], nm), local(args['m_' + nm], nm),
                            local(args['v_' + nm], nm), name=f"update_{nm}")
        for key, val in zip(('grad_', 'delta_', 'new_m_', 'new_v_'), upd):
            res[key + nm] = (jnp.swapaxes(val, 0, 1) if nm in turned else val)[None]

    outs = [loss, grad_x[None]]
    for key in ('grad_', 'delta_', 'new_m_', 'new_v_'):
        outs += [res[key + nm] for nm in order]
    return tuple(outs)
```

```python
import functools
import math

import jax
import jax.numpy as jnp
from jax import lax
from jax.experimental import pallas as pl
from jax.experimental.pallas import tpu as pltpu
from jax.experimental.pallas import tpu_sc as plsc

F32 = jnp.float32
BF16 = jnp.bfloat16
MESH = pl.DeviceIdType.MESH

N_DEV = 8
ALPHA = 2.0 ** 0.25
LN_EPS = 1e-5
CONV_CH = 512
SSM_W = 512
SSM_GROUPS = 32
SSM_GROUP = 16
SSM_STATE = 64
SSM_CH = SSM_GROUPS * SSM_STATE
SCAN_CB = 512
SCAN_NCB = SSM_CH // SCAN_CB
SCAN_GPB = SSM_GROUPS // SCAN_NCB
SCAN_UW = SCAN_GPB * SSM_GROUP
SCAN_TT = 256
ADAM_LR = 0.001
ADAM_B1 = 0.9
ADAM_B2 = 0.999
ADAM_EPS = 1e-08
ADAM_WD = 0.01
ADAM_STEP = 10
VMEM_LIMIT = 56 * 1024 * 1024


def _cparams(sem=None, **kw):
    return pltpu.CompilerParams(dimension_semantics=sem, vmem_limit_bytes=VMEM_LIMIT, **kw)


def _mm(a, b, *, name, ja=None, jb=None, jo=None, a_flat=False, b_flat=False, o_flat=False,
        ta=False, tb=False, tm, tn, tk, nj=1, out_dtype=F32, plans=(), epilogue=None):
    def dims(arr, j, flat):
        if j is None:
            return arr.shape
        if flat:
            return (arr.shape[0], arr.shape[1] // nj)
        assert arr.shape[0] == nj, (name, arr.shape, nj)
        return arr.shape[1:]

    ar, ac = dims(a, ja, a_flat)
    br, bc = dims(b, jb, b_flat)
    m, k = (ac, ar) if ta else (ar, ac)
    k2, n = (bc, br) if tb else (br, bc)
    assert k == k2, (name, a.shape, b.shape)
    assert m % tm == 0 and n % tn == 0 and k % tk == 0, (name, m, n, k, tm, tn, tk)
    njb = nj if 'b' in (ja, jb) else 1
    njc = nj if 'c' in (ja, jb) else 1
    nk = k // tk
    j_inside = njc > 1 and nk == 1 and not ta
    n_in = njc if j_inside else 1
    nred = nk if j_inside else njc * nk
    grid = (njb, m // tm, n // tn, 1 if j_inside else njc, nk)

    def make_spec(j, flat, blk, rfn, cfn, cols_per_j):
        def jsel(g, c):
            return g if j == 'b' else c
        if j is None:
            return pl.BlockSpec(blk, lambda g, i, jn, c, kk: (rfn(i, jn, kk), cfn(i, jn, kk)))
        if j == 'c' and j_inside:
            if flat:
                return pl.BlockSpec((blk[0], nj * blk[1]), lambda g, i, jn, c, kk: (rfn(i, jn, kk), 0))
            return pl.BlockSpec((nj,) + blk, lambda g, i, jn, c, kk: (0, rfn(i, jn, kk), cfn(i, jn, kk)))
        if flat:
            nb = cols_per_j // blk[1]
            return pl.BlockSpec(blk, lambda g, i, jn, c, kk: (rfn(i, jn, kk), jsel(g, c) * nb + cfn(i, jn, kk)))
        return pl.BlockSpec((None,) + blk,
                            lambda g, i, jn, c, kk: (jsel(g, c), rfn(i, jn, kk), cfn(i, jn, kk)))

    if ta:
        a_spec = make_spec(ja, a_flat, (tk, tm), lambda i, jn, kk: kk, lambda i, jn, kk: i, ac)
    else:
        a_spec = make_spec(ja, a_flat, (tm, tk), lambda i, jn, kk: i, lambda i, jn, kk: kk, ac)
    if tb:
        b_spec = make_spec(jb, b_flat, (tn, tk), lambda i, jn, kk: jn, lambda i, jn, kk: kk, bc)
    else:
        b_spec = make_spec(jb, b_flat, (tk, tn), lambda i, jn, kk: kk, lambda i, jn, kk: jn, bc)
    o_spec = make_spec(jo, o_flat, (tm, tn), lambda i, jn, kk: i, lambda i, jn, kk: jn, n)
    if jo is None:
        out_shape = (m, n)
    elif o_flat:
        out_shape = (m, nj * n)
    else:
        out_shape = (nj, m, n)

    dn = (((0 if ta else 1,), (1 if tb else 0,)), ((), ()))

    def operand(ref, j, flat, jj, width):
        if not (j == 'c' and j_inside):
            return ref[...]
        return ref[:, jj * width:(jj + 1) * width] if flat else ref[jj]

    e_fn, e_rows, e_vecs, e_dtypes = epilogue if epilogue else (None, (), (), (out_dtype,))
    assert not epilogue or (nred == 1 and jo is None), name
    n_e = len(e_rows) + len(e_vecs)
    n_o = len(e_dtypes)

    def body(a_ref, b_ref, *rest):
        e_refs, o_refs, scratch = rest[:n_e], rest[n_e:n_e + n_o], rest[n_e + n_o:]
        o_ref = o_refs[0]
        p = None
        for jj in range(n_in):
            q = lax.dot_general(operand(a_ref, ja, a_flat, jj, tk), operand(b_ref, jb, b_flat, jj, tk if tb else tn),
                                dn, preferred_element_type=F32)
            p = q if p is None else p + q
        if epilogue:
            for ref, val in zip(o_refs, e_fn(p, *[r[...] for r in e_refs])):
                ref[...] = val.astype(ref.dtype)
        elif nred == 1:
            o_ref[...] = p.astype(o_ref.dtype)
        else:
            acc = scratch[0]
            r = pl.program_id(3) * nk + pl.program_id(4)

            @pl.when(r == 0)
            def _():
                acc[...] = p

            @pl.when(r > 0)
            def _():
                acc[...] += p

            @pl.when(r == nred - 1)
            def _():
                o_ref[...] = acc[...].astype(o_ref.dtype)

    vec_spec = pl.BlockSpec((1, tn), lambda g, i, jn, c, kk: (0, jn))
    res = _call_with_plans(
        body, plans, name=name, grid=grid,
        in_specs=[a_spec, b_spec] + [o_spec] * len(e_rows) + [vec_spec] * len(e_vecs), out_specs=[o_spec] * n_o,
        out_shape=[jax.ShapeDtypeStruct(out_shape, dt) for dt in e_dtypes],
        scratch_shapes=[] if nred == 1 else [pltpu.VMEM((tm, tn), F32)],
        semantics=("parallel", "parallel", "parallel", "arbitrary", "arbitrary"), operands=(a, b, *e_rows, *e_vecs))
    outs = res[0] if epilogue else res[0][0]
    return (outs, res[1]) if plans else outs


def _sigmoid(v):
    return jax.nn.sigmoid(v)


def _row_spec(tm, cols, colblk=0):
    return pl.BlockSpec((tm, cols), lambda i: (i, colblk))


def _vec_spec(cols):
    return pl.BlockSpec((1, cols), lambda i: (0, 0))


def _ffn_in(xb, win, *, name, tm, plans=()):
    t, d = xb.shape
    nj, w, _ = win.shape
    half = nj // 2
    dn = (((1,), (1,)), ((), ()))

    def body(x_ref, wg_ref, wu_ref, a_ref, gu_ref):
        xv = x_ref[...]
        g = lax.dot_general(xv, wg_ref[...], dn, preferred_element_type=F32)
        u = lax.dot_general(xv, wu_ref[...], dn, preferred_element_type=F32)
        a_ref[...] = (g * _sigmoid(g) * u).astype(BF16)
        gu_ref[0] = g.astype(BF16)
        gu_ref[1] = u.astype(BF16)

    (a, gu), riders = _call_with_plans(
        body, plans, name=name, grid=(half, t // tm),
        in_specs=[pl.BlockSpec((tm, d), lambda j, i: (i, 0)),
                  pl.BlockSpec((None, w, d), lambda j, i: (j, 0, 0)),
                  pl.BlockSpec((None, w, d), lambda j, i: (j + half, 0, 0))],
        out_specs=[pl.BlockSpec((None, tm, w), lambda j, i: (j, i, 0)),
                   pl.BlockSpec((2, None, tm, w), lambda j, i: (0, j, i, 0))],
        out_shape=[jax.ShapeDtypeStruct((half, t, w), BF16), jax.ShapeDtypeStruct((2, half, t, w), BF16)],
        scratch_shapes=[], semantics=("parallel", "parallel"), operands=(xb, win, win))
    return a, gu, riders


def _ffn_out_dg(dfb, wout, gu, *, name, tm, plans=()):
    t, d = dfb.shape
    half, w, _ = wout.shape
    dn = (((1,), (1,)), ((), ()))

    def body(df_ref, w_ref, gu_ref, dh_ref):
        da = lax.dot_general(df_ref[...], w_ref[...], dn, preferred_element_type=F32)
        g = gu_ref[0].astype(F32)
        u = gu_ref[1].astype(F32)
        sg = _sigmoid(g)
        dh_ref[0] = (da * u * (sg * (1.0 + g * (1.0 - sg)))).astype(BF16)
        dh_ref[1] = (da * (g * sg)).astype(BF16)

    (out,), riders = _call_with_plans(
        body, plans, name=name, grid=(half, t // tm),
        in_specs=[pl.BlockSpec((tm, d), lambda j, i: (i, 0)),
                  pl.BlockSpec((None, w, d), lambda j, i: (j, 0, 0)),
                  pl.BlockSpec((2, None, tm, w), lambda j, i: (0, j, i, 0))],
        out_specs=[pl.BlockSpec((2, None, tm, w), lambda j, i: (0, j, i, 0))],
        out_shape=[jax.ShapeDtypeStruct((2, half, t, w), BF16)],
        scratch_shapes=[], semantics=("parallel", "parallel"), operands=(dfb, wout, gu))
    return out.reshape(2 * half, t, w), riders


def _ln_stats(r):
    mu = jnp.mean(r, axis=-1, keepdims=True)
    xc = r - mu
    var = jnp.mean(xc * xc, axis=-1, keepdims=True)
    rstd = lax.rsqrt(var + LN_EPS)
    return xc * rstd, rstd


def _ln_epilogue(xin, gamma, beta, fs):
    def fn(p, xv, g, b):
        r = ALPHA * xv + fs * p
        xh, _ = _ln_stats(r)
        y = xh * g + b
        return r, y, y

    return fn, (xin,), (gamma, beta), (F32, F32, BF16)


def _ln_bwd(xin, fparts, gamma, beta, grads, *, name, fs=1.0, ple=False, target=None, tm=512):
    t, d = xin.shape
    nf = len(fparts)
    ng = len(grads)
    coefs = [c for _, c in grads]
    use_t = target is not None
    n_fout = 2 if ple else 1

    def body(*refs):
        pos = 0
        x_ref = refs[pos]; pos += 1
        f_refs = refs[pos:pos + nf]; pos += nf
        g_ref, b_ref = refs[pos:pos + 2]; pos += 2
        gr_refs = refs[pos:pos + ng]; pos += ng
        if use_t:
            t_ref = refs[pos]; pos += 1
        dr_ref = refs[pos]; pos += 1
        fo_refs = refs[pos:pos + n_fout]; pos += n_fout
        dg_ref, db_ref = refs[pos:pos + 2]; pos += 2
        if use_t:
            loss_ref = refs[pos]; pos += 1
        i = pl.program_id(0)

        if ple:
            pe = f_refs[0][...]
            sg = _sigmoid(f_refs[1][...])
            resid = ALPHA * x_ref[...] + pe * sg
        else:
            resid = x_ref[...]
        xh, rstd = _ln_stats(resid)
        gam = g_ref[...]
        if use_t:
            diff = xh * gam + b_ref[...] - t_ref[...]
            dy = diff * (1.0 / d)
            lpart = 0.5 * jnp.sum(jnp.sum(diff * diff, axis=-1, keepdims=True), axis=0, keepdims=True) * (1.0 / d)
        else:
            dy = coefs[0] * gr_refs[0][...]
            for c, r in zip(coefs[1:], gr_refs[1:]):
                dy = dy + c * r[...]
        dxh = dy * gam
        m1 = jnp.mean(dxh, axis=-1, keepdims=True)
        m2 = jnp.mean(dxh * xh, axis=-1, keepdims=True)
        dr = rstd * (dxh - m1 - xh * m2)
        dr_ref[...] = dr
        if ple:
            fo_refs[0][...] = (dr * sg).astype(BF16)
            fo_refs[1][...] = (dr * pe * (sg * (1.0 - sg))).astype(BF16)
        else:
            fo_refs[0][...] = (fs * dr).astype(BF16)
        dgp = jnp.sum(dy * xh, axis=0, keepdims=True)
        dbp = jnp.sum(dy, axis=0, keepdims=True)

        @pl.when(i == 0)
        def _():
            dg_ref[...] = dgp
            db_ref[...] = dbp
            if use_t:
                loss_ref[...] = jnp.broadcast_to(lpart, loss_ref.shape)

        @pl.when(i > 0)
        def _():
            dg_ref[...] += dgp
            db_ref[...] += dbp
            if use_t:
                loss_ref[...] += jnp.broadcast_to(lpart, loss_ref.shape)

    ins = [xin, *fparts, gamma, beta, *[g for g, _ in grads]] + ([target] if use_t else [])
    in_specs = ([_row_spec(tm, d)] * (1 + nf) + [_vec_spec(d), _vec_spec(d)] + [_row_spec(tm, d)] * ng
                + ([_row_spec(tm, d)] if use_t else []))
    out_specs = [_row_spec(tm, d)] * (1 + n_fout) + [_vec_spec(d), _vec_spec(d)] + ([_vec_spec(128)] if use_t else [])
    out_shape = ([jax.ShapeDtypeStruct((t, d), F32)] + [jax.ShapeDtypeStruct((t, d), BF16)] * n_fout
                 + [jax.ShapeDtypeStruct((1, d), F32)] * 2 + ([jax.ShapeDtypeStruct((1, 128), F32)] if use_t else []))
    return pl.pallas_call(
        body, name=name, grid=(t // tm,), in_specs=in_specs, out_specs=out_specs, out_shape=out_shape,
        compiler_params=_cparams(("arbitrary",)),
    )(*ins)


def _conv_fwd(proj, cw, cb, *, name, tm=512):
    t = proj.shape[0]
    c = CONV_CH
    hb = tm // 8

    def body(b_ref, c_ref, h_ref, cp_ref, hp_ref, w_ref, bias_ref, o_ref, q_scr):
        i = pl.program_id(0)
        q = c_ref[...] * h_ref[...]
        halo = jnp.where(i > 0, cp_ref[...] * hp_ref[...], 0.0)
        q_scr[0:8, :] = halo
        q_scr[8:, :] = q
        z = (w_ref[2:3, :] * q + w_ref[1:2, :] * q_scr[pl.ds(7, tm), :] + w_ref[0:1, :] * q_scr[pl.ds(6, tm), :]
             + bias_ref[...])
        o_ref[...] = (b_ref[...] * z).astype(BF16)

    prev = lambda blk: pl.BlockSpec((8, c), lambda i: (jnp.maximum(i * hb - 1, 0), blk))
    return pl.pallas_call(
        body, name=name, grid=(t // tm,),
        in_specs=[_row_spec(tm, c, 0), _row_spec(tm, c, 1), _row_spec(tm, c, 2), prev(1), prev(2),
                  pl.BlockSpec((8, c), lambda i: (0, 0)), _vec_spec(c)],
        out_specs=_row_spec(tm, c),
        out_shape=jax.ShapeDtypeStruct((t, c), BF16),
        scratch_shapes=[pltpu.VMEM((tm + 8, c), F32)],
        compiler_params=_cparams(("parallel",)),
    )(proj, proj, proj, proj, proj, cw, cb)


def _conv_bwd(dyc, proj, cw, cb, *, name, tm=512):
    t = proj.shape[0]
    c = CONV_CH
    hb = tm // 8
    nblk = t // 8

    def body(d_ref, b_ref, c_ref, h_ref, cp_ref, hp_ref, dn_ref, bn_ref, w_ref, bias_ref,
             db_ref, dc_ref, dh_ref, dw_ref, q_scr, z_scr):
        i = pl.program_id(0)
        last = pl.num_programs(0) - 1
        cc = c_ref[...]
        ch = h_ref[...]
        q = cc * ch
        halo = jnp.where(i > 0, cp_ref[...] * hp_ref[...], 0.0)
        q_scr[0:8, :] = halo
        q_scr[8:, :] = q
        w0, w1, w2 = w_ref[0:1, :], w_ref[1:2, :], w_ref[2:3, :]
        qm1 = q_scr[pl.ds(7, tm), :]
        qm2 = q_scr[pl.ds(6, tm), :]
        z = w2 * q + w1 * qm1 + w0 * qm2 + bias_ref[...]
        d = d_ref[...]
        bb = b_ref[...]
        db_ref[...] = (d * z).astype(BF16)
        dz = d * bb
        z_scr[0:tm, :] = dz
        z_scr[tm:, :] = jnp.where(i < last, dn_ref[...] * bn_ref[...], 0.0)
        dq = w2 * dz + w1 * z_scr[pl.ds(1, tm), :] + w0 * z_scr[pl.ds(2, tm), :]
        dc_ref[...] = (dq * ch).astype(BF16)
        dh_ref[...] = (dq * cc).astype(BF16)
        row = lax.broadcasted_iota(jnp.int32, (8, c), 0)
        part = jnp.zeros((8, c), F32)
        for k, term in enumerate((dz * qm2, dz * qm1, dz * q, dz)):
            part = jnp.where(row == k, jnp.sum(term, axis=0, keepdims=True), part)

        @pl.when(i == 0)
        def _():
            dw_ref[...] = part

        @pl.when(i > 0)
        def _():
            dw_ref[...] += part

    prev = lambda blk: pl.BlockSpec((8, c), lambda i: (jnp.maximum(i * hb - 1, 0), blk))
    nxt_p = pl.BlockSpec((8, c), lambda i: (jnp.minimum((i + 1) * hb, nblk - 1), 0))
    nxt_d = pl.BlockSpec((8, c), lambda i: (jnp.minimum((i + 1) * hb, nblk - 1), 0))
    return pl.pallas_call(
        body, name=name, grid=(t // tm,),
        in_specs=[_row_spec(tm, c), _row_spec(tm, c, 0), _row_spec(tm, c, 1), _row_spec(tm, c, 2),
                  prev(1), prev(2), nxt_d, nxt_p, pl.BlockSpec((8, c), lambda i: (0, 0)), _vec_spec(c)],
        out_specs=[_row_spec(tm, c)] * 3 + [pl.BlockSpec((8, c), lambda i: (0, 0))],
        out_shape=[jax.ShapeDtypeStruct((t, c), BF16)] * 3 + [jax.ShapeDtypeStruct((8, c), F32)],
        scratch_shapes=[pltpu.VMEM((tm + 8, c), F32), pltpu.VMEM((tm + 8, c), F32)],
        compiler_params=_cparams(("arbitrary",)),
    )(dyc, proj, proj, proj, proj, proj, dyc, proj, cw, cb)


def _gate_fwd(yconv, glu, proj, *, name, tm=512):
    t, d = yconv.shape

    def body(yc_ref, ga_ref, gb_ref, gc_ref, gs_ref, o_ref):
        yssm = ga_ref[...] * _sigmoid(gb_ref[...])
        o_ref[...] = (_sigmoid(gc_ref[...]) * yc_ref[...] + _sigmoid(gs_ref[...]) * yssm).astype(BF16)

    return pl.pallas_call(
        body, name=name, grid=(t // tm,),
        in_specs=[_row_spec(tm, d), _row_spec(tm, d, 0), _row_spec(tm, d, 1), _row_spec(tm, d, 2), _row_spec(tm, d, 3)],
        out_specs=_row_spec(tm, d), out_shape=jax.ShapeDtypeStruct((t, d), BF16),
        compiler_params=_cparams(("parallel",)),
    )(yconv, glu, glu, proj, proj)


def _gate_bwd(dm, yconv, glu, proj, *, name, tm=512):
    t, d = yconv.shape

    def body(dm_ref, yc_ref, ga_ref, gb_ref, gc_ref, gs_ref, dyc_ref, dglu_ref, dgate_ref):
        dmv = dm_ref[...]
        sc = _sigmoid(gc_ref[...])
        ss = _sigmoid(gs_ref[...])
        sb = _sigmoid(gb_ref[...])
        ga = ga_ref[...]
        yssm = ga * sb
        dyc_ref[...] = (dmv * sc).astype(BF16)
        dgate_ref[:, 0:d] = (dmv * yc_ref[...] * (sc * (1.0 - sc))).astype(BF16)
        dys = dmv * ss
        dgate_ref[:, d:2 * d] = (dmv * yssm * (ss * (1.0 - ss))).astype(BF16)
        dglu_ref[:, 0:d] = (dys * sb).astype(BF16)
        dglu_ref[:, d:2 * d] = (dys * ga * (sb * (1.0 - sb))).astype(BF16)

    return pl.pallas_call(
        body, name=name, grid=(t // tm,),
        in_specs=[_row_spec(tm, d), _row_spec(tm, d), _row_spec(tm, d, 0), _row_spec(tm, d, 1),
                  _row_spec(tm, d, 2), _row_spec(tm, d, 3)],
        out_specs=[_row_spec(tm, d), _row_spec(tm, 2 * d), _row_spec(tm, 2 * d)],
        out_shape=[jax.ShapeDtypeStruct((t, d), BF16), jax.ShapeDtypeStruct((t, 2 * d), BF16),
                   jax.ShapeDtypeStruct((t, 2 * d), BF16)],
        compiler_params=_cparams(("parallel",)),
    )(dm, yconv, glu, glu, proj, proj)


_GELU_C = math.sqrt(2.0 / math.pi)


def _gelu(v):
    return 0.5 * v * (1.0 + jnp.tanh(_GELU_C * (v + 0.044715 * v * v * v)))


def _gelu_grad(v):
    th = jnp.tanh(_GELU_C * (v + 0.044715 * v * v * v))
    return 0.5 * (1.0 + th) + 0.5 * v * (1.0 - th * th) * (_GELU_C * (1.0 + 3.0 * 0.044715 * v * v))


def _cmul(ar, ai, br, bi):
    return ar * br - ai * bi, ar * bi + ai * br


def _scan_fwd(proj, bmat, tab, *, name, plans=()):
    t = proj.shape[0]
    tt, cbw = SCAN_TT, SCAN_CB
    w2 = 2 * cbw

    def body(u_ref, b_ref, tab_ref, s_ref, sb_ref, bu_scr, carry):
        ti = pl.program_id(1)

        @pl.when(ti == 0)
        def _():
            carry[...] = jnp.zeros_like(carry)

        bu_scr[...] = jnp.dot(u_ref[...].astype(BF16), b_ref[...], preferred_element_type=F32)
        row = lax.broadcasted_iota(jnp.int32, (8, cbw), 0)

        def blk(bi, c):
            cr, ci = c
            r0 = pl.multiple_of(bi * 8, 8)
            xr = bu_scr[pl.ds(r0, 8), 0:cbw]
            xi = bu_scr[pl.ds(r0, 8), cbw:w2]
            for k, sh in enumerate((1, 2, 4)):
                kr = tab_ref[k:k + 1, 0:cbw]
                ki = tab_ref[k:k + 1, cbw:w2]
                sr = jnp.where(row >= sh, pltpu.roll(xr, sh, 0), 0.0)
                si = jnp.where(row >= sh, pltpu.roll(xi, sh, 0), 0.0)
                pr, pi = _cmul(kr, ki, sr, si)
                xr = xr + pr
                xi = xi + pi
            pr, pi = _cmul(tab_ref[8:16, 0:cbw], tab_ref[8:16, cbw:w2], cr, ci)
            xr = xr + pr
            xi = xi + pi
            s_ref[pl.ds(r0, 8), 0:cbw] = xr
            s_ref[pl.ds(r0, 8), cbw:w2] = xi
            return (jnp.broadcast_to(xr[7:8, :], (8, cbw)), jnp.broadcast_to(xi[7:8, :], (8, cbw)))

        cr, ci = lax.fori_loop(0, tt // 8, blk, (carry[:, 0:cbw], carry[:, cbw:w2]))
        carry[:, 0:cbw] = cr
        carry[:, cbw:w2] = ci
        sb_ref[...] = s_ref[...].astype(BF16)

    (s, sb), riders = _call_with_plans(
        body, plans, name=name, grid=(SCAN_NCB, t // tt),
        in_specs=[pl.BlockSpec((tt, SCAN_UW), lambda cb, ti: (ti, 3 * SCAN_NCB + cb)),
                  pl.BlockSpec((None, SCAN_UW, w2), lambda cb, ti: (cb, 0, 0)),
                  pl.BlockSpec((16, w2), lambda cb, ti: (0, cb))],
        out_specs=[pl.BlockSpec((tt, w2), lambda cb, ti: (ti, cb))] * 2,
        out_shape=[jax.ShapeDtypeStruct((t, 2 * SSM_CH), F32), jax.ShapeDtypeStruct((t, 2 * SSM_CH), BF16)],
        scratch_shapes=[pltpu.VMEM((tt, w2), F32), pltpu.VMEM((8, w2), F32)],
        semantics=("parallel", "arbitrary"), operands=(proj, bmat, tab))
    return s, sb, riders


def _scan_bwd(dyb, cmat_t, s, tabb, *, name, plans=()):
    t = s.shape[0]
    tt, cbw = SCAN_TT, SCAN_CB
    w2 = 2 * cbw
    nt = t // tt
    hb = tt // 8

    def body(dy_ref, c_ref, s_ref, sp_ref, tab_ref, h_ref, da_ref, g_scr, s_scr, carry):
        ti = pl.program_id(1)

        @pl.when(ti == 0)
        def _():
            carry[...] = jnp.zeros_like(carry)
            da_ref[...] = jnp.zeros_like(da_ref)

        g_scr[...] = jnp.dot(dy_ref[...], c_ref[...], preferred_element_type=F32)
        s_scr[0:8, :] = jnp.where(ti < nt - 1, sp_ref[...], 0.0)
        s_scr[8:, :] = s_ref[...]
        row = lax.broadcasted_iota(jnp.int32, (8, cbw), 0)

        def blk(k, c):
            cr, ci, ar, ai = c
            bi = hb - 1 - k
            r0 = pl.multiple_of(bi * 8, 8)
            xr = g_scr[pl.ds(r0, 8), 0:cbw]
            xi = g_scr[pl.ds(r0, 8), cbw:w2]
            for j, sh in enumerate((1, 2, 4)):
                kr = tab_ref[j:j + 1, 0:cbw]
                ki = tab_ref[j:j + 1, cbw:w2]
                sr = jnp.where(row < 8 - sh, pltpu.roll(xr, 8 - sh, 0), 0.0)
                si = jnp.where(row < 8 - sh, pltpu.roll(xi, 8 - sh, 0), 0.0)
                pr, pi = _cmul(kr, ki, sr, si)
                xr = xr + pr
                xi = xi + pi
            pr, pi = _cmul(tab_ref[8:16, 0:cbw], tab_ref[8:16, cbw:w2], cr, ci)
            xr = xr + pr
            xi = xi + pi
            h_ref[pl.ds(r0, 8), 0:cbw] = xr.astype(BF16)
            h_ref[pl.ds(r0, 8), cbw:w2] = xi.astype(BF16)
            pvr = s_scr[pl.ds(r0, 8), 0:cbw]
            pvi = s_scr[pl.ds(r0, 8), cbw:w2]
            cur_r = s_scr[pl.ds(r0 + 8, 8), 0:cbw]
            cur_i = s_scr[pl.ds(r0 + 8, 8), cbw:w2]
            spr = jnp.where(row == 0, jnp.broadcast_to(pvr[7:8, :], (8, cbw)), pltpu.roll(cur_r, 1, 0))
            spi = jnp.where(row == 0, jnp.broadcast_to(pvi[7:8, :], (8, cbw)), pltpu.roll(cur_i, 1, 0))
            ar = ar + spr * xr + spi * xi
            ai = ai + spr * xi - spi * xr
            return (jnp.broadcast_to(xr[0:1, :], (8, cbw)), jnp.broadcast_to(xi[0:1, :], (8, cbw)), ar, ai)

        z = jnp.zeros((8, cbw), F32)
        cr, ci, ar, ai = lax.fori_loop(0, hb, blk, (carry[:, 0:cbw], carry[:, cbw:w2], z, z))
        carry[:, 0:cbw] = cr
        carry[:, cbw:w2] = ci
        da_ref[:, 0:cbw] += ar
        da_ref[:, cbw:w2] += ai

    rt = lambda ti: nt - 1 - ti
    (h, da), riders = _call_with_plans(
        body, plans, name=name, grid=(SCAN_NCB, nt),
        in_specs=[pl.BlockSpec((tt, SCAN_UW), lambda cb, ti: (rt(ti), cb)),
                  pl.BlockSpec((None, SCAN_UW, w2), lambda cb, ti: (cb, 0, 0)),
                  pl.BlockSpec((tt, w2), lambda cb, ti: (rt(ti), cb)),
                  pl.BlockSpec((8, w2), lambda cb, ti: (jnp.maximum(rt(ti) * hb - 1, 0), cb)),
                  pl.BlockSpec((16, w2), lambda cb, ti: (0, cb))],
        out_specs=[pl.BlockSpec((tt, w2), lambda cb, ti: (rt(ti), cb)),
                   pl.BlockSpec((8, w2), lambda cb, ti: (0, cb))],
        out_shape=[jax.ShapeDtypeStruct((t, 2 * SSM_CH), BF16), jax.ShapeDtypeStruct((8, 2 * SSM_CH), F32)],
        scratch_shapes=[pltpu.VMEM((tt, w2), F32), pltpu.VMEM((tt + 8, w2), F32), pltpu.VMEM((8, w2), F32)],
        semantics=("parallel", "arbitrary"), operands=(dyb, cmat_t, s, s, tabb))
    return h, da, riders


def _s5_out(ymm, proj, dvec, *, name, tm=512):
    t, w = ymm.shape

    def body(y_ref, u_ref, d_ref, yo_ref, sg_ref, ub_ref):
        u = u_ref[...]
        y = y_ref[...] + d_ref[...] * u
        yo_ref[...] = y
        sg_ref[...] = _gelu(y).astype(BF16)
        ub_ref[...] = u.astype(BF16)

    return pl.pallas_call(
        body, name=name, grid=(t // tm,),
        in_specs=[_row_spec(tm, w), _row_spec(tm, w, 3), _vec_spec(w)],
        out_specs=[_row_spec(tm, w)] * 3,
        out_shape=[jax.ShapeDtypeStruct((t, w), F32), jax.ShapeDtypeStruct((t, w), BF16), jax.ShapeDtypeStruct((t, w), BF16)],
        compiler_params=_cparams(("parallel",)),
    )(ymm, proj, dvec)


def _s5_bwd_in(dsg, y, proj, *, name, tm=512):
    t, w = y.shape

    def body(d_ref, y_ref, u_ref, dy_ref, dyb_ref, dd_ref):
        i = pl.program_id(0)
        dy = d_ref[...] * _gelu_grad(y_ref[...])
        dy_ref[...] = dy
        dyb_ref[...] = dy.astype(BF16)
        part = jnp.sum(dy * u_ref[...], axis=0, keepdims=True)

        @pl.when(i == 0)
        def _():
            dd_ref[...] = part

        @pl.when(i > 0)
        def _():
            dd_ref[...] += part

    return pl.pallas_call(
        body, name=name, grid=(t // tm,),
        in_specs=[_row_spec(tm, w), _row_spec(tm, w), _row_spec(tm, w, 3)],
        out_specs=[_row_spec(tm, w), _row_spec(tm, w), _vec_spec(w)],
        out_shape=[jax.ShapeDtypeStruct((t, w), F32), jax.ShapeDtypeStruct((t, w), BF16), jax.ShapeDtypeStruct((1, w), F32)],
        compiler_params=_cparams(("arbitrary",)),
    )(dsg, y, proj)


def _s5_du(dumm, dy, dvec, *, name, tm=512):
    t, w = dy.shape

    def body(a_ref, dy_ref, d_ref, o_ref):
        o_ref[...] = (a_ref[...] + d_ref[...] * dy_ref[...]).astype(BF16)

    return pl.pallas_call(
        body, name=name, grid=(t // tm,), in_specs=[_row_spec(tm, w), _row_spec(tm, w), _vec_spec(w)],
        out_specs=_row_spec(tm, w), out_shape=jax.ShapeDtypeStruct((t, w), BF16),
        compiler_params=_cparams(("parallel",)),
    )(dumm, dy, dvec)


def _s5_discretise(lam_re, lam_im, log_step, b_re, b_im):
    lam = lax.complex(lam_re, lam_im)
    dt = jnp.exp(log_step)[:, None]
    a = jnp.exp(lam * dt)
    bbar = ((a - 1.0) / lam)[..., None] * lax.complex(b_re, b_im)
    return jnp.real(a), jnp.imag(a), jnp.real(bbar), jnp.imag(bbar)


def _perm_cols(re, im):
    lead = re.shape[:-1]
    r = re.reshape(lead + (SCAN_NCB, 1, SCAN_CB))
    i = im.reshape(lead + (SCAN_NCB, 1, SCAN_CB))
    return jnp.concatenate([r, i], axis=-2).reshape(lead + (2 * SSM_CH,))


def _unperm_cols(x):
    lead = x.shape[:-1]
    y = x.reshape(lead + (SCAN_NCB, 2, SCAN_CB))
    return y[..., 0, :].reshape(lead + (SSM_CH,)), y[..., 1, :].reshape(lead + (SSM_CH,))


def _compact(re, im):
    _, r, c = re.shape
    eye = jnp.eye(SCAN_GPB, dtype=re.dtype)

    def half(x):
        x = x.reshape(SCAN_NCB, SCAN_GPB, r, c)
        return (eye[None, :, None, :, None] * x[:, :, :, None, :]).reshape(SCAN_NCB, SCAN_GPB * r, SCAN_GPB * c)

    return jnp.concatenate([half(re), half(im)], axis=-1)


def _compact_extract(x, r):
    c = SSM_STATE
    eye = jnp.eye(SCAN_GPB, dtype=x.dtype)
    y = x.reshape(SCAN_NCB, SCAN_GPB, r, 2, SCAN_GPB, c)
    dg = jnp.sum(y * eye[None, :, None, None, :, None], axis=4).reshape(SSM_GROUPS, r, 2, c)
    return dg[:, :, 0, :], dg[:, :, 1, :]


def _pow_table(ar, ai, descending=False):
    ar = ar.reshape(1, SSM_CH)
    ai = ai.reshape(1, SSM_CH)
    pw = [(ar, ai)]
    for _ in range(7):
        pw.append(_cmul(pw[-1][0], pw[-1][1], ar, ai))
    zero = (jnp.zeros_like(ar), jnp.zeros_like(ar))
    rows = [pw[0], pw[1], pw[3]] + [zero] * 5 + (pw[::-1] if descending else pw)
    re = jnp.concatenate([r for r, _ in rows], axis=0)
    im = jnp.concatenate([i for _, i in rows], axis=0)
    return _perm_cols(re, im)


def _place():
    x, y, c = lax.axis_index("x"), lax.axis_index("y"), lax.axis_index("c")
    chips = [(1 - x, y), (x, 1 - y), (1 - x, 1 - y)]
    return x, y, c, chips


def _dev(px, py, pc):
    return 4 * px + 2 * py + pc


class _Plan:
    def __init__(self, ins, out_shapes, sem_shapes, start, finish, middle=None):
        self.ins, self.out_shapes, self.sem_shapes = list(ins), list(out_shapes), list(sem_shapes)
        self.start, self.finish, self.middle = start, finish, middle


def _split_plan_refs(plans, in_refs, out_refs, sem_refs):
    res, i, o, s = [], 0, 0, 0
    for p in plans:
        ni, no, ns = len(p.ins), len(p.out_shapes), len(p.sem_shapes)
        res.append((in_refs[i:i + ni], out_refs[o:o + no], sem_refs[s:s + ns]))
        i, o, s = i + ni, o + no, s + ns
    return res


def _run_plans(plans, *, name):
    ins = [a for p in plans for a in p.ins]
    outs = [o for p in plans for o in p.out_shapes]
    sems = [s for p in plans for s in p.sem_shapes]
    any_spec = pl.BlockSpec(memory_space=pl.ANY)

    def body(*refs):
        parts = _split_plan_refs(plans, refs[:len(ins)], refs[len(ins):len(ins) + len(outs)], refs[len(ins) + len(outs):])
        for p, r in zip(plans, parts):
            p.start(*r)
        for p, r in zip(plans, parts):
            if p.middle:
                p.middle(*r)
        for p, r in zip(plans, parts):
            p.finish(*r)

    res = pl.pallas_call(body, name=name, in_specs=[any_spec] * len(ins), out_specs=[any_spec] * len(outs),
                         out_shape=outs, scratch_shapes=sems)(*ins)
    return _split_plan_refs(plans, [], res, [])


def _run_plans_on_sequencer(plans, peers_of, *, name, collective_id):
    ins = [a for p in plans for a in p.ins]
    outs = [o for p in plans for o in p.out_shapes]
    sems = [s for p in plans for s in p.sem_shapes]

    def body(*refs):
        x, y, c, chips = _place()
        peers = peers_of(x, y, c, chips)
        barrier = pltpu.get_barrier_semaphore()
        for peer in peers:
            pl.semaphore_signal(barrier, inc=1, device_id=peer, device_id_type=MESH)
        pl.semaphore_wait(barrier, len(peers))
        parts = _split_plan_refs(plans, refs[:len(ins)], refs[len(ins):len(ins) + len(outs)], refs[len(ins) + len(outs):])
        for p, r in zip(plans, parts):
            p.start(*r)
        for p, r in zip(plans, parts):
            if p.middle:
                p.middle(*r)
        for p, r in zip(plans, parts):
            p.finish(*r)

    res = pl.kernel(body, name=name, out_type=outs, mesh=plsc.ScalarSubcoreMesh(axis_name="seq", num_cores=1),
                    scratch_types=sems, compiler_params=pltpu.CompilerParams(collective_id=collective_id))(*ins)
    return _split_plan_refs(plans, [], list(res), [])


def _call_with_plans(body, plans, *, name, grid, in_specs, out_specs, out_shape, scratch_shapes, semantics, operands):
    plans = list(plans)
    if not plans:
        res = pl.pallas_call(body, name=name, grid=grid, in_specs=in_specs, out_specs=out_specs, out_shape=out_shape,
                             scratch_shapes=scratch_shapes, compiler_params=_cparams(semantics))(*operands)
        return list(res), []
    n_in, n_out, n_scr = len(in_specs), len(out_specs), len(scratch_shapes)
    p_ins = [a for p in plans for a in p.ins]
    p_outs = [o for p in plans for o in p.out_shapes]
    p_sems = [s for p in plans for s in p.sem_shapes]
    nsteps = math.prod(grid)
    any_spec = pl.BlockSpec(memory_space=pl.ANY)

    def wrapped(*refs):
        bounds = [n_in, len(p_ins), n_out, len(p_outs), n_scr]
        parts, pos = [], 0
        for b in bounds:
            parts.append(refs[pos:pos + b])
            pos += b
        ins, p_in, outs, p_out, scr = parts
        step = pl.program_id(0)
        for ax in range(1, len(grid)):
            step = step * grid[ax] + pl.program_id(ax)
        riders = _split_plan_refs(plans, p_in, p_out, refs[pos:])

        @pl.when(step == 0)
        def _():
            for p, r in zip(plans, riders):
                p.start(*r)

        mids = [(p, r) for p, r in zip(plans, riders) if p.middle]
        mid_step = nsteps // 2
        split_mid = mids and 0 < mid_step < nsteps - 1
        if split_mid:
            @pl.when(step == mid_step)
            def _():
                for p, r in mids:
                    p.middle(*r)

        body(*ins, *outs, *scr)

        @pl.when(step == nsteps - 1)
        def _():
            if not split_mid:
                for p, r in mids:
                    p.middle(*r)
            for p, r in zip(plans, riders):
                p.finish(*r)

    res = pl.pallas_call(
        wrapped, name=name, grid=grid, in_specs=list(in_specs) + [any_spec] * len(p_ins),
        out_specs=list(out_specs) + [any_spec] * len(p_outs), out_shape=list(out_shape) + p_outs,
        scratch_shapes=list(scratch_shapes) + p_sems, compiler_params=_cparams(("arbitrary",) * len(grid)),
    )(*operands, *p_ins)
    return list(res[:n_out]), [r[1] for r in _split_plan_refs(plans, [], res[n_out:], [])]


def _gather_plan(shards):
    n = len(shards)
    nk = 8

    def make(ins, outs, sems):
        send, recv, lsem = sems
        x, y, c, _ = _place()
        me, sib, xn, yn, dg = (x, y, c), (x, y, 1 - c), (1 - x, y, c), (x, 1 - y, c), (1 - x, 1 - y, c)

        def part(w, block, half):
            ref = outs[w].at[_dev(*block)]
            if half is None:
                return ref
            rows = shards[w].shape[0] // 2
            return ref.at[pl.ds(half * rows, rows)]

        def copy(w, k, block, to, half=None, src=None):
            dst = part(w, block, half)
            return pltpu.make_async_remote_copy(
                src_ref=dst if src is None else src, dst_ref=dst,
                send_sem=send.at[w * nk + k], recv_sem=recv.at[w * nk + k], device_id=to, device_id_type=MESH)

        mine = [pltpu.make_async_copy(ins[w], outs[w].at[_dev(*me)], lsem.at[w]) for w in range(n)]
        return copy, mine, me, sib, xn, yn, dg

    def first_copies(copy, me, sib, xn, yn, ins):
        return [copy(w, k, me, to, src=ins[w]) for w in range(n) for k, to in ((0, sib), (1, xn), (2, yn))]

    def start(ins, outs, sems):
        copy, mine, me, sib, xn, yn, _ = make(ins, outs, sems)
        for cp in mine + first_copies(copy, me, sib, xn, yn, ins):
            cp.start()

    def middle(ins, outs, sems):
        copy, _, me, sib, xn, yn, _ = make(ins, outs, sems)
        for w in range(n):
            copy(w, 1, xn, me).wait_recv()
            copy(w, 3, xn, yn, half=0).start()
            copy(w, 5, xn, sib).start()
        for w in range(n):
            copy(w, 2, yn, me).wait_recv()
            copy(w, 4, yn, xn, half=1).start()
            copy(w, 6, yn, sib).start()

    def finish(ins, outs, sems):
        copy, mine, me, sib, xn, yn, dg = make(ins, outs, sems)
        last = []
        for w in range(n):
            copy(w, 3, dg, me, half=0).wait_recv()
            copy(w, 4, dg, me, half=1).wait_recv()
            fwd = copy(w, 7, dg, sib)
            fwd.start()
            last.append(fwd)
        sx, sy, sd = (1 - me[0], me[1], 1 - me[2]), (me[0], 1 - me[1], 1 - me[2]), (1 - me[0], 1 - me[1], 1 - me[2])
        for w in range(n):
            copy(w, 0, sib, me).wait_recv()
            copy(w, 5, sx, me).wait_recv()
            copy(w, 6, sy, me).wait_recv()
            copy(w, 7, sd, me).wait_recv()
        for cp in first_copies(copy, me, sib, xn, yn, ins) + last:
            cp.wait_send()
        for w in range(n):
            copy(w, 3, xn, yn, half=0).wait_send()
            copy(w, 5, xn, sib).wait_send()
            copy(w, 4, yn, xn, half=1).wait_send()
            copy(w, 6, yn, sib).wait_send()
        for cp in mine:
            cp.wait()

    return _Plan(shards, [jax.ShapeDtypeStruct((N_DEV,) + s.shape, s.dtype) for s in shards],
                 [pltpu.SemaphoreType.DMA((nk * n,)), pltpu.SemaphoreType.DMA((nk * n,)), pltpu.SemaphoreType.DMA((n,))],
                 start, finish, middle)


def _swap_plan(copies_of, n_copies, ins, out_shapes):
    def cps(in_refs, out_refs, sems):
        return copies_of(in_refs, out_refs, sems[0], sems[1])

    def start(in_refs, out_refs, sems):
        for cp in cps(in_refs, out_refs, sems):
            cp.start()

    def finish(in_refs, out_refs, sems):
        all_cps = cps(in_refs, out_refs, sems)
        for cp in all_cps:
            cp.wait_recv()
        for cp in all_cps:
            cp.wait_send()

    return _Plan(ins, out_shapes, [pltpu.SemaphoreType.DMA((n_copies,)), pltpu.SemaphoreType.DMA((n_copies,))],
                 start, finish)


def _sibling_plan(grads):
    n = len(grads)

    def copies(ins, outs, send, recv):
        x, y, c, chips = _place()
        owners = [(x, y)] + chips
        return [pltpu.make_async_remote_copy(
            src_ref=ins[w].at[_dev(*chip, 1 - c)], dst_ref=outs[w].at[k], send_sem=send.at[w * 4 + k],
            recv_sem=recv.at[w * 4 + k], device_id=(x, y, 1 - c), device_id_type=MESH)
            for w in range(n) for k, chip in enumerate(owners)]

    return _swap_plan(copies, 4 * n, grads, [jax.ShapeDtypeStruct((4,) + g.shape[1:], g.dtype) for g in grads])


def _chip_plan(parts, js=(0, 1, 2)):
    n, nj = len(parts), len(js)

    def copies(ins, outs, send, recv):
        x, y, c, chips = _place()
        return [pltpu.make_async_remote_copy(
            src_ref=ins[w].at[j], dst_ref=outs[w * nj + k], send_sem=send.at[w * nj + k],
            recv_sem=recv.at[w * nj + k], device_id=(*chips[j], c), device_id_type=MESH)
            for w in range(n) for k, j in enumerate(js)]

    return _swap_plan(copies, n * nj, parts,
                      [jax.ShapeDtypeStruct(p.shape[1:], p.dtype) for p in parts for _ in js])


UPDATE_TILE_BYTES = 1536 * 1024


def _row_tile(r, c):
    best = 8
    for t in range(8, r + 1, 8):
        if r % t == 0 and t * c * 4 <= UPDATE_TILE_BYTES:
            best = t
    return best


def _chip_partial(g, sib, ids, *, name):
    _, r, c = g.shape
    tr = _row_tile(r, c)

    def body(ids_ref, g_ref, s_ref, o_ref):
        o_ref[...] = (g_ref[...] + s_ref[...]).astype(BF16)

    return pl.pallas_call(
        body, name=name,
        grid_spec=pltpu.PrefetchScalarGridSpec(
            num_scalar_prefetch=1, grid=(3, r // tr),
            in_specs=[pl.BlockSpec((None, tr, c), lambda j, i, ids_ref: (ids_ref[j], i, 0)),
                      pl.BlockSpec((None, tr, c), lambda j, i, ids_ref: (j + 1, i, 0))],
            out_specs=pl.BlockSpec((None, tr, c), lambda j, i, ids_ref: (j, i, 0))),
        out_shape=jax.ShapeDtypeStruct((3, r, c), BF16),
        compiler_params=_cparams(("parallel", "parallel")),
    )(ids, g, sib)


def _adamw_math(w, g, m, v):
    m = ADAM_B1 * m + (1.0 - ADAM_B1) * g
    v = ADAM_B2 * v + (1.0 - ADAM_B2) * (g * g)
    m_hat = m / (1.0 - ADAM_B1 ** ADAM_STEP)
    v_hat = v / (1.0 - ADAM_B2 ** ADAM_STEP)
    delta = -ADAM_LR * (m_hat / (jnp.sqrt(v_hat) + ADAM_EPS) + ADAM_WD * w)
    return delta, m, v


def _shard_update(g, sib, rem, me, w, m, v, *, name):
    r, c = w.shape
    tr = _row_tile(r, c)

    def body(me_ref, g_ref, s_ref, r0_ref, r1_ref, r2_ref, w_ref, m_ref, v_ref, go_ref, d_ref, mo_ref, vo_ref):
        gt = g_ref[...] + s_ref[...]
        gt = gt + r0_ref[...].astype(F32)
        gt = gt + r1_ref[...].astype(F32)
        gt = gt + r2_ref[...].astype(F32)
        go_ref[...] = gt
        d, mn, vn = _adamw_math(w_ref[...], gt, m_ref[...], v_ref[...])
        d_ref[...] = d
        mo_ref[...] = mn
        vo_ref[...] = vn

    blk = lambda k: pl.BlockSpec((None, tr, c), lambda i, me_ref: (k, i, 0))
    plain = pl.BlockSpec((tr, c), lambda i, me_ref: (i, 0))
    return pl.pallas_call(
        body, name=name,
        grid_spec=pltpu.PrefetchScalarGridSpec(
            num_scalar_prefetch=1, grid=(r // tr,),
            in_specs=[pl.BlockSpec((None, tr, c), lambda i, me_ref: (me_ref[0], i, 0)), blk(0), plain, plain, plain,
                      plain, plain, plain],
            out_specs=[plain] * 4),
        out_shape=[jax.ShapeDtypeStruct((r, c), F32)] * 4,
        compiler_params=_cparams(("parallel",)),
    )(me, g, sib, *rem, w, m, v)


def _small_update(gathered, w, m, v, *, name):
    _, r, c = gathered.shape

    def body(g_ref, w_ref, m_ref, v_ref, go_ref, d_ref, mo_ref, vo_ref):
        gt = g_ref[0]
        for k in range(1, N_DEV):
            gt = gt + g_ref[k]
        go_ref[...] = gt
        d, mn, vn = _adamw_math(w_ref[...], gt, m_ref[...], v_ref[...])
        d_ref[...] = d
        mo_ref[...] = mn
        vo_ref[...] = vn

    return pl.pallas_call(
        body, name=name, out_shape=[jax.ShapeDtypeStruct((r, c), F32)] * 4,
        compiler_params=pltpu.CompilerParams(vmem_limit_bytes=VMEM_LIMIT),
    )(gathered, w, m, v)


SMALL_UNIT = 1024


def _pack(parts):
    flat = []
    for p in parts:
        f = p.reshape(-1).astype(F32)
        pad = (-f.shape[0]) % SMALL_UNIT
        flat.append(jnp.pad(f, (0, pad)))
    return jnp.concatenate(flat).reshape(-1, 128)


def _unpack(buf, shapes):
    flat = buf.reshape(-1)
    out, off = [], 0
    for s in shapes:
        nel = math.prod(s)
        out.append(flat[off:off + nel].reshape(s))
        off += nel + ((-nel) % SMALL_UNIT)
    return out


def kernel(x, p, ffn1_w_in, ffn1_w_out, ln1_g, ln1_b, mix_w_in, conv_w, conv_b, conv_w_out, ssm_lam_re, ssm_lam_im, ssm_log_step, ssm_b_re, ssm_b_im, ssm_c_re, ssm_c_im, ssm_d, ssm_w_glu, mix_w_out, ln2_g, ln2_b, ffn2_w_in, ffn2_w_out, ln3_g, ln3_b, ple_w_in, ple_w_gate, ln4_g, ln4_b, loss_target, m_ffn1_w_in, m_ffn1_w_out, m_ln1_g, m_ln1_b, m_mix_w_in, m_conv_w, m_conv_b, m_conv_w_out, m_ssm_lam_re, m_ssm_lam_im, m_ssm_log_step, m_ssm_b_re, m_ssm_b_im, m_ssm_c_re, m_ssm_c_im, m_ssm_d, m_ssm_w_glu, m_mix_w_out, m_ln2_g, m_ln2_b, m_ffn2_w_in, m_ffn2_w_out, m_ln3_g, m_ln3_b, m_ple_w_in, m_ple_w_gate, m_ln4_g, m_ln4_b, v_ffn1_w_in, v_ffn1_w_out, v_ln1_g, v_ln1_b, v_mix_w_in, v_conv_w, v_conv_b, v_conv_w_out, v_ssm_lam_re, v_ssm_lam_im, v_ssm_log_step, v_ssm_b_re, v_ssm_b_im, v_ssm_c_re, v_ssm_c_im, v_ssm_d, v_ssm_w_glu, v_mix_w_out, v_ln2_g, v_ln2_b, v_ffn2_w_in, v_ffn2_w_out, v_ln3_g, v_ln3_b, v_ple_w_in, v_ple_w_gate, v_ln4_g, v_ln4_b):
    args = dict(locals())
    big = ['ffn1_w_in', 'ffn1_w_out', 'mix_w_in', 'conv_w_out', 'ssm_w_glu', 'mix_w_out',
           'ffn2_w_in', 'ffn2_w_out', 'ple_w_in', 'ple_w_gate']
    small = ['ln1_g', 'ln1_b', 'conv_b', 'ssm_lam_re', 'ssm_lam_im', 'ssm_log_step', 'ssm_b_re', 'ssm_b_im',
             'ssm_c_re', 'ssm_c_im', 'ssm_d', 'ln2_g', 'ln2_b', 'ln3_g', 'ln3_b', 'ln4_g', 'ln4_b']
    order = ['ffn1_w_in', 'ffn1_w_out', 'ln1_g', 'ln1_b', 'mix_w_in', 'conv_w', 'conv_b', 'conv_w_out',
             'ssm_lam_re', 'ssm_lam_im', 'ssm_log_step', 'ssm_b_re', 'ssm_b_im', 'ssm_c_re', 'ssm_c_im', 'ssm_d',
             'ssm_w_glu', 'mix_w_out', 'ln2_g', 'ln2_b', 'ffn2_w_in', 'ffn2_w_out', 'ln3_g', 'ln3_b',
             'ple_w_in', 'ple_w_gate', 'ln4_g', 'ln4_b']

    t = x.shape[1]
    d = x.shape[2]
    xc_, yc_, cc_ = lax.axis_index("x"), lax.axis_index("y"), lax.axis_index("c")
    me = (4 * xc_ + 2 * yc_ + cc_).astype(jnp.int32)
    cw_cols = conv_w.shape[2]

    turned = ('ffn1_w_in', 'ffn2_w_in')

    def local(a, nm):
        return jnp.swapaxes(a[0], 0, 1) if nm in turned else a[0]

    shard = {nm: local(args[nm], nm).astype(BF16) for nm in big}
    cw_pad = jnp.zeros((16, 128), F32).at[0:3, 0:cw_cols].set(conv_w[0])
    wf = shard['ffn1_w_in'].shape[0]

    def gather(*names):
        return _gather_plan([shard[nm] for nm in names])

    ((_, (w1in, cw_g), _),) = _run_plans([_gather_plan([shard['ffn1_w_in'], cw_pad])], name="gather_ffn1_in")
    cw_full = jnp.transpose(cw_g[:, 0:3, 0:cw_cols], (1, 0, 2)).reshape(3, N_DEV * cw_cols)
    cw8 = jnp.zeros((8, CONV_CH), F32).at[0:3, :].set(cw_full)

    s5_in = (ssm_lam_re[0], ssm_lam_im[0], ssm_log_step[0], ssm_b_re[0], ssm_b_im[0])
    (a_re, a_im, bb_re, bb_im), s5_vjp = jax.vjp(_s5_discretise, *s5_in)
    tab_f = _pow_table(a_re, a_im)
    tab_b = _pow_table(a_re, -a_im, descending=True)
    bmat_b = _compact(jnp.transpose(bb_re, (0, 2, 1)), jnp.transpose(bb_im, (0, 2, 1))).astype(BF16)
    cmat_tb = _compact(ssm_c_re[0], -ssm_c_im[0]).astype(BF16)
    dvec = ssm_d[0].reshape(1, SSM_W)

    xf = x[0]
    x_b = xf.astype(BF16)
    p_b = p[0, 0].astype(BF16)
    tgt = loss_target[0]
    tq = min(512, t)

    ffn_out = dict(ja='c', jb='c', nj=4, tm=tq, tn=d, tk=wf)
    def side_by_side(wb):
        return jnp.transpose(wb, (1, 0, 2)).reshape(wb.shape[1], N_DEV * wb.shape[2])

    tf = min(1024, t)
    a1, h1, ((w1out_g,),) = _ffn_in(x_b, w1in, name="ffn1_in", tm=tf, plans=[gather('ffn1_w_out')])
    w1out = w1out_g.reshape(4, wf, d)
    (r1, x1, x1b), ((wmix,),) = _mm(a1, w1out, name="ffn1_out", **ffn_out, plans=[gather('mix_w_in')],
                                    epilogue=_ln_epilogue(xf, ln1_g, ln1_b, 0.5))
    proj, ((wco, wglu, wmo_g),) = _mm(x1b, wmix, name="mix_in", jb='b', jo='b', o_flat=True, nj=8, tm=tq, tn=512,
                                      tk=d, plans=[gather('conv_w_out', 'ssm_w_glu', 'mix_w_out')])
    wmo = wmo_g.reshape(d, d)
    ycin = _conv_fwd(proj, cw8, conv_b, name="conv_fwd")
    wco, wglu = side_by_side(wco), side_by_side(wglu)
    yconv = _mm(ycin, wco, name="conv_out", tm=tq, tn=d, tk=CONV_CH)
    s_f, s_b, ((w2in,),) = _scan_fwd(proj, bmat_b, tab_f, name="scan_fwd", plans=[gather('ffn2_w_in')])
    blk = dict(ja='b', jb='b', jo='b', nj=SCAN_NCB)
    ymm = _mm(s_b, cmat_tb, name="ssm_read", a_flat=True, o_flat=True, tb=True, tm=tq, tn=SCAN_UW, tk=2 * SCAN_CB, **blk)
    ys, sg, u_b = _s5_out(ymm, proj, dvec, name="ssm_out")
    glu = _mm(sg, wglu, name="glu_in", tm=tq, tn=d, tk=SSM_W)
    merged = _gate_fwd(yconv, glu, proj, name="gate_fwd")
    r2, x2, x2b = _mm(merged, wmo, name="mix_out", tm=tq, tn=d, tk=d, epilogue=_ln_epilogue(x1, ln2_g, ln2_b, 1.0))
    a2, h2, ((w2out_g,),) = _ffn_in(x2b, w2in, name="ffn2_in", tm=tf, plans=[gather('ffn2_w_out')])
    w2out = w2out_g.reshape(4, wf, d)
    (r3, x3, x3b), ((wpin, wgate_g),) = _mm(a2, w2out, name="ffn2_out", **ffn_out, plans=[gather('ple_w_in', 'ple_w_gate')],
                                            epilogue=_ln_epilogue(x2, ln3_g, ln3_b, 0.5))
    wgate = wgate_g.reshape(d, d)
    pe = _mm(p_b, side_by_side(wpin), name="ple_in", tm=tq, tn=d, tk=p_b.shape[1])
    gp = _mm(x3b, wgate, name="ple_gate", tm=tq, tn=d, tk=d)

    dr4, dpe_b, dgp_b, dg4, db4, loss_part = _ln_bwd(x3, [pe, gp], ln4_g, ln4_b, [], name="ple_ln_bwd",
                                                     ple=True, target=tgt)
    gb, sib, rem = {}, {}, {}
    ids = jnp.stack([_dev(1 - xc_, yc_, cc_), _dev(xc_, 1 - yc_, cc_), _dev(1 - xc_, 1 - yc_, cc_)]).astype(jnp.int32)

    def blocked(nm, g):
        return g.reshape((N_DEV,) + args[nm].shape[1:])

    def to_sibling(*names):
        return _sibling_plan([gb[nm] for nm in names])

    def chip_sums(names, sibs):
        for nm, s in zip(names, sibs):
            sib[nm] = s
        return [_chip_partial(gb[nm], sib[nm], ids, name=f"chip_sum_{nm}") for nm in names]

    ffn_in_dg = dict(ja='c', jb='c', nj=8, tm=tq, tn=d, tk=wf)
    ffn_in_wg = dict(ja='b', jo='b', ta=True, nj=8, tm=wf, tn=d, tk=t)
    ffn_out_wg = dict(ja='b', jo='b', ta=True, nj=4, tm=wf, tn=d, tk=t)

    gb['ple_w_in'] = _mm(p_b, dpe_b, name="ple_in_wg", jb='b', jo='b', b_flat=True, ta=True, nj=8,
                         tm=p_b.shape[1], tn=128, tk=t)
    gb['ple_w_gate'] = blocked('ple_w_gate', _mm(x3b, dgp_b, name="ple_gate_wg", ta=True, tm=d, tn=d, tk=t))
    g_ple = ['ple_w_in', 'ple_w_gate']
    dx3_g, (s_,) = _mm(dgp_b, wgate, name="ple_gate_dg", tb=True, tm=tq, tn=d, tk=d, plans=[to_sibling(*g_ple)])
    part = chip_sums(g_ple, s_)

    dr3, df2_b, dg3, db3 = _ln_bwd(r3, [], ln3_g, ln3_b, [(dr4, ALPHA), (dx3_g, 1.0)], name="ffn2_ln_bwd", fs=0.5)
    dh2, _ = _ffn_out_dg(df2_b, w2out, h2, name="ffn2_out_dg", tm=tf)
    g_, (r_,) = _mm(a2, df2_b, name="ffn2_out_wg", **ffn_out_wg, plans=[_chip_plan(part)])
    gb['ffn2_w_out'] = blocked('ffn2_w_out', g_)
    rem['ple_w_in'], rem['ple_w_gate'] = r_[0:3], r_[3:6]
    dx2_f, (s_,) = _mm(dh2, w2in, name="ffn2_in_dg", **ffn_in_dg, plans=[to_sibling('ffn2_w_out')])
    part = chip_sums(['ffn2_w_out'], s_)
    gb['ffn2_w_in'], (r_,) = _mm(dh2, x2b, name="ffn2_in_wg", **ffn_in_wg, plans=[_chip_plan(part)])
    rem['ffn2_w_out'] = r_

    dr2, dmix_b, dg2, db2 = _ln_bwd(r2, [], ln2_g, ln2_b, [(dr3, ALPHA), (dx2_f, 1.0)], name="mix_ln_bwd")
    dmerged, (s_,) = _mm(dmix_b, wmo, name="mix_out_dg", tb=True, tm=tq, tn=d, tk=d, plans=[to_sibling('ffn2_w_in')])
    part = chip_sums(['ffn2_w_in'], s_)
    gb['mix_w_out'] = blocked('mix_w_out', _mm(merged, dmix_b, name="mix_out_wg", ta=True, tm=d, tn=d, tk=t))
    dyconv_b, dglu_b, dgate_b = _gate_bwd(dmerged, yconv, glu, proj, name="gate_bwd")
    gb['conv_w_out'] = _mm(ycin, dyconv_b, name="conv_out_wg", jb='b', jo='b', b_flat=True, ta=True, nj=8,
                           tm=CONV_CH, tn=128, tk=t)
    dycin = _mm(dyconv_b, wco, name="conv_out_dg", tb=True, tm=tq, tn=CONV_CH, tk=d)
    gb['ssm_w_glu'] = _mm(sg, dglu_b, name="glu_in_wg", jb='b', jo='b', b_flat=True, ta=True, nj=8,
                          tm=SSM_W, tn=256, tk=t)
    g_mix = ['mix_w_out', 'conv_w_out', 'ssm_w_glu']
    dsg, (s_,) = _mm(dglu_b, wglu, name="glu_in_dg", tb=True, tm=tq, tn=SSM_W, tk=2 * d, plans=[to_sibling(*g_mix)])
    part_mix = chip_sums(g_mix, s_)
    dys, dys_b, dd = _s5_bwd_in(dsg, ys, proj, name="ssm_out_bwd")
    h_b, da_acc, (rem['ffn2_w_in'],) = _scan_bwd(dys_b, cmat_tb, s_f, tab_b, name="scan_bwd", plans=[_chip_plan(part)])
    dumm = _mm(h_b, bmat_b, name="ssm_write_dg", a_flat=True, o_flat=True, tb=True, tm=tq, tn=SCAN_UW,
               tk=2 * SCAN_CB, **blk)
    du_b = _s5_du(dumm, dys, dvec, name="ssm_du")
    g_bmat = _mm(u_b, h_b, name="ssm_write_wg", a_flat=True, b_flat=True, ta=True, tm=SCAN_UW,
                 tn=2 * SCAN_CB, tk=t, **blk)
    g_cmat = _mm(dys_b, s_b, name="ssm_read_wg", a_flat=True, b_flat=True, ta=True, tm=SCAN_UW,
                 tn=2 * SCAN_CB, tk=t, **blk)
    dcb_b, dcc_b, dch_b, dconv = _conv_bwd(dycin, proj, cw8, conv_b, name="conv_bwd")
    dproj = jnp.concatenate([dcb_b, dcc_b, dch_b, du_b, dgate_b], axis=1)
    gb['mix_w_in'], (r_,) = _mm(x1b, dproj, name="mix_in_wg", jb='b', jo='b', b_flat=True, ta=True, nj=8,
                                tm=d, tn=512, tk=t, plans=[_chip_plan(part_mix)])
    for i, nm in enumerate(g_mix):
        rem[nm] = r_[3 * i:3 * i + 3]
    dx1_m, (s_,) = _mm(dproj, wmix, name="mix_in_dg", ja='c', jb='c', a_flat=True, tb=True, nj=8,
                       tm=tq, tn=d, tk=512, plans=[to_sibling('mix_w_in')])
    part = chip_sums(['mix_w_in'], s_)

    dr1, df1_b, dg1, db1 = _ln_bwd(r1, [], ln1_g, ln1_b, [(dr2, ALPHA), (dx1_m, 1.0)], name="ffn1_ln_bwd", fs=0.5)
    dh1, (r01,) = _ffn_out_dg(df1_b, w1out, h1, name="ffn1_out_dg", tm=tf, plans=[_chip_plan(part, js=(0, 1))])
    part_mix_in = part

    da_sum = jnp.sum(da_acc, axis=0)
    da_re, da_im = _unperm_cols(da_sum)
    gbb_re, gbb_im = [jnp.transpose(v, (0, 2, 1)) for v in _compact_extract(g_bmat, SSM_GROUP)]
    g_c_re, g_c_im_neg = _compact_extract(g_cmat, SSM_GROUP)
    g_c_im = -g_c_im_neg
    g_lam_re, g_lam_im, g_log_step, g_b_re, g_b_im = s5_vjp(
        (da_re.reshape(SSM_GROUPS, SSM_STATE), da_im.reshape(SSM_GROUPS, SSM_STATE), gbb_re, gbb_im))
    g_d = dd.reshape(SSM_GROUPS, SSM_GROUP)

    small_g = {'ln1_g': dg1, 'ln1_b': db1, 'conv_b': dconv[3:4], 'ssm_lam_re': g_lam_re, 'ssm_lam_im': g_lam_im,
               'ssm_log_step': g_log_step, 'ssm_b_re': g_b_re, 'ssm_b_im': g_b_im, 'ssm_c_re': g_c_re,
               'ssm_c_im': g_c_im, 'ssm_d': g_d, 'ln2_g': dg2, 'ln2_b': db2, 'ln3_g': dg3, 'ln3_b': db3,
               'ln4_g': dg4, 'ln4_b': db4}
    small_shapes = [args[nm].shape for nm in small] + [(3, CONV_CH), (1,)]
    g_pack = _pack([small_g[nm] for nm in small] + [dconv[0:3], loss_part[0:1, 0:1]])

    gb['ffn1_w_in'], ((g_all,),) = _mm(dh1, x_b, name="ffn1_in_wg", **ffn_in_wg, plans=[_gather_plan([g_pack])])
    g_, (s_, r2) = _mm(a1, df1_b, name="ffn1_out_wg", **ffn_out_wg,
                       plans=[to_sibling('ffn1_w_in'), _chip_plan(part_mix_in, js=(2,))])
    rem['mix_w_in'] = r01 + r2
    gb['ffn1_w_out'] = blocked('ffn1_w_out', g_)
    part = chip_sums(['ffn1_w_in'], s_)
    (grad_x,), (r_, s_) = _mm(dh1, w1in, name="ffn1_in_dg", **ffn_in_dg,
                              plans=[_chip_plan(part), to_sibling('ffn1_w_out')],
                              epilogue=(lambda pr, drv: (pr + ALPHA * drv,), (dr1,), (), (F32,)))
    rem['ffn1_w_in'] = r_
    part = chip_sums(['ffn1_w_out'], s_)
    ((_, rem['ffn1_w_out'], _),) = _run_plans_on_sequencer(
        [_chip_plan(part)], lambda x, y, c, chips: [(*chip, c) for chip in chips], name="grad_chips_ffn1_out",
        collective_id=1)

    def full_cw(a):
        return lax.dynamic_update_slice(jnp.zeros((3, CONV_CH), F32), a[0], (0, me * cw_cols))

    zero1 = jnp.zeros((1,), F32)
    w_pack = _pack([args[nm] for nm in small] + [full_cw(conv_w), zero1])
    m_pack = _pack([args['m_' + nm] for nm in small] + [full_cw(m_conv_w), zero1])
    v_pack = _pack([args['v_' + nm] for nm in small] + [full_cw(v_conv_w), zero1])
    sg_sum, sd, sm, sv = _small_update(g_all, w_pack, m_pack, v_pack, name="small_update")
    res = {}
    for key, buf in (('grad_', sg_sum), ('delta_', sd), ('new_m_', sm), ('new_v_', sv)):
        parts = _unpack(buf, small_shapes)
        for nm, val in zip(small, parts[:len(small)]):
            res[key + nm] = val
        res[key + 'conv_w'] = lax.dynamic_slice(parts[len(small)], (0, me * cw_cols), (3, cw_cols))[None]
        if key == 'grad_':
            loss = parts[-1][0]

    me1 = me.reshape(1)
    for nm in [w for w in big if w != 'ffn1_w_out'] + ['ffn1_w_out']:
        upd = _shard_update(gb[nm], sib[nm], rem[nm], me1, local(args[nm], nm), local(args['m_' + nm], nm),
                            local(args['v_' + nm], nm), name=f"update_{nm}")
        for key, val in zip(('grad_', 'delta_', 'new_m_', 'new_v_'), upd):
            res[key + nm] = (jnp.swapaxes(val, 0, 1) if nm in turned else val)[None]

    outs = [loss, grad_x[None]]
    for key in ('grad_', 'delta_', 'new_m_', 'new_v_'):
        outs += [res[key + nm] for nm in order]
    return tuple(outs)
```

```python
import functools
import math

import jax
import jax.numpy as jnp
from jax import lax
from jax.experimental import pallas as pl
from jax.experimental.pallas import tpu as pltpu
from jax.experimental.pallas import tpu_sc as plsc

F32 = jnp.float32
BF16 = jnp.bfloat16
MESH = pl.DeviceIdType.MESH

N_DEV = 8
ALPHA = 2.0 ** 0.25
LN_EPS = 1e-5
CONV_CH = 512
SSM_W = 512
SSM_GROUPS = 32
SSM_GROUP = 16
SSM_STATE = 64
SSM_CH = SSM_GROUPS * SSM_STATE
SCAN_CB = 512
SCAN_NCB = SSM_CH // SCAN_CB
SCAN_GPB = SSM_GROUPS // SCAN_NCB
SCAN_UW = SCAN_GPB * SSM_GROUP
SCAN_TT = 256
ADAM_LR = 0.001
ADAM_B1 = 0.9
ADAM_B2 = 0.999
ADAM_EPS = 1e-08
ADAM_WD = 0.01
ADAM_STEP = 10
VMEM_LIMIT = 56 * 1024 * 1024


def _cparams(sem=None, **kw):
    return pltpu.CompilerParams(dimension_semantics=sem, vmem_limit_bytes=VMEM_LIMIT, **kw)


def _mm(a, b, *, name, ja=None, jb=None, jo=None, a_flat=False, b_flat=False, o_flat=False,
        ta=False, tb=False, tm, tn, tk, nj=1, out_dtype=F32, plans=(), epilogue=None):
    def dims(arr, j, flat):
        if j is None:
            return arr.shape
        if flat:
            return (arr.shape[0], arr.shape[1] // nj)
        assert arr.shape[0] == nj, (name, arr.shape, nj)
        return arr.shape[1:]

    ar, ac = dims(a, ja, a_flat)
    br, bc = dims(b, jb, b_flat)
    m, k = (ac, ar) if ta else (ar, ac)
    k2, n = (bc, br) if tb else (br, bc)
    assert k == k2, (name, a.shape, b.shape)
    assert m % tm == 0 and n % tn == 0 and k % tk == 0, (name, m, n, k, tm, tn, tk)
    njb = nj if 'b' in (ja, jb) else 1
    njc = nj if 'c' in (ja, jb) else 1
    nk = k // tk
    j_inside = njc > 1 and nk == 1 and not ta
    n_in = njc if j_inside else 1
    nred = nk if j_inside else njc * nk
    grid = (njb, m // tm, n // tn, 1 if j_inside else njc, nk)

    def make_spec(j, flat, blk, rfn, cfn, cols_per_j):
        def jsel(g, c):
            return g if j == 'b' else c
        if j is None:
            return pl.BlockSpec(blk, lambda g, i, jn, c, kk: (rfn(i, jn, kk), cfn(i, jn, kk)))
        if j == 'c' and j_inside:
            if flat:
                return pl.BlockSpec((blk[0], nj * blk[1]), lambda g, i, jn, c, kk: (rfn(i, jn, kk), 0))
            return pl.BlockSpec((nj,) + blk, lambda g, i, jn, c, kk: (0, rfn(i, jn, kk), cfn(i, jn, kk)))
        if flat:
            nb = cols_per_j // blk[1]
            return pl.BlockSpec(blk, lambda g, i, jn, c, kk: (rfn(i, jn, kk), jsel(g, c) * nb + cfn(i, jn, kk)))
        return pl.BlockSpec((None,) + blk,
                            lambda g, i, jn, c, kk: (jsel(g, c), rfn(i, jn, kk), cfn(i, jn, kk)))

    if ta:
        a_spec = make_spec(ja, a_flat, (tk, tm), lambda i, jn, kk: kk, lambda i, jn, kk: i, ac)
    else:
        a_spec = make_spec(ja, a_flat, (tm, tk), lambda i, jn, kk: i, lambda i, jn, kk: kk, ac)
    if tb:
        b_spec = make_spec(jb, b_flat, (tn, tk), lambda i, jn, kk: jn, lambda i, jn, kk: kk, bc)
    else:
        b_spec = make_spec(jb, b_flat, (tk, tn), lambda i, jn, kk: kk, lambda i, jn, kk: jn, bc)
    o_spec = make_spec(jo, o_flat, (tm, tn), lambda i, jn, kk: i, lambda i, jn, kk: jn, n)
    if jo is None:
        out_shape = (m, n)
    elif o_flat:
        out_shape = (m, nj * n)
    else:
        out_shape = (nj, m, n)

    dn = (((0 if ta else 1,), (1 if tb else 0,)), ((), ()))

    def operand(ref, j, flat, jj, width):
        if not (j == 'c' and j_inside):
            return ref[...]
        return ref[:, jj * width:(jj + 1) * width] if flat else ref[jj]

    e_fn, e_rows, e_vecs, e_dtypes = epilogue if epilogue else (None, (), (), (out_dtype,))
    assert not epilogue or (nred == 1 and jo is None), name
    n_e = len(e_rows) + len(e_vecs)
    n_o = len(e_dtypes)

    def body(a_ref, b_ref, *rest):
        e_refs, o_refs, scratch = rest[:n_e], rest[n_e:n_e + n_o], rest[n_e + n_o:]
        o_ref = o_refs[0]
        p = None
        for jj in range(n_in):
            q = lax.dot_general(operand(a_ref, ja, a_flat, jj, tk), operand(b_ref, jb, b_flat, jj, tk if tb else tn),
                                dn, preferred_element_type=F32)
            p = q if p is None else p + q
        if epilogue:
            for ref, val in zip(o_refs, e_fn(p, *[r[...] for r in e_refs])):
                ref[...] = val.astype(ref.dtype)
        elif nred == 1:
            o_ref[...] = p.astype(o_ref.dtype)
        else:
            acc = scratch[0]
            r = pl.program_id(3) * nk + pl.program_id(4)

            @pl.when(r == 0)
            def _():
                acc[...] = p

            @pl.when(r > 0)
            def _():
                acc[...] += p

            @pl.when(r == nred - 1)
            def _():
                o_ref[...] = acc[...].astype(o_ref.dtype)

    vec_spec = pl.BlockSpec((1, tn), lambda g, i, jn, c, kk: (0, jn))
    res = _call_with_plans(
        body, plans, name=name, grid=grid,
        in_specs=[a_spec, b_spec] + [o_spec] * len(e_rows) + [vec_spec] * len(e_vecs), out_specs=[o_spec] * n_o,
        out_shape=[jax.ShapeDtypeStruct(out_shape, dt) for dt in e_dtypes],
        scratch_shapes=[] if nred == 1 else [pltpu.VMEM((tm, tn), F32)],
        semantics=("parallel", "parallel", "parallel", "arbitrary", "arbitrary"), operands=(a, b, *e_rows, *e_vecs))
    outs = res[0] if epilogue else res[0][0]
    return (outs, res[1]) if plans else outs


def _sigmoid(v):
    return jax.nn.sigmoid(v)


def _row_spec(tm, cols, colblk=0):
    return pl.BlockSpec((tm, cols), lambda i: (i, colblk))


def _vec_spec(cols):
    return pl.BlockSpec((1, cols), lambda i: (0, 0))


def _ffn_in(xb, win, *, name, tm, plans=()):
    t, d = xb.shape
    nj, w, _ = win.shape
    half = nj // 2
    dn = (((1,), (1,)), ((), ()))

    def body(x_ref, wg_ref, wu_ref, a_ref, gu_ref):
        xv = x_ref[...]
        g = lax.dot_general(xv, wg_ref[...], dn, preferred_element_type=F32)
        u = lax.dot_general(xv, wu_ref[...], dn, preferred_element_type=F32)
        a_ref[...] = (g * _sigmoid(g) * u).astype(BF16)
        gu_ref[0] = g.astype(BF16)
        gu_ref[1] = u.astype(BF16)

    (a, gu), riders = _call_with_plans(
        body, plans, name=name, grid=(half, t // tm),
        in_specs=[pl.BlockSpec((tm, d), lambda j, i: (i, 0)),
                  pl.BlockSpec((None, w, d), lambda j, i: (j, 0, 0)),
                  pl.BlockSpec((None, w, d), lambda j, i: (j + half, 0, 0))],
        out_specs=[pl.BlockSpec((None, tm, w), lambda j, i: (j, i, 0)),
                   pl.BlockSpec((2, None, tm, w), lambda j, i: (0, j, i, 0))],
        out_shape=[jax.ShapeDtypeStruct((half, t, w), BF16), jax.ShapeDtypeStruct((2, half, t, w), BF16)],
        scratch_shapes=[], semantics=("parallel", "parallel"), operands=(xb, win, win))
    return a, gu, riders


def _ffn_out_dg(dfb, wout, gu, *, name, tm, plans=()):
    t, d = dfb.shape
    half, w, _ = wout.shape
    dn = (((1,), (1,)), ((), ()))

    def body(df_ref, w_ref, gu_ref, dh_ref):
        da = lax.dot_general(df_ref[...], w_ref[...], dn, preferred_element_type=F32)
        g = gu_ref[0].astype(F32)
        u = gu_ref[1].astype(F32)
        sg = _sigmoid(g)
        dh_ref[0] = (da * u * (sg * (1.0 + g * (1.0 - sg)))).astype(BF16)
        dh_ref[1] = (da * (g * sg)).astype(BF16)

    (out,), riders = _call_with_plans(
        body, plans, name=name, grid=(half, t // tm),
        in_specs=[pl.BlockSpec((tm, d), lambda j, i: (i, 0)),
                  pl.BlockSpec((None, w, d), lambda j, i: (j, 0, 0)),
                  pl.BlockSpec((2, None, tm, w), lambda j, i: (0, j, i, 0))],
        out_specs=[pl.BlockSpec((2, None, tm, w), lambda j, i: (0, j, i, 0))],
        out_shape=[jax.ShapeDtypeStruct((2, half, t, w), BF16)],
        scratch_shapes=[], semantics=("parallel", "parallel"), operands=(dfb, wout, gu))
    return out.reshape(2 * half, t, w), riders


def _ln_stats(r):
    mu = jnp.mean(r, axis=-1, keepdims=True)
    xc = r - mu
    var = jnp.mean(xc * xc, axis=-1, keepdims=True)
    rstd = lax.rsqrt(var + LN_EPS)
    return xc * rstd, rstd


def _ln_epilogue(xin, gamma, beta, fs):
    def fn(p, xv, g, b):
        r = ALPHA * xv + fs * p
        xh, _ = _ln_stats(r)
        y = xh * g + b
        return r, y, y

    return fn, (xin,), (gamma, beta), (F32, F32, BF16)


def _ln_bwd(xin, fparts, gamma, beta, grads, *, name, fs=1.0, ple=False, target=None, tm=512):
    t, d = xin.shape
    nf = len(fparts)
    ng = len(grads)
    coefs = [c for _, c in grads]
    use_t = target is not None
    n_fout = 2 if ple else 1

    def body(*refs):
        pos = 0
        x_ref = refs[pos]; pos += 1
        f_refs = refs[pos:pos + nf]; pos += nf
        g_ref, b_ref = refs[pos:pos + 2]; pos += 2
        gr_refs = refs[pos:pos + ng]; pos += ng
        if use_t:
            t_ref = refs[pos]; pos += 1
        dr_ref = refs[pos]; pos += 1
        fo_refs = refs[pos:pos + n_fout]; pos += n_fout
        dg_ref, db_ref = refs[pos:pos + 2]; pos += 2
        if use_t:
            loss_ref = refs[pos]; pos += 1
        i = pl.program_id(0)

        if ple:
            pe = f_refs[0][...]
            sg = _sigmoid(f_refs[1][...])
            resid = ALPHA * x_ref[...] + pe * sg
        else:
            resid = x_ref[...]
        xh, rstd = _ln_stats(resid)
        gam = g_ref[...]
        if use_t:
            diff = xh * gam + b_ref[...] - t_ref[...]
            dy = diff * (1.0 / d)
            lpart = 0.5 * jnp.sum(jnp.sum(diff * diff, axis=-1, keepdims=True), axis=0, keepdims=True) * (1.0 / d)
        else:
            dy = coefs[0] * gr_refs[0][...]
            for c, r in zip(coefs[1:], gr_refs[1:]):
                dy = dy + c * r[...]
        dxh = dy * gam
        m1 = jnp.mean(dxh, axis=-1, keepdims=True)
        m2 = jnp.mean(dxh * xh, axis=-1, keepdims=True)
        dr = rstd * (dxh - m1 - xh * m2)
        dr_ref[...] = dr
        if ple:
            fo_refs[0][...] = (dr * sg).astype(BF16)
            fo_refs[1][...] = (dr * pe * (sg * (1.0 - sg))).astype(BF16)
        else:
            fo_refs[0][...] = (fs * dr).astype(BF16)
        dgp = jnp.sum(dy * xh, axis=0, keepdims=True)
        dbp = jnp.sum(dy, axis=0, keepdims=True)

        @pl.when(i == 0)
        def _():
            dg_ref[...] = dgp
            db_ref[...] = dbp
            if use_t:
                loss_ref[...] = jnp.broadcast_to(lpart, loss_ref.shape)

        @pl.when(i > 0)
        def _():
            dg_ref[...] += dgp
            db_ref[...] += dbp
            if use_t:
                loss_ref[...] += jnp.broadcast_to(lpart, loss_ref.shape)

    ins = [xin, *fparts, gamma, beta, *[g for g, _ in grads]] + ([target] if use_t else [])
    in_specs = ([_row_spec(tm, d)] * (1 + nf) + [_vec_spec(d), _vec_spec(d)] + [_row_spec(tm, d)] * ng
                + ([_row_spec(tm, d)] if use_t else []))
    out_specs = [_row_spec(tm, d)] * (1 + n_fout) + [_vec_spec(d), _vec_spec(d)] + ([_vec_spec(128)] if use_t else [])
    out_shape = ([jax.ShapeDtypeStruct((t, d), F32)] + [jax.ShapeDtypeStruct((t, d), BF16)] * n_fout
                 + [jax.ShapeDtypeStruct((1, d), F32)] * 2 + ([jax.ShapeDtypeStruct((1, 128), F32)] if use_t else []))
    return pl.pallas_call(
        body, name=name, grid=(t // tm,), in_specs=in_specs, out_specs=out_specs, out_shape=out_shape,
        compiler_params=_cparams(("arbitrary",)),
    )(*ins)


def _conv_fwd(proj, cw, cb, *, name, tm=512):
    t = proj.shape[0]
    c = CONV_CH
    hb = tm // 8

    def body(b_ref, c_ref, h_ref, cp_ref, hp_ref, w_ref, bias_ref, o_ref, q_scr):
        i = pl.program_id(0)
        q = c_ref[...] * h_ref[...]
        halo = jnp.where(i > 0, cp_ref[...] * hp_ref[...], 0.0)
        q_scr[0:8, :] = halo
        q_scr[8:, :] = q
        z = (w_ref[2:3, :] * q + w_ref[1:2, :] * q_scr[pl.ds(7, tm), :] + w_ref[0:1, :] * q_scr[pl.ds(6, tm), :]
             + bias_ref[...])
        o_ref[...] = (b_ref[...] * z).astype(BF16)

    prev = lambda blk: pl.BlockSpec((8, c), lambda i: (jnp.maximum(i * hb - 1, 0), blk))
    return pl.pallas_call(
        body, name=name, grid=(t // tm,),
        in_specs=[_row_spec(tm, c, 0), _row_spec(tm, c, 1), _row_spec(tm, c, 2), prev(1), prev(2),
                  pl.BlockSpec((8, c), lambda i: (0, 0)), _vec_spec(c)],
        out_specs=_row_spec(tm, c),
        out_shape=jax.ShapeDtypeStruct((t, c), BF16),
        scratch_shapes=[pltpu.VMEM((tm + 8, c), F32)],
        compiler_params=_cparams(("parallel",)),
    )(proj, proj, proj, proj, proj, cw, cb)


def _conv_bwd(dyc, proj, cw, cb, *, name, tm=512):
    t = proj.shape[0]
    c = CONV_CH
    hb = tm // 8
    nblk = t // 8

    def body(d_ref, b_ref, c_ref, h_ref, cp_ref, hp_ref, dn_ref, bn_ref, w_ref, bias_ref,
             db_ref, dc_ref, dh_ref, dw_ref, q_scr, z_scr):
        i = pl.program_id(0)
        last = pl.num_programs(0) - 1
        cc = c_ref[...]
        ch = h_ref[...]
        q = cc * ch
        halo = jnp.where(i > 0, cp_ref[...] * hp_ref[...], 0.0)
        q_scr[0:8, :] = halo
        q_scr[8:, :] = q
        w0, w1, w2 = w_ref[0:1, :], w_ref[1:2, :], w_ref[2:3, :]
        qm1 = q_scr[pl.ds(7, tm), :]
        qm2 = q_scr[pl.ds(6, tm), :]
        z = w2 * q + w1 * qm1 + w0 * qm2 + bias_ref[...]
        d = d_ref[...]
        bb = b_ref[...]
        db_ref[...] = (d * z).astype(BF16)
        dz = d * bb
        z_scr[0:tm, :] = dz
        z_scr[tm:, :] = jnp.where(i < last, dn_ref[...] * bn_ref[...], 0.0)
        dq = w2 * dz + w1 * z_scr[pl.ds(1, tm), :] + w0 * z_scr[pl.ds(2, tm), :]
        dc_ref[...] = (dq * ch).astype(BF16)
        dh_ref[...] = (dq * cc).astype(BF16)
        row = lax.broadcasted_iota(jnp.int32, (8, c), 0)
        part = jnp.zeros((8, c), F32)
        for k, term in enumerate((dz * qm2, dz * qm1, dz * q, dz)):
            part = jnp.where(row == k, jnp.sum(term, axis=0, keepdims=True), part)

        @pl.when(i == 0)
        def _():
            dw_ref[...] = part

        @pl.when(i > 0)
        def _():
            dw_ref[...] += part

    prev = lambda blk: pl.BlockSpec((8, c), lambda i: (jnp.maximum(i * hb - 1, 0), blk))
    nxt_p = pl.BlockSpec((8, c), lambda i: (jnp.minimum((i + 1) * hb, nblk - 1), 0))
    nxt_d = pl.BlockSpec((8, c), lambda i: (jnp.minimum((i + 1) * hb, nblk - 1), 0))
    return pl.pallas_call(
        body, name=name, grid=(t // tm,),
        in_specs=[_row_spec(tm, c), _row_spec(tm, c, 0), _row_spec(tm, c, 1), _row_spec(tm, c, 2),
                  prev(1), prev(2), nxt_d, nxt_p, pl.BlockSpec((8, c), lambda i: (0, 0)), _vec_spec(c)],
        out_specs=[_row_spec(tm, c)] * 3 + [pl.BlockSpec((8, c), lambda i: (0, 0))],
        out_shape=[jax.ShapeDtypeStruct((t, c), BF16)] * 3 + [jax.ShapeDtypeStruct((8, c), F32)],
        scratch_shapes=[pltpu.VMEM((tm + 8, c), F32), pltpu.VMEM((tm + 8, c), F32)],
        compiler_params=_cparams(("arbitrary",)),
    )(dyc, proj, proj, proj, proj, proj, dyc, proj, cw, cb)


def _gate_fwd(yconv, glu, proj, *, name, tm=512):
    t, d = yconv.shape

    def body(yc_ref, ga_ref, gb_ref, gc_ref, gs_ref, o_ref):
        yssm = ga_ref[...] * _sigmoid(gb_ref[...])
        o_ref[...] = (_sigmoid(gc_ref[...]) * yc_ref[...] + _sigmoid(gs_ref[...]) * yssm).astype(BF16)

    return pl.pallas_call(
        body, name=name, grid=(t // tm,),
        in_specs=[_row_spec(tm, d), _row_spec(tm, d, 0), _row_spec(tm, d, 1), _row_spec(tm, d, 2), _row_spec(tm, d, 3)],
        out_specs=_row_spec(tm, d), out_shape=jax.ShapeDtypeStruct((t, d), BF16),
        compiler_params=_cparams(("parallel",)),
    )(yconv, glu, glu, proj, proj)


def _gate_bwd(dm, yconv, glu, proj, *, name, tm=512):
    t, d = yconv.shape

    def body(dm_ref, yc_ref, ga_ref, gb_ref, gc_ref, gs_ref, dyc_ref, dglu_ref, dgate_ref):
        dmv = dm_ref[...]
        sc = _sigmoid(gc_ref[...])
        ss = _sigmoid(gs_ref[...])
        sb = _sigmoid(gb_ref[...])
        ga = ga_ref[...]
        yssm = ga * sb
        dyc_ref[...] = (dmv * sc).astype(BF16)
        dgate_ref[:, 0:d] = (dmv * yc_ref[...] * (sc * (1.0 - sc))).astype(BF16)
        dys = dmv * ss
        dgate_ref[:, d:2 * d] = (dmv * yssm * (ss * (1.0 - ss))).astype(BF16)
        dglu_ref[:, 0:d] = (dys * sb).astype(BF16)
        dglu_ref[:, d:2 * d] = (dys * ga * (sb * (1.0 - sb))).astype(BF16)

    return pl.pallas_call(
        body, name=name, grid=(t // tm,),
        in_specs=[_row_spec(tm, d), _row_spec(tm, d), _row_spec(tm, d, 0), _row_spec(tm, d, 1),
                  _row_spec(tm, d, 2), _row_spec(tm, d, 3)],
        out_specs=[_row_spec(tm, d), _row_spec(tm, 2 * d), _row_spec(tm, 2 * d)],
        out_shape=[jax.ShapeDtypeStruct((t, d), BF16), jax.ShapeDtypeStruct((t, 2 * d), BF16),
                   jax.ShapeDtypeStruct((t, 2 * d), BF16)],
        compiler_params=_cparams(("parallel",)),
    )(dm, yconv, glu, glu, proj, proj)


_GELU_C = math.sqrt(2.0 / math.pi)


def _gelu(v):
    return 0.5 * v * (1.0 + jnp.tanh(_GELU_C * (v + 0.044715 * v * v * v)))


def _gelu_grad(v):
    th = jnp.tanh(_GELU_C * (v + 0.044715 * v * v * v))
    return 0.5 * (1.0 + th) + 0.5 * v * (1.0 - th * th) * (_GELU_C * (1.0 + 3.0 * 0.044715 * v * v))


def _cmul(ar, ai, br, bi):
    return ar * br - ai * bi, ar * bi + ai * br


def _scan_fwd(proj, bmat, tab, *, name, plans=()):
    t = proj.shape[0]
    tt, cbw = SCAN_TT, SCAN_CB
    w2 = 2 * cbw

    def body(u_ref, b_ref, tab_ref, s_ref, sb_ref, bu_scr, carry):
        ti = pl.program_id(1)

        @pl.when(ti == 0)
        def _():
            carry[...] = jnp.zeros_like(carry)

        bu_scr[...] = jnp.dot(u_ref[...].astype(BF16), b_ref[...], preferred_element_type=F32)
        row = lax.broadcasted_iota(jnp.int32, (8, cbw), 0)

        def blk(bi, c):
            cr, ci = c
            r0 = pl.multiple_of(bi * 8, 8)
            xr = bu_scr[pl.ds(r0, 8), 0:cbw]
            xi = bu_scr[pl.ds(r0, 8), cbw:w2]
            for k, sh in enumerate((1, 2, 4)):
                kr = tab_ref[k:k + 1, 0:cbw]
                ki = tab_ref[k:k + 1, cbw:w2]
                sr = jnp.where(row >= sh, pltpu.roll(xr, sh, 0), 0.0)
                si = jnp.where(row >= sh, pltpu.roll(xi, sh, 0), 0.0)
                pr, pi = _cmul(kr, ki, sr, si)
                xr = xr + pr
                xi = xi + pi
            pr, pi = _cmul(tab_ref[8:16, 0:cbw], tab_ref[8:16, cbw:w2], cr, ci)
            xr = xr + pr
            xi = xi + pi
            s_ref[pl.ds(r0, 8), 0:cbw] = xr
            s_ref[pl.ds(r0, 8), cbw:w2] = xi
            return (jnp.broadcast_to(xr[7:8, :], (8, cbw)), jnp.broadcast_to(xi[7:8, :], (8, cbw)))

        cr, ci = lax.fori_loop(0, tt // 8, blk, (carry[:, 0:cbw], carry[:, cbw:w2]))
        carry[:, 0:cbw] = cr
        carry[:, cbw:w2] = ci
        sb_ref[...] = s_ref[...].astype(BF16)

    (s, sb), riders = _call_with_plans(
        body, plans, name=name, grid=(SCAN_NCB, t // tt),
        in_specs=[pl.BlockSpec((tt, SCAN_UW), lambda cb, ti: (ti, 3 * SCAN_NCB + cb)),
                  pl.BlockSpec((None, SCAN_UW, w2), lambda cb, ti: (cb, 0, 0)),
                  pl.BlockSpec((16, w2), lambda cb, ti: (0, cb))],
        out_specs=[pl.BlockSpec((tt, w2), lambda cb, ti: (ti, cb))] * 2,
        out_shape=[jax.ShapeDtypeStruct((t, 2 * SSM_CH), F32), jax.ShapeDtypeStruct((t, 2 * SSM_CH), BF16)],
        scratch_shapes=[pltpu.VMEM((tt, w2), F32), pltpu.VMEM((8, w2), F32)],
        semantics=("parallel", "arbitrary"), operands=(proj, bmat, tab))
    return s, sb, riders


def _scan_bwd(dyb, cmat_t, s, tabb, *, name, plans=()):
    t = s.shape[0]
    tt, cbw = SCAN_TT, SCAN_CB
    w2 = 2 * cbw
    nt = t // tt
    hb = tt // 8

    def body(dy_ref, c_ref, s_ref, sp_ref, tab_ref, h_ref, da_ref, g_scr, s_scr, carry):
        ti = pl.program_id(1)

        @pl.when(ti == 0)
        def _():
            carry[...] = jnp.zeros_like(carry)
            da_ref[...] = jnp.zeros_like(da_ref)

        g_scr[...] = jnp.dot(dy_ref[...], c_ref[...], preferred_element_type=F32)
        s_scr[0:8, :] = jnp.where(ti < nt - 1, sp_ref[...], 0.0)
        s_scr[8:, :] = s_ref[...]
        row = lax.broadcasted_iota(jnp.int32, (8, cbw), 0)

        def blk(k, c):
            cr, ci, ar, ai = c
            bi = hb - 1 - k
            r0 = pl.multiple_of(bi * 8, 8)
            xr = g_scr[pl.ds(r0, 8), 0:cbw]
            xi = g_scr[pl.ds(r0, 8), cbw:w2]
            for j, sh in enumerate((1, 2, 4)):
                kr = tab_ref[j:j + 1, 0:cbw]
                ki = tab_ref[j:j + 1, cbw:w2]
                sr = jnp.where(row < 8 - sh, pltpu.roll(xr, 8 - sh, 0), 0.0)
                si = jnp.where(row < 8 - sh, pltpu.roll(xi, 8 - sh, 0), 0.0)
                pr, pi = _cmul(kr, ki, sr, si)
                xr = xr + pr
                xi = xi + pi
            pr, pi = _cmul(tab_ref[8:16, 0:cbw], tab_ref[8:16, cbw:w2], cr, ci)
            xr = xr + pr
            xi = xi + pi
            h_ref[pl.ds(r0, 8), 0:cbw] = xr.astype(BF16)
            h_ref[pl.ds(r0, 8), cbw:w2] = xi.astype(BF16)
            pvr = s_scr[pl.ds(r0, 8), 0:cbw]
            pvi = s_scr[pl.ds(r0, 8), cbw:w2]
            cur_r = s_scr[pl.ds(r0 + 8, 8), 0:cbw]
            cur_i = s_scr[pl.ds(r0 + 8, 8), cbw:w2]
            spr = jnp.where(row == 0, jnp.broadcast_to(pvr[7:8, :], (8, cbw)), pltpu.roll(cur_r, 1, 0))
            spi = jnp.where(row == 0, jnp.broadcast_to(pvi[7:8, :], (8, cbw)), pltpu.roll(cur_i, 1, 0))
            ar = ar + spr * xr + spi * xi
            ai = ai + spr * xi - spi * xr
            return (jnp.broadcast_to(xr[0:1, :], (8, cbw)), jnp.broadcast_to(xi[0:1, :], (8, cbw)), ar, ai)

        z = jnp.zeros((8, cbw), F32)
        cr, ci, ar, ai = lax.fori_loop(0, hb, blk, (carry[:, 0:cbw], carry[:, cbw:w2], z, z))
        carry[:, 0:cbw] = cr
        carry[:, cbw:w2] = ci
        da_ref[:, 0:cbw] += ar
        da_ref[:, cbw:w2] += ai

    rt = lambda ti: nt - 1 - ti
    (h, da), riders = _call_with_plans(
        body, plans, name=name, grid=(SCAN_NCB, nt),
        in_specs=[pl.BlockSpec((tt, SCAN_UW), lambda cb, ti: (rt(ti), cb)),
                  pl.BlockSpec((None, SCAN_UW, w2), lambda cb, ti: (cb, 0, 0)),
                  pl.BlockSpec((tt, w2), lambda cb, ti: (rt(ti), cb)),
                  pl.BlockSpec((8, w2), lambda cb, ti: (jnp.maximum(rt(ti) * hb - 1, 0), cb)),
                  pl.BlockSpec((16, w2), lambda cb, ti: (0, cb))],
        out_specs=[pl.BlockSpec((tt, w2), lambda cb, ti: (rt(ti), cb)),
                   pl.BlockSpec((8, w2), lambda cb, ti: (0, cb))],
        out_shape=[jax.ShapeDtypeStruct((t, 2 * SSM_CH), BF16), jax.ShapeDtypeStruct((8, 2 * SSM_CH), F32)],
        scratch_shapes=[pltpu.VMEM((tt, w2), F32), pltpu.VMEM((tt + 8, w2), F32), pltpu.VMEM((8, w2), F32)],
        semantics=("parallel", "arbitrary"), operands=(dyb, cmat_t, s, s, tabb))
    return h, da, riders


def _s5_out(ymm, proj, dvec, *, name, tm=512):
    t, w = ymm.shape

    def body(y_ref, u_ref, d_ref, yo_ref, sg_ref, ub_ref):
        u = u_ref[...]
        y = y_ref[...] + d_ref[...] * u
        yo_ref[...] = y
        sg_ref[...] = _gelu(y).astype(BF16)
        ub_ref[...] = u.astype(BF16)

    return pl.pallas_call(
        body, name=name, grid=(t // tm,),
        in_specs=[_row_spec(tm, w), _row_spec(tm, w, 3), _vec_spec(w)],
        out_specs=[_row_spec(tm, w)] * 3,
        out_shape=[jax.ShapeDtypeStruct((t, w), F32), jax.ShapeDtypeStruct((t, w), BF16), jax.ShapeDtypeStruct((t, w), BF16)],
        compiler_params=_cparams(("parallel",)),
    )(ymm, proj, dvec)


def _s5_bwd_in(dsg, y, proj, *, name, tm=512):
    t, w = y.shape

    def body(d_ref, y_ref, u_ref, dy_ref, dyb_ref, dd_ref):
        i = pl.program_id(0)
        dy = d_ref[...] * _gelu_grad(y_ref[...])
        dy_ref[...] = dy
        dyb_ref[...] = dy.astype(BF16)
        part = jnp.sum(dy * u_ref[...], axis=0, keepdims=True)

        @pl.when(i == 0)
        def _():
            dd_ref[...] = part

        @pl.when(i > 0)
        def _():
            dd_ref[...] += part

    return pl.pallas_call(
        body, name=name, grid=(t // tm,),
        in_specs=[_row_spec(tm, w), _row_spec(tm, w), _row_spec(tm, w, 3)],
        out_specs=[_row_spec(tm, w), _row_spec(tm, w), _vec_spec(w)],
        out_shape=[jax.ShapeDtypeStruct((t, w), F32), jax.ShapeDtypeStruct((t, w), BF16), jax.ShapeDtypeStruct((1, w), F32)],
        compiler_params=_cparams(("arbitrary",)),
    )(dsg, y, proj)


def _s5_du(dumm, dy, dvec, *, name, tm=512):
    t, w = dy.shape

    def body(a_ref, dy_ref, d_ref, o_ref):
        o_ref[...] = (a_ref[...] + d_ref[...] * dy_ref[...]).astype(BF16)

    return pl.pallas_call(
        body, name=name, grid=(t // tm,), in_specs=[_row_spec(tm, w), _row_spec(tm, w), _vec_spec(w)],
        out_specs=_row_spec(tm, w), out_shape=jax.ShapeDtypeStruct((t, w), BF16),
        compiler_params=_cparams(("parallel",)),
    )(dumm, dy, dvec)


def _s5_discretise(lam_re, lam_im, log_step, b_re, b_im):
    lam = lax.complex(lam_re, lam_im)
    dt = jnp.exp(log_step)[:, None]
    a = jnp.exp(lam * dt)
    bbar = ((a - 1.0) / lam)[..., None] * lax.complex(b_re, b_im)
    return jnp.real(a), jnp.imag(a), jnp.real(bbar), jnp.imag(bbar)


def _perm_cols(re, im):
    lead = re.shape[:-1]
    r = re.reshape(lead + (SCAN_NCB, 1, SCAN_CB))
    i = im.reshape(lead + (SCAN_NCB, 1, SCAN_CB))
    return jnp.concatenate([r, i], axis=-2).reshape(lead + (2 * SSM_CH,))


def _unperm_cols(x):
    lead = x.shape[:-1]
    y = x.reshape(lead + (SCAN_NCB, 2, SCAN_CB))
    return y[..., 0, :].reshape(lead + (SSM_CH,)), y[..., 1, :].reshape(lead + (SSM_CH,))


def _compact(re, im):
    _, r, c = re.shape
    eye = jnp.eye(SCAN_GPB, dtype=re.dtype)

    def half(x):
        x = x.reshape(SCAN_NCB, SCAN_GPB, r, c)
        return (eye[None, :, None, :, None] * x[:, :, :, None, :]).reshape(SCAN_NCB, SCAN_GPB * r, SCAN_GPB * c)

    return jnp.concatenate([half(re), half(im)], axis=-1)


def _compact_extract(x, r):
    c = SSM_STATE
    eye = jnp.eye(SCAN_GPB, dtype=x.dtype)
    y = x.reshape(SCAN_NCB, SCAN_GPB, r, 2, SCAN_GPB, c)
    dg = jnp.sum(y * eye[None, :, None, None, :, None], axis=4).reshape(SSM_GROUPS, r, 2, c)
    return dg[:, :, 0, :], dg[:, :, 1, :]


def _pow_table(ar, ai, descending=False):
    ar = ar.reshape(1, SSM_CH)
    ai = ai.reshape(1, SSM_CH)
    pw = [(ar, ai)]
    for _ in range(7):
        pw.append(_cmul(pw[-1][0], pw[-1][1], ar, ai))
    zero = (jnp.zeros_like(ar), jnp.zeros_like(ar))
    rows = [pw[0], pw[1], pw[3]] + [zero] * 5 + (pw[::-1] if descending else pw)
    re = jnp.concatenate([r for r, _ in rows], axis=0)
    im = jnp.concatenate([i for _, i in rows], axis=0)
    return _perm_cols(re, im)


def _place():
    x, y, c = lax.axis_index("x"), lax.axis_index("y"), lax.axis_index("c")
    chips = [(1 - x, y), (x, 1 - y), (1 - x, 1 - y)]
    return x, y, c, chips


def _dev(px, py, pc):
    return 4 * px + 2 * py + pc


class _Plan:
    def __init__(self, ins, out_shapes, sem_shapes, start, finish, middle=None):
        self.ins, self.out_shapes, self.sem_shapes = list(ins), list(out_shapes), list(sem_shapes)
        self.start, self.finish, self.middle = start, finish, middle


def _split_plan_refs(plans, in_refs, out_refs, sem_refs):
    res, i, o, s = [], 0, 0, 0
    for p in plans:
        ni, no, ns = len(p.ins), len(p.out_shapes), len(p.sem_shapes)
        res.append((in_refs[i:i + ni], out_refs[o:o + no], sem_refs[s:s + ns]))
        i, o, s = i + ni, o + no, s + ns
    return res


def _run_plans(plans, *, name):
    ins = [a for p in plans for a in p.ins]
    outs = [o for p in plans for o in p.out_shapes]
    sems = [s for p in plans for s in p.sem_shapes]
    any_spec = pl.BlockSpec(memory_space=pl.ANY)

    def body(*refs):
        parts = _split_plan_refs(plans, refs[:len(ins)], refs[len(ins):len(ins) + len(outs)], refs[len(ins) + len(outs):])
        for p, r in zip(plans, parts):
            p.start(*r)
        for p, r in zip(plans, parts):
            if p.middle:
                p.middle(*r)
        for p, r in zip(plans, parts):
            p.finish(*r)

    res = pl.pallas_call(body, name=name, in_specs=[any_spec] * len(ins), out_specs=[any_spec] * len(outs),
                         out_shape=outs, scratch_shapes=sems)(*ins)
    return _split_plan_refs(plans, [], res, [])


def _run_plans_on_sequencer(plans, peers_of, *, name, collective_id):
    ins = [a for p in plans for a in p.ins]
    outs = [o for p in plans for o in p.out_shapes]
    sems = [s for p in plans for s in p.sem_shapes]

    def body(*refs):
        x, y, c, chips = _place()
        peers = peers_of(x, y, c, chips)
        barrier = pltpu.get_barrier_semaphore()
        for peer in peers:
            pl.semaphore_signal(barrier, inc=1, device_id=peer, device_id_type=MESH)
        pl.semaphore_wait(barrier, len(peers))
        parts = _split_plan_refs(plans, refs[:len(ins)], refs[len(ins):len(ins) + len(outs)], refs[len(ins) + len(outs):])
        for p, r in zip(plans, parts):
            p.start(*r)
        for p, r in zip(plans, parts):
            if p.middle:
                p.middle(*r)
        for p, r in zip(plans, parts):
            p.finish(*r)

    res = pl.kernel(body, name=name, out_type=outs, mesh=plsc.ScalarSubcoreMesh(axis_name="seq", num_cores=1),
                    scratch_types=sems, compiler_params=pltpu.CompilerParams(collective_id=collective_id))(*ins)
    return _split_plan_refs(plans, [], list(res), [])


def _call_with_plans(body, plans, *, name, grid, in_specs, out_specs, out_shape, scratch_shapes, semantics, operands):
    plans = list(plans)
    if not plans:
        res = pl.pallas_call(body, name=name, grid=grid, in_specs=in_specs, out_specs=out_specs, out_shape=out_shape,
                             scratch_shapes=scratch_shapes, compiler_params=_cparams(semantics))(*operands)
        return list(res), []
    n_in, n_out, n_scr = len(in_specs), len(out_specs), len(scratch_shapes)
    p_ins = [a for p in plans for a in p.ins]
    p_outs = [o for p in plans for o in p.out_shapes]
    p_sems = [s for p in plans for s in p.sem_shapes]
    nsteps = math.prod(grid)
    any_spec = pl.BlockSpec(memory_space=pl.ANY)

    def wrapped(*refs):
        bounds = [n_in, len(p_ins), n_out, len(p_outs), n_scr]
        parts, pos = [], 0
        for b in bounds:
            parts.append(refs[pos:pos + b])
            pos += b
        ins, p_in, outs, p_out, scr = parts
        step = pl.program_id(0)
        for ax in range(1, len(grid)):
            step = step * grid[ax] + pl.program_id(ax)
        riders = _split_plan_refs(plans, p_in, p_out, refs[pos:])

        @pl.when(step == 0)
        def _():
            for p, r in zip(plans, riders):
                p.start(*r)

        mids = [(p, r) for p, r in zip(plans, riders) if p.middle]
        mid_step = nsteps // 2
        split_mid = mids and 0 < mid_step < nsteps - 1
        if split_mid:
            @pl.when(step == mid_step)
            def _():
                for p, r in mids:
                    p.middle(*r)

        body(*ins, *outs, *scr)

        @pl.when(step == nsteps - 1)
        def _():
            if not split_mid:
                for p, r in mids:
                    p.middle(*r)
            for p, r in zip(plans, riders):
                p.finish(*r)

    res = pl.pallas_call(
        wrapped, name=name, grid=grid, in_specs=list(in_specs) + [any_spec] * len(p_ins),
        out_specs=list(out_specs) + [any_spec] * len(p_outs), out_shape=list(out_shape) + p_outs,
        scratch_shapes=list(scratch_shapes) + p_sems, compiler_params=_cparams(("arbitrary",) * len(grid)),
    )(*operands, *p_ins)
    return list(res[:n_out]), [r[1] for r in _split_plan_refs(plans, [], res[n_out:], [])]


def _gather_plan(shards):
    n = len(shards)
    nk = 8

    def make(ins, outs, sems):
        send, recv, lsem = sems
        x, y, c, _ = _place()
        me, sib, xn, yn, dg = (x, y, c), (x, y, 1 - c), (1 - x, y, c), (x, 1 - y, c), (1 - x, 1 - y, c)

        def part(w, block, half):
            ref = outs[w].at[_dev(*block)]
            if half is None:
                return ref
            rows = shards[w].shape[0] // 2
            return ref.at[pl.ds(half * rows, rows)]

        def copy(w, k, block, to, half=None, src=None):
            dst = part(w, block, half)
            return pltpu.make_async_remote_copy(
                src_ref=dst if src is None else src, dst_ref=dst,
                send_sem=send.at[w * nk + k], recv_sem=recv.at[w * nk + k], device_id=to, device_id_type=MESH)

        mine = [pltpu.make_async_copy(ins[w], outs[w].at[_dev(*me)], lsem.at[w]) for w in range(n)]
        return copy, mine, me, sib, xn, yn, dg

    def first_copies(copy, me, sib, xn, yn, ins):
        return [copy(w, k, me, to, src=ins[w]) for w in range(n) for k, to in ((0, sib), (1, xn), (2, yn))]

    def start(ins, outs, sems):
        copy, mine, me, sib, xn, yn, _ = make(ins, outs, sems)
        for cp in mine + first_copies(copy, me, sib, xn, yn, ins):
            cp.start()

    def middle(ins, outs, sems):
        copy, _, me, sib, xn, yn, _ = make(ins, outs, sems)
        for w in range(n):
            copy(w, 1, xn, me).wait_recv()
            copy(w, 3, xn, yn, half=0).start()
            copy(w, 5, xn, sib).start()
        for w in range(n):
            copy(w, 2, yn, me).wait_recv()
            copy(w, 4, yn, xn, half=1).start()
            copy(w, 6, yn, sib).start()

    def finish(ins, outs, sems):
        copy, mine, me, sib, xn, yn, dg = make(ins, outs, sems)
        last = []
        for w in range(n):
            copy(w, 3, dg, me, half=0).wait_recv()
            copy(w, 4, dg, me, half=1).wait_recv()
            fwd = copy(w, 7, dg, sib)
            fwd.start()
            last.append(fwd)
        sx, sy, sd = (1 - me[0], me[1], 1 - me[2]), (me[0], 1 - me[1], 1 - me[2]), (1 - me[0], 1 - me[1], 1 - me[2])
        for w in range(n):
            copy(w, 0, sib, me).wait_recv()
            copy(w, 5, sx, me).wait_recv()
            copy(w, 6, sy, me).wait_recv()
            copy(w, 7, sd, me).wait_recv()
        for cp in first_copies(copy, me, sib, xn, yn, ins) + last:
            cp.wait_send()
        for w in range(n):
            copy(w, 3, xn, yn, half=0).wait_send()
            copy(w, 5, xn, sib).wait_send()
            copy(w, 4, yn, xn, half=1).wait_send()
            copy(w, 6, yn, sib).wait_send()
        for cp in mine:
            cp.wait()

    return _Plan(shards, [jax.ShapeDtypeStruct((N_DEV,) + s.shape, s.dtype) for s in shards],
                 [pltpu.SemaphoreType.DMA((nk * n,)), pltpu.SemaphoreType.DMA((nk * n,)), pltpu.SemaphoreType.DMA((n,))],
                 start, finish, middle)


def _swap_plan(copies_of, n_copies, ins, out_shapes):
    def cps(in_refs, out_refs, sems):
        return copies_of(in_refs, out_refs, sems[0], sems[1])

    def start(in_refs, out_refs, sems):
        for cp in cps(in_refs, out_refs, sems):
            cp.start()

    def finish(in_refs, out_refs, sems):
        all_cps = cps(in_refs, out_refs, sems)
        for cp in all_cps:
            cp.wait_recv()
        for cp in all_cps:
            cp.wait_send()

    return _Plan(ins, out_shapes, [pltpu.SemaphoreType.DMA((n_copies,)), pltpu.SemaphoreType.DMA((n_copies,))],
                 start, finish)


def _sibling_plan(grads):
    n = len(grads)

    def copies(ins, outs, send, recv):
        x, y, c, chips = _place()
        owners = [(x, y)] + chips
        return [pltpu.make_async_remote_copy(
            src_ref=ins[w].at[_dev(*chip, 1 - c)], dst_ref=outs[w].at[k], send_sem=send.at[w * 4 + k],
            recv_sem=recv.at[w * 4 + k], device_id=(x, y, 1 - c), device_id_type=MESH)
            for w in range(n) for k, chip in enumerate(owners)]

    return _swap_plan(copies, 4 * n, grads, [jax.ShapeDtypeStruct((4,) + g.shape[1:], g.dtype) for g in grads])


def _chip_plan(parts, js=(0, 1, 2)):
    n, nj = len(parts), len(js)

    def copies(ins, outs, send, recv):
        x, y, c, chips = _place()
        return [pltpu.make_async_remote_copy(
            src_ref=ins[w].at[j], dst_ref=outs[w * nj + k], send_sem=send.at[w * nj + k],
            recv_sem=recv.at[w * nj + k], device_id=(*chips[j], c), device_id_type=MESH)
            for w in range(n) for k, j in enumerate(js)]

    return _swap_plan(copies, n * nj, parts,
                      [jax.ShapeDtypeStruct(p.shape[1:], p.dtype) for p in parts for _ in js])


UPDATE_TILE_BYTES = 1536 * 1024


def _row_tile(r, c):
    best = 8
    for t in range(8, r + 1, 8):
        if r % t == 0 and t * c * 4 <= UPDATE_TILE_BYTES:
            best = t
    return best


def _chip_partial(g, sib, ids, *, name):
    _, r, c = g.shape
    tr = _row_tile(r, c)

    def body(ids_ref, g_ref, s_ref, o_ref):
        o_ref[...] = (g_ref[...] + s_ref[...]).astype(BF16)

    return pl.pallas_call(
        body, name=name,
        grid_spec=pltpu.PrefetchScalarGridSpec(
            num_scalar_prefetch=1, grid=(3, r // tr),
            in_specs=[pl.BlockSpec((None, tr, c), lambda j, i, ids_ref: (ids_ref[j], i, 0)),
                      pl.BlockSpec((None, tr, c), lambda j, i, ids_ref: (j + 1, i, 0))],
            out_specs=pl.BlockSpec((None, tr, c), lambda j, i, ids_ref: (j, i, 0))),
        out_shape=jax.ShapeDtypeStruct((3, r, c), BF16),
        compiler_params=_cparams(("parallel", "parallel")),
    )(ids, g, sib)


def _adamw_math(w, g, m, v):
    m = ADAM_B1 * m + (1.0 - ADAM_B1) * g
    v = ADAM_B2 * v + (1.0 - ADAM_B2) * (g * g)
    m_hat = m / (1.0 - ADAM_B1 ** ADAM_STEP)
    v_hat = v / (1.0 - ADAM_B2 ** ADAM_STEP)
    delta = -ADAM_LR * (m_hat / (jnp.sqrt(v_hat) + ADAM_EPS) + ADAM_WD * w)
    return delta, m, v


def _shard_update(g, sib, rem, me, w, m, v, *, name):
    r, c = w.shape
    tr = _row_tile(r, c)

    def body(me_ref, g_ref, s_ref, r0_ref, r1_ref, r2_ref, w_ref, m_ref, v_ref, go_ref, d_ref, mo_ref, vo_ref):
        gt = g_ref[...] + s_ref[...]
        gt = gt + r0_ref[...].astype(F32)
        gt = gt + r1_ref[...].astype(F32)
        gt = gt + r2_ref[...].astype(F32)
        go_ref[...] = gt
        d, mn, vn = _adamw_math(w_ref[...], gt, m_ref[...], v_ref[...])
        d_ref[...] = d
        mo_ref[...] = mn
        vo_ref[...] = vn

    blk = lambda k: pl.BlockSpec((None, tr, c), lambda i, me_ref: (k, i, 0))
    plain = pl.BlockSpec((tr, c), lambda i, me_ref: (i, 0))
    return pl.pallas_call(
        body, name=name,
        grid_spec=pltpu.PrefetchScalarGridSpec(
            num_scalar_prefetch=1, grid=(r // tr,),
            in_specs=[pl.BlockSpec((None, tr, c), lambda i, me_ref: (me_ref[0], i, 0)), blk(0), plain, plain, plain,
                      plain, plain, plain],
            out_specs=[plain] * 4),
        out_shape=[jax.ShapeDtypeStruct((r, c), F32)] * 4,
        compiler_params=_cparams(("parallel",)),
    )(me, g, sib, *rem, w, m, v)


def _small_update(gathered, w, m, v, *, name):
    _, r, c = gathered.shape

    def body(g_ref, w_ref, m_ref, v_ref, go_ref, d_ref, mo_ref, vo_ref):
        gt = g_ref[0]
        for k in range(1, N_DEV):
            gt = gt + g_ref[k]
        go_ref[...] = gt
        d, mn, vn = _adamw_math(w_ref[...], gt, m_ref[...], v_ref[...])
        d_ref[...] = d
        mo_ref[...] = mn
        vo_ref[...] = vn

    return pl.pallas_call(
        body, name=name, out_shape=[jax.ShapeDtypeStruct((r, c), F32)] * 4,
        compiler_params=pltpu.CompilerParams(vmem_limit_bytes=VMEM_LIMIT),
    )(gathered, w, m, v)


SMALL_UNIT = 1024


def _pack(parts):
    flat = []
    for p in parts:
        f = p.reshape(-1).astype(F32)
        pad = (-f.shape[0]) % SMALL_UNIT
        flat.append(jnp.pad(f, (0, pad)))
    return jnp.concatenate(flat).reshape(-1, 128)


def _unpack(buf, shapes):
    flat = buf.reshape(-1)
    out, off = [], 0
    for s in shapes:
        nel = math.prod(s)
        out.append(flat[off:off + nel].reshape(s))
        off += nel + ((-nel) % SMALL_UNIT)
    return out


def kernel(x, p, ffn1_w_in, ffn1_w_out, ln1_g, ln1_b, mix_w_in, conv_w, conv_b, conv_w_out, ssm_lam_re, ssm_lam_im, ssm_log_step, ssm_b_re, ssm_b_im, ssm_c_re, ssm_c_im, ssm_d, ssm_w_glu, mix_w_out, ln2_g, ln2_b, ffn2_w_in, ffn2_w_out, ln3_g, ln3_b, ple_w_in, ple_w_gate, ln4_g, ln4_b, loss_target, m_ffn1_w_in, m_ffn1_w_out, m_ln1_g, m_ln1_b, m_mix_w_in, m_conv_w, m_conv_b, m_conv_w_out, m_ssm_lam_re, m_ssm_lam_im, m_ssm_log_step, m_ssm_b_re, m_ssm_b_im, m_ssm_c_re, m_ssm_c_im, m_ssm_d, m_ssm_w_glu, m_mix_w_out, m_ln2_g, m_ln2_b, m_ffn2_w_in, m_ffn2_w_out, m_ln3_g, m_ln3_b, m_ple_w_in, m_ple_w_gate, m_ln4_g, m_ln4_b, v_ffn1_w_in, v_ffn1_w_out, v_ln1_g, v_ln1_b, v_mix_w_in, v_conv_w, v_conv_b, v_conv_w_out, v_ssm_lam_re, v_ssm_lam_im, v_ssm_log_step, v_ssm_b_re, v_ssm_b_im, v_ssm_c_re, v_ssm_c_im, v_ssm_d, v_ssm_w_glu, v_mix_w_out, v_ln2_g, v_ln2_b, v_ffn2_w_in, v_ffn2_w_out, v_ln3_g, v_ln3_b, v_ple_w_in, v_ple_w_gate, v_ln4_g, v_ln4_b):
    args = dict(locals())
    big = ['ffn1_w_in', 'ffn1_w_out', 'mix_w_in', 'conv_w_out', 'ssm_w_glu', 'mix_w_out',
           'ffn2_w_in', 'ffn2_w_out', 'ple_w_in', 'ple_w_gate']
    small = ['ln1_g', 'ln1_b', 'conv_b', 'ssm_lam_re', 'ssm_lam_im', 'ssm_log_step', 'ssm_b_re', 'ssm_b_im',
             'ssm_c_re', 'ssm_c_im', 'ssm_d', 'ln2_g', 'ln2_b', 'ln3_g', 'ln3_b', 'ln4_g', 'ln4_b']
    order = ['ffn1_w_in', 'ffn1_w_out', 'ln1_g', 'ln1_b', 'mix_w_in', 'conv_w', 'conv_b', 'conv_w_out',
             'ssm_lam_re', 'ssm_lam_im', 'ssm_log_step', 'ssm_b_re', 'ssm_b_im', 'ssm_c_re', 'ssm_c_im', 'ssm_d',
             'ssm_w_glu', 'mix_w_out', 'ln2_g', 'ln2_b', 'ffn2_w_in', 'ffn2_w_out', 'ln3_g', 'ln3_b',
             'ple_w_in', 'ple_w_gate', 'ln4_g', 'ln4_b']

    t = x.shape[1]
    d = x.shape[2]
    xc_, yc_, cc_ = lax.axis_index("x"), lax.axis_index("y"), lax.axis_index("c")
    me = (4 * xc_ + 2 * yc_ + cc_).astype(jnp.int32)
    cw_cols = conv_w.shape[2]

    turned = ('ffn1_w_in', 'ffn2_w_in')

    def local(a, nm):
        return jnp.swapaxes(a[0], 0, 1) if nm in turned else a[0]

    shard = {nm: local(args[nm], nm).astype(BF16) for nm in big}
    cw_pad = jnp.zeros((16, 128), F32).at[0:3, 0:cw_cols].set(conv_w[0])
    wf = shard['ffn1_w_in'].shape[0]

    def gather(arrays, tag):
        ((_, got, _),) = _run_plans_on_sequencer(
            [_gather_plan(arrays)], lambda x, y, c, chips: [(x, y, 1 - c), (1 - x, y, c), (x, 1 - y, c)],
            name=f"gather_{tag}", collective_id=2)
        return got

    w1in, cw_g = gather([shard['ffn1_w_in'], cw_pad], "ffn1_in")
    (w1out_g,) = gather([shard['ffn1_w_out']], "ffn1_out")
    (wmix,) = gather([shard['mix_w_in']], "mix_in")
    wco, wglu, wmo_g = gather([shard[nm] for nm in ('conv_w_out', 'ssm_w_glu', 'mix_w_out')], "mix_rest")
    (w2in,) = gather([shard['ffn2_w_in']], "ffn2_in")
    (w2out_g,) = gather([shard['ffn2_w_out']], "ffn2_out")
    wpin, wgate_g = gather([shard['ple_w_in'], shard['ple_w_gate']], "ple")
    cw_full = jnp.transpose(cw_g[:, 0:3, 0:cw_cols], (1, 0, 2)).reshape(3, N_DEV * cw_cols)
    cw8 = jnp.zeros((8, CONV_CH), F32).at[0:3, :].set(cw_full)

    s5_in = (ssm_lam_re[0], ssm_lam_im[0], ssm_log_step[0], ssm_b_re[0], ssm_b_im[0])
    (a_re, a_im, bb_re, bb_im), s5_vjp = jax.vjp(_s5_discretise, *s5_in)
    tab_f = _pow_table(a_re, a_im)
    tab_b = _pow_table(a_re, -a_im, descending=True)
    bmat_b = _compact(jnp.transpose(bb_re, (0, 2, 1)), jnp.transpose(bb_im, (0, 2, 1))).astype(BF16)
    cmat_tb = _compact(ssm_c_re[0], -ssm_c_im[0]).astype(BF16)
    dvec = ssm_d[0].reshape(1, SSM_W)

    xf = x[0]
    x_b = xf.astype(BF16)
    p_b = p[0, 0].astype(BF16)
    tgt = loss_target[0]
    tq = min(512, t)

    ffn_out = dict(ja='c', jb='c', nj=4, tm=tq, tn=d, tk=wf)
    def side_by_side(wb):
        return jnp.transpose(wb, (1, 0, 2)).reshape(wb.shape[1], N_DEV * wb.shape[2])

    tf = min(1024, t)
    a1, h1, _ = _ffn_in(x_b, w1in, name="ffn1_in", tm=tf)
    w1out = w1out_g.reshape(4, wf, d)
    r1, x1, x1b = _mm(a1, w1out, name="ffn1_out", **ffn_out, epilogue=_ln_epilogue(xf, ln1_g, ln1_b, 0.5))
    proj = _mm(x1b, wmix, name="mix_in", jb='b', jo='b', o_flat=True, nj=8, tm=tq, tn=512, tk=d)
    wmo = wmo_g.reshape(d, d)
    ycin = _conv_fwd(proj, cw8, conv_b, name="conv_fwd")
    wco, wglu = side_by_side(wco), side_by_side(wglu)
    yconv = _mm(ycin, wco, name="conv_out", tm=tq, tn=d, tk=CONV_CH)
    s_f, s_b, _ = _scan_fwd(proj, bmat_b, tab_f, name="scan_fwd")
    blk = dict(ja='b', jb='b', jo='b', nj=SCAN_NCB)
    ymm = _mm(s_b, cmat_tb, name="ssm_read", a_flat=True, o_flat=True, tb=True, tm=tq, tn=SCAN_UW, tk=2 * SCAN_CB, **blk)
    ys, sg, u_b = _s5_out(ymm, proj, dvec, name="ssm_out")
    glu = _mm(sg, wglu, name="glu_in", tm=tq, tn=d, tk=SSM_W)
    merged = _gate_fwd(yconv, glu, proj, name="gate_fwd")
    r2, x2, x2b = _mm(merged, wmo, name="mix_out", tm=tq, tn=d, tk=d, epilogue=_ln_epilogue(x1, ln2_g, ln2_b, 1.0))
    a2, h2, _ = _ffn_in(x2b, w2in, name="ffn2_in", tm=tf)
    w2out = w2out_g.reshape(4, wf, d)
    r3, x3, x3b = _mm(a2, w2out, name="ffn2_out", **ffn_out, epilogue=_ln_epilogue(x2, ln3_g, ln3_b, 0.5))
    wgate = wgate_g.reshape(d, d)
    pe = _mm(p_b, side_by_side(wpin), name="ple_in", tm=tq, tn=d, tk=p_b.shape[1])
    gp = _mm(x3b, wgate, name="ple_gate", tm=tq, tn=d, tk=d)

    dr4, dpe_b, dgp_b, dg4, db4, loss_part = _ln_bwd(x3, [pe, gp], ln4_g, ln4_b, [], name="ple_ln_bwd",
                                                     ple=True, target=tgt)
    gb, sib, rem = {}, {}, {}
    ids = jnp.stack([_dev(1 - xc_, yc_, cc_), _dev(xc_, 1 - yc_, cc_), _dev(1 - xc_, 1 - yc_, cc_)]).astype(jnp.int32)

    def blocked(nm, g):
        return g.reshape((N_DEV,) + args[nm].shape[1:])

    def to_sibling(*names):
        return _sibling_plan([gb[nm] for nm in names])

    def chip_sums(names, sibs):
        for nm, s in zip(names, sibs):
            sib[nm] = s
        return [_chip_partial(gb[nm], sib[nm], ids, name=f"chip_sum_{nm}") for nm in names]

    ffn_in_dg = dict(ja='c', jb='c', nj=8, tm=tq, tn=d, tk=wf)
    ffn_in_wg = dict(ja='b', jo='b', ta=True, nj=8, tm=wf, tn=d, tk=t)
    ffn_out_wg = dict(ja='b', jo='b', ta=True, nj=4, tm=wf, tn=d, tk=t)

    gb['ple_w_in'] = _mm(p_b, dpe_b, name="ple_in_wg", jb='b', jo='b', b_flat=True, ta=True, nj=8,
                         tm=p_b.shape[1], tn=128, tk=t)
    gb['ple_w_gate'] = blocked('ple_w_gate', _mm(x3b, dgp_b, name="ple_gate_wg", ta=True, tm=d, tn=d, tk=t))
    g_ple = ['ple_w_in', 'ple_w_gate']
    dx3_g, (s_,) = _mm(dgp_b, wgate, name="ple_gate_dg", tb=True, tm=tq, tn=d, tk=d, plans=[to_sibling(*g_ple)])
    part = chip_sums(g_ple, s_)

    dr3, df2_b, dg3, db3 = _ln_bwd(r3, [], ln3_g, ln3_b, [(dr4, ALPHA), (dx3_g, 1.0)], name="ffn2_ln_bwd", fs=0.5)
    dh2, _ = _ffn_out_dg(df2_b, w2out, h2, name="ffn2_out_dg", tm=tf)
    g_, (r_,) = _mm(a2, df2_b, name="ffn2_out_wg", **ffn_out_wg, plans=[_chip_plan(part)])
    gb['ffn2_w_out'] = blocked('ffn2_w_out', g_)
    rem['ple_w_in'], rem['ple_w_gate'] = r_[0:3], r_[3:6]
    dx2_f, (s_,) = _mm(dh2, w2in, name="ffn2_in_dg", **ffn_in_dg, plans=[to_sibling('ffn2_w_out')])
    part = chip_sums(['ffn2_w_out'], s_)
    gb['ffn2_w_in'], (r_,) = _mm(dh2, x2b, name="ffn2_in_wg", **ffn_in_wg, plans=[_chip_plan(part)])
    rem['ffn2_w_out'] = r_

    dr2, dmix_b, dg2, db2 = _ln_bwd(r2, [], ln2_g, ln2_b, [(dr3, ALPHA), (dx2_f, 1.0)], name="mix_ln_bwd")
    dmerged, (s_,) = _mm(dmix_b, wmo, name="mix_out_dg", tb=True, tm=tq, tn=d, tk=d, plans=[to_sibling('ffn2_w_in')])
    part = chip_sums(['ffn2_w_in'], s_)
    gb['mix_w_out'] = blocked('mix_w_out', _mm(merged, dmix_b, name="mix_out_wg", ta=True, tm=d, tn=d, tk=t))
    dyconv_b, dglu_b, dgate_b = _gate_bwd(dmerged, yconv, glu, proj, name="gate_bwd")
    gb['conv_w_out'] = _mm(ycin, dyconv_b, name="conv_out_wg", jb='b', jo='b', b_flat=True, ta=True, nj=8,
                           tm=CONV_CH, tn=128, tk=t)
    dycin = _mm(dyconv_b, wco, name="conv_out_dg", tb=True, tm=tq, tn=CONV_CH, tk=d)
    gb['ssm_w_glu'] = _mm(sg, dglu_b, name="glu_in_wg", jb='b', jo='b', b_flat=True, ta=True, nj=8,
                          tm=SSM_W, tn=256, tk=t)
    g_mix = ['mix_w_out', 'conv_w_out', 'ssm_w_glu']
    dsg, (s_,) = _mm(dglu_b, wglu, name="glu_in_dg", tb=True, tm=tq, tn=SSM_W, tk=2 * d, plans=[to_sibling(*g_mix)])
    part_mix = chip_sums(g_mix, s_)
    dys, dys_b, dd = _s5_bwd_in(dsg, ys, proj, name="ssm_out_bwd")
    h_b, da_acc, (rem['ffn2_w_in'],) = _scan_bwd(dys_b, cmat_tb, s_f, tab_b, name="scan_bwd", plans=[_chip_plan(part)])
    dumm = _mm(h_b, bmat_b, name="ssm_write_dg", a_flat=True, o_flat=True, tb=True, tm=tq, tn=SCAN_UW,
               tk=2 * SCAN_CB, **blk)
    du_b = _s5_du(dumm, dys, dvec, name="ssm_du")
    g_bmat = _mm(u_b, h_b, name="ssm_write_wg", a_flat=True, b_flat=True, ta=True, tm=SCAN_UW,
                 tn=2 * SCAN_CB, tk=t, **blk)
    g_cmat = _mm(dys_b, s_b, name="ssm_read_wg", a_flat=True, b_flat=True, ta=True, tm=SCAN_UW,
                 tn=2 * SCAN_CB, tk=t, **blk)
    dcb_b, dcc_b, dch_b, dconv = _conv_bwd(dycin, proj, cw8, conv_b, name="conv_bwd")
    dproj = jnp.concatenate([dcb_b, dcc_b, dch_b, du_b, dgate_b], axis=1)
    gb['mix_w_in'], (r_,) = _mm(x1b, dproj, name="mix_in_wg", jb='b', jo='b', b_flat=True, ta=True, nj=8,
                                tm=d, tn=512, tk=t, plans=[_chip_plan(part_mix)])
    for i, nm in enumerate(g_mix):
        rem[nm] = r_[3 * i:3 * i + 3]
    dx1_m, (s_,) = _mm(dproj, wmix, name="mix_in_dg", ja='c', jb='c', a_flat=True, tb=True, nj=8,
                       tm=tq, tn=d, tk=512, plans=[to_sibling('mix_w_in')])
    part = chip_sums(['mix_w_in'], s_)

    dr1, df1_b, dg1, db1 = _ln_bwd(r1, [], ln1_g, ln1_b, [(dr2, ALPHA), (dx1_m, 1.0)], name="ffn1_ln_bwd", fs=0.5)
    dh1, (r01,) = _ffn_out_dg(df1_b, w1out, h1, name="ffn1_out_dg", tm=tf, plans=[_chip_plan(part, js=(0, 1))])
    part_mix_in = part

    da_sum = jnp.sum(da_acc, axis=0)
    da_re, da_im = _unperm_cols(da_sum)
    gbb_re, gbb_im = [jnp.transpose(v, (0, 2, 1)) for v in _compact_extract(g_bmat, SSM_GROUP)]
    g_c_re, g_c_im_neg = _compact_extract(g_cmat, SSM_GROUP)
    g_c_im = -g_c_im_neg
    g_lam_re, g_lam_im, g_log_step, g_b_re, g_b_im = s5_vjp(
        (da_re.reshape(SSM_GROUPS, SSM_STATE), da_im.reshape(SSM_GROUPS, SSM_STATE), gbb_re, gbb_im))
    g_d = dd.reshape(SSM_GROUPS, SSM_GROUP)

    small_g = {'ln1_g': dg1, 'ln1_b': db1, 'conv_b': dconv[3:4], 'ssm_lam_re': g_lam_re, 'ssm_lam_im': g_lam_im,
               'ssm_log_step': g_log_step, 'ssm_b_re': g_b_re, 'ssm_b_im': g_b_im, 'ssm_c_re': g_c_re,
               'ssm_c_im': g_c_im, 'ssm_d': g_d, 'ln2_g': dg2, 'ln2_b': db2, 'ln3_g': dg3, 'ln3_b': db3,
               'ln4_g': dg4, 'ln4_b': db4}
    small_shapes = [args[nm].shape for nm in small] + [(3, CONV_CH), (1,)]
    g_pack = _pack([small_g[nm] for nm in small] + [dconv[0:3], loss_part[0:1, 0:1]])

    (g_all,) = gather([g_pack], "small")
    gb['ffn1_w_in'] = _mm(dh1, x_b, name="ffn1_in_wg", **ffn_in_wg)
    g_, (s_, r2) = _mm(a1, df1_b, name="ffn1_out_wg", **ffn_out_wg,
                       plans=[to_sibling('ffn1_w_in'), _chip_plan(part_mix_in, js=(2,))])
    rem['mix_w_in'] = r01 + r2
    gb['ffn1_w_out'] = blocked('ffn1_w_out', g_)
    part = chip_sums(['ffn1_w_in'], s_)
    (grad_x,), (r_, s_) = _mm(dh1, w1in, name="ffn1_in_dg", **ffn_in_dg,
                              plans=[_chip_plan(part), to_sibling('ffn1_w_out')],
                              epilogue=(lambda pr, drv: (pr + ALPHA * drv,), (dr1,), (), (F32,)))
    rem['ffn1_w_in'] = r_
    part = chip_sums(['ffn1_w_out'], s_)
    ((_, rem['ffn1_w_out'], _),) = _run_plans_on_sequencer(
        [_chip_plan(part)], lambda x, y, c, chips: [(*chip, c) for chip in chips], name="grad_chips_ffn1_out",
        collective_id=1)

    def full_cw(a):
        return lax.dynamic_update_slice(jnp.zeros((3, CONV_CH), F32), a[0], (0, me * cw_cols))

    zero1 = jnp.zeros((1,), F32)
    w_pack = _pack([args[nm] for nm in small] + [full_cw(conv_w), zero1])
    m_pack = _pack([args['m_' + nm] for nm in small] + [full_cw(m_conv_w), zero1])
    v_pack = _pack([args['v_' + nm] for nm in small] + [full_cw(v_conv_w), zero1])
    sg_sum, sd, sm, sv = _small_update(g_all, w_pack, m_pack, v_pack, name="small_update")
    res = {}
    for key, buf in (('grad_', sg_sum), ('delta_', sd), ('new_m_', sm), ('new_v_', sv)):
        parts = _unpack(buf, small_shapes)
        for nm, val in zip(small, parts[:len(small)]):
            res[key + nm] = val
        res[key + 'conv_w'] = lax.dynamic_slice(parts[len(small)], (0, me * cw_cols), (3, cw_cols))[None]
        if key == 'grad_':
            loss = parts[-1][0]

    me1 = me.reshape(1)
    for nm in [w for w in big if w != 'ffn1_w_out'] + ['ffn1_w_out']:
        upd = _shard_update(gb[nm], sib[nm], rem[nm], me1, local(args[nm], nm), local(args['m_' + nm], nm),
                            local(args['v_' + nm], nm), name=f"update_{nm}")
        for key, val in zip(('grad_', 'delta_', 'new_m_', 'new_v_'), upd):
            res[key + nm] = (jnp.swapaxes(val, 0, 1) if nm in turned else val)[None]

    outs = [loss, grad_x[None]]
    for key in ('grad_', 'delta_', 'new_m_', 'new_v_'):
        outs += [res[key + nm] for nm in order]
    return tuple(outs)
```

```python
import functools
import math

import jax
import jax.numpy as jnp
from jax import lax
from jax.experimental import pallas as pl
from jax.experimental.pallas import tpu as pltpu
from jax.experimental.pallas import tpu_sc as plsc

F32 = jnp.float32
BF16 = jnp.bfloat16
MESH = pl.DeviceIdType.MESH

N_DEV = 8
ALPHA = 2.0 ** 0.25
LN_EPS = 1e-5
CONV_CH = 512
SSM_W = 512
SSM_GROUPS = 32
SSM_GROUP = 16
SSM_STATE = 64
SSM_CH = SSM_GROUPS * SSM_STATE
SCAN_CB = 512
SCAN_NCB = SSM_CH // SCAN_CB
SCAN_GPB = SSM_GROUPS // SCAN_NCB
SCAN_UW = SCAN_GPB * SSM_GROUP
SCAN_TT = 256
ADAM_LR = 0.001
ADAM_B1 = 0.9
ADAM_B2 = 0.999
ADAM_EPS = 1e-08
ADAM_WD = 0.01
ADAM_STEP = 10
VMEM_LIMIT = 56 * 1024 * 1024


def _cparams(sem=None, **kw):
    return pltpu.CompilerParams(dimension_semantics=sem, vmem_limit_bytes=VMEM_LIMIT, **kw)


def _mm(a, b, *, name, ja=None, jb=None, jo=None, a_flat=False, b_flat=False, o_flat=False,
        ta=False, tb=False, tm, tn, tk, nj=1, out_dtype=F32, plans=(), epilogue=None):
    def dims(arr, j, flat):
        if j is None:
            return arr.shape
        if flat:
            return (arr.shape[0], arr.shape[1] // nj)
        assert arr.shape[0] == nj, (name, arr.shape, nj)
        return arr.shape[1:]

    ar, ac = dims(a, ja, a_flat)
    br, bc = dims(b, jb, b_flat)
    m, k = (ac, ar) if ta else (ar, ac)
    k2, n = (bc, br) if tb else (br, bc)
    assert k == k2, (name, a.shape, b.shape)
    assert m % tm == 0 and n % tn == 0 and k % tk == 0, (name, m, n, k, tm, tn, tk)
    njb = nj if 'b' in (ja, jb) else 1
    njc = nj if 'c' in (ja, jb) else 1
    nk = k // tk
    j_inside = njc > 1 and nk == 1 and not ta
    n_in = njc if j_inside else 1
    nred = nk if j_inside else njc * nk
    grid = (njb, m // tm, n // tn, 1 if j_inside else njc, nk)

    def make_spec(j, flat, blk, rfn, cfn, cols_per_j):
        def jsel(g, c):
            return g if j == 'b' else c
        if j is None:
            return pl.BlockSpec(blk, lambda g, i, jn, c, kk: (rfn(i, jn, kk), cfn(i, jn, kk)))
        if j == 'c' and j_inside:
            if flat:
                return pl.BlockSpec((blk[0], nj * blk[1]), lambda g, i, jn, c, kk: (rfn(i, jn, kk), 0))
            return pl.BlockSpec((nj,) + blk, lambda g, i, jn, c, kk: (0, rfn(i, jn, kk), cfn(i, jn, kk)))
        if flat:
            nb = cols_per_j // blk[1]
            return pl.BlockSpec(blk, lambda g, i, jn, c, kk: (rfn(i, jn, kk), jsel(g, c) * nb + cfn(i, jn, kk)))
        return pl.BlockSpec((None,) + blk,
                            lambda g, i, jn, c, kk: (jsel(g, c), rfn(i, jn, kk), cfn(i, jn, kk)))

    if ta:
        a_spec = make_spec(ja, a_flat, (tk, tm), lambda i, jn, kk: kk, lambda i, jn, kk: i, ac)
    else:
        a_spec = make_spec(ja, a_flat, (tm, tk), lambda i, jn, kk: i, lambda i, jn, kk: kk, ac)
    if tb:
        b_spec = make_spec(jb, b_flat, (tn, tk), lambda i, jn, kk: jn, lambda i, jn, kk: kk, bc)
    else:
        b_spec = make_spec(jb, b_flat, (tk, tn), lambda i, jn, kk: kk, lambda i, jn, kk: jn, bc)
    o_spec = make_spec(jo, o_flat, (tm, tn), lambda i, jn, kk: i, lambda i, jn, kk: jn, n)
    if jo is None:
        out_shape = (m, n)
    elif o_flat:
        out_shape = (m, nj * n)
    else:
        out_shape = (nj, m, n)

    dn = (((0 if ta else 1,), (1 if tb else 0,)), ((), ()))

    def operand(ref, j, flat, jj, width):
        if not (j == 'c' and j_inside):
            return ref[...]
        return ref[:, jj * width:(jj + 1) * width] if flat else ref[jj]

    e_fn, e_rows, e_vecs, e_dtypes = epilogue if epilogue else (None, (), (), (out_dtype,))
    assert not epilogue or (nred == 1 and jo is None), name
    n_e = len(e_rows) + len(e_vecs)
    n_o = len(e_dtypes)

    def body(a_ref, b_ref, *rest):
        e_refs, o_refs, scratch = rest[:n_e], rest[n_e:n_e + n_o], rest[n_e + n_o:]
        o_ref = o_refs[0]
        p = None
        for jj in range(n_in):
            q = lax.dot_general(operand(a_ref, ja, a_flat, jj, tk), operand(b_ref, jb, b_flat, jj, tk if tb else tn),
                                dn, preferred_element_type=F32)
            p = q if p is None else p + q
        if epilogue:
            for ref, val in zip(o_refs, e_fn(p, *[r[...] for r in e_refs])):
                ref[...] = val.astype(ref.dtype)
        elif nred == 1:
            o_ref[...] = p.astype(o_ref.dtype)
        else:
            acc = scratch[0]
            r = pl.program_id(3) * nk + pl.program_id(4)

            @pl.when(r == 0)
            def _():
                acc[...] = p

            @pl.when(r > 0)
            def _():
                acc[...] += p

            @pl.when(r == nred - 1)
            def _():
                o_ref[...] = acc[...].astype(o_ref.dtype)

    vec_spec = pl.BlockSpec((1, tn), lambda g, i, jn, c, kk: (0, jn))
    res = _call_with_plans(
        body, plans, name=name, grid=grid,
        in_specs=[a_spec, b_spec] + [o_spec] * len(e_rows) + [vec_spec] * len(e_vecs), out_specs=[o_spec] * n_o,
        out_shape=[jax.ShapeDtypeStruct(out_shape, dt) for dt in e_dtypes],
        scratch_shapes=[] if nred == 1 else [pltpu.VMEM((tm, tn), F32)],
        semantics=("parallel", "parallel", "parallel", "arbitrary", "arbitrary"), operands=(a, b, *e_rows, *e_vecs))
    outs = res[0] if epilogue else res[0][0]
    return (outs, res[1]) if plans else outs


def _sigmoid(v):
    return jax.nn.sigmoid(v)


def _row_spec(tm, cols, colblk=0):
    return pl.BlockSpec((tm, cols), lambda i: (i, colblk))


def _vec_spec(cols):
    return pl.BlockSpec((1, cols), lambda i: (0, 0))


def _ffn_in(xb, win, *, name, tm, plans=()):
    t, d = xb.shape
    nj, w, _ = win.shape
    half = nj // 2
    dn = (((1,), (1,)), ((), ()))

    def body(x_ref, wg_ref, wu_ref, a_ref, gu_ref):
        xv = x_ref[...]
        g = lax.dot_general(xv, wg_ref[...], dn, preferred_element_type=F32)
        u = lax.dot_general(xv, wu_ref[...], dn, preferred_element_type=F32)
        a_ref[...] = (g * _sigmoid(g) * u).astype(BF16)
        gu_ref[0] = g.astype(BF16)
        gu_ref[1] = u.astype(BF16)

    (a, gu), riders = _call_with_plans(
        body, plans, name=name, grid=(half, t // tm),
        in_specs=[pl.BlockSpec((tm, d), lambda j, i: (i, 0)),
                  pl.BlockSpec((None, w, d), lambda j, i: (j, 0, 0)),
                  pl.BlockSpec((None, w, d), lambda j, i: (j + half, 0, 0))],
        out_specs=[pl.BlockSpec((None, tm, w), lambda j, i: (j, i, 0)),
                   pl.BlockSpec((2, None, tm, w), lambda j, i: (0, j, i, 0))],
        out_shape=[jax.ShapeDtypeStruct((half, t, w), BF16), jax.ShapeDtypeStruct((2, half, t, w), BF16)],
        scratch_shapes=[], semantics=("parallel", "parallel"), operands=(xb, win, win))
    return a, gu, riders


def _ffn_out_dg(dfb, wout, gu, *, name, tm, plans=()):
    t, d = dfb.shape
    half, w, _ = wout.shape
    dn = (((1,), (1,)), ((), ()))

    def body(df_ref, w_ref, gu_ref, dh_ref):
        da = lax.dot_general(df_ref[...], w_ref[...], dn, preferred_element_type=F32)
        g = gu_ref[0].astype(F32)
        u = gu_ref[1].astype(F32)
        sg = _sigmoid(g)
        dh_ref[0] = (da * u * (sg * (1.0 + g * (1.0 - sg)))).astype(BF16)
        dh_ref[1] = (da * (g * sg)).astype(BF16)

    (out,), riders = _call_with_plans(
        body, plans, name=name, grid=(half, t // tm),
        in_specs=[pl.BlockSpec((tm, d), lambda j, i: (i, 0)),
                  pl.BlockSpec((None, w, d), lambda j, i: (j, 0, 0)),
                  pl.BlockSpec((2, None, tm, w), lambda j, i: (0, j, i, 0))],
        out_specs=[pl.BlockSpec((2, None, tm, w), lambda j, i: (0, j, i, 0))],
        out_shape=[jax.ShapeDtypeStruct((2, half, t, w), BF16)],
        scratch_shapes=[], semantics=("parallel", "parallel"), operands=(dfb, wout, gu))
    return out.reshape(2 * half, t, w), riders


def _ln_stats(r):
    mu = jnp.mean(r, axis=-1, keepdims=True)
    xc = r - mu
    var = jnp.mean(xc * xc, axis=-1, keepdims=True)
    rstd = lax.rsqrt(var + LN_EPS)
    return xc * rstd, rstd


def _ln_epilogue(xin, gamma, beta, fs):
    def fn(p, xv, g, b):
        r = ALPHA * xv + fs * p
        xh, _ = _ln_stats(r)
        y = xh * g + b
        return r, y, y

    return fn, (xin,), (gamma, beta), (F32, F32, BF16)


def _ln_bwd(xin, fparts, gamma, beta, grads, *, name, fs=1.0, ple=False, target=None, tm=512):
    t, d = xin.shape
    nf = len(fparts)
    ng = len(grads)
    coefs = [c for _, c in grads]
    use_t = target is not None
    n_fout = 2 if ple else 1

    def body(*refs):
        pos = 0
        x_ref = refs[pos]; pos += 1
        f_refs = refs[pos:pos + nf]; pos += nf
        g_ref, b_ref = refs[pos:pos + 2]; pos += 2
        gr_refs = refs[pos:pos + ng]; pos += ng
        if use_t:
            t_ref = refs[pos]; pos += 1
        dr_ref = refs[pos]; pos += 1
        fo_refs = refs[pos:pos + n_fout]; pos += n_fout
        dg_ref, db_ref = refs[pos:pos + 2]; pos += 2
        if use_t:
            loss_ref = refs[pos]; pos += 1
        i = pl.program_id(0)

        if ple:
            pe = f_refs[0][...]
            sg = _sigmoid(f_refs[1][...])
            resid = ALPHA * x_ref[...] + pe * sg
        else:
            resid = x_ref[...]
        xh, rstd = _ln_stats(resid)
        gam = g_ref[...]
        if use_t:
            diff = xh * gam + b_ref[...] - t_ref[...]
            dy = diff * (1.0 / d)
            lpart = 0.5 * jnp.sum(jnp.sum(diff * diff, axis=-1, keepdims=True), axis=0, keepdims=True) * (1.0 / d)
        else:
            dy = coefs[0] * gr_refs[0][...]
            for c, r in zip(coefs[1:], gr_refs[1:]):
                dy = dy + c * r[...]
        dxh = dy * gam
        m1 = jnp.mean(dxh, axis=-1, keepdims=True)
        m2 = jnp.mean(dxh * xh, axis=-1, keepdims=True)
        dr = rstd * (dxh - m1 - xh * m2)
        dr_ref[...] = dr
        if ple:
            fo_refs[0][...] = (dr * sg).astype(BF16)
            fo_refs[1][...] = (dr * pe * (sg * (1.0 - sg))).astype(BF16)
        else:
            fo_refs[0][...] = (fs * dr).astype(BF16)
        dgp = jnp.sum(dy * xh, axis=0, keepdims=True)
        dbp = jnp.sum(dy, axis=0, keepdims=True)

        @pl.when(i == 0)
        def _():
            dg_ref[...] = dgp
            db_ref[...] = dbp
            if use_t:
                loss_ref[...] = jnp.broadcast_to(lpart, loss_ref.shape)

        @pl.when(i > 0)
        def _():
            dg_ref[...] += dgp
            db_ref[...] += dbp
            if use_t:
                loss_ref[...] += jnp.broadcast_to(lpart, loss_ref.shape)

    ins = [xin, *fparts, gamma, beta, *[g for g, _ in grads]] + ([target] if use_t else [])
    in_specs = ([_row_spec(tm, d)] * (1 + nf) + [_vec_spec(d), _vec_spec(d)] + [_row_spec(tm, d)] * ng
                + ([_row_spec(tm, d)] if use_t else []))
    out_specs = [_row_spec(tm, d)] * (1 + n_fout) + [_vec_spec(d), _vec_spec(d)] + ([_vec_spec(128)] if use_t else [])
    out_shape = ([jax.ShapeDtypeStruct((t, d), F32)] + [jax.ShapeDtypeStruct((t, d), BF16)] * n_fout
                 + [jax.ShapeDtypeStruct((1, d), F32)] * 2 + ([jax.ShapeDtypeStruct((1, 128), F32)] if use_t else []))
    return pl.pallas_call(
        body, name=name, grid=(t // tm,), in_specs=in_specs, out_specs=out_specs, out_shape=out_shape,
        compiler_params=_cparams(("arbitrary",)),
    )(*ins)


def _conv_fwd(proj, cw, cb, *, name, tm=512):
    t = proj.shape[0]
    c = CONV_CH
    hb = tm // 8

    def body(b_ref, c_ref, h_ref, cp_ref, hp_ref, w_ref, bias_ref, o_ref, q_scr):
        i = pl.program_id(0)
        q = c_ref[...] * h_ref[...]
        halo = jnp.where(i > 0, cp_ref[...] * hp_ref[...], 0.0)
        q_scr[0:8, :] = halo
        q_scr[8:, :] = q
        z = (w_ref[2:3, :] * q + w_ref[1:2, :] * q_scr[pl.ds(7, tm), :] + w_ref[0:1, :] * q_scr[pl.ds(6, tm), :]
             + bias_ref[...])
        o_ref[...] = (b_ref[...] * z).astype(BF16)

    prev = lambda blk: pl.BlockSpec((8, c), lambda i: (jnp.maximum(i * hb - 1, 0), blk))
    return pl.pallas_call(
        body, name=name, grid=(t // tm,),
        in_specs=[_row_spec(tm, c, 0), _row_spec(tm, c, 1), _row_spec(tm, c, 2), prev(1), prev(2),
                  pl.BlockSpec((8, c), lambda i: (0, 0)), _vec_spec(c)],
        out_specs=_row_spec(tm, c),
        out_shape=jax.ShapeDtypeStruct((t, c), BF16),
        scratch_shapes=[pltpu.VMEM((tm + 8, c), F32)],
        compiler_params=_cparams(("parallel",)),
    )(proj, proj, proj, proj, proj, cw, cb)


def _conv_bwd(dyc, proj, cw, cb, *, name, tm=512):
    t = proj.shape[0]
    c = CONV_CH
    hb = tm // 8
    nblk = t // 8

    def body(d_ref, b_ref, c_ref, h_ref, cp_ref, hp_ref, dn_ref, bn_ref, w_ref, bias_ref,
             db_ref, dc_ref, dh_ref, dw_ref, q_scr, z_scr):
        i = pl.program_id(0)
        last = pl.num_programs(0) - 1
        cc = c_ref[...]
        ch = h_ref[...]
        q = cc * ch
        halo = jnp.where(i > 0, cp_ref[...] * hp_ref[...], 0.0)
        q_scr[0:8, :] = halo
        q_scr[8:, :] = q
        w0, w1, w2 = w_ref[0:1, :], w_ref[1:2, :], w_ref[2:3, :]
        qm1 = q_scr[pl.ds(7, tm), :]
        qm2 = q_scr[pl.ds(6, tm), :]
        z = w2 * q + w1 * qm1 + w0 * qm2 + bias_ref[...]
        d = d_ref[...]
        bb = b_ref[...]
        db_ref[...] = (d * z).astype(BF16)
        dz = d * bb
        z_scr[0:tm, :] = dz
        z_scr[tm:, :] = jnp.where(i < last, dn_ref[...] * bn_ref[...], 0.0)
        dq = w2 * dz + w1 * z_scr[pl.ds(1, tm), :] + w0 * z_scr[pl.ds(2, tm), :]
        dc_ref[...] = (dq * ch).astype(BF16)
        dh_ref[...] = (dq * cc).astype(BF16)
        row = lax.broadcasted_iota(jnp.int32, (8, c), 0)
        part = jnp.zeros((8, c), F32)
        for k, term in enumerate((dz * qm2, dz * qm1, dz * q, dz)):
            part = jnp.where(row == k, jnp.sum(term, axis=0, keepdims=True), part)

        @pl.when(i == 0)
        def _():
            dw_ref[...] = part

        @pl.when(i > 0)
        def _():
            dw_ref[...] += part

    prev = lambda blk: pl.BlockSpec((8, c), lambda i: (jnp.maximum(i * hb - 1, 0), blk))
    nxt_p = pl.BlockSpec((8, c), lambda i: (jnp.minimum((i + 1) * hb, nblk - 1), 0))
    nxt_d = pl.BlockSpec((8, c), lambda i: (jnp.minimum((i + 1) * hb, nblk - 1), 0))
    return pl.pallas_call(
        body, name=name, grid=(t // tm,),
        in_specs=[_row_spec(tm, c), _row_spec(tm, c, 0), _row_spec(tm, c, 1), _row_spec(tm, c, 2),
                  prev(1), prev(2), nxt_d, nxt_p, pl.BlockSpec((8, c), lambda i: (0, 0)), _vec_spec(c)],
        out_specs=[_row_spec(tm, c)] * 3 + [pl.BlockSpec((8, c), lambda i: (0, 0))],
        out_shape=[jax.ShapeDtypeStruct((t, c), BF16)] * 3 + [jax.ShapeDtypeStruct((8, c), F32)],
        scratch_shapes=[pltpu.VMEM((tm + 8, c), F32), pltpu.VMEM((tm + 8, c), F32)],
        compiler_params=_cparams(("arbitrary",)),
    )(dyc, proj, proj, proj, proj, proj, dyc, proj, cw, cb)


def _gate_fwd(yconv, glu, proj, *, name, tm=512):
    t, d = yconv.shape

    def body(yc_ref, ga_ref, gb_ref, gc_ref, gs_ref, o_ref):
        yssm = ga_ref[...] * _sigmoid(gb_ref[...])
        o_ref[...] = (_sigmoid(gc_ref[...]) * yc_ref[...] + _sigmoid(gs_ref[...]) * yssm).astype(BF16)

    return pl.pallas_call(
        body, name=name, grid=(t // tm,),
        in_specs=[_row_spec(tm, d), _row_spec(tm, d, 0), _row_spec(tm, d, 1), _row_spec(tm, d, 2), _row_spec(tm, d, 3)],
        out_specs=_row_spec(tm, d), out_shape=jax.ShapeDtypeStruct((t, d), BF16),
        compiler_params=_cparams(("parallel",)),
    )(yconv, glu, glu, proj, proj)


def _gate_bwd(dm, yconv, glu, proj, *, name, tm=512):
    t, d = yconv.shape

    def body(dm_ref, yc_ref, ga_ref, gb_ref, gc_ref, gs_ref, dyc_ref, dglu_ref, dgate_ref):
        dmv = dm_ref[...]
        sc = _sigmoid(gc_ref[...])
        ss = _sigmoid(gs_ref[...])
        sb = _sigmoid(gb_ref[...])
        ga = ga_ref[...]
        yssm = ga * sb
        dyc_ref[...] = (dmv * sc).astype(BF16)
        dgate_ref[:, 0:d] = (dmv * yc_ref[...] * (sc * (1.0 - sc))).astype(BF16)
        dys = dmv * ss
        dgate_ref[:, d:2 * d] = (dmv * yssm * (ss * (1.0 - ss))).astype(BF16)
        dglu_ref[:, 0:d] = (dys * sb).astype(BF16)
        dglu_ref[:, d:2 * d] = (dys * ga * (sb * (1.0 - sb))).astype(BF16)

    return pl.pallas_call(
        body, name=name, grid=(t // tm,),
        in_specs=[_row_spec(tm, d), _row_spec(tm, d), _row_spec(tm, d, 0), _row_spec(tm, d, 1),
                  _row_spec(tm, d, 2), _row_spec(tm, d, 3)],
        out_specs=[_row_spec(tm, d), _row_spec(tm, 2 * d), _row_spec(tm, 2 * d)],
        out_shape=[jax.ShapeDtypeStruct((t, d), BF16), jax.ShapeDtypeStruct((t, 2 * d), BF16),
                   jax.ShapeDtypeStruct((t, 2 * d), BF16)],
        compiler_params=_cparams(("parallel",)),
    )(dm, yconv, glu, glu, proj, proj)


_GELU_C = math.sqrt(2.0 / math.pi)


def _gelu(v):
    return 0.5 * v * (1.0 + jnp.tanh(_GELU_C * (v + 0.044715 * v * v * v)))


def _gelu_grad(v):
    th = jnp.tanh(_GELU_C * (v + 0.044715 * v * v * v))
    return 0.5 * (1.0 + th) + 0.5 * v * (1.0 - th * th) * (_GELU_C * (1.0 + 3.0 * 0.044715 * v * v))


def _cmul(ar, ai, br, bi):
    return ar * br - ai * bi, ar * bi + ai * br


def _scan_fwd(proj, bmat, tab, *, name, plans=()):
    t = proj.shape[0]
    tt, cbw = SCAN_TT, SCAN_CB
    w2 = 2 * cbw

    def body(u_ref, b_ref, tab_ref, s_ref, sb_ref, bu_scr, carry):
        ti = pl.program_id(1)

        @pl.when(ti == 0)
        def _():
            carry[...] = jnp.zeros_like(carry)

        bu_scr[...] = jnp.dot(u_ref[...].astype(BF16), b_ref[...], preferred_element_type=F32)
        row = lax.broadcasted_iota(jnp.int32, (8, cbw), 0)

        def blk(bi, c):
            cr, ci = c
            r0 = pl.multiple_of(bi * 8, 8)
            xr = bu_scr[pl.ds(r0, 8), 0:cbw]
            xi = bu_scr[pl.ds(r0, 8), cbw:w2]
            for k, sh in enumerate((1, 2, 4)):
                kr = tab_ref[k:k + 1, 0:cbw]
                ki = tab_ref[k:k + 1, cbw:w2]
                sr = jnp.where(row >= sh, pltpu.roll(xr, sh, 0), 0.0)
                si = jnp.where(row >= sh, pltpu.roll(xi, sh, 0), 0.0)
                pr, pi = _cmul(kr, ki, sr, si)
                xr = xr + pr
                xi = xi + pi
            pr, pi = _cmul(tab_ref[8:16, 0:cbw], tab_ref[8:16, cbw:w2], cr, ci)
            xr = xr + pr
            xi = xi + pi
            s_ref[pl.ds(r0, 8), 0:cbw] = xr
            s_ref[pl.ds(r0, 8), cbw:w2] = xi
            return (jnp.broadcast_to(xr[7:8, :], (8, cbw)), jnp.broadcast_to(xi[7:8, :], (8, cbw)))

        cr, ci = lax.fori_loop(0, tt // 8, blk, (carry[:, 0:cbw], carry[:, cbw:w2]))
        carry[:, 0:cbw] = cr
        carry[:, cbw:w2] = ci
        sb_ref[...] = s_ref[...].astype(BF16)

    (s, sb), riders = _call_with_plans(
        body, plans, name=name, grid=(SCAN_NCB, t // tt),
        in_specs=[pl.BlockSpec((tt, SCAN_UW), lambda cb, ti: (ti, 3 * SCAN_NCB + cb)),
                  pl.BlockSpec((None, SCAN_UW, w2), lambda cb, ti: (cb, 0, 0)),
                  pl.BlockSpec((16, w2), lambda cb, ti: (0, cb))],
        out_specs=[pl.BlockSpec((tt, w2), lambda cb, ti: (ti, cb))] * 2,
        out_shape=[jax.ShapeDtypeStruct((t, 2 * SSM_CH), F32), jax.ShapeDtypeStruct((t, 2 * SSM_CH), BF16)],
        scratch_shapes=[pltpu.VMEM((tt, w2), F32), pltpu.VMEM((8, w2), F32)],
        semantics=("parallel", "arbitrary"), operands=(proj, bmat, tab))
    return s, sb, riders


def _scan_bwd(dyb, cmat_t, s, tabb, *, name, plans=()):
    t = s.shape[0]
    tt, cbw = SCAN_TT, SCAN_CB
    w2 = 2 * cbw
    nt = t // tt
    hb = tt // 8

    def body(dy_ref, c_ref, s_ref, sp_ref, tab_ref, h_ref, da_ref, g_scr, s_scr, carry):
        ti = pl.program_id(1)

        @pl.when(ti == 0)
        def _():
            carry[...] = jnp.zeros_like(carry)
            da_ref[...] = jnp.zeros_like(da_ref)

        g_scr[...] = jnp.dot(dy_ref[...], c_ref[...], preferred_element_type=F32)
        s_scr[0:8, :] = jnp.where(ti < nt - 1, sp_ref[...], 0.0)
        s_scr[8:, :] = s_ref[...]
        row = lax.broadcasted_iota(jnp.int32, (8, cbw), 0)

        def blk(k, c):
            cr, ci, ar, ai = c
            bi = hb - 1 - k
            r0 = pl.multiple_of(bi * 8, 8)
            xr = g_scr[pl.ds(r0, 8), 0:cbw]
            xi = g_scr[pl.ds(r0, 8), cbw:w2]
            for j, sh in enumerate((1, 2, 4)):
                kr = tab_ref[j:j + 1, 0:cbw]
                ki = tab_ref[j:j + 1, cbw:w2]
                sr = jnp.where(row < 8 - sh, pltpu.roll(xr, 8 - sh, 0), 0.0)
                si = jnp.where(row < 8 - sh, pltpu.roll(xi, 8 - sh, 0), 0.0)
                pr, pi = _cmul(kr, ki, sr, si)
                xr = xr + pr
                xi = xi + pi
            pr, pi = _cmul(tab_ref[8:16, 0:cbw], tab_ref[8:16, cbw:w2], cr, ci)
            xr = xr + pr
            xi = xi + pi
            h_ref[pl.ds(r0, 8), 0:cbw] = xr.astype(BF16)
            h_ref[pl.ds(r0, 8), cbw:w2] = xi.astype(BF16)
            pvr = s_scr[pl.ds(r0, 8), 0:cbw]
            pvi = s_scr[pl.ds(r0, 8), cbw:w2]
            cur_r = s_scr[pl.ds(r0 + 8, 8), 0:cbw]
            cur_i = s_scr[pl.ds(r0 + 8, 8), cbw:w2]
            spr = jnp.where(row == 0, jnp.broadcast_to(pvr[7:8, :], (8, cbw)), pltpu.roll(cur_r, 1, 0))
            spi = jnp.where(row == 0, jnp.broadcast_to(pvi[7:8, :], (8, cbw)), pltpu.roll(cur_i, 1, 0))
            ar = ar + spr * xr + spi * xi
            ai = ai + spr * xi - spi * xr
            return (jnp.broadcast_to(xr[0:1, :], (8, cbw)), jnp.broadcast_to(xi[0:1, :], (8, cbw)), ar, ai)

        z = jnp.zeros((8, cbw), F32)
        cr, ci, ar, ai = lax.fori_loop(0, hb, blk, (carry[:, 0:cbw], carry[:, cbw:w2], z, z))
        carry[:, 0:cbw] = cr
        carry[:, cbw:w2] = ci
        da_ref[:, 0:cbw] += ar
        da_ref[:, cbw:w2] += ai

    rt = lambda ti: nt - 1 - ti
    (h, da), riders = _call_with_plans(
        body, plans, name=name, grid=(SCAN_NCB, nt),
        in_specs=[pl.BlockSpec((tt, SCAN_UW), lambda cb, ti: (rt(ti), cb)),
                  pl.BlockSpec((None, SCAN_UW, w2), lambda cb, ti: (cb, 0, 0)),
                  pl.BlockSpec((tt, w2), lambda cb, ti: (rt(ti), cb)),
                  pl.BlockSpec((8, w2), lambda cb, ti: (jnp.maximum(rt(ti) * hb - 1, 0), cb)),
                  pl.BlockSpec((16, w2), lambda cb, ti: (0, cb))],
        out_specs=[pl.BlockSpec((tt, w2), lambda cb, ti: (rt(ti), cb)),
                   pl.BlockSpec((8, w2), lambda cb, ti: (0, cb))],
        out_shape=[jax.ShapeDtypeStruct((t, 2 * SSM_CH), BF16), jax.ShapeDtypeStruct((8, 2 * SSM_CH), F32)],
        scratch_shapes=[pltpu.VMEM((tt, w2), F32), pltpu.VMEM((tt + 8, w2), F32), pltpu.VMEM((8, w2), F32)],
        semantics=("parallel", "arbitrary"), operands=(dyb, cmat_t, s, s, tabb))
    return h, da, riders


def _s5_out(ymm, proj, dvec, *, name, tm=512):
    t, w = ymm.shape

    def body(y_ref, u_ref, d_ref, yo_ref, sg_ref, ub_ref):
        u = u_ref[...]
        y = y_ref[...] + d_ref[...] * u
        yo_ref[...] = y
        sg_ref[...] = _gelu(y).astype(BF16)
        ub_ref[...] = u.astype(BF16)

    return pl.pallas_call(
        body, name=name, grid=(t // tm,),
        in_specs=[_row_spec(tm, w), _row_spec(tm, w, 3), _vec_spec(w)],
        out_specs=[_row_spec(tm, w)] * 3,
        out_shape=[jax.ShapeDtypeStruct((t, w), F32), jax.ShapeDtypeStruct((t, w), BF16), jax.ShapeDtypeStruct((t, w), BF16)],
        compiler_params=_cparams(("parallel",)),
    )(ymm, proj, dvec)


def _s5_bwd_in(dsg, y, proj, *, name, tm=512):
    t, w = y.shape

    def body(d_ref, y_ref, u_ref, dy_ref, dyb_ref, dd_ref):
        i = pl.program_id(0)
        dy = d_ref[...] * _gelu_grad(y_ref[...])
        dy_ref[...] = dy
        dyb_ref[...] = dy.astype(BF16)
        part = jnp.sum(dy * u_ref[...], axis=0, keepdims=True)

        @pl.when(i == 0)
        def _():
            dd_ref[...] = part

        @pl.when(i > 0)
        def _():
            dd_ref[...] += part

    return pl.pallas_call(
        body, name=name, grid=(t // tm,),
        in_specs=[_row_spec(tm, w), _row_spec(tm, w), _row_spec(tm, w, 3)],
        out_specs=[_row_spec(tm, w), _row_spec(tm, w), _vec_spec(w)],
        out_shape=[jax.ShapeDtypeStruct((t, w), F32), jax.ShapeDtypeStruct((t, w), BF16), jax.ShapeDtypeStruct((1, w), F32)],
        compiler_params=_cparams(("arbitrary",)),
    )(dsg, y, proj)


def _s5_du(dumm, dy, dvec, *, name, tm=512):
    t, w = dy.shape

    def body(a_ref, dy_ref, d_ref, o_ref):
        o_ref[...] = (a_ref[...] + d_ref[...] * dy_ref[...]).astype(BF16)

    return pl.pallas_call(
        body, name=name, grid=(t // tm,), in_specs=[_row_spec(tm, w), _row_spec(tm, w), _vec_spec(w)],
        out_specs=_row_spec(tm, w), out_shape=jax.ShapeDtypeStruct((t, w), BF16),
        compiler_params=_cparams(("parallel",)),
    )(dumm, dy, dvec)


def _s5_discretise(lam_re, lam_im, log_step, b_re, b_im):
    lam = lax.complex(lam_re, lam_im)
    dt = jnp.exp(log_step)[:, None]
    a = jnp.exp(lam * dt)
    bbar = ((a - 1.0) / lam)[..., None] * lax.complex(b_re, b_im)
    return jnp.real(a), jnp.imag(a), jnp.real(bbar), jnp.imag(bbar)


def _perm_cols(re, im):
    lead = re.shape[:-1]
    r = re.reshape(lead + (SCAN_NCB, 1, SCAN_CB))
    i = im.reshape(lead + (SCAN_NCB, 1, SCAN_CB))
    return jnp.concatenate([r, i], axis=-2).reshape(lead + (2 * SSM_CH,))


def _unperm_cols(x):
    lead = x.shape[:-1]
    y = x.reshape(lead + (SCAN_NCB, 2, SCAN_CB))
    return y[..., 0, :].reshape(lead + (SSM_CH,)), y[..., 1, :].reshape(lead + (SSM_CH,))


def _compact(re, im):
    _, r, c = re.shape
    eye = jnp.eye(SCAN_GPB, dtype=re.dtype)

    def half(x):
        x = x.reshape(SCAN_NCB, SCAN_GPB, r, c)
        return (eye[None, :, None, :, None] * x[:, :, :, None, :]).reshape(SCAN_NCB, SCAN_GPB * r, SCAN_GPB * c)

    return jnp.concatenate([half(re), half(im)], axis=-1)


def _compact_extract(x, r):
    c = SSM_STATE
    eye = jnp.eye(SCAN_GPB, dtype=x.dtype)
    y = x.reshape(SCAN_NCB, SCAN_GPB, r, 2, SCAN_GPB, c)
    dg = jnp.sum(y * eye[None, :, None, None, :, None], axis=4).reshape(SSM_GROUPS, r, 2, c)
    return dg[:, :, 0, :], dg[:, :, 1, :]


def _pow_table(ar, ai, descending=False):
    ar = ar.reshape(1, SSM_CH)
    ai = ai.reshape(1, SSM_CH)
    pw = [(ar, ai)]
    for _ in range(7):
        pw.append(_cmul(pw[-1][0], pw[-1][1], ar, ai))
    zero = (jnp.zeros_like(ar), jnp.zeros_like(ar))
    rows = [pw[0], pw[1], pw[3]] + [zero] * 5 + (pw[::-1] if descending else pw)
    re = jnp.concatenate([r for r, _ in rows], axis=0)
    im = jnp.concatenate([i for _, i in rows], axis=0)
    return _perm_cols(re, im)


def _place():
    x, y, c = lax.axis_index("x"), lax.axis_index("y"), lax.axis_index("c")
    chips = [(1 - x, y), (x, 1 - y), (1 - x, 1 - y)]
    return x, y, c, chips


def _dev(px, py, pc):
    return 4 * px + 2 * py + pc


class _Plan:
    def __init__(self, ins, out_shapes, sem_shapes, start, finish, middle=None):
        self.ins, self.out_shapes, self.sem_shapes = list(ins), list(out_shapes), list(sem_shapes)
        self.start, self.finish, self.middle = start, finish, middle


def _split_plan_refs(plans, in_refs, out_refs, sem_refs):
    res, i, o, s = [], 0, 0, 0
    for p in plans:
        ni, no, ns = len(p.ins), len(p.out_shapes), len(p.sem_shapes)
        res.append((in_refs[i:i + ni], out_refs[o:o + no], sem_refs[s:s + ns]))
        i, o, s = i + ni, o + no, s + ns
    return res


def _run_plans(plans, *, name):
    ins = [a for p in plans for a in p.ins]
    outs = [o for p in plans for o in p.out_shapes]
    sems = [s for p in plans for s in p.sem_shapes]
    any_spec = pl.BlockSpec(memory_space=pl.ANY)

    def body(*refs):
        parts = _split_plan_refs(plans, refs[:len(ins)], refs[len(ins):len(ins) + len(outs)], refs[len(ins) + len(outs):])
        for p, r in zip(plans, parts):
            p.start(*r)
        for p, r in zip(plans, parts):
            if p.middle:
                p.middle(*r)
        for p, r in zip(plans, parts):
            p.finish(*r)

    res = pl.pallas_call(body, name=name, in_specs=[any_spec] * len(ins), out_specs=[any_spec] * len(outs),
                         out_shape=outs, scratch_shapes=sems)(*ins)
    return _split_plan_refs(plans, [], res, [])


def _run_plans_on_sequencer(plans, peers_of, *, name, collective_id):
    ins = [a for p in plans for a in p.ins]
    outs = [o for p in plans for o in p.out_shapes]
    sems = [s for p in plans for s in p.sem_shapes]

    def body(*refs):
        x, y, c, chips = _place()
        peers = peers_of(x, y, c, chips)
        barrier = pltpu.get_barrier_semaphore()
        for peer in peers:
            pl.semaphore_signal(barrier, inc=1, device_id=peer, device_id_type=MESH)
        pl.semaphore_wait(barrier, len(peers))
        parts = _split_plan_refs(plans, refs[:len(ins)], refs[len(ins):len(ins) + len(outs)], refs[len(ins) + len(outs):])
        for p, r in zip(plans, parts):
            p.start(*r)
        for p, r in zip(plans, parts):
            if p.middle:
                p.middle(*r)
        for p, r in zip(plans, parts):
            p.finish(*r)

    res = pl.kernel(body, name=name, out_type=outs, mesh=plsc.ScalarSubcoreMesh(axis_name="seq", num_cores=1),
                    scratch_types=sems, compiler_params=pltpu.CompilerParams(collective_id=collective_id))(*ins)
    return _split_plan_refs(plans, [], list(res), [])


def _call_with_plans(body, plans, *, name, grid, in_specs, out_specs, out_shape, scratch_shapes, semantics, operands):
    plans = list(plans)
    if not plans:
        res = pl.pallas_call(body, name=name, grid=grid, in_specs=in_specs, out_specs=out_specs, out_shape=out_shape,
                             scratch_shapes=scratch_shapes, compiler_params=_cparams(semantics))(*operands)
        return list(res), []
    n_in, n_out, n_scr = len(in_specs), len(out_specs), len(scratch_shapes)
    p_ins = [a for p in plans for a in p.ins]
    p_outs = [o for p in plans for o in p.out_shapes]
    p_sems = [s for p in plans for s in p.sem_shapes]
    nsteps = math.prod(grid)
    any_spec = pl.BlockSpec(memory_space=pl.ANY)

    def wrapped(*refs):
        bounds = [n_in, len(p_ins), n_out, len(p_outs), n_scr]
        parts, pos = [], 0
        for b in bounds:
            parts.append(refs[pos:pos + b])
            pos += b
        ins, p_in, outs, p_out, scr = parts
        step = pl.program_id(0)
        for ax in range(1, len(grid)):
            step = step * grid[ax] + pl.program_id(ax)
        riders = _split_plan_refs(plans, p_in, p_out, refs[pos:])

        @pl.when(step == 0)
        def _():
            for p, r in zip(plans, riders):
                p.start(*r)

        mids = [(p, r) for p, r in zip(plans, riders) if p.middle]
        mid_step = nsteps // 2
        split_mid = mids and 0 < mid_step < nsteps - 1
        if split_mid:
            @pl.when(step == mid_step)
            def _():
                for p, r in mids:
                    p.middle(*r)

        body(*ins, *outs, *scr)

        @pl.when(step == nsteps - 1)
        def _():
            if not split_mid:
                for p, r in mids:
                    p.middle(*r)
            for p, r in zip(plans, riders):
                p.finish(*r)

    res = pl.pallas_call(
        wrapped, name=name, grid=grid, in_specs=list(in_specs) + [any_spec] * len(p_ins),
        out_specs=list(out_specs) + [any_spec] * len(p_outs), out_shape=list(out_shape) + p_outs,
        scratch_shapes=list(scratch_shapes) + p_sems, compiler_params=_cparams(("arbitrary",) * len(grid)),
    )(*operands, *p_ins)
    return list(res[:n_out]), [r[1] for r in _split_plan_refs(plans, [], res[n_out:], [])]


def _gather_plan(shards):
    n = len(shards)
    nk = 8

    def make(ins, outs, sems):
        send, recv, lsem = sems
        x, y, c, _ = _place()
        me, sib, xn, yn, dg = (x, y, c), (x, y, 1 - c), (1 - x, y, c), (x, 1 - y, c), (1 - x, 1 - y, c)

        def part(w, block, half):
            ref = outs[w].at[_dev(*block)]
            if half is None:
                return ref
            rows = shards[w].shape[0] // 2
            return ref.at[pl.ds(half * rows, rows)]

        def copy(w, k, block, to, half=None, src=None):
            dst = part(w, block, half)
            return pltpu.make_async_remote_copy(
                src_ref=dst if src is None else src, dst_ref=dst,
                send_sem=send.at[w * nk + k], recv_sem=recv.at[w * nk + k], device_id=to, device_id_type=MESH)

        mine = [pltpu.make_async_copy(ins[w], outs[w].at[_dev(*me)], lsem.at[w]) for w in range(n)]
        return copy, mine, me, sib, xn, yn, dg

    def first_copies(copy, me, sib, xn, yn, ins):
        return [copy(w, k, me, to, src=ins[w]) for w in range(n) for k, to in ((0, sib), (1, xn), (2, yn))]

    def start(ins, outs, sems):
        copy, mine, me, sib, xn, yn, _ = make(ins, outs, sems)
        for cp in mine + first_copies(copy, me, sib, xn, yn, ins):
            cp.start()

    def middle(ins, outs, sems):
        copy, _, me, sib, xn, yn, _ = make(ins, outs, sems)
        for w in range(n):
            copy(w, 1, xn, me).wait_recv()
            copy(w, 3, xn, yn, half=0).start()
            copy(w, 5, xn, sib).start()
        for w in range(n):
            copy(w, 2, yn, me).wait_recv()
            copy(w, 4, yn, xn, half=1).start()
            copy(w, 6, yn, sib).start()

    def finish(ins, outs, sems):
        copy, mine, me, sib, xn, yn, dg = make(ins, outs, sems)
        last = []
        for w in range(n):
            copy(w, 3, dg, me, half=0).wait_recv()
            copy(w, 4, dg, me, half=1).wait_recv()
            fwd = copy(w, 7, dg, sib)
            fwd.start()
            last.append(fwd)
        sx, sy, sd = (1 - me[0], me[1], 1 - me[2]), (me[0], 1 - me[1], 1 - me[2]), (1 - me[0], 1 - me[1], 1 - me[2])
        for w in range(n):
            copy(w, 0, sib, me).wait_recv()
            copy(w, 5, sx, me).wait_recv()
            copy(w, 6, sy, me).wait_recv()
            copy(w, 7, sd, me).wait_recv()
        for cp in first_copies(copy, me, sib, xn, yn, ins) + last:
            cp.wait_send()
        for w in range(n):
            copy(w, 3, xn, yn, half=0).wait_send()
            copy(w, 5, xn, sib).wait_send()
            copy(w, 4, yn, xn, half=1).wait_send()
            copy(w, 6, yn, sib).wait_send()
        for cp in mine:
            cp.wait()

    return _Plan(shards, [jax.ShapeDtypeStruct((N_DEV,) + s.shape, s.dtype) for s in shards],
                 [pltpu.SemaphoreType.DMA((nk * n,)), pltpu.SemaphoreType.DMA((nk * n,)), pltpu.SemaphoreType.DMA((n,))],
                 start, finish, middle)


def _swap_plan(copies_of, n_copies, ins, out_shapes):
    def cps(in_refs, out_refs, sems):
        return copies_of(in_refs, out_refs, sems[0], sems[1])

    def start(in_refs, out_refs, sems):
        for cp in cps(in_refs, out_refs, sems):
            cp.start()

    def finish(in_refs, out_refs, sems):
        all_cps = cps(in_refs, out_refs, sems)
        for cp in all_cps:
            cp.wait_recv()
        for cp in all_cps:
            cp.wait_send()

    return _Plan(ins, out_shapes, [pltpu.SemaphoreType.DMA((n_copies,)), pltpu.SemaphoreType.DMA((n_copies,))],
                 start, finish)


def _sibling_plan(grads):
    n = len(grads)

    def copies(ins, outs, send, recv):
        x, y, c, chips = _place()
        owners = [(x, y)] + chips
        return [pltpu.make_async_remote_copy(
            src_ref=ins[w].at[_dev(*chip, 1 - c)], dst_ref=outs[w].at[k], send_sem=send.at[w * 4 + k],
            recv_sem=recv.at[w * 4 + k], device_id=(x, y, 1 - c), device_id_type=MESH)
            for w in range(n) for k, chip in enumerate(owners)]

    return _swap_plan(copies, 4 * n, grads, [jax.ShapeDtypeStruct((4,) + g.shape[1:], g.dtype) for g in grads])


def _chip_plan(parts, js=(0, 1, 2)):
    n, nj = len(parts), len(js)

    def copies(ins, outs, send, recv):
        x, y, c, chips = _place()
        return [pltpu.make_async_remote_copy(
            src_ref=ins[w].at[j], dst_ref=outs[w * nj + k], send_sem=send.at[w * nj + k],
            recv_sem=recv.at[w * nj + k], device_id=(*chips[j], c), device_id_type=MESH)
            for w in range(n) for k, j in enumerate(js)]

    return _swap_plan(copies, n * nj, parts,
                      [jax.ShapeDtypeStruct(p.shape[1:], p.dtype) for p in parts for _ in js])


UPDATE_TILE_BYTES = 1536 * 1024


def _row_tile(r, c):
    best = 8
    for t in range(8, r + 1, 8):
        if r % t == 0 and t * c * 4 <= UPDATE_TILE_BYTES:
            best = t
    return best


def _chip_partial(g, sib, ids, *, name):
    _, r, c = g.shape
    tr = _row_tile(r, c)

    def body(ids_ref, g_ref, s_ref, o_ref):
        o_ref[...] = (g_ref[...] + s_ref[...]).astype(BF16)

    return pl.pallas_call(
        body, name=name,
        grid_spec=pltpu.PrefetchScalarGridSpec(
            num_scalar_prefetch=1, grid=(3, r // tr),
            in_specs=[pl.BlockSpec((None, tr, c), lambda j, i, ids_ref: (ids_ref[j], i, 0)),
                      pl.BlockSpec((None, tr, c), lambda j, i, ids_ref: (j + 1, i, 0))],
            out_specs=pl.BlockSpec((None, tr, c), lambda j, i, ids_ref: (j, i, 0))),
        out_shape=jax.ShapeDtypeStruct((3, r, c), BF16),
        compiler_params=_cparams(("parallel", "parallel")),
    )(ids, g, sib)


def _adamw_math(w, g, m, v):
    m = ADAM_B1 * m + (1.0 - ADAM_B1) * g
    v = ADAM_B2 * v + (1.0 - ADAM_B2) * (g * g)
    m_hat = m / (1.0 - ADAM_B1 ** ADAM_STEP)
    v_hat = v / (1.0 - ADAM_B2 ** ADAM_STEP)
    delta = -ADAM_LR * (m_hat / (jnp.sqrt(v_hat) + ADAM_EPS) + ADAM_WD * w)
    return delta, m, v


def _shard_update(g, sib, rem, me, w, m, v, *, name):
    r, c = w.shape
    tr = _row_tile(r, c)

    def body(me_ref, g_ref, s_ref, r0_ref, r1_ref, r2_ref, w_ref, m_ref, v_ref, go_ref, d_ref, mo_ref, vo_ref):
        gt = g_ref[...] + s_ref[...]
        gt = gt + r0_ref[...].astype(F32)
        gt = gt + r1_ref[...].astype(F32)
        gt = gt + r2_ref[...].astype(F32)
        go_ref[...] = gt
        d, mn, vn = _adamw_math(w_ref[...], gt, m_ref[...], v_ref[...])
        d_ref[...] = d
        mo_ref[...] = mn
        vo_ref[...] = vn

    blk = lambda k: pl.BlockSpec((None, tr, c), lambda i, me_ref: (k, i, 0))
    plain = pl.BlockSpec((tr, c), lambda i, me_ref: (i, 0))
    return pl.pallas_call(
        body, name=name,
        grid_spec=pltpu.PrefetchScalarGridSpec(
            num_scalar_prefetch=1, grid=(r // tr,),
            in_specs=[pl.BlockSpec((None, tr, c), lambda i, me_ref: (me_ref[0], i, 0)), blk(0), plain, plain, plain,
                      plain, plain, plain],
            out_specs=[plain] * 4),
        out_shape=[jax.ShapeDtypeStruct((r, c), F32)] * 4,
        compiler_params=_cparams(("parallel",)),
    )(me, g, sib, *rem, w, m, v)


def _small_update(gathered, w, m, v, *, name):
    _, r, c = gathered.shape

    def body(g_ref, w_ref, m_ref, v_ref, go_ref, d_ref, mo_ref, vo_ref):
        gt = g_ref[0]
        for k in range(1, N_DEV):
            gt = gt + g_ref[k]
        go_ref[...] = gt
        d, mn, vn = _adamw_math(w_ref[...], gt, m_ref[...], v_ref[...])
        d_ref[...] = d
        mo_ref[...] = mn
        vo_ref[...] = vn

    return pl.pallas_call(
        body, name=name, out_shape=[jax.ShapeDtypeStruct((r, c), F32)] * 4,
        compiler_params=pltpu.CompilerParams(vmem_limit_bytes=VMEM_LIMIT),
    )(gathered, w, m, v)


SMALL_UNIT = 1024


def _pack(parts):
    flat = []
    for p in parts:
        f = p.reshape(-1).astype(F32)
        pad = (-f.shape[0]) % SMALL_UNIT
        flat.append(jnp.pad(f, (0, pad)))
    return jnp.concatenate(flat).reshape(-1, 128)


def _unpack(buf, shapes):
    flat = buf.reshape(-1)
    out, off = [], 0
    for s in shapes:
        nel = math.prod(s)
        out.append(flat[off:off + nel].reshape(s))
        off += nel + ((-nel) % SMALL_UNIT)
    return out


def kernel(x, p, ffn1_w_in, ffn1_w_out, ln1_g, ln1_b, mix_w_in, conv_w, conv_b, conv_w_out, ssm_lam_re, ssm_lam_im, ssm_log_step, ssm_b_re, ssm_b_im, ssm_c_re, ssm_c_im, ssm_d, ssm_w_glu, mix_w_out, ln2_g, ln2_b, ffn2_w_in, ffn2_w_out, ln3_g, ln3_b, ple_w_in, ple_w_gate, ln4_g, ln4_b, loss_target, m_ffn1_w_in, m_ffn1_w_out, m_ln1_g, m_ln1_b, m_mix_w_in, m_conv_w, m_conv_b, m_conv_w_out, m_ssm_lam_re, m_ssm_lam_im, m_ssm_log_step, m_ssm_b_re, m_ssm_b_im, m_ssm_c_re, m_ssm_c_im, m_ssm_d, m_ssm_w_glu, m_mix_w_out, m_ln2_g, m_ln2_b, m_ffn2_w_in, m_ffn2_w_out, m_ln3_g, m_ln3_b, m_ple_w_in, m_ple_w_gate, m_ln4_g, m_ln4_b, v_ffn1_w_in, v_ffn1_w_out, v_ln1_g, v_ln1_b, v_mix_w_in, v_conv_w, v_conv_b, v_conv_w_out, v_ssm_lam_re, v_ssm_lam_im, v_ssm_log_step, v_ssm_b_re, v_ssm_b_im, v_ssm_c_re, v_ssm_c_im, v_ssm_d, v_ssm_w_glu, v_mix_w_out, v_ln2_g, v_ln2_b, v_ffn2_w_in, v_ffn2_w_out, v_ln3_g, v_ln3_b, v_ple_w_in, v_ple_w_gate, v_ln4_g, v_ln4_b):
    args = dict(locals())
    big = ['ffn1_w_in', 'ffn1_w_out', 'mix_w_in', 'conv_w_out', 'ssm_w_glu', 'mix_w_out',
           'ffn2_w_in', 'ffn2_w_out', 'ple_w_in', 'ple_w_gate']
    small = ['ln1_g', 'ln1_b', 'conv_b', 'ssm_lam_re', 'ssm_lam_im', 'ssm_log_step', 'ssm_b_re', 'ssm_b_im',
             'ssm_c_re', 'ssm_c_im', 'ssm_d', 'ln2_g', 'ln2_b', 'ln3_g', 'ln3_b', 'ln4_g', 'ln4_b']
    order = ['ffn1_w_in', 'ffn1_w_out', 'ln1_g', 'ln1_b', 'mix_w_in', 'conv_w', 'conv_b', 'conv_w_out',
             'ssm_lam_re', 'ssm_lam_im', 'ssm_log_step', 'ssm_b_re', 'ssm_b_im', 'ssm_c_re', 'ssm_c_im', 'ssm_d',
             'ssm_w_glu', 'mix_w_out', 'ln2_g', 'ln2_b', 'ffn2_w_in', 'ffn2_w_out', 'ln3_g', 'ln3_b',
             'ple_w_in', 'ple_w_gate', 'ln4_g', 'ln4_b']

    t = x.shape[1]
    d = x.shape[2]
    xc_, yc_, cc_ = lax.axis_index("x"), lax.axis_index("y"), lax.axis_index("c")
    me = (4 * xc_ + 2 * yc_ + cc_).astype(jnp.int32)
    cw_cols = conv_w.shape[2]

    turned = ('ffn1_w_in', 'ffn2_w_in')

    def local(a, nm):
        return jnp.swapaxes(a[0], 0, 1) if nm in turned else a[0]

    shard = {nm: local(args[nm], nm).astype(BF16) for nm in big}
    cw_pad = jnp.zeros((16, 128), F32).at[0:3, 0:cw_cols].set(conv_w[0])
    wf = shard['ffn1_w_in'].shape[0]

    def gather(arrays, tag):
        ((_, got, _),) = _run_plans_on_sequencer(
            [_gather_plan(arrays)], lambda x, y, c, chips: [(x, y, 1 - c), (1 - x, y, c), (x, 1 - y, c)],
            name=f"gather_{tag}", collective_id=2)
        return got

    w1in, cw_g = gather([shard['ffn1_w_in'], cw_pad], "ffn1_in")
    (w1out_g,) = gather([shard['ffn1_w_out']], "ffn1_out")
    (wmix,) = gather([shard['mix_w_in']], "mix_in")
    wco, wglu, wmo_g = gather([shard[nm] for nm in ('conv_w_out', 'ssm_w_glu', 'mix_w_out')], "mix_rest")
    (w2in,) = gather([shard['ffn2_w_in']], "ffn2_in")
    (w2out_g,) = gather([shard['ffn2_w_out']], "ffn2_out")
    wpin, wgate_g = gather([shard['ple_w_in'], shard['ple_w_gate']], "ple")
    cw_full = jnp.transpose(cw_g[:, 0:3, 0:cw_cols], (1, 0, 2)).reshape(3, N_DEV * cw_cols)
    cw8 = jnp.zeros((8, CONV_CH), F32).at[0:3, :].set(cw_full)

    s5_in = (ssm_lam_re[0], ssm_lam_im[0], ssm_log_step[0], ssm_b_re[0], ssm_b_im[0])
    (a_re, a_im, bb_re, bb_im), s5_vjp = jax.vjp(_s5_discretise, *s5_in)
    tab_f = _pow_table(a_re, a_im)
    tab_b = _pow_table(a_re, -a_im, descending=True)
    bmat_b = _compact(jnp.transpose(bb_re, (0, 2, 1)), jnp.transpose(bb_im, (0, 2, 1))).astype(BF16)
    cmat_tb = _compact(ssm_c_re[0], -ssm_c_im[0]).astype(BF16)
    dvec = ssm_d[0].reshape(1, SSM_W)

    xf = x[0]
    x_b = xf.astype(BF16)
    p_b = p[0, 0].astype(BF16)
    tgt = loss_target[0]
    tq = min(512, t)

    ffn_out = dict(ja='c', jb='c', nj=4, tm=tq, tn=d, tk=wf)
    def side_by_side(wb):
        return jnp.transpose(wb, (1, 0, 2)).reshape(wb.shape[1], N_DEV * wb.shape[2])

    tf = min(1024, t)
    a1, h1, _ = _ffn_in(x_b, w1in, name="ffn1_in", tm=tf)
    w1out = w1out_g.reshape(4, wf, d)
    r1, x1, x1b = _mm(a1, w1out, name="ffn1_out", **ffn_out, epilogue=_ln_epilogue(xf, ln1_g, ln1_b, 0.5))
    proj = _mm(x1b, wmix, name="mix_in", jb='b', jo='b', o_flat=True, nj=8, tm=tq, tn=512, tk=d)
    wmo = wmo_g.reshape(d, d)
    ycin = _conv_fwd(proj, cw8, conv_b, name="conv_fwd")
    wco, wglu = side_by_side(wco), side_by_side(wglu)
    yconv = _mm(ycin, wco, name="conv_out", tm=tq, tn=d, tk=CONV_CH)
    s_f, s_b, _ = _scan_fwd(proj, bmat_b, tab_f, name="scan_fwd")
    blk = dict(ja='b', jb='b', jo='b', nj=SCAN_NCB)
    ymm = _mm(s_b, cmat_tb, name="ssm_read", a_flat=True, o_flat=True, tb=True, tm=tq, tn=SCAN_UW, tk=2 * SCAN_CB, **blk)
    ys, sg, u_b = _s5_out(ymm, proj, dvec, name="ssm_out")
    glu = _mm(sg, wglu, name="glu_in", tm=tq, tn=d, tk=SSM_W)
    merged = _gate_fwd(yconv, glu, proj, name="gate_fwd")
    r2, x2, x2b = _mm(merged, wmo, name="mix_out", tm=tq, tn=d, tk=d, epilogue=_ln_epilogue(x1, ln2_g, ln2_b, 1.0))
    a2, h2, _ = _ffn_in(x2b, w2in, name="ffn2_in", tm=tf)
    w2out = w2out_g.reshape(4, wf, d)
    r3, x3, x3b = _mm(a2, w2out, name="ffn2_out", **ffn_out, epilogue=_ln_epilogue(x2, ln3_g, ln3_b, 0.5))
    wgate = wgate_g.reshape(d, d)
    pe = _mm(p_b, side_by_side(wpin), name="ple_in", tm=tq, tn=d, tk=p_b.shape[1])
    gp = _mm(x3b, wgate, name="ple_gate", tm=tq, tn=d, tk=d)

    dr4, dpe_b, dgp_b, dg4, db4, loss_part = _ln_bwd(x3, [pe, gp], ln4_g, ln4_b, [], name="ple_ln_bwd",
                                                     ple=True, target=tgt)
    gb, sib, rem = {}, {}, {}
    ids = jnp.stack([_dev(1 - xc_, yc_, cc_), _dev(xc_, 1 - yc_, cc_), _dev(1 - xc_, 1 - yc_, cc_)]).astype(jnp.int32)

    def blocked(nm, g):
        return g.reshape((N_DEV,) + args[nm].shape[1:])

    def to_sibling(*names):
        return _sibling_plan([gb[nm] for nm in names])

    def chip_sums(names, sibs):
        for nm, s in zip(names, sibs):
            sib[nm] = s
        return [_chip_partial(gb[nm], sib[nm], ids, name=f"chip_sum_{nm}") for nm in names]

    def on_sequencer(plan, peers, tag, cid):
        ((_, got, _),) = _run_plans_on_sequencer([plan], peers, name=tag, collective_id=cid)
        return got

    waiting = []

    def send_sibling(*names):
        got = on_sequencer(to_sibling(*names), lambda x, y, c, chips: [(x, y, 1 - c)], f"grad_sibling_{names[0]}", 3)
        waiting.append((names, got))

    def send_chips():
        while waiting:
            names, got = waiting.pop(0)
            r_ = on_sequencer(_chip_plan(chip_sums(names, got)), lambda x, y, c, chips: [(*chip, c) for chip in chips],
                              f"grad_chips_{names[0]}", 1)
            for i, nm in enumerate(names):
                rem[nm] = r_[3 * i:3 * i + 3]

    ffn_in_dg = dict(ja='c', jb='c', nj=8, tm=tq, tn=d, tk=wf)
    ffn_in_wg = dict(ja='b', jo='b', ta=True, nj=8, tm=wf, tn=d, tk=t)
    ffn_out_wg = dict(ja='b', jo='b', ta=True, nj=4, tm=wf, tn=d, tk=t)

    gb['ple_w_in'] = _mm(p_b, dpe_b, name="ple_in_wg", jb='b', jo='b', b_flat=True, ta=True, nj=8,
                         tm=p_b.shape[1], tn=128, tk=t)
    gb['ple_w_gate'] = blocked('ple_w_gate', _mm(x3b, dgp_b, name="ple_gate_wg", ta=True, tm=d, tn=d, tk=t))
    g_ple = ('ple_w_in', 'ple_w_gate')
    dx3_g, (s_,) = _mm(dgp_b, wgate, name="ple_gate_dg", tb=True, tm=tq, tn=d, tk=d, plans=[to_sibling(*g_ple)])
    waiting.append((g_ple, s_))

    dr3, df2_b, dg3, db3 = _ln_bwd(r3, [], ln3_g, ln3_b, [(dr4, ALPHA), (dx3_g, 1.0)], name="ffn2_ln_bwd", fs=0.5)
    dh2, _ = _ffn_out_dg(df2_b, w2out, h2, name="ffn2_out_dg", tm=tf)
    send_chips()
    gb['ffn2_w_out'] = blocked('ffn2_w_out', _mm(a2, df2_b, name="ffn2_out_wg", **ffn_out_wg))
    send_sibling('ffn2_w_out')
    dx2_f = _mm(dh2, w2in, name="ffn2_in_dg", **ffn_in_dg)
    send_chips()
    gb['ffn2_w_in'] = _mm(dh2, x2b, name="ffn2_in_wg", **ffn_in_wg)
    send_sibling('ffn2_w_in')

    dr2, dmix_b, dg2, db2 = _ln_bwd(r2, [], ln2_g, ln2_b, [(dr3, ALPHA), (dx2_f, 1.0)], name="mix_ln_bwd")
    dmerged = _mm(dmix_b, wmo, name="mix_out_dg", tb=True, tm=tq, tn=d, tk=d)
    gb['mix_w_out'] = blocked('mix_w_out', _mm(merged, dmix_b, name="mix_out_wg", ta=True, tm=d, tn=d, tk=t))
    dyconv_b, dglu_b, dgate_b = _gate_bwd(dmerged, yconv, glu, proj, name="gate_bwd")
    send_chips()
    gb['conv_w_out'] = _mm(ycin, dyconv_b, name="conv_out_wg", jb='b', jo='b', b_flat=True, ta=True, nj=8,
                           tm=CONV_CH, tn=128, tk=t)
    dycin = _mm(dyconv_b, wco, name="conv_out_dg", tb=True, tm=tq, tn=CONV_CH, tk=d)
    gb['ssm_w_glu'] = _mm(sg, dglu_b, name="glu_in_wg", jb='b', jo='b', b_flat=True, ta=True, nj=8,
                          tm=SSM_W, tn=256, tk=t)
    send_sibling('mix_w_out', 'conv_w_out', 'ssm_w_glu')
    dsg = _mm(dglu_b, wglu, name="glu_in_dg", tb=True, tm=tq, tn=SSM_W, tk=2 * d)
    dys, dys_b, dd = _s5_bwd_in(dsg, ys, proj, name="ssm_out_bwd")
    h_b, da_acc, _ = _scan_bwd(dys_b, cmat_tb, s_f, tab_b, name="scan_bwd")
    send_chips()
    dumm = _mm(h_b, bmat_b, name="ssm_write_dg", a_flat=True, o_flat=True, tb=True, tm=tq, tn=SCAN_UW,
               tk=2 * SCAN_CB, **blk)
    du_b = _s5_du(dumm, dys, dvec, name="ssm_du")
    g_bmat = _mm(u_b, h_b, name="ssm_write_wg", a_flat=True, b_flat=True, ta=True, tm=SCAN_UW,
                 tn=2 * SCAN_CB, tk=t, **blk)
    g_cmat = _mm(dys_b, s_b, name="ssm_read_wg", a_flat=True, b_flat=True, ta=True, tm=SCAN_UW,
                 tn=2 * SCAN_CB, tk=t, **blk)
    dcb_b, dcc_b, dch_b, dconv = _conv_bwd(dycin, proj, cw8, conv_b, name="conv_bwd")
    dproj = jnp.concatenate([dcb_b, dcc_b, dch_b, du_b, dgate_b], axis=1)
    gb['mix_w_in'] = _mm(x1b, dproj, name="mix_in_wg", jb='b', jo='b', b_flat=True, ta=True, nj=8,
                         tm=d, tn=512, tk=t)
    send_sibling('mix_w_in')
    dx1_m = _mm(dproj, wmix, name="mix_in_dg", ja='c', jb='c', a_flat=True, tb=True, nj=8, tm=tq, tn=d, tk=512)

    dr1, df1_b, dg1, db1 = _ln_bwd(r1, [], ln1_g, ln1_b, [(dr2, ALPHA), (dx1_m, 1.0)], name="ffn1_ln_bwd", fs=0.5)
    dh1, _ = _ffn_out_dg(df1_b, w1out, h1, name="ffn1_out_dg", tm=tf)
    send_chips()

    da_sum = jnp.sum(da_acc, axis=0)
    da_re, da_im = _unperm_cols(da_sum)
    gbb_re, gbb_im = [jnp.transpose(v, (0, 2, 1)) for v in _compact_extract(g_bmat, SSM_GROUP)]
    g_c_re, g_c_im_neg = _compact_extract(g_cmat, SSM_GROUP)
    g_c_im = -g_c_im_neg
    g_lam_re, g_lam_im, g_log_step, g_b_re, g_b_im = s5_vjp(
        (da_re.reshape(SSM_GROUPS, SSM_STATE), da_im.reshape(SSM_GROUPS, SSM_STATE), gbb_re, gbb_im))
    g_d = dd.reshape(SSM_GROUPS, SSM_GROUP)

    small_g = {'ln1_g': dg1, 'ln1_b': db1, 'conv_b': dconv[3:4], 'ssm_lam_re': g_lam_re, 'ssm_lam_im': g_lam_im,
               'ssm_log_step': g_log_step, 'ssm_b_re': g_b_re, 'ssm_b_im': g_b_im, 'ssm_c_re': g_c_re,
               'ssm_c_im': g_c_im, 'ssm_d': g_d, 'ln2_g': dg2, 'ln2_b': db2, 'ln3_g': dg3, 'ln3_b': db3,
               'ln4_g': dg4, 'ln4_b': db4}
    small_shapes = [args[nm].shape for nm in small] + [(3, CONV_CH), (1,)]
    g_pack = _pack([small_g[nm] for nm in small] + [dconv[0:3], loss_part[0:1, 0:1]])

    (g_all,) = gather([g_pack], "small")
    gb['ffn1_w_in'] = _mm(dh1, x_b, name="ffn1_in_wg", **ffn_in_wg)
    send_sibling('ffn1_w_in')
    gb['ffn1_w_out'] = blocked('ffn1_w_out', _mm(a1, df1_b, name="ffn1_out_wg", **ffn_out_wg))
    send_chips()
    send_sibling('ffn1_w_out')
    (grad_x,) = _mm(dh1, w1in, name="ffn1_in_dg", **ffn_in_dg,
                    epilogue=(lambda pr, drv: (pr + ALPHA * drv,), (dr1,), (), (F32,)))
    send_chips()

    def full_cw(a):
        return lax.dynamic_update_slice(jnp.zeros((3, CONV_CH), F32), a[0], (0, me * cw_cols))

    zero1 = jnp.zeros((1,), F32)
    w_pack = _pack([args[nm] for nm in small] + [full_cw(conv_w), zero1])
    m_pack = _pack([args['m_' + nm] for nm in small] + [full_cw(m_conv_w), zero1])
    v_pack = _pack([args['v_' + nm] for nm in small] + [full_cw(v_conv_w), zero1])
    sg_sum, sd, sm, sv = _small_update(g_all, w_pack, m_pack, v_pack, name="small_update")
    res = {}
    for key, buf in (('grad_', sg_sum), ('delta_', sd), ('new_m_', sm), ('new_v_', sv)):
        parts = _unpack(buf, small_shapes)
        for nm, val in zip(small, parts[:len(small)]):
            res[key + nm] = val
        res[key + 'conv_w'] = lax.dynamic_slice(parts[len(small)], (0, me * cw_cols), (3, cw_cols))[None]
        if key == 'grad_':
            loss = parts[-1][0]

    me1 = me.reshape(1)
    last = ['ffn1_w_in', 'ffn1_w_out']
    for nm in [w for w in big if w not in last] + last:
        upd = _shard_update(gb[nm], sib[nm], rem[nm], me1, local(args[nm], nm), local(args['m_' + nm], nm),
                            local(args['v_' + nm], nm), name=f"update_{nm}")
        for key, val in zip(('grad_', 'delta_', 'new_m_', 'new_v_'), upd):
            res[key + nm] = (jnp.swapaxes(val, 0, 1) if nm in turned else val)[None]

    outs = [loss, grad_x[None]]
    for key in ('grad_', 'delta_', 'new_m_', 'new_v_'):
        outs += [res[key + nm] for nm in order]
    return tuple(outs)
```

```python
import functools
import math

import jax
import jax.numpy as jnp
from jax import lax
from jax.experimental import pallas as pl
from jax.experimental.pallas import tpu as pltpu
from jax.experimental.pallas import tpu_sc as plsc

F32 = jnp.float32
BF16 = jnp.bfloat16
MESH = pl.DeviceIdType.MESH

N_DEV = 8
ALPHA = 2.0 ** 0.25
LN_EPS = 1e-5
CONV_CH = 512
SSM_W = 512
SSM_GROUPS = 32
SSM_GROUP = 16
SSM_STATE = 64
SSM_CH = SSM_GROUPS * SSM_STATE
SCAN_CB = 512
SCAN_NCB = SSM_CH // SCAN_CB
SCAN_GPB = SSM_GROUPS // SCAN_NCB
SCAN_UW = SCAN_GPB * SSM_GROUP
SCAN_TT = 256
ADAM_LR = 0.001
ADAM_B1 = 0.9
ADAM_B2 = 0.999
ADAM_EPS = 1e-08
ADAM_WD = 0.01
ADAM_STEP = 10
VMEM_LIMIT = 56 * 1024 * 1024


def _cparams(sem=None, **kw):
    return pltpu.CompilerParams(dimension_semantics=sem, vmem_limit_bytes=VMEM_LIMIT, **kw)


def _mm(a, b, *, name, ja=None, jb=None, jo=None, a_flat=False, b_flat=False, o_flat=False,
        ta=False, tb=False, tm, tn, tk, nj=1, out_dtype=F32, plans=(), epilogue=None):
    def dims(arr, j, flat):
        if j is None:
            return arr.shape
        if flat:
            return (arr.shape[0], arr.shape[1] // nj)
        assert arr.shape[0] == nj, (name, arr.shape, nj)
        return arr.shape[1:]

    ar, ac = dims(a, ja, a_flat)
    br, bc = dims(b, jb, b_flat)
    m, k = (ac, ar) if ta else (ar, ac)
    k2, n = (bc, br) if tb else (br, bc)
    assert k == k2, (name, a.shape, b.shape)
    assert m % tm == 0 and n % tn == 0 and k % tk == 0, (name, m, n, k, tm, tn, tk)
    njb = nj if 'b' in (ja, jb) else 1
    njc = nj if 'c' in (ja, jb) else 1
    nk = k // tk
    j_inside = njc > 1 and nk == 1 and not ta
    n_in = njc if j_inside else 1
    nred = nk if j_inside else njc * nk
    grid = (njb, m // tm, n // tn, 1 if j_inside else njc, nk)

    def make_spec(j, flat, blk, rfn, cfn, cols_per_j):
        def jsel(g, c):
            return g if j == 'b' else c
        if j is None:
            return pl.BlockSpec(blk, lambda g, i, jn, c, kk: (rfn(i, jn, kk), cfn(i, jn, kk)))
        if j == 'c' and j_inside:
            if flat:
                return pl.BlockSpec((blk[0], nj * blk[1]), lambda g, i, jn, c, kk: (rfn(i, jn, kk), 0))
            return pl.BlockSpec((nj,) + blk, lambda g, i, jn, c, kk: (0, rfn(i, jn, kk), cfn(i, jn, kk)))
        if flat:
            nb = cols_per_j // blk[1]
            return pl.BlockSpec(blk, lambda g, i, jn, c, kk: (rfn(i, jn, kk), jsel(g, c) * nb + cfn(i, jn, kk)))
        return pl.BlockSpec((None,) + blk,
                            lambda g, i, jn, c, kk: (jsel(g, c), rfn(i, jn, kk), cfn(i, jn, kk)))

    if ta:
        a_spec = make_spec(ja, a_flat, (tk, tm), lambda i, jn, kk: kk, lambda i, jn, kk: i, ac)
    else:
        a_spec = make_spec(ja, a_flat, (tm, tk), lambda i, jn, kk: i, lambda i, jn, kk: kk, ac)
    if tb:
        b_spec = make_spec(jb, b_flat, (tn, tk), lambda i, jn, kk: jn, lambda i, jn, kk: kk, bc)
    else:
        b_spec = make_spec(jb, b_flat, (tk, tn), lambda i, jn, kk: kk, lambda i, jn, kk: jn, bc)
    o_spec = make_spec(jo, o_flat, (tm, tn), lambda i, jn, kk: i, lambda i, jn, kk: jn, n)
    if jo is None:
        out_shape = (m, n)
    elif o_flat:
        out_shape = (m, nj * n)
    else:
        out_shape = (nj, m, n)

    dn = (((0 if ta else 1,), (1 if tb else 0,)), ((), ()))

    def operand(ref, j, flat, jj, width):
        if not (j == 'c' and j_inside):
            return ref[...]
        return ref[:, jj * width:(jj + 1) * width] if flat else ref[jj]

    e_fn, e_rows, e_vecs, e_dtypes = epilogue if epilogue else (None, (), (), (out_dtype,))
    assert not epilogue or (nred == 1 and jo is None), name
    n_e = len(e_rows) + len(e_vecs)
    n_o = len(e_dtypes)

    def body(a_ref, b_ref, *rest):
        e_refs, o_refs, scratch = rest[:n_e], rest[n_e:n_e + n_o], rest[n_e + n_o:]
        o_ref = o_refs[0]
        p = None
        for jj in range(n_in):
            q = lax.dot_general(operand(a_ref, ja, a_flat, jj, tk), operand(b_ref, jb, b_flat, jj, tk if tb else tn),
                                dn, preferred_element_type=F32)
            p = q if p is None else p + q
        if epilogue:
            for ref, val in zip(o_refs, e_fn(p, *[r[...] for r in e_refs])):
                ref[...] = val.astype(ref.dtype)
        elif nred == 1:
            o_ref[...] = p.astype(o_ref.dtype)
        else:
            acc = scratch[0]
            r = pl.program_id(3) * nk + pl.program_id(4)

            @pl.when(r == 0)
            def _():
                acc[...] = p

            @pl.when(r > 0)
            def _():
                acc[...] += p

            @pl.when(r == nred - 1)
            def _():
                o_ref[...] = acc[...].astype(o_ref.dtype)

    vec_spec = pl.BlockSpec((1, tn), lambda g, i, jn, c, kk: (0, jn))
    res = _call_with_plans(
        body, plans, name=name, grid=grid,
        in_specs=[a_spec, b_spec] + [o_spec] * len(e_rows) + [vec_spec] * len(e_vecs), out_specs=[o_spec] * n_o,
        out_shape=[jax.ShapeDtypeStruct(out_shape, dt) for dt in e_dtypes],
        scratch_shapes=[] if nred == 1 else [pltpu.VMEM((tm, tn), F32)],
        semantics=("parallel", "parallel", "parallel", "arbitrary", "arbitrary"), operands=(a, b, *e_rows, *e_vecs))
    outs = res[0] if epilogue else res[0][0]
    return (outs, res[1]) if plans else outs


def _sigmoid(v):
    return jax.nn.sigmoid(v)


def _row_spec(tm, cols, colblk=0):
    return pl.BlockSpec((tm, cols), lambda i: (i, colblk))


def _vec_spec(cols):
    return pl.BlockSpec((1, cols), lambda i: (0, 0))


def _ffn_in(xb, win, *, name, tm, plans=()):
    t, d = xb.shape
    nj, w, _ = win.shape
    half = nj // 2
    dn = (((1,), (1,)), ((), ()))

    def body(x_ref, wg_ref, wu_ref, a_ref, gu_ref):
        xv = x_ref[...]
        g = lax.dot_general(xv, wg_ref[...], dn, preferred_element_type=F32)
        u = lax.dot_general(xv, wu_ref[...], dn, preferred_element_type=F32)
        a_ref[...] = (g * _sigmoid(g) * u).astype(BF16)
        gu_ref[0] = g.astype(BF16)
        gu_ref[1] = u.astype(BF16)

    (a, gu), riders = _call_with_plans(
        body, plans, name=name, grid=(half, t // tm),
        in_specs=[pl.BlockSpec((tm, d), lambda j, i: (i, 0)),
                  pl.BlockSpec((None, w, d), lambda j, i: (j, 0, 0)),
                  pl.BlockSpec((None, w, d), lambda j, i: (j + half, 0, 0))],
        out_specs=[pl.BlockSpec((None, tm, w), lambda j, i: (j, i, 0)),
                   pl.BlockSpec((2, None, tm, w), lambda j, i: (0, j, i, 0))],
        out_shape=[jax.ShapeDtypeStruct((half, t, w), BF16), jax.ShapeDtypeStruct((2, half, t, w), BF16)],
        scratch_shapes=[], semantics=("parallel", "parallel"), operands=(xb, win, win))
    return a, gu, riders


def _ffn_out_dg(dfb, wout, gu, *, name, tm, plans=()):
    t, d = dfb.shape
    half, w, _ = wout.shape
    dn = (((1,), (1,)), ((), ()))

    def body(df_ref, w_ref, gu_ref, dh_ref):
        da = lax.dot_general(df_ref[...], w_ref[...], dn, preferred_element_type=F32)
        g = gu_ref[0].astype(F32)
        u = gu_ref[1].astype(F32)
        sg = _sigmoid(g)
        dh_ref[0] = (da * u * (sg * (1.0 + g * (1.0 - sg)))).astype(BF16)
        dh_ref[1] = (da * (g * sg)).astype(BF16)

    (out,), riders = _call_with_plans(
        body, plans, name=name, grid=(half, t // tm),
        in_specs=[pl.BlockSpec((tm, d), lambda j, i: (i, 0)),
                  pl.BlockSpec((None, w, d), lambda j, i: (j, 0, 0)),
                  pl.BlockSpec((2, None, tm, w), lambda j, i: (0, j, i, 0))],
        out_specs=[pl.BlockSpec((2, None, tm, w), lambda j, i: (0, j, i, 0))],
        out_shape=[jax.ShapeDtypeStruct((2, half, t, w), BF16)],
        scratch_shapes=[], semantics=("parallel", "parallel"), operands=(dfb, wout, gu))
    return out.reshape(2 * half, t, w), riders


def _ln_stats(r):
    mu = jnp.mean(r, axis=-1, keepdims=True)
    xc = r - mu
    var = jnp.mean(xc * xc, axis=-1, keepdims=True)
    rstd = lax.rsqrt(var + LN_EPS)
    return xc * rstd, rstd


def _ln_epilogue(xin, gamma, beta, fs):
    def fn(p, xv, g, b):
        r = ALPHA * xv + fs * p
        xh, _ = _ln_stats(r)
        y = xh * g + b
        return r, y, y

    return fn, (xin,), (gamma, beta), (F32, F32, BF16)


def _ln_bwd(xin, fparts, gamma, beta, grads, *, name, fs=1.0, ple=False, target=None, tm=512):
    t, d = xin.shape
    nf = len(fparts)
    ng = len(grads)
    coefs = [c for _, c in grads]
    use_t = target is not None
    n_fout = 2 if ple else 1

    def body(*refs):
        pos = 0
        x_ref = refs[pos]; pos += 1
        f_refs = refs[pos:pos + nf]; pos += nf
        g_ref, b_ref = refs[pos:pos + 2]; pos += 2
        gr_refs = refs[pos:pos + ng]; pos += ng
        if use_t:
            t_ref = refs[pos]; pos += 1
        dr_ref = refs[pos]; pos += 1
        fo_refs = refs[pos:pos + n_fout]; pos += n_fout
        dg_ref, db_ref = refs[pos:pos + 2]; pos += 2
        if use_t:
            loss_ref = refs[pos]; pos += 1
        i = pl.program_id(0)

        if ple:
            pe = f_refs[0][...]
            sg = _sigmoid(f_refs[1][...])
            resid = ALPHA * x_ref[...] + pe * sg
        else:
            resid = x_ref[...]
        xh, rstd = _ln_stats(resid)
        gam = g_ref[...]
        if use_t:
            diff = xh * gam + b_ref[...] - t_ref[...]
            dy = diff * (1.0 / d)
            lpart = 0.5 * jnp.sum(jnp.sum(diff * diff, axis=-1, keepdims=True), axis=0, keepdims=True) * (1.0 / d)
        else:
            dy = coefs[0] * gr_refs[0][...]
            for c, r in zip(coefs[1:], gr_refs[1:]):
                dy = dy + c * r[...]
        dxh = dy * gam
        m1 = jnp.mean(dxh, axis=-1, keepdims=True)
        m2 = jnp.mean(dxh * xh, axis=-1, keepdims=True)
        dr = rstd * (dxh - m1 - xh * m2)
        dr_ref[...] = dr
        if ple:
            fo_refs[0][...] = (dr * sg).astype(BF16)
            fo_refs[1][...] = (dr * pe * (sg * (1.0 - sg))).astype(BF16)
        else:
            fo_refs[0][...] = (fs * dr).astype(BF16)
        dgp = jnp.sum(dy * xh, axis=0, keepdims=True)
        dbp = jnp.sum(dy, axis=0, keepdims=True)

        @pl.when(i == 0)
        def _():
            dg_ref[...] = dgp
            db_ref[...] = dbp
            if use_t:
                loss_ref[...] = jnp.broadcast_to(lpart, loss_ref.shape)

        @pl.when(i > 0)
        def _():
            dg_ref[...] += dgp
            db_ref[...] += dbp
            if use_t:
                loss_ref[...] += jnp.broadcast_to(lpart, loss_ref.shape)

    ins = [xin, *fparts, gamma, beta, *[g for g, _ in grads]] + ([target] if use_t else [])
    in_specs = ([_row_spec(tm, d)] * (1 + nf) + [_vec_spec(d), _vec_spec(d)] + [_row_spec(tm, d)] * ng
                + ([_row_spec(tm, d)] if use_t else []))
    out_specs = [_row_spec(tm, d)] * (1 + n_fout) + [_vec_spec(d), _vec_spec(d)] + ([_vec_spec(128)] if use_t else [])
    out_shape = ([jax.ShapeDtypeStruct((t, d), F32)] + [jax.ShapeDtypeStruct((t, d), BF16)] * n_fout
                 + [jax.ShapeDtypeStruct((1, d), F32)] * 2 + ([jax.ShapeDtypeStruct((1, 128), F32)] if use_t else []))
    return pl.pallas_call(
        body, name=name, grid=(t // tm,), in_specs=in_specs, out_specs=out_specs, out_shape=out_shape,
        compiler_params=_cparams(("arbitrary",)),
    )(*ins)


def _conv_fwd(proj, cw, cb, *, name, tm=512):
    t = proj.shape[0]
    c = CONV_CH
    hb = tm // 8

    def body(b_ref, c_ref, h_ref, cp_ref, hp_ref, w_ref, bias_ref, o_ref, q_scr):
        i = pl.program_id(0)
        q = c_ref[...] * h_ref[...]
        halo = jnp.where(i > 0, cp_ref[...] * hp_ref[...], 0.0)
        q_scr[0:8, :] = halo
        q_scr[8:, :] = q
        z = (w_ref[2:3, :] * q + w_ref[1:2, :] * q_scr[pl.ds(7, tm), :] + w_ref[0:1, :] * q_scr[pl.ds(6, tm), :]
             + bias_ref[...])
        o_ref[...] = (b_ref[...] * z).astype(BF16)

    prev = lambda blk: pl.BlockSpec((8, c), lambda i: (jnp.maximum(i * hb - 1, 0), blk))
    return pl.pallas_call(
        body, name=name, grid=(t // tm,),
        in_specs=[_row_spec(tm, c, 0), _row_spec(tm, c, 1), _row_spec(tm, c, 2), prev(1), prev(2),
                  pl.BlockSpec((8, c), lambda i: (0, 0)), _vec_spec(c)],
        out_specs=_row_spec(tm, c),
        out_shape=jax.ShapeDtypeStruct((t, c), BF16),
        scratch_shapes=[pltpu.VMEM((tm + 8, c), F32)],
        compiler_params=_cparams(("parallel",)),
    )(proj, proj, proj, proj, proj, cw, cb)


def _conv_bwd(dyc, proj, cw, cb, *, name, tm=512):
    t = proj.shape[0]
    c = CONV_CH
    hb = tm // 8
    nblk = t // 8

    def body(d_ref, b_ref, c_ref, h_ref, cp_ref, hp_ref, dn_ref, bn_ref, w_ref, bias_ref,
             db_ref, dc_ref, dh_ref, dw_ref, q_scr, z_scr):
        i = pl.program_id(0)
        last = pl.num_programs(0) - 1
        cc = c_ref[...]
        ch = h_ref[...]
        q = cc * ch
        halo = jnp.where(i > 0, cp_ref[...] * hp_ref[...], 0.0)
        q_scr[0:8, :] = halo
        q_scr[8:, :] = q
        w0, w1, w2 = w_ref[0:1, :], w_ref[1:2, :], w_ref[2:3, :]
        qm1 = q_scr[pl.ds(7, tm), :]
        qm2 = q_scr[pl.ds(6, tm), :]
        z = w2 * q + w1 * qm1 + w0 * qm2 + bias_ref[...]
        d = d_ref[...]
        bb = b_ref[...]
        db_ref[...] = (d * z).astype(BF16)
        dz = d * bb
        z_scr[0:tm, :] = dz
        z_scr[tm:, :] = jnp.where(i < last, dn_ref[...] * bn_ref[...], 0.0)
        dq = w2 * dz + w1 * z_scr[pl.ds(1, tm), :] + w0 * z_scr[pl.ds(2, tm), :]
        dc_ref[...] = (dq * ch).astype(BF16)
        dh_ref[...] = (dq * cc).astype(BF16)
        row = lax.broadcasted_iota(jnp.int32, (8, c), 0)
        part = jnp.zeros((8, c), F32)
        for k, term in enumerate((dz * qm2, dz * qm1, dz * q, dz)):
            part = jnp.where(row == k, jnp.sum(term, axis=0, keepdims=True), part)

        @pl.when(i == 0)
        def _():
            dw_ref[...] = part

        @pl.when(i > 0)
        def _():
            dw_ref[...] += part

    prev = lambda blk: pl.BlockSpec((8, c), lambda i: (jnp.maximum(i * hb - 1, 0), blk))
    nxt_p = pl.BlockSpec((8, c), lambda i: (jnp.minimum((i + 1) * hb, nblk - 1), 0))
    nxt_d = pl.BlockSpec((8, c), lambda i: (jnp.minimum((i + 1) * hb, nblk - 1), 0))
    return pl.pallas_call(
        body, name=name, grid=(t // tm,),
        in_specs=[_row_spec(tm, c), _row_spec(tm, c, 0), _row_spec(tm, c, 1), _row_spec(tm, c, 2),
                  prev(1), prev(2), nxt_d, nxt_p, pl.BlockSpec((8, c), lambda i: (0, 0)), _vec_spec(c)],
        out_specs=[_row_spec(tm, c)] * 3 + [pl.BlockSpec((8, c), lambda i: (0, 0))],
        out_shape=[jax.ShapeDtypeStruct((t, c), BF16)] * 3 + [jax.ShapeDtypeStruct((8, c), F32)],
        scratch_shapes=[pltpu.VMEM((tm + 8, c), F32), pltpu.VMEM((tm + 8, c), F32)],
        compiler_params=_cparams(("arbitrary",)),
    )(dyc, proj, proj, proj, proj, proj, dyc, proj, cw, cb)


def _gate_fwd(yconv, glu, proj, *, name, tm=512):
    t, d = yconv.shape

    def body(yc_ref, ga_ref, gb_ref, gc_ref, gs_ref, o_ref):
        yssm = ga_ref[...] * _sigmoid(gb_ref[...])
        o_ref[...] = (_sigmoid(gc_ref[...]) * yc_ref[...] + _sigmoid(gs_ref[...]) * yssm).astype(BF16)

    return pl.pallas_call(
        body, name=name, grid=(t // tm,),
        in_specs=[_row_spec(tm, d), _row_spec(tm, d, 0), _row_spec(tm, d, 1), _row_spec(tm, d, 2), _row_spec(tm, d, 3)],
        out_specs=_row_spec(tm, d), out_shape=jax.ShapeDtypeStruct((t, d), BF16),
        compiler_params=_cparams(("parallel",)),
    )(yconv, glu, glu, proj, proj)


def _gate_bwd(dm, yconv, glu, proj, *, name, tm=512):
    t, d = yconv.shape

    def body(dm_ref, yc_ref, ga_ref, gb_ref, gc_ref, gs_ref, dyc_ref, dglu_ref, dgate_ref):
        dmv = dm_ref[...]
        sc = _sigmoid(gc_ref[...])
        ss = _sigmoid(gs_ref[...])
        sb = _sigmoid(gb_ref[...])
        ga = ga_ref[...]
        yssm = ga * sb
        dyc_ref[...] = (dmv * sc).astype(BF16)
        dgate_ref[:, 0:d] = (dmv * yc_ref[...] * (sc * (1.0 - sc))).astype(BF16)
        dys = dmv * ss
        dgate_ref[:, d:2 * d] = (dmv * yssm * (ss * (1.0 - ss))).astype(BF16)
        dglu_ref[:, 0:d] = (dys * sb).astype(BF16)
        dglu_ref[:, d:2 * d] = (dys * ga * (sb * (1.0 - sb))).astype(BF16)

    return pl.pallas_call(
        body, name=name, grid=(t // tm,),
        in_specs=[_row_spec(tm, d), _row_spec(tm, d), _row_spec(tm, d, 0), _row_spec(tm, d, 1),
                  _row_spec(tm, d, 2), _row_spec(tm, d, 3)],
        out_specs=[_row_spec(tm, d), _row_spec(tm, 2 * d), _row_spec(tm, 2 * d)],
        out_shape=[jax.ShapeDtypeStruct((t, d), BF16), jax.ShapeDtypeStruct((t, 2 * d), BF16),
                   jax.ShapeDtypeStruct((t, 2 * d), BF16)],
        compiler_params=_cparams(("parallel",)),
    )(dm, yconv, glu, glu, proj, proj)


_GELU_C = math.sqrt(2.0 / math.pi)


def _gelu(v):
    return 0.5 * v * (1.0 + jnp.tanh(_GELU_C * (v + 0.044715 * v * v * v)))


def _gelu_grad(v):
    th = jnp.tanh(_GELU_C * (v + 0.044715 * v * v * v))
    return 0.5 * (1.0 + th) + 0.5 * v * (1.0 - th * th) * (_GELU_C * (1.0 + 3.0 * 0.044715 * v * v))


def _cmul(ar, ai, br, bi):
    return ar * br - ai * bi, ar * bi + ai * br


def _scan_fwd(proj, bmat, tab, *, name, plans=()):
    t = proj.shape[0]
    tt, cbw = SCAN_TT, SCAN_CB
    w2 = 2 * cbw

    def body(u_ref, b_ref, tab_ref, s_ref, sb_ref, bu_scr, carry):
        ti = pl.program_id(1)

        @pl.when(ti == 0)
        def _():
            carry[...] = jnp.zeros_like(carry)

        bu_scr[...] = jnp.dot(u_ref[...].astype(BF16), b_ref[...], preferred_element_type=F32)
        row = lax.broadcasted_iota(jnp.int32, (8, cbw), 0)

        def blk(bi, c):
            cr, ci = c
            r0 = pl.multiple_of(bi * 8, 8)
            xr = bu_scr[pl.ds(r0, 8), 0:cbw]
            xi = bu_scr[pl.ds(r0, 8), cbw:w2]
            for k, sh in enumerate((1, 2, 4)):
                kr = tab_ref[k:k + 1, 0:cbw]
                ki = tab_ref[k:k + 1, cbw:w2]
                sr = jnp.where(row >= sh, pltpu.roll(xr, sh, 0), 0.0)
                si = jnp.where(row >= sh, pltpu.roll(xi, sh, 0), 0.0)
                pr, pi = _cmul(kr, ki, sr, si)
                xr = xr + pr
                xi = xi + pi
            pr, pi = _cmul(tab_ref[8:16, 0:cbw], tab_ref[8:16, cbw:w2], cr, ci)
            xr = xr + pr
            xi = xi + pi
            s_ref[pl.ds(r0, 8), 0:cbw] = xr
            s_ref[pl.ds(r0, 8), cbw:w2] = xi
            return (jnp.broadcast_to(xr[7:8, :], (8, cbw)), jnp.broadcast_to(xi[7:8, :], (8, cbw)))

        cr, ci = lax.fori_loop(0, tt // 8, blk, (carry[:, 0:cbw], carry[:, cbw:w2]))
        carry[:, 0:cbw] = cr
        carry[:, cbw:w2] = ci
        sb_ref[...] = s_ref[...].astype(BF16)

    (s, sb), riders = _call_with_plans(
        body, plans, name=name, grid=(SCAN_NCB, t // tt),
        in_specs=[pl.BlockSpec((tt, SCAN_UW), lambda cb, ti: (ti, 3 * SCAN_NCB + cb)),
                  pl.BlockSpec((None, SCAN_UW, w2), lambda cb, ti: (cb, 0, 0)),
                  pl.BlockSpec((16, w2), lambda cb, ti: (0, cb))],
        out_specs=[pl.BlockSpec((tt, w2), lambda cb, ti: (ti, cb))] * 2,
        out_shape=[jax.ShapeDtypeStruct((t, 2 * SSM_CH), F32), jax.ShapeDtypeStruct((t, 2 * SSM_CH), BF16)],
        scratch_shapes=[pltpu.VMEM((tt, w2), F32), pltpu.VMEM((8, w2), F32)],
        semantics=("parallel", "arbitrary"), operands=(proj, bmat, tab))
    return s, sb, riders


def _scan_bwd(dyb, cmat_t, s, tabb, *, name, plans=()):
    t = s.shape[0]
    tt, cbw = SCAN_TT, SCAN_CB
    w2 = 2 * cbw
    nt = t // tt
    hb = tt // 8

    def body(dy_ref, c_ref, s_ref, sp_ref, tab_ref, h_ref, da_ref, g_scr, s_scr, carry):
        ti = pl.program_id(1)

        @pl.when(ti == 0)
        def _():
            carry[...] = jnp.zeros_like(carry)
            da_ref[...] = jnp.zeros_like(da_ref)

        g_scr[...] = jnp.dot(dy_ref[...], c_ref[...], preferred_element_type=F32)
        s_scr[0:8, :] = jnp.where(ti < nt - 1, sp_ref[...], 0.0)
        s_scr[8:, :] = s_ref[...]
        row = lax.broadcasted_iota(jnp.int32, (8, cbw), 0)

        def blk(k, c):
            cr, ci, ar, ai = c
            bi = hb - 1 - k
            r0 = pl.multiple_of(bi * 8, 8)
            xr = g_scr[pl.ds(r0, 8), 0:cbw]
            xi = g_scr[pl.ds(r0, 8), cbw:w2]
            for j, sh in enumerate((1, 2, 4)):
                kr = tab_ref[j:j + 1, 0:cbw]
                ki = tab_ref[j:j + 1, cbw:w2]
                sr = jnp.where(row < 8 - sh, pltpu.roll(xr, 8 - sh, 0), 0.0)
                si = jnp.where(row < 8 - sh, pltpu.roll(xi, 8 - sh, 0), 0.0)
                pr, pi = _cmul(kr, ki, sr, si)
                xr = xr + pr
                xi = xi + pi
            pr, pi = _cmul(tab_ref[8:16, 0:cbw], tab_ref[8:16, cbw:w2], cr, ci)
            xr = xr + pr
            xi = xi + pi
            h_ref[pl.ds(r0, 8), 0:cbw] = xr.astype(BF16)
            h_ref[pl.ds(r0, 8), cbw:w2] = xi.astype(BF16)
            pvr = s_scr[pl.ds(r0, 8), 0:cbw]
            pvi = s_scr[pl.ds(r0, 8), cbw:w2]
            cur_r = s_scr[pl.ds(r0 + 8, 8), 0:cbw]
            cur_i = s_scr[pl.ds(r0 + 8, 8), cbw:w2]
            spr = jnp.where(row == 0, jnp.broadcast_to(pvr[7:8, :], (8, cbw)), pltpu.roll(cur_r, 1, 0))
            spi = jnp.where(row == 0, jnp.broadcast_to(pvi[7:8, :], (8, cbw)), pltpu.roll(cur_i, 1, 0))
            ar = ar + spr * xr + spi * xi
            ai = ai + spr * xi - spi * xr
            return (jnp.broadcast_to(xr[0:1, :], (8, cbw)), jnp.broadcast_to(xi[0:1, :], (8, cbw)), ar, ai)

        z = jnp.zeros((8, cbw), F32)
        cr, ci, ar, ai = lax.fori_loop(0, hb, blk, (carry[:, 0:cbw], carry[:, cbw:w2], z, z))
        carry[:, 0:cbw] = cr
        carry[:, cbw:w2] = ci
        da_ref[:, 0:cbw] += ar
        da_ref[:, cbw:w2] += ai

    rt = lambda ti: nt - 1 - ti
    (h, da), riders = _call_with_plans(
        body, plans, name=name, grid=(SCAN_NCB, nt),
        in_specs=[pl.BlockSpec((tt, SCAN_UW), lambda cb, ti: (rt(ti), cb)),
                  pl.BlockSpec((None, SCAN_UW, w2), lambda cb, ti: (cb, 0, 0)),
                  pl.BlockSpec((tt, w2), lambda cb, ti: (rt(ti), cb)),
                  pl.BlockSpec((8, w2), lambda cb, ti: (jnp.maximum(rt(ti) * hb - 1, 0), cb)),
                  pl.BlockSpec((16, w2), lambda cb, ti: (0, cb))],
        out_specs=[pl.BlockSpec((tt, w2), lambda cb, ti: (rt(ti), cb)),
                   pl.BlockSpec((8, w2), lambda cb, ti: (0, cb))],
        out_shape=[jax.ShapeDtypeStruct((t, 2 * SSM_CH), BF16), jax.ShapeDtypeStruct((8, 2 * SSM_CH), F32)],
        scratch_shapes=[pltpu.VMEM((tt, w2), F32), pltpu.VMEM((tt + 8, w2), F32), pltpu.VMEM((8, w2), F32)],
        semantics=("parallel", "arbitrary"), operands=(dyb, cmat_t, s, s, tabb))
    return h, da, riders


def _s5_out(ymm, proj, dvec, *, name, tm=512):
    t, w = ymm.shape

    def body(y_ref, u_ref, d_ref, yo_ref, sg_ref, ub_ref):
        u = u_ref[...]
        y = y_ref[...] + d_ref[...] * u
        yo_ref[...] = y
        sg_ref[...] = _gelu(y).astype(BF16)
        ub_ref[...] = u.astype(BF16)

    return pl.pallas_call(
        body, name=name, grid=(t // tm,),
        in_specs=[_row_spec(tm, w), _row_spec(tm, w, 3), _vec_spec(w)],
        out_specs=[_row_spec(tm, w)] * 3,
        out_shape=[jax.ShapeDtypeStruct((t, w), F32), jax.ShapeDtypeStruct((t, w), BF16), jax.ShapeDtypeStruct((t, w), BF16)],
        compiler_params=_cparams(("parallel",)),
    )(ymm, proj, dvec)


def _s5_bwd_in(dsg, y, proj, *, name, tm=512):
    t, w = y.shape

    def body(d_ref, y_ref, u_ref, dy_ref, dyb_ref, dd_ref):
        i = pl.program_id(0)
        dy = d_ref[...] * _gelu_grad(y_ref[...])
        dy_ref[...] = dy
        dyb_ref[...] = dy.astype(BF16)
        part = jnp.sum(dy * u_ref[...], axis=0, keepdims=True)

        @pl.when(i == 0)
        def _():
            dd_ref[...] = part

        @pl.when(i > 0)
        def _():
            dd_ref[...] += part

    return pl.pallas_call(
        body, name=name, grid=(t // tm,),
        in_specs=[_row_spec(tm, w), _row_spec(tm, w), _row_spec(tm, w, 3)],
        out_specs=[_row_spec(tm, w), _row_spec(tm, w), _vec_spec(w)],
        out_shape=[jax.ShapeDtypeStruct((t, w), F32), jax.ShapeDtypeStruct((t, w), BF16), jax.ShapeDtypeStruct((1, w), F32)],
        compiler_params=_cparams(("arbitrary",)),
    )(dsg, y, proj)


def _s5_du(dumm, dy, dvec, *, name, tm=512):
    t, w = dy.shape

    def body(a_ref, dy_ref, d_ref, o_ref):
        o_ref[...] = (a_ref[...] + d_ref[...] * dy_ref[...]).astype(BF16)

    return pl.pallas_call(
        body, name=name, grid=(t // tm,), in_specs=[_row_spec(tm, w), _row_spec(tm, w), _vec_spec(w)],
        out_specs=_row_spec(tm, w), out_shape=jax.ShapeDtypeStruct((t, w), BF16),
        compiler_params=_cparams(("parallel",)),
    )(dumm, dy, dvec)


def _s5_discretise(lam_re, lam_im, log_step, b_re, b_im):
    lam = lax.complex(lam_re, lam_im)
    dt = jnp.exp(log_step)[:, None]
    a = jnp.exp(lam * dt)
    bbar = ((a - 1.0) / lam)[..., None] * lax.complex(b_re, b_im)
    return jnp.real(a), jnp.imag(a), jnp.real(bbar), jnp.imag(bbar)


def _perm_cols(re, im):
    lead = re.shape[:-1]
    r = re.reshape(lead + (SCAN_NCB, 1, SCAN_CB))
    i = im.reshape(lead + (SCAN_NCB, 1, SCAN_CB))
    return jnp.concatenate([r, i], axis=-2).reshape(lead + (2 * SSM_CH,))


def _unperm_cols(x):
    lead = x.shape[:-1]
    y = x.reshape(lead + (SCAN_NCB, 2, SCAN_CB))
    return y[..., 0, :].reshape(lead + (SSM_CH,)), y[..., 1, :].reshape(lead + (SSM_CH,))


def _compact(re, im):
    _, r, c = re.shape
    eye = jnp.eye(SCAN_GPB, dtype=re.dtype)

    def half(x):
        x = x.reshape(SCAN_NCB, SCAN_GPB, r, c)
        return (eye[None, :, None, :, None] * x[:, :, :, None, :]).reshape(SCAN_NCB, SCAN_GPB * r, SCAN_GPB * c)

    return jnp.concatenate([half(re), half(im)], axis=-1)


def _compact_extract(x, r):
    c = SSM_STATE
    eye = jnp.eye(SCAN_GPB, dtype=x.dtype)
    y = x.reshape(SCAN_NCB, SCAN_GPB, r, 2, SCAN_GPB, c)
    dg = jnp.sum(y * eye[None, :, None, None, :, None], axis=4).reshape(SSM_GROUPS, r, 2, c)
    return dg[:, :, 0, :], dg[:, :, 1, :]


def _pow_table(ar, ai, descending=False):
    ar = ar.reshape(1, SSM_CH)
    ai = ai.reshape(1, SSM_CH)
    pw = [(ar, ai)]
    for _ in range(7):
        pw.append(_cmul(pw[-1][0], pw[-1][1], ar, ai))
    zero = (jnp.zeros_like(ar), jnp.zeros_like(ar))
    rows = [pw[0], pw[1], pw[3]] + [zero] * 5 + (pw[::-1] if descending else pw)
    re = jnp.concatenate([r for r, _ in rows], axis=0)
    im = jnp.concatenate([i for _, i in rows], axis=0)
    return _perm_cols(re, im)


def _place():
    x, y, c = lax.axis_index("x"), lax.axis_index("y"), lax.axis_index("c")
    chips = [(1 - x, y), (x, 1 - y), (1 - x, 1 - y)]
    return x, y, c, chips


def _dev(px, py, pc):
    return 4 * px + 2 * py + pc


class _Plan:
    def __init__(self, ins, out_shapes, sem_shapes, start, finish, middle=None):
        self.ins, self.out_shapes, self.sem_shapes = list(ins), list(out_shapes), list(sem_shapes)
        self.start, self.finish, self.middle = start, finish, middle


def _split_plan_refs(plans, in_refs, out_refs, sem_refs):
    res, i, o, s = [], 0, 0, 0
    for p in plans:
        ni, no, ns = len(p.ins), len(p.out_shapes), len(p.sem_shapes)
        res.append((in_refs[i:i + ni], out_refs[o:o + no], sem_refs[s:s + ns]))
        i, o, s = i + ni, o + no, s + ns
    return res


def _run_plans(plans, *, name):
    ins = [a for p in plans for a in p.ins]
    outs = [o for p in plans for o in p.out_shapes]
    sems = [s for p in plans for s in p.sem_shapes]
    any_spec = pl.BlockSpec(memory_space=pl.ANY)

    def body(*refs):
        parts = _split_plan_refs(plans, refs[:len(ins)], refs[len(ins):len(ins) + len(outs)], refs[len(ins) + len(outs):])
        for p, r in zip(plans, parts):
            p.start(*r)
        for p, r in zip(plans, parts):
            if p.middle:
                p.middle(*r)
        for p, r in zip(plans, parts):
            p.finish(*r)

    res = pl.pallas_call(body, name=name, in_specs=[any_spec] * len(ins), out_specs=[any_spec] * len(outs),
                         out_shape=outs, scratch_shapes=sems)(*ins)
    return _split_plan_refs(plans, [], res, [])


def _run_plans_on_sequencer(plans, peers_of, *, name, collective_id):
    ins = [a for p in plans for a in p.ins]
    outs = [o for p in plans for o in p.out_shapes]
    sems = [s for p in plans for s in p.sem_shapes]

    def body(*refs):
        x, y, c, chips = _place()
        peers = peers_of(x, y, c, chips)
        barrier = pltpu.get_barrier_semaphore()
        for peer in peers:
            pl.semaphore_signal(barrier, inc=1, device_id=peer, device_id_type=MESH)
        pl.semaphore_wait(barrier, len(peers))
        parts = _split_plan_refs(plans, refs[:len(ins)], refs[len(ins):len(ins) + len(outs)], refs[len(ins) + len(outs):])
        for p, r in zip(plans, parts):
            p.start(*r)
        for p, r in zip(plans, parts):
            if p.middle:
                p.middle(*r)
        for p, r in zip(plans, parts):
            p.finish(*r)

    res = pl.kernel(body, name=name, out_type=outs, mesh=plsc.ScalarSubcoreMesh(axis_name="seq", num_cores=1),
                    scratch_types=sems, compiler_params=pltpu.CompilerParams(collective_id=collective_id))(*ins)
    return _split_plan_refs(plans, [], list(res), [])


def _call_with_plans(body, plans, *, name, grid, in_specs, out_specs, out_shape, scratch_shapes, semantics, operands):
    plans = list(plans)
    if not plans:
        res = pl.pallas_call(body, name=name, grid=grid, in_specs=in_specs, out_specs=out_specs, out_shape=out_shape,
                             scratch_shapes=scratch_shapes, compiler_params=_cparams(semantics))(*operands)
        return list(res), []
    n_in, n_out, n_scr = len(in_specs), len(out_specs), len(scratch_shapes)
    p_ins = [a for p in plans for a in p.ins]
    p_outs = [o for p in plans for o in p.out_shapes]
    p_sems = [s for p in plans for s in p.sem_shapes]
    nsteps = math.prod(grid)
    any_spec = pl.BlockSpec(memory_space=pl.ANY)

    def wrapped(*refs):
        bounds = [n_in, len(p_ins), n_out, len(p_outs), n_scr]
        parts, pos = [], 0
        for b in bounds:
            parts.append(refs[pos:pos + b])
            pos += b
        ins, p_in, outs, p_out, scr = parts
        step = pl.program_id(0)
        for ax in range(1, len(grid)):
            step = step * grid[ax] + pl.program_id(ax)
        riders = _split_plan_refs(plans, p_in, p_out, refs[pos:])

        @pl.when(step == 0)
        def _():
            for p, r in zip(plans, riders):
                p.start(*r)

        mids = [(p, r) for p, r in zip(plans, riders) if p.middle]
        mid_step = nsteps // 2
        split_mid = mids and 0 < mid_step < nsteps - 1
        if split_mid:
            @pl.when(step == mid_step)
            def _():
                for p, r in mids:
                    p.middle(*r)

        body(*ins, *outs, *scr)

        @pl.when(step == nsteps - 1)
        def _():
            if not split_mid:
                for p, r in mids:
                    p.middle(*r)
            for p, r in zip(plans, riders):
                p.finish(*r)

    res = pl.pallas_call(
        wrapped, name=name, grid=grid, in_specs=list(in_specs) + [any_spec] * len(p_ins),
        out_specs=list(out_specs) + [any_spec] * len(p_outs), out_shape=list(out_shape) + p_outs,
        scratch_shapes=list(scratch_shapes) + p_sems, compiler_params=_cparams(("arbitrary",) * len(grid)),
    )(*operands, *p_ins)
    return list(res[:n_out]), [r[1] for r in _split_plan_refs(plans, [], res[n_out:], [])]


def _gather_plan(shards):
    n = len(shards)
    nk = 8

    def make(ins, outs, sems):
        send, recv, lsem = sems
        x, y, c, _ = _place()
        me, sib, xn, yn, dg = (x, y, c), (x, y, 1 - c), (1 - x, y, c), (x, 1 - y, c), (1 - x, 1 - y, c)

        def part(w, block, half):
            ref = outs[w].at[_dev(*block)]
            if half is None:
                return ref
            rows = shards[w].shape[0] // 2
            return ref.at[pl.ds(half * rows, rows)]

        def copy(w, k, block, to, half=None, src=None):
            dst = part(w, block, half)
            return pltpu.make_async_remote_copy(
                src_ref=dst if src is None else src, dst_ref=dst,
                send_sem=send.at[w * nk + k], recv_sem=recv.at[w * nk + k], device_id=to, device_id_type=MESH)

        mine = [pltpu.make_async_copy(ins[w], outs[w].at[_dev(*me)], lsem.at[w]) for w in range(n)]
        return copy, mine, me, sib, xn, yn, dg

    def first_copies(copy, me, sib, xn, yn, ins):
        return [copy(w, k, me, to, src=ins[w]) for w in range(n) for k, to in ((0, sib), (1, xn), (2, yn))]

    def start(ins, outs, sems):
        copy, mine, me, sib, xn, yn, _ = make(ins, outs, sems)
        for cp in mine + first_copies(copy, me, sib, xn, yn, ins):
            cp.start()

    def middle(ins, outs, sems):
        copy, _, me, sib, xn, yn, _ = make(ins, outs, sems)
        for w in range(n):
            copy(w, 1, xn, me).wait_recv()
            copy(w, 3, xn, yn, half=0).start()
            copy(w, 5, xn, sib).start()
        for w in range(n):
            copy(w, 2, yn, me).wait_recv()
            copy(w, 4, yn, xn, half=1).start()
            copy(w, 6, yn, sib).start()

    def finish(ins, outs, sems):
        copy, mine, me, sib, xn, yn, dg = make(ins, outs, sems)
        last = []
        for w in range(n):
            copy(w, 3, dg, me, half=0).wait_recv()
            copy(w, 4, dg, me, half=1).wait_recv()
            fwd = copy(w, 7, dg, sib)
            fwd.start()
            last.append(fwd)
        sx, sy, sd = (1 - me[0], me[1], 1 - me[2]), (me[0], 1 - me[1], 1 - me[2]), (1 - me[0], 1 - me[1], 1 - me[2])
        for w in range(n):
            copy(w, 0, sib, me).wait_recv()
            copy(w, 5, sx, me).wait_recv()
            copy(w, 6, sy, me).wait_recv()
            copy(w, 7, sd, me).wait_recv()
        for cp in first_copies(copy, me, sib, xn, yn, ins) + last:
            cp.wait_send()
        for w in range(n):
            copy(w, 3, xn, yn, half=0).wait_send()
            copy(w, 5, xn, sib).wait_send()
            copy(w, 4, yn, xn, half=1).wait_send()
            copy(w, 6, yn, sib).wait_send()
        for cp in mine:
            cp.wait()

    return _Plan(shards, [jax.ShapeDtypeStruct((N_DEV,) + s.shape, s.dtype) for s in shards],
                 [pltpu.SemaphoreType.DMA((nk * n,)), pltpu.SemaphoreType.DMA((nk * n,)), pltpu.SemaphoreType.DMA((n,))],
                 start, finish, middle)


def _swap_plan(copies_of, n_copies, ins, out_shapes):
    def cps(in_refs, out_refs, sems):
        return copies_of(in_refs, out_refs, sems[0], sems[1])

    def start(in_refs, out_refs, sems):
        for cp in cps(in_refs, out_refs, sems):
            cp.start()

    def finish(in_refs, out_refs, sems):
        all_cps = cps(in_refs, out_refs, sems)
        for cp in all_cps:
            cp.wait_recv()
        for cp in all_cps:
            cp.wait_send()

    return _Plan(ins, out_shapes, [pltpu.SemaphoreType.DMA((n_copies,)), pltpu.SemaphoreType.DMA((n_copies,))],
                 start, finish)


def _sibling_plan(grads):
    n = len(grads)

    def copies(ins, outs, send, recv):
        x, y, c, chips = _place()
        owners = [(x, y)] + chips
        return [pltpu.make_async_remote_copy(
            src_ref=ins[w].at[_dev(*chip, 1 - c)], dst_ref=outs[w].at[k], send_sem=send.at[w * 4 + k],
            recv_sem=recv.at[w * 4 + k], device_id=(x, y, 1 - c), device_id_type=MESH)
            for w in range(n) for k, chip in enumerate(owners)]

    return _swap_plan(copies, 4 * n, grads, [jax.ShapeDtypeStruct((4,) + g.shape[1:], g.dtype) for g in grads])


def _chip_plan(parts, js=(0, 1, 2)):
    n, nj = len(parts), len(js)

    def copies(ins, outs, send, recv):
        x, y, c, chips = _place()
        return [pltpu.make_async_remote_copy(
            src_ref=ins[w].at[j], dst_ref=outs[w * nj + k], send_sem=send.at[w * nj + k],
            recv_sem=recv.at[w * nj + k], device_id=(*chips[j], c), device_id_type=MESH)
            for w in range(n) for k, j in enumerate(js)]

    return _swap_plan(copies, n * nj, parts,
                      [jax.ShapeDtypeStruct(p.shape[1:], p.dtype) for p in parts for _ in js])


UPDATE_TILE_BYTES = 1536 * 1024


def _row_tile(r, c):
    best = 8
    for t in range(8, r + 1, 8):
        if r % t == 0 and t * c * 4 <= UPDATE_TILE_BYTES:
            best = t
    return best


def _chip_partial(g, sib, ids, *, name):
    _, r, c = g.shape
    tr = _row_tile(r, c)

    def body(ids_ref, g_ref, s_ref, o_ref):
        o_ref[...] = (g_ref[...] + s_ref[...]).astype(BF16)

    return pl.pallas_call(
        body, name=name,
        grid_spec=pltpu.PrefetchScalarGridSpec(
            num_scalar_prefetch=1, grid=(3, r // tr),
            in_specs=[pl.BlockSpec((None, tr, c), lambda j, i, ids_ref: (ids_ref[j], i, 0)),
                      pl.BlockSpec((None, tr, c), lambda j, i, ids_ref: (j + 1, i, 0))],
            out_specs=pl.BlockSpec((None, tr, c), lambda j, i, ids_ref: (j, i, 0))),
        out_shape=jax.ShapeDtypeStruct((3, r, c), BF16),
        compiler_params=_cparams(("parallel", "parallel")),
    )(ids, g, sib)


def _adamw_math(w, g, m, v):
    m = ADAM_B1 * m + (1.0 - ADAM_B1) * g
    v = ADAM_B2 * v + (1.0 - ADAM_B2) * (g * g)
    m_hat = m / (1.0 - ADAM_B1 ** ADAM_STEP)
    v_hat = v / (1.0 - ADAM_B2 ** ADAM_STEP)
    delta = -ADAM_LR * (m_hat / (jnp.sqrt(v_hat) + ADAM_EPS) + ADAM_WD * w)
    return delta, m, v


def _shard_update(g, sib, rem, me, w, m, v, *, name):
    r, c = w.shape
    tr = _row_tile(r, c)

    def body(me_ref, g_ref, s_ref, r0_ref, r1_ref, r2_ref, w_ref, m_ref, v_ref, go_ref, d_ref, mo_ref, vo_ref):
        gt = g_ref[...] + s_ref[...]
        gt = gt + r0_ref[...].astype(F32)
        gt = gt + r1_ref[...].astype(F32)
        gt = gt + r2_ref[...].astype(F32)
        go_ref[...] = gt
        d, mn, vn = _adamw_math(w_ref[...], gt, m_ref[...], v_ref[...])
        d_ref[...] = d
        mo_ref[...] = mn
        vo_ref[...] = vn

    blk = lambda k: pl.BlockSpec((None, tr, c), lambda i, me_ref: (k, i, 0))
    plain = pl.BlockSpec((tr, c), lambda i, me_ref: (i, 0))
    return pl.pallas_call(
        body, name=name,
        grid_spec=pltpu.PrefetchScalarGridSpec(
            num_scalar_prefetch=1, grid=(r // tr,),
            in_specs=[pl.BlockSpec((None, tr, c), lambda i, me_ref: (me_ref[0], i, 0)), blk(0), plain, plain, plain,
                      plain, plain, plain],
            out_specs=[plain] * 4),
        out_shape=[jax.ShapeDtypeStruct((r, c), F32)] * 4,
        compiler_params=_cparams(("parallel",)),
    )(me, g, sib, *rem, w, m, v)


def _small_update(gathered, w, m, v, *, name):
    _, r, c = gathered.shape

    def body(g_ref, w_ref, m_ref, v_ref, go_ref, d_ref, mo_ref, vo_ref):
        gt = g_ref[0]
        for k in range(1, N_DEV):
            gt = gt + g_ref[k]
        go_ref[...] = gt
        d, mn, vn = _adamw_math(w_ref[...], gt, m_ref[...], v_ref[...])
        d_ref[...] = d
        mo_ref[...] = mn
        vo_ref[...] = vn

    return pl.pallas_call(
        body, name=name, out_shape=[jax.ShapeDtypeStruct((r, c), F32)] * 4,
        compiler_params=pltpu.CompilerParams(vmem_limit_bytes=VMEM_LIMIT),
    )(gathered, w, m, v)


SMALL_UNIT = 1024


def _pack(parts):
    flat = []
    for p in parts:
        f = p.reshape(-1).astype(F32)
        pad = (-f.shape[0]) % SMALL_UNIT
        flat.append(jnp.pad(f, (0, pad)))
    return jnp.concatenate(flat).reshape(-1, 128)


def _unpack(buf, shapes):
    flat = buf.reshape(-1)
    out, off = [], 0
    for s in shapes:
        nel = math.prod(s)
        out.append(flat[off:off + nel].reshape(s))
        off += nel + ((-nel) % SMALL_UNIT)
    return out


def kernel(x, p, ffn1_w_in, ffn1_w_out, ln1_g, ln1_b, mix_w_in, conv_w, conv_b, conv_w_out, ssm_lam_re, ssm_lam_im, ssm_log_step, ssm_b_re, ssm_b_im, ssm_c_re, ssm_c_im, ssm_d, ssm_w_glu, mix_w_out, ln2_g, ln2_b, ffn2_w_in, ffn2_w_out, ln3_g, ln3_b, ple_w_in, ple_w_gate, ln4_g, ln4_b, loss_target, m_ffn1_w_in, m_ffn1_w_out, m_ln1_g, m_ln1_b, m_mix_w_in, m_conv_w, m_conv_b, m_conv_w_out, m_ssm_lam_re, m_ssm_lam_im, m_ssm_log_step, m_ssm_b_re, m_ssm_b_im, m_ssm_c_re, m_ssm_c_im, m_ssm_d, m_ssm_w_glu, m_mix_w_out, m_ln2_g, m_ln2_b, m_ffn2_w_in, m_ffn2_w_out, m_ln3_g, m_ln3_b, m_ple_w_in, m_ple_w_gate, m_ln4_g, m_ln4_b, v_ffn1_w_in, v_ffn1_w_out, v_ln1_g, v_ln1_b, v_mix_w_in, v_conv_w, v_conv_b, v_conv_w_out, v_ssm_lam_re, v_ssm_lam_im, v_ssm_log_step, v_ssm_b_re, v_ssm_b_im, v_ssm_c_re, v_ssm_c_im, v_ssm_d, v_ssm_w_glu, v_mix_w_out, v_ln2_g, v_ln2_b, v_ffn2_w_in, v_ffn2_w_out, v_ln3_g, v_ln3_b, v_ple_w_in, v_ple_w_gate, v_ln4_g, v_ln4_b):
    args = dict(locals())
    big = ['ffn1_w_in', 'ffn1_w_out', 'mix_w_in', 'conv_w_out', 'ssm_w_glu', 'mix_w_out',
           'ffn2_w_in', 'ffn2_w_out', 'ple_w_in', 'ple_w_gate']
    small = ['ln1_g', 'ln1_b', 'conv_b', 'ssm_lam_re', 'ssm_lam_im', 'ssm_log_step', 'ssm_b_re', 'ssm_b_im',
             'ssm_c_re', 'ssm_c_im', 'ssm_d', 'ln2_g', 'ln2_b', 'ln3_g', 'ln3_b', 'ln4_g', 'ln4_b']
    order = ['ffn1_w_in', 'ffn1_w_out', 'ln1_g', 'ln1_b', 'mix_w_in', 'conv_w', 'conv_b', 'conv_w_out',
             'ssm_lam_re', 'ssm_lam_im', 'ssm_log_step', 'ssm_b_re', 'ssm_b_im', 'ssm_c_re', 'ssm_c_im', 'ssm_d',
             'ssm_w_glu', 'mix_w_out', 'ln2_g', 'ln2_b', 'ffn2_w_in', 'ffn2_w_out', 'ln3_g', 'ln3_b',
             'ple_w_in', 'ple_w_gate', 'ln4_g', 'ln4_b']

    t = x.shape[1]
    d = x.shape[2]
    xc_, yc_, cc_ = lax.axis_index("x"), lax.axis_index("y"), lax.axis_index("c")
    me = (4 * xc_ + 2 * yc_ + cc_).astype(jnp.int32)
    cw_cols = conv_w.shape[2]

    turned = ('ffn1_w_in', 'ffn2_w_in')

    def local(a, nm):
        return jnp.swapaxes(a[0], 0, 1) if nm in turned else a[0]

    shard = {nm: local(args[nm], nm).astype(BF16) for nm in big}
    cw_pad = jnp.zeros((16, 128), F32).at[0:3, 0:cw_cols].set(conv_w[0])
    wf = shard['ffn1_w_in'].shape[0]

    def gather(arrays, tag):
        ((_, got, _),) = _run_plans_on_sequencer(
            [_gather_plan(arrays)], lambda x, y, c, chips: [(x, y, 1 - c), (1 - x, y, c), (x, 1 - y, c)],
            name=f"gather_{tag}", collective_id=2)
        return got

    w1in, cw_g = gather([shard['ffn1_w_in'], cw_pad], "ffn1_in")
    (w1out_g,) = gather([shard['ffn1_w_out']], "ffn1_out")
    (wmix,) = gather([shard['mix_w_in']], "mix_in")
    wco, wglu, wmo_g = gather([shard[nm] for nm in ('conv_w_out', 'ssm_w_glu', 'mix_w_out')], "mix_rest")
    (w2in,) = gather([shard['ffn2_w_in']], "ffn2_in")
    (w2out_g,) = gather([shard['ffn2_w_out']], "ffn2_out")
    wpin, wgate_g = gather([shard['ple_w_in'], shard['ple_w_gate']], "ple")
    cw_full = jnp.transpose(cw_g[:, 0:3, 0:cw_cols], (1, 0, 2)).reshape(3, N_DEV * cw_cols)
    cw8 = jnp.zeros((8, CONV_CH), F32).at[0:3, :].set(cw_full)

    s5_in = (ssm_lam_re[0], ssm_lam_im[0], ssm_log_step[0], ssm_b_re[0], ssm_b_im[0])
    (a_re, a_im, bb_re, bb_im), s5_vjp = jax.vjp(_s5_discretise, *s5_in)
    tab_f = _pow_table(a_re, a_im)
    tab_b = _pow_table(a_re, -a_im, descending=True)
    bmat_b = _compact(jnp.transpose(bb_re, (0, 2, 1)), jnp.transpose(bb_im, (0, 2, 1))).astype(BF16)
    cmat_tb = _compact(ssm_c_re[0], -ssm_c_im[0]).astype(BF16)
    dvec = ssm_d[0].reshape(1, SSM_W)

    xf = x[0]
    x_b = xf.astype(BF16)
    p_b = p[0, 0].astype(BF16)
    tgt = loss_target[0]
    tq = min(512, t)

    ffn_out = dict(ja='c', jb='c', nj=4, tm=tq, tn=d, tk=wf)
    def side_by_side(wb):
        return jnp.transpose(wb, (1, 0, 2)).reshape(wb.shape[1], N_DEV * wb.shape[2])

    tf = min(1024, t)
    a1, h1, _ = _ffn_in(x_b, w1in, name="ffn1_in", tm=tf)
    w1out = w1out_g.reshape(4, wf, d)
    r1, x1, x1b = _mm(a1, w1out, name="ffn1_out", **ffn_out, epilogue=_ln_epilogue(xf, ln1_g, ln1_b, 0.5))
    proj = _mm(x1b, wmix, name="mix_in", jb='b', jo='b', o_flat=True, nj=8, tm=tq, tn=512, tk=d)
    wmo = wmo_g.reshape(d, d)
    ycin = _conv_fwd(proj, cw8, conv_b, name="conv_fwd")
    wco, wglu = side_by_side(wco), side_by_side(wglu)
    yconv = _mm(ycin, wco, name="conv_out", tm=tq, tn=d, tk=CONV_CH)
    s_f, s_b, _ = _scan_fwd(proj, bmat_b, tab_f, name="scan_fwd")
    blk = dict(ja='b', jb='b', jo='b', nj=SCAN_NCB)
    ymm = _mm(s_b, cmat_tb, name="ssm_read", a_flat=True, o_flat=True, tb=True, tm=tq, tn=SCAN_UW, tk=2 * SCAN_CB, **blk)
    ys, sg, u_b = _s5_out(ymm, proj, dvec, name="ssm_out")
    glu = _mm(sg, wglu, name="glu_in", tm=tq, tn=d, tk=SSM_W)
    merged = _gate_fwd(yconv, glu, proj, name="gate_fwd")
    r2, x2, x2b = _mm(merged, wmo, name="mix_out", tm=tq, tn=d, tk=d, epilogue=_ln_epilogue(x1, ln2_g, ln2_b, 1.0))
    a2, h2, _ = _ffn_in(x2b, w2in, name="ffn2_in", tm=tf)
    w2out = w2out_g.reshape(4, wf, d)
    r3, x3, x3b = _mm(a2, w2out, name="ffn2_out", **ffn_out, epilogue=_ln_epilogue(x2, ln3_g, ln3_b, 0.5))
    wgate = wgate_g.reshape(d, d)
    pe = _mm(p_b, side_by_side(wpin), name="ple_in", tm=tq, tn=d, tk=p_b.shape[1])
    gp = _mm(x3b, wgate, name="ple_gate", tm=tq, tn=d, tk=d)

    dr4, dpe_b, dgp_b, dg4, db4, loss_part = _ln_bwd(x3, [pe, gp], ln4_g, ln4_b, [], name="ple_ln_bwd",
                                                     ple=True, target=tgt)
    gb, sib, rem = {}, {}, {}
    ids = jnp.stack([_dev(1 - xc_, yc_, cc_), _dev(xc_, 1 - yc_, cc_), _dev(1 - xc_, 1 - yc_, cc_)]).astype(jnp.int32)

    def blocked(nm, g):
        return g.reshape((N_DEV,) + args[nm].shape[1:])

    def to_sibling(*names):
        return _sibling_plan([gb[nm] for nm in names])

    def chip_sums(names, sibs):
        for nm, s in zip(names, sibs):
            sib[nm] = s
        return [_chip_partial(gb[nm], sib[nm], ids, name=f"chip_sum_{nm}") for nm in names]

    def on_sequencer(plan, peers, tag, cid):
        ((_, got, _),) = _run_plans_on_sequencer([plan], peers, name=tag, collective_id=cid)
        return got

    waiting = []

    def send_sibling(*names):
        got = on_sequencer(to_sibling(*names), lambda x, y, c, chips: [(x, y, 1 - c)], f"grad_sibling_{names[0]}", 3)
        waiting.append((names, got))

    def send_chips(count=None):
        for _ in range(len(waiting) if count is None else count):
            names, got = waiting.pop(0)
            r_ = on_sequencer(_chip_plan(chip_sums(names, got)), lambda x, y, c, chips: [(*chip, c) for chip in chips],
                              f"grad_chips_{names[0]}", 1)
            for i, nm in enumerate(names):
                rem[nm] = r_[3 * i:3 * i + 3]

    ffn_in_dg = dict(ja='c', jb='c', nj=8, tm=tq, tn=d, tk=wf)
    ffn_in_wg = dict(ja='b', jo='b', ta=True, nj=8, tm=wf, tn=d, tk=t)
    ffn_out_wg = dict(ja='b', jo='b', ta=True, nj=4, tm=wf, tn=d, tk=t)

    gb['ple_w_in'] = _mm(p_b, dpe_b, name="ple_in_wg", jb='b', jo='b', b_flat=True, ta=True, nj=8,
                         tm=p_b.shape[1], tn=128, tk=t)
    gb['ple_w_gate'] = blocked('ple_w_gate', _mm(x3b, dgp_b, name="ple_gate_wg", ta=True, tm=d, tn=d, tk=t))
    g_ple = ('ple_w_in', 'ple_w_gate')
    dx3_g, (s_,) = _mm(dgp_b, wgate, name="ple_gate_dg", tb=True, tm=tq, tn=d, tk=d, plans=[to_sibling(*g_ple)])
    waiting.append((g_ple, s_))

    dr3, df2_b, dg3, db3 = _ln_bwd(r3, [], ln3_g, ln3_b, [(dr4, ALPHA), (dx3_g, 1.0)], name="ffn2_ln_bwd", fs=0.5)
    send_chips()
    dh2, _ = _ffn_out_dg(df2_b, w2out, h2, name="ffn2_out_dg", tm=tf)
    gb['ffn2_w_out'] = blocked('ffn2_w_out', _mm(a2, df2_b, name="ffn2_out_wg", **ffn_out_wg))
    send_sibling('ffn2_w_out')
    dx2_f = _mm(dh2, w2in, name="ffn2_in_dg", **ffn_in_dg)
    send_chips()
    gb['ffn2_w_in'] = _mm(dh2, x2b, name="ffn2_in_wg", **ffn_in_wg)

    dr2, dmix_b, dg2, db2 = _ln_bwd(r2, [], ln2_g, ln2_b, [(dr3, ALPHA), (dx2_f, 1.0)], name="mix_ln_bwd")
    send_sibling('ffn2_w_in')
    dmerged = _mm(dmix_b, wmo, name="mix_out_dg", tb=True, tm=tq, tn=d, tk=d)
    gb['mix_w_out'] = blocked('mix_w_out', _mm(merged, dmix_b, name="mix_out_wg", ta=True, tm=d, tn=d, tk=t))
    dyconv_b, dglu_b, dgate_b = _gate_bwd(dmerged, yconv, glu, proj, name="gate_bwd")
    send_chips()
    gb['conv_w_out'] = _mm(ycin, dyconv_b, name="conv_out_wg", jb='b', jo='b', b_flat=True, ta=True, nj=8,
                           tm=CONV_CH, tn=128, tk=t)
    dycin = _mm(dyconv_b, wco, name="conv_out_dg", tb=True, tm=tq, tn=CONV_CH, tk=d)
    gb['ssm_w_glu'] = _mm(sg, dglu_b, name="glu_in_wg", jb='b', jo='b', b_flat=True, ta=True, nj=8,
                          tm=SSM_W, tn=256, tk=t)
    dsg = _mm(dglu_b, wglu, name="glu_in_dg", tb=True, tm=tq, tn=SSM_W, tk=2 * d)
    dys, dys_b, dd = _s5_bwd_in(dsg, ys, proj, name="ssm_out_bwd")
    h_b, da_acc, _ = _scan_bwd(dys_b, cmat_tb, s_f, tab_b, name="scan_bwd")
    send_sibling('mix_w_out', 'conv_w_out', 'ssm_w_glu')
    dumm = _mm(h_b, bmat_b, name="ssm_write_dg", a_flat=True, o_flat=True, tb=True, tm=tq, tn=SCAN_UW,
               tk=2 * SCAN_CB, **blk)
    du_b = _s5_du(dumm, dys, dvec, name="ssm_du")
    g_bmat = _mm(u_b, h_b, name="ssm_write_wg", a_flat=True, b_flat=True, ta=True, tm=SCAN_UW,
                 tn=2 * SCAN_CB, tk=t, **blk)
    send_chips()
    g_cmat = _mm(dys_b, s_b, name="ssm_read_wg", a_flat=True, b_flat=True, ta=True, tm=SCAN_UW,
                 tn=2 * SCAN_CB, tk=t, **blk)
    dcb_b, dcc_b, dch_b, dconv = _conv_bwd(dycin, proj, cw8, conv_b, name="conv_bwd")
    dproj = jnp.concatenate([dcb_b, dcc_b, dch_b, du_b, dgate_b], axis=1)
    gb['mix_w_in'] = _mm(x1b, dproj, name="mix_in_wg", jb='b', jo='b', b_flat=True, ta=True, nj=8,
                         tm=d, tn=512, tk=t)
    send_sibling('mix_w_in')
    dx1_m = _mm(dproj, wmix, name="mix_in_dg", ja='c', jb='c', a_flat=True, tb=True, nj=8, tm=tq, tn=d, tk=512)
    send_chips()

    dr1, df1_b, dg1, db1 = _ln_bwd(r1, [], ln1_g, ln1_b, [(dr2, ALPHA), (dx1_m, 1.0)], name="ffn1_ln_bwd", fs=0.5)
    dh1, _ = _ffn_out_dg(df1_b, w1out, h1, name="ffn1_out_dg", tm=tf)

    da_sum = jnp.sum(da_acc, axis=0)
    da_re, da_im = _unperm_cols(da_sum)
    gbb_re, gbb_im = [jnp.transpose(v, (0, 2, 1)) for v in _compact_extract(g_bmat, SSM_GROUP)]
    g_c_re, g_c_im_neg = _compact_extract(g_cmat, SSM_GROUP)
    g_c_im = -g_c_im_neg
    g_lam_re, g_lam_im, g_log_step, g_b_re, g_b_im = s5_vjp(
        (da_re.reshape(SSM_GROUPS, SSM_STATE), da_im.reshape(SSM_GROUPS, SSM_STATE), gbb_re, gbb_im))
    g_d = dd.reshape(SSM_GROUPS, SSM_GROUP)

    small_g = {'ln1_g': dg1, 'ln1_b': db1, 'conv_b': dconv[3:4], 'ssm_lam_re': g_lam_re, 'ssm_lam_im': g_lam_im,
               'ssm_log_step': g_log_step, 'ssm_b_re': g_b_re, 'ssm_b_im': g_b_im, 'ssm_c_re': g_c_re,
               'ssm_c_im': g_c_im, 'ssm_d': g_d, 'ln2_g': dg2, 'ln2_b': db2, 'ln3_g': dg3, 'ln3_b': db3,
               'ln4_g': dg4, 'ln4_b': db4}
    small_shapes = [args[nm].shape for nm in small] + [(3, CONV_CH), (1,)]
    g_pack = _pack([small_g[nm] for nm in small] + [dconv[0:3], loss_part[0:1, 0:1]])

    res = {}
    me1 = me.reshape(1)

    def update(nm):
        upd = _shard_update(gb[nm], sib[nm], rem[nm], me1, local(args[nm], nm), local(args['m_' + nm], nm),
                            local(args['v_' + nm], nm), name=f"update_{nm}")
        for key, val in zip(('grad_', 'delta_', 'new_m_', 'new_v_'), upd):
            res[key + nm] = (jnp.swapaxes(val, 0, 1) if nm in turned else val)[None]

    (g_all,) = gather([g_pack], "small")
    gb['ffn1_w_in'] = _mm(dh1, x_b, name="ffn1_in_wg", **ffn_in_wg)
    send_sibling('ffn1_w_in')
    gb['ffn1_w_out'] = blocked('ffn1_w_out', _mm(a1, df1_b, name="ffn1_out_wg", **ffn_out_wg))
    send_sibling('ffn1_w_out')
    send_chips(1)
    (grad_x,) = _mm(dh1, w1in, name="ffn1_in_dg", **ffn_in_dg,
                    epilogue=(lambda pr, drv: (pr + ALPHA * drv,), (dr1,), (), (F32,)))
    last = ['ffn1_w_in', 'ffn1_w_out']
    for nm in big:
        if nm not in last:
            update(nm)
    send_chips()

    def full_cw(a):
        return lax.dynamic_update_slice(jnp.zeros((3, CONV_CH), F32), a[0], (0, me * cw_cols))

    zero1 = jnp.zeros((1,), F32)
    w_pack = _pack([args[nm] for nm in small] + [full_cw(conv_w), zero1])
    m_pack = _pack([args['m_' + nm] for nm in small] + [full_cw(m_conv_w), zero1])
    v_pack = _pack([args['v_' + nm] for nm in small] + [full_cw(v_conv_w), zero1])
    sg_sum, sd, sm, sv = _small_update(g_all, w_pack, m_pack, v_pack, name="small_update")
    for key, buf in (('grad_', sg_sum), ('delta_', sd), ('new_m_', sm), ('new_v_', sv)):
        parts = _unpack(buf, small_shapes)
        for nm, val in zip(small, parts[:len(small)]):
            res[key + nm] = val
        res[key + 'conv_w'] = lax.dynamic_slice(parts[len(small)], (0, me * cw_cols), (3, cw_cols))[None]
        if key == 'grad_':
            loss = parts[-1][0]

    for nm in last:
        update(nm)

    outs = [loss, grad_x[None]]
    for key in ('grad_', 'delta_', 'new_m_', 'new_v_'):
        outs += [res[key + nm] for nm in order]
    return tuple(outs)
```

```python
import functools
import math

import jax
import jax.numpy as jnp
from jax import lax
from jax.experimental import pallas as pl
from jax.experimental.pallas import tpu as pltpu
from jax.experimental.pallas import tpu_sc as plsc

F32 = jnp.float32
BF16 = jnp.bfloat16
MESH = pl.DeviceIdType.MESH

N_DEV = 8
ALPHA = 2.0 ** 0.25
LN_EPS = 1e-5
CONV_CH = 512
SSM_W = 512
SSM_GROUPS = 32
SSM_GROUP = 16
SSM_STATE = 64
SSM_CH = SSM_GROUPS * SSM_STATE
SCAN_CB = 512
SCAN_NCB = SSM_CH // SCAN_CB
SCAN_GPB = SSM_GROUPS // SCAN_NCB
SCAN_UW = SCAN_GPB * SSM_GROUP
SCAN_TT = 256
ADAM_LR = 0.001
ADAM_B1 = 0.9
ADAM_B2 = 0.999
ADAM_EPS = 1e-08
ADAM_WD = 0.01
ADAM_STEP = 10
VMEM_LIMIT = 56 * 1024 * 1024


def _cparams(sem=None, **kw):
    return pltpu.CompilerParams(dimension_semantics=sem, vmem_limit_bytes=VMEM_LIMIT, **kw)


def _mm(a, b, *, name, ja=None, jb=None, jo=None, a_flat=False, b_flat=False, o_flat=False,
        ta=False, tb=False, tm, tn, tk, nj=1, out_dtype=F32, plans=(), epilogue=None):
    def dims(arr, j, flat):
        if j is None:
            return arr.shape
        if flat:
            return (arr.shape[0], arr.shape[1] // nj)
        assert arr.shape[0] == nj, (name, arr.shape, nj)
        return arr.shape[1:]

    ar, ac = dims(a, ja, a_flat)
    br, bc = dims(b, jb, b_flat)
    m, k = (ac, ar) if ta else (ar, ac)
    k2, n = (bc, br) if tb else (br, bc)
    assert k == k2, (name, a.shape, b.shape)
    assert m % tm == 0 and n % tn == 0 and k % tk == 0, (name, m, n, k, tm, tn, tk)
    njb = nj if 'b' in (ja, jb) else 1
    njc = nj if 'c' in (ja, jb) else 1
    nk = k // tk
    j_inside = njc > 1 and nk == 1 and not ta
    n_in = njc if j_inside else 1
    nred = nk if j_inside else njc * nk
    grid = (njb, m // tm, n // tn, 1 if j_inside else njc, nk)

    def make_spec(j, flat, blk, rfn, cfn, cols_per_j):
        def jsel(g, c):
            return g if j == 'b' else c
        if j is None:
            return pl.BlockSpec(blk, lambda g, i, jn, c, kk: (rfn(i, jn, kk), cfn(i, jn, kk)))
        if j == 'c' and j_inside:
            if flat:
                return pl.BlockSpec((blk[0], nj * blk[1]), lambda g, i, jn, c, kk: (rfn(i, jn, kk), 0))
            return pl.BlockSpec((nj,) + blk, lambda g, i, jn, c, kk: (0, rfn(i, jn, kk), cfn(i, jn, kk)))
        if flat:
            nb = cols_per_j // blk[1]
            return pl.BlockSpec(blk, lambda g, i, jn, c, kk: (rfn(i, jn, kk), jsel(g, c) * nb + cfn(i, jn, kk)))
        return pl.BlockSpec((None,) + blk,
                            lambda g, i, jn, c, kk: (jsel(g, c), rfn(i, jn, kk), cfn(i, jn, kk)))

    if ta:
        a_spec = make_spec(ja, a_flat, (tk, tm), lambda i, jn, kk: kk, lambda i, jn, kk: i, ac)
    else:
        a_spec = make_spec(ja, a_flat, (tm, tk), lambda i, jn, kk: i, lambda i, jn, kk: kk, ac)
    if tb:
        b_spec = make_spec(jb, b_flat, (tn, tk), lambda i, jn, kk: jn, lambda i, jn, kk: kk, bc)
    else:
        b_spec = make_spec(jb, b_flat, (tk, tn), lambda i, jn, kk: kk, lambda i, jn, kk: jn, bc)
    o_spec = make_spec(jo, o_flat, (tm, tn), lambda i, jn, kk: i, lambda i, jn, kk: jn, n)
    if jo is None:
        out_shape = (m, n)
    elif o_flat:
        out_shape = (m, nj * n)
    else:
        out_shape = (nj, m, n)

    dn = (((0 if ta else 1,), (1 if tb else 0,)), ((), ()))

    def operand(ref, j, flat, jj, width):
        if not (j == 'c' and j_inside):
            return ref[...]
        return ref[:, jj * width:(jj + 1) * width] if flat else ref[jj]

    e_fn, e_rows, e_vecs, e_dtypes = epilogue if epilogue else (None, (), (), (out_dtype,))
    assert not epilogue or (nred == 1 and jo is None), name
    n_e = len(e_rows) + len(e_vecs)
    n_o = len(e_dtypes)

    def body(a_ref, b_ref, *rest):
        e_refs, o_refs, scratch = rest[:n_e], rest[n_e:n_e + n_o], rest[n_e + n_o:]
        o_ref = o_refs[0]
        p = None
        for jj in range(n_in):
            q = lax.dot_general(operand(a_ref, ja, a_flat, jj, tk), operand(b_ref, jb, b_flat, jj, tk if tb else tn),
                                dn, preferred_element_type=F32)
            p = q if p is None else p + q
        if epilogue:
            for ref, val in zip(o_refs, e_fn(p, *[r[...] for r in e_refs])):
                ref[...] = val.astype(ref.dtype)
        elif nred == 1:
            o_ref[...] = p.astype(o_ref.dtype)
        else:
            acc = scratch[0]
            r = pl.program_id(3) * nk + pl.program_id(4)

            @pl.when(r == 0)
            def _():
                acc[...] = p

            @pl.when(r > 0)
            def _():
                acc[...] += p

            @pl.when(r == nred - 1)
            def _():
                o_ref[...] = acc[...].astype(o_ref.dtype)

    vec_spec = pl.BlockSpec((1, tn), lambda g, i, jn, c, kk: (0, jn))
    res = _call_with_plans(
        body, plans, name=name, grid=grid,
        in_specs=[a_spec, b_spec] + [o_spec] * len(e_rows) + [vec_spec] * len(e_vecs), out_specs=[o_spec] * n_o,
        out_shape=[jax.ShapeDtypeStruct(out_shape, dt) for dt in e_dtypes],
        scratch_shapes=[] if nred == 1 else [pltpu.VMEM((tm, tn), F32)],
        semantics=("parallel", "parallel", "parallel", "arbitrary", "arbitrary"), operands=(a, b, *e_rows, *e_vecs))
    outs = res[0] if epilogue else res[0][0]
    return (outs, res[1]) if plans else outs


def _sigmoid(v):
    return jax.nn.sigmoid(v)


def _row_spec(tm, cols, colblk=0):
    return pl.BlockSpec((tm, cols), lambda i: (i, colblk))


def _vec_spec(cols):
    return pl.BlockSpec((1, cols), lambda i: (0, 0))


def _ffn_in(xb, win, *, name, tm, plans=()):
    t, d = xb.shape
    nj, w, _ = win.shape
    half = nj // 2
    dn = (((1,), (1,)), ((), ()))

    def body(x_ref, wg_ref, wu_ref, a_ref, gu_ref):
        xv = x_ref[...]
        g = lax.dot_general(xv, wg_ref[...], dn, preferred_element_type=F32)
        u = lax.dot_general(xv, wu_ref[...], dn, preferred_element_type=F32)
        a_ref[...] = (g * _sigmoid(g) * u).astype(BF16)
        gu_ref[0] = g.astype(BF16)
        gu_ref[1] = u.astype(BF16)

    (a, gu), riders = _call_with_plans(
        body, plans, name=name, grid=(half, t // tm),
        in_specs=[pl.BlockSpec((tm, d), lambda j, i: (i, 0)),
                  pl.BlockSpec((None, w, d), lambda j, i: (j, 0, 0)),
                  pl.BlockSpec((None, w, d), lambda j, i: (j + half, 0, 0))],
        out_specs=[pl.BlockSpec((None, tm, w), lambda j, i: (j, i, 0)),
                   pl.BlockSpec((2, None, tm, w), lambda j, i: (0, j, i, 0))],
        out_shape=[jax.ShapeDtypeStruct((half, t, w), BF16), jax.ShapeDtypeStruct((2, half, t, w), BF16)],
        scratch_shapes=[], semantics=("parallel", "parallel"), operands=(xb, win, win))
    return a, gu, riders


def _ffn_out_dg(dfb, wout, gu, *, name, tm, plans=()):
    t, d = dfb.shape
    half, w, _ = wout.shape
    dn = (((1,), (1,)), ((), ()))

    def body(df_ref, w_ref, gu_ref, dh_ref):
        da = lax.dot_general(df_ref[...], w_ref[...], dn, preferred_element_type=F32)
        g = gu_ref[0].astype(F32)
        u = gu_ref[1].astype(F32)
        sg = _sigmoid(g)
        dh_ref[0] = (da * u * (sg * (1.0 + g * (1.0 - sg)))).astype(BF16)
        dh_ref[1] = (da * (g * sg)).astype(BF16)

    (out,), riders = _call_with_plans(
        body, plans, name=name, grid=(half, t // tm),
        in_specs=[pl.BlockSpec((tm, d), lambda j, i: (i, 0)),
                  pl.BlockSpec((None, w, d), lambda j, i: (j, 0, 0)),
                  pl.BlockSpec((2, None, tm, w), lambda j, i: (0, j, i, 0))],
        out_specs=[pl.BlockSpec((2, None, tm, w), lambda j, i: (0, j, i, 0))],
        out_shape=[jax.ShapeDtypeStruct((2, half, t, w), BF16)],
        scratch_shapes=[], semantics=("parallel", "parallel"), operands=(dfb, wout, gu))
    return out.reshape(2 * half, t, w), riders


def _ln_stats(r):
    mu = jnp.mean(r, axis=-1, keepdims=True)
    xc = r - mu
    var = jnp.mean(xc * xc, axis=-1, keepdims=True)
    rstd = lax.rsqrt(var + LN_EPS)
    return xc * rstd, rstd


def _ln_epilogue(xin, gamma, beta, fs):
    def fn(p, xv, g, b):
        r = ALPHA * xv + fs * p
        xh, _ = _ln_stats(r)
        y = xh * g + b
        return r, y, y

    return fn, (xin,), (gamma, beta), (F32, F32, BF16)


def _ln_bwd(xin, fparts, gamma, beta, grads, *, name, fs=1.0, ple=False, target=None, tm=512):
    t, d = xin.shape
    nf = len(fparts)
    ng = len(grads)
    coefs = [c for _, c in grads]
    use_t = target is not None
    n_fout = 2 if ple else 1

    def body(*refs):
        pos = 0
        x_ref = refs[pos]; pos += 1
        f_refs = refs[pos:pos + nf]; pos += nf
        g_ref, b_ref = refs[pos:pos + 2]; pos += 2
        gr_refs = refs[pos:pos + ng]; pos += ng
        if use_t:
            t_ref = refs[pos]; pos += 1
        dr_ref = refs[pos]; pos += 1
        fo_refs = refs[pos:pos + n_fout]; pos += n_fout
        dg_ref, db_ref = refs[pos:pos + 2]; pos += 2
        if use_t:
            loss_ref = refs[pos]; pos += 1
        i = pl.program_id(0)

        if ple:
            pe = f_refs[0][...]
            sg = _sigmoid(f_refs[1][...])
            resid = ALPHA * x_ref[...] + pe * sg
        else:
            resid = x_ref[...]
        xh, rstd = _ln_stats(resid)
        gam = g_ref[...]
        if use_t:
            diff = xh * gam + b_ref[...] - t_ref[...]
            dy = diff * (1.0 / d)
            lpart = 0.5 * jnp.sum(jnp.sum(diff * diff, axis=-1, keepdims=True), axis=0, keepdims=True) * (1.0 / d)
        else:
            dy = coefs[0] * gr_refs[0][...]
            for c, r in zip(coefs[1:], gr_refs[1:]):
                dy = dy + c * r[...]
        dxh = dy * gam
        m1 = jnp.mean(dxh, axis=-1, keepdims=True)
        m2 = jnp.mean(dxh * xh, axis=-1, keepdims=True)
        dr = rstd * (dxh - m1 - xh * m2)
        dr_ref[...] = dr
        if ple:
            fo_refs[0][...] = (dr * sg).astype(BF16)
            fo_refs[1][...] = (dr * pe * (sg * (1.0 - sg))).astype(BF16)
        else:
            fo_refs[0][...] = (fs * dr).astype(BF16)
        dgp = jnp.sum(dy * xh, axis=0, keepdims=True)
        dbp = jnp.sum(dy, axis=0, keepdims=True)

        @pl.when(i == 0)
        def _():
            dg_ref[...] = dgp
            db_ref[...] = dbp
            if use_t:
                loss_ref[...] = jnp.broadcast_to(lpart, loss_ref.shape)

        @pl.when(i > 0)
        def _():
            dg_ref[...] += dgp
            db_ref[...] += dbp
            if use_t:
                loss_ref[...] += jnp.broadcast_to(lpart, loss_ref.shape)

    ins = [xin, *fparts, gamma, beta, *[g for g, _ in grads]] + ([target] if use_t else [])
    in_specs = ([_row_spec(tm, d)] * (1 + nf) + [_vec_spec(d), _vec_spec(d)] + [_row_spec(tm, d)] * ng
                + ([_row_spec(tm, d)] if use_t else []))
    out_specs = [_row_spec(tm, d)] * (1 + n_fout) + [_vec_spec(d), _vec_spec(d)] + ([_vec_spec(128)] if use_t else [])
    out_shape = ([jax.ShapeDtypeStruct((t, d), F32)] + [jax.ShapeDtypeStruct((t, d), BF16)] * n_fout
                 + [jax.ShapeDtypeStruct((1, d), F32)] * 2 + ([jax.ShapeDtypeStruct((1, 128), F32)] if use_t else []))
    return pl.pallas_call(
        body, name=name, grid=(t // tm,), in_specs=in_specs, out_specs=out_specs, out_shape=out_shape,
        compiler_params=_cparams(("arbitrary",)),
    )(*ins)


def _conv_fwd(proj, cw, cb, *, name, tm=512):
    t = proj.shape[0]
    c = CONV_CH
    hb = tm // 8

    def body(b_ref, c_ref, h_ref, cp_ref, hp_ref, w_ref, bias_ref, o_ref, q_scr):
        i = pl.program_id(0)
        q = c_ref[...] * h_ref[...]
        halo = jnp.where(i > 0, cp_ref[...] * hp_ref[...], 0.0)
        q_scr[0:8, :] = halo
        q_scr[8:, :] = q
        z = (w_ref[2:3, :] * q + w_ref[1:2, :] * q_scr[pl.ds(7, tm), :] + w_ref[0:1, :] * q_scr[pl.ds(6, tm), :]
             + bias_ref[...])
        o_ref[...] = (b_ref[...] * z).astype(BF16)

    prev = lambda blk: pl.BlockSpec((8, c), lambda i: (jnp.maximum(i * hb - 1, 0), blk))
    return pl.pallas_call(
        body, name=name, grid=(t // tm,),
        in_specs=[_row_spec(tm, c, 0), _row_spec(tm, c, 1), _row_spec(tm, c, 2), prev(1), prev(2),
                  pl.BlockSpec((8, c), lambda i: (0, 0)), _vec_spec(c)],
        out_specs=_row_spec(tm, c),
        out_shape=jax.ShapeDtypeStruct((t, c), BF16),
        scratch_shapes=[pltpu.VMEM((tm + 8, c), F32)],
        compiler_params=_cparams(("parallel",)),
    )(proj, proj, proj, proj, proj, cw, cb)


def _conv_bwd(dyc, proj, cw, cb, *, name, tm=512):
    t = proj.shape[0]
    c = CONV_CH
    hb = tm // 8
    nblk = t // 8

    def body(d_ref, b_ref, c_ref, h_ref, cp_ref, hp_ref, dn_ref, bn_ref, w_ref, bias_ref,
             db_ref, dc_ref, dh_ref, dw_ref, q_scr, z_scr):
        i = pl.program_id(0)
        last = pl.num_programs(0) - 1
        cc = c_ref[...]
        ch = h_ref[...]
        q = cc * ch
        halo = jnp.where(i > 0, cp_ref[...] * hp_ref[...], 0.0)
        q_scr[0:8, :] = halo
        q_scr[8:, :] = q
        w0, w1, w2 = w_ref[0:1, :], w_ref[1:2, :], w_ref[2:3, :]
        qm1 = q_scr[pl.ds(7, tm), :]
        qm2 = q_scr[pl.ds(6, tm), :]
        z = w2 * q + w1 * qm1 + w0 * qm2 + bias_ref[...]
        d = d_ref[...]
        bb = b_ref[...]
        db_ref[...] = (d * z).astype(BF16)
        dz = d * bb
        z_scr[0:tm, :] = dz
        z_scr[tm:, :] = jnp.where(i < last, dn_ref[...] * bn_ref[...], 0.0)
        dq = w2 * dz + w1 * z_scr[pl.ds(1, tm), :] + w0 * z_scr[pl.ds(2, tm), :]
        dc_ref[...] = (dq * ch).astype(BF16)
        dh_ref[...] = (dq * cc).astype(BF16)
        row = lax.broadcasted_iota(jnp.int32, (8, c), 0)
        part = jnp.zeros((8, c), F32)
        for k, term in enumerate((dz * qm2, dz * qm1, dz * q, dz)):
            part = jnp.where(row == k, jnp.sum(term, axis=0, keepdims=True), part)

        @pl.when(i == 0)
        def _():
            dw_ref[...] = part

        @pl.when(i > 0)
        def _():
            dw_ref[...] += part

    prev = lambda blk: pl.BlockSpec((8, c), lambda i: (jnp.maximum(i * hb - 1, 0), blk))
    nxt_p = pl.BlockSpec((8, c), lambda i: (jnp.minimum((i + 1) * hb, nblk - 1), 0))
    nxt_d = pl.BlockSpec((8, c), lambda i: (jnp.minimum((i + 1) * hb, nblk - 1), 0))
    return pl.pallas_call(
        body, name=name, grid=(t // tm,),
        in_specs=[_row_spec(tm, c), _row_spec(tm, c, 0), _row_spec(tm, c, 1), _row_spec(tm, c, 2),
                  prev(1), prev(2), nxt_d, nxt_p, pl.BlockSpec((8, c), lambda i: (0, 0)), _vec_spec(c)],
        out_specs=[_row_spec(tm, c)] * 3 + [pl.BlockSpec((8, c), lambda i: (0, 0))],
        out_shape=[jax.ShapeDtypeStruct((t, c), BF16)] * 3 + [jax.ShapeDtypeStruct((8, c), F32)],
        scratch_shapes=[pltpu.VMEM((tm + 8, c), F32), pltpu.VMEM((tm + 8, c), F32)],
        compiler_params=_cparams(("arbitrary",)),
    )(dyc, proj, proj, proj, proj, proj, dyc, proj, cw, cb)


def _gate_fwd(yconv, glu, proj, *, name, tm=512):
    t, d = yconv.shape

    def body(yc_ref, ga_ref, gb_ref, gc_ref, gs_ref, o_ref):
        yssm = ga_ref[...] * _sigmoid(gb_ref[...])
        o_ref[...] = (_sigmoid(gc_ref[...]) * yc_ref[...] + _sigmoid(gs_ref[...]) * yssm).astype(BF16)

    return pl.pallas_call(
        body, name=name, grid=(t // tm,),
        in_specs=[_row_spec(tm, d), _row_spec(tm, d, 0), _row_spec(tm, d, 1), _row_spec(tm, d, 2), _row_spec(tm, d, 3)],
        out_specs=_row_spec(tm, d), out_shape=jax.ShapeDtypeStruct((t, d), BF16),
        compiler_params=_cparams(("parallel",)),
    )(yconv, glu, glu, proj, proj)


def _gate_bwd(dm, yconv, glu, proj, *, name, tm=512):
    t, d = yconv.shape

    def body(dm_ref, yc_ref, ga_ref, gb_ref, gc_ref, gs_ref, dyc_ref, dglu_ref, dgate_ref):
        dmv = dm_ref[...]
        sc = _sigmoid(gc_ref[...])
        ss = _sigmoid(gs_ref[...])
        sb = _sigmoid(gb_ref[...])
        ga = ga_ref[...]
        yssm = ga * sb
        dyc_ref[...] = (dmv * sc).astype(BF16)
        dgate_ref[:, 0:d] = (dmv * yc_ref[...] * (sc * (1.0 - sc))).astype(BF16)
        dys = dmv * ss
        dgate_ref[:, d:2 * d] = (dmv * yssm * (ss * (1.0 - ss))).astype(BF16)
        dglu_ref[:, 0:d] = (dys * sb).astype(BF16)
        dglu_ref[:, d:2 * d] = (dys * ga * (sb * (1.0 - sb))).astype(BF16)

    return pl.pallas_call(
        body, name=name, grid=(t // tm,),
        in_specs=[_row_spec(tm, d), _row_spec(tm, d), _row_spec(tm, d, 0), _row_spec(tm, d, 1),
                  _row_spec(tm, d, 2), _row_spec(tm, d, 3)],
        out_specs=[_row_spec(tm, d), _row_spec(tm, 2 * d), _row_spec(tm, 2 * d)],
        out_shape=[jax.ShapeDtypeStruct((t, d), BF16), jax.ShapeDtypeStruct((t, 2 * d), BF16),
                   jax.ShapeDtypeStruct((t, 2 * d), BF16)],
        compiler_params=_cparams(("parallel",)),
    )(dm, yconv, glu, glu, proj, proj)


_GELU_C = math.sqrt(2.0 / math.pi)


def _gelu(v):
    return 0.5 * v * (1.0 + jnp.tanh(_GELU_C * (v + 0.044715 * v * v * v)))


def _gelu_grad(v):
    th = jnp.tanh(_GELU_C * (v + 0.044715 * v * v * v))
    return 0.5 * (1.0 + th) + 0.5 * v * (1.0 - th * th) * (_GELU_C * (1.0 + 3.0 * 0.044715 * v * v))


def _cmul(ar, ai, br, bi):
    return ar * br - ai * bi, ar * bi + ai * br


def _scan_fwd(proj, bmat, tab, *, name, plans=()):
    t = proj.shape[0]
    tt, cbw = SCAN_TT, SCAN_CB
    w2 = 2 * cbw

    def body(u_ref, b_ref, tab_ref, s_ref, sb_ref, bu_scr, carry):
        ti = pl.program_id(1)

        @pl.when(ti == 0)
        def _():
            carry[...] = jnp.zeros_like(carry)

        bu_scr[...] = jnp.dot(u_ref[...].astype(BF16), b_ref[...], preferred_element_type=F32)
        row = lax.broadcasted_iota(jnp.int32, (8, cbw), 0)

        def blk(bi, c):
            cr, ci = c
            r0 = pl.multiple_of(bi * 8, 8)
            xr = bu_scr[pl.ds(r0, 8), 0:cbw]
            xi = bu_scr[pl.ds(r0, 8), cbw:w2]
            for k, sh in enumerate((1, 2, 4)):
                kr = tab_ref[k:k + 1, 0:cbw]
                ki = tab_ref[k:k + 1, cbw:w2]
                sr = jnp.where(row >= sh, pltpu.roll(xr, sh, 0), 0.0)
                si = jnp.where(row >= sh, pltpu.roll(xi, sh, 0), 0.0)
                pr, pi = _cmul(kr, ki, sr, si)
                xr = xr + pr
                xi = xi + pi
            pr, pi = _cmul(tab_ref[8:16, 0:cbw], tab_ref[8:16, cbw:w2], cr, ci)
            xr = xr + pr
            xi = xi + pi
            s_ref[pl.ds(r0, 8), 0:cbw] = xr
            s_ref[pl.ds(r0, 8), cbw:w2] = xi
            return (jnp.broadcast_to(xr[7:8, :], (8, cbw)), jnp.broadcast_to(xi[7:8, :], (8, cbw)))

        cr, ci = lax.fori_loop(0, tt // 8, blk, (carry[:, 0:cbw], carry[:, cbw:w2]))
        carry[:, 0:cbw] = cr
        carry[:, cbw:w2] = ci
        sb_ref[...] = s_ref[...].astype(BF16)

    (s, sb), riders = _call_with_plans(
        body, plans, name=name, grid=(SCAN_NCB, t // tt),
        in_specs=[pl.BlockSpec((tt, SCAN_UW), lambda cb, ti: (ti, 3 * SCAN_NCB + cb)),
                  pl.BlockSpec((None, SCAN_UW, w2), lambda cb, ti: (cb, 0, 0)),
                  pl.BlockSpec((16, w2), lambda cb, ti: (0, cb))],
        out_specs=[pl.BlockSpec((tt, w2), lambda cb, ti: (ti, cb))] * 2,
        out_shape=[jax.ShapeDtypeStruct((t, 2 * SSM_CH), F32), jax.ShapeDtypeStruct((t, 2 * SSM_CH), BF16)],
        scratch_shapes=[pltpu.VMEM((tt, w2), F32), pltpu.VMEM((8, w2), F32)],
        semantics=("parallel", "arbitrary"), operands=(proj, bmat, tab))
    return s, sb, riders


def _scan_bwd(dyb, cmat_t, s, tabb, *, name, plans=()):
    t = s.shape[0]
    tt, cbw = SCAN_TT, SCAN_CB
    w2 = 2 * cbw
    nt = t // tt
    hb = tt // 8

    def body(dy_ref, c_ref, s_ref, sp_ref, tab_ref, h_ref, da_ref, g_scr, s_scr, carry):
        ti = pl.program_id(1)

        @pl.when(ti == 0)
        def _():
            carry[...] = jnp.zeros_like(carry)
            da_ref[...] = jnp.zeros_like(da_ref)

        g_scr[...] = jnp.dot(dy_ref[...], c_ref[...], preferred_element_type=F32)
        s_scr[0:8, :] = jnp.where(ti < nt - 1, sp_ref[...], 0.0)
        s_scr[8:, :] = s_ref[...]
        row = lax.broadcasted_iota(jnp.int32, (8, cbw), 0)

        def blk(k, c):
            cr, ci, ar, ai = c
            bi = hb - 1 - k
            r0 = pl.multiple_of(bi * 8, 8)
            xr = g_scr[pl.ds(r0, 8), 0:cbw]
            xi = g_scr[pl.ds(r0, 8), cbw:w2]
            for j, sh in enumerate((1, 2, 4)):
                kr = tab_ref[j:j + 1, 0:cbw]
                ki = tab_ref[j:j + 1, cbw:w2]
                sr = jnp.where(row < 8 - sh, pltpu.roll(xr, 8 - sh, 0), 0.0)
                si = jnp.where(row < 8 - sh, pltpu.roll(xi, 8 - sh, 0), 0.0)
                pr, pi = _cmul(kr, ki, sr, si)
                xr = xr + pr
                xi = xi + pi
            pr, pi = _cmul(tab_ref[8:16, 0:cbw], tab_ref[8:16, cbw:w2], cr, ci)
            xr = xr + pr
            xi = xi + pi
            h_ref[pl.ds(r0, 8), 0:cbw] = xr.astype(BF16)
            h_ref[pl.ds(r0, 8), cbw:w2] = xi.astype(BF16)
            pvr = s_scr[pl.ds(r0, 8), 0:cbw]
            pvi = s_scr[pl.ds(r0, 8), cbw:w2]
            cur_r = s_scr[pl.ds(r0 + 8, 8), 0:cbw]
            cur_i = s_scr[pl.ds(r0 + 8, 8), cbw:w2]
            spr = jnp.where(row == 0, jnp.broadcast_to(pvr[7:8, :], (8, cbw)), pltpu.roll(cur_r, 1, 0))
            spi = jnp.where(row == 0, jnp.broadcast_to(pvi[7:8, :], (8, cbw)), pltpu.roll(cur_i, 1, 0))
            ar = ar + spr * xr + spi * xi
            ai = ai + spr * xi - spi * xr
            return (jnp.broadcast_to(xr[0:1, :], (8, cbw)), jnp.broadcast_to(xi[0:1, :], (8, cbw)), ar, ai)

        z = jnp.zeros((8, cbw), F32)
        cr, ci, ar, ai = lax.fori_loop(0, hb, blk, (carry[:, 0:cbw], carry[:, cbw:w2], z, z))
        carry[:, 0:cbw] = cr
        carry[:, cbw:w2] = ci
        da_ref[:, 0:cbw] += ar
        da_ref[:, cbw:w2] += ai

    rt = lambda ti: nt - 1 - ti
    (h, da), riders = _call_with_plans(
        body, plans, name=name, grid=(SCAN_NCB, nt),
        in_specs=[pl.BlockSpec((tt, SCAN_UW), lambda cb, ti: (rt(ti), cb)),
                  pl.BlockSpec((None, SCAN_UW, w2), lambda cb, ti: (cb, 0, 0)),
                  pl.BlockSpec((tt, w2), lambda cb, ti: (rt(ti), cb)),
                  pl.BlockSpec((8, w2), lambda cb, ti: (jnp.maximum(rt(ti) * hb - 1, 0), cb)),
                  pl.BlockSpec((16, w2), lambda cb, ti: (0, cb))],
        out_specs=[pl.BlockSpec((tt, w2), lambda cb, ti: (rt(ti), cb)),
                   pl.BlockSpec((8, w2), lambda cb, ti: (0, cb))],
        out_shape=[jax.ShapeDtypeStruct((t, 2 * SSM_CH), BF16), jax.ShapeDtypeStruct((8, 2 * SSM_CH), F32)],
        scratch_shapes=[pltpu.VMEM((tt, w2), F32), pltpu.VMEM((tt + 8, w2), F32), pltpu.VMEM((8, w2), F32)],
        semantics=("parallel", "arbitrary"), operands=(dyb, cmat_t, s, s, tabb))
    return h, da, riders


def _s5_out(ymm, proj, dvec, *, name, tm=512):
    t, w = ymm.shape

    def body(y_ref, u_ref, d_ref, yo_ref, sg_ref, ub_ref):
        u = u_ref[...]
        y = y_ref[...] + d_ref[...] * u
        yo_ref[...] = y
        sg_ref[...] = _gelu(y).astype(BF16)
        ub_ref[...] = u.astype(BF16)

    return pl.pallas_call(
        body, name=name, grid=(t // tm,),
        in_specs=[_row_spec(tm, w), _row_spec(tm, w, 3), _vec_spec(w)],
        out_specs=[_row_spec(tm, w)] * 3,
        out_shape=[jax.ShapeDtypeStruct((t, w), F32), jax.ShapeDtypeStruct((t, w), BF16), jax.ShapeDtypeStruct((t, w), BF16)],
        compiler_params=_cparams(("parallel",)),
    )(ymm, proj, dvec)


def _s5_bwd_in(dsg, y, proj, *, name, tm=512):
    t, w = y.shape

    def body(d_ref, y_ref, u_ref, dy_ref, dyb_ref, dd_ref):
        i = pl.program_id(0)
        dy = d_ref[...] * _gelu_grad(y_ref[...])
        dy_ref[...] = dy
        dyb_ref[...] = dy.astype(BF16)
        part = jnp.sum(dy * u_ref[...], axis=0, keepdims=True)

        @pl.when(i == 0)
        def _():
            dd_ref[...] = part

        @pl.when(i > 0)
        def _():
            dd_ref[...] += part

    return pl.pallas_call(
        body, name=name, grid=(t // tm,),
        in_specs=[_row_spec(tm, w), _row_spec(tm, w), _row_spec(tm, w, 3)],
        out_specs=[_row_spec(tm, w), _row_spec(tm, w), _vec_spec(w)],
        out_shape=[jax.ShapeDtypeStruct((t, w), F32), jax.ShapeDtypeStruct((t, w), BF16), jax.ShapeDtypeStruct((1, w), F32)],
        compiler_params=_cparams(("arbitrary",)),
    )(dsg, y, proj)


def _s5_du(dumm, dy, dvec, *, name, tm=512):
    t, w = dy.shape

    def body(a_ref, dy_ref, d_ref, o_ref):
        o_ref[...] = (a_ref[...] + d_ref[...] * dy_ref[...]).astype(BF16)

    return pl.pallas_call(
        body, name=name, grid=(t // tm,), in_specs=[_row_spec(tm, w), _row_spec(tm, w), _vec_spec(w)],
        out_specs=_row_spec(tm, w), out_shape=jax.ShapeDtypeStruct((t, w), BF16),
        compiler_params=_cparams(("parallel",)),
    )(dumm, dy, dvec)


def _s5_discretise(lam_re, lam_im, log_step, b_re, b_im):
    lam = lax.complex(lam_re, lam_im)
    dt = jnp.exp(log_step)[:, None]
    a = jnp.exp(lam * dt)
    bbar = ((a - 1.0) / lam)[..., None] * lax.complex(b_re, b_im)
    return jnp.real(a), jnp.imag(a), jnp.real(bbar), jnp.imag(bbar)


def _perm_cols(re, im):
    lead = re.shape[:-1]
    r = re.reshape(lead + (SCAN_NCB, 1, SCAN_CB))
    i = im.reshape(lead + (SCAN_NCB, 1, SCAN_CB))
    return jnp.concatenate([r, i], axis=-2).reshape(lead + (2 * SSM_CH,))


def _unperm_cols(x):
    lead = x.shape[:-1]
    y = x.reshape(lead + (SCAN_NCB, 2, SCAN_CB))
    return y[..., 0, :].reshape(lead + (SSM_CH,)), y[..., 1, :].reshape(lead + (SSM_CH,))


def _compact(re, im):
    _, r, c = re.shape
    eye = jnp.eye(SCAN_GPB, dtype=re.dtype)

    def half(x):
        x = x.reshape(SCAN_NCB, SCAN_GPB, r, c)
        return (eye[None, :, None, :, None] * x[:, :, :, None, :]).reshape(SCAN_NCB, SCAN_GPB * r, SCAN_GPB * c)

    return jnp.concatenate([half(re), half(im)], axis=-1)


def _compact_extract(x, r):
    c = SSM_STATE
    eye = jnp.eye(SCAN_GPB, dtype=x.dtype)
    y = x.reshape(SCAN_NCB, SCAN_GPB, r, 2, SCAN_GPB, c)
    dg = jnp.sum(y * eye[None, :, None, None, :, None], axis=4).reshape(SSM_GROUPS, r, 2, c)
    return dg[:, :, 0, :], dg[:, :, 1, :]


def _pow_table(ar, ai, descending=False):
    ar = ar.reshape(1, SSM_CH)
    ai = ai.reshape(1, SSM_CH)
    pw = [(ar, ai)]
    for _ in range(7):
        pw.append(_cmul(pw[-1][0], pw[-1][1], ar, ai))
    zero = (jnp.zeros_like(ar), jnp.zeros_like(ar))
    rows = [pw[0], pw[1], pw[3]] + [zero] * 5 + (pw[::-1] if descending else pw)
    re = jnp.concatenate([r for r, _ in rows], axis=0)
    im = jnp.concatenate([i for _, i in rows], axis=0)
    return _perm_cols(re, im)


def _place():
    x, y, c = lax.axis_index("x"), lax.axis_index("y"), lax.axis_index("c")
    chips = [(1 - x, y), (x, 1 - y), (1 - x, 1 - y)]
    return x, y, c, chips


def _dev(px, py, pc):
    return 4 * px + 2 * py + pc


class _Plan:
    def __init__(self, ins, out_shapes, sem_shapes, start, finish, middle=None):
        self.ins, self.out_shapes, self.sem_shapes = list(ins), list(out_shapes), list(sem_shapes)
        self.start, self.finish, self.middle = start, finish, middle


def _split_plan_refs(plans, in_refs, out_refs, sem_refs):
    res, i, o, s = [], 0, 0, 0
    for p in plans:
        ni, no, ns = len(p.ins), len(p.out_shapes), len(p.sem_shapes)
        res.append((in_refs[i:i + ni], out_refs[o:o + no], sem_refs[s:s + ns]))
        i, o, s = i + ni, o + no, s + ns
    return res


def _run_plans(plans, *, name):
    ins = [a for p in plans for a in p.ins]
    outs = [o for p in plans for o in p.out_shapes]
    sems = [s for p in plans for s in p.sem_shapes]
    any_spec = pl.BlockSpec(memory_space=pl.ANY)

    def body(*refs):
        parts = _split_plan_refs(plans, refs[:len(ins)], refs[len(ins):len(ins) + len(outs)], refs[len(ins) + len(outs):])
        for p, r in zip(plans, parts):
            p.start(*r)
        for p, r in zip(plans, parts):
            if p.middle:
                p.middle(*r)
        for p, r in zip(plans, parts):
            p.finish(*r)

    res = pl.pallas_call(body, name=name, in_specs=[any_spec] * len(ins), out_specs=[any_spec] * len(outs),
                         out_shape=outs, scratch_shapes=sems)(*ins)
    return _split_plan_refs(plans, [], res, [])


def _run_plans_on_sequencer(plans, peers_of, *, name, collective_id, after=()):
    ins = [a for p in plans for a in p.ins]
    outs = [o for p in plans for o in p.out_shapes]
    sems = [s for p in plans for s in p.sem_shapes]

    def body(*refs):
        x, y, c, chips = _place()
        peers = peers_of(x, y, c, chips)
        barrier = pltpu.get_barrier_semaphore()
        for peer in peers:
            pl.semaphore_signal(barrier, inc=1, device_id=peer, device_id_type=MESH)
        pl.semaphore_wait(barrier, len(peers))
        n_in = len(ins) + len(after)
        parts = _split_plan_refs(plans, refs[:len(ins)], refs[n_in:n_in + len(outs)], refs[n_in + len(outs):])
        for p, r in zip(plans, parts):
            p.start(*r)
        for p, r in zip(plans, parts):
            if p.middle:
                p.middle(*r)
        for p, r in zip(plans, parts):
            p.finish(*r)

    res = pl.kernel(body, name=name, out_type=outs, mesh=plsc.ScalarSubcoreMesh(axis_name="seq", num_cores=1),
                    scratch_types=sems, compiler_params=pltpu.CompilerParams(collective_id=collective_id))(*ins, *after)
    return _split_plan_refs(plans, [], list(res), [])


def _call_with_plans(body, plans, *, name, grid, in_specs, out_specs, out_shape, scratch_shapes, semantics, operands):
    plans = list(plans)
    if not plans:
        res = pl.pallas_call(body, name=name, grid=grid, in_specs=in_specs, out_specs=out_specs, out_shape=out_shape,
                             scratch_shapes=scratch_shapes, compiler_params=_cparams(semantics))(*operands)
        return list(res), []
    n_in, n_out, n_scr = len(in_specs), len(out_specs), len(scratch_shapes)
    p_ins = [a for p in plans for a in p.ins]
    p_outs = [o for p in plans for o in p.out_shapes]
    p_sems = [s for p in plans for s in p.sem_shapes]
    nsteps = math.prod(grid)
    any_spec = pl.BlockSpec(memory_space=pl.ANY)

    def wrapped(*refs):
        bounds = [n_in, len(p_ins), n_out, len(p_outs), n_scr]
        parts, pos = [], 0
        for b in bounds:
            parts.append(refs[pos:pos + b])
            pos += b
        ins, p_in, outs, p_out, scr = parts
        step = pl.program_id(0)
        for ax in range(1, len(grid)):
            step = step * grid[ax] + pl.program_id(ax)
        riders = _split_plan_refs(plans, p_in, p_out, refs[pos:])

        @pl.when(step == 0)
        def _():
            for p, r in zip(plans, riders):
                p.start(*r)

        mids = [(p, r) for p, r in zip(plans, riders) if p.middle]
        mid_step = nsteps // 2
        split_mid = mids and 0 < mid_step < nsteps - 1
        if split_mid:
            @pl.when(step == mid_step)
            def _():
                for p, r in mids:
                    p.middle(*r)

        body(*ins, *outs, *scr)

        @pl.when(step == nsteps - 1)
        def _():
            if not split_mid:
                for p, r in mids:
                    p.middle(*r)
            for p, r in zip(plans, riders):
                p.finish(*r)

    res = pl.pallas_call(
        wrapped, name=name, grid=grid, in_specs=list(in_specs) + [any_spec] * len(p_ins),
        out_specs=list(out_specs) + [any_spec] * len(p_outs), out_shape=list(out_shape) + p_outs,
        scratch_shapes=list(scratch_shapes) + p_sems, compiler_params=_cparams(("arbitrary",) * len(grid)),
    )(*operands, *p_ins)
    return list(res[:n_out]), [r[1] for r in _split_plan_refs(plans, [], res[n_out:], [])]


def _gather_plan(shards):
    n = len(shards)
    nk = 8

    def make(ins, outs, sems):
        send, recv, lsem = sems
        x, y, c, _ = _place()
        me, sib, xn, yn, dg = (x, y, c), (x, y, 1 - c), (1 - x, y, c), (x, 1 - y, c), (1 - x, 1 - y, c)

        def part(w, block, half):
            ref = outs[w].at[_dev(*block)]
            if half is None:
                return ref
            rows = shards[w].shape[0] // 2
            return ref.at[pl.ds(half * rows, rows)]

        def copy(w, k, block, to, half=None, src=None):
            dst = part(w, block, half)
            return pltpu.make_async_remote_copy(
                src_ref=dst if src is None else src, dst_ref=dst,
                send_sem=send.at[w * nk + k], recv_sem=recv.at[w * nk + k], device_id=to, device_id_type=MESH)

        mine = [pltpu.make_async_copy(ins[w], outs[w].at[_dev(*me)], lsem.at[w]) for w in range(n)]
        return copy, mine, me, sib, xn, yn, dg

    def first_copies(copy, me, sib, xn, yn, ins):
        return [copy(w, k, me, to, src=ins[w]) for w in range(n) for k, to in ((0, sib), (1, xn), (2, yn))]

    def start(ins, outs, sems):
        copy, mine, me, sib, xn, yn, _ = make(ins, outs, sems)
        for cp in mine + first_copies(copy, me, sib, xn, yn, ins):
            cp.start()

    def middle(ins, outs, sems):
        copy, _, me, sib, xn, yn, _ = make(ins, outs, sems)
        for w in range(n):
            copy(w, 1, xn, me).wait_recv()
            copy(w, 3, xn, yn, half=0).start()
            copy(w, 5, xn, sib).start()
        for w in range(n):
            copy(w, 2, yn, me).wait_recv()
            copy(w, 4, yn, xn, half=1).start()
            copy(w, 6, yn, sib).start()

    def finish(ins, outs, sems):
        copy, mine, me, sib, xn, yn, dg = make(ins, outs, sems)
        last = []
        for w in range(n):
            copy(w, 3, dg, me, half=0).wait_recv()
            copy(w, 4, dg, me, half=1).wait_recv()
            fwd = copy(w, 7, dg, sib)
            fwd.start()
            last.append(fwd)
        sx, sy, sd = (1 - me[0], me[1], 1 - me[2]), (me[0], 1 - me[1], 1 - me[2]), (1 - me[0], 1 - me[1], 1 - me[2])
        for w in range(n):
            copy(w, 0, sib, me).wait_recv()
            copy(w, 5, sx, me).wait_recv()
            copy(w, 6, sy, me).wait_recv()
            copy(w, 7, sd, me).wait_recv()
        for cp in first_copies(copy, me, sib, xn, yn, ins) + last:
            cp.wait_send()
        for w in range(n):
            copy(w, 3, xn, yn, half=0).wait_send()
            copy(w, 5, xn, sib).wait_send()
            copy(w, 4, yn, xn, half=1).wait_send()
            copy(w, 6, yn, sib).wait_send()
        for cp in mine:
            cp.wait()

    return _Plan(shards, [jax.ShapeDtypeStruct((N_DEV,) + s.shape, s.dtype) for s in shards],
                 [pltpu.SemaphoreType.DMA((nk * n,)), pltpu.SemaphoreType.DMA((nk * n,)), pltpu.SemaphoreType.DMA((n,))],
                 start, finish, middle)


def _swap_plan(copies_of, n_copies, ins, out_shapes):
    def cps(in_refs, out_refs, sems):
        return copies_of(in_refs, out_refs, sems[0], sems[1])

    def start(in_refs, out_refs, sems):
        for cp in cps(in_refs, out_refs, sems):
            cp.start()

    def finish(in_refs, out_refs, sems):
        all_cps = cps(in_refs, out_refs, sems)
        for cp in all_cps:
            cp.wait_recv()
        for cp in all_cps:
            cp.wait_send()

    return _Plan(ins, out_shapes, [pltpu.SemaphoreType.DMA((n_copies,)), pltpu.SemaphoreType.DMA((n_copies,))],
                 start, finish)


def _sibling_plan(grads):
    n = len(grads)

    def copies(ins, outs, send, recv):
        x, y, c, chips = _place()
        owners = [(x, y)] + chips
        return [pltpu.make_async_remote_copy(
            src_ref=ins[w].at[_dev(*chip, 1 - c)], dst_ref=outs[w].at[k], send_sem=send.at[w * 4 + k],
            recv_sem=recv.at[w * 4 + k], device_id=(x, y, 1 - c), device_id_type=MESH)
            for w in range(n) for k, chip in enumerate(owners)]

    return _swap_plan(copies, 4 * n, grads, [jax.ShapeDtypeStruct((4,) + g.shape[1:], g.dtype) for g in grads])


def _chip_plan(parts, js=(0, 1, 2)):
    n, nj = len(parts), len(js)

    def copies(ins, outs, send, recv):
        x, y, c, chips = _place()
        return [pltpu.make_async_remote_copy(
            src_ref=ins[w].at[j], dst_ref=outs[w * nj + k], send_sem=send.at[w * nj + k],
            recv_sem=recv.at[w * nj + k], device_id=(*chips[j], c), device_id_type=MESH)
            for w in range(n) for k, j in enumerate(js)]

    return _swap_plan(copies, n * nj, parts,
                      [jax.ShapeDtypeStruct(p.shape[1:], p.dtype) for p in parts for _ in js])


UPDATE_TILE_BYTES = 1536 * 1024


def _row_tile(r, c):
    best = 8
    for t in range(8, r + 1, 8):
        if r % t == 0 and t * c * 4 <= UPDATE_TILE_BYTES:
            best = t
    return best


def _chip_partial(g, sib, ids, *, name):
    _, r, c = g.shape
    tr = _row_tile(r, c)

    def body(ids_ref, g_ref, s_ref, o_ref):
        o_ref[...] = (g_ref[...] + s_ref[...]).astype(BF16)

    return pl.pallas_call(
        body, name=name,
        grid_spec=pltpu.PrefetchScalarGridSpec(
            num_scalar_prefetch=1, grid=(3, r // tr),
            in_specs=[pl.BlockSpec((None, tr, c), lambda j, i, ids_ref: (ids_ref[j], i, 0)),
                      pl.BlockSpec((None, tr, c), lambda j, i, ids_ref: (j + 1, i, 0))],
            out_specs=pl.BlockSpec((None, tr, c), lambda j, i, ids_ref: (j, i, 0))),
        out_shape=jax.ShapeDtypeStruct((3, r, c), BF16),
        compiler_params=_cparams(("parallel", "parallel")),
    )(ids, g, sib)


def _adamw_math(w, g, m, v):
    m = ADAM_B1 * m + (1.0 - ADAM_B1) * g
    v = ADAM_B2 * v + (1.0 - ADAM_B2) * (g * g)
    m_hat = m / (1.0 - ADAM_B1 ** ADAM_STEP)
    v_hat = v / (1.0 - ADAM_B2 ** ADAM_STEP)
    delta = -ADAM_LR * (m_hat / (jnp.sqrt(v_hat) + ADAM_EPS) + ADAM_WD * w)
    return delta, m, v


def _shard_update(g, sib, rem, me, w, m, v, *, name):
    r, c = w.shape
    tr = _row_tile(r, c)

    def body(me_ref, g_ref, s_ref, r0_ref, r1_ref, r2_ref, w_ref, m_ref, v_ref, go_ref, d_ref, mo_ref, vo_ref):
        gt = g_ref[...] + s_ref[...]
        gt = gt + r0_ref[...].astype(F32)
        gt = gt + r1_ref[...].astype(F32)
        gt = gt + r2_ref[...].astype(F32)
        go_ref[...] = gt
        d, mn, vn = _adamw_math(w_ref[...], gt, m_ref[...], v_ref[...])
        d_ref[...] = d
        mo_ref[...] = mn
        vo_ref[...] = vn

    blk = lambda k: pl.BlockSpec((None, tr, c), lambda i, me_ref: (k, i, 0))
    plain = pl.BlockSpec((tr, c), lambda i, me_ref: (i, 0))
    return pl.pallas_call(
        body, name=name,
        grid_spec=pltpu.PrefetchScalarGridSpec(
            num_scalar_prefetch=1, grid=(r // tr,),
            in_specs=[pl.BlockSpec((None, tr, c), lambda i, me_ref: (me_ref[0], i, 0)), blk(0), plain, plain, plain,
                      plain, plain, plain],
            out_specs=[plain] * 4),
        out_shape=[jax.ShapeDtypeStruct((r, c), F32)] * 4,
        compiler_params=_cparams(("parallel",)),
    )(me, g, sib, *rem, w, m, v)


def _small_update(gathered, w, m, v, *, name):
    _, r, c = gathered.shape

    def body(g_ref, w_ref, m_ref, v_ref, go_ref, d_ref, mo_ref, vo_ref):
        gt = g_ref[0]
        for k in range(1, N_DEV):
            gt = gt + g_ref[k]
        go_ref[...] = gt
        d, mn, vn = _adamw_math(w_ref[...], gt, m_ref[...], v_ref[...])
        d_ref[...] = d
        mo_ref[...] = mn
        vo_ref[...] = vn

    return pl.pallas_call(
        body, name=name, out_shape=[jax.ShapeDtypeStruct((r, c), F32)] * 4,
        compiler_params=pltpu.CompilerParams(vmem_limit_bytes=VMEM_LIMIT),
    )(gathered, w, m, v)


SMALL_UNIT = 1024


def _pack(parts):
    flat = []
    for p in parts:
        f = p.reshape(-1).astype(F32)
        pad = (-f.shape[0]) % SMALL_UNIT
        flat.append(jnp.pad(f, (0, pad)))
    return jnp.concatenate(flat).reshape(-1, 128)


def _unpack(buf, shapes):
    flat = buf.reshape(-1)
    out, off = [], 0
    for s in shapes:
        nel = math.prod(s)
        out.append(flat[off:off + nel].reshape(s))
        off += nel + ((-nel) % SMALL_UNIT)
    return out


def kernel(x, p, ffn1_w_in, ffn1_w_out, ln1_g, ln1_b, mix_w_in, conv_w, conv_b, conv_w_out, ssm_lam_re, ssm_lam_im, ssm_log_step, ssm_b_re, ssm_b_im, ssm_c_re, ssm_c_im, ssm_d, ssm_w_glu, mix_w_out, ln2_g, ln2_b, ffn2_w_in, ffn2_w_out, ln3_g, ln3_b, ple_w_in, ple_w_gate, ln4_g, ln4_b, loss_target, m_ffn1_w_in, m_ffn1_w_out, m_ln1_g, m_ln1_b, m_mix_w_in, m_conv_w, m_conv_b, m_conv_w_out, m_ssm_lam_re, m_ssm_lam_im, m_ssm_log_step, m_ssm_b_re, m_ssm_b_im, m_ssm_c_re, m_ssm_c_im, m_ssm_d, m_ssm_w_glu, m_mix_w_out, m_ln2_g, m_ln2_b, m_ffn2_w_in, m_ffn2_w_out, m_ln3_g, m_ln3_b, m_ple_w_in, m_ple_w_gate, m_ln4_g, m_ln4_b, v_ffn1_w_in, v_ffn1_w_out, v_ln1_g, v_ln1_b, v_mix_w_in, v_conv_w, v_conv_b, v_conv_w_out, v_ssm_lam_re, v_ssm_lam_im, v_ssm_log_step, v_ssm_b_re, v_ssm_b_im, v_ssm_c_re, v_ssm_c_im, v_ssm_d, v_ssm_w_glu, v_mix_w_out, v_ln2_g, v_ln2_b, v_ffn2_w_in, v_ffn2_w_out, v_ln3_g, v_ln3_b, v_ple_w_in, v_ple_w_gate, v_ln4_g, v_ln4_b):
    args = dict(locals())
    big = ['ffn1_w_in', 'ffn1_w_out', 'mix_w_in', 'conv_w_out', 'ssm_w_glu', 'mix_w_out',
           'ffn2_w_in', 'ffn2_w_out', 'ple_w_in', 'ple_w_gate']
    small = ['ln1_g', 'ln1_b', 'conv_b', 'ssm_lam_re', 'ssm_lam_im', 'ssm_log_step', 'ssm_b_re', 'ssm_b_im',
             'ssm_c_re', 'ssm_c_im', 'ssm_d', 'ln2_g', 'ln2_b', 'ln3_g', 'ln3_b', 'ln4_g', 'ln4_b']
    order = ['ffn1_w_in', 'ffn1_w_out', 'ln1_g', 'ln1_b', 'mix_w_in', 'conv_w', 'conv_b', 'conv_w_out',
             'ssm_lam_re', 'ssm_lam_im', 'ssm_log_step', 'ssm_b_re', 'ssm_b_im', 'ssm_c_re', 'ssm_c_im', 'ssm_d',
             'ssm_w_glu', 'mix_w_out', 'ln2_g', 'ln2_b', 'ffn2_w_in', 'ffn2_w_out', 'ln3_g', 'ln3_b',
             'ple_w_in', 'ple_w_gate', 'ln4_g', 'ln4_b']

    t = x.shape[1]
    d = x.shape[2]
    xc_, yc_, cc_ = lax.axis_index("x"), lax.axis_index("y"), lax.axis_index("c")
    me = (4 * xc_ + 2 * yc_ + cc_).astype(jnp.int32)
    cw_cols = conv_w.shape[2]

    turned = ('ffn1_w_in', 'ffn2_w_in')

    def local(a, nm):
        return jnp.swapaxes(a[0], 0, 1) if nm in turned else a[0]

    shard = {nm: local(args[nm], nm).astype(BF16) for nm in big}
    cw_pad = jnp.zeros((16, 128), F32).at[0:3, 0:cw_cols].set(conv_w[0])
    wf = shard['ffn1_w_in'].shape[0]

    def gather(arrays, tag, after=()):
        ((_, got, _),) = _run_plans_on_sequencer(
            [_gather_plan(arrays)], lambda x, y, c, chips: [(x, y, 1 - c), (1 - x, y, c), (x, 1 - y, c)],
            name=f"gather_{tag}", collective_id=2, after=after)
        return got

    w1in, cw_g = gather([shard['ffn1_w_in'], cw_pad], "ffn1_in")
    (w1out_g,) = gather([shard['ffn1_w_out']], "ffn1_out")
    (wmix,) = gather([shard['mix_w_in']], "mix_in")
    wco, wglu, wmo_g = gather([shard[nm] for nm in ('conv_w_out', 'ssm_w_glu', 'mix_w_out')], "mix_rest")
    (w2in,) = gather([shard['ffn2_w_in']], "ffn2_in")
    (w2out_g,) = gather([shard['ffn2_w_out']], "ffn2_out")
    wpin, wgate_g = gather([shard['ple_w_in'], shard['ple_w_gate']], "ple")
    cw_full = jnp.transpose(cw_g[:, 0:3, 0:cw_cols], (1, 0, 2)).reshape(3, N_DEV * cw_cols)
    cw8 = jnp.zeros((8, CONV_CH), F32).at[0:3, :].set(cw_full)

    s5_in = (ssm_lam_re[0], ssm_lam_im[0], ssm_log_step[0], ssm_b_re[0], ssm_b_im[0])
    (a_re, a_im, bb_re, bb_im), s5_vjp = jax.vjp(_s5_discretise, *s5_in)
    tab_f = _pow_table(a_re, a_im)
    tab_b = _pow_table(a_re, -a_im, descending=True)
    bmat_b = _compact(jnp.transpose(bb_re, (0, 2, 1)), jnp.transpose(bb_im, (0, 2, 1))).astype(BF16)
    cmat_tb = _compact(ssm_c_re[0], -ssm_c_im[0]).astype(BF16)
    dvec = ssm_d[0].reshape(1, SSM_W)

    xf = x[0]
    x_b = xf.astype(BF16)
    p_b = p[0, 0].astype(BF16)
    tgt = loss_target[0]
    tq = min(512, t)

    ffn_out = dict(ja='c', jb='c', nj=4, tm=tq, tn=d, tk=wf)
    def side_by_side(wb):
        return jnp.transpose(wb, (1, 0, 2)).reshape(wb.shape[1], N_DEV * wb.shape[2])

    tf = min(1024, t)
    a1, h1, _ = _ffn_in(x_b, w1in, name="ffn1_in", tm=tf)
    w1out = w1out_g.reshape(4, wf, d)
    r1, x1, x1b = _mm(a1, w1out, name="ffn1_out", **ffn_out, epilogue=_ln_epilogue(xf, ln1_g, ln1_b, 0.5))
    proj = _mm(x1b, wmix, name="mix_in", jb='b', jo='b', o_flat=True, nj=8, tm=tq, tn=512, tk=d)
    wmo = wmo_g.reshape(d, d)
    ycin = _conv_fwd(proj, cw8, conv_b, name="conv_fwd")
    wco, wglu = side_by_side(wco), side_by_side(wglu)
    yconv = _mm(ycin, wco, name="conv_out", tm=tq, tn=d, tk=CONV_CH)
    s_f, s_b, _ = _scan_fwd(proj, bmat_b, tab_f, name="scan_fwd")
    blk = dict(ja='b', jb='b', jo='b', nj=SCAN_NCB)
    ymm = _mm(s_b, cmat_tb, name="ssm_read", a_flat=True, o_flat=True, tb=True, tm=tq, tn=SCAN_UW, tk=2 * SCAN_CB, **blk)
    ys, sg, u_b = _s5_out(ymm, proj, dvec, name="ssm_out")
    glu = _mm(sg, wglu, name="glu_in", tm=tq, tn=d, tk=SSM_W)
    merged = _gate_fwd(yconv, glu, proj, name="gate_fwd")
    r2, x2, x2b = _mm(merged, wmo, name="mix_out", tm=tq, tn=d, tk=d, epilogue=_ln_epilogue(x1, ln2_g, ln2_b, 1.0))
    a2, h2, _ = _ffn_in(x2b, w2in, name="ffn2_in", tm=tf)
    w2out = w2out_g.reshape(4, wf, d)
    r3, x3, x3b = _mm(a2, w2out, name="ffn2_out", **ffn_out, epilogue=_ln_epilogue(x2, ln3_g, ln3_b, 0.5))
    wgate = wgate_g.reshape(d, d)
    pe = _mm(p_b, side_by_side(wpin), name="ple_in", tm=tq, tn=d, tk=p_b.shape[1])
    gp = _mm(x3b, wgate, name="ple_gate", tm=tq, tn=d, tk=d)

    dr4, dpe_b, dgp_b, dg4, db4, loss_part = _ln_bwd(x3, [pe, gp], ln4_g, ln4_b, [], name="ple_ln_bwd",
                                                     ple=True, target=tgt)
    gb, sib, rem = {}, {}, {}
    ids = jnp.stack([_dev(1 - xc_, yc_, cc_), _dev(xc_, 1 - yc_, cc_), _dev(1 - xc_, 1 - yc_, cc_)]).astype(jnp.int32)

    def blocked(nm, g):
        return g.reshape((N_DEV,) + args[nm].shape[1:])

    def to_sibling(*names):
        return _sibling_plan([gb[nm] for nm in names])

    def chip_sums(names, sibs):
        for nm, s in zip(names, sibs):
            sib[nm] = s
        return [_chip_partial(gb[nm], sib[nm], ids, name=f"chip_sum_{nm}") for nm in names]

    def on_sequencer(plan, peers, tag, cid, after=()):
        ((_, got, _),) = _run_plans_on_sequencer([plan], peers, name=tag, collective_id=cid, after=after)
        return got

    waiting = []

    def send_sibling(*names, after=()):
        got = on_sequencer(to_sibling(*names), lambda x, y, c, chips: [(x, y, 1 - c)], f"grad_sibling_{names[0]}", 3,
                           after=after)
        waiting.append((names, got))

    def send_chips(count=None, after=()):
        for _ in range(len(waiting) if count is None else count):
            names, got = waiting.pop(0)
            r_ = on_sequencer(_chip_plan(chip_sums(names, got)), lambda x, y, c, chips: [(*chip, c) for chip in chips],
                              f"grad_chips_{names[0]}", 1, after=after)
            for i, nm in enumerate(names):
                rem[nm] = r_[3 * i:3 * i + 3]

    ffn_in_dg = dict(ja='c', jb='c', nj=8, tm=tq, tn=d, tk=wf)
    ffn_in_wg = dict(ja='b', jo='b', ta=True, nj=8, tm=wf, tn=d, tk=t)
    ffn_out_wg = dict(ja='b', jo='b', ta=True, nj=4, tm=wf, tn=d, tk=t)

    gb['ple_w_in'] = _mm(p_b, dpe_b, name="ple_in_wg", jb='b', jo='b', b_flat=True, ta=True, nj=8,
                         tm=p_b.shape[1], tn=128, tk=t)
    gb['ple_w_gate'] = blocked('ple_w_gate', _mm(x3b, dgp_b, name="ple_gate_wg", ta=True, tm=d, tn=d, tk=t))
    g_ple = ('ple_w_in', 'ple_w_gate')
    dx3_g, (s_,) = _mm(dgp_b, wgate, name="ple_gate_dg", tb=True, tm=tq, tn=d, tk=d, plans=[to_sibling(*g_ple)])
    waiting.append((g_ple, s_))

    dr3, df2_b, dg3, db3 = _ln_bwd(r3, [], ln3_g, ln3_b, [(dr4, ALPHA), (dx3_g, 1.0)], name="ffn2_ln_bwd", fs=0.5)
    send_chips()
    dh2, _ = _ffn_out_dg(df2_b, w2out, h2, name="ffn2_out_dg", tm=tf)
    gb['ffn2_w_out'] = blocked('ffn2_w_out', _mm(a2, df2_b, name="ffn2_out_wg", **ffn_out_wg))
    send_sibling('ffn2_w_out')
    dx2_f = _mm(dh2, w2in, name="ffn2_in_dg", **ffn_in_dg)
    send_chips()
    gb['ffn2_w_in'] = _mm(dh2, x2b, name="ffn2_in_wg", **ffn_in_wg)

    dr2, dmix_b, dg2, db2 = _ln_bwd(r2, [], ln2_g, ln2_b, [(dr3, ALPHA), (dx2_f, 1.0)], name="mix_ln_bwd")
    dmerged = _mm(dmix_b, wmo, name="mix_out_dg", tb=True, tm=tq, tn=d, tk=d)
    send_sibling('ffn2_w_in', after=(dmerged,))
    gb['mix_w_out'] = blocked('mix_w_out', _mm(merged, dmix_b, name="mix_out_wg", ta=True, tm=d, tn=d, tk=t))
    dyconv_b, dglu_b, dgate_b = _gate_bwd(dmerged, yconv, glu, proj, name="gate_bwd")
    send_chips()
    gb['conv_w_out'] = _mm(ycin, dyconv_b, name="conv_out_wg", jb='b', jo='b', b_flat=True, ta=True, nj=8,
                           tm=CONV_CH, tn=128, tk=t)
    dycin = _mm(dyconv_b, wco, name="conv_out_dg", tb=True, tm=tq, tn=CONV_CH, tk=d)
    gb['ssm_w_glu'] = _mm(sg, dglu_b, name="glu_in_wg", jb='b', jo='b', b_flat=True, ta=True, nj=8,
                          tm=SSM_W, tn=256, tk=t)
    dsg = _mm(dglu_b, wglu, name="glu_in_dg", tb=True, tm=tq, tn=SSM_W, tk=2 * d)
    dys, dys_b, dd = _s5_bwd_in(dsg, ys, proj, name="ssm_out_bwd")
    h_b, da_acc, _ = _scan_bwd(dys_b, cmat_tb, s_f, tab_b, name="scan_bwd")
    send_sibling('mix_w_out', 'conv_w_out', 'ssm_w_glu', after=(h_b,))
    dumm = _mm(h_b, bmat_b, name="ssm_write_dg", a_flat=True, o_flat=True, tb=True, tm=tq, tn=SCAN_UW,
               tk=2 * SCAN_CB, **blk)
    du_b = _s5_du(dumm, dys, dvec, name="ssm_du")
    g_bmat = _mm(u_b, h_b, name="ssm_write_wg", a_flat=True, b_flat=True, ta=True, tm=SCAN_UW,
                 tn=2 * SCAN_CB, tk=t, **blk)
    send_chips()
    g_cmat = _mm(dys_b, s_b, name="ssm_read_wg", a_flat=True, b_flat=True, ta=True, tm=SCAN_UW,
                 tn=2 * SCAN_CB, tk=t, **blk)
    dcb_b, dcc_b, dch_b, dconv = _conv_bwd(dycin, proj, cw8, conv_b, name="conv_bwd")
    dproj = jnp.concatenate([dcb_b, dcc_b, dch_b, du_b, dgate_b], axis=1)
    gb['mix_w_in'] = _mm(x1b, dproj, name="mix_in_wg", jb='b', jo='b', b_flat=True, ta=True, nj=8,
                         tm=d, tn=512, tk=t)
    send_sibling('mix_w_in')
    dx1_m = _mm(dproj, wmix, name="mix_in_dg", ja='c', jb='c', a_flat=True, tb=True, nj=8, tm=tq, tn=d, tk=512)
    send_chips()

    dr1, df1_b, dg1, db1 = _ln_bwd(r1, [], ln1_g, ln1_b, [(dr2, ALPHA), (dx1_m, 1.0)], name="ffn1_ln_bwd", fs=0.5)
    dh1, _ = _ffn_out_dg(df1_b, w1out, h1, name="ffn1_out_dg", tm=tf)

    da_sum = jnp.sum(da_acc, axis=0)
    da_re, da_im = _unperm_cols(da_sum)
    gbb_re, gbb_im = [jnp.transpose(v, (0, 2, 1)) for v in _compact_extract(g_bmat, SSM_GROUP)]
    g_c_re, g_c_im_neg = _compact_extract(g_cmat, SSM_GROUP)
    g_c_im = -g_c_im_neg
    g_lam_re, g_lam_im, g_log_step, g_b_re, g_b_im = s5_vjp(
        (da_re.reshape(SSM_GROUPS, SSM_STATE), da_im.reshape(SSM_GROUPS, SSM_STATE), gbb_re, gbb_im))
    g_d = dd.reshape(SSM_GROUPS, SSM_GROUP)

    small_g = {'ln1_g': dg1, 'ln1_b': db1, 'conv_b': dconv[3:4], 'ssm_lam_re': g_lam_re, 'ssm_lam_im': g_lam_im,
               'ssm_log_step': g_log_step, 'ssm_b_re': g_b_re, 'ssm_b_im': g_b_im, 'ssm_c_re': g_c_re,
               'ssm_c_im': g_c_im, 'ssm_d': g_d, 'ln2_g': dg2, 'ln2_b': db2, 'ln3_g': dg3, 'ln3_b': db3,
               'ln4_g': dg4, 'ln4_b': db4}
    small_shapes = [args[nm].shape for nm in small] + [(3, CONV_CH), (1,)]
    g_pack = _pack([small_g[nm] for nm in small] + [dconv[0:3], loss_part[0:1, 0:1]])

    res = {}
    me1 = me.reshape(1)

    def update(nm):
        upd = _shard_update(gb[nm], sib[nm], rem[nm], me1, local(args[nm], nm), local(args['m_' + nm], nm),
                            local(args['v_' + nm], nm), name=f"update_{nm}")
        for key, val in zip(('grad_', 'delta_', 'new_m_', 'new_v_'), upd):
            res[key + nm] = (jnp.swapaxes(val, 0, 1) if nm in turned else val)[None]

    (g_all,) = gather([g_pack], "small", after=(dh1,))
    gb['ffn1_w_in'] = _mm(dh1, x_b, name="ffn1_in_wg", **ffn_in_wg)
    send_sibling('ffn1_w_in')
    gb['ffn1_w_out'] = blocked('ffn1_w_out', _mm(a1, df1_b, name="ffn1_out_wg", **ffn_out_wg))
    send_sibling('ffn1_w_out')
    send_chips(1)
    (grad_x,) = _mm(dh1, w1in, name="ffn1_in_dg", **ffn_in_dg,
                    epilogue=(lambda pr, drv: (pr + ALPHA * drv,), (dr1,), (), (F32,)))
    last = ['ffn1_w_in', 'ffn1_w_out']
    for nm in big:
        if nm not in last:
            update(nm)
    send_chips(after=(res['new_v_ffn2_w_out'],))

    def full_cw(a):
        return lax.dynamic_update_slice(jnp.zeros((3, CONV_CH), F32), a[0], (0, me * cw_cols))

    zero1 = jnp.zeros((1,), F32)
    w_pack = _pack([args[nm] for nm in small] + [full_cw(conv_w), zero1])
    m_pack = _pack([args['m_' + nm] for nm in small] + [full_cw(m_conv_w), zero1])
    v_pack = _pack([args['v_' + nm] for nm in small] + [full_cw(v_conv_w), zero1])
    sg_sum, sd, sm, sv = _small_update(g_all, w_pack, m_pack, v_pack, name="small_update")
    for key, buf in (('grad_', sg_sum), ('delta_', sd), ('new_m_', sm), ('new_v_', sv)):
        parts = _unpack(buf, small_shapes)
        for nm, val in zip(small, parts[:len(small)]):
            res[key + nm] = val
        res[key + 'conv_w'] = lax.dynamic_slice(parts[len(small)], (0, me * cw_cols), (3, cw_cols))[None]
        if key == 'grad_':
            loss = parts[-1][0]

    for nm in last:
        update(nm)

    outs = [loss, grad_x[None]]
    for key in ('grad_', 'delta_', 'new_m_', 'new_v_'):
        outs += [res[key + nm] for nm in order]
    return tuple(outs)
```

```python
import functools
import math

import jax
import jax.numpy as jnp
from jax import lax
from jax.experimental import pallas as pl
from jax.experimental.pallas import tpu as pltpu
from jax.experimental.pallas import tpu_sc as plsc

F32 = jnp.float32
BF16 = jnp.bfloat16
MESH = pl.DeviceIdType.MESH

N_DEV = 8
ALPHA = 2.0 ** 0.25
LN_EPS = 1e-5
CONV_CH = 512
SSM_W = 512
SSM_GROUPS = 32
SSM_GROUP = 16
SSM_STATE = 64
SSM_CH = SSM_GROUPS * SSM_STATE
SCAN_CB = 512
SCAN_NCB = SSM_CH // SCAN_CB
SCAN_GPB = SSM_GROUPS // SCAN_NCB
SCAN_UW = SCAN_GPB * SSM_GROUP
SCAN_TT = 256
ADAM_LR = 0.001
ADAM_B1 = 0.9
ADAM_B2 = 0.999
ADAM_EPS = 1e-08
ADAM_WD = 0.01
ADAM_STEP = 10
VMEM_LIMIT = 56 * 1024 * 1024


def _cparams(sem=None, **kw):
    return pltpu.CompilerParams(dimension_semantics=sem, vmem_limit_bytes=VMEM_LIMIT, **kw)


def _mm(a, b, *, name, ja=None, jb=None, jo=None, a_flat=False, b_flat=False, o_flat=False,
        ta=False, tb=False, tm, tn, tk, nj=1, out_dtype=F32, plans=(), epilogue=None, after=()):
    def dims(arr, j, flat):
        if j is None:
            return arr.shape
        if flat:
            return (arr.shape[0], arr.shape[1] // nj)
        assert arr.shape[0] == nj, (name, arr.shape, nj)
        return arr.shape[1:]

    ar, ac = dims(a, ja, a_flat)
    br, bc = dims(b, jb, b_flat)
    m, k = (ac, ar) if ta else (ar, ac)
    k2, n = (bc, br) if tb else (br, bc)
    assert k == k2, (name, a.shape, b.shape)
    assert m % tm == 0 and n % tn == 0 and k % tk == 0, (name, m, n, k, tm, tn, tk)
    njb = nj if 'b' in (ja, jb) else 1
    njc = nj if 'c' in (ja, jb) else 1
    nk = k // tk
    j_inside = njc > 1 and nk == 1 and not ta
    n_in = njc if j_inside else 1
    nred = nk if j_inside else njc * nk
    grid = (njb, m // tm, n // tn, 1 if j_inside else njc, nk)

    def make_spec(j, flat, blk, rfn, cfn, cols_per_j):
        def jsel(g, c):
            return g if j == 'b' else c
        if j is None:
            return pl.BlockSpec(blk, lambda g, i, jn, c, kk: (rfn(i, jn, kk), cfn(i, jn, kk)))
        if j == 'c' and j_inside:
            if flat:
                return pl.BlockSpec((blk[0], nj * blk[1]), lambda g, i, jn, c, kk: (rfn(i, jn, kk), 0))
            return pl.BlockSpec((nj,) + blk, lambda g, i, jn, c, kk: (0, rfn(i, jn, kk), cfn(i, jn, kk)))
        if flat:
            nb = cols_per_j // blk[1]
            return pl.BlockSpec(blk, lambda g, i, jn, c, kk: (rfn(i, jn, kk), jsel(g, c) * nb + cfn(i, jn, kk)))
        return pl.BlockSpec((None,) + blk,
                            lambda g, i, jn, c, kk: (jsel(g, c), rfn(i, jn, kk), cfn(i, jn, kk)))

    if ta:
        a_spec = make_spec(ja, a_flat, (tk, tm), lambda i, jn, kk: kk, lambda i, jn, kk: i, ac)
    else:
        a_spec = make_spec(ja, a_flat, (tm, tk), lambda i, jn, kk: i, lambda i, jn, kk: kk, ac)
    if tb:
        b_spec = make_spec(jb, b_flat, (tn, tk), lambda i, jn, kk: jn, lambda i, jn, kk: kk, bc)
    else:
        b_spec = make_spec(jb, b_flat, (tk, tn), lambda i, jn, kk: kk, lambda i, jn, kk: jn, bc)
    o_spec = make_spec(jo, o_flat, (tm, tn), lambda i, jn, kk: i, lambda i, jn, kk: jn, n)
    if jo is None:
        out_shape = (m, n)
    elif o_flat:
        out_shape = (m, nj * n)
    else:
        out_shape = (nj, m, n)

    dn = (((0 if ta else 1,), (1 if tb else 0,)), ((), ()))

    def operand(ref, j, flat, jj, width):
        if not (j == 'c' and j_inside):
            return ref[...]
        return ref[:, jj * width:(jj + 1) * width] if flat else ref[jj]

    e_fn, e_rows, e_vecs, e_dtypes = epilogue if epilogue else (None, (), (), (out_dtype,))
    assert not epilogue or (nred == 1 and jo is None), name
    n_e = len(e_rows) + len(e_vecs)
    n_o = len(e_dtypes)
    n_a = len(after)

    def body(a_ref, b_ref, *rest):
        e_refs, o_refs, scratch = rest[:n_e], rest[n_e + n_a:n_e + n_a + n_o], rest[n_e + n_a + n_o:]
        o_ref = o_refs[0]
        p = None
        for jj in range(n_in):
            q = lax.dot_general(operand(a_ref, ja, a_flat, jj, tk), operand(b_ref, jb, b_flat, jj, tk if tb else tn),
                                dn, preferred_element_type=F32)
            p = q if p is None else p + q
        if epilogue:
            for ref, val in zip(o_refs, e_fn(p, *[r[...] for r in e_refs])):
                ref[...] = val.astype(ref.dtype)
        elif nred == 1:
            o_ref[...] = p.astype(o_ref.dtype)
        else:
            acc = scratch[0]
            r = pl.program_id(3) * nk + pl.program_id(4)

            @pl.when(r == 0)
            def _():
                acc[...] = p

            @pl.when(r > 0)
            def _():
                acc[...] += p

            @pl.when(r == nred - 1)
            def _():
                o_ref[...] = acc[...].astype(o_ref.dtype)

    vec_spec = pl.BlockSpec((1, tn), lambda g, i, jn, c, kk: (0, jn))
    res = _call_with_plans(
        body, plans, name=name, grid=grid,
        in_specs=([a_spec, b_spec] + [o_spec] * len(e_rows) + [vec_spec] * len(e_vecs)
                  + [pl.BlockSpec(memory_space=pl.ANY)] * n_a),
        out_specs=[o_spec] * n_o, out_shape=[jax.ShapeDtypeStruct(out_shape, dt) for dt in e_dtypes],
        scratch_shapes=[] if nred == 1 else [pltpu.VMEM((tm, tn), F32)],
        semantics=("parallel", "parallel", "parallel", "arbitrary", "arbitrary"),
        operands=(a, b, *e_rows, *e_vecs, *after))
    outs = res[0] if epilogue else res[0][0]
    return (outs, res[1]) if plans else outs


def _sigmoid(v):
    return jax.nn.sigmoid(v)


def _row_spec(tm, cols, colblk=0):
    return pl.BlockSpec((tm, cols), lambda i: (i, colblk))


def _vec_spec(cols):
    return pl.BlockSpec((1, cols), lambda i: (0, 0))


def _ffn_in(xb, win, *, name, tm, plans=()):
    t, d = xb.shape
    nj, w, _ = win.shape
    half = nj // 2
    dn = (((1,), (1,)), ((), ()))

    def body(x_ref, wg_ref, wu_ref, a_ref, gu_ref):
        xv = x_ref[...]
        g = lax.dot_general(xv, wg_ref[...], dn, preferred_element_type=F32)
        u = lax.dot_general(xv, wu_ref[...], dn, preferred_element_type=F32)
        a_ref[...] = (g * _sigmoid(g) * u).astype(BF16)
        gu_ref[0] = g.astype(BF16)
        gu_ref[1] = u.astype(BF16)

    (a, gu), riders = _call_with_plans(
        body, plans, name=name, grid=(half, t // tm),
        in_specs=[pl.BlockSpec((tm, d), lambda j, i: (i, 0)),
                  pl.BlockSpec((None, w, d), lambda j, i: (j, 0, 0)),
                  pl.BlockSpec((None, w, d), lambda j, i: (j + half, 0, 0))],
        out_specs=[pl.BlockSpec((None, tm, w), lambda j, i: (j, i, 0)),
                   pl.BlockSpec((2, None, tm, w), lambda j, i: (0, j, i, 0))],
        out_shape=[jax.ShapeDtypeStruct((half, t, w), BF16), jax.ShapeDtypeStruct((2, half, t, w), BF16)],
        scratch_shapes=[], semantics=("parallel", "parallel"), operands=(xb, win, win))
    return a, gu, riders


def _ffn_out_dg(dfb, wout, gu, *, name, tm, plans=()):
    t, d = dfb.shape
    half, w, _ = wout.shape
    dn = (((1,), (1,)), ((), ()))

    def body(df_ref, w_ref, gu_ref, dh_ref):
        da = lax.dot_general(df_ref[...], w_ref[...], dn, preferred_element_type=F32)
        g = gu_ref[0].astype(F32)
        u = gu_ref[1].astype(F32)
        sg = _sigmoid(g)
        dh_ref[0] = (da * u * (sg * (1.0 + g * (1.0 - sg)))).astype(BF16)
        dh_ref[1] = (da * (g * sg)).astype(BF16)

    (out,), riders = _call_with_plans(
        body, plans, name=name, grid=(half, t // tm),
        in_specs=[pl.BlockSpec((tm, d), lambda j, i: (i, 0)),
                  pl.BlockSpec((None, w, d), lambda j, i: (j, 0, 0)),
                  pl.BlockSpec((2, None, tm, w), lambda j, i: (0, j, i, 0))],
        out_specs=[pl.BlockSpec((2, None, tm, w), lambda j, i: (0, j, i, 0))],
        out_shape=[jax.ShapeDtypeStruct((2, half, t, w), BF16)],
        scratch_shapes=[], semantics=("parallel", "parallel"), operands=(dfb, wout, gu))
    return out.reshape(2 * half, t, w), riders


def _ln_stats(r):
    mu = jnp.mean(r, axis=-1, keepdims=True)
    xc = r - mu
    var = jnp.mean(xc * xc, axis=-1, keepdims=True)
    rstd = lax.rsqrt(var + LN_EPS)
    return xc * rstd, rstd


def _ln_epilogue(xin, gamma, beta, fs):
    def fn(p, xv, g, b):
        r = ALPHA * xv + fs * p
        xh, _ = _ln_stats(r)
        y = xh * g + b
        return r, y, y

    return fn, (xin,), (gamma, beta), (F32, F32, BF16)


def _ln_bwd(xin, fparts, gamma, beta, grads, *, name, fs=1.0, ple=False, target=None, tm=512):
    t, d = xin.shape
    nf = len(fparts)
    ng = len(grads)
    coefs = [c for _, c in grads]
    use_t = target is not None
    n_fout = 2 if ple else 1

    def body(*refs):
        pos = 0
        x_ref = refs[pos]; pos += 1
        f_refs = refs[pos:pos + nf]; pos += nf
        g_ref, b_ref = refs[pos:pos + 2]; pos += 2
        gr_refs = refs[pos:pos + ng]; pos += ng
        if use_t:
            t_ref = refs[pos]; pos += 1
        dr_ref = refs[pos]; pos += 1
        fo_refs = refs[pos:pos + n_fout]; pos += n_fout
        dg_ref, db_ref = refs[pos:pos + 2]; pos += 2
        if use_t:
            loss_ref = refs[pos]; pos += 1
        i = pl.program_id(0)

        if ple:
            pe = f_refs[0][...]
            sg = _sigmoid(f_refs[1][...])
            resid = ALPHA * x_ref[...] + pe * sg
        else:
            resid = x_ref[...]
        xh, rstd = _ln_stats(resid)
        gam = g_ref[...]
        if use_t:
            diff = xh * gam + b_ref[...] - t_ref[...]
            dy = diff * (1.0 / d)
            lpart = 0.5 * jnp.sum(jnp.sum(diff * diff, axis=-1, keepdims=True), axis=0, keepdims=True) * (1.0 / d)
        else:
            dy = coefs[0] * gr_refs[0][...]
            for c, r in zip(coefs[1:], gr_refs[1:]):
                dy = dy + c * r[...]
        dxh = dy * gam
        m1 = jnp.mean(dxh, axis=-1, keepdims=True)
        m2 = jnp.mean(dxh * xh, axis=-1, keepdims=True)
        dr = rstd * (dxh - m1 - xh * m2)
        dr_ref[...] = dr
        if ple:
            fo_refs[0][...] = (dr * sg).astype(BF16)
            fo_refs[1][...] = (dr * pe * (sg * (1.0 - sg))).astype(BF16)
        else:
            fo_refs[0][...] = (fs * dr).astype(BF16)
        dgp = jnp.sum(dy * xh, axis=0, keepdims=True)
        dbp = jnp.sum(dy, axis=0, keepdims=True)

        @pl.when(i == 0)
        def _():
            dg_ref[...] = dgp
            db_ref[...] = dbp
            if use_t:
                loss_ref[...] = jnp.broadcast_to(lpart, loss_ref.shape)

        @pl.when(i > 0)
        def _():
            dg_ref[...] += dgp
            db_ref[...] += dbp
            if use_t:
                loss_ref[...] += jnp.broadcast_to(lpart, loss_ref.shape)

    ins = [xin, *fparts, gamma, beta, *[g for g, _ in grads]] + ([target] if use_t else [])
    in_specs = ([_row_spec(tm, d)] * (1 + nf) + [_vec_spec(d), _vec_spec(d)] + [_row_spec(tm, d)] * ng
                + ([_row_spec(tm, d)] if use_t else []))
    out_specs = [_row_spec(tm, d)] * (1 + n_fout) + [_vec_spec(d), _vec_spec(d)] + ([_vec_spec(128)] if use_t else [])
    out_shape = ([jax.ShapeDtypeStruct((t, d), F32)] + [jax.ShapeDtypeStruct((t, d), BF16)] * n_fout
                 + [jax.ShapeDtypeStruct((1, d), F32)] * 2 + ([jax.ShapeDtypeStruct((1, 128), F32)] if use_t else []))
    return pl.pallas_call(
        body, name=name, grid=(t // tm,), in_specs=in_specs, out_specs=out_specs, out_shape=out_shape,
        compiler_params=_cparams(("arbitrary",)),
    )(*ins)


def _conv_fwd(proj, cw, cb, *, name, tm=512):
    t = proj.shape[0]
    c = CONV_CH
    hb = tm // 8

    def body(b_ref, c_ref, h_ref, cp_ref, hp_ref, w_ref, bias_ref, o_ref, q_scr):
        i = pl.program_id(0)
        q = c_ref[...] * h_ref[...]
        halo = jnp.where(i > 0, cp_ref[...] * hp_ref[...], 0.0)
        q_scr[0:8, :] = halo
        q_scr[8:, :] = q
        z = (w_ref[2:3, :] * q + w_ref[1:2, :] * q_scr[pl.ds(7, tm), :] + w_ref[0:1, :] * q_scr[pl.ds(6, tm), :]
             + bias_ref[...])
        o_ref[...] = (b_ref[...] * z).astype(BF16)

    prev = lambda blk: pl.BlockSpec((8, c), lambda i: (jnp.maximum(i * hb - 1, 0), blk))
    return pl.pallas_call(
        body, name=name, grid=(t // tm,),
        in_specs=[_row_spec(tm, c, 0), _row_spec(tm, c, 1), _row_spec(tm, c, 2), prev(1), prev(2),
                  pl.BlockSpec((8, c), lambda i: (0, 0)), _vec_spec(c)],
        out_specs=_row_spec(tm, c),
        out_shape=jax.ShapeDtypeStruct((t, c), BF16),
        scratch_shapes=[pltpu.VMEM((tm + 8, c), F32)],
        compiler_params=_cparams(("parallel",)),
    )(proj, proj, proj, proj, proj, cw, cb)


def _conv_bwd(dyc, proj, cw, cb, *, name, tm=512):
    t = proj.shape[0]
    c = CONV_CH
    hb = tm // 8
    nblk = t // 8

    def body(d_ref, b_ref, c_ref, h_ref, cp_ref, hp_ref, dn_ref, bn_ref, w_ref, bias_ref,
             db_ref, dc_ref, dh_ref, dw_ref, q_scr, z_scr):
        i = pl.program_id(0)
        last = pl.num_programs(0) - 1
        cc = c_ref[...]
        ch = h_ref[...]
        q = cc * ch
        halo = jnp.where(i > 0, cp_ref[...] * hp_ref[...], 0.0)
        q_scr[0:8, :] = halo
        q_scr[8:, :] = q
        w0, w1, w2 = w_ref[0:1, :], w_ref[1:2, :], w_ref[2:3, :]
        qm1 = q_scr[pl.ds(7, tm), :]
        qm2 = q_scr[pl.ds(6, tm), :]
        z = w2 * q + w1 * qm1 + w0 * qm2 + bias_ref[...]
        d = d_ref[...]
        bb = b_ref[...]
        db_ref[...] = (d * z).astype(BF16)
        dz = d * bb
        z_scr[0:tm, :] = dz
        z_scr[tm:, :] = jnp.where(i < last, dn_ref[...] * bn_ref[...], 0.0)
        dq = w2 * dz + w1 * z_scr[pl.ds(1, tm), :] + w0 * z_scr[pl.ds(2, tm), :]
        dc_ref[...] = (dq * ch).astype(BF16)
        dh_ref[...] = (dq * cc).astype(BF16)
        row = lax.broadcasted_iota(jnp.int32, (8, c), 0)
        part = jnp.zeros((8, c), F32)
        for k, term in enumerate((dz * qm2, dz * qm1, dz * q, dz)):
            part = jnp.where(row == k, jnp.sum(term, axis=0, keepdims=True), part)

        @pl.when(i == 0)
        def _():
            dw_ref[...] = part

        @pl.when(i > 0)
        def _():
            dw_ref[...] += part

    prev = lambda blk: pl.BlockSpec((8, c), lambda i: (jnp.maximum(i * hb - 1, 0), blk))
    nxt_p = pl.BlockSpec((8, c), lambda i: (jnp.minimum((i + 1) * hb, nblk - 1), 0))
    nxt_d = pl.BlockSpec((8, c), lambda i: (jnp.minimum((i + 1) * hb, nblk - 1), 0))
    return pl.pallas_call(
        body, name=name, grid=(t // tm,),
        in_specs=[_row_spec(tm, c), _row_spec(tm, c, 0), _row_spec(tm, c, 1), _row_spec(tm, c, 2),
                  prev(1), prev(2), nxt_d, nxt_p, pl.BlockSpec((8, c), lambda i: (0, 0)), _vec_spec(c)],
        out_specs=[_row_spec(tm, c)] * 3 + [pl.BlockSpec((8, c), lambda i: (0, 0))],
        out_shape=[jax.ShapeDtypeStruct((t, c), BF16)] * 3 + [jax.ShapeDtypeStruct((8, c), F32)],
        scratch_shapes=[pltpu.VMEM((tm + 8, c), F32), pltpu.VMEM((tm + 8, c), F32)],
        compiler_params=_cparams(("arbitrary",)),
    )(dyc, proj, proj, proj, proj, proj, dyc, proj, cw, cb)


def _gate_fwd(yconv, glu, proj, *, name, tm=512):
    t, d = yconv.shape

    def body(yc_ref, ga_ref, gb_ref, gc_ref, gs_ref, o_ref):
        yssm = ga_ref[...] * _sigmoid(gb_ref[...])
        o_ref[...] = (_sigmoid(gc_ref[...]) * yc_ref[...] + _sigmoid(gs_ref[...]) * yssm).astype(BF16)

    return pl.pallas_call(
        body, name=name, grid=(t // tm,),
        in_specs=[_row_spec(tm, d), _row_spec(tm, d, 0), _row_spec(tm, d, 1), _row_spec(tm, d, 2), _row_spec(tm, d, 3)],
        out_specs=_row_spec(tm, d), out_shape=jax.ShapeDtypeStruct((t, d), BF16),
        compiler_params=_cparams(("parallel",)),
    )(yconv, glu, glu, proj, proj)


def _gate_bwd(dm, yconv, glu, proj, *, name, tm=512):
    t, d = yconv.shape

    def body(dm_ref, yc_ref, ga_ref, gb_ref, gc_ref, gs_ref, dyc_ref, dglu_ref, dgate_ref):
        dmv = dm_ref[...]
        sc = _sigmoid(gc_ref[...])
        ss = _sigmoid(gs_ref[...])
        sb = _sigmoid(gb_ref[...])
        ga = ga_ref[...]
        yssm = ga * sb
        dyc_ref[...] = (dmv * sc).astype(BF16)
        dgate_ref[:, 0:d] = (dmv * yc_ref[...] * (sc * (1.0 - sc))).astype(BF16)
        dys = dmv * ss
        dgate_ref[:, d:2 * d] = (dmv * yssm * (ss * (1.0 - ss))).astype(BF16)
        dglu_ref[:, 0:d] = (dys * sb).astype(BF16)
        dglu_ref[:, d:2 * d] = (dys * ga * (sb * (1.0 - sb))).astype(BF16)

    return pl.pallas_call(
        body, name=name, grid=(t // tm,),
        in_specs=[_row_spec(tm, d), _row_spec(tm, d), _row_spec(tm, d, 0), _row_spec(tm, d, 1),
                  _row_spec(tm, d, 2), _row_spec(tm, d, 3)],
        out_specs=[_row_spec(tm, d), _row_spec(tm, 2 * d), _row_spec(tm, 2 * d)],
        out_shape=[jax.ShapeDtypeStruct((t, d), BF16), jax.ShapeDtypeStruct((t, 2 * d), BF16),
                   jax.ShapeDtypeStruct((t, 2 * d), BF16)],
        compiler_params=_cparams(("parallel",)),
    )(dm, yconv, glu, glu, proj, proj)


_GELU_C = math.sqrt(2.0 / math.pi)


def _gelu(v):
    return 0.5 * v * (1.0 + jnp.tanh(_GELU_C * (v + 0.044715 * v * v * v)))


def _gelu_grad(v):
    th = jnp.tanh(_GELU_C * (v + 0.044715 * v * v * v))
    return 0.5 * (1.0 + th) + 0.5 * v * (1.0 - th * th) * (_GELU_C * (1.0 + 3.0 * 0.044715 * v * v))


def _cmul(ar, ai, br, bi):
    return ar * br - ai * bi, ar * bi + ai * br


def _scan_fwd(proj, bmat, tab, *, name, plans=()):
    t = proj.shape[0]
    tt, cbw = SCAN_TT, SCAN_CB
    w2 = 2 * cbw

    def body(u_ref, b_ref, tab_ref, s_ref, sb_ref, bu_scr, carry):
        ti = pl.program_id(1)

        @pl.when(ti == 0)
        def _():
            carry[...] = jnp.zeros_like(carry)

        bu_scr[...] = jnp.dot(u_ref[...].astype(BF16), b_ref[...], preferred_element_type=F32)
        row = lax.broadcasted_iota(jnp.int32, (8, cbw), 0)

        def blk(bi, c):
            cr, ci = c
            r0 = pl.multiple_of(bi * 8, 8)
            xr = bu_scr[pl.ds(r0, 8), 0:cbw]
            xi = bu_scr[pl.ds(r0, 8), cbw:w2]
            for k, sh in enumerate((1, 2, 4)):
                kr = tab_ref[k:k + 1, 0:cbw]
                ki = tab_ref[k:k + 1, cbw:w2]
                sr = jnp.where(row >= sh, pltpu.roll(xr, sh, 0), 0.0)
                si = jnp.where(row >= sh, pltpu.roll(xi, sh, 0), 0.0)
                pr, pi = _cmul(kr, ki, sr, si)
                xr = xr + pr
                xi = xi + pi
            pr, pi = _cmul(tab_ref[8:16, 0:cbw], tab_ref[8:16, cbw:w2], cr, ci)
            xr = xr + pr
            xi = xi + pi
            s_ref[pl.ds(r0, 8), 0:cbw] = xr
            s_ref[pl.ds(r0, 8), cbw:w2] = xi
            return (jnp.broadcast_to(xr[7:8, :], (8, cbw)), jnp.broadcast_to(xi[7:8, :], (8, cbw)))

        cr, ci = lax.fori_loop(0, tt // 8, blk, (carry[:, 0:cbw], carry[:, cbw:w2]))
        carry[:, 0:cbw] = cr
        carry[:, cbw:w2] = ci
        sb_ref[...] = s_ref[...].astype(BF16)

    (s, sb), riders = _call_with_plans(
        body, plans, name=name, grid=(SCAN_NCB, t // tt),
        in_specs=[pl.BlockSpec((tt, SCAN_UW), lambda cb, ti: (ti, 3 * SCAN_NCB + cb)),
                  pl.BlockSpec((None, SCAN_UW, w2), lambda cb, ti: (cb, 0, 0)),
                  pl.BlockSpec((16, w2), lambda cb, ti: (0, cb))],
        out_specs=[pl.BlockSpec((tt, w2), lambda cb, ti: (ti, cb))] * 2,
        out_shape=[jax.ShapeDtypeStruct((t, 2 * SSM_CH), F32), jax.ShapeDtypeStruct((t, 2 * SSM_CH), BF16)],
        scratch_shapes=[pltpu.VMEM((tt, w2), F32), pltpu.VMEM((8, w2), F32)],
        semantics=("parallel", "arbitrary"), operands=(proj, bmat, tab))
    return s, sb, riders


def _scan_bwd(dyb, cmat_t, s, tabb, *, name, plans=()):
    t = s.shape[0]
    tt, cbw = SCAN_TT, SCAN_CB
    w2 = 2 * cbw
    nt = t // tt
    hb = tt // 8

    def body(dy_ref, c_ref, s_ref, sp_ref, tab_ref, h_ref, da_ref, g_scr, s_scr, carry):
        ti = pl.program_id(1)

        @pl.when(ti == 0)
        def _():
            carry[...] = jnp.zeros_like(carry)
            da_ref[...] = jnp.zeros_like(da_ref)

        g_scr[...] = jnp.dot(dy_ref[...], c_ref[...], preferred_element_type=F32)
        s_scr[0:8, :] = jnp.where(ti < nt - 1, sp_ref[...], 0.0)
        s_scr[8:, :] = s_ref[...]
        row = lax.broadcasted_iota(jnp.int32, (8, cbw), 0)

        def blk(k, c):
            cr, ci, ar, ai = c
            bi = hb - 1 - k
            r0 = pl.multiple_of(bi * 8, 8)
            xr = g_scr[pl.ds(r0, 8), 0:cbw]
            xi = g_scr[pl.ds(r0, 8), cbw:w2]
            for j, sh in enumerate((1, 2, 4)):
                kr = tab_ref[j:j + 1, 0:cbw]
                ki = tab_ref[j:j + 1, cbw:w2]
                sr = jnp.where(row < 8 - sh, pltpu.roll(xr, 8 - sh, 0), 0.0)
                si = jnp.where(row < 8 - sh, pltpu.roll(xi, 8 - sh, 0), 0.0)
                pr, pi = _cmul(kr, ki, sr, si)
                xr = xr + pr
                xi = xi + pi
            pr, pi = _cmul(tab_ref[8:16, 0:cbw], tab_ref[8:16, cbw:w2], cr, ci)
            xr = xr + pr
            xi = xi + pi
            h_ref[pl.ds(r0, 8), 0:cbw] = xr.astype(BF16)
            h_ref[pl.ds(r0, 8), cbw:w2] = xi.astype(BF16)
            pvr = s_scr[pl.ds(r0, 8), 0:cbw]
            pvi = s_scr[pl.ds(r0, 8), cbw:w2]
            cur_r = s_scr[pl.ds(r0 + 8, 8), 0:cbw]
            cur_i = s_scr[pl.ds(r0 + 8, 8), cbw:w2]
            spr = jnp.where(row == 0, jnp.broadcast_to(pvr[7:8, :], (8, cbw)), pltpu.roll(cur_r, 1, 0))
            spi = jnp.where(row == 0, jnp.broadcast_to(pvi[7:8, :], (8, cbw)), pltpu.roll(cur_i, 1, 0))
            ar = ar + spr * xr + spi * xi
            ai = ai + spr * xi - spi * xr
            return (jnp.broadcast_to(xr[0:1, :], (8, cbw)), jnp.broadcast_to(xi[0:1, :], (8, cbw)), ar, ai)

        z = jnp.zeros((8, cbw), F32)
        cr, ci, ar, ai = lax.fori_loop(0, hb, blk, (carry[:, 0:cbw], carry[:, cbw:w2], z, z))
        carry[:, 0:cbw] = cr
        carry[:, cbw:w2] = ci
        da_ref[:, 0:cbw] += ar
        da_ref[:, cbw:w2] += ai

    rt = lambda ti: nt - 1 - ti
    (h, da), riders = _call_with_plans(
        body, plans, name=name, grid=(SCAN_NCB, nt),
        in_specs=[pl.BlockSpec((tt, SCAN_UW), lambda cb, ti: (rt(ti), cb)),
                  pl.BlockSpec((None, SCAN_UW, w2), lambda cb, ti: (cb, 0, 0)),
                  pl.BlockSpec((tt, w2), lambda cb, ti: (rt(ti), cb)),
                  pl.BlockSpec((8, w2), lambda cb, ti: (jnp.maximum(rt(ti) * hb - 1, 0), cb)),
                  pl.BlockSpec((16, w2), lambda cb, ti: (0, cb))],
        out_specs=[pl.BlockSpec((tt, w2), lambda cb, ti: (rt(ti), cb)),
                   pl.BlockSpec((8, w2), lambda cb, ti: (0, cb))],
        out_shape=[jax.ShapeDtypeStruct((t, 2 * SSM_CH), BF16), jax.ShapeDtypeStruct((8, 2 * SSM_CH), F32)],
        scratch_shapes=[pltpu.VMEM((tt, w2), F32), pltpu.VMEM((tt + 8, w2), F32), pltpu.VMEM((8, w2), F32)],
        semantics=("parallel", "arbitrary"), operands=(dyb, cmat_t, s, s, tabb))
    return h, da, riders


def _s5_out(ymm, proj, dvec, *, name, tm=512):
    t, w = ymm.shape

    def body(y_ref, u_ref, d_ref, yo_ref, sg_ref, ub_ref):
        u = u_ref[...]
        y = y_ref[...] + d_ref[...] * u
        yo_ref[...] = y
        sg_ref[...] = _gelu(y).astype(BF16)
        ub_ref[...] = u.astype(BF16)

    return pl.pallas_call(
        body, name=name, grid=(t // tm,),
        in_specs=[_row_spec(tm, w), _row_spec(tm, w, 3), _vec_spec(w)],
        out_specs=[_row_spec(tm, w)] * 3,
        out_shape=[jax.ShapeDtypeStruct((t, w), F32), jax.ShapeDtypeStruct((t, w), BF16), jax.ShapeDtypeStruct((t, w), BF16)],
        compiler_params=_cparams(("parallel",)),
    )(ymm, proj, dvec)


def _s5_bwd_in(dsg, y, proj, *, name, tm=512):
    t, w = y.shape

    def body(d_ref, y_ref, u_ref, dy_ref, dyb_ref, dd_ref):
        i = pl.program_id(0)
        dy = d_ref[...] * _gelu_grad(y_ref[...])
        dy_ref[...] = dy
        dyb_ref[...] = dy.astype(BF16)
        part = jnp.sum(dy * u_ref[...], axis=0, keepdims=True)

        @pl.when(i == 0)
        def _():
            dd_ref[...] = part

        @pl.when(i > 0)
        def _():
            dd_ref[...] += part

    return pl.pallas_call(
        body, name=name, grid=(t // tm,),
        in_specs=[_row_spec(tm, w), _row_spec(tm, w), _row_spec(tm, w, 3)],
        out_specs=[_row_spec(tm, w), _row_spec(tm, w), _vec_spec(w)],
        out_shape=[jax.ShapeDtypeStruct((t, w), F32), jax.ShapeDtypeStruct((t, w), BF16), jax.ShapeDtypeStruct((1, w), F32)],
        compiler_params=_cparams(("arbitrary",)),
    )(dsg, y, proj)


def _s5_du(dumm, dy, dvec, *, name, tm=512):
    t, w = dy.shape

    def body(a_ref, dy_ref, d_ref, o_ref):
        o_ref[...] = (a_ref[...] + d_ref[...] * dy_ref[...]).astype(BF16)

    return pl.pallas_call(
        body, name=name, grid=(t // tm,), in_specs=[_row_spec(tm, w), _row_spec(tm, w), _vec_spec(w)],
        out_specs=_row_spec(tm, w), out_shape=jax.ShapeDtypeStruct((t, w), BF16),
        compiler_params=_cparams(("parallel",)),
    )(dumm, dy, dvec)


def _s5_discretise(lam_re, lam_im, log_step, b_re, b_im):
    lam = lax.complex(lam_re, lam_im)
    dt = jnp.exp(log_step)[:, None]
    a = jnp.exp(lam * dt)
    bbar = ((a - 1.0) / lam)[..., None] * lax.complex(b_re, b_im)
    return jnp.real(a), jnp.imag(a), jnp.real(bbar), jnp.imag(bbar)


def _perm_cols(re, im):
    lead = re.shape[:-1]
    r = re.reshape(lead + (SCAN_NCB, 1, SCAN_CB))
    i = im.reshape(lead + (SCAN_NCB, 1, SCAN_CB))
    return jnp.concatenate([r, i], axis=-2).reshape(lead + (2 * SSM_CH,))


def _unperm_cols(x):
    lead = x.shape[:-1]
    y = x.reshape(lead + (SCAN_NCB, 2, SCAN_CB))
    return y[..., 0, :].reshape(lead + (SSM_CH,)), y[..., 1, :].reshape(lead + (SSM_CH,))


def _compact(re, im):
    _, r, c = re.shape
    eye = jnp.eye(SCAN_GPB, dtype=re.dtype)

    def half(x):
        x = x.reshape(SCAN_NCB, SCAN_GPB, r, c)
        return (eye[None, :, None, :, None] * x[:, :, :, None, :]).reshape(SCAN_NCB, SCAN_GPB * r, SCAN_GPB * c)

    return jnp.concatenate([half(re), half(im)], axis=-1)


def _compact_extract(x, r):
    c = SSM_STATE
    eye = jnp.eye(SCAN_GPB, dtype=x.dtype)
    y = x.reshape(SCAN_NCB, SCAN_GPB, r, 2, SCAN_GPB, c)
    dg = jnp.sum(y * eye[None, :, None, None, :, None], axis=4).reshape(SSM_GROUPS, r, 2, c)
    return dg[:, :, 0, :], dg[:, :, 1, :]


def _pow_table(ar, ai, descending=False):
    ar = ar.reshape(1, SSM_CH)
    ai = ai.reshape(1, SSM_CH)
    pw = [(ar, ai)]
    for _ in range(7):
        pw.append(_cmul(pw[-1][0], pw[-1][1], ar, ai))
    zero = (jnp.zeros_like(ar), jnp.zeros_like(ar))
    rows = [pw[0], pw[1], pw[3]] + [zero] * 5 + (pw[::-1] if descending else pw)
    re = jnp.concatenate([r for r, _ in rows], axis=0)
    im = jnp.concatenate([i for _, i in rows], axis=0)
    return _perm_cols(re, im)


def _place():
    x, y, c = lax.axis_index("x"), lax.axis_index("y"), lax.axis_index("c")
    chips = [(1 - x, y), (x, 1 - y), (1 - x, 1 - y)]
    return x, y, c, chips


def _dev(px, py, pc):
    return 4 * px + 2 * py + pc


class _Plan:
    def __init__(self, ins, out_shapes, sem_shapes, start, finish, middle=None):
        self.ins, self.out_shapes, self.sem_shapes = list(ins), list(out_shapes), list(sem_shapes)
        self.start, self.finish, self.middle = start, finish, middle


def _split_plan_refs(plans, in_refs, out_refs, sem_refs):
    res, i, o, s = [], 0, 0, 0
    for p in plans:
        ni, no, ns = len(p.ins), len(p.out_shapes), len(p.sem_shapes)
        res.append((in_refs[i:i + ni], out_refs[o:o + no], sem_refs[s:s + ns]))
        i, o, s = i + ni, o + no, s + ns
    return res


def _run_plans(plans, *, name):
    ins = [a for p in plans for a in p.ins]
    outs = [o for p in plans for o in p.out_shapes]
    sems = [s for p in plans for s in p.sem_shapes]
    any_spec = pl.BlockSpec(memory_space=pl.ANY)

    def body(*refs):
        parts = _split_plan_refs(plans, refs[:len(ins)], refs[len(ins):len(ins) + len(outs)], refs[len(ins) + len(outs):])
        for p, r in zip(plans, parts):
            p.start(*r)
        for p, r in zip(plans, parts):
            if p.middle:
                p.middle(*r)
        for p, r in zip(plans, parts):
            p.finish(*r)

    res = pl.pallas_call(body, name=name, in_specs=[any_spec] * len(ins), out_specs=[any_spec] * len(outs),
                         out_shape=outs, scratch_shapes=sems)(*ins)
    return _split_plan_refs(plans, [], res, [])


def _run_plans_on_sequencer(plans, peers_of, *, name, collective_id, after=()):
    ins = [a for p in plans for a in p.ins]
    outs = [o for p in plans for o in p.out_shapes]
    sems = [s for p in plans for s in p.sem_shapes]

    def body(*refs):
        x, y, c, chips = _place()
        peers = peers_of(x, y, c, chips)
        barrier = pltpu.get_barrier_semaphore()
        for peer in peers:
            pl.semaphore_signal(barrier, inc=1, device_id=peer, device_id_type=MESH)
        pl.semaphore_wait(barrier, len(peers))
        n_in = len(ins) + len(after)
        parts = _split_plan_refs(plans, refs[:len(ins)], refs[n_in:n_in + len(outs)], refs[n_in + len(outs):])
        for p, r in zip(plans, parts):
            p.start(*r)
        for p, r in zip(plans, parts):
            if p.middle:
                p.middle(*r)
        for p, r in zip(plans, parts):
            p.finish(*r)

    res = pl.kernel(body, name=name, out_type=outs, mesh=plsc.ScalarSubcoreMesh(axis_name="seq", num_cores=1),
                    scratch_types=sems, compiler_params=pltpu.CompilerParams(collective_id=collective_id))(*ins, *after)
    return _split_plan_refs(plans, [], list(res), [])


def _call_with_plans(body, plans, *, name, grid, in_specs, out_specs, out_shape, scratch_shapes, semantics, operands):
    plans = list(plans)
    if not plans:
        res = pl.pallas_call(body, name=name, grid=grid, in_specs=in_specs, out_specs=out_specs, out_shape=out_shape,
                             scratch_shapes=scratch_shapes, compiler_params=_cparams(semantics))(*operands)
        return list(res), []
    n_in, n_out, n_scr = len(in_specs), len(out_specs), len(scratch_shapes)
    p_ins = [a for p in plans for a in p.ins]
    p_outs = [o for p in plans for o in p.out_shapes]
    p_sems = [s for p in plans for s in p.sem_shapes]
    nsteps = math.prod(grid)
    any_spec = pl.BlockSpec(memory_space=pl.ANY)

    def wrapped(*refs):
        bounds = [n_in, len(p_ins), n_out, len(p_outs), n_scr]
        parts, pos = [], 0
        for b in bounds:
            parts.append(refs[pos:pos + b])
            pos += b
        ins, p_in, outs, p_out, scr = parts
        step = pl.program_id(0)
        for ax in range(1, len(grid)):
            step = step * grid[ax] + pl.program_id(ax)
        riders = _split_plan_refs(plans, p_in, p_out, refs[pos:])

        @pl.when(step == 0)
        def _():
            for p, r in zip(plans, riders):
                p.start(*r)

        mids = [(p, r) for p, r in zip(plans, riders) if p.middle]
        mid_step = nsteps // 2
        split_mid = mids and 0 < mid_step < nsteps - 1
        if split_mid:
            @pl.when(step == mid_step)
            def _():
                for p, r in mids:
                    p.middle(*r)

        body(*ins, *outs, *scr)

        @pl.when(step == nsteps - 1)
        def _():
            if not split_mid:
                for p, r in mids:
                    p.middle(*r)
            for p, r in zip(plans, riders):
                p.finish(*r)

    res = pl.pallas_call(
        wrapped, name=name, grid=grid, in_specs=list(in_specs) + [any_spec] * len(p_ins),
        out_specs=list(out_specs) + [any_spec] * len(p_outs), out_shape=list(out_shape) + p_outs,
        scratch_shapes=list(scratch_shapes) + p_sems, compiler_params=_cparams(("arbitrary",) * len(grid)),
    )(*operands, *p_ins)
    return list(res[:n_out]), [r[1] for r in _split_plan_refs(plans, [], res[n_out:], [])]


def _gather_plan(shards):
    n = len(shards)
    nk = 8

    def make(ins, outs, sems):
        send, recv, lsem = sems
        x, y, c, _ = _place()
        me, sib, xn, yn, dg = (x, y, c), (x, y, 1 - c), (1 - x, y, c), (x, 1 - y, c), (1 - x, 1 - y, c)

        def part(w, block, half):
            ref = outs[w].at[_dev(*block)]
            if half is None:
                return ref
            rows = shards[w].shape[0] // 2
            return ref.at[pl.ds(half * rows, rows)]

        def copy(w, k, block, to, half=None, src=None):
            dst = part(w, block, half)
            return pltpu.make_async_remote_copy(
                src_ref=dst if src is None else src, dst_ref=dst,
                send_sem=send.at[w * nk + k], recv_sem=recv.at[w * nk + k], device_id=to, device_id_type=MESH)

        mine = [pltpu.make_async_copy(ins[w], outs[w].at[_dev(*me)], lsem.at[w]) for w in range(n)]
        return copy, mine, me, sib, xn, yn, dg

    def first_copies(copy, me, sib, xn, yn, ins):
        return [copy(w, k, me, to, src=ins[w]) for w in range(n) for k, to in ((0, sib), (1, xn), (2, yn))]

    def start(ins, outs, sems):
        copy, mine, me, sib, xn, yn, _ = make(ins, outs, sems)
        for cp in mine + first_copies(copy, me, sib, xn, yn, ins):
            cp.start()

    def middle(ins, outs, sems):
        copy, _, me, sib, xn, yn, _ = make(ins, outs, sems)
        for w in range(n):
            copy(w, 1, xn, me).wait_recv()
            copy(w, 3, xn, yn, half=0).start()
            copy(w, 5, xn, sib).start()
        for w in range(n):
            copy(w, 2, yn, me).wait_recv()
            copy(w, 4, yn, xn, half=1).start()
            copy(w, 6, yn, sib).start()

    def finish(ins, outs, sems):
        copy, mine, me, sib, xn, yn, dg = make(ins, outs, sems)
        last = []
        for w in range(n):
            copy(w, 3, dg, me, half=0).wait_recv()
            copy(w, 4, dg, me, half=1).wait_recv()
            fwd = copy(w, 7, dg, sib)
            fwd.start()
            last.append(fwd)
        sx, sy, sd = (1 - me[0], me[1], 1 - me[2]), (me[0], 1 - me[1], 1 - me[2]), (1 - me[0], 1 - me[1], 1 - me[2])
        for w in range(n):
            copy(w, 0, sib, me).wait_recv()
            copy(w, 5, sx, me).wait_recv()
            copy(w, 6, sy, me).wait_recv()
            copy(w, 7, sd, me).wait_recv()
        for cp in first_copies(copy, me, sib, xn, yn, ins) + last:
            cp.wait_send()
        for w in range(n):
            copy(w, 3, xn, yn, half=0).wait_send()
            copy(w, 5, xn, sib).wait_send()
            copy(w, 4, yn, xn, half=1).wait_send()
            copy(w, 6, yn, sib).wait_send()
        for cp in mine:
            cp.wait()

    return _Plan(shards, [jax.ShapeDtypeStruct((N_DEV,) + s.shape, s.dtype) for s in shards],
                 [pltpu.SemaphoreType.DMA((nk * n,)), pltpu.SemaphoreType.DMA((nk * n,)), pltpu.SemaphoreType.DMA((n,))],
                 start, finish, middle)


def _swap_plan(copies_of, n_copies, ins, out_shapes):
    def cps(in_refs, out_refs, sems):
        return copies_of(in_refs, out_refs, sems[0], sems[1])

    def start(in_refs, out_refs, sems):
        for cp in cps(in_refs, out_refs, sems):
            cp.start()

    def finish(in_refs, out_refs, sems):
        all_cps = cps(in_refs, out_refs, sems)
        for cp in all_cps:
            cp.wait_recv()
        for cp in all_cps:
            cp.wait_send()

    return _Plan(ins, out_shapes, [pltpu.SemaphoreType.DMA((n_copies,)), pltpu.SemaphoreType.DMA((n_copies,))],
                 start, finish)


def _sibling_plan(grads):
    n = len(grads)

    def copies(ins, outs, send, recv):
        x, y, c, chips = _place()
        owners = [(x, y)] + chips
        return [pltpu.make_async_remote_copy(
            src_ref=ins[w].at[_dev(*chip, 1 - c)], dst_ref=outs[w].at[k], send_sem=send.at[w * 4 + k],
            recv_sem=recv.at[w * 4 + k], device_id=(x, y, 1 - c), device_id_type=MESH)
            for w in range(n) for k, chip in enumerate(owners)]

    return _swap_plan(copies, 4 * n, grads, [jax.ShapeDtypeStruct((4,) + g.shape[1:], g.dtype) for g in grads])


def _chip_plan(parts, js=(0, 1, 2)):
    n, nj = len(parts), len(js)

    def copies(ins, outs, send, recv):
        x, y, c, chips = _place()
        return [pltpu.make_async_remote_copy(
            src_ref=ins[w].at[j], dst_ref=outs[w * nj + k], send_sem=send.at[w * nj + k],
            recv_sem=recv.at[w * nj + k], device_id=(*chips[j], c), device_id_type=MESH)
            for w in range(n) for k, j in enumerate(js)]

    return _swap_plan(copies, n * nj, parts,
                      [jax.ShapeDtypeStruct(p.shape[1:], p.dtype) for p in parts for _ in js])


UPDATE_TILE_BYTES = 1536 * 1024


def _row_tile(r, c):
    best = 8
    for t in range(8, r + 1, 8):
        if r % t == 0 and t * c * 4 <= UPDATE_TILE_BYTES:
            best = t
    return best


def _chip_partial(g, sib, ids, *, name):
    _, r, c = g.shape
    tr = _row_tile(r, c)

    def body(ids_ref, g_ref, s_ref, o_ref):
        o_ref[...] = (g_ref[...] + s_ref[...]).astype(BF16)

    return pl.pallas_call(
        body, name=name,
        grid_spec=pltpu.PrefetchScalarGridSpec(
            num_scalar_prefetch=1, grid=(3, r // tr),
            in_specs=[pl.BlockSpec((None, tr, c), lambda j, i, ids_ref: (ids_ref[j], i, 0)),
                      pl.BlockSpec((None, tr, c), lambda j, i, ids_ref: (j + 1, i, 0))],
            out_specs=pl.BlockSpec((None, tr, c), lambda j, i, ids_ref: (j, i, 0))),
        out_shape=jax.ShapeDtypeStruct((3, r, c), BF16),
        compiler_params=_cparams(("parallel", "parallel")),
    )(ids, g, sib)


def _adamw_math(w, g, m, v):
    m = ADAM_B1 * m + (1.0 - ADAM_B1) * g
    v = ADAM_B2 * v + (1.0 - ADAM_B2) * (g * g)
    m_hat = m / (1.0 - ADAM_B1 ** ADAM_STEP)
    v_hat = v / (1.0 - ADAM_B2 ** ADAM_STEP)
    delta = -ADAM_LR * (m_hat / (jnp.sqrt(v_hat) + ADAM_EPS) + ADAM_WD * w)
    return delta, m, v


def _shard_update(g, sib, rem, me, w, m, v, *, name):
    r, c = w.shape
    tr = _row_tile(r, c)

    def body(me_ref, g_ref, s_ref, r0_ref, r1_ref, r2_ref, w_ref, m_ref, v_ref, go_ref, d_ref, mo_ref, vo_ref):
        gt = g_ref[...] + s_ref[...]
        gt = gt + r0_ref[...].astype(F32)
        gt = gt + r1_ref[...].astype(F32)
        gt = gt + r2_ref[...].astype(F32)
        go_ref[...] = gt
        d, mn, vn = _adamw_math(w_ref[...], gt, m_ref[...], v_ref[...])
        d_ref[...] = d
        mo_ref[...] = mn
        vo_ref[...] = vn

    blk = lambda k: pl.BlockSpec((None, tr, c), lambda i, me_ref: (k, i, 0))
    plain = pl.BlockSpec((tr, c), lambda i, me_ref: (i, 0))
    return pl.pallas_call(
        body, name=name,
        grid_spec=pltpu.PrefetchScalarGridSpec(
            num_scalar_prefetch=1, grid=(r // tr,),
            in_specs=[pl.BlockSpec((None, tr, c), lambda i, me_ref: (me_ref[0], i, 0)), blk(0), plain, plain, plain,
                      plain, plain, plain],
            out_specs=[plain] * 4),
        out_shape=[jax.ShapeDtypeStruct((r, c), F32)] * 4,
        compiler_params=_cparams(("parallel",)),
    )(me, g, sib, *rem, w, m, v)


def _small_update(gathered, w, m, v, *, name):
    _, r, c = gathered.shape

    def body(g_ref, w_ref, m_ref, v_ref, go_ref, d_ref, mo_ref, vo_ref):
        gt = g_ref[0]
        for k in range(1, N_DEV):
            gt = gt + g_ref[k]
        go_ref[...] = gt
        d, mn, vn = _adamw_math(w_ref[...], gt, m_ref[...], v_ref[...])
        d_ref[...] = d
        mo_ref[...] = mn
        vo_ref[...] = vn

    return pl.pallas_call(
        body, name=name, out_shape=[jax.ShapeDtypeStruct((r, c), F32)] * 4,
        compiler_params=pltpu.CompilerParams(vmem_limit_bytes=VMEM_LIMIT),
    )(gathered, w, m, v)


SMALL_UNIT = 1024


def _pack(parts):
    flat = []
    for p in parts:
        f = p.reshape(-1).astype(F32)
        pad = (-f.shape[0]) % SMALL_UNIT
        flat.append(jnp.pad(f, (0, pad)))
    return jnp.concatenate(flat).reshape(-1, 128)


def _unpack(buf, shapes):
    flat = buf.reshape(-1)
    out, off = [], 0
    for s in shapes:
        nel = math.prod(s)
        out.append(flat[off:off + nel].reshape(s))
        off += nel + ((-nel) % SMALL_UNIT)
    return out


def kernel(x, p, ffn1_w_in, ffn1_w_out, ln1_g, ln1_b, mix_w_in, conv_w, conv_b, conv_w_out, ssm_lam_re, ssm_lam_im, ssm_log_step, ssm_b_re, ssm_b_im, ssm_c_re, ssm_c_im, ssm_d, ssm_w_glu, mix_w_out, ln2_g, ln2_b, ffn2_w_in, ffn2_w_out, ln3_g, ln3_b, ple_w_in, ple_w_gate, ln4_g, ln4_b, loss_target, m_ffn1_w_in, m_ffn1_w_out, m_ln1_g, m_ln1_b, m_mix_w_in, m_conv_w, m_conv_b, m_conv_w_out, m_ssm_lam_re, m_ssm_lam_im, m_ssm_log_step, m_ssm_b_re, m_ssm_b_im, m_ssm_c_re, m_ssm_c_im, m_ssm_d, m_ssm_w_glu, m_mix_w_out, m_ln2_g, m_ln2_b, m_ffn2_w_in, m_ffn2_w_out, m_ln3_g, m_ln3_b, m_ple_w_in, m_ple_w_gate, m_ln4_g, m_ln4_b, v_ffn1_w_in, v_ffn1_w_out, v_ln1_g, v_ln1_b, v_mix_w_in, v_conv_w, v_conv_b, v_conv_w_out, v_ssm_lam_re, v_ssm_lam_im, v_ssm_log_step, v_ssm_b_re, v_ssm_b_im, v_ssm_c_re, v_ssm_c_im, v_ssm_d, v_ssm_w_glu, v_mix_w_out, v_ln2_g, v_ln2_b, v_ffn2_w_in, v_ffn2_w_out, v_ln3_g, v_ln3_b, v_ple_w_in, v_ple_w_gate, v_ln4_g, v_ln4_b):
    args = dict(locals())
    big = ['ffn1_w_in', 'ffn1_w_out', 'mix_w_in', 'conv_w_out', 'ssm_w_glu', 'mix_w_out',
           'ffn2_w_in', 'ffn2_w_out', 'ple_w_in', 'ple_w_gate']
    small = ['ln1_g', 'ln1_b', 'conv_b', 'ssm_lam_re', 'ssm_lam_im', 'ssm_log_step', 'ssm_b_re', 'ssm_b_im',
             'ssm_c_re', 'ssm_c_im', 'ssm_d', 'ln2_g', 'ln2_b', 'ln3_g', 'ln3_b', 'ln4_g', 'ln4_b']
    order = ['ffn1_w_in', 'ffn1_w_out', 'ln1_g', 'ln1_b', 'mix_w_in', 'conv_w', 'conv_b', 'conv_w_out',
             'ssm_lam_re', 'ssm_lam_im', 'ssm_log_step', 'ssm_b_re', 'ssm_b_im', 'ssm_c_re', 'ssm_c_im', 'ssm_d',
             'ssm_w_glu', 'mix_w_out', 'ln2_g', 'ln2_b', 'ffn2_w_in', 'ffn2_w_out', 'ln3_g', 'ln3_b',
             'ple_w_in', 'ple_w_gate', 'ln4_g', 'ln4_b']

    t = x.shape[1]
    d = x.shape[2]
    xc_, yc_, cc_ = lax.axis_index("x"), lax.axis_index("y"), lax.axis_index("c")
    me = (4 * xc_ + 2 * yc_ + cc_).astype(jnp.int32)
    cw_cols = conv_w.shape[2]

    turned = ('ffn1_w_in', 'ffn2_w_in')

    def local(a, nm):
        return jnp.swapaxes(a[0], 0, 1) if nm in turned else a[0]

    shard = {nm: local(args[nm], nm).astype(BF16) for nm in big}
    cw_pad = jnp.zeros((16, 128), F32).at[0:3, 0:cw_cols].set(conv_w[0])
    wf = shard['ffn1_w_in'].shape[0]

    issued = []

    def gather(arrays, tag, after=()):
        ((_, got, _),) = _run_plans_on_sequencer(
            [_gather_plan(arrays)], lambda x, y, c, chips: [(x, y, 1 - c), (1 - x, y, c), (x, 1 - y, c)],
            name=f"gather_{tag}", collective_id=2, after=tuple(after) + tuple(issued[-1:]))
        issued.append(got[0])
        return got

    w1in, cw_g = gather([shard['ffn1_w_in'], cw_pad], "ffn1_in")
    (w1out_g,) = gather([shard['ffn1_w_out']], "ffn1_out")
    (wmix,) = gather([shard['mix_w_in']], "mix_in")
    wco, wglu, wmo_g = gather([shard[nm] for nm in ('conv_w_out', 'ssm_w_glu', 'mix_w_out')], "mix_rest")
    (w2in,) = gather([shard['ffn2_w_in']], "ffn2_in")
    (w2out_g,) = gather([shard['ffn2_w_out']], "ffn2_out")
    wpin, wgate_g = gather([shard['ple_w_in'], shard['ple_w_gate']], "ple")
    cw_full = jnp.transpose(cw_g[:, 0:3, 0:cw_cols], (1, 0, 2)).reshape(3, N_DEV * cw_cols)
    cw8 = jnp.zeros((8, CONV_CH), F32).at[0:3, :].set(cw_full)

    s5_in = (ssm_lam_re[0], ssm_lam_im[0], ssm_log_step[0], ssm_b_re[0], ssm_b_im[0])
    (a_re, a_im, bb_re, bb_im), s5_vjp = jax.vjp(_s5_discretise, *s5_in)
    tab_f = _pow_table(a_re, a_im)
    tab_b = _pow_table(a_re, -a_im, descending=True)
    bmat_b = _compact(jnp.transpose(bb_re, (0, 2, 1)), jnp.transpose(bb_im, (0, 2, 1))).astype(BF16)
    cmat_tb = _compact(ssm_c_re[0], -ssm_c_im[0]).astype(BF16)
    dvec = ssm_d[0].reshape(1, SSM_W)

    xf = x[0]
    x_b = xf.astype(BF16)
    p_b = p[0, 0].astype(BF16)
    tgt = loss_target[0]
    tq = min(512, t)

    ffn_out = dict(ja='c', jb='c', nj=4, tm=tq, tn=d, tk=wf)
    def side_by_side(wb):
        return jnp.transpose(wb, (1, 0, 2)).reshape(wb.shape[1], N_DEV * wb.shape[2])

    tf = min(1024, t)
    a1, h1, _ = _ffn_in(x_b, w1in, name="ffn1_in", tm=tf)
    w1out = w1out_g.reshape(4, wf, d)
    r1, x1, x1b = _mm(a1, w1out, name="ffn1_out", **ffn_out, epilogue=_ln_epilogue(xf, ln1_g, ln1_b, 0.5))
    proj = _mm(x1b, wmix, name="mix_in", jb='b', jo='b', o_flat=True, nj=8, tm=tq, tn=512, tk=d)
    wmo = wmo_g.reshape(d, d)
    ycin = _conv_fwd(proj, cw8, conv_b, name="conv_fwd")
    wco, wglu = side_by_side(wco), side_by_side(wglu)
    yconv = _mm(ycin, wco, name="conv_out", tm=tq, tn=d, tk=CONV_CH)
    s_f, s_b, _ = _scan_fwd(proj, bmat_b, tab_f, name="scan_fwd")
    blk = dict(ja='b', jb='b', jo='b', nj=SCAN_NCB)
    ymm = _mm(s_b, cmat_tb, name="ssm_read", a_flat=True, o_flat=True, tb=True, tm=tq, tn=SCAN_UW, tk=2 * SCAN_CB, **blk)
    ys, sg, u_b = _s5_out(ymm, proj, dvec, name="ssm_out")
    glu = _mm(sg, wglu, name="glu_in", tm=tq, tn=d, tk=SSM_W)
    merged = _gate_fwd(yconv, glu, proj, name="gate_fwd")
    r2, x2, x2b = _mm(merged, wmo, name="mix_out", tm=tq, tn=d, tk=d, epilogue=_ln_epilogue(x1, ln2_g, ln2_b, 1.0))
    a2, h2, _ = _ffn_in(x2b, w2in, name="ffn2_in", tm=tf)
    w2out = w2out_g.reshape(4, wf, d)
    r3, x3, x3b = _mm(a2, w2out, name="ffn2_out", **ffn_out, epilogue=_ln_epilogue(x2, ln3_g, ln3_b, 0.5))
    wgate = wgate_g.reshape(d, d)
    pe = _mm(p_b, side_by_side(wpin), name="ple_in", tm=tq, tn=d, tk=p_b.shape[1])
    gp = _mm(x3b, wgate, name="ple_gate", tm=tq, tn=d, tk=d)

    dr4, dpe_b, dgp_b, dg4, db4, loss_part = _ln_bwd(x3, [pe, gp], ln4_g, ln4_b, [], name="ple_ln_bwd",
                                                     ple=True, target=tgt)
    gb, sib, rem = {}, {}, {}
    ids = jnp.stack([_dev(1 - xc_, yc_, cc_), _dev(xc_, 1 - yc_, cc_), _dev(1 - xc_, 1 - yc_, cc_)]).astype(jnp.int32)

    def blocked(nm, g):
        return g.reshape((N_DEV,) + args[nm].shape[1:])

    def to_sibling(*names):
        return _sibling_plan([gb[nm] for nm in names])

    def chip_sums(names, sibs):
        for nm, s in zip(names, sibs):
            sib[nm] = s
        return [_chip_partial(gb[nm], sib[nm], ids, name=f"chip_sum_{nm}") for nm in names]

    def on_sequencer(plan, peers, tag, cid, after=()):
        ((_, got, _),) = _run_plans_on_sequencer([plan], peers, name=tag, collective_id=cid,
                                                 after=tuple(after) + tuple(issued[-1:]))
        issued.append(got[0])
        return got

    waiting = []

    def send_sibling(*names, after=()):
        got = on_sequencer(to_sibling(*names), lambda x, y, c, chips: [(x, y, 1 - c)], f"grad_sibling_{names[0]}", 3,
                           after=after)
        waiting.append((names, got))

    def send_chips(count=None, after=()):
        pin = ()
        for _ in range(len(waiting) if count is None else count):
            names, got = waiting.pop(0)
            parts = chip_sums(names, got)
            r_ = on_sequencer(_chip_plan(parts), lambda x, y, c, chips: [(*chip, c) for chip in chips],
                              f"grad_chips_{names[0]}", 1, after=after)
            for i, nm in enumerate(names):
                rem[nm] = r_[3 * i:3 * i + 3]
            pin = (parts[0],)
        return pin

    ffn_in_dg = dict(ja='c', jb='c', nj=8, tm=tq, tn=d, tk=wf)
    ffn_in_wg = dict(ja='b', jo='b', ta=True, nj=8, tm=wf, tn=d, tk=t)
    ffn_out_wg = dict(ja='b', jo='b', ta=True, nj=4, tm=wf, tn=d, tk=t)

    gb['ple_w_in'] = _mm(p_b, dpe_b, name="ple_in_wg", jb='b', jo='b', b_flat=True, ta=True, nj=8,
                         tm=p_b.shape[1], tn=128, tk=t)
    gb['ple_w_gate'] = blocked('ple_w_gate', _mm(x3b, dgp_b, name="ple_gate_wg", ta=True, tm=d, tn=d, tk=t))
    g_ple = ('ple_w_in', 'ple_w_gate')
    dx3_g, (s_,) = _mm(dgp_b, wgate, name="ple_gate_dg", tb=True, tm=tq, tn=d, tk=d, plans=[to_sibling(*g_ple)])
    waiting.append((g_ple, s_))

    dr3, df2_b, dg3, db3 = _ln_bwd(r3, [], ln3_g, ln3_b, [(dr4, ALPHA), (dx3_g, 1.0)], name="ffn2_ln_bwd", fs=0.5)
    pin = send_chips()
    dh2, _ = _ffn_out_dg(df2_b, w2out, h2, name="ffn2_out_dg", tm=tf)
    gb['ffn2_w_out'] = blocked('ffn2_w_out', _mm(a2, df2_b, name="ffn2_out_wg", **ffn_out_wg, after=pin))
    send_sibling('ffn2_w_out')
    dx2_f = _mm(dh2, w2in, name="ffn2_in_dg", **ffn_in_dg)
    pin = send_chips()
    gb['ffn2_w_in'] = _mm(dh2, x2b, name="ffn2_in_wg", **ffn_in_wg, after=pin)

    dr2, dmix_b, dg2, db2 = _ln_bwd(r2, [], ln2_g, ln2_b, [(dr3, ALPHA), (dx2_f, 1.0)], name="mix_ln_bwd")
    dmerged = _mm(dmix_b, wmo, name="mix_out_dg", tb=True, tm=tq, tn=d, tk=d)
    send_sibling('ffn2_w_in', after=(dmerged,))
    gb['mix_w_out'] = blocked('mix_w_out', _mm(merged, dmix_b, name="mix_out_wg", ta=True, tm=d, tn=d, tk=t))
    dyconv_b, dglu_b, dgate_b = _gate_bwd(dmerged, yconv, glu, proj, name="gate_bwd")
    pin = send_chips()
    gb['conv_w_out'] = _mm(ycin, dyconv_b, name="conv_out_wg", jb='b', jo='b', b_flat=True, ta=True, nj=8,
                           tm=CONV_CH, tn=128, tk=t, after=pin)
    dycin = _mm(dyconv_b, wco, name="conv_out_dg", tb=True, tm=tq, tn=CONV_CH, tk=d)
    gb['ssm_w_glu'] = _mm(sg, dglu_b, name="glu_in_wg", jb='b', jo='b', b_flat=True, ta=True, nj=8,
                          tm=SSM_W, tn=256, tk=t)
    dsg = _mm(dglu_b, wglu, name="glu_in_dg", tb=True, tm=tq, tn=SSM_W, tk=2 * d)
    dys, dys_b, dd = _s5_bwd_in(dsg, ys, proj, name="ssm_out_bwd")
    h_b, da_acc, _ = _scan_bwd(dys_b, cmat_tb, s_f, tab_b, name="scan_bwd")
    send_sibling('mix_w_out', 'conv_w_out', 'ssm_w_glu', after=(h_b,))
    dumm = _mm(h_b, bmat_b, name="ssm_write_dg", a_flat=True, o_flat=True, tb=True, tm=tq, tn=SCAN_UW,
               tk=2 * SCAN_CB, **blk)
    du_b = _s5_du(dumm, dys, dvec, name="ssm_du")
    g_bmat = _mm(u_b, h_b, name="ssm_write_wg", a_flat=True, b_flat=True, ta=True, tm=SCAN_UW,
                 tn=2 * SCAN_CB, tk=t, **blk)
    pin = send_chips()
    g_cmat = _mm(dys_b, s_b, name="ssm_read_wg", a_flat=True, b_flat=True, ta=True, tm=SCAN_UW,
                 tn=2 * SCAN_CB, tk=t, **blk, after=pin)
    dcb_b, dcc_b, dch_b, dconv = _conv_bwd(dycin, proj, cw8, conv_b, name="conv_bwd")
    dproj = jnp.concatenate([dcb_b, dcc_b, dch_b, du_b, dgate_b], axis=1)
    gb['mix_w_in'] = _mm(x1b, dproj, name="mix_in_wg", jb='b', jo='b', b_flat=True, ta=True, nj=8,
                         tm=d, tn=512, tk=t)
    send_sibling('mix_w_in')
    dx1_m = _mm(dproj, wmix, name="mix_in_dg", ja='c', jb='c', a_flat=True, tb=True, nj=8, tm=tq, tn=d, tk=512)
    pin_mix = send_chips()

    dr1, df1_b, dg1, db1 = _ln_bwd(r1, [], ln1_g, ln1_b, [(dr2, ALPHA), (dx1_m, 1.0)], name="ffn1_ln_bwd", fs=0.5)
    dh1, _ = _ffn_out_dg(df1_b, w1out, h1, name="ffn1_out_dg", tm=tf)

    da_sum = jnp.sum(da_acc, axis=0)
    da_re, da_im = _unperm_cols(da_sum)
    gbb_re, gbb_im = [jnp.transpose(v, (0, 2, 1)) for v in _compact_extract(g_bmat, SSM_GROUP)]
    g_c_re, g_c_im_neg = _compact_extract(g_cmat, SSM_GROUP)
    g_c_im = -g_c_im_neg
    g_lam_re, g_lam_im, g_log_step, g_b_re, g_b_im = s5_vjp(
        (da_re.reshape(SSM_GROUPS, SSM_STATE), da_im.reshape(SSM_GROUPS, SSM_STATE), gbb_re, gbb_im))
    g_d = dd.reshape(SSM_GROUPS, SSM_GROUP)

    small_g = {'ln1_g': dg1, 'ln1_b': db1, 'conv_b': dconv[3:4], 'ssm_lam_re': g_lam_re, 'ssm_lam_im': g_lam_im,
               'ssm_log_step': g_log_step, 'ssm_b_re': g_b_re, 'ssm_b_im': g_b_im, 'ssm_c_re': g_c_re,
               'ssm_c_im': g_c_im, 'ssm_d': g_d, 'ln2_g': dg2, 'ln2_b': db2, 'ln3_g': dg3, 'ln3_b': db3,
               'ln4_g': dg4, 'ln4_b': db4}
    small_shapes = [args[nm].shape for nm in small] + [(3, CONV_CH), (1,)]
    g_pack = _pack([small_g[nm] for nm in small] + [dconv[0:3], loss_part[0:1, 0:1]])

    res = {}
    me1 = me.reshape(1)

    def update(nm):
        upd = _shard_update(gb[nm], sib[nm], rem[nm], me1, local(args[nm], nm), local(args['m_' + nm], nm),
                            local(args['v_' + nm], nm), name=f"update_{nm}")
        for key, val in zip(('grad_', 'delta_', 'new_m_', 'new_v_'), upd):
            res[key + nm] = (jnp.swapaxes(val, 0, 1) if nm in turned else val)[None]

    (g_all,) = gather([g_pack], "small", after=(dh1,))
    gb['ffn1_w_in'] = _mm(dh1, x_b, name="ffn1_in_wg", **ffn_in_wg, after=pin_mix)
    send_sibling('ffn1_w_in')
    gb['ffn1_w_out'] = blocked('ffn1_w_out', _mm(a1, df1_b, name="ffn1_out_wg", **ffn_out_wg))
    send_sibling('ffn1_w_out')
    pin = send_chips(1)
    (grad_x,) = _mm(dh1, w1in, name="ffn1_in_dg", **ffn_in_dg, after=pin,
                    epilogue=(lambda pr, drv: (pr + ALPHA * drv,), (dr1,), (), (F32,)))
    last = ['ffn1_w_in', 'ffn1_w_out']
    for nm in big:
        if nm not in last:
            update(nm)
    send_chips(after=(res['new_v_ffn2_w_out'],))

    def full_cw(a):
        return lax.dynamic_update_slice(jnp.zeros((3, CONV_CH), F32), a[0], (0, me * cw_cols))

    zero1 = jnp.zeros((1,), F32)
    w_pack = _pack([args[nm] for nm in small] + [full_cw(conv_w), zero1])
    m_pack = _pack([args['m_' + nm] for nm in small] + [full_cw(m_conv_w), zero1])
    v_pack = _pack([args['v_' + nm] for nm in small] + [full_cw(v_conv_w), zero1])
    sg_sum, sd, sm, sv = _small_update(g_all, w_pack, m_pack, v_pack, name="small_update")
    for key, buf in (('grad_', sg_sum), ('delta_', sd), ('new_m_', sm), ('new_v_', sv)):
        parts = _unpack(buf, small_shapes)
        for nm, val in zip(small, parts[:len(small)]):
            res[key + nm] = val
        res[key + 'conv_w'] = lax.dynamic_slice(parts[len(small)], (0, me * cw_cols), (3, cw_cols))[None]
        if key == 'grad_':
            loss = parts[-1][0]

    for nm in last:
        update(nm)

    outs = [loss, grad_x[None]]
    for key in ('grad_', 'delta_', 'new_m_', 'new_v_'):
        outs += [res[key + nm] for nm in order]
    return tuple(outs)
```

```python
import functools
import math

import jax
import jax.numpy as jnp
from jax import lax
from jax.experimental import pallas as pl
from jax.experimental.pallas import tpu as pltpu
from jax.experimental.pallas import tpu_sc as plsc

F32 = jnp.float32
BF16 = jnp.bfloat16
MESH = pl.DeviceIdType.MESH

N_DEV = 8
ALPHA = 2.0 ** 0.25
LN_EPS = 1e-5
CONV_CH = 512
SSM_W = 512
SSM_GROUPS = 32
SSM_GROUP = 16
SSM_STATE = 64
SSM_CH = SSM_GROUPS * SSM_STATE
SCAN_CB = 512
SCAN_NCB = SSM_CH // SCAN_CB
SCAN_GPB = SSM_GROUPS // SCAN_NCB
SCAN_UW = SCAN_GPB * SSM_GROUP
SCAN_TT = 256
ADAM_LR = 0.001
ADAM_B1 = 0.9
ADAM_B2 = 0.999
ADAM_EPS = 1e-08
ADAM_WD = 0.01
ADAM_STEP = 10
VMEM_LIMIT = 56 * 1024 * 1024


def _cparams(sem=None, **kw):
    return pltpu.CompilerParams(dimension_semantics=sem, vmem_limit_bytes=VMEM_LIMIT, **kw)


def _mm(a, b, *, name, ja=None, jb=None, jo=None, a_flat=False, b_flat=False, o_flat=False,
        ta=False, tb=False, tm, tn, tk, nj=1, out_dtype=F32, plans=(), epilogue=None, after=()):
    def dims(arr, j, flat):
        if j is None:
            return arr.shape
        if flat:
            return (arr.shape[0], arr.shape[1] // nj)
        assert arr.shape[0] == nj, (name, arr.shape, nj)
        return arr.shape[1:]

    ar, ac = dims(a, ja, a_flat)
    br, bc = dims(b, jb, b_flat)
    m, k = (ac, ar) if ta else (ar, ac)
    k2, n = (bc, br) if tb else (br, bc)
    assert k == k2, (name, a.shape, b.shape)
    assert m % tm == 0 and n % tn == 0 and k % tk == 0, (name, m, n, k, tm, tn, tk)
    njb = nj if 'b' in (ja, jb) else 1
    njc = nj if 'c' in (ja, jb) else 1
    nk = k // tk
    j_inside = njc > 1 and nk == 1 and not ta
    n_in = njc if j_inside else 1
    nred = nk if j_inside else njc * nk
    grid = (njb, m // tm, n // tn, 1 if j_inside else njc, nk)

    def make_spec(j, flat, blk, rfn, cfn, cols_per_j):
        def jsel(g, c):
            return g if j == 'b' else c
        if j is None:
            return pl.BlockSpec(blk, lambda g, i, jn, c, kk: (rfn(i, jn, kk), cfn(i, jn, kk)))
        if j == 'c' and j_inside:
            if flat:
                return pl.BlockSpec((blk[0], nj * blk[1]), lambda g, i, jn, c, kk: (rfn(i, jn, kk), 0))
            return pl.BlockSpec((nj,) + blk, lambda g, i, jn, c, kk: (0, rfn(i, jn, kk), cfn(i, jn, kk)))
        if flat:
            nb = cols_per_j // blk[1]
            return pl.BlockSpec(blk, lambda g, i, jn, c, kk: (rfn(i, jn, kk), jsel(g, c) * nb + cfn(i, jn, kk)))
        return pl.BlockSpec((None,) + blk,
                            lambda g, i, jn, c, kk: (jsel(g, c), rfn(i, jn, kk), cfn(i, jn, kk)))

    if ta:
        a_spec = make_spec(ja, a_flat, (tk, tm), lambda i, jn, kk: kk, lambda i, jn, kk: i, ac)
    else:
        a_spec = make_spec(ja, a_flat, (tm, tk), lambda i, jn, kk: i, lambda i, jn, kk: kk, ac)
    if tb:
        b_spec = make_spec(jb, b_flat, (tn, tk), lambda i, jn, kk: jn, lambda i, jn, kk: kk, bc)
    else:
        b_spec = make_spec(jb, b_flat, (tk, tn), lambda i, jn, kk: kk, lambda i, jn, kk: jn, bc)
    o_spec = make_spec(jo, o_flat, (tm, tn), lambda i, jn, kk: i, lambda i, jn, kk: jn, n)
    if jo is None:
        out_shape = (m, n)
    elif o_flat:
        out_shape = (m, nj * n)
    else:
        out_shape = (nj, m, n)

    dn = (((0 if ta else 1,), (1 if tb else 0,)), ((), ()))

    def operand(ref, j, flat, jj, width):
        if not (j == 'c' and j_inside):
            return ref[...]
        return ref[:, jj * width:(jj + 1) * width] if flat else ref[jj]

    e_fn, e_rows, e_vecs, e_dtypes = epilogue if epilogue else (None, (), (), (out_dtype,))
    assert not epilogue or (nred == 1 and jo is None), name
    n_e = len(e_rows) + len(e_vecs)
    n_o = len(e_dtypes)
    n_a = len(after)

    def body(a_ref, b_ref, *rest):
        e_refs, o_refs, scratch = rest[:n_e], rest[n_e + n_a:n_e + n_a + n_o], rest[n_e + n_a + n_o:]
        o_ref = o_refs[0]
        p = None
        for jj in range(n_in):
            q = lax.dot_general(operand(a_ref, ja, a_flat, jj, tk), operand(b_ref, jb, b_flat, jj, tk if tb else tn),
                                dn, preferred_element_type=F32)
            p = q if p is None else p + q
        if epilogue:
            for ref, val in zip(o_refs, e_fn(p, *[r[...] for r in e_refs])):
                ref[...] = val.astype(ref.dtype)
        elif nred == 1:
            o_ref[...] = p.astype(o_ref.dtype)
        else:
            acc = scratch[0]
            r = pl.program_id(3) * nk + pl.program_id(4)

            @pl.when(r == 0)
            def _():
                acc[...] = p

            @pl.when(r > 0)
            def _():
                acc[...] += p

            @pl.when(r == nred - 1)
            def _():
                o_ref[...] = acc[...].astype(o_ref.dtype)

    vec_spec = pl.BlockSpec((1, tn), lambda g, i, jn, c, kk: (0, jn))
    res = _call_with_plans(
        body, plans, name=name, grid=grid,
        in_specs=([a_spec, b_spec] + [o_spec] * len(e_rows) + [vec_spec] * len(e_vecs)
                  + [pl.BlockSpec(memory_space=pl.ANY)] * n_a),
        out_specs=[o_spec] * n_o, out_shape=[jax.ShapeDtypeStruct(out_shape, dt) for dt in e_dtypes],
        scratch_shapes=[] if nred == 1 else [pltpu.VMEM((tm, tn), F32)],
        semantics=("parallel", "parallel", "parallel", "arbitrary", "arbitrary"),
        operands=(a, b, *e_rows, *e_vecs, *after))
    outs = res[0] if epilogue else res[0][0]
    return (outs, res[1]) if plans else outs


def _sigmoid(v):
    return jax.nn.sigmoid(v)


def _row_spec(tm, cols, colblk=0):
    return pl.BlockSpec((tm, cols), lambda i: (i, colblk))


def _vec_spec(cols):
    return pl.BlockSpec((1, cols), lambda i: (0, 0))


def _ffn_in(xb, win, *, name, tm, plans=()):
    t, d = xb.shape
    nj, w, _ = win.shape
    half = nj // 2
    dn = (((1,), (1,)), ((), ()))

    def body(x_ref, wg_ref, wu_ref, a_ref, gu_ref):
        xv = x_ref[...]
        g = lax.dot_general(xv, wg_ref[...], dn, preferred_element_type=F32)
        u = lax.dot_general(xv, wu_ref[...], dn, preferred_element_type=F32)
        a_ref[...] = (g * _sigmoid(g) * u).astype(BF16)
        gu_ref[0] = g.astype(BF16)
        gu_ref[1] = u.astype(BF16)

    (a, gu), riders = _call_with_plans(
        body, plans, name=name, grid=(half, t // tm),
        in_specs=[pl.BlockSpec((tm, d), lambda j, i: (i, 0)),
                  pl.BlockSpec((None, w, d), lambda j, i: (j, 0, 0)),
                  pl.BlockSpec((None, w, d), lambda j, i: (j + half, 0, 0))],
        out_specs=[pl.BlockSpec((None, tm, w), lambda j, i: (j, i, 0)),
                   pl.BlockSpec((2, None, tm, w), lambda j, i: (0, j, i, 0))],
        out_shape=[jax.ShapeDtypeStruct((half, t, w), BF16), jax.ShapeDtypeStruct((2, half, t, w), BF16)],
        scratch_shapes=[], semantics=("parallel", "parallel"), operands=(xb, win, win))
    return a, gu, riders


def _ffn_out_dg(dfb, wout, gu, *, name, tm, plans=()):
    t, d = dfb.shape
    half, w, _ = wout.shape
    dn = (((1,), (1,)), ((), ()))

    def body(df_ref, w_ref, gu_ref, dh_ref):
        da = lax.dot_general(df_ref[...], w_ref[...], dn, preferred_element_type=F32)
        g = gu_ref[0].astype(F32)
        u = gu_ref[1].astype(F32)
        sg = _sigmoid(g)
        dh_ref[0] = (da * u * (sg * (1.0 + g * (1.0 - sg)))).astype(BF16)
        dh_ref[1] = (da * (g * sg)).astype(BF16)

    (out,), riders = _call_with_plans(
        body, plans, name=name, grid=(half, t // tm),
        in_specs=[pl.BlockSpec((tm, d), lambda j, i: (i, 0)),
                  pl.BlockSpec((None, w, d), lambda j, i: (j, 0, 0)),
                  pl.BlockSpec((2, None, tm, w), lambda j, i: (0, j, i, 0))],
        out_specs=[pl.BlockSpec((2, None, tm, w), lambda j, i: (0, j, i, 0))],
        out_shape=[jax.ShapeDtypeStruct((2, half, t, w), BF16)],
        scratch_shapes=[], semantics=("parallel", "parallel"), operands=(dfb, wout, gu))
    return out.reshape(2 * half, t, w), riders


def _ln_stats(r):
    mu = jnp.mean(r, axis=-1, keepdims=True)
    xc = r - mu
    var = jnp.mean(xc * xc, axis=-1, keepdims=True)
    rstd = lax.rsqrt(var + LN_EPS)
    return xc * rstd, rstd


def _ln_epilogue(xin, gamma, beta, fs):
    def fn(p, xv, g, b):
        r = ALPHA * xv + fs * p
        xh, _ = _ln_stats(r)
        y = xh * g + b
        return r, y, y

    return fn, (xin,), (gamma, beta), (F32, F32, BF16)


def _ln_bwd(xin, fparts, gamma, beta, grads, *, name, fs=1.0, ple=False, target=None, tm=512):
    t, d = xin.shape
    nf = len(fparts)
    ng = len(grads)
    coefs = [c for _, c in grads]
    use_t = target is not None
    n_fout = 2 if ple else 1

    def body(*refs):
        pos = 0
        x_ref = refs[pos]; pos += 1
        f_refs = refs[pos:pos + nf]; pos += nf
        g_ref, b_ref = refs[pos:pos + 2]; pos += 2
        gr_refs = refs[pos:pos + ng]; pos += ng
        if use_t:
            t_ref = refs[pos]; pos += 1
        dr_ref = refs[pos]; pos += 1
        fo_refs = refs[pos:pos + n_fout]; pos += n_fout
        dg_ref, db_ref = refs[pos:pos + 2]; pos += 2
        if use_t:
            loss_ref = refs[pos]; pos += 1
        i = pl.program_id(0)

        if ple:
            pe = f_refs[0][...]
            sg = _sigmoid(f_refs[1][...])
            resid = ALPHA * x_ref[...] + pe * sg
        else:
            resid = x_ref[...]
        xh, rstd = _ln_stats(resid)
        gam = g_ref[...]
        if use_t:
            diff = xh * gam + b_ref[...] - t_ref[...]
            dy = diff * (1.0 / d)
            lpart = 0.5 * jnp.sum(jnp.sum(diff * diff, axis=-1, keepdims=True), axis=0, keepdims=True) * (1.0 / d)
        else:
            dy = coefs[0] * gr_refs[0][...]
            for c, r in zip(coefs[1:], gr_refs[1:]):
                dy = dy + c * r[...]
        dxh = dy * gam
        m1 = jnp.mean(dxh, axis=-1, keepdims=True)
        m2 = jnp.mean(dxh * xh, axis=-1, keepdims=True)
        dr = rstd * (dxh - m1 - xh * m2)
        dr_ref[...] = dr
        if ple:
            fo_refs[0][...] = (dr * sg).astype(BF16)
            fo_refs[1][...] = (dr * pe * (sg * (1.0 - sg))).astype(BF16)
        else:
            fo_refs[0][...] = (fs * dr).astype(BF16)
        dgp = jnp.sum(dy * xh, axis=0, keepdims=True)
        dbp = jnp.sum(dy, axis=0, keepdims=True)

        @pl.when(i == 0)
        def _():
            dg_ref[...] = dgp
            db_ref[...] = dbp
            if use_t:
                loss_ref[...] = jnp.broadcast_to(lpart, loss_ref.shape)

        @pl.when(i > 0)
        def _():
            dg_ref[...] += dgp
            db_ref[...] += dbp
            if use_t:
                loss_ref[...] += jnp.broadcast_to(lpart, loss_ref.shape)

    ins = [xin, *fparts, gamma, beta, *[g for g, _ in grads]] + ([target] if use_t else [])
    in_specs = ([_row_spec(tm, d)] * (1 + nf) + [_vec_spec(d), _vec_spec(d)] + [_row_spec(tm, d)] * ng
                + ([_row_spec(tm, d)] if use_t else []))
    out_specs = [_row_spec(tm, d)] * (1 + n_fout) + [_vec_spec(d), _vec_spec(d)] + ([_vec_spec(128)] if use_t else [])
    out_shape = ([jax.ShapeDtypeStruct((t, d), F32)] + [jax.ShapeDtypeStruct((t, d), BF16)] * n_fout
                 + [jax.ShapeDtypeStruct((1, d), F32)] * 2 + ([jax.ShapeDtypeStruct((1, 128), F32)] if use_t else []))
    return pl.pallas_call(
        body, name=name, grid=(t // tm,), in_specs=in_specs, out_specs=out_specs, out_shape=out_shape,
        compiler_params=_cparams(("arbitrary",)),
    )(*ins)


def _conv_fwd(proj, cw, cb, *, name, tm=512):
    t = proj.shape[0]
    c = CONV_CH
    hb = tm // 8

    def body(b_ref, c_ref, h_ref, cp_ref, hp_ref, w_ref, bias_ref, o_ref, q_scr):
        i = pl.program_id(0)
        q = c_ref[...] * h_ref[...]
        halo = jnp.where(i > 0, cp_ref[...] * hp_ref[...], 0.0)
        q_scr[0:8, :] = halo
        q_scr[8:, :] = q
        z = (w_ref[2:3, :] * q + w_ref[1:2, :] * q_scr[pl.ds(7, tm), :] + w_ref[0:1, :] * q_scr[pl.ds(6, tm), :]
             + bias_ref[...])
        o_ref[...] = (b_ref[...] * z).astype(BF16)

    prev = lambda blk: pl.BlockSpec((8, c), lambda i: (jnp.maximum(i * hb - 1, 0), blk))
    return pl.pallas_call(
        body, name=name, grid=(t // tm,),
        in_specs=[_row_spec(tm, c, 0), _row_spec(tm, c, 1), _row_spec(tm, c, 2), prev(1), prev(2),
                  pl.BlockSpec((8, c), lambda i: (0, 0)), _vec_spec(c)],
        out_specs=_row_spec(tm, c),
        out_shape=jax.ShapeDtypeStruct((t, c), BF16),
        scratch_shapes=[pltpu.VMEM((tm + 8, c), F32)],
        compiler_params=_cparams(("parallel",)),
    )(proj, proj, proj, proj, proj, cw, cb)


def _conv_bwd(dyc, proj, cw, cb, *, name, tm=512):
    t = proj.shape[0]
    c = CONV_CH
    hb = tm // 8
    nblk = t // 8

    def body(d_ref, b_ref, c_ref, h_ref, cp_ref, hp_ref, dn_ref, bn_ref, w_ref, bias_ref,
             db_ref, dc_ref, dh_ref, dw_ref, q_scr, z_scr):
        i = pl.program_id(0)
        last = pl.num_programs(0) - 1
        cc = c_ref[...]
        ch = h_ref[...]
        q = cc * ch
        halo = jnp.where(i > 0, cp_ref[...] * hp_ref[...], 0.0)
        q_scr[0:8, :] = halo
        q_scr[8:, :] = q
        w0, w1, w2 = w_ref[0:1, :], w_ref[1:2, :], w_ref[2:3, :]
        qm1 = q_scr[pl.ds(7, tm), :]
        qm2 = q_scr[pl.ds(6, tm), :]
        z = w2 * q + w1 * qm1 + w0 * qm2 + bias_ref[...]
        d = d_ref[...]
        bb = b_ref[...]
        db_ref[...] = (d * z).astype(BF16)
        dz = d * bb
        z_scr[0:tm, :] = dz
        z_scr[tm:, :] = jnp.where(i < last, dn_ref[...] * bn_ref[...], 0.0)
        dq = w2 * dz + w1 * z_scr[pl.ds(1, tm), :] + w0 * z_scr[pl.ds(2, tm), :]
        dc_ref[...] = (dq * ch).astype(BF16)
        dh_ref[...] = (dq * cc).astype(BF16)
        row = lax.broadcasted_iota(jnp.int32, (8, c), 0)
        part = jnp.zeros((8, c), F32)
        for k, term in enumerate((dz * qm2, dz * qm1, dz * q, dz)):
            part = jnp.where(row == k, jnp.sum(term, axis=0, keepdims=True), part)

        @pl.when(i == 0)
        def _():
            dw_ref[...] = part

        @pl.when(i > 0)
        def _():
            dw_ref[...] += part

    prev = lambda blk: pl.BlockSpec((8, c), lambda i: (jnp.maximum(i * hb - 1, 0), blk))
    nxt_p = pl.BlockSpec((8, c), lambda i: (jnp.minimum((i + 1) * hb, nblk - 1), 0))
    nxt_d = pl.BlockSpec((8, c), lambda i: (jnp.minimum((i + 1) * hb, nblk - 1), 0))
    return pl.pallas_call(
        body, name=name, grid=(t // tm,),
        in_specs=[_row_spec(tm, c), _row_spec(tm, c, 0), _row_spec(tm, c, 1), _row_spec(tm, c, 2),
                  prev(1), prev(2), nxt_d, nxt_p, pl.BlockSpec((8, c), lambda i: (0, 0)), _vec_spec(c)],
        out_specs=[_row_spec(tm, c)] * 3 + [pl.BlockSpec((8, c), lambda i: (0, 0))],
        out_shape=[jax.ShapeDtypeStruct((t, c), BF16)] * 3 + [jax.ShapeDtypeStruct((8, c), F32)],
        scratch_shapes=[pltpu.VMEM((tm + 8, c), F32), pltpu.VMEM((tm + 8, c), F32)],
        compiler_params=_cparams(("arbitrary",)),
    )(dyc, proj, proj, proj, proj, proj, dyc, proj, cw, cb)


def _gate_fwd(yconv, glu, proj, *, name, tm=512):
    t, d = yconv.shape

    def body(yc_ref, ga_ref, gb_ref, gc_ref, gs_ref, o_ref):
        yssm = ga_ref[...] * _sigmoid(gb_ref[...])
        o_ref[...] = (_sigmoid(gc_ref[...]) * yc_ref[...] + _sigmoid(gs_ref[...]) * yssm).astype(BF16)

    return pl.pallas_call(
        body, name=name, grid=(t // tm,),
        in_specs=[_row_spec(tm, d), _row_spec(tm, d, 0), _row_spec(tm, d, 1), _row_spec(tm, d, 2), _row_spec(tm, d, 3)],
        out_specs=_row_spec(tm, d), out_shape=jax.ShapeDtypeStruct((t, d), BF16),
        compiler_params=_cparams(("parallel",)),
    )(yconv, glu, glu, proj, proj)


def _gate_bwd(dm, yconv, glu, proj, *, name, tm=512):
    t, d = yconv.shape

    def body(dm_ref, yc_ref, ga_ref, gb_ref, gc_ref, gs_ref, dyc_ref, dglu_ref, dgate_ref):
        dmv = dm_ref[...]
        sc = _sigmoid(gc_ref[...])
        ss = _sigmoid(gs_ref[...])
        sb = _sigmoid(gb_ref[...])
        ga = ga_ref[...]
        yssm = ga * sb
        dyc_ref[...] = (dmv * sc).astype(BF16)
        dgate_ref[:, 0:d] = (dmv * yc_ref[...] * (sc * (1.0 - sc))).astype(BF16)
        dys = dmv * ss
        dgate_ref[:, d:2 * d] = (dmv * yssm * (ss * (1.0 - ss))).astype(BF16)
        dglu_ref[:, 0:d] = (dys * sb).astype(BF16)
        dglu_ref[:, d:2 * d] = (dys * ga * (sb * (1.0 - sb))).astype(BF16)

    return pl.pallas_call(
        body, name=name, grid=(t // tm,),
        in_specs=[_row_spec(tm, d), _row_spec(tm, d), _row_spec(tm, d, 0), _row_spec(tm, d, 1),
                  _row_spec(tm, d, 2), _row_spec(tm, d, 3)],
        out_specs=[_row_spec(tm, d), _row_spec(tm, 2 * d), _row_spec(tm, 2 * d)],
        out_shape=[jax.ShapeDtypeStruct((t, d), BF16), jax.ShapeDtypeStruct((t, 2 * d), BF16),
                   jax.ShapeDtypeStruct((t, 2 * d), BF16)],
        compiler_params=_cparams(("parallel",)),
    )(dm, yconv, glu, glu, proj, proj)


_GELU_C = math.sqrt(2.0 / math.pi)


def _gelu(v):
    return 0.5 * v * (1.0 + jnp.tanh(_GELU_C * (v + 0.044715 * v * v * v)))


def _gelu_grad(v):
    th = jnp.tanh(_GELU_C * (v + 0.044715 * v * v * v))
    return 0.5 * (1.0 + th) + 0.5 * v * (1.0 - th * th) * (_GELU_C * (1.0 + 3.0 * 0.044715 * v * v))


def _cmul(ar, ai, br, bi):
    return ar * br - ai * bi, ar * bi + ai * br


def _scan_fwd(proj, bmat, tab, *, name, plans=()):
    t = proj.shape[0]
    tt, cbw = SCAN_TT, SCAN_CB
    w2 = 2 * cbw

    def body(u_ref, b_ref, tab_ref, s_ref, sb_ref, bu_scr, carry):
        ti = pl.program_id(1)

        @pl.when(ti == 0)
        def _():
            carry[...] = jnp.zeros_like(carry)

        bu_scr[...] = jnp.dot(u_ref[...].astype(BF16), b_ref[...], preferred_element_type=F32)
        row = lax.broadcasted_iota(jnp.int32, (8, cbw), 0)

        def blk(bi, c):
            cr, ci = c
            r0 = pl.multiple_of(bi * 8, 8)
            xr = bu_scr[pl.ds(r0, 8), 0:cbw]
            xi = bu_scr[pl.ds(r0, 8), cbw:w2]
            for k, sh in enumerate((1, 2, 4)):
                kr = tab_ref[k:k + 1, 0:cbw]
                ki = tab_ref[k:k + 1, cbw:w2]
                sr = jnp.where(row >= sh, pltpu.roll(xr, sh, 0), 0.0)
                si = jnp.where(row >= sh, pltpu.roll(xi, sh, 0), 0.0)
                pr, pi = _cmul(kr, ki, sr, si)
                xr = xr + pr
                xi = xi + pi
            pr, pi = _cmul(tab_ref[8:16, 0:cbw], tab_ref[8:16, cbw:w2], cr, ci)
            xr = xr + pr
            xi = xi + pi
            s_ref[pl.ds(r0, 8), 0:cbw] = xr
            s_ref[pl.ds(r0, 8), cbw:w2] = xi
            return (jnp.broadcast_to(xr[7:8, :], (8, cbw)), jnp.broadcast_to(xi[7:8, :], (8, cbw)))

        cr, ci = lax.fori_loop(0, tt // 8, blk, (carry[:, 0:cbw], carry[:, cbw:w2]))
        carry[:, 0:cbw] = cr
        carry[:, cbw:w2] = ci
        sb_ref[...] = s_ref[...].astype(BF16)

    (s, sb), riders = _call_with_plans(
        body, plans, name=name, grid=(SCAN_NCB, t // tt),
        in_specs=[pl.BlockSpec((tt, SCAN_UW), lambda cb, ti: (ti, 3 * SCAN_NCB + cb)),
                  pl.BlockSpec((None, SCAN_UW, w2), lambda cb, ti: (cb, 0, 0)),
                  pl.BlockSpec((16, w2), lambda cb, ti: (0, cb))],
        out_specs=[pl.BlockSpec((tt, w2), lambda cb, ti: (ti, cb))] * 2,
        out_shape=[jax.ShapeDtypeStruct((t, 2 * SSM_CH), F32), jax.ShapeDtypeStruct((t, 2 * SSM_CH), BF16)],
        scratch_shapes=[pltpu.VMEM((tt, w2), F32), pltpu.VMEM((8, w2), F32)],
        semantics=("parallel", "arbitrary"), operands=(proj, bmat, tab))
    return s, sb, riders


def _scan_bwd(dyb, cmat_t, s, tabb, *, name, plans=()):
    t = s.shape[0]
    tt, cbw = SCAN_TT, SCAN_CB
    w2 = 2 * cbw
    nt = t // tt
    hb = tt // 8

    def body(dy_ref, c_ref, s_ref, sp_ref, tab_ref, h_ref, da_ref, g_scr, s_scr, carry):
        ti = pl.program_id(1)

        @pl.when(ti == 0)
        def _():
            carry[...] = jnp.zeros_like(carry)
            da_ref[...] = jnp.zeros_like(da_ref)

        g_scr[...] = jnp.dot(dy_ref[...], c_ref[...], preferred_element_type=F32)
        s_scr[0:8, :] = jnp.where(ti < nt - 1, sp_ref[...], 0.0)
        s_scr[8:, :] = s_ref[...]
        row = lax.broadcasted_iota(jnp.int32, (8, cbw), 0)

        def blk(k, c):
            cr, ci, ar, ai = c
            bi = hb - 1 - k
            r0 = pl.multiple_of(bi * 8, 8)
            xr = g_scr[pl.ds(r0, 8), 0:cbw]
            xi = g_scr[pl.ds(r0, 8), cbw:w2]
            for j, sh in enumerate((1, 2, 4)):
                kr = tab_ref[j:j + 1, 0:cbw]
                ki = tab_ref[j:j + 1, cbw:w2]
                sr = jnp.where(row < 8 - sh, pltpu.roll(xr, 8 - sh, 0), 0.0)
                si = jnp.where(row < 8 - sh, pltpu.roll(xi, 8 - sh, 0), 0.0)
                pr, pi = _cmul(kr, ki, sr, si)
                xr = xr + pr
                xi = xi + pi
            pr, pi = _cmul(tab_ref[8:16, 0:cbw], tab_ref[8:16, cbw:w2], cr, ci)
            xr = xr + pr
            xi = xi + pi
            h_ref[pl.ds(r0, 8), 0:cbw] = xr.astype(BF16)
            h_ref[pl.ds(r0, 8), cbw:w2] = xi.astype(BF16)
            pvr = s_scr[pl.ds(r0, 8), 0:cbw]
            pvi = s_scr[pl.ds(r0, 8), cbw:w2]
            cur_r = s_scr[pl.ds(r0 + 8, 8), 0:cbw]
            cur_i = s_scr[pl.ds(r0 + 8, 8), cbw:w2]
            spr = jnp.where(row == 0, jnp.broadcast_to(pvr[7:8, :], (8, cbw)), pltpu.roll(cur_r, 1, 0))
            spi = jnp.where(row == 0, jnp.broadcast_to(pvi[7:8, :], (8, cbw)), pltpu.roll(cur_i, 1, 0))
            ar = ar + spr * xr + spi * xi
            ai = ai + spr * xi - spi * xr
            return (jnp.broadcast_to(xr[0:1, :], (8, cbw)), jnp.broadcast_to(xi[0:1, :], (8, cbw)), ar, ai)

        z = jnp.zeros((8, cbw), F32)
        cr, ci, ar, ai = lax.fori_loop(0, hb, blk, (carry[:, 0:cbw], carry[:, cbw:w2], z, z))
        carry[:, 0:cbw] = cr
        carry[:, cbw:w2] = ci
        da_ref[:, 0:cbw] += ar
        da_ref[:, cbw:w2] += ai

    rt = lambda ti: nt - 1 - ti
    (h, da), riders = _call_with_plans(
        body, plans, name=name, grid=(SCAN_NCB, nt),
        in_specs=[pl.BlockSpec((tt, SCAN_UW), lambda cb, ti: (rt(ti), cb)),
                  pl.BlockSpec((None, SCAN_UW, w2), lambda cb, ti: (cb, 0, 0)),
                  pl.BlockSpec((tt, w2), lambda cb, ti: (rt(ti), cb)),
                  pl.BlockSpec((8, w2), lambda cb, ti: (jnp.maximum(rt(ti) * hb - 1, 0), cb)),
                  pl.BlockSpec((16, w2), lambda cb, ti: (0, cb))],
        out_specs=[pl.BlockSpec((tt, w2), lambda cb, ti: (rt(ti), cb)),
                   pl.BlockSpec((8, w2), lambda cb, ti: (0, cb))],
        out_shape=[jax.ShapeDtypeStruct((t, 2 * SSM_CH), BF16), jax.ShapeDtypeStruct((8, 2 * SSM_CH), F32)],
        scratch_shapes=[pltpu.VMEM((tt, w2), F32), pltpu.VMEM((tt + 8, w2), F32), pltpu.VMEM((8, w2), F32)],
        semantics=("parallel", "arbitrary"), operands=(dyb, cmat_t, s, s, tabb))
    return h, da, riders


def _s5_out(ymm, proj, dvec, *, name, tm=512):
    t, w = ymm.shape

    def body(y_ref, u_ref, d_ref, yo_ref, sg_ref, ub_ref):
        u = u_ref[...]
        y = y_ref[...] + d_ref[...] * u
        yo_ref[...] = y
        sg_ref[...] = _gelu(y).astype(BF16)
        ub_ref[...] = u.astype(BF16)

    return pl.pallas_call(
        body, name=name, grid=(t // tm,),
        in_specs=[_row_spec(tm, w), _row_spec(tm, w, 3), _vec_spec(w)],
        out_specs=[_row_spec(tm, w)] * 3,
        out_shape=[jax.ShapeDtypeStruct((t, w), F32), jax.ShapeDtypeStruct((t, w), BF16), jax.ShapeDtypeStruct((t, w), BF16)],
        compiler_params=_cparams(("parallel",)),
    )(ymm, proj, dvec)


def _s5_bwd_in(dsg, y, proj, *, name, tm=512):
    t, w = y.shape

    def body(d_ref, y_ref, u_ref, dy_ref, dyb_ref, dd_ref):
        i = pl.program_id(0)
        dy = d_ref[...] * _gelu_grad(y_ref[...])
        dy_ref[...] = dy
        dyb_ref[...] = dy.astype(BF16)
        part = jnp.sum(dy * u_ref[...], axis=0, keepdims=True)

        @pl.when(i == 0)
        def _():
            dd_ref[...] = part

        @pl.when(i > 0)
        def _():
            dd_ref[...] += part

    return pl.pallas_call(
        body, name=name, grid=(t // tm,),
        in_specs=[_row_spec(tm, w), _row_spec(tm, w), _row_spec(tm, w, 3)],
        out_specs=[_row_spec(tm, w), _row_spec(tm, w), _vec_spec(w)],
        out_shape=[jax.ShapeDtypeStruct((t, w), F32), jax.ShapeDtypeStruct((t, w), BF16), jax.ShapeDtypeStruct((1, w), F32)],
        compiler_params=_cparams(("arbitrary",)),
    )(dsg, y, proj)


def _s5_du(dumm, dy, dvec, *, name, tm=512):
    t, w = dy.shape

    def body(a_ref, dy_ref, d_ref, o_ref):
        o_ref[...] = (a_ref[...] + d_ref[...] * dy_ref[...]).astype(BF16)

    return pl.pallas_call(
        body, name=name, grid=(t // tm,), in_specs=[_row_spec(tm, w), _row_spec(tm, w), _vec_spec(w)],
        out_specs=_row_spec(tm, w), out_shape=jax.ShapeDtypeStruct((t, w), BF16),
        compiler_params=_cparams(("parallel",)),
    )(dumm, dy, dvec)


def _s5_discretise(lam_re, lam_im, log_step, b_re, b_im):
    lam = lax.complex(lam_re, lam_im)
    dt = jnp.exp(log_step)[:, None]
    a = jnp.exp(lam * dt)
    bbar = ((a - 1.0) / lam)[..., None] * lax.complex(b_re, b_im)
    return jnp.real(a), jnp.imag(a), jnp.real(bbar), jnp.imag(bbar)


def _perm_cols(re, im):
    lead = re.shape[:-1]
    r = re.reshape(lead + (SCAN_NCB, 1, SCAN_CB))
    i = im.reshape(lead + (SCAN_NCB, 1, SCAN_CB))
    return jnp.concatenate([r, i], axis=-2).reshape(lead + (2 * SSM_CH,))


def _unperm_cols(x):
    lead = x.shape[:-1]
    y = x.reshape(lead + (SCAN_NCB, 2, SCAN_CB))
    return y[..., 0, :].reshape(lead + (SSM_CH,)), y[..., 1, :].reshape(lead + (SSM_CH,))


def _compact(re, im):
    _, r, c = re.shape
    eye = jnp.eye(SCAN_GPB, dtype=re.dtype)

    def half(x):
        x = x.reshape(SCAN_NCB, SCAN_GPB, r, c)
        return (eye[None, :, None, :, None] * x[:, :, :, None, :]).reshape(SCAN_NCB, SCAN_GPB * r, SCAN_GPB * c)

    return jnp.concatenate([half(re), half(im)], axis=-1)


def _compact_extract(x, r):
    c = SSM_STATE
    eye = jnp.eye(SCAN_GPB, dtype=x.dtype)
    y = x.reshape(SCAN_NCB, SCAN_GPB, r, 2, SCAN_GPB, c)
    dg = jnp.sum(y * eye[None, :, None, None, :, None], axis=4).reshape(SSM_GROUPS, r, 2, c)
    return dg[:, :, 0, :], dg[:, :, 1, :]


def _pow_table(ar, ai, descending=False):
    ar = ar.reshape(1, SSM_CH)
    ai = ai.reshape(1, SSM_CH)
    pw = [(ar, ai)]
    for _ in range(7):
        pw.append(_cmul(pw[-1][0], pw[-1][1], ar, ai))
    zero = (jnp.zeros_like(ar), jnp.zeros_like(ar))
    rows = [pw[0], pw[1], pw[3]] + [zero] * 5 + (pw[::-1] if descending else pw)
    re = jnp.concatenate([r for r, _ in rows], axis=0)
    im = jnp.concatenate([i for _, i in rows], axis=0)
    return _perm_cols(re, im)


def _place():
    x, y, c = lax.axis_index("x"), lax.axis_index("y"), lax.axis_index("c")
    chips = [(1 - x, y), (x, 1 - y), (1 - x, 1 - y)]
    return x, y, c, chips


def _dev(px, py, pc):
    return 4 * px + 2 * py + pc


class _Plan:
    def __init__(self, ins, out_shapes, sem_shapes, start, finish, middle=None):
        self.ins, self.out_shapes, self.sem_shapes = list(ins), list(out_shapes), list(sem_shapes)
        self.start, self.finish, self.middle = start, finish, middle


def _split_plan_refs(plans, in_refs, out_refs, sem_refs):
    res, i, o, s = [], 0, 0, 0
    for p in plans:
        ni, no, ns = len(p.ins), len(p.out_shapes), len(p.sem_shapes)
        res.append((in_refs[i:i + ni], out_refs[o:o + no], sem_refs[s:s + ns]))
        i, o, s = i + ni, o + no, s + ns
    return res


def _run_plans(plans, *, name):
    ins = [a for p in plans for a in p.ins]
    outs = [o for p in plans for o in p.out_shapes]
    sems = [s for p in plans for s in p.sem_shapes]
    any_spec = pl.BlockSpec(memory_space=pl.ANY)

    def body(*refs):
        parts = _split_plan_refs(plans, refs[:len(ins)], refs[len(ins):len(ins) + len(outs)], refs[len(ins) + len(outs):])
        for p, r in zip(plans, parts):
            p.start(*r)
        for p, r in zip(plans, parts):
            if p.middle:
                p.middle(*r)
        for p, r in zip(plans, parts):
            p.finish(*r)

    res = pl.pallas_call(body, name=name, in_specs=[any_spec] * len(ins), out_specs=[any_spec] * len(outs),
                         out_shape=outs, scratch_shapes=sems)(*ins)
    return _split_plan_refs(plans, [], res, [])


def _run_plans_on_sequencer(plans, peers_of, *, name, collective_id, after=()):
    ins = [a for p in plans for a in p.ins]
    outs = [o for p in plans for o in p.out_shapes]
    sems = [s for p in plans for s in p.sem_shapes]

    def body(*refs):
        x, y, c, chips = _place()
        peers = peers_of(x, y, c, chips)
        barrier = pltpu.get_barrier_semaphore()
        for peer in peers:
            pl.semaphore_signal(barrier, inc=1, device_id=peer, device_id_type=MESH)
        pl.semaphore_wait(barrier, len(peers))
        n_in = len(ins) + len(after)
        parts = _split_plan_refs(plans, refs[:len(ins)], refs[n_in:n_in + len(outs)], refs[n_in + len(outs):])
        for p, r in zip(plans, parts):
            p.start(*r)
        for p, r in zip(plans, parts):
            if p.middle:
                p.middle(*r)
        for p, r in zip(plans, parts):
            p.finish(*r)

    res = pl.kernel(body, name=name, out_type=outs, mesh=plsc.ScalarSubcoreMesh(axis_name="seq", num_cores=1),
                    scratch_types=sems, compiler_params=pltpu.CompilerParams(collective_id=collective_id))(*ins, *after)
    return _split_plan_refs(plans, [], list(res), [])


def _call_with_plans(body, plans, *, name, grid, in_specs, out_specs, out_shape, scratch_shapes, semantics, operands):
    plans = list(plans)
    if not plans:
        res = pl.pallas_call(body, name=name, grid=grid, in_specs=in_specs, out_specs=out_specs, out_shape=out_shape,
                             scratch_shapes=scratch_shapes, compiler_params=_cparams(semantics))(*operands)
        return list(res), []
    n_in, n_out, n_scr = len(in_specs), len(out_specs), len(scratch_shapes)
    p_ins = [a for p in plans for a in p.ins]
    p_outs = [o for p in plans for o in p.out_shapes]
    p_sems = [s for p in plans for s in p.sem_shapes]
    nsteps = math.prod(grid)
    any_spec = pl.BlockSpec(memory_space=pl.ANY)

    def wrapped(*refs):
        bounds = [n_in, len(p_ins), n_out, len(p_outs), n_scr]
        parts, pos = [], 0
        for b in bounds:
            parts.append(refs[pos:pos + b])
            pos += b
        ins, p_in, outs, p_out, scr = parts
        step = pl.program_id(0)
        for ax in range(1, len(grid)):
            step = step * grid[ax] + pl.program_id(ax)
        riders = _split_plan_refs(plans, p_in, p_out, refs[pos:])

        @pl.when(step == 0)
        def _():
            for p, r in zip(plans, riders):
                p.start(*r)

        mids = [(p, r) for p, r in zip(plans, riders) if p.middle]
        mid_step = nsteps // 2
        split_mid = mids and 0 < mid_step < nsteps - 1
        if split_mid:
            @pl.when(step == mid_step)
            def _():
                for p, r in mids:
                    p.middle(*r)

        body(*ins, *outs, *scr)

        @pl.when(step == nsteps - 1)
        def _():
            if not split_mid:
                for p, r in mids:
                    p.middle(*r)
            for p, r in zip(plans, riders):
                p.finish(*r)

    res = pl.pallas_call(
        wrapped, name=name, grid=grid, in_specs=list(in_specs) + [any_spec] * len(p_ins),
        out_specs=list(out_specs) + [any_spec] * len(p_outs), out_shape=list(out_shape) + p_outs,
        scratch_shapes=list(scratch_shapes) + p_sems, compiler_params=_cparams(("arbitrary",) * len(grid)),
    )(*operands, *p_ins)
    return list(res[:n_out]), [r[1] for r in _split_plan_refs(plans, [], res[n_out:], [])]


def _gather_plan(shards):
    n = len(shards)
    nk = 8

    def make(ins, outs, sems):
        send, recv, lsem = sems
        x, y, c, _ = _place()
        me, sib, xn, yn, dg = (x, y, c), (x, y, 1 - c), (1 - x, y, c), (x, 1 - y, c), (1 - x, 1 - y, c)

        def part(w, block, half):
            ref = outs[w].at[_dev(*block)]
            if half is None:
                return ref
            rows = shards[w].shape[0] // 2
            return ref.at[pl.ds(half * rows, rows)]

        def copy(w, k, block, to, half=None, src=None):
            dst = part(w, block, half)
            return pltpu.make_async_remote_copy(
                src_ref=dst if src is None else src, dst_ref=dst,
                send_sem=send.at[w * nk + k], recv_sem=recv.at[w * nk + k], device_id=to, device_id_type=MESH)

        mine = [pltpu.make_async_copy(ins[w], outs[w].at[_dev(*me)], lsem.at[w]) for w in range(n)]
        return copy, mine, me, sib, xn, yn, dg

    def first_copies(copy, me, sib, xn, yn, ins):
        return [copy(w, k, me, to, src=ins[w]) for w in range(n) for k, to in ((0, sib), (1, xn), (2, yn))]

    def start(ins, outs, sems):
        copy, mine, me, sib, xn, yn, _ = make(ins, outs, sems)
        for cp in mine + first_copies(copy, me, sib, xn, yn, ins):
            cp.start()

    def middle(ins, outs, sems):
        copy, _, me, sib, xn, yn, _ = make(ins, outs, sems)
        for w in range(n):
            copy(w, 1, xn, me).wait_recv()
            copy(w, 3, xn, yn, half=0).start()
            copy(w, 5, xn, sib).start()
        for w in range(n):
            copy(w, 2, yn, me).wait_recv()
            copy(w, 4, yn, xn, half=1).start()
            copy(w, 6, yn, sib).start()

    def finish(ins, outs, sems):
        copy, mine, me, sib, xn, yn, dg = make(ins, outs, sems)
        last = []
        for w in range(n):
            copy(w, 3, dg, me, half=0).wait_recv()
            copy(w, 4, dg, me, half=1).wait_recv()
            fwd = copy(w, 7, dg, sib)
            fwd.start()
            last.append(fwd)
        sx, sy, sd = (1 - me[0], me[1], 1 - me[2]), (me[0], 1 - me[1], 1 - me[2]), (1 - me[0], 1 - me[1], 1 - me[2])
        for w in range(n):
            copy(w, 0, sib, me).wait_recv()
            copy(w, 5, sx, me).wait_recv()
            copy(w, 6, sy, me).wait_recv()
            copy(w, 7, sd, me).wait_recv()
        for cp in first_copies(copy, me, sib, xn, yn, ins) + last:
            cp.wait_send()
        for w in range(n):
            copy(w, 3, xn, yn, half=0).wait_send()
            copy(w, 5, xn, sib).wait_send()
            copy(w, 4, yn, xn, half=1).wait_send()
            copy(w, 6, yn, sib).wait_send()
        for cp in mine:
            cp.wait()

    return _Plan(shards, [jax.ShapeDtypeStruct((N_DEV,) + s.shape, s.dtype) for s in shards],
                 [pltpu.SemaphoreType.DMA((nk * n,)), pltpu.SemaphoreType.DMA((nk * n,)), pltpu.SemaphoreType.DMA((n,))],
                 start, finish, middle)


def _swap_plan(copies_of, n_copies, ins, out_shapes):
    def cps(in_refs, out_refs, sems):
        return copies_of(in_refs, out_refs, sems[0], sems[1])

    def start(in_refs, out_refs, sems):
        for cp in cps(in_refs, out_refs, sems):
            cp.start()

    def finish(in_refs, out_refs, sems):
        all_cps = cps(in_refs, out_refs, sems)
        for cp in all_cps:
            cp.wait_recv()
        for cp in all_cps:
            cp.wait_send()

    return _Plan(ins, out_shapes, [pltpu.SemaphoreType.DMA((n_copies,)), pltpu.SemaphoreType.DMA((n_copies,))],
                 start, finish)


def _sibling_plan(grads):
    n = len(grads)

    def copies(ins, outs, send, recv):
        x, y, c, chips = _place()
        owners = [(x, y)] + chips
        return [pltpu.make_async_remote_copy(
            src_ref=ins[w].at[_dev(*chip, 1 - c)], dst_ref=outs[w].at[k], send_sem=send.at[w * 4 + k],
            recv_sem=recv.at[w * 4 + k], device_id=(x, y, 1 - c), device_id_type=MESH)
            for w in range(n) for k, chip in enumerate(owners)]

    return _swap_plan(copies, 4 * n, grads, [jax.ShapeDtypeStruct((4,) + g.shape[1:], g.dtype) for g in grads])


def _chip_plan(parts, js=(0, 1, 2)):
    n, nj = len(parts), len(js)

    def copies(ins, outs, send, recv):
        x, y, c, chips = _place()
        return [pltpu.make_async_remote_copy(
            src_ref=ins[w].at[j], dst_ref=outs[w * nj + k], send_sem=send.at[w * nj + k],
            recv_sem=recv.at[w * nj + k], device_id=(*chips[j], c), device_id_type=MESH)
            for w in range(n) for k, j in enumerate(js)]

    return _swap_plan(copies, n * nj, parts,
                      [jax.ShapeDtypeStruct(p.shape[1:], p.dtype) for p in parts for _ in js])


UPDATE_TILE_BYTES = 1536 * 1024


def _row_tile(r, c):
    best = 8
    for t in range(8, r + 1, 8):
        if r % t == 0 and t * c * 4 <= UPDATE_TILE_BYTES:
            best = t
    return best


def _chip_partial(g, sib, ids, *, name):
    _, r, c = g.shape
    tr = _row_tile(r, c)

    def body(ids_ref, g_ref, s_ref, o_ref):
        o_ref[...] = (g_ref[...] + s_ref[...]).astype(BF16)

    return pl.pallas_call(
        body, name=name,
        grid_spec=pltpu.PrefetchScalarGridSpec(
            num_scalar_prefetch=1, grid=(3, r // tr),
            in_specs=[pl.BlockSpec((None, tr, c), lambda j, i, ids_ref: (ids_ref[j], i, 0)),
                      pl.BlockSpec((None, tr, c), lambda j, i, ids_ref: (j + 1, i, 0))],
            out_specs=pl.BlockSpec((None, tr, c), lambda j, i, ids_ref: (j, i, 0))),
        out_shape=jax.ShapeDtypeStruct((3, r, c), BF16),
        compiler_params=_cparams(("parallel", "parallel")),
    )(ids, g, sib)


def _adamw_math(w, g, m, v):
    m = ADAM_B1 * m + (1.0 - ADAM_B1) * g
    v = ADAM_B2 * v + (1.0 - ADAM_B2) * (g * g)
    m_hat = m / (1.0 - ADAM_B1 ** ADAM_STEP)
    v_hat = v / (1.0 - ADAM_B2 ** ADAM_STEP)
    delta = -ADAM_LR * (m_hat / (jnp.sqrt(v_hat) + ADAM_EPS) + ADAM_WD * w)
    return delta, m, v


def _shard_update(g, sib, rem, me, w, m, v, *, name):
    r, c = w.shape
    tr = _row_tile(r, c)

    def body(me_ref, g_ref, s_ref, r0_ref, r1_ref, r2_ref, w_ref, m_ref, v_ref, go_ref, d_ref, mo_ref, vo_ref):
        gt = g_ref[...] + s_ref[...]
        gt = gt + r0_ref[...].astype(F32)
        gt = gt + r1_ref[...].astype(F32)
        gt = gt + r2_ref[...].astype(F32)
        go_ref[...] = gt
        d, mn, vn = _adamw_math(w_ref[...], gt, m_ref[...], v_ref[...])
        d_ref[...] = d
        mo_ref[...] = mn
        vo_ref[...] = vn

    blk = lambda k: pl.BlockSpec((None, tr, c), lambda i, me_ref: (k, i, 0))
    plain = pl.BlockSpec((tr, c), lambda i, me_ref: (i, 0))
    return pl.pallas_call(
        body, name=name,
        grid_spec=pltpu.PrefetchScalarGridSpec(
            num_scalar_prefetch=1, grid=(r // tr,),
            in_specs=[pl.BlockSpec((None, tr, c), lambda i, me_ref: (me_ref[0], i, 0)), blk(0), plain, plain, plain,
                      plain, plain, plain],
            out_specs=[plain] * 4),
        out_shape=[jax.ShapeDtypeStruct((r, c), F32)] * 4,
        compiler_params=_cparams(("parallel",)),
    )(me, g, sib, *rem, w, m, v)


def _small_update(gathered, w, m, v, *, name):
    _, r, c = gathered.shape

    def body(g_ref, w_ref, m_ref, v_ref, go_ref, d_ref, mo_ref, vo_ref):
        gt = g_ref[0]
        for k in range(1, N_DEV):
            gt = gt + g_ref[k]
        go_ref[...] = gt
        d, mn, vn = _adamw_math(w_ref[...], gt, m_ref[...], v_ref[...])
        d_ref[...] = d
        mo_ref[...] = mn
        vo_ref[...] = vn

    return pl.pallas_call(
        body, name=name, out_shape=[jax.ShapeDtypeStruct((r, c), F32)] * 4,
        compiler_params=pltpu.CompilerParams(vmem_limit_bytes=VMEM_LIMIT),
    )(gathered, w, m, v)


SMALL_UNIT = 1024


def _pack(parts):
    flat = []
    for p in parts:
        f = p.reshape(-1).astype(F32)
        pad = (-f.shape[0]) % SMALL_UNIT
        flat.append(jnp.pad(f, (0, pad)))
    return jnp.concatenate(flat).reshape(-1, 128)


def _unpack(buf, shapes):
    flat = buf.reshape(-1)
    out, off = [], 0
    for s in shapes:
        nel = math.prod(s)
        out.append(flat[off:off + nel].reshape(s))
        off += nel + ((-nel) % SMALL_UNIT)
    return out


def kernel(x, p, ffn1_w_in, ffn1_w_out, ln1_g, ln1_b, mix_w_in, conv_w, conv_b, conv_w_out, ssm_lam_re, ssm_lam_im, ssm_log_step, ssm_b_re, ssm_b_im, ssm_c_re, ssm_c_im, ssm_d, ssm_w_glu, mix_w_out, ln2_g, ln2_b, ffn2_w_in, ffn2_w_out, ln3_g, ln3_b, ple_w_in, ple_w_gate, ln4_g, ln4_b, loss_target, m_ffn1_w_in, m_ffn1_w_out, m_ln1_g, m_ln1_b, m_mix_w_in, m_conv_w, m_conv_b, m_conv_w_out, m_ssm_lam_re, m_ssm_lam_im, m_ssm_log_step, m_ssm_b_re, m_ssm_b_im, m_ssm_c_re, m_ssm_c_im, m_ssm_d, m_ssm_w_glu, m_mix_w_out, m_ln2_g, m_ln2_b, m_ffn2_w_in, m_ffn2_w_out, m_ln3_g, m_ln3_b, m_ple_w_in, m_ple_w_gate, m_ln4_g, m_ln4_b, v_ffn1_w_in, v_ffn1_w_out, v_ln1_g, v_ln1_b, v_mix_w_in, v_conv_w, v_conv_b, v_conv_w_out, v_ssm_lam_re, v_ssm_lam_im, v_ssm_log_step, v_ssm_b_re, v_ssm_b_im, v_ssm_c_re, v_ssm_c_im, v_ssm_d, v_ssm_w_glu, v_mix_w_out, v_ln2_g, v_ln2_b, v_ffn2_w_in, v_ffn2_w_out, v_ln3_g, v_ln3_b, v_ple_w_in, v_ple_w_gate, v_ln4_g, v_ln4_b):
    args = dict(locals())
    big = ['ffn1_w_in', 'ffn1_w_out', 'mix_w_in', 'conv_w_out', 'ssm_w_glu', 'mix_w_out',
           'ffn2_w_in', 'ffn2_w_out', 'ple_w_in', 'ple_w_gate']
    small = ['ln1_g', 'ln1_b', 'conv_b', 'ssm_lam_re', 'ssm_lam_im', 'ssm_log_step', 'ssm_b_re', 'ssm_b_im',
             'ssm_c_re', 'ssm_c_im', 'ssm_d', 'ln2_g', 'ln2_b', 'ln3_g', 'ln3_b', 'ln4_g', 'ln4_b']
    order = ['ffn1_w_in', 'ffn1_w_out', 'ln1_g', 'ln1_b', 'mix_w_in', 'conv_w', 'conv_b', 'conv_w_out',
             'ssm_lam_re', 'ssm_lam_im', 'ssm_log_step', 'ssm_b_re', 'ssm_b_im', 'ssm_c_re', 'ssm_c_im', 'ssm_d',
             'ssm_w_glu', 'mix_w_out', 'ln2_g', 'ln2_b', 'ffn2_w_in', 'ffn2_w_out', 'ln3_g', 'ln3_b',
             'ple_w_in', 'ple_w_gate', 'ln4_g', 'ln4_b']

    t = x.shape[1]
    d = x.shape[2]
    xc_, yc_, cc_ = lax.axis_index("x"), lax.axis_index("y"), lax.axis_index("c")
    me = (4 * xc_ + 2 * yc_ + cc_).astype(jnp.int32)
    cw_cols = conv_w.shape[2]

    turned = ('ffn1_w_in', 'ffn2_w_in')

    def local(a, nm):
        return jnp.swapaxes(a[0], 0, 1) if nm in turned else a[0]

    shard = {nm: local(args[nm], nm).astype(BF16) for nm in big}
    cw_pad = jnp.zeros((16, 128), F32).at[0:3, 0:cw_cols].set(conv_w[0])
    wf = shard['ffn1_w_in'].shape[0]

    def gather(arrays, tag, after=()):
        ((_, got, _),) = _run_plans_on_sequencer(
            [_gather_plan(arrays)], lambda x, y, c, chips: [(x, y, 1 - c), (1 - x, y, c), (x, 1 - y, c)],
            name=f"gather_{tag}", collective_id=2, after=after)
        return got

    w1in, cw_g = gather([shard['ffn1_w_in'], cw_pad], "ffn1_in")
    (w1out_g,) = gather([shard['ffn1_w_out']], "ffn1_out")
    (wmix,) = gather([shard['mix_w_in']], "mix_in")
    wco, wglu, wmo_g = gather([shard[nm] for nm in ('conv_w_out', 'ssm_w_glu', 'mix_w_out')], "mix_rest")
    (w2in,) = gather([shard['ffn2_w_in']], "ffn2_in")
    (w2out_g,) = gather([shard['ffn2_w_out']], "ffn2_out")
    wpin, wgate_g = gather([shard['ple_w_in'], shard['ple_w_gate']], "ple")
    cw_full = jnp.transpose(cw_g[:, 0:3, 0:cw_cols], (1, 0, 2)).reshape(3, N_DEV * cw_cols)
    cw8 = jnp.zeros((8, CONV_CH), F32).at[0:3, :].set(cw_full)

    s5_in = (ssm_lam_re[0], ssm_lam_im[0], ssm_log_step[0], ssm_b_re[0], ssm_b_im[0])
    (a_re, a_im, bb_re, bb_im), s5_vjp = jax.vjp(_s5_discretise, *s5_in)
    tab_f = _pow_table(a_re, a_im)
    tab_b = _pow_table(a_re, -a_im, descending=True)
    bmat_b = _compact(jnp.transpose(bb_re, (0, 2, 1)), jnp.transpose(bb_im, (0, 2, 1))).astype(BF16)
    cmat_tb = _compact(ssm_c_re[0], -ssm_c_im[0]).astype(BF16)
    dvec = ssm_d[0].reshape(1, SSM_W)

    xf = x[0]
    x_b = xf.astype(BF16)
    p_b = p[0, 0].astype(BF16)
    tgt = loss_target[0]
    tq = min(512, t)

    ffn_out = dict(ja='c', jb='c', nj=4, tm=tq, tn=d, tk=wf)
    def side_by_side(wb):
        return jnp.transpose(wb, (1, 0, 2)).reshape(wb.shape[1], N_DEV * wb.shape[2])

    tf = min(1024, t)
    a1, h1, _ = _ffn_in(x_b, w1in, name="ffn1_in", tm=tf)
    w1out = w1out_g.reshape(4, wf, d)
    r1, x1, x1b = _mm(a1, w1out, name="ffn1_out", **ffn_out, epilogue=_ln_epilogue(xf, ln1_g, ln1_b, 0.5))
    proj = _mm(x1b, wmix, name="mix_in", jb='b', jo='b', o_flat=True, nj=8, tm=tq, tn=512, tk=d)
    wmo = wmo_g.reshape(d, d)
    ycin = _conv_fwd(proj, cw8, conv_b, name="conv_fwd")
    wco, wglu = side_by_side(wco), side_by_side(wglu)
    yconv = _mm(ycin, wco, name="conv_out", tm=tq, tn=d, tk=CONV_CH)
    s_f, s_b, _ = _scan_fwd(proj, bmat_b, tab_f, name="scan_fwd")
    blk = dict(ja='b', jb='b', jo='b', nj=SCAN_NCB)
    ymm = _mm(s_b, cmat_tb, name="ssm_read", a_flat=True, o_flat=True, tb=True, tm=tq, tn=SCAN_UW, tk=2 * SCAN_CB, **blk)
    ys, sg, u_b = _s5_out(ymm, proj, dvec, name="ssm_out")
    glu = _mm(sg, wglu, name="glu_in", tm=tq, tn=d, tk=SSM_W)
    merged = _gate_fwd(yconv, glu, proj, name="gate_fwd")
    r2, x2, x2b = _mm(merged, wmo, name="mix_out", tm=tq, tn=d, tk=d, epilogue=_ln_epilogue(x1, ln2_g, ln2_b, 1.0))
    a2, h2, _ = _ffn_in(x2b, w2in, name="ffn2_in", tm=tf)
    w2out = w2out_g.reshape(4, wf, d)
    r3, x3, x3b = _mm(a2, w2out, name="ffn2_out", **ffn_out, epilogue=_ln_epilogue(x2, ln3_g, ln3_b, 0.5))
    wgate = wgate_g.reshape(d, d)
    pe = _mm(p_b, side_by_side(wpin), name="ple_in", tm=tq, tn=d, tk=p_b.shape[1])
    gp = _mm(x3b, wgate, name="ple_gate", tm=tq, tn=d, tk=d)

    dr4, dpe_b, dgp_b, dg4, db4, loss_part = _ln_bwd(x3, [pe, gp], ln4_g, ln4_b, [], name="ple_ln_bwd",
                                                     ple=True, target=tgt)
    gb, sib, rem = {}, {}, {}
    ids = jnp.stack([_dev(1 - xc_, yc_, cc_), _dev(xc_, 1 - yc_, cc_), _dev(1 - xc_, 1 - yc_, cc_)]).astype(jnp.int32)

    def blocked(nm, g):
        return g.reshape((N_DEV,) + args[nm].shape[1:])

    def to_sibling(*names):
        return _sibling_plan([gb[nm] for nm in names])

    def chip_sums(names, sibs):
        for nm, s in zip(names, sibs):
            sib[nm] = s
        return [_chip_partial(gb[nm], sib[nm], ids, name=f"chip_sum_{nm}") for nm in names]

    def on_sequencer(plan, peers, tag, cid, after=()):
        ((_, got, _),) = _run_plans_on_sequencer([plan], peers, name=tag, collective_id=cid, after=after)
        return got

    waiting = []

    def send_sibling(*names, after=()):
        got = on_sequencer(to_sibling(*names), lambda x, y, c, chips: [(x, y, 1 - c)], f"grad_sibling_{names[0]}", 3,
                           after=after)
        waiting.append((names, got))

    def send_chips(count=None, after=()):
        pin = ()
        for _ in range(len(waiting) if count is None else count):
            names, got = waiting.pop(0)
            parts = chip_sums(names, got)
            r_ = on_sequencer(_chip_plan(parts), lambda x, y, c, chips: [(*chip, c) for chip in chips],
                              f"grad_chips_{names[0]}", 1, after=after)
            for i, nm in enumerate(names):
                rem[nm] = r_[3 * i:3 * i + 3]
            pin = (parts[0],)
        return pin

    ffn_in_dg = dict(ja='c', jb='c', nj=8, tm=tq, tn=d, tk=wf)
    ffn_in_wg = dict(ja='b', jo='b', ta=True, nj=8, tm=wf, tn=d, tk=t)
    ffn_out_wg = dict(ja='b', jo='b', ta=True, nj=4, tm=wf, tn=d, tk=t)

    gb['ple_w_in'] = _mm(p_b, dpe_b, name="ple_in_wg", jb='b', jo='b', b_flat=True, ta=True, nj=8,
                         tm=p_b.shape[1], tn=128, tk=t)
    gb['ple_w_gate'] = blocked('ple_w_gate', _mm(x3b, dgp_b, name="ple_gate_wg", ta=True, tm=d, tn=d, tk=t))
    g_ple = ('ple_w_in', 'ple_w_gate')
    dx3_g, (s_,) = _mm(dgp_b, wgate, name="ple_gate_dg", tb=True, tm=tq, tn=d, tk=d, plans=[to_sibling(*g_ple)])
    waiting.append((g_ple, s_))

    dr3, df2_b, dg3, db3 = _ln_bwd(r3, [], ln3_g, ln3_b, [(dr4, ALPHA), (dx3_g, 1.0)], name="ffn2_ln_bwd", fs=0.5)
    pin = send_chips()
    dh2, _ = _ffn_out_dg(df2_b, w2out, h2, name="ffn2_out_dg", tm=tf)
    gb['ffn2_w_out'] = blocked('ffn2_w_out', _mm(a2, df2_b, name="ffn2_out_wg", **ffn_out_wg, after=pin))
    send_sibling('ffn2_w_out')
    dx2_f = _mm(dh2, w2in, name="ffn2_in_dg", **ffn_in_dg)
    pin = send_chips()
    gb['ffn2_w_in'] = _mm(dh2, x2b, name="ffn2_in_wg", **ffn_in_wg, after=pin)

    dr2, dmix_b, dg2, db2 = _ln_bwd(r2, [], ln2_g, ln2_b, [(dr3, ALPHA), (dx2_f, 1.0)], name="mix_ln_bwd")
    dmerged = _mm(dmix_b, wmo, name="mix_out_dg", tb=True, tm=tq, tn=d, tk=d)
    send_sibling('ffn2_w_in', after=(dmerged,))
    gb['mix_w_out'] = blocked('mix_w_out', _mm(merged, dmix_b, name="mix_out_wg", ta=True, tm=d, tn=d, tk=t))
    dyconv_b, dglu_b, dgate_b = _gate_bwd(dmerged, yconv, glu, proj, name="gate_bwd")
    pin = send_chips()
    gb['conv_w_out'] = _mm(ycin, dyconv_b, name="conv_out_wg", jb='b', jo='b', b_flat=True, ta=True, nj=8,
                           tm=CONV_CH, tn=128, tk=t, after=pin)
    dycin = _mm(dyconv_b, wco, name="conv_out_dg", tb=True, tm=tq, tn=CONV_CH, tk=d)
    gb['ssm_w_glu'] = _mm(sg, dglu_b, name="glu_in_wg", jb='b', jo='b', b_flat=True, ta=True, nj=8,
                          tm=SSM_W, tn=256, tk=t)
    dsg = _mm(dglu_b, wglu, name="glu_in_dg", tb=True, tm=tq, tn=SSM_W, tk=2 * d)
    dys, dys_b, dd = _s5_bwd_in(dsg, ys, proj, name="ssm_out_bwd")
    h_b, da_acc, _ = _scan_bwd(dys_b, cmat_tb, s_f, tab_b, name="scan_bwd")
    send_sibling('mix_w_out', 'conv_w_out', 'ssm_w_glu', after=(h_b,))
    dumm = _mm(h_b, bmat_b, name="ssm_write_dg", a_flat=True, o_flat=True, tb=True, tm=tq, tn=SCAN_UW,
               tk=2 * SCAN_CB, **blk)
    du_b = _s5_du(dumm, dys, dvec, name="ssm_du")
    g_bmat = _mm(u_b, h_b, name="ssm_write_wg", a_flat=True, b_flat=True, ta=True, tm=SCAN_UW,
                 tn=2 * SCAN_CB, tk=t, **blk)
    pin = send_chips()
    g_cmat = _mm(dys_b, s_b, name="ssm_read_wg", a_flat=True, b_flat=True, ta=True, tm=SCAN_UW,
                 tn=2 * SCAN_CB, tk=t, **blk, after=pin)
    dcb_b, dcc_b, dch_b, dconv = _conv_bwd(dycin, proj, cw8, conv_b, name="conv_bwd")
    dproj = jnp.concatenate([dcb_b, dcc_b, dch_b, du_b, dgate_b], axis=1)
    gb['mix_w_in'] = _mm(x1b, dproj, name="mix_in_wg", jb='b', jo='b', b_flat=True, ta=True, nj=8,
                         tm=d, tn=512, tk=t)
    send_sibling('mix_w_in')
    dx1_m = _mm(dproj, wmix, name="mix_in_dg", ja='c', jb='c', a_flat=True, tb=True, nj=8, tm=tq, tn=d, tk=512)
    pin_mix = send_chips()

    dr1, df1_b, dg1, db1 = _ln_bwd(r1, [], ln1_g, ln1_b, [(dr2, ALPHA), (dx1_m, 1.0)], name="ffn1_ln_bwd", fs=0.5)
    dh1, _ = _ffn_out_dg(df1_b, w1out, h1, name="ffn1_out_dg", tm=tf)

    da_sum = jnp.sum(da_acc, axis=0)
    da_re, da_im = _unperm_cols(da_sum)
    gbb_re, gbb_im = [jnp.transpose(v, (0, 2, 1)) for v in _compact_extract(g_bmat, SSM_GROUP)]
    g_c_re, g_c_im_neg = _compact_extract(g_cmat, SSM_GROUP)
    g_c_im = -g_c_im_neg
    g_lam_re, g_lam_im, g_log_step, g_b_re, g_b_im = s5_vjp(
        (da_re.reshape(SSM_GROUPS, SSM_STATE), da_im.reshape(SSM_GROUPS, SSM_STATE), gbb_re, gbb_im))
    g_d = dd.reshape(SSM_GROUPS, SSM_GROUP)

    small_g = {'ln1_g': dg1, 'ln1_b': db1, 'conv_b': dconv[3:4], 'ssm_lam_re': g_lam_re, 'ssm_lam_im': g_lam_im,
               'ssm_log_step': g_log_step, 'ssm_b_re': g_b_re, 'ssm_b_im': g_b_im, 'ssm_c_re': g_c_re,
               'ssm_c_im': g_c_im, 'ssm_d': g_d, 'ln2_g': dg2, 'ln2_b': db2, 'ln3_g': dg3, 'ln3_b': db3,
               'ln4_g': dg4, 'ln4_b': db4}
    small_shapes = [args[nm].shape for nm in small] + [(3, CONV_CH), (1,)]
    g_pack = _pack([small_g[nm] for nm in small] + [dconv[0:3], loss_part[0:1, 0:1]])

    res = {}
    me1 = me.reshape(1)

    def update(nm):
        upd = _shard_update(gb[nm], sib[nm], rem[nm], me1, local(args[nm], nm), local(args['m_' + nm], nm),
                            local(args['v_' + nm], nm), name=f"update_{nm}")
        for key, val in zip(('grad_', 'delta_', 'new_m_', 'new_v_'), upd):
            res[key + nm] = (jnp.swapaxes(val, 0, 1) if nm in turned else val)[None]

    (g_all,) = gather([g_pack], "small", after=(dh1,))
    gb['ffn1_w_in'] = _mm(dh1, x_b, name="ffn1_in_wg", **ffn_in_wg, after=pin_mix)
    send_sibling('ffn1_w_in')
    gb['ffn1_w_out'] = blocked('ffn1_w_out', _mm(a1, df1_b, name="ffn1_out_wg", **ffn_out_wg))
    send_sibling('ffn1_w_out')
    pin = send_chips(1)
    (grad_x,) = _mm(dh1, w1in, name="ffn1_in_dg", **ffn_in_dg, after=pin,
                    epilogue=(lambda pr, drv: (pr + ALPHA * drv,), (dr1,), (), (F32,)))
    last = ['ffn1_w_in', 'ffn1_w_out']
    for nm in big:
        if nm not in last:
            update(nm)
    send_chips(after=(res['new_v_ffn2_w_out'],))

    def full_cw(a):
        return lax.dynamic_update_slice(jnp.zeros((3, CONV_CH), F32), a[0], (0, me * cw_cols))

    zero1 = jnp.zeros((1,), F32)
    w_pack = _pack([args[nm] for nm in small] + [full_cw(conv_w), zero1])
    m_pack = _pack([args['m_' + nm] for nm in small] + [full_cw(m_conv_w), zero1])
    v_pack = _pack([args['v_' + nm] for nm in small] + [full_cw(v_conv_w), zero1])
    sg_sum, sd, sm, sv = _small_update(g_all, w_pack, m_pack, v_pack, name="small_update")
    for key, buf in (('grad_', sg_sum), ('delta_', sd), ('new_m_', sm), ('new_v_', sv)):
        parts = _unpack(buf, small_shapes)
        for nm, val in zip(small, parts[:len(small)]):
            res[key + nm] = val
        res[key + 'conv_w'] = lax.dynamic_slice(parts[len(small)], (0, me * cw_cols), (3, cw_cols))[None]
        if key == 'grad_':
            loss = parts[-1][0]

    for nm in last:
        update(nm)

    outs = [loss, grad_x[None]]
    for key in ('grad_', 'delta_', 'new_m_', 'new_v_'):
        outs += [res[key + nm] for nm in order]
    return tuple(outs)
```

```python
import functools
import math

import jax
import jax.numpy as jnp
from jax import lax
from jax.experimental import pallas as pl
from jax.experimental.pallas import tpu as pltpu
from jax.experimental.pallas import tpu_sc as plsc

F32 = jnp.float32
BF16 = jnp.bfloat16
MESH = pl.DeviceIdType.MESH

N_DEV = 8
ALPHA = 2.0 ** 0.25
LN_EPS = 1e-5
CONV_CH = 512
SSM_W = 512
SSM_GROUPS = 32
SSM_GROUP = 16
SSM_STATE = 64
SSM_CH = SSM_GROUPS * SSM_STATE
SCAN_CB = 512
SCAN_NCB = SSM_CH // SCAN_CB
SCAN_GPB = SSM_GROUPS // SCAN_NCB
SCAN_UW = SCAN_GPB * SSM_GROUP
SCAN_TT = 256
ADAM_LR = 0.001
ADAM_B1 = 0.9
ADAM_B2 = 0.999
ADAM_EPS = 1e-08
ADAM_WD = 0.01
ADAM_STEP = 10
VMEM_LIMIT = 56 * 1024 * 1024


def _cparams(sem=None, **kw):
    return pltpu.CompilerParams(dimension_semantics=sem, vmem_limit_bytes=VMEM_LIMIT, **kw)


def _mm(a, b, *, name, ja=None, jb=None, jo=None, a_flat=False, b_flat=False, o_flat=False,
        ta=False, tb=False, tm, tn, tk, nj=1, out_dtype=F32, plans=(), epilogue=None):
    def dims(arr, j, flat):
        if j is None:
            return arr.shape
        if flat:
            return (arr.shape[0], arr.shape[1] // nj)
        assert arr.shape[0] == nj, (name, arr.shape, nj)
        return arr.shape[1:]

    ar, ac = dims(a, ja, a_flat)
    br, bc = dims(b, jb, b_flat)
    m, k = (ac, ar) if ta else (ar, ac)
    k2, n = (bc, br) if tb else (br, bc)
    assert k == k2, (name, a.shape, b.shape)
    assert m % tm == 0 and n % tn == 0 and k % tk == 0, (name, m, n, k, tm, tn, tk)
    njb = nj if 'b' in (ja, jb) else 1
    njc = nj if 'c' in (ja, jb) else 1
    nk = k // tk
    j_inside = njc > 1 and nk == 1 and not ta
    n_in = njc if j_inside else 1
    nred = nk if j_inside else njc * nk
    grid = (njb, m // tm, n // tn, 1 if j_inside else njc, nk)

    def make_spec(j, flat, blk, rfn, cfn, cols_per_j):
        def jsel(g, c):
            return g if j == 'b' else c
        if j is None:
            return pl.BlockSpec(blk, lambda g, i, jn, c, kk: (rfn(i, jn, kk), cfn(i, jn, kk)))
        if j == 'c' and j_inside:
            if flat:
                return pl.BlockSpec((blk[0], nj * blk[1]), lambda g, i, jn, c, kk: (rfn(i, jn, kk), 0))
            return pl.BlockSpec((nj,) + blk, lambda g, i, jn, c, kk: (0, rfn(i, jn, kk), cfn(i, jn, kk)))
        if flat:
            nb = cols_per_j // blk[1]
            return pl.BlockSpec(blk, lambda g, i, jn, c, kk: (rfn(i, jn, kk), jsel(g, c) * nb + cfn(i, jn, kk)))
        return pl.BlockSpec((None,) + blk,
                            lambda g, i, jn, c, kk: (jsel(g, c), rfn(i, jn, kk), cfn(i, jn, kk)))

    if ta:
        a_spec = make_spec(ja, a_flat, (tk, tm), lambda i, jn, kk: kk, lambda i, jn, kk: i, ac)
    else:
        a_spec = make_spec(ja, a_flat, (tm, tk), lambda i, jn, kk: i, lambda i, jn, kk: kk, ac)
    if tb:
        b_spec = make_spec(jb, b_flat, (tn, tk), lambda i, jn, kk: jn, lambda i, jn, kk: kk, bc)
    else:
        b_spec = make_spec(jb, b_flat, (tk, tn), lambda i, jn, kk: kk, lambda i, jn, kk: jn, bc)
    o_spec = make_spec(jo, o_flat, (tm, tn), lambda i, jn, kk: i, lambda i, jn, kk: jn, n)
    if jo is None:
        out_shape = (m, n)
    elif o_flat:
        out_shape = (m, nj * n)
    else:
        out_shape = (nj, m, n)

    dn = (((0 if ta else 1,), (1 if tb else 0,)), ((), ()))

    def operand(ref, j, flat, jj, width):
        if not (j == 'c' and j_inside):
            return ref[...]
        return ref[:, jj * width:(jj + 1) * width] if flat else ref[jj]

    e_fn, e_rows, e_vecs, e_dtypes = epilogue if epilogue else (None, (), (), (out_dtype,))
    assert not epilogue or (nred == 1 and jo is None), name
    n_e = len(e_rows) + len(e_vecs)
    n_o = len(e_dtypes)

    def body(a_ref, b_ref, *rest):
        e_refs, o_refs, scratch = rest[:n_e], rest[n_e:n_e + n_o], rest[n_e + n_o:]
        o_ref = o_refs[0]
        p = None
        for jj in range(n_in):
            q = lax.dot_general(operand(a_ref, ja, a_flat, jj, tk), operand(b_ref, jb, b_flat, jj, tk if tb else tn),
                                dn, preferred_element_type=F32)
            p = q if p is None else p + q
        if epilogue:
            for ref, val in zip(o_refs, e_fn(p, *[r[...] for r in e_refs])):
                ref[...] = val.astype(ref.dtype)
        elif nred == 1:
            o_ref[...] = p.astype(o_ref.dtype)
        else:
            acc = scratch[0]
            r = pl.program_id(3) * nk + pl.program_id(4)

            @pl.when(r == 0)
            def _():
                acc[...] = p

            @pl.when(r > 0)
            def _():
                acc[...] += p

            @pl.when(r == nred - 1)
            def _():
                o_ref[...] = acc[...].astype(o_ref.dtype)

    vec_spec = pl.BlockSpec((1, tn), lambda g, i, jn, c, kk: (0, jn))
    res = _call_with_plans(
        body, plans, name=name, grid=grid,
        in_specs=[a_spec, b_spec] + [o_spec] * len(e_rows) + [vec_spec] * len(e_vecs), out_specs=[o_spec] * n_o,
        out_shape=[jax.ShapeDtypeStruct(out_shape, dt) for dt in e_dtypes],
        scratch_shapes=[] if nred == 1 else [pltpu.VMEM((tm, tn), F32)],
        semantics=("parallel", "parallel", "parallel", "arbitrary", "arbitrary"), operands=(a, b, *e_rows, *e_vecs))
    outs = res[0] if epilogue else res[0][0]
    return (outs, res[1]) if plans else outs


def _sigmoid(v):
    return jax.nn.sigmoid(v)


def _row_spec(tm, cols, colblk=0):
    return pl.BlockSpec((tm, cols), lambda i: (i, colblk))


def _vec_spec(cols):
    return pl.BlockSpec((1, cols), lambda i: (0, 0))


def _ffn_in(xb, win, *, name, tm, plans=()):
    t, d = xb.shape
    nj, w, _ = win.shape
    half = nj // 2
    dn = (((1,), (1,)), ((), ()))

    def body(x_ref, wg_ref, wu_ref, a_ref, gu_ref):
        xv = x_ref[...]
        g = lax.dot_general(xv, wg_ref[...], dn, preferred_element_type=F32)
        u = lax.dot_general(xv, wu_ref[...], dn, preferred_element_type=F32)
        a_ref[...] = (g * _sigmoid(g) * u).astype(BF16)
        gu_ref[0] = g.astype(BF16)
        gu_ref[1] = u.astype(BF16)

    (a, gu), riders = _call_with_plans(
        body, plans, name=name, grid=(half, t // tm),
        in_specs=[pl.BlockSpec((tm, d), lambda j, i: (i, 0)),
                  pl.BlockSpec((None, w, d), lambda j, i: (j, 0, 0)),
                  pl.BlockSpec((None, w, d), lambda j, i: (j + half, 0, 0))],
        out_specs=[pl.BlockSpec((None, tm, w), lambda j, i: (j, i, 0)),
                   pl.BlockSpec((2, None, tm, w), lambda j, i: (0, j, i, 0))],
        out_shape=[jax.ShapeDtypeStruct((half, t, w), BF16), jax.ShapeDtypeStruct((2, half, t, w), BF16)],
        scratch_shapes=[], semantics=("parallel", "parallel"), operands=(xb, win, win))
    return a, gu, riders


def _ffn_out_dg(dfb, wout, gu, *, name, tm, plans=()):
    t, d = dfb.shape
    half, w, _ = wout.shape
    dn = (((1,), (1,)), ((), ()))

    def body(df_ref, w_ref, gu_ref, dh_ref):
        da = lax.dot_general(df_ref[...], w_ref[...], dn, preferred_element_type=F32)
        g = gu_ref[0].astype(F32)
        u = gu_ref[1].astype(F32)
        sg = _sigmoid(g)
        dh_ref[0] = (da * u * (sg * (1.0 + g * (1.0 - sg)))).astype(BF16)
        dh_ref[1] = (da * (g * sg)).astype(BF16)

    (out,), riders = _call_with_plans(
        body, plans, name=name, grid=(half, t // tm),
        in_specs=[pl.BlockSpec((tm, d), lambda j, i: (i, 0)),
                  pl.BlockSpec((None, w, d), lambda j, i: (j, 0, 0)),
                  pl.BlockSpec((2, None, tm, w), lambda j, i: (0, j, i, 0))],
        out_specs=[pl.BlockSpec((2, None, tm, w), lambda j, i: (0, j, i, 0))],
        out_shape=[jax.ShapeDtypeStruct((2, half, t, w), BF16)],
        scratch_shapes=[], semantics=("parallel", "parallel"), operands=(dfb, wout, gu))
    return out.reshape(2 * half, t, w), riders


def _ln_stats(r):
    mu = jnp.mean(r, axis=-1, keepdims=True)
    xc = r - mu
    var = jnp.mean(xc * xc, axis=-1, keepdims=True)
    rstd = lax.rsqrt(var + LN_EPS)
    return xc * rstd, rstd


def _ln_epilogue(xin, gamma, beta, fs):
    def fn(p, xv, g, b):
        r = ALPHA * xv + fs * p
        xh, _ = _ln_stats(r)
        y = xh * g + b
        return r, y, y

    return fn, (xin,), (gamma, beta), (F32, F32, BF16)


def _ln_bwd(xin, fparts, gamma, beta, grads, *, name, fs=1.0, ple=False, target=None, tm=512):
    t, d = xin.shape
    nf = len(fparts)
    ng = len(grads)
    coefs = [c for _, c in grads]
    use_t = target is not None
    n_fout = 2 if ple else 1

    def body(*refs):
        pos = 0
        x_ref = refs[pos]; pos += 1
        f_refs = refs[pos:pos + nf]; pos += nf
        g_ref, b_ref = refs[pos:pos + 2]; pos += 2
        gr_refs = refs[pos:pos + ng]; pos += ng
        if use_t:
            t_ref = refs[pos]; pos += 1
        dr_ref = refs[pos]; pos += 1
        fo_refs = refs[pos:pos + n_fout]; pos += n_fout
        dg_ref, db_ref = refs[pos:pos + 2]; pos += 2
        if use_t:
            loss_ref = refs[pos]; pos += 1
        i = pl.program_id(0)

        if ple:
            pe = f_refs[0][...]
            sg = _sigmoid(f_refs[1][...])
            resid = ALPHA * x_ref[...] + pe * sg
        else:
            resid = x_ref[...]
        xh, rstd = _ln_stats(resid)
        gam = g_ref[...]
        if use_t:
            diff = xh * gam + b_ref[...] - t_ref[...]
            dy = diff * (1.0 / d)
            lpart = 0.5 * jnp.sum(jnp.sum(diff * diff, axis=-1, keepdims=True), axis=0, keepdims=True) * (1.0 / d)
        else:
            dy = coefs[0] * gr_refs[0][...]
            for c, r in zip(coefs[1:], gr_refs[1:]):
                dy = dy + c * r[...]
        dxh = dy * gam
        m1 = jnp.mean(dxh, axis=-1, keepdims=True)
        m2 = jnp.mean(dxh * xh, axis=-1, keepdims=True)
        dr = rstd * (dxh - m1 - xh * m2)
        dr_ref[...] = dr
        if ple:
            fo_refs[0][...] = (dr * sg).astype(BF16)
            fo_refs[1][...] = (dr * pe * (sg * (1.0 - sg))).astype(BF16)
        else:
            fo_refs[0][...] = (fs * dr).astype(BF16)
        dgp = jnp.sum(dy * xh, axis=0, keepdims=True)
        dbp = jnp.sum(dy, axis=0, keepdims=True)

        @pl.when(i == 0)
        def _():
            dg_ref[...] = dgp
            db_ref[...] = dbp
            if use_t:
                loss_ref[...] = jnp.broadcast_to(lpart, loss_ref.shape)

        @pl.when(i > 0)
        def _():
            dg_ref[...] += dgp
            db_ref[...] += dbp
            if use_t:
                loss_ref[...] += jnp.broadcast_to(lpart, loss_ref.shape)

    ins = [xin, *fparts, gamma, beta, *[g for g, _ in grads]] + ([target] if use_t else [])
    in_specs = ([_row_spec(tm, d)] * (1 + nf) + [_vec_spec(d), _vec_spec(d)] + [_row_spec(tm, d)] * ng
                + ([_row_spec(tm, d)] if use_t else []))
    out_specs = [_row_spec(tm, d)] * (1 + n_fout) + [_vec_spec(d), _vec_spec(d)] + ([_vec_spec(128)] if use_t else [])
    out_shape = ([jax.ShapeDtypeStruct((t, d), F32)] + [jax.ShapeDtypeStruct((t, d), BF16)] * n_fout
                 + [jax.ShapeDtypeStruct((1, d), F32)] * 2 + ([jax.ShapeDtypeStruct((1, 128), F32)] if use_t else []))
    return pl.pallas_call(
        body, name=name, grid=(t // tm,), in_specs=in_specs, out_specs=out_specs, out_shape=out_shape,
        compiler_params=_cparams(("arbitrary",)),
    )(*ins)


def _conv_fwd(proj, cw, cb, *, name, tm=512):
    t = proj.shape[0]
    c = CONV_CH
    hb = tm // 8

    def body(b_ref, c_ref, h_ref, cp_ref, hp_ref, w_ref, bias_ref, o_ref, q_scr):
        i = pl.program_id(0)
        q = c_ref[...] * h_ref[...]
        halo = jnp.where(i > 0, cp_ref[...] * hp_ref[...], 0.0)
        q_scr[0:8, :] = halo
        q_scr[8:, :] = q
        z = (w_ref[2:3, :] * q + w_ref[1:2, :] * q_scr[pl.ds(7, tm), :] + w_ref[0:1, :] * q_scr[pl.ds(6, tm), :]
             + bias_ref[...])
        o_ref[...] = (b_ref[...] * z).astype(BF16)

    prev = lambda blk: pl.BlockSpec((8, c), lambda i: (jnp.maximum(i * hb - 1, 0), blk))
    return pl.pallas_call(
        body, name=name, grid=(t // tm,),
        in_specs=[_row_spec(tm, c, 0), _row_spec(tm, c, 1), _row_spec(tm, c, 2), prev(1), prev(2),
                  pl.BlockSpec((8, c), lambda i: (0, 0)), _vec_spec(c)],
        out_specs=_row_spec(tm, c),
        out_shape=jax.ShapeDtypeStruct((t, c), BF16),
        scratch_shapes=[pltpu.VMEM((tm + 8, c), F32)],
        compiler_params=_cparams(("parallel",)),
    )(proj, proj, proj, proj, proj, cw, cb)


def _conv_bwd(dyc, proj, cw, cb, *, name, tm=512):
    t = proj.shape[0]
    c = CONV_CH
    hb = tm // 8
    nblk = t // 8

    def body(d_ref, b_ref, c_ref, h_ref, cp_ref, hp_ref, dn_ref, bn_ref, w_ref, bias_ref,
             db_ref, dc_ref, dh_ref, dw_ref, q_scr, z_scr):
        i = pl.program_id(0)
        last = pl.num_programs(0) - 1
        cc = c_ref[...]
        ch = h_ref[...]
        q = cc * ch
        halo = jnp.where(i > 0, cp_ref[...] * hp_ref[...], 0.0)
        q_scr[0:8, :] = halo
        q_scr[8:, :] = q
        w0, w1, w2 = w_ref[0:1, :], w_ref[1:2, :], w_ref[2:3, :]
        qm1 = q_scr[pl.ds(7, tm), :]
        qm2 = q_scr[pl.ds(6, tm), :]
        z = w2 * q + w1 * qm1 + w0 * qm2 + bias_ref[...]
        d = d_ref[...]
        bb = b_ref[...]
        db_ref[...] = (d * z).astype(BF16)
        dz = d * bb
        z_scr[0:tm, :] = dz
        z_scr[tm:, :] = jnp.where(i < last, dn_ref[...] * bn_ref[...], 0.0)
        dq = w2 * dz + w1 * z_scr[pl.ds(1, tm), :] + w0 * z_scr[pl.ds(2, tm), :]
        dc_ref[...] = (dq * ch).astype(BF16)
        dh_ref[...] = (dq * cc).astype(BF16)
        row = lax.broadcasted_iota(jnp.int32, (8, c), 0)
        part = jnp.zeros((8, c), F32)
        for k, term in enumerate((dz * qm2, dz * qm1, dz * q, dz)):
            part = jnp.where(row == k, jnp.sum(term, axis=0, keepdims=True), part)

        @pl.when(i == 0)
        def _():
            dw_ref[...] = part

        @pl.when(i > 0)
        def _():
            dw_ref[...] += part

    prev = lambda blk: pl.BlockSpec((8, c), lambda i: (jnp.maximum(i * hb - 1, 0), blk))
    nxt_p = pl.BlockSpec((8, c), lambda i: (jnp.minimum((i + 1) * hb, nblk - 1), 0))
    nxt_d = pl.BlockSpec((8, c), lambda i: (jnp.minimum((i + 1) * hb, nblk - 1), 0))
    return pl.pallas_call(
        body, name=name, grid=(t // tm,),
        in_specs=[_row_spec(tm, c), _row_spec(tm, c, 0), _row_spec(tm, c, 1), _row_spec(tm, c, 2),
                  prev(1), prev(2), nxt_d, nxt_p, pl.BlockSpec((8, c), lambda i: (0, 0)), _vec_spec(c)],
        out_specs=[_row_spec(tm, c)] * 3 + [pl.BlockSpec((8, c), lambda i: (0, 0))],
        out_shape=[jax.ShapeDtypeStruct((t, c), BF16)] * 3 + [jax.ShapeDtypeStruct((8, c), F32)],
        scratch_shapes=[pltpu.VMEM((tm + 8, c), F32), pltpu.VMEM((tm + 8, c), F32)],
        compiler_params=_cparams(("arbitrary",)),
    )(dyc, proj, proj, proj, proj, proj, dyc, proj, cw, cb)


def _gate_fwd(yconv, glu, proj, *, name, tm=512):
    t, d = yconv.shape

    def body(yc_ref, ga_ref, gb_ref, gc_ref, gs_ref, o_ref):
        yssm = ga_ref[...] * _sigmoid(gb_ref[...])
        o_ref[...] = (_sigmoid(gc_ref[...]) * yc_ref[...] + _sigmoid(gs_ref[...]) * yssm).astype(BF16)

    return pl.pallas_call(
        body, name=name, grid=(t // tm,),
        in_specs=[_row_spec(tm, d), _row_spec(tm, d, 0), _row_spec(tm, d, 1), _row_spec(tm, d, 2), _row_spec(tm, d, 3)],
        out_specs=_row_spec(tm, d), out_shape=jax.ShapeDtypeStruct((t, d), BF16),
        compiler_params=_cparams(("parallel",)),
    )(yconv, glu, glu, proj, proj)


def _gate_bwd(dm, yconv, glu, proj, *, name, tm=512):
    t, d = yconv.shape

    def body(dm_ref, yc_ref, ga_ref, gb_ref, gc_ref, gs_ref, dyc_ref, dglu_ref, dgate_ref):
        dmv = dm_ref[...]
        sc = _sigmoid(gc_ref[...])
        ss = _sigmoid(gs_ref[...])
        sb = _sigmoid(gb_ref[...])
        ga = ga_ref[...]
        yssm = ga * sb
        dyc_ref[...] = (dmv * sc).astype(BF16)
        dgate_ref[:, 0:d] = (dmv * yc_ref[...] * (sc * (1.0 - sc))).astype(BF16)
        dys = dmv * ss
        dgate_ref[:, d:2 * d] = (dmv * yssm * (ss * (1.0 - ss))).astype(BF16)
        dglu_ref[:, 0:d] = (dys * sb).astype(BF16)
        dglu_ref[:, d:2 * d] = (dys * ga * (sb * (1.0 - sb))).astype(BF16)

    return pl.pallas_call(
        body, name=name, grid=(t // tm,),
        in_specs=[_row_spec(tm, d), _row_spec(tm, d), _row_spec(tm, d, 0), _row_spec(tm, d, 1),
                  _row_spec(tm, d, 2), _row_spec(tm, d, 3)],
        out_specs=[_row_spec(tm, d), _row_spec(tm, 2 * d), _row_spec(tm, 2 * d)],
        out_shape=[jax.ShapeDtypeStruct((t, d), BF16), jax.ShapeDtypeStruct((t, 2 * d), BF16),
                   jax.ShapeDtypeStruct((t, 2 * d), BF16)],
        compiler_params=_cparams(("parallel",)),
    )(dm, yconv, glu, glu, proj, proj)


_GELU_C = math.sqrt(2.0 / math.pi)


def _gelu(v):
    return 0.5 * v * (1.0 + jnp.tanh(_GELU_C * (v + 0.044715 * v * v * v)))


def _gelu_grad(v):
    th = jnp.tanh(_GELU_C * (v + 0.044715 * v * v * v))
    return 0.5 * (1.0 + th) + 0.5 * v * (1.0 - th * th) * (_GELU_C * (1.0 + 3.0 * 0.044715 * v * v))


def _cmul(ar, ai, br, bi):
    return ar * br - ai * bi, ar * bi + ai * br


def _scan_fwd(proj, bmat, tab, *, name, plans=()):
    t = proj.shape[0]
    tt, cbw = SCAN_TT, SCAN_CB
    w2 = 2 * cbw

    def body(u_ref, b_ref, tab_ref, s_ref, sb_ref, bu_scr, carry):
        ti = pl.program_id(1)

        @pl.when(ti == 0)
        def _():
            carry[...] = jnp.zeros_like(carry)

        bu_scr[...] = jnp.dot(u_ref[...].astype(BF16), b_ref[...], preferred_element_type=F32)
        row = lax.broadcasted_iota(jnp.int32, (8, cbw), 0)

        def blk(bi, c):
            cr, ci = c
            r0 = pl.multiple_of(bi * 8, 8)
            xr = bu_scr[pl.ds(r0, 8), 0:cbw]
            xi = bu_scr[pl.ds(r0, 8), cbw:w2]
            for k, sh in enumerate((1, 2, 4)):
                kr = tab_ref[k:k + 1, 0:cbw]
                ki = tab_ref[k:k + 1, cbw:w2]
                sr = jnp.where(row >= sh, pltpu.roll(xr, sh, 0), 0.0)
                si = jnp.where(row >= sh, pltpu.roll(xi, sh, 0), 0.0)
                pr, pi = _cmul(kr, ki, sr, si)
                xr = xr + pr
                xi = xi + pi
            pr, pi = _cmul(tab_ref[8:16, 0:cbw], tab_ref[8:16, cbw:w2], cr, ci)
            xr = xr + pr
            xi = xi + pi
            s_ref[pl.ds(r0, 8), 0:cbw] = xr
            s_ref[pl.ds(r0, 8), cbw:w2] = xi
            return (jnp.broadcast_to(xr[7:8, :], (8, cbw)), jnp.broadcast_to(xi[7:8, :], (8, cbw)))

        cr, ci = lax.fori_loop(0, tt // 8, blk, (carry[:, 0:cbw], carry[:, cbw:w2]))
        carry[:, 0:cbw] = cr
        carry[:, cbw:w2] = ci
        sb_ref[...] = s_ref[...].astype(BF16)

    (s, sb), riders = _call_with_plans(
        body, plans, name=name, grid=(SCAN_NCB, t // tt),
        in_specs=[pl.BlockSpec((tt, SCAN_UW), lambda cb, ti: (ti, 3 * SCAN_NCB + cb)),
                  pl.BlockSpec((None, SCAN_UW, w2), lambda cb, ti: (cb, 0, 0)),
                  pl.BlockSpec((16, w2), lambda cb, ti: (0, cb))],
        out_specs=[pl.BlockSpec((tt, w2), lambda cb, ti: (ti, cb))] * 2,
        out_shape=[jax.ShapeDtypeStruct((t, 2 * SSM_CH), F32), jax.ShapeDtypeStruct((t, 2 * SSM_CH), BF16)],
        scratch_shapes=[pltpu.VMEM((tt, w2), F32), pltpu.VMEM((8, w2), F32)],
        semantics=("parallel", "arbitrary"), operands=(proj, bmat, tab))
    return s, sb, riders


def _scan_bwd(dyb, cmat_t, s, tabb, *, name, plans=()):
    t = s.shape[0]
    tt, cbw = SCAN_TT, SCAN_CB
    w2 = 2 * cbw
    nt = t // tt
    hb = tt // 8

    def body(dy_ref, c_ref, s_ref, sp_ref, tab_ref, h_ref, da_ref, g_scr, s_scr, carry):
        ti = pl.program_id(1)

        @pl.when(ti == 0)
        def _():
            carry[...] = jnp.zeros_like(carry)
            da_ref[...] = jnp.zeros_like(da_ref)

        g_scr[...] = jnp.dot(dy_ref[...], c_ref[...], preferred_element_type=F32)
        s_scr[0:8, :] = jnp.where(ti < nt - 1, sp_ref[...], 0.0)
        s_scr[8:, :] = s_ref[...]
        row = lax.broadcasted_iota(jnp.int32, (8, cbw), 0)

        def blk(k, c):
            cr, ci, ar, ai = c
            bi = hb - 1 - k
            r0 = pl.multiple_of(bi * 8, 8)
            xr = g_scr[pl.ds(r0, 8), 0:cbw]
            xi = g_scr[pl.ds(r0, 8), cbw:w2]
            for j, sh in enumerate((1, 2, 4)):
                kr = tab_ref[j:j + 1, 0:cbw]
                ki = tab_ref[j:j + 1, cbw:w2]
                sr = jnp.where(row < 8 - sh, pltpu.roll(xr, 8 - sh, 0), 0.0)
                si = jnp.where(row < 8 - sh, pltpu.roll(xi, 8 - sh, 0), 0.0)
                pr, pi = _cmul(kr, ki, sr, si)
                xr = xr + pr
                xi = xi + pi
            pr, pi = _cmul(tab_ref[8:16, 0:cbw], tab_ref[8:16, cbw:w2], cr, ci)
            xr = xr + pr
            xi = xi + pi
            h_ref[pl.ds(r0, 8), 0:cbw] = xr.astype(BF16)
            h_ref[pl.ds(r0, 8), cbw:w2] = xi.astype(BF16)
            pvr = s_scr[pl.ds(r0, 8), 0:cbw]
            pvi = s_scr[pl.ds(r0, 8), cbw:w2]
            cur_r = s_scr[pl.ds(r0 + 8, 8), 0:cbw]
            cur_i = s_scr[pl.ds(r0 + 8, 8), cbw:w2]
            spr = jnp.where(row == 0, jnp.broadcast_to(pvr[7:8, :], (8, cbw)), pltpu.roll(cur_r, 1, 0))
            spi = jnp.where(row == 0, jnp.broadcast_to(pvi[7:8, :], (8, cbw)), pltpu.roll(cur_i, 1, 0))
            ar = ar + spr * xr + spi * xi
            ai = ai + spr * xi - spi * xr
            return (jnp.broadcast_to(xr[0:1, :], (8, cbw)), jnp.broadcast_to(xi[0:1, :], (8, cbw)), ar, ai)

        z = jnp.zeros((8, cbw), F32)
        cr, ci, ar, ai = lax.fori_loop(0, hb, blk, (carry[:, 0:cbw], carry[:, cbw:w2], z, z))
        carry[:, 0:cbw] = cr
        carry[:, cbw:w2] = ci
        da_ref[:, 0:cbw] += ar
        da_ref[:, cbw:w2] += ai

    rt = lambda ti: nt - 1 - ti
    (h, da), riders = _call_with_plans(
        body, plans, name=name, grid=(SCAN_NCB, nt),
        in_specs=[pl.BlockSpec((tt, SCAN_UW), lambda cb, ti: (rt(ti), cb)),
                  pl.BlockSpec((None, SCAN_UW, w2), lambda cb, ti: (cb, 0, 0)),
                  pl.BlockSpec((tt, w2), lambda cb, ti: (rt(ti), cb)),
                  pl.BlockSpec((8, w2), lambda cb, ti: (jnp.maximum(rt(ti) * hb - 1, 0), cb)),
                  pl.BlockSpec((16, w2), lambda cb, ti: (0, cb))],
        out_specs=[pl.BlockSpec((tt, w2), lambda cb, ti: (rt(ti), cb)),
                   pl.BlockSpec((8, w2), lambda cb, ti: (0, cb))],
        out_shape=[jax.ShapeDtypeStruct((t, 2 * SSM_CH), BF16), jax.ShapeDtypeStruct((8, 2 * SSM_CH), F32)],
        scratch_shapes=[pltpu.VMEM((tt, w2), F32), pltpu.VMEM((tt + 8, w2), F32), pltpu.VMEM((8, w2), F32)],
        semantics=("parallel", "arbitrary"), operands=(dyb, cmat_t, s, s, tabb))
    return h, da, riders


def _s5_out(ymm, proj, dvec, *, name, tm=512):
    t, w = ymm.shape

    def body(y_ref, u_ref, d_ref, yo_ref, sg_ref, ub_ref):
        u = u_ref[...]
        y = y_ref[...] + d_ref[...] * u
        yo_ref[...] = y
        sg_ref[...] = _gelu(y).astype(BF16)
        ub_ref[...] = u.astype(BF16)

    return pl.pallas_call(
        body, name=name, grid=(t // tm,),
        in_specs=[_row_spec(tm, w), _row_spec(tm, w, 3), _vec_spec(w)],
        out_specs=[_row_spec(tm, w)] * 3,
        out_shape=[jax.ShapeDtypeStruct((t, w), F32), jax.ShapeDtypeStruct((t, w), BF16), jax.ShapeDtypeStruct((t, w), BF16)],
        compiler_params=_cparams(("parallel",)),
    )(ymm, proj, dvec)


def _s5_bwd_in(dsg, y, proj, *, name, tm=512):
    t, w = y.shape

    def body(d_ref, y_ref, u_ref, dy_ref, dyb_ref, dd_ref):
        i = pl.program_id(0)
        dy = d_ref[...] * _gelu_grad(y_ref[...])
        dy_ref[...] = dy
        dyb_ref[...] = dy.astype(BF16)
        part = jnp.sum(dy * u_ref[...], axis=0, keepdims=True)

        @pl.when(i == 0)
        def _():
            dd_ref[...] = part

        @pl.when(i > 0)
        def _():
            dd_ref[...] += part

    return pl.pallas_call(
        body, name=name, grid=(t // tm,),
        in_specs=[_row_spec(tm, w), _row_spec(tm, w), _row_spec(tm, w, 3)],
        out_specs=[_row_spec(tm, w), _row_spec(tm, w), _vec_spec(w)],
        out_shape=[jax.ShapeDtypeStruct((t, w), F32), jax.ShapeDtypeStruct((t, w), BF16), jax.ShapeDtypeStruct((1, w), F32)],
        compiler_params=_cparams(("arbitrary",)),
    )(dsg, y, proj)


def _s5_du(dumm, dy, dvec, *, name, tm=512):
    t, w = dy.shape

    def body(a_ref, dy_ref, d_ref, o_ref):
        o_ref[...] = (a_ref[...] + d_ref[...] * dy_ref[...]).astype(BF16)

    return pl.pallas_call(
        body, name=name, grid=(t // tm,), in_specs=[_row_spec(tm, w), _row_spec(tm, w), _vec_spec(w)],
        out_specs=_row_spec(tm, w), out_shape=jax.ShapeDtypeStruct((t, w), BF16),
        compiler_params=_cparams(("parallel",)),
    )(dumm, dy, dvec)


def _s5_discretise(lam_re, lam_im, log_step, b_re, b_im):
    lam = lax.complex(lam_re, lam_im)
    dt = jnp.exp(log_step)[:, None]
    a = jnp.exp(lam * dt)
    bbar = ((a - 1.0) / lam)[..., None] * lax.complex(b_re, b_im)
    return jnp.real(a), jnp.imag(a), jnp.real(bbar), jnp.imag(bbar)


def _perm_cols(re, im):
    lead = re.shape[:-1]
    r = re.reshape(lead + (SCAN_NCB, 1, SCAN_CB))
    i = im.reshape(lead + (SCAN_NCB, 1, SCAN_CB))
    return jnp.concatenate([r, i], axis=-2).reshape(lead + (2 * SSM_CH,))


def _unperm_cols(x):
    lead = x.shape[:-1]
    y = x.reshape(lead + (SCAN_NCB, 2, SCAN_CB))
    return y[..., 0, :].reshape(lead + (SSM_CH,)), y[..., 1, :].reshape(lead + (SSM_CH,))


def _compact(re, im):
    _, r, c = re.shape
    eye = jnp.eye(SCAN_GPB, dtype=re.dtype)

    def half(x):
        x = x.reshape(SCAN_NCB, SCAN_GPB, r, c)
        return (eye[None, :, None, :, None] * x[:, :, :, None, :]).reshape(SCAN_NCB, SCAN_GPB * r, SCAN_GPB * c)

    return jnp.concatenate([half(re), half(im)], axis=-1)


def _compact_extract(x, r):
    c = SSM_STATE
    eye = jnp.eye(SCAN_GPB, dtype=x.dtype)
    y = x.reshape(SCAN_NCB, SCAN_GPB, r, 2, SCAN_GPB, c)
    dg = jnp.sum(y * eye[None, :, None, None, :, None], axis=4).reshape(SSM_GROUPS, r, 2, c)
    return dg[:, :, 0, :], dg[:, :, 1, :]


def _pow_table(ar, ai, descending=False):
    ar = ar.reshape(1, SSM_CH)
    ai = ai.reshape(1, SSM_CH)
    pw = [(ar, ai)]
    for _ in range(7):
        pw.append(_cmul(pw[-1][0], pw[-1][1], ar, ai))
    zero = (jnp.zeros_like(ar), jnp.zeros_like(ar))
    rows = [pw[0], pw[1], pw[3]] + [zero] * 5 + (pw[::-1] if descending else pw)
    re = jnp.concatenate([r for r, _ in rows], axis=0)
    im = jnp.concatenate([i for _, i in rows], axis=0)
    return _perm_cols(re, im)


def _place():
    x, y, c = lax.axis_index("x"), lax.axis_index("y"), lax.axis_index("c")
    chips = [(1 - x, y), (x, 1 - y), (1 - x, 1 - y)]
    return x, y, c, chips


def _dev(px, py, pc):
    return 4 * px + 2 * py + pc


class _Plan:
    def __init__(self, ins, out_shapes, sem_shapes, start, finish, middle=None):
        self.ins, self.out_shapes, self.sem_shapes = list(ins), list(out_shapes), list(sem_shapes)
        self.start, self.finish, self.middle = start, finish, middle


def _split_plan_refs(plans, in_refs, out_refs, sem_refs):
    res, i, o, s = [], 0, 0, 0
    for p in plans:
        ni, no, ns = len(p.ins), len(p.out_shapes), len(p.sem_shapes)
        res.append((in_refs[i:i + ni], out_refs[o:o + no], sem_refs[s:s + ns]))
        i, o, s = i + ni, o + no, s + ns
    return res


def _run_plans(plans, *, name):
    ins = [a for p in plans for a in p.ins]
    outs = [o for p in plans for o in p.out_shapes]
    sems = [s for p in plans for s in p.sem_shapes]
    any_spec = pl.BlockSpec(memory_space=pl.ANY)

    def body(*refs):
        parts = _split_plan_refs(plans, refs[:len(ins)], refs[len(ins):len(ins) + len(outs)], refs[len(ins) + len(outs):])
        for p, r in zip(plans, parts):
            p.start(*r)
        for p, r in zip(plans, parts):
            if p.middle:
                p.middle(*r)
        for p, r in zip(plans, parts):
            p.finish(*r)

    res = pl.pallas_call(body, name=name, in_specs=[any_spec] * len(ins), out_specs=[any_spec] * len(outs),
                         out_shape=outs, scratch_shapes=sems)(*ins)
    return _split_plan_refs(plans, [], res, [])


def _run_plans_on_sequencer(plans, peers_of, *, name, collective_id, after=()):
    ins = [a for p in plans for a in p.ins]
    outs = [o for p in plans for o in p.out_shapes]
    sems = [s for p in plans for s in p.sem_shapes]

    def body(*refs):
        x, y, c, chips = _place()
        peers = peers_of(x, y, c, chips)
        barrier = pltpu.get_barrier_semaphore()
        for peer in peers:
            pl.semaphore_signal(barrier, inc=1, device_id=peer, device_id_type=MESH)
        pl.semaphore_wait(barrier, len(peers))
        n_in = len(ins) + len(after)
        parts = _split_plan_refs(plans, refs[:len(ins)], refs[n_in:n_in + len(outs)], refs[n_in + len(outs):])
        for p, r in zip(plans, parts):
            p.start(*r)
        for p, r in zip(plans, parts):
            if p.middle:
                p.middle(*r)
        for p, r in zip(plans, parts):
            p.finish(*r)

    res = pl.kernel(body, name=name, out_type=outs, mesh=plsc.ScalarSubcoreMesh(axis_name="seq", num_cores=1),
                    scratch_types=sems, compiler_params=pltpu.CompilerParams(collective_id=collective_id))(*ins, *after)
    return _split_plan_refs(plans, [], list(res), [])


def _call_with_plans(body, plans, *, name, grid, in_specs, out_specs, out_shape, scratch_shapes, semantics, operands):
    plans = list(plans)
    if not plans:
        res = pl.pallas_call(body, name=name, grid=grid, in_specs=in_specs, out_specs=out_specs, out_shape=out_shape,
                             scratch_shapes=scratch_shapes, compiler_params=_cparams(semantics))(*operands)
        return list(res), []
    n_in, n_out, n_scr = len(in_specs), len(out_specs), len(scratch_shapes)
    p_ins = [a for p in plans for a in p.ins]
    p_outs = [o for p in plans for o in p.out_shapes]
    p_sems = [s for p in plans for s in p.sem_shapes]
    nsteps = math.prod(grid)
    any_spec = pl.BlockSpec(memory_space=pl.ANY)

    def wrapped(*refs):
        bounds = [n_in, len(p_ins), n_out, len(p_outs), n_scr]
        parts, pos = [], 0
        for b in bounds:
            parts.append(refs[pos:pos + b])
            pos += b
        ins, p_in, outs, p_out, scr = parts
        step = pl.program_id(0)
        for ax in range(1, len(grid)):
            step = step * grid[ax] + pl.program_id(ax)
        riders = _split_plan_refs(plans, p_in, p_out, refs[pos:])

        @pl.when(step == 0)
        def _():
            for p, r in zip(plans, riders):
                p.start(*r)

        mids = [(p, r) for p, r in zip(plans, riders) if p.middle]
        mid_step = nsteps // 2
        split_mid = mids and 0 < mid_step < nsteps - 1
        if split_mid:
            @pl.when(step == mid_step)
            def _():
                for p, r in mids:
                    p.middle(*r)

        body(*ins, *outs, *scr)

        @pl.when(step == nsteps - 1)
        def _():
            if not split_mid:
                for p, r in mids:
                    p.middle(*r)
            for p, r in zip(plans, riders):
                p.finish(*r)

    res = pl.pallas_call(
        wrapped, name=name, grid=grid, in_specs=list(in_specs) + [any_spec] * len(p_ins),
        out_specs=list(out_specs) + [any_spec] * len(p_outs), out_shape=list(out_shape) + p_outs,
        scratch_shapes=list(scratch_shapes) + p_sems, compiler_params=_cparams(("arbitrary",) * len(grid)),
    )(*operands, *p_ins)
    return list(res[:n_out]), [r[1] for r in _split_plan_refs(plans, [], res[n_out:], [])]


def _gather_plan(shards):
    n = len(shards)
    nk = 8

    def make(ins, outs, sems):
        send, recv, lsem = sems
        x, y, c, _ = _place()
        me, sib, xn, yn, dg = (x, y, c), (x, y, 1 - c), (1 - x, y, c), (x, 1 - y, c), (1 - x, 1 - y, c)

        def part(w, block, half):
            ref = outs[w].at[_dev(*block)]
            if half is None:
                return ref
            rows = shards[w].shape[0] // 2
            return ref.at[pl.ds(half * rows, rows)]

        def copy(w, k, block, to, half=None, src=None):
            dst = part(w, block, half)
            return pltpu.make_async_remote_copy(
                src_ref=dst if src is None else src, dst_ref=dst,
                send_sem=send.at[w * nk + k], recv_sem=recv.at[w * nk + k], device_id=to, device_id_type=MESH)

        mine = [pltpu.make_async_copy(ins[w], outs[w].at[_dev(*me)], lsem.at[w]) for w in range(n)]
        return copy, mine, me, sib, xn, yn, dg

    def first_copies(copy, me, sib, xn, yn, ins):
        return [copy(w, k, me, to, src=ins[w]) for w in range(n) for k, to in ((0, sib), (1, xn), (2, yn))]

    def start(ins, outs, sems):
        copy, mine, me, sib, xn, yn, _ = make(ins, outs, sems)
        for cp in mine + first_copies(copy, me, sib, xn, yn, ins):
            cp.start()

    def middle(ins, outs, sems):
        copy, _, me, sib, xn, yn, _ = make(ins, outs, sems)
        for w in range(n):
            copy(w, 1, xn, me).wait_recv()
            copy(w, 3, xn, yn, half=0).start()
            copy(w, 5, xn, sib).start()
        for w in range(n):
            copy(w, 2, yn, me).wait_recv()
            copy(w, 4, yn, xn, half=1).start()
            copy(w, 6, yn, sib).start()

    def finish(ins, outs, sems):
        copy, mine, me, sib, xn, yn, dg = make(ins, outs, sems)
        last = []
        for w in range(n):
            copy(w, 3, dg, me, half=0).wait_recv()
            copy(w, 4, dg, me, half=1).wait_recv()
            fwd = copy(w, 7, dg, sib)
            fwd.start()
            last.append(fwd)
        sx, sy, sd = (1 - me[0], me[1], 1 - me[2]), (me[0], 1 - me[1], 1 - me[2]), (1 - me[0], 1 - me[1], 1 - me[2])
        for w in range(n):
            copy(w, 0, sib, me).wait_recv()
            copy(w, 5, sx, me).wait_recv()
            copy(w, 6, sy, me).wait_recv()
            copy(w, 7, sd, me).wait_recv()
        for cp in first_copies(copy, me, sib, xn, yn, ins) + last:
            cp.wait_send()
        for w in range(n):
            copy(w, 3, xn, yn, half=0).wait_send()
            copy(w, 5, xn, sib).wait_send()
            copy(w, 4, yn, xn, half=1).wait_send()
            copy(w, 6, yn, sib).wait_send()
        for cp in mine:
            cp.wait()

    return _Plan(shards, [jax.ShapeDtypeStruct((N_DEV,) + s.shape, s.dtype) for s in shards],
                 [pltpu.SemaphoreType.DMA((nk * n,)), pltpu.SemaphoreType.DMA((nk * n,)), pltpu.SemaphoreType.DMA((n,))],
                 start, finish, middle)


def _swap_plan(copies_of, n_copies, ins, out_shapes):
    def cps(in_refs, out_refs, sems):
        return copies_of(in_refs, out_refs, sems[0], sems[1])

    def start(in_refs, out_refs, sems):
        for cp in cps(in_refs, out_refs, sems):
            cp.start()

    def finish(in_refs, out_refs, sems):
        all_cps = cps(in_refs, out_refs, sems)
        for cp in all_cps:
            cp.wait_recv()
        for cp in all_cps:
            cp.wait_send()

    return _Plan(ins, out_shapes, [pltpu.SemaphoreType.DMA((n_copies,)), pltpu.SemaphoreType.DMA((n_copies,))],
                 start, finish)


def _sibling_plan(grads):
    n = len(grads)

    def copies(ins, outs, send, recv):
        x, y, c, chips = _place()
        owners = [(x, y)] + chips
        return [pltpu.make_async_remote_copy(
            src_ref=ins[w].at[_dev(*chip, 1 - c)], dst_ref=outs[w].at[k], send_sem=send.at[w * 4 + k],
            recv_sem=recv.at[w * 4 + k], device_id=(x, y, 1 - c), device_id_type=MESH)
            for w in range(n) for k, chip in enumerate(owners)]

    return _swap_plan(copies, 4 * n, grads, [jax.ShapeDtypeStruct((4,) + g.shape[1:], g.dtype) for g in grads])


def _chip_plan(parts, js=(0, 1, 2)):
    n, nj = len(parts), len(js)

    def copies(ins, outs, send, recv):
        x, y, c, chips = _place()
        return [pltpu.make_async_remote_copy(
            src_ref=ins[w].at[j], dst_ref=outs[w * nj + k], send_sem=send.at[w * nj + k],
            recv_sem=recv.at[w * nj + k], device_id=(*chips[j], c), device_id_type=MESH)
            for w in range(n) for k, j in enumerate(js)]

    return _swap_plan(copies, n * nj, parts,
                      [jax.ShapeDtypeStruct(p.shape[1:], p.dtype) for p in parts for _ in js])


UPDATE_TILE_BYTES = 1536 * 1024


def _row_tile(r, c):
    best = 8
    for t in range(8, r + 1, 8):
        if r % t == 0 and t * c * 4 <= UPDATE_TILE_BYTES:
            best = t
    return best


def _chip_partial(g, sib, ids, *, name):
    _, r, c = g.shape
    tr = _row_tile(r, c)

    def body(ids_ref, g_ref, s_ref, o_ref):
        o_ref[...] = (g_ref[...] + s_ref[...]).astype(BF16)

    return pl.pallas_call(
        body, name=name,
        grid_spec=pltpu.PrefetchScalarGridSpec(
            num_scalar_prefetch=1, grid=(3, r // tr),
            in_specs=[pl.BlockSpec((None, tr, c), lambda j, i, ids_ref: (ids_ref[j], i, 0)),
                      pl.BlockSpec((None, tr, c), lambda j, i, ids_ref: (j + 1, i, 0))],
            out_specs=pl.BlockSpec((None, tr, c), lambda j, i, ids_ref: (j, i, 0))),
        out_shape=jax.ShapeDtypeStruct((3, r, c), BF16),
        compiler_params=_cparams(("parallel", "parallel")),
    )(ids, g, sib)


def _adamw_math(w, g, m, v):
    m = ADAM_B1 * m + (1.0 - ADAM_B1) * g
    v = ADAM_B2 * v + (1.0 - ADAM_B2) * (g * g)
    m_hat = m / (1.0 - ADAM_B1 ** ADAM_STEP)
    v_hat = v / (1.0 - ADAM_B2 ** ADAM_STEP)
    delta = -ADAM_LR * (m_hat / (jnp.sqrt(v_hat) + ADAM_EPS) + ADAM_WD * w)
    return delta, m, v


def _shard_update(g, sib, rem, me, w, m, v, *, name):
    r, c = w.shape
    tr = _row_tile(r, c)

    def body(me_ref, g_ref, s_ref, r0_ref, r1_ref, r2_ref, w_ref, m_ref, v_ref, go_ref, d_ref, mo_ref, vo_ref):
        gt = g_ref[...] + s_ref[...]
        gt = gt + r0_ref[...].astype(F32)
        gt = gt + r1_ref[...].astype(F32)
        gt = gt + r2_ref[...].astype(F32)
        go_ref[...] = gt
        d, mn, vn = _adamw_math(w_ref[...], gt, m_ref[...], v_ref[...])
        d_ref[...] = d
        mo_ref[...] = mn
        vo_ref[...] = vn

    blk = lambda k: pl.BlockSpec((None, tr, c), lambda i, me_ref: (k, i, 0))
    plain = pl.BlockSpec((tr, c), lambda i, me_ref: (i, 0))
    return pl.pallas_call(
        body, name=name,
        grid_spec=pltpu.PrefetchScalarGridSpec(
            num_scalar_prefetch=1, grid=(r // tr,),
            in_specs=[pl.BlockSpec((None, tr, c), lambda i, me_ref: (me_ref[0], i, 0)), blk(0), plain, plain, plain,
                      plain, plain, plain],
            out_specs=[plain] * 4),
        out_shape=[jax.ShapeDtypeStruct((r, c), F32)] * 4,
        compiler_params=_cparams(("parallel",)),
    )(me, g, sib, *rem, w, m, v)


def _small_update(gathered, w, m, v, *, name):
    _, r, c = gathered.shape

    def body(g_ref, w_ref, m_ref, v_ref, go_ref, d_ref, mo_ref, vo_ref):
        gt = g_ref[0]
        for k in range(1, N_DEV):
            gt = gt + g_ref[k]
        go_ref[...] = gt
        d, mn, vn = _adamw_math(w_ref[...], gt, m_ref[...], v_ref[...])
        d_ref[...] = d
        mo_ref[...] = mn
        vo_ref[...] = vn

    return pl.pallas_call(
        body, name=name, out_shape=[jax.ShapeDtypeStruct((r, c), F32)] * 4,
        compiler_params=pltpu.CompilerParams(vmem_limit_bytes=VMEM_LIMIT),
    )(gathered, w, m, v)


SMALL_UNIT = 1024


def _pack(parts):
    flat = []
    for p in parts:
        f = p.reshape(-1).astype(F32)
        pad = (-f.shape[0]) % SMALL_UNIT
        flat.append(jnp.pad(f, (0, pad)))
    return jnp.concatenate(flat).reshape(-1, 128)


def _unpack(buf, shapes):
    flat = buf.reshape(-1)
    out, off = [], 0
    for s in shapes:
        nel = math.prod(s)
        out.append(flat[off:off + nel].reshape(s))
        off += nel + ((-nel) % SMALL_UNIT)
    return out


def kernel(x, p, ffn1_w_in, ffn1_w_out, ln1_g, ln1_b, mix_w_in, conv_w, conv_b, conv_w_out, ssm_lam_re, ssm_lam_im, ssm_log_step, ssm_b_re, ssm_b_im, ssm_c_re, ssm_c_im, ssm_d, ssm_w_glu, mix_w_out, ln2_g, ln2_b, ffn2_w_in, ffn2_w_out, ln3_g, ln3_b, ple_w_in, ple_w_gate, ln4_g, ln4_b, loss_target, m_ffn1_w_in, m_ffn1_w_out, m_ln1_g, m_ln1_b, m_mix_w_in, m_conv_w, m_conv_b, m_conv_w_out, m_ssm_lam_re, m_ssm_lam_im, m_ssm_log_step, m_ssm_b_re, m_ssm_b_im, m_ssm_c_re, m_ssm_c_im, m_ssm_d, m_ssm_w_glu, m_mix_w_out, m_ln2_g, m_ln2_b, m_ffn2_w_in, m_ffn2_w_out, m_ln3_g, m_ln3_b, m_ple_w_in, m_ple_w_gate, m_ln4_g, m_ln4_b, v_ffn1_w_in, v_ffn1_w_out, v_ln1_g, v_ln1_b, v_mix_w_in, v_conv_w, v_conv_b, v_conv_w_out, v_ssm_lam_re, v_ssm_lam_im, v_ssm_log_step, v_ssm_b_re, v_ssm_b_im, v_ssm_c_re, v_ssm_c_im, v_ssm_d, v_ssm_w_glu, v_mix_w_out, v_ln2_g, v_ln2_b, v_ffn2_w_in, v_ffn2_w_out, v_ln3_g, v_ln3_b, v_ple_w_in, v_ple_w_gate, v_ln4_g, v_ln4_b):
    args = dict(locals())
    big = ['ffn1_w_in', 'ffn1_w_out', 'mix_w_in', 'conv_w_out', 'ssm_w_glu', 'mix_w_out',
           'ffn2_w_in', 'ffn2_w_out', 'ple_w_in', 'ple_w_gate']
    small = ['ln1_g', 'ln1_b', 'conv_b', 'ssm_lam_re', 'ssm_lam_im', 'ssm_log_step', 'ssm_b_re', 'ssm_b_im',
             'ssm_c_re', 'ssm_c_im', 'ssm_d', 'ln2_g', 'ln2_b', 'ln3_g', 'ln3_b', 'ln4_g', 'ln4_b']
    order = ['ffn1_w_in', 'ffn1_w_out', 'ln1_g', 'ln1_b', 'mix_w_in', 'conv_w', 'conv_b', 'conv_w_out',
             'ssm_lam_re', 'ssm_lam_im', 'ssm_log_step', 'ssm_b_re', 'ssm_b_im', 'ssm_c_re', 'ssm_c_im', 'ssm_d',
             'ssm_w_glu', 'mix_w_out', 'ln2_g', 'ln2_b', 'ffn2_w_in', 'ffn2_w_out', 'ln3_g', 'ln3_b',
             'ple_w_in', 'ple_w_gate', 'ln4_g', 'ln4_b']

    t = x.shape[1]
    d = x.shape[2]
    xc_, yc_, cc_ = lax.axis_index("x"), lax.axis_index("y"), lax.axis_index("c")
    me = (4 * xc_ + 2 * yc_ + cc_).astype(jnp.int32)
    cw_cols = conv_w.shape[2]

    turned = ('ffn1_w_in', 'ffn2_w_in')

    def local(a, nm):
        return jnp.swapaxes(a[0], 0, 1) if nm in turned else a[0]

    shard = {nm: local(args[nm], nm).astype(BF16) for nm in big}
    cw_pad = jnp.zeros((16, 128), F32).at[0:3, 0:cw_cols].set(conv_w[0])
    wf = shard['ffn1_w_in'].shape[0]

    def tie(chain, *others):
        out = lax.optimization_barrier((chain, *others))
        return out[0], out[1:]

    def gather(arrays, tag, after=()):
        ((_, got, _),) = _run_plans_on_sequencer(
            [_gather_plan(arrays)], lambda x, y, c, chips: [(x, y, 1 - c), (1 - x, y, c), (x, 1 - y, c)],
            name=f"gather_{tag}", collective_id=2, after=after)
        return got

    w1in, cw_g = gather([shard['ffn1_w_in'], cw_pad], "ffn1_in")
    (w1out_g,) = gather([shard['ffn1_w_out']], "ffn1_out")
    (wmix,) = gather([shard['mix_w_in']], "mix_in")
    wco, wglu, wmo_g = gather([shard[nm] for nm in ('conv_w_out', 'ssm_w_glu', 'mix_w_out')], "mix_rest")
    (w2in,) = gather([shard['ffn2_w_in']], "ffn2_in")
    (w2out_g,) = gather([shard['ffn2_w_out']], "ffn2_out")
    wpin, wgate_g = gather([shard['ple_w_in'], shard['ple_w_gate']], "ple")
    cw_full = jnp.transpose(cw_g[:, 0:3, 0:cw_cols], (1, 0, 2)).reshape(3, N_DEV * cw_cols)
    cw8 = jnp.zeros((8, CONV_CH), F32).at[0:3, :].set(cw_full)

    s5_in = (ssm_lam_re[0], ssm_lam_im[0], ssm_log_step[0], ssm_b_re[0], ssm_b_im[0])
    (a_re, a_im, bb_re, bb_im), s5_vjp = jax.vjp(_s5_discretise, *s5_in)
    tab_f = _pow_table(a_re, a_im)
    tab_b = _pow_table(a_re, -a_im, descending=True)
    bmat_b = _compact(jnp.transpose(bb_re, (0, 2, 1)), jnp.transpose(bb_im, (0, 2, 1))).astype(BF16)
    cmat_tb = _compact(ssm_c_re[0], -ssm_c_im[0]).astype(BF16)
    dvec = ssm_d[0].reshape(1, SSM_W)

    xf = x[0]
    x_b = xf.astype(BF16)
    p_b = p[0, 0].astype(BF16)
    tgt = loss_target[0]
    tq = min(512, t)

    ffn_out = dict(ja='c', jb='c', nj=4, tm=tq, tn=d, tk=wf)
    def side_by_side(wb):
        return jnp.transpose(wb, (1, 0, 2)).reshape(wb.shape[1], N_DEV * wb.shape[2])

    tf = min(1024, t)
    a1, h1, _ = _ffn_in(x_b, w1in, name="ffn1_in", tm=tf)
    w1out = w1out_g.reshape(4, wf, d)
    r1, x1, x1b = _mm(a1, w1out, name="ffn1_out", **ffn_out, epilogue=_ln_epilogue(xf, ln1_g, ln1_b, 0.5))
    proj = _mm(x1b, wmix, name="mix_in", jb='b', jo='b', o_flat=True, nj=8, tm=tq, tn=512, tk=d)
    _, (wco, wglu, wmo_g) = tie(x1b, wco, wglu, wmo_g)
    wmo = wmo_g.reshape(d, d)
    ycin = _conv_fwd(proj, cw8, conv_b, name="conv_fwd")
    wco, wglu = side_by_side(wco), side_by_side(wglu)
    yconv = _mm(ycin, wco, name="conv_out", tm=tq, tn=d, tk=CONV_CH)
    s_f, s_b, _ = _scan_fwd(proj, bmat_b, tab_f, name="scan_fwd")
    blk = dict(ja='b', jb='b', jo='b', nj=SCAN_NCB)
    ymm = _mm(s_b, cmat_tb, name="ssm_read", a_flat=True, o_flat=True, tb=True, tm=tq, tn=SCAN_UW, tk=2 * SCAN_CB, **blk)
    ys, sg, u_b = _s5_out(ymm, proj, dvec, name="ssm_out")
    glu = _mm(sg, wglu, name="glu_in", tm=tq, tn=d, tk=SSM_W)
    merged = _gate_fwd(yconv, glu, proj, name="gate_fwd")
    r2, x2, x2b = _mm(merged, wmo, name="mix_out", tm=tq, tn=d, tk=d, epilogue=_ln_epilogue(x1, ln2_g, ln2_b, 1.0))
    a2, h2, _ = _ffn_in(x2b, w2in, name="ffn2_in", tm=tf)
    w2out = w2out_g.reshape(4, wf, d)
    r3, x3, x3b = _mm(a2, w2out, name="ffn2_out", **ffn_out, epilogue=_ln_epilogue(x2, ln3_g, ln3_b, 0.5))
    wgate = wgate_g.reshape(d, d)
    _, (wpin,) = tie(x2b, wpin)
    pe = _mm(p_b, side_by_side(wpin), name="ple_in", tm=tq, tn=d, tk=p_b.shape[1])
    gp = _mm(x3b, wgate, name="ple_gate", tm=tq, tn=d, tk=d)

    dr4, dpe_b, dgp_b, dg4, db4, loss_part = _ln_bwd(x3, [pe, gp], ln4_g, ln4_b, [], name="ple_ln_bwd",
                                                     ple=True, target=tgt)
    gb, sib, rem = {}, {}, {}
    ids = jnp.stack([_dev(1 - xc_, yc_, cc_), _dev(xc_, 1 - yc_, cc_), _dev(1 - xc_, 1 - yc_, cc_)]).astype(jnp.int32)

    def blocked(nm, g):
        return g.reshape((N_DEV,) + args[nm].shape[1:])

    def to_sibling(*names):
        return _sibling_plan([gb[nm] for nm in names])

    def chip_sums(names, sibs):
        for nm, s in zip(names, sibs):
            sib[nm] = s
        return [_chip_partial(gb[nm], sib[nm], ids, name=f"chip_sum_{nm}") for nm in names]

    def on_sequencer(plan, peers, tag, cid, after=()):
        ((_, got, _),) = _run_plans_on_sequencer([plan], peers, name=tag, collective_id=cid, after=after)
        return got

    waiting = []

    def send_sibling(*names, after=()):
        got = on_sequencer(to_sibling(*names), lambda x, y, c, chips: [(x, y, 1 - c)], f"grad_sibling_{names[0]}", 3,
                           after=after)
        waiting.append((names, got))

    def send_chips(chain, count=None):
        for _ in range(len(waiting) if count is None else count):
            names, got = waiting.pop(0)
            chain, parts = tie(chain, *chip_sums(names, got))
            r_ = on_sequencer(_chip_plan(list(parts)), lambda x, y, c, chips: [(*chip, c) for chip in chips],
                              f"grad_chips_{names[0]}", 1)
            for i, nm in enumerate(names):
                rem[nm] = r_[3 * i:3 * i + 3]
        return chain

    ffn_in_dg = dict(ja='c', jb='c', nj=8, tm=tq, tn=d, tk=wf)
    ffn_in_wg = dict(ja='b', jo='b', ta=True, nj=8, tm=wf, tn=d, tk=t)
    ffn_out_wg = dict(ja='b', jo='b', ta=True, nj=4, tm=wf, tn=d, tk=t)

    gb['ple_w_in'] = _mm(p_b, dpe_b, name="ple_in_wg", jb='b', jo='b', b_flat=True, ta=True, nj=8,
                         tm=p_b.shape[1], tn=128, tk=t)
    gb['ple_w_gate'] = blocked('ple_w_gate', _mm(x3b, dgp_b, name="ple_gate_wg", ta=True, tm=d, tn=d, tk=t))
    g_ple = ('ple_w_in', 'ple_w_gate')
    dx3_g, (s_,) = _mm(dgp_b, wgate, name="ple_gate_dg", tb=True, tm=tq, tn=d, tk=d, plans=[to_sibling(*g_ple)])
    waiting.append((g_ple, s_))

    dr3, df2_b, dg3, db3 = _ln_bwd(r3, [], ln3_g, ln3_b, [(dr4, ALPHA), (dx3_g, 1.0)], name="ffn2_ln_bwd", fs=0.5)
    df2_b = send_chips(df2_b)
    gb['ffn2_w_out'] = blocked('ffn2_w_out', _mm(a2, df2_b, name="ffn2_out_wg", **ffn_out_wg))
    df2_b, (gb['ffn2_w_out'],) = tie(df2_b, gb['ffn2_w_out'])
    send_sibling('ffn2_w_out')
    dh2, _ = _ffn_out_dg(df2_b, w2out, h2, name="ffn2_out_dg", tm=tf)
    dx2_f = _mm(dh2, w2in, name="ffn2_in_dg", **ffn_in_dg)
    dx2_f = send_chips(dx2_f)
    gb['ffn2_w_in'] = _mm(dh2, x2b, name="ffn2_in_wg", **ffn_in_wg)
    dx2_f, (gb['ffn2_w_in'],) = tie(dx2_f, gb['ffn2_w_in'])

    dr2, dmix_b, dg2, db2 = _ln_bwd(r2, [], ln2_g, ln2_b, [(dr3, ALPHA), (dx2_f, 1.0)], name="mix_ln_bwd")
    dmerged = _mm(dmix_b, wmo, name="mix_out_dg", tb=True, tm=tq, tn=d, tk=d)
    send_sibling('ffn2_w_in', after=(dmerged,))
    gb['mix_w_out'] = blocked('mix_w_out', _mm(merged, dmix_b, name="mix_out_wg", ta=True, tm=d, tn=d, tk=t))
    dyconv_b, dglu_b, dgate_b = _gate_bwd(dmerged, yconv, glu, proj, name="gate_bwd")
    gb['conv_w_out'] = _mm(ycin, dyconv_b, name="conv_out_wg", jb='b', jo='b', b_flat=True, ta=True, nj=8,
                           tm=CONV_CH, tn=128, tk=t)
    dycin = _mm(dyconv_b, wco, name="conv_out_dg", tb=True, tm=tq, tn=CONV_CH, tk=d)
    gb['ssm_w_glu'] = _mm(sg, dglu_b, name="glu_in_wg", jb='b', jo='b', b_flat=True, ta=True, nj=8,
                          tm=SSM_W, tn=256, tk=t)
    dsg = _mm(dglu_b, wglu, name="glu_in_dg", tb=True, tm=tq, tn=SSM_W, tk=2 * d)
    dsg = send_chips(dsg)
    dys, dys_b, dd = _s5_bwd_in(dsg, ys, proj, name="ssm_out_bwd")
    h_b, da_acc, _ = _scan_bwd(dys_b, cmat_tb, s_f, tab_b, name="scan_bwd")
    send_sibling('mix_w_out', 'conv_w_out', 'ssm_w_glu', after=(h_b,))
    dumm = _mm(h_b, bmat_b, name="ssm_write_dg", a_flat=True, o_flat=True, tb=True, tm=tq, tn=SCAN_UW,
               tk=2 * SCAN_CB, **blk)
    du_b = _s5_du(dumm, dys, dvec, name="ssm_du")
    g_bmat = _mm(u_b, h_b, name="ssm_write_wg", a_flat=True, b_flat=True, ta=True, tm=SCAN_UW,
                 tn=2 * SCAN_CB, tk=t, **blk)
    g_cmat = _mm(dys_b, s_b, name="ssm_read_wg", a_flat=True, b_flat=True, ta=True, tm=SCAN_UW,
                 tn=2 * SCAN_CB, tk=t, **blk)
    dcb_b, dcc_b, dch_b, dconv = _conv_bwd(dycin, proj, cw8, conv_b, name="conv_bwd")
    dproj = jnp.concatenate([dcb_b, dcc_b, dch_b, du_b, dgate_b], axis=1)
    dproj = send_chips(dproj)
    gb['mix_w_in'] = _mm(x1b, dproj, name="mix_in_wg", jb='b', jo='b', b_flat=True, ta=True, nj=8,
                         tm=d, tn=512, tk=t)
    send_sibling('mix_w_in')
    dx1_m = _mm(dproj, wmix, name="mix_in_dg", ja='c', jb='c', a_flat=True, tb=True, nj=8, tm=tq, tn=d, tk=512)
    dx1_m, (gb['mix_w_in'],) = tie(dx1_m, gb['mix_w_in'])

    dr1, df1_b, dg1, db1 = _ln_bwd(r1, [], ln1_g, ln1_b, [(dr2, ALPHA), (dx1_m, 1.0)], name="ffn1_ln_bwd", fs=0.5)
    gb['ffn1_w_out'] = blocked('ffn1_w_out', _mm(a1, df1_b, name="ffn1_out_wg", **ffn_out_wg))
    df1_b, (gb['ffn1_w_out'],) = tie(df1_b, gb['ffn1_w_out'])
    send_sibling('ffn1_w_out')
    df1_b = send_chips(df1_b, 1)
    dh1, _ = _ffn_out_dg(df1_b, w1out, h1, name="ffn1_out_dg", tm=tf)
    gb['ffn1_w_in'] = _mm(dh1, x_b, name="ffn1_in_wg", **ffn_in_wg)
    dh1, (gb['ffn1_w_in'],) = tie(dh1, gb['ffn1_w_in'])
    send_sibling('ffn1_w_in')
    dh1 = send_chips(dh1, 1)
    (grad_x,) = _mm(dh1, w1in, name="ffn1_in_dg", **ffn_in_dg,
                    epilogue=(lambda pr, drv: (pr + ALPHA * drv,), (dr1,), (), (F32,)))

    da_sum = jnp.sum(da_acc, axis=0)
    da_re, da_im = _unperm_cols(da_sum)
    gbb_re, gbb_im = [jnp.transpose(v, (0, 2, 1)) for v in _compact_extract(g_bmat, SSM_GROUP)]
    g_c_re, g_c_im_neg = _compact_extract(g_cmat, SSM_GROUP)
    g_c_im = -g_c_im_neg
    g_lam_re, g_lam_im, g_log_step, g_b_re, g_b_im = s5_vjp(
        (da_re.reshape(SSM_GROUPS, SSM_STATE), da_im.reshape(SSM_GROUPS, SSM_STATE), gbb_re, gbb_im))
    g_d = dd.reshape(SSM_GROUPS, SSM_GROUP)

    small_g = {'ln1_g': dg1, 'ln1_b': db1, 'conv_b': dconv[3:4], 'ssm_lam_re': g_lam_re, 'ssm_lam_im': g_lam_im,
               'ssm_log_step': g_log_step, 'ssm_b_re': g_b_re, 'ssm_b_im': g_b_im, 'ssm_c_re': g_c_re,
               'ssm_c_im': g_c_im, 'ssm_d': g_d, 'ln2_g': dg2, 'ln2_b': db2, 'ln3_g': dg3, 'ln3_b': db3,
               'ln4_g': dg4, 'ln4_b': db4}
    small_shapes = [args[nm].shape for nm in small] + [(3, CONV_CH), (1,)]
    g_pack = _pack([small_g[nm] for nm in small] + [dconv[0:3], loss_part[0:1, 0:1]])

    res = {}
    me1, _ = tie(me.reshape(1), grad_x)
    me1 = send_chips(me1, 1)

    def update(nm):
        upd = _shard_update(gb[nm], sib[nm], rem[nm], me1, local(args[nm], nm), local(args['m_' + nm], nm),
                            local(args['v_' + nm], nm), name=f"update_{nm}")
        for key, val in zip(('grad_', 'delta_', 'new_m_', 'new_v_'), upd):
            res[key + nm] = (jnp.swapaxes(val, 0, 1) if nm in turned else val)[None]

    (g_all,) = gather([g_pack], "small", after=(dh1,))
    last = ['ffn1_w_out', 'ffn1_w_in']
    for nm in big:
        if nm not in last:
            update(nm)

    def full_cw(a):
        return lax.dynamic_update_slice(jnp.zeros((3, CONV_CH), F32), a[0], (0, me * cw_cols))

    zero1 = jnp.zeros((1,), F32)
    w_pack = _pack([args[nm] for nm in small] + [full_cw(conv_w), zero1])
    m_pack = _pack([args['m_' + nm] for nm in small] + [full_cw(m_conv_w), zero1])
    v_pack = _pack([args['v_' + nm] for nm in small] + [full_cw(v_conv_w), zero1])
    sg_sum, sd, sm, sv = _small_update(g_all, w_pack, m_pack, v_pack, name="small_update")
    for key, buf in (('grad_', sg_sum), ('delta_', sd), ('new_m_', sm), ('new_v_', sv)):
        parts = _unpack(buf, small_shapes)
        for nm, val in zip(small, parts[:len(small)]):
            res[key + nm] = val
        res[key + 'conv_w'] = lax.dynamic_slice(parts[len(small)], (0, me * cw_cols), (3, cw_cols))[None]
        if key == 'grad_':
            loss = parts[-1][0]

    for nm in last:
        update(nm)

    outs = [loss, grad_x[None]]
    for key in ('grad_', 'delta_', 'new_m_', 'new_v_'):
        outs += [res[key + nm] for nm in order]
    return tuple(outs)
```

```python
import functools
import math

import jax
import jax.numpy as jnp
from jax import lax
from jax.experimental import pallas as pl
from jax.experimental.pallas import tpu as pltpu
from jax.experimental.pallas import tpu_sc as plsc

F32 = jnp.float32
BF16 = jnp.bfloat16
MESH = pl.DeviceIdType.MESH

N_DEV = 8
ALPHA = 2.0 ** 0.25
LN_EPS = 1e-5
CONV_CH = 512
SSM_W = 512
SSM_GROUPS = 32
SSM_GROUP = 16
SSM_STATE = 64
SSM_CH = SSM_GROUPS * SSM_STATE
SCAN_CB = 512
SCAN_NCB = SSM_CH // SCAN_CB
SCAN_GPB = SSM_GROUPS // SCAN_NCB
SCAN_UW = SCAN_GPB * SSM_GROUP
SCAN_TT = 256
ADAM_LR = 0.001
ADAM_B1 = 0.9
ADAM_B2 = 0.999
ADAM_EPS = 1e-08
ADAM_WD = 0.01
ADAM_STEP = 10
VMEM_LIMIT = 56 * 1024 * 1024


def _cparams(sem=None, **kw):
    return pltpu.CompilerParams(dimension_semantics=sem, vmem_limit_bytes=VMEM_LIMIT, **kw)


def _mm(a, b, *, name, ja=None, jb=None, jo=None, a_flat=False, b_flat=False, o_flat=False,
        ta=False, tb=False, tm, tn, tk, nj=1, out_dtype=F32, plans=(), epilogue=None):
    def dims(arr, j, flat):
        if j is None:
            return arr.shape
        if flat:
            return (arr.shape[0], arr.shape[1] // nj)
        assert arr.shape[0] == nj, (name, arr.shape, nj)
        return arr.shape[1:]

    ar, ac = dims(a, ja, a_flat)
    br, bc = dims(b, jb, b_flat)
    m, k = (ac, ar) if ta else (ar, ac)
    k2, n = (bc, br) if tb else (br, bc)
    assert k == k2, (name, a.shape, b.shape)
    assert m % tm == 0 and n % tn == 0 and k % tk == 0, (name, m, n, k, tm, tn, tk)
    njb = nj if 'b' in (ja, jb) else 1
    njc = nj if 'c' in (ja, jb) else 1
    nk = k // tk
    j_inside = njc > 1 and nk == 1 and not ta
    n_in = njc if j_inside else 1
    nred = nk if j_inside else njc * nk
    grid = (njb, m // tm, n // tn, 1 if j_inside else njc, nk)

    def make_spec(j, flat, blk, rfn, cfn, cols_per_j):
        def jsel(g, c):
            return g if j == 'b' else c
        if j is None:
            return pl.BlockSpec(blk, lambda g, i, jn, c, kk: (rfn(i, jn, kk), cfn(i, jn, kk)))
        if j == 'c' and j_inside:
            if flat:
                return pl.BlockSpec((blk[0], nj * blk[1]), lambda g, i, jn, c, kk: (rfn(i, jn, kk), 0))
            return pl.BlockSpec((nj,) + blk, lambda g, i, jn, c, kk: (0, rfn(i, jn, kk), cfn(i, jn, kk)))
        if flat:
            nb = cols_per_j // blk[1]
            return pl.BlockSpec(blk, lambda g, i, jn, c, kk: (rfn(i, jn, kk), jsel(g, c) * nb + cfn(i, jn, kk)))
        return pl.BlockSpec((None,) + blk,
                            lambda g, i, jn, c, kk: (jsel(g, c), rfn(i, jn, kk), cfn(i, jn, kk)))

    if ta:
        a_spec = make_spec(ja, a_flat, (tk, tm), lambda i, jn, kk: kk, lambda i, jn, kk: i, ac)
    else:
        a_spec = make_spec(ja, a_flat, (tm, tk), lambda i, jn, kk: i, lambda i, jn, kk: kk, ac)
    if tb:
        b_spec = make_spec(jb, b_flat, (tn, tk), lambda i, jn, kk: jn, lambda i, jn, kk: kk, bc)
    else:
        b_spec = make_spec(jb, b_flat, (tk, tn), lambda i, jn, kk: kk, lambda i, jn, kk: jn, bc)
    o_spec = make_spec(jo, o_flat, (tm, tn), lambda i, jn, kk: i, lambda i, jn, kk: jn, n)
    if jo is None:
        out_shape = (m, n)
    elif o_flat:
        out_shape = (m, nj * n)
    else:
        out_shape = (nj, m, n)

    dn = (((0 if ta else 1,), (1 if tb else 0,)), ((), ()))

    def operand(ref, j, flat, jj, width):
        if not (j == 'c' and j_inside):
            return ref[...]
        return ref[:, jj * width:(jj + 1) * width] if flat else ref[jj]

    e_fn, e_rows, e_vecs, e_dtypes = epilogue if epilogue else (None, (), (), (out_dtype,))
    assert not epilogue or (nred == 1 and jo is None), name
    n_e = len(e_rows) + len(e_vecs)
    n_o = len(e_dtypes)

    def body(a_ref, b_ref, *rest):
        e_refs, o_refs, scratch = rest[:n_e], rest[n_e:n_e + n_o], rest[n_e + n_o:]
        o_ref = o_refs[0]
        p = None
        for jj in range(n_in):
            q = lax.dot_general(operand(a_ref, ja, a_flat, jj, tk), operand(b_ref, jb, b_flat, jj, tk if tb else tn),
                                dn, preferred_element_type=F32)
            p = q if p is None else p + q
        if epilogue:
            for ref, val in zip(o_refs, e_fn(p, *[r[...] for r in e_refs])):
                ref[...] = val.astype(ref.dtype)
        elif nred == 1:
            o_ref[...] = p.astype(o_ref.dtype)
        else:
            acc = scratch[0]
            r = pl.program_id(3) * nk + pl.program_id(4)

            @pl.when(r == 0)
            def _():
                acc[...] = p

            @pl.when(r > 0)
            def _():
                acc[...] += p

            @pl.when(r == nred - 1)
            def _():
                o_ref[...] = acc[...].astype(o_ref.dtype)

    vec_spec = pl.BlockSpec((1, tn), lambda g, i, jn, c, kk: (0, jn))
    res = _call_with_plans(
        body, plans, name=name, grid=grid,
        in_specs=[a_spec, b_spec] + [o_spec] * len(e_rows) + [vec_spec] * len(e_vecs), out_specs=[o_spec] * n_o,
        out_shape=[jax.ShapeDtypeStruct(out_shape, dt) for dt in e_dtypes],
        scratch_shapes=[] if nred == 1 else [pltpu.VMEM((tm, tn), F32)],
        semantics=("parallel", "parallel", "parallel", "arbitrary", "arbitrary"), operands=(a, b, *e_rows, *e_vecs))
    outs = res[0] if epilogue else res[0][0]
    return (outs, res[1]) if plans else outs


def _sigmoid(v):
    return jax.nn.sigmoid(v)


def _row_spec(tm, cols, colblk=0):
    return pl.BlockSpec((tm, cols), lambda i: (i, colblk))


def _vec_spec(cols):
    return pl.BlockSpec((1, cols), lambda i: (0, 0))


def _ffn_in(xb, win, *, name, tm, plans=()):
    t, d = xb.shape
    nj, w, _ = win.shape
    half = nj // 2
    dn = (((1,), (1,)), ((), ()))

    def body(x_ref, wg_ref, wu_ref, a_ref, gu_ref):
        xv = x_ref[...]
        g = lax.dot_general(xv, wg_ref[...], dn, preferred_element_type=F32)
        u = lax.dot_general(xv, wu_ref[...], dn, preferred_element_type=F32)
        a_ref[...] = (g * _sigmoid(g) * u).astype(BF16)
        gu_ref[0] = g.astype(BF16)
        gu_ref[1] = u.astype(BF16)

    (a, gu), riders = _call_with_plans(
        body, plans, name=name, grid=(half, t // tm),
        in_specs=[pl.BlockSpec((tm, d), lambda j, i: (i, 0)),
                  pl.BlockSpec((None, w, d), lambda j, i: (j, 0, 0)),
                  pl.BlockSpec((None, w, d), lambda j, i: (j + half, 0, 0))],
        out_specs=[pl.BlockSpec((None, tm, w), lambda j, i: (j, i, 0)),
                   pl.BlockSpec((2, None, tm, w), lambda j, i: (0, j, i, 0))],
        out_shape=[jax.ShapeDtypeStruct((half, t, w), BF16), jax.ShapeDtypeStruct((2, half, t, w), BF16)],
        scratch_shapes=[], semantics=("parallel", "parallel"), operands=(xb, win, win))
    return a, gu, riders


def _ffn_out_dg(dfb, wout, gu, *, name, tm, plans=()):
    t, d = dfb.shape
    half, w, _ = wout.shape
    dn = (((1,), (1,)), ((), ()))

    def body(df_ref, w_ref, gu_ref, dh_ref):
        da = lax.dot_general(df_ref[...], w_ref[...], dn, preferred_element_type=F32)
        g = gu_ref[0].astype(F32)
        u = gu_ref[1].astype(F32)
        sg = _sigmoid(g)
        dh_ref[0] = (da * u * (sg * (1.0 + g * (1.0 - sg)))).astype(BF16)
        dh_ref[1] = (da * (g * sg)).astype(BF16)

    (out,), riders = _call_with_plans(
        body, plans, name=name, grid=(half, t // tm),
        in_specs=[pl.BlockSpec((tm, d), lambda j, i: (i, 0)),
                  pl.BlockSpec((None, w, d), lambda j, i: (j, 0, 0)),
                  pl.BlockSpec((2, None, tm, w), lambda j, i: (0, j, i, 0))],
        out_specs=[pl.BlockSpec((2, None, tm, w), lambda j, i: (0, j, i, 0))],
        out_shape=[jax.ShapeDtypeStruct((2, half, t, w), BF16)],
        scratch_shapes=[], semantics=("parallel", "parallel"), operands=(dfb, wout, gu))
    return out.reshape(2 * half, t, w), riders


def _ln_stats(r):
    mu = jnp.mean(r, axis=-1, keepdims=True)
    xc = r - mu
    var = jnp.mean(xc * xc, axis=-1, keepdims=True)
    rstd = lax.rsqrt(var + LN_EPS)
    return xc * rstd, rstd


def _ln_epilogue(xin, gamma, beta, fs):
    def fn(p, xv, g, b):
        r = ALPHA * xv + fs * p
        xh, _ = _ln_stats(r)
        y = xh * g + b
        return r, y, y

    return fn, (xin,), (gamma, beta), (F32, F32, BF16)


def _ln_bwd(xin, fparts, gamma, beta, grads, *, name, fs=1.0, ple=False, target=None, tm=512):
    t, d = xin.shape
    nf = len(fparts)
    ng = len(grads)
    coefs = [c for _, c in grads]
    use_t = target is not None
    n_fout = 2 if ple else 1

    def body(*refs):
        pos = 0
        x_ref = refs[pos]; pos += 1
        f_refs = refs[pos:pos + nf]; pos += nf
        g_ref, b_ref = refs[pos:pos + 2]; pos += 2
        gr_refs = refs[pos:pos + ng]; pos += ng
        if use_t:
            t_ref = refs[pos]; pos += 1
        dr_ref = refs[pos]; pos += 1
        fo_refs = refs[pos:pos + n_fout]; pos += n_fout
        dg_ref, db_ref = refs[pos:pos + 2]; pos += 2
        if use_t:
            loss_ref = refs[pos]; pos += 1
        i = pl.program_id(0)

        if ple:
            pe = f_refs[0][...]
            sg = _sigmoid(f_refs[1][...])
            resid = ALPHA * x_ref[...] + pe * sg
        else:
            resid = x_ref[...]
        xh, rstd = _ln_stats(resid)
        gam = g_ref[...]
        if use_t:
            diff = xh * gam + b_ref[...] - t_ref[...]
            dy = diff * (1.0 / d)
            lpart = 0.5 * jnp.sum(jnp.sum(diff * diff, axis=-1, keepdims=True), axis=0, keepdims=True) * (1.0 / d)
        else:
            dy = coefs[0] * gr_refs[0][...]
            for c, r in zip(coefs[1:], gr_refs[1:]):
                dy = dy + c * r[...]
        dxh = dy * gam
        m1 = jnp.mean(dxh, axis=-1, keepdims=True)
        m2 = jnp.mean(dxh * xh, axis=-1, keepdims=True)
        dr = rstd * (dxh - m1 - xh * m2)
        dr_ref[...] = dr
        if ple:
            fo_refs[0][...] = (dr * sg).astype(BF16)
            fo_refs[1][...] = (dr * pe * (sg * (1.0 - sg))).astype(BF16)
        else:
            fo_refs[0][...] = (fs * dr).astype(BF16)
        dgp = jnp.sum(dy * xh, axis=0, keepdims=True)
        dbp = jnp.sum(dy, axis=0, keepdims=True)

        @pl.when(i == 0)
        def _():
            dg_ref[...] = dgp
            db_ref[...] = dbp
            if use_t:
                loss_ref[...] = jnp.broadcast_to(lpart, loss_ref.shape)

        @pl.when(i > 0)
        def _():
            dg_ref[...] += dgp
            db_ref[...] += dbp
            if use_t:
                loss_ref[...] += jnp.broadcast_to(lpart, loss_ref.shape)

    ins = [xin, *fparts, gamma, beta, *[g for g, _ in grads]] + ([target] if use_t else [])
    in_specs = ([_row_spec(tm, d)] * (1 + nf) + [_vec_spec(d), _vec_spec(d)] + [_row_spec(tm, d)] * ng
                + ([_row_spec(tm, d)] if use_t else []))
    out_specs = [_row_spec(tm, d)] * (1 + n_fout) + [_vec_spec(d), _vec_spec(d)] + ([_vec_spec(128)] if use_t else [])
    out_shape = ([jax.ShapeDtypeStruct((t, d), F32)] + [jax.ShapeDtypeStruct((t, d), BF16)] * n_fout
                 + [jax.ShapeDtypeStruct((1, d), F32)] * 2 + ([jax.ShapeDtypeStruct((1, 128), F32)] if use_t else []))
    return pl.pallas_call(
        body, name=name, grid=(t // tm,), in_specs=in_specs, out_specs=out_specs, out_shape=out_shape,
        compiler_params=_cparams(("arbitrary",)),
    )(*ins)


def _conv_fwd(proj, cw, cb, *, name, tm=512):
    t = proj.shape[0]
    c = CONV_CH
    hb = tm // 8

    def body(b_ref, c_ref, h_ref, cp_ref, hp_ref, w_ref, bias_ref, o_ref, q_scr):
        i = pl.program_id(0)
        q = c_ref[...] * h_ref[...]
        halo = jnp.where(i > 0, cp_ref[...] * hp_ref[...], 0.0)
        q_scr[0:8, :] = halo
        q_scr[8:, :] = q
        z = (w_ref[2:3, :] * q + w_ref[1:2, :] * q_scr[pl.ds(7, tm), :] + w_ref[0:1, :] * q_scr[pl.ds(6, tm), :]
             + bias_ref[...])
        o_ref[...] = (b_ref[...] * z).astype(BF16)

    prev = lambda blk: pl.BlockSpec((8, c), lambda i: (jnp.maximum(i * hb - 1, 0), blk))
    return pl.pallas_call(
        body, name=name, grid=(t // tm,),
        in_specs=[_row_spec(tm, c, 0), _row_spec(tm, c, 1), _row_spec(tm, c, 2), prev(1), prev(2),
                  pl.BlockSpec((8, c), lambda i: (0, 0)), _vec_spec(c)],
        out_specs=_row_spec(tm, c),
        out_shape=jax.ShapeDtypeStruct((t, c), BF16),
        scratch_shapes=[pltpu.VMEM((tm + 8, c), F32)],
        compiler_params=_cparams(("parallel",)),
    )(proj, proj, proj, proj, proj, cw, cb)


def _conv_bwd(dyc, proj, cw, cb, *, name, tm=512):
    t = proj.shape[0]
    c = CONV_CH
    hb = tm // 8
    nblk = t // 8

    def body(d_ref, b_ref, c_ref, h_ref, cp_ref, hp_ref, dn_ref, bn_ref, w_ref, bias_ref,
             db_ref, dc_ref, dh_ref, dw_ref, q_scr, z_scr):
        i = pl.program_id(0)
        last = pl.num_programs(0) - 1
        cc = c_ref[...]
        ch = h_ref[...]
        q = cc * ch
        halo = jnp.where(i > 0, cp_ref[...] * hp_ref[...], 0.0)
        q_scr[0:8, :] = halo
        q_scr[8:, :] = q
        w0, w1, w2 = w_ref[0:1, :], w_ref[1:2, :], w_ref[2:3, :]
        qm1 = q_scr[pl.ds(7, tm), :]
        qm2 = q_scr[pl.ds(6, tm), :]
        z = w2 * q + w1 * qm1 + w0 * qm2 + bias_ref[...]
        d = d_ref[...]
        bb = b_ref[...]
        db_ref[...] = (d * z).astype(BF16)
        dz = d * bb
        z_scr[0:tm, :] = dz
        z_scr[tm:, :] = jnp.where(i < last, dn_ref[...] * bn_ref[...], 0.0)
        dq = w2 * dz + w1 * z_scr[pl.ds(1, tm), :] + w0 * z_scr[pl.ds(2, tm), :]
        dc_ref[...] = (dq * ch).astype(BF16)
        dh_ref[...] = (dq * cc).astype(BF16)
        row = lax.broadcasted_iota(jnp.int32, (8, c), 0)
        part = jnp.zeros((8, c), F32)
        for k, term in enumerate((dz * qm2, dz * qm1, dz * q, dz)):
            part = jnp.where(row == k, jnp.sum(term, axis=0, keepdims=True), part)

        @pl.when(i == 0)
        def _():
            dw_ref[...] = part

        @pl.when(i > 0)
        def _():
            dw_ref[...] += part

    prev = lambda blk: pl.BlockSpec((8, c), lambda i: (jnp.maximum(i * hb - 1, 0), blk))
    nxt_p = pl.BlockSpec((8, c), lambda i: (jnp.minimum((i + 1) * hb, nblk - 1), 0))
    nxt_d = pl.BlockSpec((8, c), lambda i: (jnp.minimum((i + 1) * hb, nblk - 1), 0))
    return pl.pallas_call(
        body, name=name, grid=(t // tm,),
        in_specs=[_row_spec(tm, c), _row_spec(tm, c, 0), _row_spec(tm, c, 1), _row_spec(tm, c, 2),
                  prev(1), prev(2), nxt_d, nxt_p, pl.BlockSpec((8, c), lambda i: (0, 0)), _vec_spec(c)],
        out_specs=[_row_spec(tm, c)] * 3 + [pl.BlockSpec((8, c), lambda i: (0, 0))],
        out_shape=[jax.ShapeDtypeStruct((t, c), BF16)] * 3 + [jax.ShapeDtypeStruct((8, c), F32)],
        scratch_shapes=[pltpu.VMEM((tm + 8, c), F32), pltpu.VMEM((tm + 8, c), F32)],
        compiler_params=_cparams(("arbitrary",)),
    )(dyc, proj, proj, proj, proj, proj, dyc, proj, cw, cb)


def _gate_fwd(yconv, glu, proj, *, name, tm=512):
    t, d = yconv.shape

    def body(yc_ref, ga_ref, gb_ref, gc_ref, gs_ref, o_ref):
        yssm = ga_ref[...] * _sigmoid(gb_ref[...])
        o_ref[...] = (_sigmoid(gc_ref[...]) * yc_ref[...] + _sigmoid(gs_ref[...]) * yssm).astype(BF16)

    return pl.pallas_call(
        body, name=name, grid=(t // tm,),
        in_specs=[_row_spec(tm, d), _row_spec(tm, d, 0), _row_spec(tm, d, 1), _row_spec(tm, d, 2), _row_spec(tm, d, 3)],
        out_specs=_row_spec(tm, d), out_shape=jax.ShapeDtypeStruct((t, d), BF16),
        compiler_params=_cparams(("parallel",)),
    )(yconv, glu, glu, proj, proj)


def _gate_bwd(dm, yconv, glu, proj, *, name, tm=512):
    t, d = yconv.shape

    def body(dm_ref, yc_ref, ga_ref, gb_ref, gc_ref, gs_ref, dyc_ref, dglu_ref, dgate_ref):
        dmv = dm_ref[...]
        sc = _sigmoid(gc_ref[...])
        ss = _sigmoid(gs_ref[...])
        sb = _sigmoid(gb_ref[...])
        ga = ga_ref[...]
        yssm = ga * sb
        dyc_ref[...] = (dmv * sc).astype(BF16)
        dgate_ref[:, 0:d] = (dmv * yc_ref[...] * (sc * (1.0 - sc))).astype(BF16)
        dys = dmv * ss
        dgate_ref[:, d:2 * d] = (dmv * yssm * (ss * (1.0 - ss))).astype(BF16)
        dglu_ref[:, 0:d] = (dys * sb).astype(BF16)
        dglu_ref[:, d:2 * d] = (dys * ga * (sb * (1.0 - sb))).astype(BF16)

    return pl.pallas_call(
        body, name=name, grid=(t // tm,),
        in_specs=[_row_spec(tm, d), _row_spec(tm, d), _row_spec(tm, d, 0), _row_spec(tm, d, 1),
                  _row_spec(tm, d, 2), _row_spec(tm, d, 3)],
        out_specs=[_row_spec(tm, d), _row_spec(tm, 2 * d), _row_spec(tm, 2 * d)],
        out_shape=[jax.ShapeDtypeStruct((t, d), BF16), jax.ShapeDtypeStruct((t, 2 * d), BF16),
                   jax.ShapeDtypeStruct((t, 2 * d), BF16)],
        compiler_params=_cparams(("parallel",)),
    )(dm, yconv, glu, glu, proj, proj)


_GELU_C = math.sqrt(2.0 / math.pi)


def _gelu(v):
    return 0.5 * v * (1.0 + jnp.tanh(_GELU_C * (v + 0.044715 * v * v * v)))


def _gelu_grad(v):
    th = jnp.tanh(_GELU_C * (v + 0.044715 * v * v * v))
    return 0.5 * (1.0 + th) + 0.5 * v * (1.0 - th * th) * (_GELU_C * (1.0 + 3.0 * 0.044715 * v * v))


def _cmul(ar, ai, br, bi):
    return ar * br - ai * bi, ar * bi + ai * br


def _scan_fwd(proj, bmat, tab, *, name, plans=()):
    t = proj.shape[0]
    tt, cbw = SCAN_TT, SCAN_CB
    w2 = 2 * cbw

    def body(u_ref, b_ref, tab_ref, s_ref, sb_ref, bu_scr, carry):
        ti = pl.program_id(1)

        @pl.when(ti == 0)
        def _():
            carry[...] = jnp.zeros_like(carry)

        bu_scr[...] = jnp.dot(u_ref[...].astype(BF16), b_ref[...], preferred_element_type=F32)
        row = lax.broadcasted_iota(jnp.int32, (8, cbw), 0)

        def blk(bi, c):
            cr, ci = c
            r0 = pl.multiple_of(bi * 8, 8)
            xr = bu_scr[pl.ds(r0, 8), 0:cbw]
            xi = bu_scr[pl.ds(r0, 8), cbw:w2]
            for k, sh in enumerate((1, 2, 4)):
                kr = tab_ref[k:k + 1, 0:cbw]
                ki = tab_ref[k:k + 1, cbw:w2]
                sr = jnp.where(row >= sh, pltpu.roll(xr, sh, 0), 0.0)
                si = jnp.where(row >= sh, pltpu.roll(xi, sh, 0), 0.0)
                pr, pi = _cmul(kr, ki, sr, si)
                xr = xr + pr
                xi = xi + pi
            pr, pi = _cmul(tab_ref[8:16, 0:cbw], tab_ref[8:16, cbw:w2], cr, ci)
            xr = xr + pr
            xi = xi + pi
            s_ref[pl.ds(r0, 8), 0:cbw] = xr
            s_ref[pl.ds(r0, 8), cbw:w2] = xi
            return (jnp.broadcast_to(xr[7:8, :], (8, cbw)), jnp.broadcast_to(xi[7:8, :], (8, cbw)))

        cr, ci = lax.fori_loop(0, tt // 8, blk, (carry[:, 0:cbw], carry[:, cbw:w2]))
        carry[:, 0:cbw] = cr
        carry[:, cbw:w2] = ci
        sb_ref[...] = s_ref[...].astype(BF16)

    (s, sb), riders = _call_with_plans(
        body, plans, name=name, grid=(SCAN_NCB, t // tt),
        in_specs=[pl.BlockSpec((tt, SCAN_UW), lambda cb, ti: (ti, 3 * SCAN_NCB + cb)),
                  pl.BlockSpec((None, SCAN_UW, w2), lambda cb, ti: (cb, 0, 0)),
                  pl.BlockSpec((16, w2), lambda cb, ti: (0, cb))],
        out_specs=[pl.BlockSpec((tt, w2), lambda cb, ti: (ti, cb))] * 2,
        out_shape=[jax.ShapeDtypeStruct((t, 2 * SSM_CH), F32), jax.ShapeDtypeStruct((t, 2 * SSM_CH), BF16)],
        scratch_shapes=[pltpu.VMEM((tt, w2), F32), pltpu.VMEM((8, w2), F32)],
        semantics=("parallel", "arbitrary"), operands=(proj, bmat, tab))
    return s, sb, riders


def _scan_bwd(dyb, cmat_t, s, tabb, *, name, plans=()):
    t = s.shape[0]
    tt, cbw = SCAN_TT, SCAN_CB
    w2 = 2 * cbw
    nt = t // tt
    hb = tt // 8

    def body(dy_ref, c_ref, s_ref, sp_ref, tab_ref, h_ref, da_ref, g_scr, s_scr, carry):
        ti = pl.program_id(1)

        @pl.when(ti == 0)
        def _():
            carry[...] = jnp.zeros_like(carry)
            da_ref[...] = jnp.zeros_like(da_ref)

        g_scr[...] = jnp.dot(dy_ref[...], c_ref[...], preferred_element_type=F32)
        s_scr[0:8, :] = jnp.where(ti < nt - 1, sp_ref[...], 0.0)
        s_scr[8:, :] = s_ref[...]
        row = lax.broadcasted_iota(jnp.int32, (8, cbw), 0)

        def blk(k, c):
            cr, ci, ar, ai = c
            bi = hb - 1 - k
            r0 = pl.multiple_of(bi * 8, 8)
            xr = g_scr[pl.ds(r0, 8), 0:cbw]
            xi = g_scr[pl.ds(r0, 8), cbw:w2]
            for j, sh in enumerate((1, 2, 4)):
                kr = tab_ref[j:j + 1, 0:cbw]
                ki = tab_ref[j:j + 1, cbw:w2]
                sr = jnp.where(row < 8 - sh, pltpu.roll(xr, 8 - sh, 0), 0.0)
                si = jnp.where(row < 8 - sh, pltpu.roll(xi, 8 - sh, 0), 0.0)
                pr, pi = _cmul(kr, ki, sr, si)
                xr = xr + pr
                xi = xi + pi
            pr, pi = _cmul(tab_ref[8:16, 0:cbw], tab_ref[8:16, cbw:w2], cr, ci)
            xr = xr + pr
            xi = xi + pi
            h_ref[pl.ds(r0, 8), 0:cbw] = xr.astype(BF16)
            h_ref[pl.ds(r0, 8), cbw:w2] = xi.astype(BF16)
            pvr = s_scr[pl.ds(r0, 8), 0:cbw]
            pvi = s_scr[pl.ds(r0, 8), cbw:w2]
            cur_r = s_scr[pl.ds(r0 + 8, 8), 0:cbw]
            cur_i = s_scr[pl.ds(r0 + 8, 8), cbw:w2]
            spr = jnp.where(row == 0, jnp.broadcast_to(pvr[7:8, :], (8, cbw)), pltpu.roll(cur_r, 1, 0))
            spi = jnp.where(row == 0, jnp.broadcast_to(pvi[7:8, :], (8, cbw)), pltpu.roll(cur_i, 1, 0))
            ar = ar + spr * xr + spi * xi
            ai = ai + spr * xi - spi * xr
            return (jnp.broadcast_to(xr[0:1, :], (8, cbw)), jnp.broadcast_to(xi[0:1, :], (8, cbw)), ar, ai)

        z = jnp.zeros((8, cbw), F32)
        cr, ci, ar, ai = lax.fori_loop(0, hb, blk, (carry[:, 0:cbw], carry[:, cbw:w2], z, z))
        carry[:, 0:cbw] = cr
        carry[:, cbw:w2] = ci
        da_ref[:, 0:cbw] += ar
        da_ref[:, cbw:w2] += ai

    rt = lambda ti: nt - 1 - ti
    (h, da), riders = _call_with_plans(
        body, plans, name=name, grid=(SCAN_NCB, nt),
        in_specs=[pl.BlockSpec((tt, SCAN_UW), lambda cb, ti: (rt(ti), cb)),
                  pl.BlockSpec((None, SCAN_UW, w2), lambda cb, ti: (cb, 0, 0)),
                  pl.BlockSpec((tt, w2), lambda cb, ti: (rt(ti), cb)),
                  pl.BlockSpec((8, w2), lambda cb, ti: (jnp.maximum(rt(ti) * hb - 1, 0), cb)),
                  pl.BlockSpec((16, w2), lambda cb, ti: (0, cb))],
        out_specs=[pl.BlockSpec((tt, w2), lambda cb, ti: (rt(ti), cb)),
                   pl.BlockSpec((8, w2), lambda cb, ti: (0, cb))],
        out_shape=[jax.ShapeDtypeStruct((t, 2 * SSM_CH), BF16), jax.ShapeDtypeStruct((8, 2 * SSM_CH), F32)],
        scratch_shapes=[pltpu.VMEM((tt, w2), F32), pltpu.VMEM((tt + 8, w2), F32), pltpu.VMEM((8, w2), F32)],
        semantics=("parallel", "arbitrary"), operands=(dyb, cmat_t, s, s, tabb))
    return h, da, riders


def _s5_out(ymm, proj, dvec, *, name, tm=512):
    t, w = ymm.shape

    def body(y_ref, u_ref, d_ref, yo_ref, sg_ref, ub_ref):
        u = u_ref[...]
        y = y_ref[...] + d_ref[...] * u
        yo_ref[...] = y
        sg_ref[...] = _gelu(y).astype(BF16)
        ub_ref[...] = u.astype(BF16)

    return pl.pallas_call(
        body, name=name, grid=(t // tm,),
        in_specs=[_row_spec(tm, w), _row_spec(tm, w, 3), _vec_spec(w)],
        out_specs=[_row_spec(tm, w)] * 3,
        out_shape=[jax.ShapeDtypeStruct((t, w), F32), jax.ShapeDtypeStruct((t, w), BF16), jax.ShapeDtypeStruct((t, w), BF16)],
        compiler_params=_cparams(("parallel",)),
    )(ymm, proj, dvec)


def _s5_bwd_in(dsg, y, proj, *, name, tm=512):
    t, w = y.shape

    def body(d_ref, y_ref, u_ref, dy_ref, dyb_ref, dd_ref):
        i = pl.program_id(0)
        dy = d_ref[...] * _gelu_grad(y_ref[...])
        dy_ref[...] = dy
        dyb_ref[...] = dy.astype(BF16)
        part = jnp.sum(dy * u_ref[...], axis=0, keepdims=True)

        @pl.when(i == 0)
        def _():
            dd_ref[...] = part

        @pl.when(i > 0)
        def _():
            dd_ref[...] += part

    return pl.pallas_call(
        body, name=name, grid=(t // tm,),
        in_specs=[_row_spec(tm, w), _row_spec(tm, w), _row_spec(tm, w, 3)],
        out_specs=[_row_spec(tm, w), _row_spec(tm, w), _vec_spec(w)],
        out_shape=[jax.ShapeDtypeStruct((t, w), F32), jax.ShapeDtypeStruct((t, w), BF16), jax.ShapeDtypeStruct((1, w), F32)],
        compiler_params=_cparams(("arbitrary",)),
    )(dsg, y, proj)


def _s5_du(dumm, dy, dvec, *, name, tm=512):
    t, w = dy.shape

    def body(a_ref, dy_ref, d_ref, o_ref):
        o_ref[...] = (a_ref[...] + d_ref[...] * dy_ref[...]).astype(BF16)

    return pl.pallas_call(
        body, name=name, grid=(t // tm,), in_specs=[_row_spec(tm, w), _row_spec(tm, w), _vec_spec(w)],
        out_specs=_row_spec(tm, w), out_shape=jax.ShapeDtypeStruct((t, w), BF16),
        compiler_params=_cparams(("parallel",)),
    )(dumm, dy, dvec)


def _s5_discretise(lam_re, lam_im, log_step, b_re, b_im):
    lam = lax.complex(lam_re, lam_im)
    dt = jnp.exp(log_step)[:, None]
    a = jnp.exp(lam * dt)
    bbar = ((a - 1.0) / lam)[..., None] * lax.complex(b_re, b_im)
    return jnp.real(a), jnp.imag(a), jnp.real(bbar), jnp.imag(bbar)


def _perm_cols(re, im):
    lead = re.shape[:-1]
    r = re.reshape(lead + (SCAN_NCB, 1, SCAN_CB))
    i = im.reshape(lead + (SCAN_NCB, 1, SCAN_CB))
    return jnp.concatenate([r, i], axis=-2).reshape(lead + (2 * SSM_CH,))


def _unperm_cols(x):
    lead = x.shape[:-1]
    y = x.reshape(lead + (SCAN_NCB, 2, SCAN_CB))
    return y[..., 0, :].reshape(lead + (SSM_CH,)), y[..., 1, :].reshape(lead + (SSM_CH,))


def _compact(re, im):
    _, r, c = re.shape
    eye = jnp.eye(SCAN_GPB, dtype=re.dtype)

    def half(x):
        x = x.reshape(SCAN_NCB, SCAN_GPB, r, c)
        return (eye[None, :, None, :, None] * x[:, :, :, None, :]).reshape(SCAN_NCB, SCAN_GPB * r, SCAN_GPB * c)

    return jnp.concatenate([half(re), half(im)], axis=-1)


def _compact_extract(x, r):
    c = SSM_STATE
    eye = jnp.eye(SCAN_GPB, dtype=x.dtype)
    y = x.reshape(SCAN_NCB, SCAN_GPB, r, 2, SCAN_GPB, c)
    dg = jnp.sum(y * eye[None, :, None, None, :, None], axis=4).reshape(SSM_GROUPS, r, 2, c)
    return dg[:, :, 0, :], dg[:, :, 1, :]


def _pow_table(ar, ai, descending=False):
    ar = ar.reshape(1, SSM_CH)
    ai = ai.reshape(1, SSM_CH)
    pw = [(ar, ai)]
    for _ in range(7):
        pw.append(_cmul(pw[-1][0], pw[-1][1], ar, ai))
    zero = (jnp.zeros_like(ar), jnp.zeros_like(ar))
    rows = [pw[0], pw[1], pw[3]] + [zero] * 5 + (pw[::-1] if descending else pw)
    re = jnp.concatenate([r for r, _ in rows], axis=0)
    im = jnp.concatenate([i for _, i in rows], axis=0)
    return _perm_cols(re, im)


def _place():
    x, y, c = lax.axis_index("x"), lax.axis_index("y"), lax.axis_index("c")
    chips = [(1 - x, y), (x, 1 - y), (1 - x, 1 - y)]
    return x, y, c, chips


def _dev(px, py, pc):
    return 4 * px + 2 * py + pc


class _Plan:
    def __init__(self, ins, out_shapes, sem_shapes, start, finish, middle=None):
        self.ins, self.out_shapes, self.sem_shapes = list(ins), list(out_shapes), list(sem_shapes)
        self.start, self.finish, self.middle = start, finish, middle


def _split_plan_refs(plans, in_refs, out_refs, sem_refs):
    res, i, o, s = [], 0, 0, 0
    for p in plans:
        ni, no, ns = len(p.ins), len(p.out_shapes), len(p.sem_shapes)
        res.append((in_refs[i:i + ni], out_refs[o:o + no], sem_refs[s:s + ns]))
        i, o, s = i + ni, o + no, s + ns
    return res


def _run_plans(plans, *, name):
    ins = [a for p in plans for a in p.ins]
    outs = [o for p in plans for o in p.out_shapes]
    sems = [s for p in plans for s in p.sem_shapes]
    any_spec = pl.BlockSpec(memory_space=pl.ANY)

    def body(*refs):
        parts = _split_plan_refs(plans, refs[:len(ins)], refs[len(ins):len(ins) + len(outs)], refs[len(ins) + len(outs):])
        for p, r in zip(plans, parts):
            p.start(*r)
        for p, r in zip(plans, parts):
            if p.middle:
                p.middle(*r)
        for p, r in zip(plans, parts):
            p.finish(*r)

    res = pl.pallas_call(body, name=name, in_specs=[any_spec] * len(ins), out_specs=[any_spec] * len(outs),
                         out_shape=outs, scratch_shapes=sems)(*ins)
    return _split_plan_refs(plans, [], res, [])


def _run_plans_on_sequencer(plans, peers_of, *, name, collective_id, after=()):
    ins = [a for p in plans for a in p.ins]
    outs = [o for p in plans for o in p.out_shapes]
    sems = [s for p in plans for s in p.sem_shapes]

    def body(*refs):
        x, y, c, chips = _place()
        peers = peers_of(x, y, c, chips)
        barrier = pltpu.get_barrier_semaphore()
        for peer in peers:
            pl.semaphore_signal(barrier, inc=1, device_id=peer, device_id_type=MESH)
        pl.semaphore_wait(barrier, len(peers))
        n_in = len(ins) + len(after)
        parts = _split_plan_refs(plans, refs[:len(ins)], refs[n_in:n_in + len(outs)], refs[n_in + len(outs):])
        for p, r in zip(plans, parts):
            p.start(*r)
        for p, r in zip(plans, parts):
            if p.middle:
                p.middle(*r)
        for p, r in zip(plans, parts):
            p.finish(*r)

    res = pl.kernel(body, name=name, out_type=outs, mesh=plsc.ScalarSubcoreMesh(axis_name="seq", num_cores=1),
                    scratch_types=sems, compiler_params=pltpu.CompilerParams(collective_id=collective_id))(*ins, *after)
    return _split_plan_refs(plans, [], list(res), [])


def _call_with_plans(body, plans, *, name, grid, in_specs, out_specs, out_shape, scratch_shapes, semantics, operands):
    plans = list(plans)
    if not plans:
        res = pl.pallas_call(body, name=name, grid=grid, in_specs=in_specs, out_specs=out_specs, out_shape=out_shape,
                             scratch_shapes=scratch_shapes, compiler_params=_cparams(semantics))(*operands)
        return list(res), []
    n_in, n_out, n_scr = len(in_specs), len(out_specs), len(scratch_shapes)
    p_ins = [a for p in plans for a in p.ins]
    p_outs = [o for p in plans for o in p.out_shapes]
    p_sems = [s for p in plans for s in p.sem_shapes]
    nsteps = math.prod(grid)
    any_spec = pl.BlockSpec(memory_space=pl.ANY)

    def wrapped(*refs):
        bounds = [n_in, len(p_ins), n_out, len(p_outs), n_scr]
        parts, pos = [], 0
        for b in bounds:
            parts.append(refs[pos:pos + b])
            pos += b
        ins, p_in, outs, p_out, scr = parts
        step = pl.program_id(0)
        for ax in range(1, len(grid)):
            step = step * grid[ax] + pl.program_id(ax)
        riders = _split_plan_refs(plans, p_in, p_out, refs[pos:])

        @pl.when(step == 0)
        def _():
            for p, r in zip(plans, riders):
                p.start(*r)

        mids = [(p, r) for p, r in zip(plans, riders) if p.middle]
        mid_step = nsteps // 2
        split_mid = mids and 0 < mid_step < nsteps - 1
        if split_mid:
            @pl.when(step == mid_step)
            def _():
                for p, r in mids:
                    p.middle(*r)

        body(*ins, *outs, *scr)

        @pl.when(step == nsteps - 1)
        def _():
            if not split_mid:
                for p, r in mids:
                    p.middle(*r)
            for p, r in zip(plans, riders):
                p.finish(*r)

    res = pl.pallas_call(
        wrapped, name=name, grid=grid, in_specs=list(in_specs) + [any_spec] * len(p_ins),
        out_specs=list(out_specs) + [any_spec] * len(p_outs), out_shape=list(out_shape) + p_outs,
        scratch_shapes=list(scratch_shapes) + p_sems, compiler_params=_cparams(("arbitrary",) * len(grid)),
    )(*operands, *p_ins)
    return list(res[:n_out]), [r[1] for r in _split_plan_refs(plans, [], res[n_out:], [])]


def _gather_plan(shards):
    n = len(shards)
    nk = 8

    def make(ins, outs, sems):
        send, recv, lsem = sems
        x, y, c, _ = _place()
        me, sib, xn, yn, dg = (x, y, c), (x, y, 1 - c), (1 - x, y, c), (x, 1 - y, c), (1 - x, 1 - y, c)

        def part(w, block, half):
            ref = outs[w].at[_dev(*block)]
            if half is None:
                return ref
            rows = shards[w].shape[0] // 2
            return ref.at[pl.ds(half * rows, rows)]

        def copy(w, k, block, to, half=None, src=None):
            dst = part(w, block, half)
            return pltpu.make_async_remote_copy(
                src_ref=dst if src is None else src, dst_ref=dst,
                send_sem=send.at[w * nk + k], recv_sem=recv.at[w * nk + k], device_id=to, device_id_type=MESH)

        mine = [pltpu.make_async_copy(ins[w], outs[w].at[_dev(*me)], lsem.at[w]) for w in range(n)]
        return copy, mine, me, sib, xn, yn, dg

    def first_copies(copy, me, sib, xn, yn, ins):
        return [copy(w, k, me, to, src=ins[w]) for w in range(n) for k, to in ((0, sib), (1, xn), (2, yn))]

    def start(ins, outs, sems):
        copy, mine, me, sib, xn, yn, _ = make(ins, outs, sems)
        for cp in mine + first_copies(copy, me, sib, xn, yn, ins):
            cp.start()

    def middle(ins, outs, sems):
        copy, _, me, sib, xn, yn, _ = make(ins, outs, sems)
        for w in range(n):
            copy(w, 1, xn, me).wait_recv()
            copy(w, 3, xn, yn, half=0).start()
            copy(w, 5, xn, sib).start()
        for w in range(n):
            copy(w, 2, yn, me).wait_recv()
            copy(w, 4, yn, xn, half=1).start()
            copy(w, 6, yn, sib).start()

    def finish(ins, outs, sems):
        copy, mine, me, sib, xn, yn, dg = make(ins, outs, sems)
        last = []
        for w in range(n):
            copy(w, 3, dg, me, half=0).wait_recv()
            copy(w, 4, dg, me, half=1).wait_recv()
            fwd = copy(w, 7, dg, sib)
            fwd.start()
            last.append(fwd)
        sx, sy, sd = (1 - me[0], me[1], 1 - me[2]), (me[0], 1 - me[1], 1 - me[2]), (1 - me[0], 1 - me[1], 1 - me[2])
        for w in range(n):
            copy(w, 0, sib, me).wait_recv()
            copy(w, 5, sx, me).wait_recv()
            copy(w, 6, sy, me).wait_recv()
            copy(w, 7, sd, me).wait_recv()
        for cp in first_copies(copy, me, sib, xn, yn, ins) + last:
            cp.wait_send()
        for w in range(n):
            copy(w, 3, xn, yn, half=0).wait_send()
            copy(w, 5, xn, sib).wait_send()
            copy(w, 4, yn, xn, half=1).wait_send()
            copy(w, 6, yn, sib).wait_send()
        for cp in mine:
            cp.wait()

    return _Plan(shards, [jax.ShapeDtypeStruct((N_DEV,) + s.shape, s.dtype) for s in shards],
                 [pltpu.SemaphoreType.DMA((nk * n,)), pltpu.SemaphoreType.DMA((nk * n,)), pltpu.SemaphoreType.DMA((n,))],
                 start, finish, middle)


def _swap_plan(copies_of, n_copies, ins, out_shapes):
    def cps(in_refs, out_refs, sems):
        return copies_of(in_refs, out_refs, sems[0], sems[1])

    def start(in_refs, out_refs, sems):
        for cp in cps(in_refs, out_refs, sems):
            cp.start()

    def finish(in_refs, out_refs, sems):
        all_cps = cps(in_refs, out_refs, sems)
        for cp in all_cps:
            cp.wait_recv()
        for cp in all_cps:
            cp.wait_send()

    return _Plan(ins, out_shapes, [pltpu.SemaphoreType.DMA((n_copies,)), pltpu.SemaphoreType.DMA((n_copies,))],
                 start, finish)


def _sibling_plan(grads):
    n = len(grads)

    def copies(ins, outs, send, recv):
        x, y, c, chips = _place()
        owners = [(x, y)] + chips
        return [pltpu.make_async_remote_copy(
            src_ref=ins[w].at[_dev(*chip, 1 - c)], dst_ref=outs[w].at[k], send_sem=send.at[w * 4 + k],
            recv_sem=recv.at[w * 4 + k], device_id=(x, y, 1 - c), device_id_type=MESH)
            for w in range(n) for k, chip in enumerate(owners)]

    return _swap_plan(copies, 4 * n, grads, [jax.ShapeDtypeStruct((4,) + g.shape[1:], g.dtype) for g in grads])


def _chip_plan(parts, js=(0, 1, 2)):
    n, nj = len(parts), len(js)

    def copies(ins, outs, send, recv):
        x, y, c, chips = _place()
        return [pltpu.make_async_remote_copy(
            src_ref=ins[w].at[j], dst_ref=outs[w * nj + k], send_sem=send.at[w * nj + k],
            recv_sem=recv.at[w * nj + k], device_id=(*chips[j], c), device_id_type=MESH)
            for w in range(n) for k, j in enumerate(js)]

    return _swap_plan(copies, n * nj, parts,
                      [jax.ShapeDtypeStruct(p.shape[1:], p.dtype) for p in parts for _ in js])


UPDATE_TILE_BYTES = 1536 * 1024


def _row_tile(r, c):
    best = 8
    for t in range(8, r + 1, 8):
        if r % t == 0 and t * c * 4 <= UPDATE_TILE_BYTES:
            best = t
    return best


def _chip_partial(g, sib, ids, *, name):
    _, r, c = g.shape
    tr = _row_tile(r, c)

    def body(ids_ref, g_ref, s_ref, o_ref):
        o_ref[...] = (g_ref[...] + s_ref[...]).astype(BF16)

    return pl.pallas_call(
        body, name=name,
        grid_spec=pltpu.PrefetchScalarGridSpec(
            num_scalar_prefetch=1, grid=(3, r // tr),
            in_specs=[pl.BlockSpec((None, tr, c), lambda j, i, ids_ref: (ids_ref[j], i, 0)),
                      pl.BlockSpec((None, tr, c), lambda j, i, ids_ref: (j + 1, i, 0))],
            out_specs=pl.BlockSpec((None, tr, c), lambda j, i, ids_ref: (j, i, 0))),
        out_shape=jax.ShapeDtypeStruct((3, r, c), BF16),
        compiler_params=_cparams(("parallel", "parallel")),
    )(ids, g, sib)


def _adamw_math(w, g, m, v):
    m = ADAM_B1 * m + (1.0 - ADAM_B1) * g
    v = ADAM_B2 * v + (1.0 - ADAM_B2) * (g * g)
    m_hat = m / (1.0 - ADAM_B1 ** ADAM_STEP)
    v_hat = v / (1.0 - ADAM_B2 ** ADAM_STEP)
    delta = -ADAM_LR * (m_hat / (jnp.sqrt(v_hat) + ADAM_EPS) + ADAM_WD * w)
    return delta, m, v


def _shard_update(g, sib, rem, me, w, m, v, *, name):
    r, c = w.shape
    tr = _row_tile(r, c)

    def body(me_ref, g_ref, s_ref, r0_ref, r1_ref, r2_ref, w_ref, m_ref, v_ref, go_ref, d_ref, mo_ref, vo_ref):
        gt = g_ref[...] + s_ref[...]
        gt = gt + r0_ref[...].astype(F32)
        gt = gt + r1_ref[...].astype(F32)
        gt = gt + r2_ref[...].astype(F32)
        go_ref[...] = gt
        d, mn, vn = _adamw_math(w_ref[...], gt, m_ref[...], v_ref[...])
        d_ref[...] = d
        mo_ref[...] = mn
        vo_ref[...] = vn

    blk = lambda k: pl.BlockSpec((None, tr, c), lambda i, me_ref: (k, i, 0))
    plain = pl.BlockSpec((tr, c), lambda i, me_ref: (i, 0))
    return pl.pallas_call(
        body, name=name,
        grid_spec=pltpu.PrefetchScalarGridSpec(
            num_scalar_prefetch=1, grid=(r // tr,),
            in_specs=[pl.BlockSpec((None, tr, c), lambda i, me_ref: (me_ref[0], i, 0)), blk(0), plain, plain, plain,
                      plain, plain, plain],
            out_specs=[plain] * 4),
        out_shape=[jax.ShapeDtypeStruct((r, c), F32)] * 4,
        compiler_params=_cparams(("parallel",)),
    )(me, g, sib, *rem, w, m, v)


def _small_update(gathered, w, m, v, *, name):
    _, r, c = gathered.shape

    def body(g_ref, w_ref, m_ref, v_ref, go_ref, d_ref, mo_ref, vo_ref):
        gt = g_ref[0]
        for k in range(1, N_DEV):
            gt = gt + g_ref[k]
        go_ref[...] = gt
        d, mn, vn = _adamw_math(w_ref[...], gt, m_ref[...], v_ref[...])
        d_ref[...] = d
        mo_ref[...] = mn
        vo_ref[...] = vn

    return pl.pallas_call(
        body, name=name, out_shape=[jax.ShapeDtypeStruct((r, c), F32)] * 4,
        compiler_params=pltpu.CompilerParams(vmem_limit_bytes=VMEM_LIMIT),
    )(gathered, w, m, v)


SMALL_UNIT = 1024


def _pack(parts):
    flat = []
    for p in parts:
        f = p.reshape(-1).astype(F32)
        pad = (-f.shape[0]) % SMALL_UNIT
        flat.append(jnp.pad(f, (0, pad)))
    return jnp.concatenate(flat).reshape(-1, 128)


def _unpack(buf, shapes):
    flat = buf.reshape(-1)
    out, off = [], 0
    for s in shapes:
        nel = math.prod(s)
        out.append(flat[off:off + nel].reshape(s))
        off += nel + ((-nel) % SMALL_UNIT)
    return out


def kernel(x, p, ffn1_w_in, ffn1_w_out, ln1_g, ln1_b, mix_w_in, conv_w, conv_b, conv_w_out, ssm_lam_re, ssm_lam_im, ssm_log_step, ssm_b_re, ssm_b_im, ssm_c_re, ssm_c_im, ssm_d, ssm_w_glu, mix_w_out, ln2_g, ln2_b, ffn2_w_in, ffn2_w_out, ln3_g, ln3_b, ple_w_in, ple_w_gate, ln4_g, ln4_b, loss_target, m_ffn1_w_in, m_ffn1_w_out, m_ln1_g, m_ln1_b, m_mix_w_in, m_conv_w, m_conv_b, m_conv_w_out, m_ssm_lam_re, m_ssm_lam_im, m_ssm_log_step, m_ssm_b_re, m_ssm_b_im, m_ssm_c_re, m_ssm_c_im, m_ssm_d, m_ssm_w_glu, m_mix_w_out, m_ln2_g, m_ln2_b, m_ffn2_w_in, m_ffn2_w_out, m_ln3_g, m_ln3_b, m_ple_w_in, m_ple_w_gate, m_ln4_g, m_ln4_b, v_ffn1_w_in, v_ffn1_w_out, v_ln1_g, v_ln1_b, v_mix_w_in, v_conv_w, v_conv_b, v_conv_w_out, v_ssm_lam_re, v_ssm_lam_im, v_ssm_log_step, v_ssm_b_re, v_ssm_b_im, v_ssm_c_re, v_ssm_c_im, v_ssm_d, v_ssm_w_glu, v_mix_w_out, v_ln2_g, v_ln2_b, v_ffn2_w_in, v_ffn2_w_out, v_ln3_g, v_ln3_b, v_ple_w_in, v_ple_w_gate, v_ln4_g, v_ln4_b):
    args = dict(locals())
    big = ['ffn1_w_in', 'ffn1_w_out', 'mix_w_in', 'conv_w_out', 'ssm_w_glu', 'mix_w_out',
           'ffn2_w_in', 'ffn2_w_out', 'ple_w_in', 'ple_w_gate']
    small = ['ln1_g', 'ln1_b', 'conv_b', 'ssm_lam_re', 'ssm_lam_im', 'ssm_log_step', 'ssm_b_re', 'ssm_b_im',
             'ssm_c_re', 'ssm_c_im', 'ssm_d', 'ln2_g', 'ln2_b', 'ln3_g', 'ln3_b', 'ln4_g', 'ln4_b']
    order = ['ffn1_w_in', 'ffn1_w_out', 'ln1_g', 'ln1_b', 'mix_w_in', 'conv_w', 'conv_b', 'conv_w_out',
             'ssm_lam_re', 'ssm_lam_im', 'ssm_log_step', 'ssm_b_re', 'ssm_b_im', 'ssm_c_re', 'ssm_c_im', 'ssm_d',
             'ssm_w_glu', 'mix_w_out', 'ln2_g', 'ln2_b', 'ffn2_w_in', 'ffn2_w_out', 'ln3_g', 'ln3_b',
             'ple_w_in', 'ple_w_gate', 'ln4_g', 'ln4_b']

    t = x.shape[1]
    d = x.shape[2]
    xc_, yc_, cc_ = lax.axis_index("x"), lax.axis_index("y"), lax.axis_index("c")
    me = (4 * xc_ + 2 * yc_ + cc_).astype(jnp.int32)
    cw_cols = conv_w.shape[2]

    turned = ('ffn1_w_in', 'ffn2_w_in')

    def local(a, nm):
        return jnp.swapaxes(a[0], 0, 1) if nm in turned else a[0]

    shard = {nm: local(args[nm], nm).astype(BF16) for nm in big}
    cw_pad = jnp.zeros((16, 128), F32).at[0:3, 0:cw_cols].set(conv_w[0])
    wf = shard['ffn1_w_in'].shape[0]

    def tie(chain, *others):
        out = lax.optimization_barrier((chain, *others))
        return out[0], out[1:]

    def gather(arrays, tag, after=()):
        ((_, got, _),) = _run_plans_on_sequencer(
            [_gather_plan(arrays)], lambda x, y, c, chips: [(x, y, 1 - c), (1 - x, y, c), (x, 1 - y, c)],
            name=f"gather_{tag}", collective_id=2, after=after)
        return got

    w1in, cw_g = gather([shard['ffn1_w_in'], cw_pad], "ffn1_in")
    (w1out_g,) = gather([shard['ffn1_w_out']], "ffn1_out")
    (wmix,) = gather([shard['mix_w_in']], "mix_in")
    wco, wglu, wmo_g = gather([shard[nm] for nm in ('conv_w_out', 'ssm_w_glu', 'mix_w_out')], "mix_rest")
    (w2in,) = gather([shard['ffn2_w_in']], "ffn2_in")
    (w2out_g,) = gather([shard['ffn2_w_out']], "ffn2_out")
    wpin, wgate_g = gather([shard['ple_w_in'], shard['ple_w_gate']], "ple")
    cw_full = jnp.transpose(cw_g[:, 0:3, 0:cw_cols], (1, 0, 2)).reshape(3, N_DEV * cw_cols)
    cw8 = jnp.zeros((8, CONV_CH), F32).at[0:3, :].set(cw_full)

    s5_in = (ssm_lam_re[0], ssm_lam_im[0], ssm_log_step[0], ssm_b_re[0], ssm_b_im[0])
    (a_re, a_im, bb_re, bb_im), s5_vjp = jax.vjp(_s5_discretise, *s5_in)
    tab_f = _pow_table(a_re, a_im)
    tab_b = _pow_table(a_re, -a_im, descending=True)
    bmat_b = _compact(jnp.transpose(bb_re, (0, 2, 1)), jnp.transpose(bb_im, (0, 2, 1))).astype(BF16)
    cmat_tb = _compact(ssm_c_re[0], -ssm_c_im[0]).astype(BF16)
    dvec = ssm_d[0].reshape(1, SSM_W)

    xf = x[0]
    x_b = xf.astype(BF16)
    p_b = p[0, 0].astype(BF16)
    tgt = loss_target[0]
    tq = min(512, t)

    ffn_out = dict(ja='c', jb='c', nj=4, tm=tq, tn=d, tk=wf)
    def side_by_side(wb):
        return jnp.transpose(wb, (1, 0, 2)).reshape(wb.shape[1], N_DEV * wb.shape[2])

    tf = min(1024, t)
    a1, h1, _ = _ffn_in(x_b, w1in, name="ffn1_in", tm=tf)
    w1out = w1out_g.reshape(4, wf, d)
    r1, x1, x1b = _mm(a1, w1out, name="ffn1_out", **ffn_out, epilogue=_ln_epilogue(xf, ln1_g, ln1_b, 0.5))
    proj = _mm(x1b, wmix, name="mix_in", jb='b', jo='b', o_flat=True, nj=8, tm=tq, tn=512, tk=d)
    _, (wco, wglu, wmo_g) = tie(x1b, wco, wglu, wmo_g)
    wmo = wmo_g.reshape(d, d)
    ycin = _conv_fwd(proj, cw8, conv_b, name="conv_fwd")
    wco, wglu = side_by_side(wco), side_by_side(wglu)
    yconv = _mm(ycin, wco, name="conv_out", tm=tq, tn=d, tk=CONV_CH)
    s_f, s_b, _ = _scan_fwd(proj, bmat_b, tab_f, name="scan_fwd")
    blk = dict(ja='b', jb='b', jo='b', nj=SCAN_NCB)
    ymm = _mm(s_b, cmat_tb, name="ssm_read", a_flat=True, o_flat=True, tb=True, tm=tq, tn=SCAN_UW, tk=2 * SCAN_CB, **blk)
    ys, sg, u_b = _s5_out(ymm, proj, dvec, name="ssm_out")
    glu = _mm(sg, wglu, name="glu_in", tm=tq, tn=d, tk=SSM_W)
    merged = _gate_fwd(yconv, glu, proj, name="gate_fwd")
    r2, x2, x2b = _mm(merged, wmo, name="mix_out", tm=tq, tn=d, tk=d, epilogue=_ln_epilogue(x1, ln2_g, ln2_b, 1.0))
    a2, h2, _ = _ffn_in(x2b, w2in, name="ffn2_in", tm=tf)
    w2out = w2out_g.reshape(4, wf, d)
    r3, x3, x3b = _mm(a2, w2out, name="ffn2_out", **ffn_out, epilogue=_ln_epilogue(x2, ln3_g, ln3_b, 0.5))
    wgate = wgate_g.reshape(d, d)
    _, (wpin,) = tie(x2b, wpin)
    pe = _mm(p_b, side_by_side(wpin), name="ple_in", tm=tq, tn=d, tk=p_b.shape[1])
    gp = _mm(x3b, wgate, name="ple_gate", tm=tq, tn=d, tk=d)

    dr4, dpe_b, dgp_b, dg4, db4, loss_part = _ln_bwd(x3, [pe, gp], ln4_g, ln4_b, [], name="ple_ln_bwd",
                                                     ple=True, target=tgt)
    gb, sib, rem = {}, {}, {}
    ids = jnp.stack([_dev(1 - xc_, yc_, cc_), _dev(xc_, 1 - yc_, cc_), _dev(1 - xc_, 1 - yc_, cc_)]).astype(jnp.int32)

    def blocked(nm, g):
        return g.reshape((N_DEV,) + args[nm].shape[1:])

    def to_sibling(*names):
        return _sibling_plan([gb[nm] for nm in names])

    def chip_sums(names, sibs):
        for nm, s in zip(names, sibs):
            sib[nm] = s
        return [_chip_partial(gb[nm], sib[nm], ids, name=f"chip_sum_{nm}") for nm in names]

    def on_sequencer(plan, peers, tag, cid, after=()):
        ((_, got, _),) = _run_plans_on_sequencer([plan], peers, name=tag, collective_id=cid, after=after)
        return got

    waiting = []
    arrived = []

    def send_sibling(*names, after=()):
        got = on_sequencer(to_sibling(*names), lambda x, y, c, chips: [(x, y, 1 - c)], f"grad_sibling_{names[0]}", 3,
                           after=after)
        waiting.append((names, got))

    def send_chips(chain, count=None):
        for _ in range(len(waiting) if count is None else count):
            names, got = waiting.pop(0)
            sums = chip_sums(names, got)
            chain, rest = tie(chain, *sums, *arrived)
            r_ = on_sequencer(_chip_plan(list(rest[:len(sums)])), lambda x, y, c, chips: [(*chip, c) for chip in chips],
                              f"grad_chips_{names[0]}", 1)
            arrived[:] = list(r_)
            for i, nm in enumerate(names):
                rem[nm] = r_[3 * i:3 * i + 3]
        return chain

    ffn_in_dg = dict(ja='c', jb='c', nj=8, tm=tq, tn=d, tk=wf)
    ffn_in_wg = dict(ja='b', jo='b', ta=True, nj=8, tm=wf, tn=d, tk=t)
    ffn_out_wg = dict(ja='b', jo='b', ta=True, nj=4, tm=wf, tn=d, tk=t)

    gb['ple_w_in'] = _mm(p_b, dpe_b, name="ple_in_wg", jb='b', jo='b', b_flat=True, ta=True, nj=8,
                         tm=p_b.shape[1], tn=128, tk=t)
    gb['ple_w_gate'] = blocked('ple_w_gate', _mm(x3b, dgp_b, name="ple_gate_wg", ta=True, tm=d, tn=d, tk=t))
    g_ple = ('ple_w_in', 'ple_w_gate')
    dx3_g, (s_,) = _mm(dgp_b, wgate, name="ple_gate_dg", tb=True, tm=tq, tn=d, tk=d, plans=[to_sibling(*g_ple)])
    waiting.append((g_ple, s_))

    dr3, df2_b, dg3, db3 = _ln_bwd(r3, [], ln3_g, ln3_b, [(dr4, ALPHA), (dx3_g, 1.0)], name="ffn2_ln_bwd", fs=0.5)
    df2_b = send_chips(df2_b)
    gb['ffn2_w_out'] = blocked('ffn2_w_out', _mm(a2, df2_b, name="ffn2_out_wg", **ffn_out_wg))
    df2_b, (gb['ffn2_w_out'],) = tie(df2_b, gb['ffn2_w_out'])
    send_sibling('ffn2_w_out')
    dh2, _ = _ffn_out_dg(df2_b, w2out, h2, name="ffn2_out_dg", tm=tf)
    dx2_f = _mm(dh2, w2in, name="ffn2_in_dg", **ffn_in_dg)
    dx2_f = send_chips(dx2_f)
    gb['ffn2_w_in'] = _mm(dh2, x2b, name="ffn2_in_wg", **ffn_in_wg)
    dx2_f, (gb['ffn2_w_in'],) = tie(dx2_f, gb['ffn2_w_in'])

    dr2, dmix_b, dg2, db2 = _ln_bwd(r2, [], ln2_g, ln2_b, [(dr3, ALPHA), (dx2_f, 1.0)], name="mix_ln_bwd")
    dmerged = _mm(dmix_b, wmo, name="mix_out_dg", tb=True, tm=tq, tn=d, tk=d)
    send_sibling('ffn2_w_in', after=(dmerged,))
    gb['mix_w_out'] = blocked('mix_w_out', _mm(merged, dmix_b, name="mix_out_wg", ta=True, tm=d, tn=d, tk=t))
    dyconv_b, dglu_b, dgate_b = _gate_bwd(dmerged, yconv, glu, proj, name="gate_bwd")
    gb['conv_w_out'] = _mm(ycin, dyconv_b, name="conv_out_wg", jb='b', jo='b', b_flat=True, ta=True, nj=8,
                           tm=CONV_CH, tn=128, tk=t)
    dycin = _mm(dyconv_b, wco, name="conv_out_dg", tb=True, tm=tq, tn=CONV_CH, tk=d)
    gb['ssm_w_glu'] = _mm(sg, dglu_b, name="glu_in_wg", jb='b', jo='b', b_flat=True, ta=True, nj=8,
                          tm=SSM_W, tn=256, tk=t)
    dsg = _mm(dglu_b, wglu, name="glu_in_dg", tb=True, tm=tq, tn=SSM_W, tk=2 * d)
    dsg = send_chips(dsg)
    dys, dys_b, dd = _s5_bwd_in(dsg, ys, proj, name="ssm_out_bwd")
    h_b, da_acc, _ = _scan_bwd(dys_b, cmat_tb, s_f, tab_b, name="scan_bwd")
    send_sibling('mix_w_out', 'conv_w_out', 'ssm_w_glu', after=(h_b,))
    dumm = _mm(h_b, bmat_b, name="ssm_write_dg", a_flat=True, o_flat=True, tb=True, tm=tq, tn=SCAN_UW,
               tk=2 * SCAN_CB, **blk)
    du_b = _s5_du(dumm, dys, dvec, name="ssm_du")
    g_bmat = _mm(u_b, h_b, name="ssm_write_wg", a_flat=True, b_flat=True, ta=True, tm=SCAN_UW,
                 tn=2 * SCAN_CB, tk=t, **blk)
    g_cmat = _mm(dys_b, s_b, name="ssm_read_wg", a_flat=True, b_flat=True, ta=True, tm=SCAN_UW,
                 tn=2 * SCAN_CB, tk=t, **blk)
    dcb_b, dcc_b, dch_b, dconv = _conv_bwd(dycin, proj, cw8, conv_b, name="conv_bwd")
    dproj = jnp.concatenate([dcb_b, dcc_b, dch_b, du_b, dgate_b], axis=1)
    dproj = send_chips(dproj)
    gb['mix_w_in'] = _mm(x1b, dproj, name="mix_in_wg", jb='b', jo='b', b_flat=True, ta=True, nj=8,
                         tm=d, tn=512, tk=t)
    send_sibling('mix_w_in')
    dx1_m = _mm(dproj, wmix, name="mix_in_dg", ja='c', jb='c', a_flat=True, tb=True, nj=8, tm=tq, tn=d, tk=512)
    dx1_m, (gb['mix_w_in'],) = tie(dx1_m, gb['mix_w_in'])

    dr1, df1_b, dg1, db1 = _ln_bwd(r1, [], ln1_g, ln1_b, [(dr2, ALPHA), (dx1_m, 1.0)], name="ffn1_ln_bwd", fs=0.5)
    gb['ffn1_w_out'] = blocked('ffn1_w_out', _mm(a1, df1_b, name="ffn1_out_wg", **ffn_out_wg))
    df1_b, (gb['ffn1_w_out'],) = tie(df1_b, gb['ffn1_w_out'])
    send_sibling('ffn1_w_out')
    df1_b = send_chips(df1_b, 1)
    dh1, _ = _ffn_out_dg(df1_b, w1out, h1, name="ffn1_out_dg", tm=tf)
    gb['ffn1_w_in'] = _mm(dh1, x_b, name="ffn1_in_wg", **ffn_in_wg)
    dh1, (gb['ffn1_w_in'],) = tie(dh1, gb['ffn1_w_in'])
    send_sibling('ffn1_w_in')
    dh1 = send_chips(dh1, 1)
    (grad_x,) = _mm(dh1, w1in, name="ffn1_in_dg", **ffn_in_dg,
                    epilogue=(lambda pr, drv: (pr + ALPHA * drv,), (dr1,), (), (F32,)))

    da_sum = jnp.sum(da_acc, axis=0)
    da_re, da_im = _unperm_cols(da_sum)
    gbb_re, gbb_im = [jnp.transpose(v, (0, 2, 1)) for v in _compact_extract(g_bmat, SSM_GROUP)]
    g_c_re, g_c_im_neg = _compact_extract(g_cmat, SSM_GROUP)
    g_c_im = -g_c_im_neg
    g_lam_re, g_lam_im, g_log_step, g_b_re, g_b_im = s5_vjp(
        (da_re.reshape(SSM_GROUPS, SSM_STATE), da_im.reshape(SSM_GROUPS, SSM_STATE), gbb_re, gbb_im))
    g_d = dd.reshape(SSM_GROUPS, SSM_GROUP)

    small_g = {'ln1_g': dg1, 'ln1_b': db1, 'conv_b': dconv[3:4], 'ssm_lam_re': g_lam_re, 'ssm_lam_im': g_lam_im,
               'ssm_log_step': g_log_step, 'ssm_b_re': g_b_re, 'ssm_b_im': g_b_im, 'ssm_c_re': g_c_re,
               'ssm_c_im': g_c_im, 'ssm_d': g_d, 'ln2_g': dg2, 'ln2_b': db2, 'ln3_g': dg3, 'ln3_b': db3,
               'ln4_g': dg4, 'ln4_b': db4}
    small_shapes = [args[nm].shape for nm in small] + [(3, CONV_CH), (1,)]
    g_pack = _pack([small_g[nm] for nm in small] + [dconv[0:3], loss_part[0:1, 0:1]])

    res = {}
    me1, _ = tie(me.reshape(1), grad_x)
    me1 = send_chips(me1, 1)

    def update(nm):
        upd = _shard_update(gb[nm], sib[nm], rem[nm], me1, local(args[nm], nm), local(args['m_' + nm], nm),
                            local(args['v_' + nm], nm), name=f"update_{nm}")
        for key, val in zip(('grad_', 'delta_', 'new_m_', 'new_v_'), upd):
            res[key + nm] = (jnp.swapaxes(val, 0, 1) if nm in turned else val)[None]

    (g_all,) = gather([g_pack], "small", after=(dh1,))
    last = ['ffn1_w_out', 'ffn1_w_in']
    for nm in big:
        if nm not in last:
            update(nm)

    def full_cw(a):
        return lax.dynamic_update_slice(jnp.zeros((3, CONV_CH), F32), a[0], (0, me * cw_cols))

    zero1 = jnp.zeros((1,), F32)
    w_pack = _pack([args[nm] for nm in small] + [full_cw(conv_w), zero1])
    m_pack = _pack([args['m_' + nm] for nm in small] + [full_cw(m_conv_w), zero1])
    v_pack = _pack([args['v_' + nm] for nm in small] + [full_cw(v_conv_w), zero1])
    sg_sum, sd, sm, sv = _small_update(g_all, w_pack, m_pack, v_pack, name="small_update")
    for key, buf in (('grad_', sg_sum), ('delta_', sd), ('new_m_', sm), ('new_v_', sv)):
        parts = _unpack(buf, small_shapes)
        for nm, val in zip(small, parts[:len(small)]):
            res[key + nm] = val
        res[key + 'conv_w'] = lax.dynamic_slice(parts[len(small)], (0, me * cw_cols), (3, cw_cols))[None]
        if key == 'grad_':
            loss = parts[-1][0]

    for nm in last:
        update(nm)

    outs = [loss, grad_x[None]]
    for key in ('grad_', 'delta_', 'new_m_', 'new_v_'):
        outs += [res[key + nm] for nm in order]
    return tuple(outs)
```

```python
import functools
import math

import jax
import jax.numpy as jnp
from jax import lax
from jax.experimental import pallas as pl
from jax.experimental.pallas import tpu as pltpu
from jax.experimental.pallas import tpu_sc as plsc

F32 = jnp.float32
BF16 = jnp.bfloat16
MESH = pl.DeviceIdType.MESH

N_DEV = 8
ALPHA = 2.0 ** 0.25
LN_EPS = 1e-5
CONV_CH = 512
SSM_W = 512
SSM_GROUPS = 32
SSM_GROUP = 16
SSM_STATE = 64
SSM_CH = SSM_GROUPS * SSM_STATE
SCAN_CB = 512
SCAN_NCB = SSM_CH // SCAN_CB
SCAN_GPB = SSM_GROUPS // SCAN_NCB
SCAN_UW = SCAN_GPB * SSM_GROUP
SCAN_TT = 256
ADAM_LR = 0.001
ADAM_B1 = 0.9
ADAM_B2 = 0.999
ADAM_EPS = 1e-08
ADAM_WD = 0.01
ADAM_STEP = 10
VMEM_LIMIT = 56 * 1024 * 1024


def _cparams(sem=None, **kw):
    return pltpu.CompilerParams(dimension_semantics=sem, vmem_limit_bytes=VMEM_LIMIT, **kw)


def _mm(a, b, *, name, ja=None, jb=None, jo=None, a_flat=False, b_flat=False, o_flat=False,
        ta=False, tb=False, tm, tn, tk, nj=1, out_dtype=F32, plans=(), epilogue=None):
    def dims(arr, j, flat):
        if j is None:
            return arr.shape
        if flat:
            return (arr.shape[0], arr.shape[1] // nj)
        assert arr.shape[0] == nj, (name, arr.shape, nj)
        return arr.shape[1:]

    ar, ac = dims(a, ja, a_flat)
    br, bc = dims(b, jb, b_flat)
    m, k = (ac, ar) if ta else (ar, ac)
    k2, n = (bc, br) if tb else (br, bc)
    assert k == k2, (name, a.shape, b.shape)
    assert m % tm == 0 and n % tn == 0 and k % tk == 0, (name, m, n, k, tm, tn, tk)
    njb = nj if 'b' in (ja, jb) else 1
    njc = nj if 'c' in (ja, jb) else 1
    nk = k // tk
    j_inside = njc > 1 and nk == 1 and not ta
    n_in = njc if j_inside else 1
    nred = nk if j_inside else njc * nk
    grid = (njb, m // tm, n // tn, 1 if j_inside else njc, nk)

    def make_spec(j, flat, blk, rfn, cfn, cols_per_j):
        def jsel(g, c):
            return g if j == 'b' else c
        if j is None:
            return pl.BlockSpec(blk, lambda g, i, jn, c, kk: (rfn(i, jn, kk), cfn(i, jn, kk)))
        if j == 'c' and j_inside:
            if flat:
                return pl.BlockSpec((blk[0], nj * blk[1]), lambda g, i, jn, c, kk: (rfn(i, jn, kk), 0))
            return pl.BlockSpec((nj,) + blk, lambda g, i, jn, c, kk: (0, rfn(i, jn, kk), cfn(i, jn, kk)))
        if flat:
            nb = cols_per_j // blk[1]
            return pl.BlockSpec(blk, lambda g, i, jn, c, kk: (rfn(i, jn, kk), jsel(g, c) * nb + cfn(i, jn, kk)))
        return pl.BlockSpec((None,) + blk,
                            lambda g, i, jn, c, kk: (jsel(g, c), rfn(i, jn, kk), cfn(i, jn, kk)))

    if ta:
        a_spec = make_spec(ja, a_flat, (tk, tm), lambda i, jn, kk: kk, lambda i, jn, kk: i, ac)
    else:
        a_spec = make_spec(ja, a_flat, (tm, tk), lambda i, jn, kk: i, lambda i, jn, kk: kk, ac)
    if tb:
        b_spec = make_spec(jb, b_flat, (tn, tk), lambda i, jn, kk: jn, lambda i, jn, kk: kk, bc)
    else:
        b_spec = make_spec(jb, b_flat, (tk, tn), lambda i, jn, kk: kk, lambda i, jn, kk: jn, bc)
    o_spec = make_spec(jo, o_flat, (tm, tn), lambda i, jn, kk: i, lambda i, jn, kk: jn, n)
    if jo is None:
        out_shape = (m, n)
    elif o_flat:
        out_shape = (m, nj * n)
    else:
        out_shape = (nj, m, n)

    dn = (((0 if ta else 1,), (1 if tb else 0,)), ((), ()))

    def operand(ref, j, flat, jj, width):
        if not (j == 'c' and j_inside):
            return ref[...]
        return ref[:, jj * width:(jj + 1) * width] if flat else ref[jj]

    e_fn, e_rows, e_vecs, e_dtypes = epilogue if epilogue else (None, (), (), (out_dtype,))
    assert not epilogue or (nred == 1 and jo is None), name
    n_e = len(e_rows) + len(e_vecs)
    n_o = len(e_dtypes)

    def body(a_ref, b_ref, *rest):
        e_refs, o_refs, scratch = rest[:n_e], rest[n_e:n_e + n_o], rest[n_e + n_o:]
        o_ref = o_refs[0]
        p = None
        for jj in range(n_in):
            q = lax.dot_general(operand(a_ref, ja, a_flat, jj, tk), operand(b_ref, jb, b_flat, jj, tk if tb else tn),
                                dn, preferred_element_type=F32)
            p = q if p is None else p + q
        if epilogue:
            for ref, val in zip(o_refs, e_fn(p, *[r[...] for r in e_refs])):
                ref[...] = val.astype(ref.dtype)
        elif nred == 1:
            o_ref[...] = p.astype(o_ref.dtype)
        else:
            acc = scratch[0]
            r = pl.program_id(3) * nk + pl.program_id(4)

            @pl.when(r == 0)
            def _():
                acc[...] = p

            @pl.when(r > 0)
            def _():
                acc[...] += p

            @pl.when(r == nred - 1)
            def _():
                o_ref[...] = acc[...].astype(o_ref.dtype)

    vec_spec = pl.BlockSpec((1, tn), lambda g, i, jn, c, kk: (0, jn))
    res = _call_with_plans(
        body, plans, name=name, grid=grid,
        in_specs=[a_spec, b_spec] + [o_spec] * len(e_rows) + [vec_spec] * len(e_vecs), out_specs=[o_spec] * n_o,
        out_shape=[jax.ShapeDtypeStruct(out_shape, dt) for dt in e_dtypes],
        scratch_shapes=[] if nred == 1 else [pltpu.VMEM((tm, tn), F32)],
        semantics=("parallel", "parallel", "parallel", "arbitrary", "arbitrary"), operands=(a, b, *e_rows, *e_vecs))
    outs = res[0] if epilogue else res[0][0]
    return (outs, res[1]) if plans else outs


def _sigmoid(v):
    return jax.nn.sigmoid(v)


def _row_spec(tm, cols, colblk=0):
    return pl.BlockSpec((tm, cols), lambda i: (i, colblk))


def _vec_spec(cols):
    return pl.BlockSpec((1, cols), lambda i: (0, 0))


def _ffn_in(xb, win, *, name, tm, plans=()):
    t, d = xb.shape
    nj, w, _ = win.shape
    half = nj // 2
    dn = (((1,), (1,)), ((), ()))

    def body(x_ref, wg_ref, wu_ref, a_ref, gu_ref):
        xv = x_ref[...]
        g = lax.dot_general(xv, wg_ref[...], dn, preferred_element_type=F32)
        u = lax.dot_general(xv, wu_ref[...], dn, preferred_element_type=F32)
        a_ref[...] = (g * _sigmoid(g) * u).astype(BF16)
        gu_ref[0] = g.astype(BF16)
        gu_ref[1] = u.astype(BF16)

    (a, gu), riders = _call_with_plans(
        body, plans, name=name, grid=(half, t // tm),
        in_specs=[pl.BlockSpec((tm, d), lambda j, i: (i, 0)),
                  pl.BlockSpec((None, w, d), lambda j, i: (j, 0, 0)),
                  pl.BlockSpec((None, w, d), lambda j, i: (j + half, 0, 0))],
        out_specs=[pl.BlockSpec((None, tm, w), lambda j, i: (j, i, 0)),
                   pl.BlockSpec((2, None, tm, w), lambda j, i: (0, j, i, 0))],
        out_shape=[jax.ShapeDtypeStruct((half, t, w), BF16), jax.ShapeDtypeStruct((2, half, t, w), BF16)],
        scratch_shapes=[], semantics=("parallel", "parallel"), operands=(xb, win, win))
    return a, gu, riders


def _ffn_out_dg(dfb, wout, gu, *, name, tm, plans=()):
    t, d = dfb.shape
    half, w, _ = wout.shape
    dn = (((1,), (1,)), ((), ()))

    def body(df_ref, w_ref, gu_ref, dh_ref):
        da = lax.dot_general(df_ref[...], w_ref[...], dn, preferred_element_type=F32)
        g = gu_ref[0].astype(F32)
        u = gu_ref[1].astype(F32)
        sg = _sigmoid(g)
        dh_ref[0] = (da * u * (sg * (1.0 + g * (1.0 - sg)))).astype(BF16)
        dh_ref[1] = (da * (g * sg)).astype(BF16)

    (out,), riders = _call_with_plans(
        body, plans, name=name, grid=(half, t // tm),
        in_specs=[pl.BlockSpec((tm, d), lambda j, i: (i, 0)),
                  pl.BlockSpec((None, w, d), lambda j, i: (j, 0, 0)),
                  pl.BlockSpec((2, None, tm, w), lambda j, i: (0, j, i, 0))],
        out_specs=[pl.BlockSpec((2, None, tm, w), lambda j, i: (0, j, i, 0))],
        out_shape=[jax.ShapeDtypeStruct((2, half, t, w), BF16)],
        scratch_shapes=[], semantics=("parallel", "parallel"), operands=(dfb, wout, gu))
    return out.reshape(2 * half, t, w), riders


def _ln_stats(r):
    mu = jnp.mean(r, axis=-1, keepdims=True)
    xc = r - mu
    var = jnp.mean(xc * xc, axis=-1, keepdims=True)
    rstd = lax.rsqrt(var + LN_EPS)
    return xc * rstd, rstd


def _ln_epilogue(xin, gamma, beta, fs):
    def fn(p, xv, g, b):
        r = ALPHA * xv + fs * p
        xh, _ = _ln_stats(r)
        y = xh * g + b
        return r, y, y

    return fn, (xin,), (gamma, beta), (F32, F32, BF16)


def _ln_bwd(xin, fparts, gamma, beta, grads, *, name, fs=1.0, ple=False, target=None, tm=512):
    t, d = xin.shape
    nf = len(fparts)
    ng = len(grads)
    coefs = [c for _, c in grads]
    use_t = target is not None
    n_fout = 2 if ple else 1

    def body(*refs):
        pos = 0
        x_ref = refs[pos]; pos += 1
        f_refs = refs[pos:pos + nf]; pos += nf
        g_ref, b_ref = refs[pos:pos + 2]; pos += 2
        gr_refs = refs[pos:pos + ng]; pos += ng
        if use_t:
            t_ref = refs[pos]; pos += 1
        dr_ref = refs[pos]; pos += 1
        fo_refs = refs[pos:pos + n_fout]; pos += n_fout
        dg_ref, db_ref = refs[pos:pos + 2]; pos += 2
        if use_t:
            loss_ref = refs[pos]; pos += 1
        i = pl.program_id(0)

        if ple:
            pe = f_refs[0][...]
            sg = _sigmoid(f_refs[1][...])
            resid = ALPHA * x_ref[...] + pe * sg
        else:
            resid = x_ref[...]
        xh, rstd = _ln_stats(resid)
        gam = g_ref[...]
        if use_t:
            diff = xh * gam + b_ref[...] - t_ref[...]
            dy = diff * (1.0 / d)
            lpart = 0.5 * jnp.sum(jnp.sum(diff * diff, axis=-1, keepdims=True), axis=0, keepdims=True) * (1.0 / d)
        else:
            dy = coefs[0] * gr_refs[0][...]
            for c, r in zip(coefs[1:], gr_refs[1:]):
                dy = dy + c * r[...]
        dxh = dy * gam
        m1 = jnp.mean(dxh, axis=-1, keepdims=True)
        m2 = jnp.mean(dxh * xh, axis=-1, keepdims=True)
        dr = rstd * (dxh - m1 - xh * m2)
        dr_ref[...] = dr
        if ple:
            fo_refs[0][...] = (dr * sg).astype(BF16)
            fo_refs[1][...] = (dr * pe * (sg * (1.0 - sg))).astype(BF16)
        else:
            fo_refs[0][...] = (fs * dr).astype(BF16)
        dgp = jnp.sum(dy * xh, axis=0, keepdims=True)
        dbp = jnp.sum(dy, axis=0, keepdims=True)

        @pl.when(i == 0)
        def _():
            dg_ref[...] = dgp
            db_ref[...] = dbp
            if use_t:
                loss_ref[...] = jnp.broadcast_to(lpart, loss_ref.shape)

        @pl.when(i > 0)
        def _():
            dg_ref[...] += dgp
            db_ref[...] += dbp
            if use_t:
                loss_ref[...] += jnp.broadcast_to(lpart, loss_ref.shape)

    ins = [xin, *fparts, gamma, beta, *[g for g, _ in grads]] + ([target] if use_t else [])
    in_specs = ([_row_spec(tm, d)] * (1 + nf) + [_vec_spec(d), _vec_spec(d)] + [_row_spec(tm, d)] * ng
                + ([_row_spec(tm, d)] if use_t else []))
    out_specs = [_row_spec(tm, d)] * (1 + n_fout) + [_vec_spec(d), _vec_spec(d)] + ([_vec_spec(128)] if use_t else [])
    out_shape = ([jax.ShapeDtypeStruct((t, d), F32)] + [jax.ShapeDtypeStruct((t, d), BF16)] * n_fout
                 + [jax.ShapeDtypeStruct((1, d), F32)] * 2 + ([jax.ShapeDtypeStruct((1, 128), F32)] if use_t else []))
    return pl.pallas_call(
        body, name=name, grid=(t // tm,), in_specs=in_specs, out_specs=out_specs, out_shape=out_shape,
        compiler_params=_cparams(("arbitrary",)),
    )(*ins)


def _conv_fwd(proj, cw, cb, *, name, tm=512):
    t = proj.shape[0]
    c = CONV_CH
    hb = tm // 8

    def body(b_ref, c_ref, h_ref, cp_ref, hp_ref, w_ref, bias_ref, o_ref, q_scr):
        i = pl.program_id(0)
        q = c_ref[...] * h_ref[...]
        halo = jnp.where(i > 0, cp_ref[...] * hp_ref[...], 0.0)
        q_scr[0:8, :] = halo
        q_scr[8:, :] = q
        z = (w_ref[2:3, :] * q + w_ref[1:2, :] * q_scr[pl.ds(7, tm), :] + w_ref[0:1, :] * q_scr[pl.ds(6, tm), :]
             + bias_ref[...])
        o_ref[...] = (b_ref[...] * z).astype(BF16)

    prev = lambda blk: pl.BlockSpec((8, c), lambda i: (jnp.maximum(i * hb - 1, 0), blk))
    return pl.pallas_call(
        body, name=name, grid=(t // tm,),
        in_specs=[_row_spec(tm, c, 0), _row_spec(tm, c, 1), _row_spec(tm, c, 2), prev(1), prev(2),
                  pl.BlockSpec((8, c), lambda i: (0, 0)), _vec_spec(c)],
        out_specs=_row_spec(tm, c),
        out_shape=jax.ShapeDtypeStruct((t, c), BF16),
        scratch_shapes=[pltpu.VMEM((tm + 8, c), F32)],
        compiler_params=_cparams(("parallel",)),
    )(proj, proj, proj, proj, proj, cw, cb)


def _conv_bwd(dyc, proj, cw, cb, *, name, tm=512):
    t = proj.shape[0]
    c = CONV_CH
    hb = tm // 8
    nblk = t // 8

    def body(d_ref, b_ref, c_ref, h_ref, cp_ref, hp_ref, dn_ref, bn_ref, w_ref, bias_ref,
             db_ref, dc_ref, dh_ref, dw_ref, q_scr, z_scr):
        i = pl.program_id(0)
        last = pl.num_programs(0) - 1
        cc = c_ref[...]
        ch = h_ref[...]
        q = cc * ch
        halo = jnp.where(i > 0, cp_ref[...] * hp_ref[...], 0.0)
        q_scr[0:8, :] = halo
        q_scr[8:, :] = q
        w0, w1, w2 = w_ref[0:1, :], w_ref[1:2, :], w_ref[2:3, :]
        qm1 = q_scr[pl.ds(7, tm), :]
        qm2 = q_scr[pl.ds(6, tm), :]
        z = w2 * q + w1 * qm1 + w0 * qm2 + bias_ref[...]
        d = d_ref[...]
        bb = b_ref[...]
        db_ref[...] = (d * z).astype(BF16)
        dz = d * bb
        z_scr[0:tm, :] = dz
        z_scr[tm:, :] = jnp.where(i < last, dn_ref[...] * bn_ref[...], 0.0)
        dq = w2 * dz + w1 * z_scr[pl.ds(1, tm), :] + w0 * z_scr[pl.ds(2, tm), :]
        dc_ref[...] = (dq * ch).astype(BF16)
        dh_ref[...] = (dq * cc).astype(BF16)
        row = lax.broadcasted_iota(jnp.int32, (8, c), 0)
        part = jnp.zeros((8, c), F32)
        for k, term in enumerate((dz * qm2, dz * qm1, dz * q, dz)):
            part = jnp.where(row == k, jnp.sum(term, axis=0, keepdims=True), part)

        @pl.when(i == 0)
        def _():
            dw_ref[...] = part

        @pl.when(i > 0)
        def _():
            dw_ref[...] += part

    prev = lambda blk: pl.BlockSpec((8, c), lambda i: (jnp.maximum(i * hb - 1, 0), blk))
    nxt_p = pl.BlockSpec((8, c), lambda i: (jnp.minimum((i + 1) * hb, nblk - 1), 0))
    nxt_d = pl.BlockSpec((8, c), lambda i: (jnp.minimum((i + 1) * hb, nblk - 1), 0))
    return pl.pallas_call(
        body, name=name, grid=(t // tm,),
        in_specs=[_row_spec(tm, c), _row_spec(tm, c, 0), _row_spec(tm, c, 1), _row_spec(tm, c, 2),
                  prev(1), prev(2), nxt_d, nxt_p, pl.BlockSpec((8, c), lambda i: (0, 0)), _vec_spec(c)],
        out_specs=[_row_spec(tm, c)] * 3 + [pl.BlockSpec((8, c), lambda i: (0, 0))],
        out_shape=[jax.ShapeDtypeStruct((t, c), BF16)] * 3 + [jax.ShapeDtypeStruct((8, c), F32)],
        scratch_shapes=[pltpu.VMEM((tm + 8, c), F32), pltpu.VMEM((tm + 8, c), F32)],
        compiler_params=_cparams(("arbitrary",)),
    )(dyc, proj, proj, proj, proj, proj, dyc, proj, cw, cb)


def _gate_fwd(yconv, glu, proj, *, name, tm=512):
    t, d = yconv.shape

    def body(yc_ref, ga_ref, gb_ref, gc_ref, gs_ref, o_ref):
        yssm = ga_ref[...] * _sigmoid(gb_ref[...])
        o_ref[...] = (_sigmoid(gc_ref[...]) * yc_ref[...] + _sigmoid(gs_ref[...]) * yssm).astype(BF16)

    return pl.pallas_call(
        body, name=name, grid=(t // tm,),
        in_specs=[_row_spec(tm, d), _row_spec(tm, d, 0), _row_spec(tm, d, 1), _row_spec(tm, d, 2), _row_spec(tm, d, 3)],
        out_specs=_row_spec(tm, d), out_shape=jax.ShapeDtypeStruct((t, d), BF16),
        compiler_params=_cparams(("parallel",)),
    )(yconv, glu, glu, proj, proj)


def _gate_bwd(dm, yconv, glu, proj, *, name, tm=512):
    t, d = yconv.shape

    def body(dm_ref, yc_ref, ga_ref, gb_ref, gc_ref, gs_ref, dyc_ref, dglu_ref, dgate_ref):
        dmv = dm_ref[...]
        sc = _sigmoid(gc_ref[...])
        ss = _sigmoid(gs_ref[...])
        sb = _sigmoid(gb_ref[...])
        ga = ga_ref[...]
        yssm = ga * sb
        dyc_ref[...] = (dmv * sc).astype(BF16)
        dgate_ref[:, 0:d] = (dmv * yc_ref[...] * (sc * (1.0 - sc))).astype(BF16)
        dys = dmv * ss
        dgate_ref[:, d:2 * d] = (dmv * yssm * (ss * (1.0 - ss))).astype(BF16)
        dglu_ref[:, 0:d] = (dys * sb).astype(BF16)
        dglu_ref[:, d:2 * d] = (dys * ga * (sb * (1.0 - sb))).astype(BF16)

    return pl.pallas_call(
        body, name=name, grid=(t // tm,),
        in_specs=[_row_spec(tm, d), _row_spec(tm, d), _row_spec(tm, d, 0), _row_spec(tm, d, 1),
                  _row_spec(tm, d, 2), _row_spec(tm, d, 3)],
        out_specs=[_row_spec(tm, d), _row_spec(tm, 2 * d), _row_spec(tm, 2 * d)],
        out_shape=[jax.ShapeDtypeStruct((t, d), BF16), jax.ShapeDtypeStruct((t, 2 * d), BF16),
                   jax.ShapeDtypeStruct((t, 2 * d), BF16)],
        compiler_params=_cparams(("parallel",)),
    )(dm, yconv, glu, glu, proj, proj)


_GELU_C = math.sqrt(2.0 / math.pi)


def _gelu(v):
    return 0.5 * v * (1.0 + jnp.tanh(_GELU_C * (v + 0.044715 * v * v * v)))


def _gelu_grad(v):
    th = jnp.tanh(_GELU_C * (v + 0.044715 * v * v * v))
    return 0.5 * (1.0 + th) + 0.5 * v * (1.0 - th * th) * (_GELU_C * (1.0 + 3.0 * 0.044715 * v * v))


def _cmul(ar, ai, br, bi):
    return ar * br - ai * bi, ar * bi + ai * br


def _scan_fwd(proj, bmat, tab, *, name, plans=()):
    t = proj.shape[0]
    tt, cbw = SCAN_TT, SCAN_CB
    w2 = 2 * cbw

    def body(u_ref, b_ref, tab_ref, s_ref, sb_ref, bu_scr, carry):
        ti = pl.program_id(1)

        @pl.when(ti == 0)
        def _():
            carry[...] = jnp.zeros_like(carry)

        bu_scr[...] = jnp.dot(u_ref[...].astype(BF16), b_ref[...], preferred_element_type=F32)
        row = lax.broadcasted_iota(jnp.int32, (8, cbw), 0)

        def blk(bi, c):
            cr, ci = c
            r0 = pl.multiple_of(bi * 8, 8)
            xr = bu_scr[pl.ds(r0, 8), 0:cbw]
            xi = bu_scr[pl.ds(r0, 8), cbw:w2]
            for k, sh in enumerate((1, 2, 4)):
                kr = tab_ref[k:k + 1, 0:cbw]
                ki = tab_ref[k:k + 1, cbw:w2]
                sr = jnp.where(row >= sh, pltpu.roll(xr, sh, 0), 0.0)
                si = jnp.where(row >= sh, pltpu.roll(xi, sh, 0), 0.0)
                pr, pi = _cmul(kr, ki, sr, si)
                xr = xr + pr
                xi = xi + pi
            pr, pi = _cmul(tab_ref[8:16, 0:cbw], tab_ref[8:16, cbw:w2], cr, ci)
            xr = xr + pr
            xi = xi + pi
            s_ref[pl.ds(r0, 8), 0:cbw] = xr
            s_ref[pl.ds(r0, 8), cbw:w2] = xi
            return (jnp.broadcast_to(xr[7:8, :], (8, cbw)), jnp.broadcast_to(xi[7:8, :], (8, cbw)))

        cr, ci = lax.fori_loop(0, tt // 8, blk, (carry[:, 0:cbw], carry[:, cbw:w2]))
        carry[:, 0:cbw] = cr
        carry[:, cbw:w2] = ci
        sb_ref[...] = s_ref[...].astype(BF16)

    (s, sb), riders = _call_with_plans(
        body, plans, name=name, grid=(SCAN_NCB, t // tt),
        in_specs=[pl.BlockSpec((tt, SCAN_UW), lambda cb, ti: (ti, 3 * SCAN_NCB + cb)),
                  pl.BlockSpec((None, SCAN_UW, w2), lambda cb, ti: (cb, 0, 0)),
                  pl.BlockSpec((16, w2), lambda cb, ti: (0, cb))],
        out_specs=[pl.BlockSpec((tt, w2), lambda cb, ti: (ti, cb))] * 2,
        out_shape=[jax.ShapeDtypeStruct((t, 2 * SSM_CH), F32), jax.ShapeDtypeStruct((t, 2 * SSM_CH), BF16)],
        scratch_shapes=[pltpu.VMEM((tt, w2), F32), pltpu.VMEM((8, w2), F32)],
        semantics=("parallel", "arbitrary"), operands=(proj, bmat, tab))
    return s, sb, riders


def _scan_bwd(dyb, cmat_t, s, tabb, *, name, plans=()):
    t = s.shape[0]
    tt, cbw = SCAN_TT, SCAN_CB
    w2 = 2 * cbw
    nt = t // tt
    hb = tt // 8

    def body(dy_ref, c_ref, s_ref, sp_ref, tab_ref, h_ref, da_ref, g_scr, s_scr, carry):
        ti = pl.program_id(1)

        @pl.when(ti == 0)
        def _():
            carry[...] = jnp.zeros_like(carry)
            da_ref[...] = jnp.zeros_like(da_ref)

        g_scr[...] = jnp.dot(dy_ref[...], c_ref[...], preferred_element_type=F32)
        s_scr[0:8, :] = jnp.where(ti < nt - 1, sp_ref[...], 0.0)
        s_scr[8:, :] = s_ref[...]
        row = lax.broadcasted_iota(jnp.int32, (8, cbw), 0)

        def blk(k, c):
            cr, ci, ar, ai = c
            bi = hb - 1 - k
            r0 = pl.multiple_of(bi * 8, 8)
            xr = g_scr[pl.ds(r0, 8), 0:cbw]
            xi = g_scr[pl.ds(r0, 8), cbw:w2]
            for j, sh in enumerate((1, 2, 4)):
                kr = tab_ref[j:j + 1, 0:cbw]
                ki = tab_ref[j:j + 1, cbw:w2]
                sr = jnp.where(row < 8 - sh, pltpu.roll(xr, 8 - sh, 0), 0.0)
                si = jnp.where(row < 8 - sh, pltpu.roll(xi, 8 - sh, 0), 0.0)
                pr, pi = _cmul(kr, ki, sr, si)
                xr = xr + pr
                xi = xi + pi
            pr, pi = _cmul(tab_ref[8:16, 0:cbw], tab_ref[8:16, cbw:w2], cr, ci)
            xr = xr + pr
            xi = xi + pi
            h_ref[pl.ds(r0, 8), 0:cbw] = xr.astype(BF16)
            h_ref[pl.ds(r0, 8), cbw:w2] = xi.astype(BF16)
            pvr = s_scr[pl.ds(r0, 8), 0:cbw]
            pvi = s_scr[pl.ds(r0, 8), cbw:w2]
            cur_r = s_scr[pl.ds(r0 + 8, 8), 0:cbw]
            cur_i = s_scr[pl.ds(r0 + 8, 8), cbw:w2]
            spr = jnp.where(row == 0, jnp.broadcast_to(pvr[7:8, :], (8, cbw)), pltpu.roll(cur_r, 1, 0))
            spi = jnp.where(row == 0, jnp.broadcast_to(pvi[7:8, :], (8, cbw)), pltpu.roll(cur_i, 1, 0))
            ar = ar + spr * xr + spi * xi
            ai = ai + spr * xi - spi * xr
            return (jnp.broadcast_to(xr[0:1, :], (8, cbw)), jnp.broadcast_to(xi[0:1, :], (8, cbw)), ar, ai)

        z = jnp.zeros((8, cbw), F32)
        cr, ci, ar, ai = lax.fori_loop(0, hb, blk, (carry[:, 0:cbw], carry[:, cbw:w2], z, z))
        carry[:, 0:cbw] = cr
        carry[:, cbw:w2] = ci
        da_ref[:, 0:cbw] += ar
        da_ref[:, cbw:w2] += ai

    rt = lambda ti: nt - 1 - ti
    (h, da), riders = _call_with_plans(
        body, plans, name=name, grid=(SCAN_NCB, nt),
        in_specs=[pl.BlockSpec((tt, SCAN_UW), lambda cb, ti: (rt(ti), cb)),
                  pl.BlockSpec((None, SCAN_UW, w2), lambda cb, ti: (cb, 0, 0)),
                  pl.BlockSpec((tt, w2), lambda cb, ti: (rt(ti), cb)),
                  pl.BlockSpec((8, w2), lambda cb, ti: (jnp.maximum(rt(ti) * hb - 1, 0), cb)),
                  pl.BlockSpec((16, w2), lambda cb, ti: (0, cb))],
        out_specs=[pl.BlockSpec((tt, w2), lambda cb, ti: (rt(ti), cb)),
                   pl.BlockSpec((8, w2), lambda cb, ti: (0, cb))],
        out_shape=[jax.ShapeDtypeStruct((t, 2 * SSM_CH), BF16), jax.ShapeDtypeStruct((8, 2 * SSM_CH), F32)],
        scratch_shapes=[pltpu.VMEM((tt, w2), F32), pltpu.VMEM((tt + 8, w2), F32), pltpu.VMEM((8, w2), F32)],
        semantics=("parallel", "arbitrary"), operands=(dyb, cmat_t, s, s, tabb))
    return h, da, riders


def _s5_out(ymm, proj, dvec, *, name, tm=512):
    t, w = ymm.shape

    def body(y_ref, u_ref, d_ref, yo_ref, sg_ref, ub_ref):
        u = u_ref[...]
        y = y_ref[...] + d_ref[...] * u
        yo_ref[...] = y
        sg_ref[...] = _gelu(y).astype(BF16)
        ub_ref[...] = u.astype(BF16)

    return pl.pallas_call(
        body, name=name, grid=(t // tm,),
        in_specs=[_row_spec(tm, w), _row_spec(tm, w, 3), _vec_spec(w)],
        out_specs=[_row_spec(tm, w)] * 3,
        out_shape=[jax.ShapeDtypeStruct((t, w), F32), jax.ShapeDtypeStruct((t, w), BF16), jax.ShapeDtypeStruct((t, w), BF16)],
        compiler_params=_cparams(("parallel",)),
    )(ymm, proj, dvec)


def _s5_bwd_in(dsg, y, proj, *, name, tm=512):
    t, w = y.shape

    def body(d_ref, y_ref, u_ref, dy_ref, dyb_ref, dd_ref):
        i = pl.program_id(0)
        dy = d_ref[...] * _gelu_grad(y_ref[...])
        dy_ref[...] = dy
        dyb_ref[...] = dy.astype(BF16)
        part = jnp.sum(dy * u_ref[...], axis=0, keepdims=True)

        @pl.when(i == 0)
        def _():
            dd_ref[...] = part

        @pl.when(i > 0)
        def _():
            dd_ref[...] += part

    return pl.pallas_call(
        body, name=name, grid=(t // tm,),
        in_specs=[_row_spec(tm, w), _row_spec(tm, w), _row_spec(tm, w, 3)],
        out_specs=[_row_spec(tm, w), _row_spec(tm, w), _vec_spec(w)],
        out_shape=[jax.ShapeDtypeStruct((t, w), F32), jax.ShapeDtypeStruct((t, w), BF16), jax.ShapeDtypeStruct((1, w), F32)],
        compiler_params=_cparams(("arbitrary",)),
    )(dsg, y, proj)


def _s5_du(dumm, dy, dvec, *, name, tm=512):
    t, w = dy.shape

    def body(a_ref, dy_ref, d_ref, o_ref):
        o_ref[...] = (a_ref[...] + d_ref[...] * dy_ref[...]).astype(BF16)

    return pl.pallas_call(
        body, name=name, grid=(t // tm,), in_specs=[_row_spec(tm, w), _row_spec(tm, w), _vec_spec(w)],
        out_specs=_row_spec(tm, w), out_shape=jax.ShapeDtypeStruct((t, w), BF16),
        compiler_params=_cparams(("parallel",)),
    )(dumm, dy, dvec)


def _s5_discretise(lam_re, lam_im, log_step, b_re, b_im):
    lam = lax.complex(lam_re, lam_im)
    dt = jnp.exp(log_step)[:, None]
    a = jnp.exp(lam * dt)
    bbar = ((a - 1.0) / lam)[..., None] * lax.complex(b_re, b_im)
    return jnp.real(a), jnp.imag(a), jnp.real(bbar), jnp.imag(bbar)


def _perm_cols(re, im):
    lead = re.shape[:-1]
    r = re.reshape(lead + (SCAN_NCB, 1, SCAN_CB))
    i = im.reshape(lead + (SCAN_NCB, 1, SCAN_CB))
    return jnp.concatenate([r, i], axis=-2).reshape(lead + (2 * SSM_CH,))


def _unperm_cols(x):
    lead = x.shape[:-1]
    y = x.reshape(lead + (SCAN_NCB, 2, SCAN_CB))
    return y[..., 0, :].reshape(lead + (SSM_CH,)), y[..., 1, :].reshape(lead + (SSM_CH,))


def _compact(re, im):
    _, r, c = re.shape
    eye = jnp.eye(SCAN_GPB, dtype=re.dtype)

    def half(x):
        x = x.reshape(SCAN_NCB, SCAN_GPB, r, c)
        return (eye[None, :, None, :, None] * x[:, :, :, None, :]).reshape(SCAN_NCB, SCAN_GPB * r, SCAN_GPB * c)

    return jnp.concatenate([half(re), half(im)], axis=-1)


def _compact_extract(x, r):
    c = SSM_STATE
    eye = jnp.eye(SCAN_GPB, dtype=x.dtype)
    y = x.reshape(SCAN_NCB, SCAN_GPB, r, 2, SCAN_GPB, c)
    dg = jnp.sum(y * eye[None, :, None, None, :, None], axis=4).reshape(SSM_GROUPS, r, 2, c)
    return dg[:, :, 0, :], dg[:, :, 1, :]


def _pow_table(ar, ai, descending=False):
    ar = ar.reshape(1, SSM_CH)
    ai = ai.reshape(1, SSM_CH)
    pw = [(ar, ai)]
    for _ in range(7):
        pw.append(_cmul(pw[-1][0], pw[-1][1], ar, ai))
    zero = (jnp.zeros_like(ar), jnp.zeros_like(ar))
    rows = [pw[0], pw[1], pw[3]] + [zero] * 5 + (pw[::-1] if descending else pw)
    re = jnp.concatenate([r for r, _ in rows], axis=0)
    im = jnp.concatenate([i for _, i in rows], axis=0)
    return _perm_cols(re, im)


def _place():
    x, y, c = lax.axis_index("x"), lax.axis_index("y"), lax.axis_index("c")
    chips = [(1 - x, y), (x, 1 - y), (1 - x, 1 - y)]
    return x, y, c, chips


def _dev(px, py, pc):
    return 4 * px + 2 * py + pc


class _Plan:
    def __init__(self, ins, out_shapes, sem_shapes, start, finish, middle=None):
        self.ins, self.out_shapes, self.sem_shapes = list(ins), list(out_shapes), list(sem_shapes)
        self.start, self.finish, self.middle = start, finish, middle


def _split_plan_refs(plans, in_refs, out_refs, sem_refs):
    res, i, o, s = [], 0, 0, 0
    for p in plans:
        ni, no, ns = len(p.ins), len(p.out_shapes), len(p.sem_shapes)
        res.append((in_refs[i:i + ni], out_refs[o:o + no], sem_refs[s:s + ns]))
        i, o, s = i + ni, o + no, s + ns
    return res


def _run_plans(plans, *, name):
    ins = [a for p in plans for a in p.ins]
    outs = [o for p in plans for o in p.out_shapes]
    sems = [s for p in plans for s in p.sem_shapes]
    any_spec = pl.BlockSpec(memory_space=pl.ANY)

    def body(*refs):
        parts = _split_plan_refs(plans, refs[:len(ins)], refs[len(ins):len(ins) + len(outs)], refs[len(ins) + len(outs):])
        for p, r in zip(plans, parts):
            p.start(*r)
        for p, r in zip(plans, parts):
            if p.middle:
                p.middle(*r)
        for p, r in zip(plans, parts):
            p.finish(*r)

    res = pl.pallas_call(body, name=name, in_specs=[any_spec] * len(ins), out_specs=[any_spec] * len(outs),
                         out_shape=outs, scratch_shapes=sems)(*ins)
    return _split_plan_refs(plans, [], res, [])


def _run_plans_on_sequencer(plans, peers_of, *, name, collective_id, after=()):
    ins = [a for p in plans for a in p.ins]
    outs = [o for p in plans for o in p.out_shapes]
    sems = [s for p in plans for s in p.sem_shapes]

    def body(*refs):
        x, y, c, chips = _place()
        peers = peers_of(x, y, c, chips)
        barrier = pltpu.get_barrier_semaphore()
        for peer in peers:
            pl.semaphore_signal(barrier, inc=1, device_id=peer, device_id_type=MESH)
        pl.semaphore_wait(barrier, len(peers))
        n_in = len(ins) + len(after)
        parts = _split_plan_refs(plans, refs[:len(ins)], refs[n_in:n_in + len(outs)], refs[n_in + len(outs):])
        for p, r in zip(plans, parts):
            p.start(*r)
        for p, r in zip(plans, parts):
            if p.middle:
                p.middle(*r)
        for p, r in zip(plans, parts):
            p.finish(*r)

    res = pl.kernel(body, name=name, out_type=outs, mesh=plsc.ScalarSubcoreMesh(axis_name="seq", num_cores=1),
                    scratch_types=sems, compiler_params=pltpu.CompilerParams(collective_id=collective_id))(*ins, *after)
    return _split_plan_refs(plans, [], list(res), [])


def _call_with_plans(body, plans, *, name, grid, in_specs, out_specs, out_shape, scratch_shapes, semantics, operands):
    plans = list(plans)
    if not plans:
        res = pl.pallas_call(body, name=name, grid=grid, in_specs=in_specs, out_specs=out_specs, out_shape=out_shape,
                             scratch_shapes=scratch_shapes, compiler_params=_cparams(semantics))(*operands)
        return list(res), []
    n_in, n_out, n_scr = len(in_specs), len(out_specs), len(scratch_shapes)
    p_ins = [a for p in plans for a in p.ins]
    p_outs = [o for p in plans for o in p.out_shapes]
    p_sems = [s for p in plans for s in p.sem_shapes]
    nsteps = math.prod(grid)
    any_spec = pl.BlockSpec(memory_space=pl.ANY)

    def wrapped(*refs):
        bounds = [n_in, len(p_ins), n_out, len(p_outs), n_scr]
        parts, pos = [], 0
        for b in bounds:
            parts.append(refs[pos:pos + b])
            pos += b
        ins, p_in, outs, p_out, scr = parts
        step = pl.program_id(0)
        for ax in range(1, len(grid)):
            step = step * grid[ax] + pl.program_id(ax)
        riders = _split_plan_refs(plans, p_in, p_out, refs[pos:])

        @pl.when(step == 0)
        def _():
            for p, r in zip(plans, riders):
                p.start(*r)

        mids = [(p, r) for p, r in zip(plans, riders) if p.middle]
        mid_step = nsteps // 2
        split_mid = mids and 0 < mid_step < nsteps - 1
        if split_mid:
            @pl.when(step == mid_step)
            def _():
                for p, r in mids:
                    p.middle(*r)

        body(*ins, *outs, *scr)

        @pl.when(step == nsteps - 1)
        def _():
            if not split_mid:
                for p, r in mids:
                    p.middle(*r)
            for p, r in zip(plans, riders):
                p.finish(*r)

    res = pl.pallas_call(
        wrapped, name=name, grid=grid, in_specs=list(in_specs) + [any_spec] * len(p_ins),
        out_specs=list(out_specs) + [any_spec] * len(p_outs), out_shape=list(out_shape) + p_outs,
        scratch_shapes=list(scratch_shapes) + p_sems, compiler_params=_cparams(("arbitrary",) * len(grid)),
    )(*operands, *p_ins)
    return list(res[:n_out]), [r[1] for r in _split_plan_refs(plans, [], res[n_out:], [])]


def _gather_plan(shards):
    n = len(shards)
    nk = 8

    def make(ins, outs, sems):
        send, recv, lsem = sems
        x, y, c, _ = _place()
        me, sib, xn, yn, dg = (x, y, c), (x, y, 1 - c), (1 - x, y, c), (x, 1 - y, c), (1 - x, 1 - y, c)

        def part(w, block, half):
            ref = outs[w].at[_dev(*block)]
            if half is None:
                return ref
            rows = shards[w].shape[0] // 2
            return ref.at[pl.ds(half * rows, rows)]

        def copy(w, k, block, to, half=None, src=None):
            dst = part(w, block, half)
            return pltpu.make_async_remote_copy(
                src_ref=dst if src is None else src, dst_ref=dst,
                send_sem=send.at[w * nk + k], recv_sem=recv.at[w * nk + k], device_id=to, device_id_type=MESH)

        mine = [pltpu.make_async_copy(ins[w], outs[w].at[_dev(*me)], lsem.at[w]) for w in range(n)]
        return copy, mine, me, sib, xn, yn, dg

    def first_copies(copy, me, sib, xn, yn, ins):
        return [copy(w, k, me, to, src=ins[w]) for w in range(n) for k, to in ((0, sib), (1, xn), (2, yn))]

    def start(ins, outs, sems):
        copy, mine, me, sib, xn, yn, _ = make(ins, outs, sems)
        for cp in mine + first_copies(copy, me, sib, xn, yn, ins):
            cp.start()

    def middle(ins, outs, sems):
        copy, _, me, sib, xn, yn, _ = make(ins, outs, sems)
        for w in range(n):
            copy(w, 1, xn, me).wait_recv()
            copy(w, 3, xn, yn, half=0).start()
            copy(w, 5, xn, sib).start()
        for w in range(n):
            copy(w, 2, yn, me).wait_recv()
            copy(w, 4, yn, xn, half=1).start()
            copy(w, 6, yn, sib).start()

    def finish(ins, outs, sems):
        copy, mine, me, sib, xn, yn, dg = make(ins, outs, sems)
        last = []
        for w in range(n):
            copy(w, 3, dg, me, half=0).wait_recv()
            copy(w, 4, dg, me, half=1).wait_recv()
            fwd = copy(w, 7, dg, sib)
            fwd.start()
            last.append(fwd)
        sx, sy, sd = (1 - me[0], me[1], 1 - me[2]), (me[0], 1 - me[1], 1 - me[2]), (1 - me[0], 1 - me[1], 1 - me[2])
        for w in range(n):
            copy(w, 0, sib, me).wait_recv()
            copy(w, 5, sx, me).wait_recv()
            copy(w, 6, sy, me).wait_recv()
            copy(w, 7, sd, me).wait_recv()
        for cp in first_copies(copy, me, sib, xn, yn, ins) + last:
            cp.wait_send()
        for w in range(n):
            copy(w, 3, xn, yn, half=0).wait_send()
            copy(w, 5, xn, sib).wait_send()
            copy(w, 4, yn, xn, half=1).wait_send()
            copy(w, 6, yn, sib).wait_send()
        for cp in mine:
            cp.wait()

    return _Plan(shards, [jax.ShapeDtypeStruct((N_DEV,) + s.shape, s.dtype) for s in shards],
                 [pltpu.SemaphoreType.DMA((nk * n,)), pltpu.SemaphoreType.DMA((nk * n,)), pltpu.SemaphoreType.DMA((n,))],
                 start, finish, middle)


def _swap_plan(copies_of, n_copies, ins, out_shapes):
    def cps(in_refs, out_refs, sems):
        return copies_of(in_refs, out_refs, sems[0], sems[1])

    def start(in_refs, out_refs, sems):
        for cp in cps(in_refs, out_refs, sems):
            cp.start()

    def finish(in_refs, out_refs, sems):
        all_cps = cps(in_refs, out_refs, sems)
        for cp in all_cps:
            cp.wait_recv()
        for cp in all_cps:
            cp.wait_send()

    return _Plan(ins, out_shapes, [pltpu.SemaphoreType.DMA((n_copies,)), pltpu.SemaphoreType.DMA((n_copies,))],
                 start, finish)


def _sibling_plan(grads):
    n = len(grads)

    def copies(ins, outs, send, recv):
        x, y, c, chips = _place()
        owners = [(x, y)] + chips
        return [pltpu.make_async_remote_copy(
            src_ref=ins[w].at[_dev(*chip, 1 - c)], dst_ref=outs[w].at[k], send_sem=send.at[w * 4 + k],
            recv_sem=recv.at[w * 4 + k], device_id=(x, y, 1 - c), device_id_type=MESH)
            for w in range(n) for k, chip in enumerate(owners)]

    return _swap_plan(copies, 4 * n, grads, [jax.ShapeDtypeStruct((4,) + g.shape[1:], g.dtype) for g in grads])


def _chip_plan(parts, js=(0, 1, 2)):
    n, nj = len(parts), len(js)

    def copies(ins, outs, send, recv):
        x, y, c, chips = _place()
        return [pltpu.make_async_remote_copy(
            src_ref=ins[w].at[j], dst_ref=outs[w * nj + k], send_sem=send.at[w * nj + k],
            recv_sem=recv.at[w * nj + k], device_id=(*chips[j], c), device_id_type=MESH)
            for w in range(n) for k, j in enumerate(js)]

    return _swap_plan(copies, n * nj, parts,
                      [jax.ShapeDtypeStruct(p.shape[1:], p.dtype) for p in parts for _ in js])


UPDATE_TILE_BYTES = 1536 * 1024


def _row_tile(r, c):
    best = 8
    for t in range(8, r + 1, 8):
        if r % t == 0 and t * c * 4 <= UPDATE_TILE_BYTES:
            best = t
    return best


def _chip_partial(g, sib, ids, *, name):
    _, r, c = g.shape
    tr = _row_tile(r, c)

    def body(ids_ref, g_ref, s_ref, o_ref):
        o_ref[...] = (g_ref[...] + s_ref[...]).astype(BF16)

    return pl.pallas_call(
        body, name=name,
        grid_spec=pltpu.PrefetchScalarGridSpec(
            num_scalar_prefetch=1, grid=(3, r // tr),
            in_specs=[pl.BlockSpec((None, tr, c), lambda j, i, ids_ref: (ids_ref[j], i, 0)),
                      pl.BlockSpec((None, tr, c), lambda j, i, ids_ref: (j + 1, i, 0))],
            out_specs=pl.BlockSpec((None, tr, c), lambda j, i, ids_ref: (j, i, 0))),
        out_shape=jax.ShapeDtypeStruct((3, r, c), BF16),
        compiler_params=_cparams(("parallel", "parallel")),
    )(ids, g, sib)


def _adamw_math(w, g, m, v):
    m = ADAM_B1 * m + (1.0 - ADAM_B1) * g
    v = ADAM_B2 * v + (1.0 - ADAM_B2) * (g * g)
    m_hat = m / (1.0 - ADAM_B1 ** ADAM_STEP)
    v_hat = v / (1.0 - ADAM_B2 ** ADAM_STEP)
    delta = -ADAM_LR * (m_hat / (jnp.sqrt(v_hat) + ADAM_EPS) + ADAM_WD * w)
    return delta, m, v


def _shard_update(g, sib, rem, me, w, m, v, *, name):
    r, c = w.shape
    tr = _row_tile(r, c)

    def body(me_ref, g_ref, s_ref, r0_ref, r1_ref, r2_ref, w_ref, m_ref, v_ref, go_ref, d_ref, mo_ref, vo_ref):
        gt = g_ref[...] + s_ref[...]
        gt = gt + r0_ref[...].astype(F32)
        gt = gt + r1_ref[...].astype(F32)
        gt = gt + r2_ref[...].astype(F32)
        go_ref[...] = gt
        d, mn, vn = _adamw_math(w_ref[...], gt, m_ref[...], v_ref[...])
        d_ref[...] = d
        mo_ref[...] = mn
        vo_ref[...] = vn

    blk = lambda k: pl.BlockSpec((None, tr, c), lambda i, me_ref: (k, i, 0))
    plain = pl.BlockSpec((tr, c), lambda i, me_ref: (i, 0))
    return pl.pallas_call(
        body, name=name,
        grid_spec=pltpu.PrefetchScalarGridSpec(
            num_scalar_prefetch=1, grid=(r // tr,),
            in_specs=[pl.BlockSpec((None, tr, c), lambda i, me_ref: (me_ref[0], i, 0)), blk(0), plain, plain, plain,
                      plain, plain, plain],
            out_specs=[plain] * 4),
        out_shape=[jax.ShapeDtypeStruct((r, c), F32)] * 4,
        compiler_params=_cparams(("parallel",)),
    )(me, g, sib, *rem, w, m, v)


def _small_update(gathered, w, m, v, *, name):
    _, r, c = gathered.shape

    def body(g_ref, w_ref, m_ref, v_ref, go_ref, d_ref, mo_ref, vo_ref):
        gt = g_ref[0]
        for k in range(1, N_DEV):
            gt = gt + g_ref[k]
        go_ref[...] = gt
        d, mn, vn = _adamw_math(w_ref[...], gt, m_ref[...], v_ref[...])
        d_ref[...] = d
        mo_ref[...] = mn
        vo_ref[...] = vn

    return pl.pallas_call(
        body, name=name, out_shape=[jax.ShapeDtypeStruct((r, c), F32)] * 4,
        compiler_params=pltpu.CompilerParams(vmem_limit_bytes=VMEM_LIMIT),
    )(gathered, w, m, v)


SMALL_UNIT = 1024


def _pack(parts):
    flat = []
    for p in parts:
        f = p.reshape(-1).astype(F32)
        pad = (-f.shape[0]) % SMALL_UNIT
        flat.append(jnp.pad(f, (0, pad)))
    return jnp.concatenate(flat).reshape(-1, 128)


def _unpack(buf, shapes):
    flat = buf.reshape(-1)
    out, off = [], 0
    for s in shapes:
        nel = math.prod(s)
        out.append(flat[off:off + nel].reshape(s))
        off += nel + ((-nel) % SMALL_UNIT)
    return out


def kernel(x, p, ffn1_w_in, ffn1_w_out, ln1_g, ln1_b, mix_w_in, conv_w, conv_b, conv_w_out, ssm_lam_re, ssm_lam_im, ssm_log_step, ssm_b_re, ssm_b_im, ssm_c_re, ssm_c_im, ssm_d, ssm_w_glu, mix_w_out, ln2_g, ln2_b, ffn2_w_in, ffn2_w_out, ln3_g, ln3_b, ple_w_in, ple_w_gate, ln4_g, ln4_b, loss_target, m_ffn1_w_in, m_ffn1_w_out, m_ln1_g, m_ln1_b, m_mix_w_in, m_conv_w, m_conv_b, m_conv_w_out, m_ssm_lam_re, m_ssm_lam_im, m_ssm_log_step, m_ssm_b_re, m_ssm_b_im, m_ssm_c_re, m_ssm_c_im, m_ssm_d, m_ssm_w_glu, m_mix_w_out, m_ln2_g, m_ln2_b, m_ffn2_w_in, m_ffn2_w_out, m_ln3_g, m_ln3_b, m_ple_w_in, m_ple_w_gate, m_ln4_g, m_ln4_b, v_ffn1_w_in, v_ffn1_w_out, v_ln1_g, v_ln1_b, v_mix_w_in, v_conv_w, v_conv_b, v_conv_w_out, v_ssm_lam_re, v_ssm_lam_im, v_ssm_log_step, v_ssm_b_re, v_ssm_b_im, v_ssm_c_re, v_ssm_c_im, v_ssm_d, v_ssm_w_glu, v_mix_w_out, v_ln2_g, v_ln2_b, v_ffn2_w_in, v_ffn2_w_out, v_ln3_g, v_ln3_b, v_ple_w_in, v_ple_w_gate, v_ln4_g, v_ln4_b):
    args = dict(locals())
    big = ['ffn1_w_in', 'ffn1_w_out', 'mix_w_in', 'conv_w_out', 'ssm_w_glu', 'mix_w_out',
           'ffn2_w_in', 'ffn2_w_out', 'ple_w_in', 'ple_w_gate']
    small = ['ln1_g', 'ln1_b', 'conv_b', 'ssm_lam_re', 'ssm_lam_im', 'ssm_log_step', 'ssm_b_re', 'ssm_b_im',
             'ssm_c_re', 'ssm_c_im', 'ssm_d', 'ln2_g', 'ln2_b', 'ln3_g', 'ln3_b', 'ln4_g', 'ln4_b']
    order = ['ffn1_w_in', 'ffn1_w_out', 'ln1_g', 'ln1_b', 'mix_w_in', 'conv_w', 'conv_b', 'conv_w_out',
             'ssm_lam_re', 'ssm_lam_im', 'ssm_log_step', 'ssm_b_re', 'ssm_b_im', 'ssm_c_re', 'ssm_c_im', 'ssm_d',
             'ssm_w_glu', 'mix_w_out', 'ln2_g', 'ln2_b', 'ffn2_w_in', 'ffn2_w_out', 'ln3_g', 'ln3_b',
             'ple_w_in', 'ple_w_gate', 'ln4_g', 'ln4_b']

    t = x.shape[1]
    d = x.shape[2]
    xc_, yc_, cc_ = lax.axis_index("x"), lax.axis_index("y"), lax.axis_index("c")
    me = (4 * xc_ + 2 * yc_ + cc_).astype(jnp.int32)
    cw_cols = conv_w.shape[2]

    turned = ('ffn1_w_in', 'ffn2_w_in')

    def local(a, nm):
        return jnp.swapaxes(a[0], 0, 1) if nm in turned else a[0]

    shard = {nm: local(args[nm], nm).astype(BF16) for nm in big}
    cw_pad = jnp.zeros((16, 128), F32).at[0:3, 0:cw_cols].set(conv_w[0])
    wf = shard['ffn1_w_in'].shape[0]

    def tie(chain, *others):
        out = lax.optimization_barrier((chain, *others))
        return out[0], out[1:]

    def gather(arrays, tag, after=()):
        ((_, got, _),) = _run_plans_on_sequencer(
            [_gather_plan(arrays)], lambda x, y, c, chips: [(x, y, 1 - c), (1 - x, y, c), (x, 1 - y, c)],
            name=f"gather_{tag}", collective_id=2, after=after)
        return got

    w1in, cw_g = gather([shard['ffn1_w_in'], cw_pad], "ffn1_in")
    (w1out_g,) = gather([shard['ffn1_w_out']], "ffn1_out")
    (wmix,) = gather([shard['mix_w_in']], "mix_in")
    wco, wglu, wmo_g = gather([shard[nm] for nm in ('conv_w_out', 'ssm_w_glu', 'mix_w_out')], "mix_rest")
    (w2in,) = gather([shard['ffn2_w_in']], "ffn2_in")
    (w2out_g,) = gather([shard['ffn2_w_out']], "ffn2_out")
    wpin, wgate_g = gather([shard['ple_w_in'], shard['ple_w_gate']], "ple")
    cw_full = jnp.transpose(cw_g[:, 0:3, 0:cw_cols], (1, 0, 2)).reshape(3, N_DEV * cw_cols)
    cw8 = jnp.zeros((8, CONV_CH), F32).at[0:3, :].set(cw_full)

    s5_in = (ssm_lam_re[0], ssm_lam_im[0], ssm_log_step[0], ssm_b_re[0], ssm_b_im[0])
    (a_re, a_im, bb_re, bb_im), s5_vjp = jax.vjp(_s5_discretise, *s5_in)
    tab_f = _pow_table(a_re, a_im)
    tab_b = _pow_table(a_re, -a_im, descending=True)
    bmat_b = _compact(jnp.transpose(bb_re, (0, 2, 1)), jnp.transpose(bb_im, (0, 2, 1))).astype(BF16)
    cmat_tb = _compact(ssm_c_re[0], -ssm_c_im[0]).astype(BF16)
    dvec = ssm_d[0].reshape(1, SSM_W)

    xf = x[0]
    x_b = xf.astype(BF16)
    p_b = p[0, 0].astype(BF16)
    tgt = loss_target[0]
    tq = min(512, t)

    ffn_out = dict(ja='c', jb='c', nj=4, tm=tq, tn=d, tk=wf)
    def side_by_side(wb):
        return jnp.transpose(wb, (1, 0, 2)).reshape(wb.shape[1], N_DEV * wb.shape[2])

    tf = min(1024, t)
    a1, h1, _ = _ffn_in(x_b, w1in, name="ffn1_in", tm=tf)
    w1out = w1out_g.reshape(4, wf, d)
    r1, x1, x1b = _mm(a1, w1out, name="ffn1_out", **ffn_out, epilogue=_ln_epilogue(xf, ln1_g, ln1_b, 0.5))
    proj = _mm(x1b, wmix, name="mix_in", jb='b', jo='b', o_flat=True, nj=8, tm=tq, tn=512, tk=d)
    _, (wco, wglu, wmo_g) = tie(proj, wco, wglu, wmo_g)
    wmo = wmo_g.reshape(d, d)
    ycin = _conv_fwd(proj, cw8, conv_b, name="conv_fwd")
    wco, wglu = side_by_side(wco), side_by_side(wglu)
    yconv = _mm(ycin, wco, name="conv_out", tm=tq, tn=d, tk=CONV_CH)
    s_f, s_b, _ = _scan_fwd(proj, bmat_b, tab_f, name="scan_fwd")
    blk = dict(ja='b', jb='b', jo='b', nj=SCAN_NCB)
    ymm = _mm(s_b, cmat_tb, name="ssm_read", a_flat=True, o_flat=True, tb=True, tm=tq, tn=SCAN_UW, tk=2 * SCAN_CB, **blk)
    ys, sg, u_b = _s5_out(ymm, proj, dvec, name="ssm_out")
    glu = _mm(sg, wglu, name="glu_in", tm=tq, tn=d, tk=SSM_W)
    merged = _gate_fwd(yconv, glu, proj, name="gate_fwd")
    r2, x2, x2b = _mm(merged, wmo, name="mix_out", tm=tq, tn=d, tk=d, epilogue=_ln_epilogue(x1, ln2_g, ln2_b, 1.0))
    a2, h2, _ = _ffn_in(x2b, w2in, name="ffn2_in", tm=tf)
    w2out = w2out_g.reshape(4, wf, d)
    r3, x3, x3b = _mm(a2, w2out, name="ffn2_out", **ffn_out, epilogue=_ln_epilogue(x2, ln3_g, ln3_b, 0.5))
    wgate = wgate_g.reshape(d, d)
    _, (wpin,) = tie(x2b, wpin)
    pe = _mm(p_b, side_by_side(wpin), name="ple_in", tm=tq, tn=d, tk=p_b.shape[1])
    gp = _mm(x3b, wgate, name="ple_gate", tm=tq, tn=d, tk=d)

    dr4, dpe_b, dgp_b, dg4, db4, loss_part = _ln_bwd(x3, [pe, gp], ln4_g, ln4_b, [], name="ple_ln_bwd",
                                                     ple=True, target=tgt)
    gb, sib, rem = {}, {}, {}
    ids = jnp.stack([_dev(1 - xc_, yc_, cc_), _dev(xc_, 1 - yc_, cc_), _dev(1 - xc_, 1 - yc_, cc_)]).astype(jnp.int32)

    def blocked(nm, g):
        return g.reshape((N_DEV,) + args[nm].shape[1:])

    def to_sibling(*names):
        return _sibling_plan([gb[nm] for nm in names])

    def chip_sums(names, sibs):
        for nm, s in zip(names, sibs):
            sib[nm] = s
        return [_chip_partial(gb[nm], sib[nm], ids, name=f"chip_sum_{nm}") for nm in names]

    def on_sequencer(plan, peers, tag, cid, after=()):
        ((_, got, _),) = _run_plans_on_sequencer([plan], peers, name=tag, collective_id=cid, after=after)
        return got

    waiting = []
    arrived = []

    def send_sibling(*names, after=()):
        got = on_sequencer(to_sibling(*names), lambda x, y, c, chips: [(x, y, 1 - c)], f"grad_sibling_{names[0]}", 3,
                           after=after)
        waiting.append((names, got))

    def send_chips(chain, count=None):
        for _ in range(len(waiting) if count is None else count):
            names, got = waiting.pop(0)
            sums = chip_sums(names, got)
            chain, rest = tie(chain, *sums, *arrived)
            r_ = on_sequencer(_chip_plan(list(rest[:len(sums)])), lambda x, y, c, chips: [(*chip, c) for chip in chips],
                              f"grad_chips_{names[0]}", 1)
            arrived[:] = list(r_)
            for i, nm in enumerate(names):
                rem[nm] = r_[3 * i:3 * i + 3]
        return chain

    ffn_in_dg = dict(ja='c', jb='c', nj=8, tm=tq, tn=d, tk=wf)
    ffn_in_wg = dict(ja='b', jo='b', ta=True, nj=8, tm=wf, tn=d, tk=t)
    ffn_out_wg = dict(ja='b', jo='b', ta=True, nj=4, tm=wf, tn=d, tk=t)

    gb['ple_w_in'] = _mm(p_b, dpe_b, name="ple_in_wg", jb='b', jo='b', b_flat=True, ta=True, nj=8,
                         tm=p_b.shape[1], tn=128, tk=t)
    gb['ple_w_gate'] = blocked('ple_w_gate', _mm(x3b, dgp_b, name="ple_gate_wg", ta=True, tm=d, tn=d, tk=t))
    g_ple = ('ple_w_in', 'ple_w_gate')
    dx3_g, (s_,) = _mm(dgp_b, wgate, name="ple_gate_dg", tb=True, tm=tq, tn=d, tk=d, plans=[to_sibling(*g_ple)])
    waiting.append((g_ple, s_))

    dr3, df2_b, dg3, db3 = _ln_bwd(r3, [], ln3_g, ln3_b, [(dr4, ALPHA), (dx3_g, 1.0)], name="ffn2_ln_bwd", fs=0.5)
    df2_b = send_chips(df2_b)
    gb['ffn2_w_out'] = blocked('ffn2_w_out', _mm(a2, df2_b, name="ffn2_out_wg", **ffn_out_wg))
    df2_b, (gb['ffn2_w_out'],) = tie(df2_b, gb['ffn2_w_out'])
    send_sibling('ffn2_w_out')
    dh2, _ = _ffn_out_dg(df2_b, w2out, h2, name="ffn2_out_dg", tm=tf)
    dx2_f = _mm(dh2, w2in, name="ffn2_in_dg", **ffn_in_dg)
    dx2_f = send_chips(dx2_f)
    gb['ffn2_w_in'] = _mm(dh2, x2b, name="ffn2_in_wg", **ffn_in_wg)
    dx2_f, (gb['ffn2_w_in'],) = tie(dx2_f, gb['ffn2_w_in'])

    dr2, dmix_b, dg2, db2 = _ln_bwd(r2, [], ln2_g, ln2_b, [(dr3, ALPHA), (dx2_f, 1.0)], name="mix_ln_bwd")
    dmerged = _mm(dmix_b, wmo, name="mix_out_dg", tb=True, tm=tq, tn=d, tk=d)
    send_sibling('ffn2_w_in', after=(dmerged,))
    gb['mix_w_out'] = blocked('mix_w_out', _mm(merged, dmix_b, name="mix_out_wg", ta=True, tm=d, tn=d, tk=t))
    dyconv_b, dglu_b, dgate_b = _gate_bwd(dmerged, yconv, glu, proj, name="gate_bwd")
    gb['conv_w_out'] = _mm(ycin, dyconv_b, name="conv_out_wg", jb='b', jo='b', b_flat=True, ta=True, nj=8,
                           tm=CONV_CH, tn=128, tk=t)
    dycin = _mm(dyconv_b, wco, name="conv_out_dg", tb=True, tm=tq, tn=CONV_CH, tk=d)
    gb['ssm_w_glu'] = _mm(sg, dglu_b, name="glu_in_wg", jb='b', jo='b', b_flat=True, ta=True, nj=8,
                          tm=SSM_W, tn=256, tk=t)
    dsg = _mm(dglu_b, wglu, name="glu_in_dg", tb=True, tm=tq, tn=SSM_W, tk=2 * d)
    dsg = send_chips(dsg)
    dys, dys_b, dd = _s5_bwd_in(dsg, ys, proj, name="ssm_out_bwd")
    h_b, da_acc, _ = _scan_bwd(dys_b, cmat_tb, s_f, tab_b, name="scan_bwd")
    send_sibling('mix_w_out', 'conv_w_out', 'ssm_w_glu', after=(h_b,))
    dumm = _mm(h_b, bmat_b, name="ssm_write_dg", a_flat=True, o_flat=True, tb=True, tm=tq, tn=SCAN_UW,
               tk=2 * SCAN_CB, **blk)
    du_b = _s5_du(dumm, dys, dvec, name="ssm_du")
    g_bmat = _mm(u_b, h_b, name="ssm_write_wg", a_flat=True, b_flat=True, ta=True, tm=SCAN_UW,
                 tn=2 * SCAN_CB, tk=t, **blk)
    g_cmat = _mm(dys_b, s_b, name="ssm_read_wg", a_flat=True, b_flat=True, ta=True, tm=SCAN_UW,
                 tn=2 * SCAN_CB, tk=t, **blk)
    dcb_b, dcc_b, dch_b, dconv = _conv_bwd(dycin, proj, cw8, conv_b, name="conv_bwd")
    dproj = jnp.concatenate([dcb_b, dcc_b, dch_b, du_b, dgate_b], axis=1)
    dproj = send_chips(dproj)
    gb['mix_w_in'] = _mm(x1b, dproj, name="mix_in_wg", jb='b', jo='b', b_flat=True, ta=True, nj=8,
                         tm=d, tn=512, tk=t)
    send_sibling('mix_w_in')
    dx1_m = _mm(dproj, wmix, name="mix_in_dg", ja='c', jb='c', a_flat=True, tb=True, nj=8, tm=tq, tn=d, tk=512)
    dx1_m, (gb['mix_w_in'],) = tie(dx1_m, gb['mix_w_in'])

    dr1, df1_b, dg1, db1 = _ln_bwd(r1, [], ln1_g, ln1_b, [(dr2, ALPHA), (dx1_m, 1.0)], name="ffn1_ln_bwd", fs=0.5)
    gb['ffn1_w_out'] = blocked('ffn1_w_out', _mm(a1, df1_b, name="ffn1_out_wg", **ffn_out_wg))
    df1_b, (gb['ffn1_w_out'],) = tie(df1_b, gb['ffn1_w_out'])
    send_sibling('ffn1_w_out')
    df1_b = send_chips(df1_b, 1)
    dh1, _ = _ffn_out_dg(df1_b, w1out, h1, name="ffn1_out_dg", tm=tf)
    gb['ffn1_w_in'] = _mm(dh1, x_b, name="ffn1_in_wg", **ffn_in_wg)
    dh1, (gb['ffn1_w_in'],) = tie(dh1, gb['ffn1_w_in'])
    send_sibling('ffn1_w_in')
    dh1 = send_chips(dh1, 1)
    (grad_x,) = _mm(dh1, w1in, name="ffn1_in_dg", **ffn_in_dg,
                    epilogue=(lambda pr, drv: (pr + ALPHA * drv,), (dr1,), (), (F32,)))

    da_sum = jnp.sum(da_acc, axis=0)
    da_re, da_im = _unperm_cols(da_sum)
    gbb_re, gbb_im = [jnp.transpose(v, (0, 2, 1)) for v in _compact_extract(g_bmat, SSM_GROUP)]
    g_c_re, g_c_im_neg = _compact_extract(g_cmat, SSM_GROUP)
    g_c_im = -g_c_im_neg
    g_lam_re, g_lam_im, g_log_step, g_b_re, g_b_im = s5_vjp(
        (da_re.reshape(SSM_GROUPS, SSM_STATE), da_im.reshape(SSM_GROUPS, SSM_STATE), gbb_re, gbb_im))
    g_d = dd.reshape(SSM_GROUPS, SSM_GROUP)

    small_g = {'ln1_g': dg1, 'ln1_b': db1, 'conv_b': dconv[3:4], 'ssm_lam_re': g_lam_re, 'ssm_lam_im': g_lam_im,
               'ssm_log_step': g_log_step, 'ssm_b_re': g_b_re, 'ssm_b_im': g_b_im, 'ssm_c_re': g_c_re,
               'ssm_c_im': g_c_im, 'ssm_d': g_d, 'ln2_g': dg2, 'ln2_b': db2, 'ln3_g': dg3, 'ln3_b': db3,
               'ln4_g': dg4, 'ln4_b': db4}
    small_shapes = [args[nm].shape for nm in small] + [(3, CONV_CH), (1,)]
    g_pack = _pack([small_g[nm] for nm in small] + [dconv[0:3], loss_part[0:1, 0:1]])

    res = {}
    me1, _ = tie(me.reshape(1), grad_x)
    me1 = send_chips(me1, 1)

    def update(nm):
        upd = _shard_update(gb[nm], sib[nm], rem[nm], me1, local(args[nm], nm), local(args['m_' + nm], nm),
                            local(args['v_' + nm], nm), name=f"update_{nm}")
        for key, val in zip(('grad_', 'delta_', 'new_m_', 'new_v_'), upd):
            res[key + nm] = (jnp.swapaxes(val, 0, 1) if nm in turned else val)[None]

    (g_all,) = gather([g_pack], "small", after=(dh1,))
    last = ['ffn1_w_out', 'ffn1_w_in']
    for nm in big:
        if nm not in last:
            update(nm)

    def full_cw(a):
        return lax.dynamic_update_slice(jnp.zeros((3, CONV_CH), F32), a[0], (0, me * cw_cols))

    zero1 = jnp.zeros((1,), F32)
    w_pack = _pack([args[nm] for nm in small] + [full_cw(conv_w), zero1])
    m_pack = _pack([args['m_' + nm] for nm in small] + [full_cw(m_conv_w), zero1])
    v_pack = _pack([args['v_' + nm] for nm in small] + [full_cw(v_conv_w), zero1])
    sg_sum, sd, sm, sv = _small_update(g_all, w_pack, m_pack, v_pack, name="small_update")
    for key, buf in (('grad_', sg_sum), ('delta_', sd), ('new_m_', sm), ('new_v_', sv)):
        parts = _unpack(buf, small_shapes)
        for nm, val in zip(small, parts[:len(small)]):
            res[key + nm] = val
        res[key + 'conv_w'] = lax.dynamic_slice(parts[len(small)], (0, me * cw_cols), (3, cw_cols))[None]
        if key == 'grad_':
            loss = parts[-1][0]

    for nm in last:
        update(nm)

    outs = [loss, grad_x[None]]
    for key in ('grad_', 'delta_', 'new_m_', 'new_v_'):
        outs += [res[key + nm] for nm in order]
    return tuple(outs)
```

```python
import functools
import math

import jax
import jax.numpy as jnp
from jax import lax
from jax.experimental import pallas as pl
from jax.experimental.pallas import tpu as pltpu
from jax.experimental.pallas import tpu_sc as plsc

F32 = jnp.float32
BF16 = jnp.bfloat16
MESH = pl.DeviceIdType.MESH

N_DEV = 8
ALPHA = 2.0 ** 0.25
LN_EPS = 1e-5
CONV_CH = 512
SSM_W = 512
SSM_GROUPS = 32
SSM_GROUP = 16
SSM_STATE = 64
SSM_CH = SSM_GROUPS * SSM_STATE
SCAN_CB = 512
SCAN_NCB = SSM_CH // SCAN_CB
SCAN_GPB = SSM_GROUPS // SCAN_NCB
SCAN_UW = SCAN_GPB * SSM_GROUP
SCAN_TT = 256
ADAM_LR = 0.001
ADAM_B1 = 0.9
ADAM_B2 = 0.999
ADAM_EPS = 1e-08
ADAM_WD = 0.01
ADAM_STEP = 10
VMEM_LIMIT = 56 * 1024 * 1024


def _cparams(sem=None, **kw):
    return pltpu.CompilerParams(dimension_semantics=sem, vmem_limit_bytes=VMEM_LIMIT, **kw)


def _mm(a, b, *, name, ja=None, jb=None, jo=None, a_flat=False, b_flat=False, o_flat=False,
        ta=False, tb=False, tm, tn, tk, nj=1, out_dtype=F32, plans=(), epilogue=None):
    def dims(arr, j, flat):
        if j is None:
            return arr.shape
        if flat:
            return (arr.shape[0], arr.shape[1] // nj)
        assert arr.shape[0] == nj, (name, arr.shape, nj)
        return arr.shape[1:]

    ar, ac = dims(a, ja, a_flat)
    br, bc = dims(b, jb, b_flat)
    m, k = (ac, ar) if ta else (ar, ac)
    k2, n = (bc, br) if tb else (br, bc)
    assert k == k2, (name, a.shape, b.shape)
    assert m % tm == 0 and n % tn == 0 and k % tk == 0, (name, m, n, k, tm, tn, tk)
    njb = nj if 'b' in (ja, jb) else 1
    njc = nj if 'c' in (ja, jb) else 1
    nk = k // tk
    j_inside = njc > 1 and nk == 1 and not ta
    n_in = njc if j_inside else 1
    nred = nk if j_inside else njc * nk
    grid = (njb, m // tm, n // tn, 1 if j_inside else njc, nk)

    def make_spec(j, flat, blk, rfn, cfn, cols_per_j):
        def jsel(g, c):
            return g if j == 'b' else c
        if j is None:
            return pl.BlockSpec(blk, lambda g, i, jn, c, kk: (rfn(i, jn, kk), cfn(i, jn, kk)))
        if j == 'c' and j_inside:
            if flat:
                return pl.BlockSpec((blk[0], nj * blk[1]), lambda g, i, jn, c, kk: (rfn(i, jn, kk), 0))
            return pl.BlockSpec((nj,) + blk, lambda g, i, jn, c, kk: (0, rfn(i, jn, kk), cfn(i, jn, kk)))
        if flat:
            nb = cols_per_j // blk[1]
            return pl.BlockSpec(blk, lambda g, i, jn, c, kk: (rfn(i, jn, kk), jsel(g, c) * nb + cfn(i, jn, kk)))
        return pl.BlockSpec((None,) + blk,
                            lambda g, i, jn, c, kk: (jsel(g, c), rfn(i, jn, kk), cfn(i, jn, kk)))

    if ta:
        a_spec = make_spec(ja, a_flat, (tk, tm), lambda i, jn, kk: kk, lambda i, jn, kk: i, ac)
    else:
        a_spec = make_spec(ja, a_flat, (tm, tk), lambda i, jn, kk: i, lambda i, jn, kk: kk, ac)
    if tb:
        b_spec = make_spec(jb, b_flat, (tn, tk), lambda i, jn, kk: jn, lambda i, jn, kk: kk, bc)
    else:
        b_spec = make_spec(jb, b_flat, (tk, tn), lambda i, jn, kk: kk, lambda i, jn, kk: jn, bc)
    o_spec = make_spec(jo, o_flat, (tm, tn), lambda i, jn, kk: i, lambda i, jn, kk: jn, n)
    if jo is None:
        out_shape = (m, n)
    elif o_flat:
        out_shape = (m, nj * n)
    else:
        out_shape = (nj, m, n)

    dn = (((0 if ta else 1,), (1 if tb else 0,)), ((), ()))

    def operand(ref, j, flat, jj, width):
        if not (j == 'c' and j_inside):
            return ref[...]
        return ref[:, jj * width:(jj + 1) * width] if flat else ref[jj]

    e_fn, e_rows, e_vecs, e_dtypes = epilogue if epilogue else (None, (), (), (out_dtype,))
    assert not epilogue or (nred == 1 and jo is None), name
    n_e = len(e_rows) + len(e_vecs)
    n_o = len(e_dtypes)

    def body(a_ref, b_ref, *rest):
        e_refs, o_refs, scratch = rest[:n_e], rest[n_e:n_e + n_o], rest[n_e + n_o:]
        o_ref = o_refs[0]
        p = None
        for jj in range(n_in):
            q = lax.dot_general(operand(a_ref, ja, a_flat, jj, tk), operand(b_ref, jb, b_flat, jj, tk if tb else tn),
                                dn, preferred_element_type=F32)
            p = q if p is None else p + q
        if epilogue:
            for ref, val in zip(o_refs, e_fn(p, *[r[...] for r in e_refs])):
                ref[...] = val.astype(ref.dtype)
        elif nred == 1:
            o_ref[...] = p.astype(o_ref.dtype)
        else:
            acc = scratch[0]
            r = pl.program_id(3) * nk + pl.program_id(4)

            @pl.when(r == 0)
            def _():
                acc[...] = p

            @pl.when(r > 0)
            def _():
                acc[...] += p

            @pl.when(r == nred - 1)
            def _():
                o_ref[...] = acc[...].astype(o_ref.dtype)

    vec_spec = pl.BlockSpec((1, tn), lambda g, i, jn, c, kk: (0, jn))
    res = _call_with_plans(
        body, plans, name=name, grid=grid,
        in_specs=[a_spec, b_spec] + [o_spec] * len(e_rows) + [vec_spec] * len(e_vecs), out_specs=[o_spec] * n_o,
        out_shape=[jax.ShapeDtypeStruct(out_shape, dt) for dt in e_dtypes],
        scratch_shapes=[] if nred == 1 else [pltpu.VMEM((tm, tn), F32)],
        semantics=("parallel", "parallel", "parallel", "arbitrary", "arbitrary"), operands=(a, b, *e_rows, *e_vecs))
    outs = res[0] if epilogue else res[0][0]
    return (outs, res[1]) if plans else outs


def _sigmoid(v):
    return jax.nn.sigmoid(v)


def _row_spec(tm, cols, colblk=0):
    return pl.BlockSpec((tm, cols), lambda i: (i, colblk))


def _vec_spec(cols):
    return pl.BlockSpec((1, cols), lambda i: (0, 0))


def _ffn_in(xb, win, *, name, tm, plans=()):
    t, d = xb.shape
    nj, w, _ = win.shape
    half = nj // 2
    dn = (((1,), (1,)), ((), ()))

    def body(x_ref, wg_ref, wu_ref, a_ref, gu_ref):
        xv = x_ref[...]
        g = lax.dot_general(xv, wg_ref[...], dn, preferred_element_type=F32)
        u = lax.dot_general(xv, wu_ref[...], dn, preferred_element_type=F32)
        a_ref[...] = (g * _sigmoid(g) * u).astype(BF16)
        gu_ref[0] = g.astype(BF16)
        gu_ref[1] = u.astype(BF16)

    (a, gu), riders = _call_with_plans(
        body, plans, name=name, grid=(half, t // tm),
        in_specs=[pl.BlockSpec((tm, d), lambda j, i: (i, 0)),
                  pl.BlockSpec((None, w, d), lambda j, i: (j, 0, 0)),
                  pl.BlockSpec((None, w, d), lambda j, i: (j + half, 0, 0))],
        out_specs=[pl.BlockSpec((None, tm, w), lambda j, i: (j, i, 0)),
                   pl.BlockSpec((2, None, tm, w), lambda j, i: (0, j, i, 0))],
        out_shape=[jax.ShapeDtypeStruct((half, t, w), BF16), jax.ShapeDtypeStruct((2, half, t, w), BF16)],
        scratch_shapes=[], semantics=("parallel", "parallel"), operands=(xb, win, win))
    return a, gu, riders


def _ffn_out_dg(dfb, wout, gu, *, name, tm, plans=()):
    t, d = dfb.shape
    half, w, _ = wout.shape
    dn = (((1,), (1,)), ((), ()))

    def body(df_ref, w_ref, gu_ref, dh_ref):
        da = lax.dot_general(df_ref[...], w_ref[...], dn, preferred_element_type=F32)
        g = gu_ref[0].astype(F32)
        u = gu_ref[1].astype(F32)
        sg = _sigmoid(g)
        dh_ref[0] = (da * u * (sg * (1.0 + g * (1.0 - sg)))).astype(BF16)
        dh_ref[1] = (da * (g * sg)).astype(BF16)

    (out,), riders = _call_with_plans(
        body, plans, name=name, grid=(half, t // tm),
        in_specs=[pl.BlockSpec((tm, d), lambda j, i: (i, 0)),
                  pl.BlockSpec((None, w, d), lambda j, i: (j, 0, 0)),
                  pl.BlockSpec((2, None, tm, w), lambda j, i: (0, j, i, 0))],
        out_specs=[pl.BlockSpec((2, None, tm, w), lambda j, i: (0, j, i, 0))],
        out_shape=[jax.ShapeDtypeStruct((2, half, t, w), BF16)],
        scratch_shapes=[], semantics=("parallel", "parallel"), operands=(dfb, wout, gu))
    return out.reshape(2 * half, t, w), riders


def _ln_stats(r):
    mu = jnp.mean(r, axis=-1, keepdims=True)
    xc = r - mu
    var = jnp.mean(xc * xc, axis=-1, keepdims=True)
    rstd = lax.rsqrt(var + LN_EPS)
    return xc * rstd, rstd


def _ln_epilogue(xin, gamma, beta, fs):
    def fn(p, xv, g, b):
        r = ALPHA * xv + fs * p
        xh, _ = _ln_stats(r)
        y = xh * g + b
        return r, y, y

    return fn, (xin,), (gamma, beta), (F32, F32, BF16)


def _ln_bwd(xin, fparts, gamma, beta, grads, *, name, fs=1.0, ple=False, target=None, tm=512):
    t, d = xin.shape
    nf = len(fparts)
    ng = len(grads)
    coefs = [c for _, c in grads]
    use_t = target is not None
    n_fout = 2 if ple else 1

    def body(*refs):
        pos = 0
        x_ref = refs[pos]; pos += 1
        f_refs = refs[pos:pos + nf]; pos += nf
        g_ref, b_ref = refs[pos:pos + 2]; pos += 2
        gr_refs = refs[pos:pos + ng]; pos += ng
        if use_t:
            t_ref = refs[pos]; pos += 1
        dr_ref = refs[pos]; pos += 1
        fo_refs = refs[pos:pos + n_fout]; pos += n_fout
        dg_ref, db_ref = refs[pos:pos + 2]; pos += 2
        if use_t:
            loss_ref = refs[pos]; pos += 1
        i = pl.program_id(0)

        if ple:
            pe = f_refs[0][...]
            sg = _sigmoid(f_refs[1][...])
            resid = ALPHA * x_ref[...] + pe * sg
        else:
            resid = x_ref[...]
        xh, rstd = _ln_stats(resid)
        gam = g_ref[...]
        if use_t:
            diff = xh * gam + b_ref[...] - t_ref[...]
            dy = diff * (1.0 / d)
            lpart = 0.5 * jnp.sum(jnp.sum(diff * diff, axis=-1, keepdims=True), axis=0, keepdims=True) * (1.0 / d)
        else:
            dy = coefs[0] * gr_refs[0][...]
            for c, r in zip(coefs[1:], gr_refs[1:]):
                dy = dy + c * r[...]
        dxh = dy * gam
        m1 = jnp.mean(dxh, axis=-1, keepdims=True)
        m2 = jnp.mean(dxh * xh, axis=-1, keepdims=True)
        dr = rstd * (dxh - m1 - xh * m2)
        dr_ref[...] = dr
        if ple:
            fo_refs[0][...] = (dr * sg).astype(BF16)
            fo_refs[1][...] = (dr * pe * (sg * (1.0 - sg))).astype(BF16)
        else:
            fo_refs[0][...] = (fs * dr).astype(BF16)
        dgp = jnp.sum(dy * xh, axis=0, keepdims=True)
        dbp = jnp.sum(dy, axis=0, keepdims=True)

        @pl.when(i == 0)
        def _():
            dg_ref[...] = dgp
            db_ref[...] = dbp
            if use_t:
                loss_ref[...] = jnp.broadcast_to(lpart, loss_ref.shape)

        @pl.when(i > 0)
        def _():
            dg_ref[...] += dgp
            db_ref[...] += dbp
            if use_t:
                loss_ref[...] += jnp.broadcast_to(lpart, loss_ref.shape)

    ins = [xin, *fparts, gamma, beta, *[g for g, _ in grads]] + ([target] if use_t else [])
    in_specs = ([_row_spec(tm, d)] * (1 + nf) + [_vec_spec(d), _vec_spec(d)] + [_row_spec(tm, d)] * ng
                + ([_row_spec(tm, d)] if use_t else []))
    out_specs = [_row_spec(tm, d)] * (1 + n_fout) + [_vec_spec(d), _vec_spec(d)] + ([_vec_spec(128)] if use_t else [])
    out_shape = ([jax.ShapeDtypeStruct((t, d), F32)] + [jax.ShapeDtypeStruct((t, d), BF16)] * n_fout
                 + [jax.ShapeDtypeStruct((1, d), F32)] * 2 + ([jax.ShapeDtypeStruct((1, 128), F32)] if use_t else []))
    return pl.pallas_call(
        body, name=name, grid=(t // tm,), in_specs=in_specs, out_specs=out_specs, out_shape=out_shape,
        compiler_params=_cparams(("arbitrary",)),
    )(*ins)


def _conv_fwd(proj, cw, cb, *, name, tm=512):
    t = proj.shape[0]
    c = CONV_CH
    hb = tm // 8

    def body(b_ref, c_ref, h_ref, cp_ref, hp_ref, w_ref, bias_ref, o_ref, q_scr):
        i = pl.program_id(0)
        q = c_ref[...] * h_ref[...]
        halo = jnp.where(i > 0, cp_ref[...] * hp_ref[...], 0.0)
        q_scr[0:8, :] = halo
        q_scr[8:, :] = q
        z = (w_ref[2:3, :] * q + w_ref[1:2, :] * q_scr[pl.ds(7, tm), :] + w_ref[0:1, :] * q_scr[pl.ds(6, tm), :]
             + bias_ref[...])
        o_ref[...] = (b_ref[...] * z).astype(BF16)

    prev = lambda blk: pl.BlockSpec((8, c), lambda i: (jnp.maximum(i * hb - 1, 0), blk))
    return pl.pallas_call(
        body, name=name, grid=(t // tm,),
        in_specs=[_row_spec(tm, c, 0), _row_spec(tm, c, 1), _row_spec(tm, c, 2), prev(1), prev(2),
                  pl.BlockSpec((8, c), lambda i: (0, 0)), _vec_spec(c)],
        out_specs=_row_spec(tm, c),
        out_shape=jax.ShapeDtypeStruct((t, c), BF16),
        scratch_shapes=[pltpu.VMEM((tm + 8, c), F32)],
        compiler_params=_cparams(("parallel",)),
    )(proj, proj, proj, proj, proj, cw, cb)


def _conv_bwd(dyc, proj, cw, cb, *, name, tm=512):
    t = proj.shape[0]
    c = CONV_CH
    hb = tm // 8
    nblk = t // 8

    def body(d_ref, b_ref, c_ref, h_ref, cp_ref, hp_ref, dn_ref, bn_ref, w_ref, bias_ref,
             db_ref, dc_ref, dh_ref, dw_ref, q_scr, z_scr):
        i = pl.program_id(0)
        last = pl.num_programs(0) - 1
        cc = c_ref[...]
        ch = h_ref[...]
        q = cc * ch
        halo = jnp.where(i > 0, cp_ref[...] * hp_ref[...], 0.0)
        q_scr[0:8, :] = halo
        q_scr[8:, :] = q
        w0, w1, w2 = w_ref[0:1, :], w_ref[1:2, :], w_ref[2:3, :]
        qm1 = q_scr[pl.ds(7, tm), :]
        qm2 = q_scr[pl.ds(6, tm), :]
        z = w2 * q + w1 * qm1 + w0 * qm2 + bias_ref[...]
        d = d_ref[...]
        bb = b_ref[...]
        db_ref[...] = (d * z).astype(BF16)
        dz = d * bb
        z_scr[0:tm, :] = dz
        z_scr[tm:, :] = jnp.where(i < last, dn_ref[...] * bn_ref[...], 0.0)
        dq = w2 * dz + w1 * z_scr[pl.ds(1, tm), :] + w0 * z_scr[pl.ds(2, tm), :]
        dc_ref[...] = (dq * ch).astype(BF16)
        dh_ref[...] = (dq * cc).astype(BF16)
        row = lax.broadcasted_iota(jnp.int32, (8, c), 0)
        part = jnp.zeros((8, c), F32)
        for k, term in enumerate((dz * qm2, dz * qm1, dz * q, dz)):
            part = jnp.where(row == k, jnp.sum(term, axis=0, keepdims=True), part)

        @pl.when(i == 0)
        def _():
            dw_ref[...] = part

        @pl.when(i > 0)
        def _():
            dw_ref[...] += part

    prev = lambda blk: pl.BlockSpec((8, c), lambda i: (jnp.maximum(i * hb - 1, 0), blk))
    nxt_p = pl.BlockSpec((8, c), lambda i: (jnp.minimum((i + 1) * hb, nblk - 1), 0))
    nxt_d = pl.BlockSpec((8, c), lambda i: (jnp.minimum((i + 1) * hb, nblk - 1), 0))
    return pl.pallas_call(
        body, name=name, grid=(t // tm,),
        in_specs=[_row_spec(tm, c), _row_spec(tm, c, 0), _row_spec(tm, c, 1), _row_spec(tm, c, 2),
                  prev(1), prev(2), nxt_d, nxt_p, pl.BlockSpec((8, c), lambda i: (0, 0)), _vec_spec(c)],
        out_specs=[_row_spec(tm, c)] * 3 + [pl.BlockSpec((8, c), lambda i: (0, 0))],
        out_shape=[jax.ShapeDtypeStruct((t, c), BF16)] * 3 + [jax.ShapeDtypeStruct((8, c), F32)],
        scratch_shapes=[pltpu.VMEM((tm + 8, c), F32), pltpu.VMEM((tm + 8, c), F32)],
        compiler_params=_cparams(("arbitrary",)),
    )(dyc, proj, proj, proj, proj, proj, dyc, proj, cw, cb)


def _gate_fwd(yconv, glu, proj, *, name, tm=512):
    t, d = yconv.shape

    def body(yc_ref, ga_ref, gb_ref, gc_ref, gs_ref, o_ref):
        yssm = ga_ref[...] * _sigmoid(gb_ref[...])
        o_ref[...] = (_sigmoid(gc_ref[...]) * yc_ref[...] + _sigmoid(gs_ref[...]) * yssm).astype(BF16)

    return pl.pallas_call(
        body, name=name, grid=(t // tm,),
        in_specs=[_row_spec(tm, d), _row_spec(tm, d, 0), _row_spec(tm, d, 1), _row_spec(tm, d, 2), _row_spec(tm, d, 3)],
        out_specs=_row_spec(tm, d), out_shape=jax.ShapeDtypeStruct((t, d), BF16),
        compiler_params=_cparams(("parallel",)),
    )(yconv, glu, glu, proj, proj)


def _gate_bwd(dm, yconv, glu, proj, *, name, tm=512):
    t, d = yconv.shape

    def body(dm_ref, yc_ref, ga_ref, gb_ref, gc_ref, gs_ref, dyc_ref, dglu_ref, dgate_ref):
        dmv = dm_ref[...]
        sc = _sigmoid(gc_ref[...])
        ss = _sigmoid(gs_ref[...])
        sb = _sigmoid(gb_ref[...])
        ga = ga_ref[...]
        yssm = ga * sb
        dyc_ref[...] = (dmv * sc).astype(BF16)
        dgate_ref[:, 0:d] = (dmv * yc_ref[...] * (sc * (1.0 - sc))).astype(BF16)
        dys = dmv * ss
        dgate_ref[:, d:2 * d] = (dmv * yssm * (ss * (1.0 - ss))).astype(BF16)
        dglu_ref[:, 0:d] = (dys * sb).astype(BF16)
        dglu_ref[:, d:2 * d] = (dys * ga * (sb * (1.0 - sb))).astype(BF16)

    return pl.pallas_call(
        body, name=name, grid=(t // tm,),
        in_specs=[_row_spec(tm, d), _row_spec(tm, d), _row_spec(tm, d, 0), _row_spec(tm, d, 1),
                  _row_spec(tm, d, 2), _row_spec(tm, d, 3)],
        out_specs=[_row_spec(tm, d), _row_spec(tm, 2 * d), _row_spec(tm, 2 * d)],
        out_shape=[jax.ShapeDtypeStruct((t, d), BF16), jax.ShapeDtypeStruct((t, 2 * d), BF16),
                   jax.ShapeDtypeStruct((t, 2 * d), BF16)],
        compiler_params=_cparams(("parallel",)),
    )(dm, yconv, glu, glu, proj, proj)


_GELU_C = math.sqrt(2.0 / math.pi)


def _gelu(v):
    return 0.5 * v * (1.0 + jnp.tanh(_GELU_C * (v + 0.044715 * v * v * v)))


def _gelu_grad(v):
    th = jnp.tanh(_GELU_C * (v + 0.044715 * v * v * v))
    return 0.5 * (1.0 + th) + 0.5 * v * (1.0 - th * th) * (_GELU_C * (1.0 + 3.0 * 0.044715 * v * v))


def _cmul(ar, ai, br, bi):
    return ar * br - ai * bi, ar * bi + ai * br


def _scan_fwd(proj, bmat, tab, *, name, plans=()):
    t = proj.shape[0]
    tt, cbw = SCAN_TT, SCAN_CB
    w2 = 2 * cbw

    def body(u_ref, b_ref, tab_ref, s_ref, sb_ref, bu_scr, carry):
        ti = pl.program_id(1)

        @pl.when(ti == 0)
        def _():
            carry[...] = jnp.zeros_like(carry)

        bu_scr[...] = jnp.dot(u_ref[...].astype(BF16), b_ref[...], preferred_element_type=F32)
        row = lax.broadcasted_iota(jnp.int32, (8, cbw), 0)

        def blk(bi, c):
            cr, ci = c
            r0 = pl.multiple_of(bi * 8, 8)
            xr = bu_scr[pl.ds(r0, 8), 0:cbw]
            xi = bu_scr[pl.ds(r0, 8), cbw:w2]
            for k, sh in enumerate((1, 2, 4)):
                kr = tab_ref[k:k + 1, 0:cbw]
                ki = tab_ref[k:k + 1, cbw:w2]
                sr = jnp.where(row >= sh, pltpu.roll(xr, sh, 0), 0.0)
                si = jnp.where(row >= sh, pltpu.roll(xi, sh, 0), 0.0)
                pr, pi = _cmul(kr, ki, sr, si)
                xr = xr + pr
                xi = xi + pi
            pr, pi = _cmul(tab_ref[8:16, 0:cbw], tab_ref[8:16, cbw:w2], cr, ci)
            xr = xr + pr
            xi = xi + pi
            s_ref[pl.ds(r0, 8), 0:cbw] = xr
            s_ref[pl.ds(r0, 8), cbw:w2] = xi
            return (jnp.broadcast_to(xr[7:8, :], (8, cbw)), jnp.broadcast_to(xi[7:8, :], (8, cbw)))

        cr, ci = lax.fori_loop(0, tt // 8, blk, (carry[:, 0:cbw], carry[:, cbw:w2]))
        carry[:, 0:cbw] = cr
        carry[:, cbw:w2] = ci
        sb_ref[...] = s_ref[...].astype(BF16)

    (s, sb), riders = _call_with_plans(
        body, plans, name=name, grid=(SCAN_NCB, t // tt),
        in_specs=[pl.BlockSpec((tt, SCAN_UW), lambda cb, ti: (ti, 3 * SCAN_NCB + cb)),
                  pl.BlockSpec((None, SCAN_UW, w2), lambda cb, ti: (cb, 0, 0)),
                  pl.BlockSpec((16, w2), lambda cb, ti: (0, cb))],
        out_specs=[pl.BlockSpec((tt, w2), lambda cb, ti: (ti, cb))] * 2,
        out_shape=[jax.ShapeDtypeStruct((t, 2 * SSM_CH), F32), jax.ShapeDtypeStruct((t, 2 * SSM_CH), BF16)],
        scratch_shapes=[pltpu.VMEM((tt, w2), F32), pltpu.VMEM((8, w2), F32)],
        semantics=("parallel", "arbitrary"), operands=(proj, bmat, tab))
    return s, sb, riders


def _scan_bwd(dyb, cmat_t, s, tabb, *, name, plans=()):
    t = s.shape[0]
    tt, cbw = SCAN_TT, SCAN_CB
    w2 = 2 * cbw
    nt = t // tt
    hb = tt // 8

    def body(dy_ref, c_ref, s_ref, sp_ref, tab_ref, h_ref, da_ref, g_scr, s_scr, carry):
        ti = pl.program_id(1)

        @pl.when(ti == 0)
        def _():
            carry[...] = jnp.zeros_like(carry)
            da_ref[...] = jnp.zeros_like(da_ref)

        g_scr[...] = jnp.dot(dy_ref[...], c_ref[...], preferred_element_type=F32)
        s_scr[0:8, :] = jnp.where(ti < nt - 1, sp_ref[...], 0.0)
        s_scr[8:, :] = s_ref[...]
        row = lax.broadcasted_iota(jnp.int32, (8, cbw), 0)

        def blk(k, c):
            cr, ci, ar, ai = c
            bi = hb - 1 - k
            r0 = pl.multiple_of(bi * 8, 8)
            xr = g_scr[pl.ds(r0, 8), 0:cbw]
            xi = g_scr[pl.ds(r0, 8), cbw:w2]
            for j, sh in enumerate((1, 2, 4)):
                kr = tab_ref[j:j + 1, 0:cbw]
                ki = tab_ref[j:j + 1, cbw:w2]
                sr = jnp.where(row < 8 - sh, pltpu.roll(xr, 8 - sh, 0), 0.0)
                si = jnp.where(row < 8 - sh, pltpu.roll(xi, 8 - sh, 0), 0.0)
                pr, pi = _cmul(kr, ki, sr, si)
                xr = xr + pr
                xi = xi + pi
            pr, pi = _cmul(tab_ref[8:16, 0:cbw], tab_ref[8:16, cbw:w2], cr, ci)
            xr = xr + pr
            xi = xi + pi
            h_ref[pl.ds(r0, 8), 0:cbw] = xr.astype(BF16)
            h_ref[pl.ds(r0, 8), cbw:w2] = xi.astype(BF16)
            pvr = s_scr[pl.ds(r0, 8), 0:cbw]
            pvi = s_scr[pl.ds(r0, 8), cbw:w2]
            cur_r = s_scr[pl.ds(r0 + 8, 8), 0:cbw]
            cur_i = s_scr[pl.ds(r0 + 8, 8), cbw:w2]
            spr = jnp.where(row == 0, jnp.broadcast_to(pvr[7:8, :], (8, cbw)), pltpu.roll(cur_r, 1, 0))
            spi = jnp.where(row == 0, jnp.broadcast_to(pvi[7:8, :], (8, cbw)), pltpu.roll(cur_i, 1, 0))
            ar = ar + spr * xr + spi * xi
            ai = ai + spr * xi - spi * xr
            return (jnp.broadcast_to(xr[0:1, :], (8, cbw)), jnp.broadcast_to(xi[0:1, :], (8, cbw)), ar, ai)

        z = jnp.zeros((8, cbw), F32)
        cr, ci, ar, ai = lax.fori_loop(0, hb, blk, (carry[:, 0:cbw], carry[:, cbw:w2], z, z))
        carry[:, 0:cbw] = cr
        carry[:, cbw:w2] = ci
        da_ref[:, 0:cbw] += ar
        da_ref[:, cbw:w2] += ai

    rt = lambda ti: nt - 1 - ti
    (h, da), riders = _call_with_plans(
        body, plans, name=name, grid=(SCAN_NCB, nt),
        in_specs=[pl.BlockSpec((tt, SCAN_UW), lambda cb, ti: (rt(ti), cb)),
                  pl.BlockSpec((None, SCAN_UW, w2), lambda cb, ti: (cb, 0, 0)),
                  pl.BlockSpec((tt, w2), lambda cb, ti: (rt(ti), cb)),
                  pl.BlockSpec((8, w2), lambda cb, ti: (jnp.maximum(rt(ti) * hb - 1, 0), cb)),
                  pl.BlockSpec((16, w2), lambda cb, ti: (0, cb))],
        out_specs=[pl.BlockSpec((tt, w2), lambda cb, ti: (rt(ti), cb)),
                   pl.BlockSpec((8, w2), lambda cb, ti: (0, cb))],
        out_shape=[jax.ShapeDtypeStruct((t, 2 * SSM_CH), BF16), jax.ShapeDtypeStruct((8, 2 * SSM_CH), F32)],
        scratch_shapes=[pltpu.VMEM((tt, w2), F32), pltpu.VMEM((tt + 8, w2), F32), pltpu.VMEM((8, w2), F32)],
        semantics=("parallel", "arbitrary"), operands=(dyb, cmat_t, s, s, tabb))
    return h, da, riders


def _s5_out(ymm, proj, dvec, *, name, tm=512):
    t, w = ymm.shape

    def body(y_ref, u_ref, d_ref, yo_ref, sg_ref, ub_ref):
        u = u_ref[...]
        y = y_ref[...] + d_ref[...] * u
        yo_ref[...] = y
        sg_ref[...] = _gelu(y).astype(BF16)
        ub_ref[...] = u.astype(BF16)

    return pl.pallas_call(
        body, name=name, grid=(t // tm,),
        in_specs=[_row_spec(tm, w), _row_spec(tm, w, 3), _vec_spec(w)],
        out_specs=[_row_spec(tm, w)] * 3,
        out_shape=[jax.ShapeDtypeStruct((t, w), F32), jax.ShapeDtypeStruct((t, w), BF16), jax.ShapeDtypeStruct((t, w), BF16)],
        compiler_params=_cparams(("parallel",)),
    )(ymm, proj, dvec)


def _s5_bwd_in(dsg, y, proj, *, name, tm=512):
    t, w = y.shape

    def body(d_ref, y_ref, u_ref, dy_ref, dyb_ref, dd_ref):
        i = pl.program_id(0)
        dy = d_ref[...] * _gelu_grad(y_ref[...])
        dy_ref[...] = dy
        dyb_ref[...] = dy.astype(BF16)
        part = jnp.sum(dy * u_ref[...], axis=0, keepdims=True)

        @pl.when(i == 0)
        def _():
            dd_ref[...] = part

        @pl.when(i > 0)
        def _():
            dd_ref[...] += part

    return pl.pallas_call(
        body, name=name, grid=(t // tm,),
        in_specs=[_row_spec(tm, w), _row_spec(tm, w), _row_spec(tm, w, 3)],
        out_specs=[_row_spec(tm, w), _row_spec(tm, w), _vec_spec(w)],
        out_shape=[jax.ShapeDtypeStruct((t, w), F32), jax.ShapeDtypeStruct((t, w), BF16), jax.ShapeDtypeStruct((1, w), F32)],
        compiler_params=_cparams(("arbitrary",)),
    )(dsg, y, proj)


def _s5_du(dumm, dy, dvec, *, name, tm=512):
    t, w = dy.shape

    def body(a_ref, dy_ref, d_ref, o_ref):
        o_ref[...] = (a_ref[...] + d_ref[...] * dy_ref[...]).astype(BF16)

    return pl.pallas_call(
        body, name=name, grid=(t // tm,), in_specs=[_row_spec(tm, w), _row_spec(tm, w), _vec_spec(w)],
        out_specs=_row_spec(tm, w), out_shape=jax.ShapeDtypeStruct((t, w), BF16),
        compiler_params=_cparams(("parallel",)),
    )(dumm, dy, dvec)


def _s5_discretise(lam_re, lam_im, log_step, b_re, b_im):
    lam = lax.complex(lam_re, lam_im)
    dt = jnp.exp(log_step)[:, None]
    a = jnp.exp(lam * dt)
    bbar = ((a - 1.0) / lam)[..., None] * lax.complex(b_re, b_im)
    return jnp.real(a), jnp.imag(a), jnp.real(bbar), jnp.imag(bbar)


def _perm_cols(re, im):
    lead = re.shape[:-1]
    r = re.reshape(lead + (SCAN_NCB, 1, SCAN_CB))
    i = im.reshape(lead + (SCAN_NCB, 1, SCAN_CB))
    return jnp.concatenate([r, i], axis=-2).reshape(lead + (2 * SSM_CH,))


def _unperm_cols(x):
    lead = x.shape[:-1]
    y = x.reshape(lead + (SCAN_NCB, 2, SCAN_CB))
    return y[..., 0, :].reshape(lead + (SSM_CH,)), y[..., 1, :].reshape(lead + (SSM_CH,))


def _compact(re, im):
    _, r, c = re.shape
    eye = jnp.eye(SCAN_GPB, dtype=re.dtype)

    def half(x):
        x = x.reshape(SCAN_NCB, SCAN_GPB, r, c)
        return (eye[None, :, None, :, None] * x[:, :, :, None, :]).reshape(SCAN_NCB, SCAN_GPB * r, SCAN_GPB * c)

    return jnp.concatenate([half(re), half(im)], axis=-1)


def _compact_extract(x, r):
    c = SSM_STATE
    eye = jnp.eye(SCAN_GPB, dtype=x.dtype)
    y = x.reshape(SCAN_NCB, SCAN_GPB, r, 2, SCAN_GPB, c)
    dg = jnp.sum(y * eye[None, :, None, None, :, None], axis=4).reshape(SSM_GROUPS, r, 2, c)
    return dg[:, :, 0, :], dg[:, :, 1, :]


def _pow_table(ar, ai, descending=False):
    ar = ar.reshape(1, SSM_CH)
    ai = ai.reshape(1, SSM_CH)
    pw = [(ar, ai)]
    for _ in range(7):
        pw.append(_cmul(pw[-1][0], pw[-1][1], ar, ai))
    zero = (jnp.zeros_like(ar), jnp.zeros_like(ar))
    rows = [pw[0], pw[1], pw[3]] + [zero] * 5 + (pw[::-1] if descending else pw)
    re = jnp.concatenate([r for r, _ in rows], axis=0)
    im = jnp.concatenate([i for _, i in rows], axis=0)
    return _perm_cols(re, im)


def _place():
    x, y, c = lax.axis_index("x"), lax.axis_index("y"), lax.axis_index("c")
    chips = [(1 - x, y), (x, 1 - y), (1 - x, 1 - y)]
    return x, y, c, chips


def _dev(px, py, pc):
    return 4 * px + 2 * py + pc


class _Plan:
    def __init__(self, ins, out_shapes, sem_shapes, start, finish, middle=None):
        self.ins, self.out_shapes, self.sem_shapes = list(ins), list(out_shapes), list(sem_shapes)
        self.start, self.finish, self.middle = start, finish, middle


def _split_plan_refs(plans, in_refs, out_refs, sem_refs):
    res, i, o, s = [], 0, 0, 0
    for p in plans:
        ni, no, ns = len(p.ins), len(p.out_shapes), len(p.sem_shapes)
        res.append((in_refs[i:i + ni], out_refs[o:o + no], sem_refs[s:s + ns]))
        i, o, s = i + ni, o + no, s + ns
    return res


def _run_plans(plans, *, name):
    ins = [a for p in plans for a in p.ins]
    outs = [o for p in plans for o in p.out_shapes]
    sems = [s for p in plans for s in p.sem_shapes]
    any_spec = pl.BlockSpec(memory_space=pl.ANY)

    def body(*refs):
        parts = _split_plan_refs(plans, refs[:len(ins)], refs[len(ins):len(ins) + len(outs)], refs[len(ins) + len(outs):])
        for p, r in zip(plans, parts):
            p.start(*r)
        for p, r in zip(plans, parts):
            if p.middle:
                p.middle(*r)
        for p, r in zip(plans, parts):
            p.finish(*r)

    res = pl.pallas_call(body, name=name, in_specs=[any_spec] * len(ins), out_specs=[any_spec] * len(outs),
                         out_shape=outs, scratch_shapes=sems)(*ins)
    return _split_plan_refs(plans, [], res, [])


def _run_plans_on_sequencer(plans, peers_of, *, name, collective_id, after=()):
    ins = [a for p in plans for a in p.ins]
    outs = [o for p in plans for o in p.out_shapes]
    sems = [s for p in plans for s in p.sem_shapes]

    def body(*refs):
        x, y, c, chips = _place()
        peers = peers_of(x, y, c, chips)
        barrier = pltpu.get_barrier_semaphore()
        for peer in peers:
            pl.semaphore_signal(barrier, inc=1, device_id=peer, device_id_type=MESH)
        pl.semaphore_wait(barrier, len(peers))
        n_in = len(ins) + len(after)
        parts = _split_plan_refs(plans, refs[:len(ins)], refs[n_in:n_in + len(outs)], refs[n_in + len(outs):])
        for p, r in zip(plans, parts):
            p.start(*r)
        for p, r in zip(plans, parts):
            if p.middle:
                p.middle(*r)
        for p, r in zip(plans, parts):
            p.finish(*r)

    res = pl.kernel(body, name=name, out_type=outs, mesh=plsc.ScalarSubcoreMesh(axis_name="seq", num_cores=1),
                    scratch_types=sems, compiler_params=pltpu.CompilerParams(collective_id=collective_id))(*ins, *after)
    return _split_plan_refs(plans, [], list(res), [])


def _call_with_plans(body, plans, *, name, grid, in_specs, out_specs, out_shape, scratch_shapes, semantics, operands):
    plans = list(plans)
    if not plans:
        res = pl.pallas_call(body, name=name, grid=grid, in_specs=in_specs, out_specs=out_specs, out_shape=out_shape,
                             scratch_shapes=scratch_shapes, compiler_params=_cparams(semantics))(*operands)
        return list(res), []
    n_in, n_out, n_scr = len(in_specs), len(out_specs), len(scratch_shapes)
    p_ins = [a for p in plans for a in p.ins]
    p_outs = [o for p in plans for o in p.out_shapes]
    p_sems = [s for p in plans for s in p.sem_shapes]
    nsteps = math.prod(grid)
    any_spec = pl.BlockSpec(memory_space=pl.ANY)

    def wrapped(*refs):
        bounds = [n_in, len(p_ins), n_out, len(p_outs), n_scr]
        parts, pos = [], 0
        for b in bounds:
            parts.append(refs[pos:pos + b])
            pos += b
        ins, p_in, outs, p_out, scr = parts
        step = pl.program_id(0)
        for ax in range(1, len(grid)):
            step = step * grid[ax] + pl.program_id(ax)
        riders = _split_plan_refs(plans, p_in, p_out, refs[pos:])

        @pl.when(step == 0)
        def _():
            for p, r in zip(plans, riders):
                p.start(*r)

        mids = [(p, r) for p, r in zip(plans, riders) if p.middle]
        mid_step = nsteps // 2
        split_mid = mids and 0 < mid_step < nsteps - 1
        if split_mid:
            @pl.when(step == mid_step)
            def _():
                for p, r in mids:
                    p.middle(*r)

        body(*ins, *outs, *scr)

        @pl.when(step == nsteps - 1)
        def _():
            if not split_mid:
                for p, r in mids:
                    p.middle(*r)
            for p, r in zip(plans, riders):
                p.finish(*r)

    res = pl.pallas_call(
        wrapped, name=name, grid=grid, in_specs=list(in_specs) + [any_spec] * len(p_ins),
        out_specs=list(out_specs) + [any_spec] * len(p_outs), out_shape=list(out_shape) + p_outs,
        scratch_shapes=list(scratch_shapes) + p_sems, compiler_params=_cparams(("arbitrary",) * len(grid)),
    )(*operands, *p_ins)
    return list(res[:n_out]), [r[1] for r in _split_plan_refs(plans, [], res[n_out:], [])]


def _gather_plan(shards):
    n = len(shards)
    nk = 8

    def make(ins, outs, sems):
        send, recv, lsem = sems
        x, y, c, _ = _place()
        me, sib, xn, yn, dg = (x, y, c), (x, y, 1 - c), (1 - x, y, c), (x, 1 - y, c), (1 - x, 1 - y, c)

        def part(w, block, half):
            ref = outs[w].at[_dev(*block)]
            if half is None:
                return ref
            rows = shards[w].shape[0] // 2
            return ref.at[pl.ds(half * rows, rows)]

        def copy(w, k, block, to, half=None, src=None):
            dst = part(w, block, half)
            return pltpu.make_async_remote_copy(
                src_ref=dst if src is None else src, dst_ref=dst,
                send_sem=send.at[w * nk + k], recv_sem=recv.at[w * nk + k], device_id=to, device_id_type=MESH)

        mine = [pltpu.make_async_copy(ins[w], outs[w].at[_dev(*me)], lsem.at[w]) for w in range(n)]
        return copy, mine, me, sib, xn, yn, dg

    def first_copies(copy, me, sib, xn, yn, ins):
        return [copy(w, k, me, to, src=ins[w]) for w in range(n) for k, to in ((0, sib), (1, xn), (2, yn))]

    def start(ins, outs, sems):
        copy, mine, me, sib, xn, yn, _ = make(ins, outs, sems)
        for cp in mine + first_copies(copy, me, sib, xn, yn, ins):
            cp.start()

    def middle(ins, outs, sems):
        copy, _, me, sib, xn, yn, _ = make(ins, outs, sems)
        for w in range(n):
            copy(w, 1, xn, me).wait_recv()
            copy(w, 3, xn, yn, half=0).start()
            copy(w, 5, xn, sib).start()
        for w in range(n):
            copy(w, 2, yn, me).wait_recv()
            copy(w, 4, yn, xn, half=1).start()
            copy(w, 6, yn, sib).start()

    def finish(ins, outs, sems):
        copy, mine, me, sib, xn, yn, dg = make(ins, outs, sems)
        last = []
        for w in range(n):
            copy(w, 3, dg, me, half=0).wait_recv()
            copy(w, 4, dg, me, half=1).wait_recv()
            fwd = copy(w, 7, dg, sib)
            fwd.start()
            last.append(fwd)
        sx, sy, sd = (1 - me[0], me[1], 1 - me[2]), (me[0], 1 - me[1], 1 - me[2]), (1 - me[0], 1 - me[1], 1 - me[2])
        for w in range(n):
            copy(w, 0, sib, me).wait_recv()
            copy(w, 5, sx, me).wait_recv()
            copy(w, 6, sy, me).wait_recv()
            copy(w, 7, sd, me).wait_recv()
        for cp in first_copies(copy, me, sib, xn, yn, ins) + last:
            cp.wait_send()
        for w in range(n):
            copy(w, 3, xn, yn, half=0).wait_send()
            copy(w, 5, xn, sib).wait_send()
            copy(w, 4, yn, xn, half=1).wait_send()
            copy(w, 6, yn, sib).wait_send()
        for cp in mine:
            cp.wait()

    return _Plan(shards, [jax.ShapeDtypeStruct((N_DEV,) + s.shape, s.dtype) for s in shards],
                 [pltpu.SemaphoreType.DMA((nk * n,)), pltpu.SemaphoreType.DMA((nk * n,)), pltpu.SemaphoreType.DMA((n,))],
                 start, finish, middle)


def _swap_plan(copies_of, n_copies, ins, out_shapes):
    def cps(in_refs, out_refs, sems):
        return copies_of(in_refs, out_refs, sems[0], sems[1])

    def start(in_refs, out_refs, sems):
        for cp in cps(in_refs, out_refs, sems):
            cp.start()

    def finish(in_refs, out_refs, sems):
        all_cps = cps(in_refs, out_refs, sems)
        for cp in all_cps:
            cp.wait_recv()
        for cp in all_cps:
            cp.wait_send()

    return _Plan(ins, out_shapes, [pltpu.SemaphoreType.DMA((n_copies,)), pltpu.SemaphoreType.DMA((n_copies,))],
                 start, finish)


def _sibling_plan(grads):
    n = len(grads)

    def copies(ins, outs, send, recv):
        x, y, c, chips = _place()
        owners = [(x, y)] + chips
        return [pltpu.make_async_remote_copy(
            src_ref=ins[w].at[_dev(*chip, 1 - c)], dst_ref=outs[w].at[k], send_sem=send.at[w * 4 + k],
            recv_sem=recv.at[w * 4 + k], device_id=(x, y, 1 - c), device_id_type=MESH)
            for w in range(n) for k, chip in enumerate(owners)]

    return _swap_plan(copies, 4 * n, grads, [jax.ShapeDtypeStruct((4,) + g.shape[1:], g.dtype) for g in grads])


def _chip_plan(parts, js=(0, 1, 2)):
    n, nj = len(parts), len(js)

    def copies(ins, outs, send, recv):
        x, y, c, chips = _place()
        return [pltpu.make_async_remote_copy(
            src_ref=ins[w].at[j], dst_ref=outs[w * nj + k], send_sem=send.at[w * nj + k],
            recv_sem=recv.at[w * nj + k], device_id=(*chips[j], c), device_id_type=MESH)
            for w in range(n) for k, j in enumerate(js)]

    return _swap_plan(copies, n * nj, parts,
                      [jax.ShapeDtypeStruct(p.shape[1:], p.dtype) for p in parts for _ in js])


UPDATE_TILE_BYTES = 1536 * 1024


def _row_tile(r, c):
    best = 8
    for t in range(8, r + 1, 8):
        if r % t == 0 and t * c * 4 <= UPDATE_TILE_BYTES:
            best = t
    return best


def _chip_partial(g, sib, ids, *, name):
    _, r, c = g.shape
    tr = _row_tile(r, c)

    def body(ids_ref, g_ref, s_ref, o_ref):
        o_ref[...] = (g_ref[...] + s_ref[...]).astype(BF16)

    return pl.pallas_call(
        body, name=name,
        grid_spec=pltpu.PrefetchScalarGridSpec(
            num_scalar_prefetch=1, grid=(3, r // tr),
            in_specs=[pl.BlockSpec((None, tr, c), lambda j, i, ids_ref: (ids_ref[j], i, 0)),
                      pl.BlockSpec((None, tr, c), lambda j, i, ids_ref: (j + 1, i, 0))],
            out_specs=pl.BlockSpec((None, tr, c), lambda j, i, ids_ref: (j, i, 0))),
        out_shape=jax.ShapeDtypeStruct((3, r, c), BF16),
        compiler_params=_cparams(("parallel", "parallel")),
    )(ids, g, sib)


def _adamw_math(w, g, m, v):
    m = ADAM_B1 * m + (1.0 - ADAM_B1) * g
    v = ADAM_B2 * v + (1.0 - ADAM_B2) * (g * g)
    m_hat = m / (1.0 - ADAM_B1 ** ADAM_STEP)
    v_hat = v / (1.0 - ADAM_B2 ** ADAM_STEP)
    delta = -ADAM_LR * (m_hat / (jnp.sqrt(v_hat) + ADAM_EPS) + ADAM_WD * w)
    return delta, m, v


def _shard_update(g, sib, rem, me, w, m, v, *, name):
    r, c = w.shape
    tr = _row_tile(r, c)

    def body(me_ref, g_ref, s_ref, r0_ref, r1_ref, r2_ref, w_ref, m_ref, v_ref, go_ref, d_ref, mo_ref, vo_ref):
        gt = g_ref[...] + s_ref[...]
        gt = gt + r0_ref[...].astype(F32)
        gt = gt + r1_ref[...].astype(F32)
        gt = gt + r2_ref[...].astype(F32)
        go_ref[...] = gt
        d, mn, vn = _adamw_math(w_ref[...], gt, m_ref[...], v_ref[...])
        d_ref[...] = d
        mo_ref[...] = mn
        vo_ref[...] = vn

    blk = lambda k: pl.BlockSpec((None, tr, c), lambda i, me_ref: (k, i, 0))
    plain = pl.BlockSpec((tr, c), lambda i, me_ref: (i, 0))
    return pl.pallas_call(
        body, name=name,
        grid_spec=pltpu.PrefetchScalarGridSpec(
            num_scalar_prefetch=1, grid=(r // tr,),
            in_specs=[pl.BlockSpec((None, tr, c), lambda i, me_ref: (me_ref[0], i, 0)), blk(0), plain, plain, plain,
                      plain, plain, plain],
            out_specs=[plain] * 4),
        out_shape=[jax.ShapeDtypeStruct((r, c), F32)] * 4,
        compiler_params=_cparams(("parallel",)),
    )(me, g, sib, *rem, w, m, v)


def _small_update(gathered, w, m, v, *, name):
    _, r, c = gathered.shape

    def body(g_ref, w_ref, m_ref, v_ref, go_ref, d_ref, mo_ref, vo_ref):
        gt = g_ref[0]
        for k in range(1, N_DEV):
            gt = gt + g_ref[k]
        go_ref[...] = gt
        d, mn, vn = _adamw_math(w_ref[...], gt, m_ref[...], v_ref[...])
        d_ref[...] = d
        mo_ref[...] = mn
        vo_ref[...] = vn

    return pl.pallas_call(
        body, name=name, out_shape=[jax.ShapeDtypeStruct((r, c), F32)] * 4,
        compiler_params=pltpu.CompilerParams(vmem_limit_bytes=VMEM_LIMIT),
    )(gathered, w, m, v)


SMALL_UNIT = 1024


def _pack(parts):
    flat = []
    for p in parts:
        f = p.reshape(-1).astype(F32)
        pad = (-f.shape[0]) % SMALL_UNIT
        flat.append(jnp.pad(f, (0, pad)))
    return jnp.concatenate(flat).reshape(-1, 128)


def _unpack(buf, shapes):
    flat = buf.reshape(-1)
    out, off = [], 0
    for s in shapes:
        nel = math.prod(s)
        out.append(flat[off:off + nel].reshape(s))
        off += nel + ((-nel) % SMALL_UNIT)
    return out


def kernel(x, p, ffn1_w_in, ffn1_w_out, ln1_g, ln1_b, mix_w_in, conv_w, conv_b, conv_w_out, ssm_lam_re, ssm_lam_im, ssm_log_step, ssm_b_re, ssm_b_im, ssm_c_re, ssm_c_im, ssm_d, ssm_w_glu, mix_w_out, ln2_g, ln2_b, ffn2_w_in, ffn2_w_out, ln3_g, ln3_b, ple_w_in, ple_w_gate, ln4_g, ln4_b, loss_target, m_ffn1_w_in, m_ffn1_w_out, m_ln1_g, m_ln1_b, m_mix_w_in, m_conv_w, m_conv_b, m_conv_w_out, m_ssm_lam_re, m_ssm_lam_im, m_ssm_log_step, m_ssm_b_re, m_ssm_b_im, m_ssm_c_re, m_ssm_c_im, m_ssm_d, m_ssm_w_glu, m_mix_w_out, m_ln2_g, m_ln2_b, m_ffn2_w_in, m_ffn2_w_out, m_ln3_g, m_ln3_b, m_ple_w_in, m_ple_w_gate, m_ln4_g, m_ln4_b, v_ffn1_w_in, v_ffn1_w_out, v_ln1_g, v_ln1_b, v_mix_w_in, v_conv_w, v_conv_b, v_conv_w_out, v_ssm_lam_re, v_ssm_lam_im, v_ssm_log_step, v_ssm_b_re, v_ssm_b_im, v_ssm_c_re, v_ssm_c_im, v_ssm_d, v_ssm_w_glu, v_mix_w_out, v_ln2_g, v_ln2_b, v_ffn2_w_in, v_ffn2_w_out, v_ln3_g, v_ln3_b, v_ple_w_in, v_ple_w_gate, v_ln4_g, v_ln4_b):
    args = dict(locals())
    big = ['ffn1_w_in', 'ffn1_w_out', 'mix_w_in', 'conv_w_out', 'ssm_w_glu', 'mix_w_out',
           'ffn2_w_in', 'ffn2_w_out', 'ple_w_in', 'ple_w_gate']
    small = ['ln1_g', 'ln1_b', 'conv_b', 'ssm_lam_re', 'ssm_lam_im', 'ssm_log_step', 'ssm_b_re', 'ssm_b_im',
             'ssm_c_re', 'ssm_c_im', 'ssm_d', 'ln2_g', 'ln2_b', 'ln3_g', 'ln3_b', 'ln4_g', 'ln4_b']
    order = ['ffn1_w_in', 'ffn1_w_out', 'ln1_g', 'ln1_b', 'mix_w_in', 'conv_w', 'conv_b', 'conv_w_out',
             'ssm_lam_re', 'ssm_lam_im', 'ssm_log_step', 'ssm_b_re', 'ssm_b_im', 'ssm_c_re', 'ssm_c_im', 'ssm_d',
             'ssm_w_glu', 'mix_w_out', 'ln2_g', 'ln2_b', 'ffn2_w_in', 'ffn2_w_out', 'ln3_g', 'ln3_b',
             'ple_w_in', 'ple_w_gate', 'ln4_g', 'ln4_b']

    t = x.shape[1]
    d = x.shape[2]
    xc_, yc_, cc_ = lax.axis_index("x"), lax.axis_index("y"), lax.axis_index("c")
    me = (4 * xc_ + 2 * yc_ + cc_).astype(jnp.int32)
    cw_cols = conv_w.shape[2]

    turned = ('ffn1_w_in', 'ffn2_w_in')

    def local(a, nm):
        return jnp.swapaxes(a[0], 0, 1) if nm in turned else a[0]

    shard = {nm: local(args[nm], nm).astype(BF16) for nm in big}
    cw_pad = jnp.zeros((16, 128), F32).at[0:3, 0:cw_cols].set(conv_w[0])
    wf = shard['ffn1_w_in'].shape[0]

    def tie(chain, *others):
        out = lax.optimization_barrier((chain, *others))
        return out[0], out[1:]

    def gather(arrays, tag, after=()):
        ((_, got, _),) = _run_plans_on_sequencer(
            [_gather_plan(arrays)], lambda x, y, c, chips: [(x, y, 1 - c), (1 - x, y, c), (x, 1 - y, c)],
            name=f"gather_{tag}", collective_id=2, after=after)
        return got

    w1in, cw_g = gather([shard['ffn1_w_in'], cw_pad], "ffn1_in")
    (w1out_g,) = gather([shard['ffn1_w_out']], "ffn1_out")
    (wmix,) = gather([shard['mix_w_in']], "mix_in")
    wco, wglu, wmo_g = gather([shard[nm] for nm in ('conv_w_out', 'ssm_w_glu', 'mix_w_out')], "mix_rest")
    (w2in,) = gather([shard['ffn2_w_in']], "ffn2_in")
    (w2out_g,) = gather([shard['ffn2_w_out']], "ffn2_out")
    wpin, wgate_g = gather([shard['ple_w_in'], shard['ple_w_gate']], "ple")
    cw_full = jnp.transpose(cw_g[:, 0:3, 0:cw_cols], (1, 0, 2)).reshape(3, N_DEV * cw_cols)
    cw8 = jnp.zeros((8, CONV_CH), F32).at[0:3, :].set(cw_full)

    s5_in = (ssm_lam_re[0], ssm_lam_im[0], ssm_log_step[0], ssm_b_re[0], ssm_b_im[0])
    (a_re, a_im, bb_re, bb_im), s5_vjp = jax.vjp(_s5_discretise, *s5_in)
    tab_f = _pow_table(a_re, a_im)
    tab_b = _pow_table(a_re, -a_im, descending=True)
    bmat_b = _compact(jnp.transpose(bb_re, (0, 2, 1)), jnp.transpose(bb_im, (0, 2, 1))).astype(BF16)
    cmat_tb = _compact(ssm_c_re[0], -ssm_c_im[0]).astype(BF16)
    dvec = ssm_d[0].reshape(1, SSM_W)

    xf = x[0]
    x_b = xf.astype(BF16)
    p_b = p[0, 0].astype(BF16)
    tgt = loss_target[0]
    tq = min(512, t)

    ffn_out = dict(ja='c', jb='c', nj=4, tm=tq, tn=d, tk=wf)
    def side_by_side(wb):
        return jnp.transpose(wb, (1, 0, 2)).reshape(wb.shape[1], N_DEV * wb.shape[2])

    tf = min(1024, t)
    a1, h1, _ = _ffn_in(x_b, w1in, name="ffn1_in", tm=tf)
    w1out = w1out_g.reshape(4, wf, d)
    r1, x1, x1b = _mm(a1, w1out, name="ffn1_out", **ffn_out, epilogue=_ln_epilogue(xf, ln1_g, ln1_b, 0.5))
    proj = _mm(x1b, wmix, name="mix_in", jb='b', jo='b', o_flat=True, nj=8, tm=tq, tn=512, tk=d)
    _, (wco, wglu, wmo_g) = tie(proj, wco, wglu, wmo_g)
    wmo = wmo_g.reshape(d, d)
    ycin = _conv_fwd(proj, cw8, conv_b, name="conv_fwd")
    wco, wglu = side_by_side(wco), side_by_side(wglu)
    yconv = _mm(ycin, wco, name="conv_out", tm=tq, tn=d, tk=CONV_CH)
    s_f, s_b, _ = _scan_fwd(proj, bmat_b, tab_f, name="scan_fwd")
    blk = dict(ja='b', jb='b', jo='b', nj=SCAN_NCB)
    ymm = _mm(s_b, cmat_tb, name="ssm_read", a_flat=True, o_flat=True, tb=True, tm=tq, tn=SCAN_UW, tk=2 * SCAN_CB, **blk)
    ys, sg, u_b = _s5_out(ymm, proj, dvec, name="ssm_out")
    glu = _mm(sg, wglu, name="glu_in", tm=tq, tn=d, tk=SSM_W)
    merged = _gate_fwd(yconv, glu, proj, name="gate_fwd")
    r2, x2, x2b = _mm(merged, wmo, name="mix_out", tm=tq, tn=d, tk=d, epilogue=_ln_epilogue(x1, ln2_g, ln2_b, 1.0))
    a2, h2, _ = _ffn_in(x2b, w2in, name="ffn2_in", tm=tf)
    w2out = w2out_g.reshape(4, wf, d)
    r3, x3, x3b = _mm(a2, w2out, name="ffn2_out", **ffn_out, epilogue=_ln_epilogue(x2, ln3_g, ln3_b, 0.5))
    wgate = wgate_g.reshape(d, d)
    _, (wpin,) = tie(x2b, wpin)
    pe = _mm(p_b, side_by_side(wpin), name="ple_in", tm=tq, tn=d, tk=p_b.shape[1])
    gp = _mm(x3b, wgate, name="ple_gate", tm=tq, tn=d, tk=d)

    dr4, dpe_b, dgp_b, dg4, db4, loss_part = _ln_bwd(x3, [pe, gp], ln4_g, ln4_b, [], name="ple_ln_bwd",
                                                     ple=True, target=tgt)
    gb, sib, rem = {}, {}, {}
    ids = jnp.stack([_dev(1 - xc_, yc_, cc_), _dev(xc_, 1 - yc_, cc_), _dev(1 - xc_, 1 - yc_, cc_)]).astype(jnp.int32)

    def blocked(nm, g):
        return g.reshape((N_DEV,) + args[nm].shape[1:])

    def to_sibling(*names):
        return _sibling_plan([gb[nm] for nm in names])

    def chip_sums(names, sibs):
        for nm, s in zip(names, sibs):
            sib[nm] = s
        return [_chip_partial(gb[nm], sib[nm], ids, name=f"chip_sum_{nm}") for nm in names]

    def on_sequencer(plan, peers, tag, cid, after=()):
        ((_, got, _),) = _run_plans_on_sequencer([plan], peers, name=tag, collective_id=cid, after=after)
        return got

    waiting = []
    arrived = []

    def send_sibling(*names, after=()):
        got = on_sequencer(to_sibling(*names), lambda x, y, c, chips: [(x, y, 1 - c)], f"grad_sibling_{names[0]}", 3,
                           after=after)
        waiting.append((names, got))

    def send_chips(chain, count=None):
        for _ in range(len(waiting) if count is None else count):
            names, got = waiting.pop(0)
            sums = chip_sums(names, got)
            chain, rest = tie(chain, *sums, *arrived)
            r_ = on_sequencer(_chip_plan(list(rest[:len(sums)])), lambda x, y, c, chips: [(*chip, c) for chip in chips],
                              f"grad_chips_{names[0]}", 1)
            arrived[:] = list(r_)
            for i, nm in enumerate(names):
                rem[nm] = r_[3 * i:3 * i + 3]
        return chain

    ffn_in_dg = dict(ja='c', jb='c', nj=8, tm=tq, tn=d, tk=wf)
    ffn_in_wg = dict(ja='b', jo='b', ta=True, nj=8, tm=wf, tn=d, tk=t)
    ffn_out_wg = dict(ja='b', jo='b', ta=True, nj=4, tm=wf, tn=d, tk=t)

    gb['ple_w_in'] = _mm(p_b, dpe_b, name="ple_in_wg", jb='b', jo='b', b_flat=True, ta=True, nj=8,
                         tm=p_b.shape[1], tn=128, tk=t)
    gb['ple_w_gate'] = blocked('ple_w_gate', _mm(x3b, dgp_b, name="ple_gate_wg", ta=True, tm=d, tn=d, tk=t))
    g_ple = ('ple_w_in', 'ple_w_gate')
    dx3_g, (s_,) = _mm(dgp_b, wgate, name="ple_gate_dg", tb=True, tm=tq, tn=d, tk=d, plans=[to_sibling(*g_ple)])
    waiting.append((g_ple, s_))

    dr3, df2_b, dg3, db3 = _ln_bwd(r3, [], ln3_g, ln3_b, [(dr4, ALPHA), (dx3_g, 1.0)], name="ffn2_ln_bwd", fs=0.5)
    df2_b = send_chips(df2_b)
    gb['ffn2_w_out'] = blocked('ffn2_w_out', _mm(a2, df2_b, name="ffn2_out_wg", **ffn_out_wg))
    df2_b, (gb['ffn2_w_out'],) = tie(df2_b, gb['ffn2_w_out'])
    send_sibling('ffn2_w_out')
    dh2, _ = _ffn_out_dg(df2_b, w2out, h2, name="ffn2_out_dg", tm=tf)
    dx2_f = _mm(dh2, w2in, name="ffn2_in_dg", **ffn_in_dg)
    dx2_f = send_chips(dx2_f)
    gb['ffn2_w_in'] = _mm(dh2, x2b, name="ffn2_in_wg", **ffn_in_wg)
    dx2_f, (gb['ffn2_w_in'],) = tie(dx2_f, gb['ffn2_w_in'])

    dr2, dmix_b, dg2, db2 = _ln_bwd(r2, [], ln2_g, ln2_b, [(dr3, ALPHA), (dx2_f, 1.0)], name="mix_ln_bwd")
    dmerged = _mm(dmix_b, wmo, name="mix_out_dg", tb=True, tm=tq, tn=d, tk=d)
    send_sibling('ffn2_w_in', after=(dmerged,))
    gb['mix_w_out'] = blocked('mix_w_out', _mm(merged, dmix_b, name="mix_out_wg", ta=True, tm=d, tn=d, tk=t))
    dyconv_b, dglu_b, dgate_b = _gate_bwd(dmerged, yconv, glu, proj, name="gate_bwd")
    gb['conv_w_out'] = _mm(ycin, dyconv_b, name="conv_out_wg", jb='b', jo='b', b_flat=True, ta=True, nj=8,
                           tm=CONV_CH, tn=128, tk=t)
    dycin = _mm(dyconv_b, wco, name="conv_out_dg", tb=True, tm=tq, tn=CONV_CH, tk=d)
    gb['ssm_w_glu'] = _mm(sg, dglu_b, name="glu_in_wg", jb='b', jo='b', b_flat=True, ta=True, nj=8,
                          tm=SSM_W, tn=256, tk=t)
    dsg = _mm(dglu_b, wglu, name="glu_in_dg", tb=True, tm=tq, tn=SSM_W, tk=2 * d)
    dsg = send_chips(dsg)
    dys, dys_b, dd = _s5_bwd_in(dsg, ys, proj, name="ssm_out_bwd")
    h_b, da_acc, _ = _scan_bwd(dys_b, cmat_tb, s_f, tab_b, name="scan_bwd")
    send_sibling('mix_w_out', 'conv_w_out', 'ssm_w_glu', after=(h_b,))
    dumm = _mm(h_b, bmat_b, name="ssm_write_dg", a_flat=True, o_flat=True, tb=True, tm=tq, tn=SCAN_UW,
               tk=2 * SCAN_CB, **blk)
    du_b = _s5_du(dumm, dys, dvec, name="ssm_du")
    g_bmat = _mm(u_b, h_b, name="ssm_write_wg", a_flat=True, b_flat=True, ta=True, tm=SCAN_UW,
                 tn=2 * SCAN_CB, tk=t, **blk)
    g_cmat = _mm(dys_b, s_b, name="ssm_read_wg", a_flat=True, b_flat=True, ta=True, tm=SCAN_UW,
                 tn=2 * SCAN_CB, tk=t, **blk)
    dcb_b, dcc_b, dch_b, dconv = _conv_bwd(dycin, proj, cw8, conv_b, name="conv_bwd")
    dproj = jnp.concatenate([dcb_b, dcc_b, dch_b, du_b, dgate_b], axis=1)
    dproj = send_chips(dproj)
    gb['mix_w_in'] = _mm(x1b, dproj, name="mix_in_wg", jb='b', jo='b', b_flat=True, ta=True, nj=8,
                         tm=d, tn=512, tk=t)
    send_sibling('mix_w_in')
    dx1_m = _mm(dproj, wmix, name="mix_in_dg", ja='c', jb='c', a_flat=True, tb=True, nj=8, tm=tq, tn=d, tk=512)
    dx1_m, (gb['mix_w_in'],) = tie(dx1_m, gb['mix_w_in'])

    dr1, df1_b, dg1, db1 = _ln_bwd(r1, [], ln1_g, ln1_b, [(dr2, ALPHA), (dx1_m, 1.0)], name="ffn1_ln_bwd", fs=0.5)
    df1_b = send_chips(df1_b, 1)
    gb['ffn1_w_out'] = blocked('ffn1_w_out', _mm(a1, df1_b, name="ffn1_out_wg", **ffn_out_wg))
    df1_b, (gb['ffn1_w_out'],) = tie(df1_b, gb['ffn1_w_out'])
    send_sibling('ffn1_w_out')
    dh1, _ = _ffn_out_dg(df1_b, w1out, h1, name="ffn1_out_dg", tm=tf)
    dh1 = send_chips(dh1, 1)
    gb['ffn1_w_in'] = _mm(dh1, x_b, name="ffn1_in_wg", **ffn_in_wg)
    dh1, (gb['ffn1_w_in'],) = tie(dh1, gb['ffn1_w_in'])
    send_sibling('ffn1_w_in')
    (grad_x,) = _mm(dh1, w1in, name="ffn1_in_dg", **ffn_in_dg,
                    epilogue=(lambda pr, drv: (pr + ALPHA * drv,), (dr1,), (), (F32,)))

    da_sum = jnp.sum(da_acc, axis=0)
    da_re, da_im = _unperm_cols(da_sum)
    gbb_re, gbb_im = [jnp.transpose(v, (0, 2, 1)) for v in _compact_extract(g_bmat, SSM_GROUP)]
    g_c_re, g_c_im_neg = _compact_extract(g_cmat, SSM_GROUP)
    g_c_im = -g_c_im_neg
    g_lam_re, g_lam_im, g_log_step, g_b_re, g_b_im = s5_vjp(
        (da_re.reshape(SSM_GROUPS, SSM_STATE), da_im.reshape(SSM_GROUPS, SSM_STATE), gbb_re, gbb_im))
    g_d = dd.reshape(SSM_GROUPS, SSM_GROUP)

    small_g = {'ln1_g': dg1, 'ln1_b': db1, 'conv_b': dconv[3:4], 'ssm_lam_re': g_lam_re, 'ssm_lam_im': g_lam_im,
               'ssm_log_step': g_log_step, 'ssm_b_re': g_b_re, 'ssm_b_im': g_b_im, 'ssm_c_re': g_c_re,
               'ssm_c_im': g_c_im, 'ssm_d': g_d, 'ln2_g': dg2, 'ln2_b': db2, 'ln3_g': dg3, 'ln3_b': db3,
               'ln4_g': dg4, 'ln4_b': db4}
    small_shapes = [args[nm].shape for nm in small] + [(3, CONV_CH), (1,)]
    g_pack = _pack([small_g[nm] for nm in small] + [dconv[0:3], loss_part[0:1, 0:1]])

    res = {}
    me1, _ = tie(me.reshape(1), grad_x)
    me1 = send_chips(me1, 1)

    def update(nm):
        upd = _shard_update(gb[nm], sib[nm], rem[nm], me1, local(args[nm], nm), local(args['m_' + nm], nm),
                            local(args['v_' + nm], nm), name=f"update_{nm}")
        for key, val in zip(('grad_', 'delta_', 'new_m_', 'new_v_'), upd):
            res[key + nm] = (jnp.swapaxes(val, 0, 1) if nm in turned else val)[None]

    (g_all,) = gather([g_pack], "small", after=(dh1,))
    last = ['ffn1_w_out', 'ffn1_w_in']
    for nm in big:
        if nm not in last:
            update(nm)

    def full_cw(a):
        return lax.dynamic_update_slice(jnp.zeros((3, CONV_CH), F32), a[0], (0, me * cw_cols))

    zero1 = jnp.zeros((1,), F32)
    w_pack = _pack([args[nm] for nm in small] + [full_cw(conv_w), zero1])
    m_pack = _pack([args['m_' + nm] for nm in small] + [full_cw(m_conv_w), zero1])
    v_pack = _pack([args['v_' + nm] for nm in small] + [full_cw(v_conv_w), zero1])
    sg_sum, sd, sm, sv = _small_update(g_all, w_pack, m_pack, v_pack, name="small_update")
    for key, buf in (('grad_', sg_sum), ('delta_', sd), ('new_m_', sm), ('new_v_', sv)):
        parts = _unpack(buf, small_shapes)
        for nm, val in zip(small, parts[:len(small)]):
            res[key + nm] = val
        res[key + 'conv_w'] = lax.dynamic_slice(parts[len(small)], (0, me * cw_cols), (3, cw_cols))[None]
        if key == 'grad_':
            loss = parts[-1][0]

    for nm in last:
        update(nm)

    outs = [loss, grad_x[None]]
    for key in ('grad_', 'delta_', 'new_m_', 'new_v_'):
        outs += [res[key + nm] for nm in order]
    return tuple(outs)
```

```python
import functools
import math

import jax
import jax.numpy as jnp
from jax import lax
from jax.experimental import pallas as pl
from jax.experimental.pallas import tpu as pltpu
from jax.experimental.pallas import tpu_sc as plsc

F32 = jnp.float32
BF16 = jnp.bfloat16
MESH = pl.DeviceIdType.MESH

N_DEV = 8
ALPHA = 2.0 ** 0.25
LN_EPS = 1e-5
CONV_CH = 512
SSM_W = 512
SSM_GROUPS = 32
SSM_GROUP = 16
SSM_STATE = 64
SSM_CH = SSM_GROUPS * SSM_STATE
SCAN_CB = 512
SCAN_NCB = SSM_CH // SCAN_CB
SCAN_GPB = SSM_GROUPS // SCAN_NCB
SCAN_UW = SCAN_GPB * SSM_GROUP
SCAN_TT = 256
ADAM_LR = 0.001
ADAM_B1 = 0.9
ADAM_B2 = 0.999
ADAM_EPS = 1e-08
ADAM_WD = 0.01
ADAM_STEP = 10
VMEM_LIMIT = 56 * 1024 * 1024


def _cparams(sem=None, **kw):
    return pltpu.CompilerParams(dimension_semantics=sem, vmem_limit_bytes=VMEM_LIMIT, **kw)


def _mm(a, b, *, name, ja=None, jb=None, jo=None, a_flat=False, b_flat=False, o_flat=False,
        ta=False, tb=False, tm, tn, tk, nj=1, out_dtype=F32, plans=(), epilogue=None):
    def dims(arr, j, flat):
        if j is None:
            return arr.shape
        if flat:
            return (arr.shape[0], arr.shape[1] // nj)
        assert arr.shape[0] == nj, (name, arr.shape, nj)
        return arr.shape[1:]

    ar, ac = dims(a, ja, a_flat)
    br, bc = dims(b, jb, b_flat)
    m, k = (ac, ar) if ta else (ar, ac)
    k2, n = (bc, br) if tb else (br, bc)
    assert k == k2, (name, a.shape, b.shape)
    assert m % tm == 0 and n % tn == 0 and k % tk == 0, (name, m, n, k, tm, tn, tk)
    njb = nj if 'b' in (ja, jb) else 1
    njc = nj if 'c' in (ja, jb) else 1
    nk = k // tk
    j_inside = njc > 1 and nk == 1 and not ta
    n_in = njc if j_inside else 1
    nred = nk if j_inside else njc * nk
    grid = (njb, m // tm, n // tn, 1 if j_inside else njc, nk)

    def make_spec(j, flat, blk, rfn, cfn, cols_per_j):
        def jsel(g, c):
            return g if j == 'b' else c
        if j is None:
            return pl.BlockSpec(blk, lambda g, i, jn, c, kk: (rfn(i, jn, kk), cfn(i, jn, kk)))
        if j == 'c' and j_inside:
            if flat:
                return pl.BlockSpec((blk[0], nj * blk[1]), lambda g, i, jn, c, kk: (rfn(i, jn, kk), 0))
            return pl.BlockSpec((nj,) + blk, lambda g, i, jn, c, kk: (0, rfn(i, jn, kk), cfn(i, jn, kk)))
        if flat:
            nb = cols_per_j // blk[1]
            return pl.BlockSpec(blk, lambda g, i, jn, c, kk: (rfn(i, jn, kk), jsel(g, c) * nb + cfn(i, jn, kk)))
        return pl.BlockSpec((None,) + blk,
                            lambda g, i, jn, c, kk: (jsel(g, c), rfn(i, jn, kk), cfn(i, jn, kk)))

    if ta:
        a_spec = make_spec(ja, a_flat, (tk, tm), lambda i, jn, kk: kk, lambda i, jn, kk: i, ac)
    else:
        a_spec = make_spec(ja, a_flat, (tm, tk), lambda i, jn, kk: i, lambda i, jn, kk: kk, ac)
    if tb:
        b_spec = make_spec(jb, b_flat, (tn, tk), lambda i, jn, kk: jn, lambda i, jn, kk: kk, bc)
    else:
        b_spec = make_spec(jb, b_flat, (tk, tn), lambda i, jn, kk: kk, lambda i, jn, kk: jn, bc)
    o_spec = make_spec(jo, o_flat, (tm, tn), lambda i, jn, kk: i, lambda i, jn, kk: jn, n)
    if jo is None:
        out_shape = (m, n)
    elif o_flat:
        out_shape = (m, nj * n)
    else:
        out_shape = (nj, m, n)

    dn = (((0 if ta else 1,), (1 if tb else 0,)), ((), ()))

    def operand(ref, j, flat, jj, width):
        if not (j == 'c' and j_inside):
            return ref[...]
        return ref[:, jj * width:(jj + 1) * width] if flat else ref[jj]

    e_fn, e_rows, e_vecs, e_dtypes = epilogue if epilogue else (None, (), (), (out_dtype,))
    assert not epilogue or (nred == 1 and jo is None), name
    n_e = len(e_rows) + len(e_vecs)
    n_o = len(e_dtypes)

    def body(a_ref, b_ref, *rest):
        e_refs, o_refs, scratch = rest[:n_e], rest[n_e:n_e + n_o], rest[n_e + n_o:]
        o_ref = o_refs[0]
        p = None
        for jj in range(n_in):
            q = lax.dot_general(operand(a_ref, ja, a_flat, jj, tk), operand(b_ref, jb, b_flat, jj, tk if tb else tn),
                                dn, preferred_element_type=F32)
            p = q if p is None else p + q
        if epilogue:
            for ref, val in zip(o_refs, e_fn(p, *[r[...] for r in e_refs])):
                ref[...] = val.astype(ref.dtype)
        elif nred == 1:
            o_ref[...] = p.astype(o_ref.dtype)
        else:
            acc = scratch[0]
            r = pl.program_id(3) * nk + pl.program_id(4)

            @pl.when(r == 0)
            def _():
                acc[...] = p

            @pl.when(r > 0)
            def _():
                acc[...] += p

            @pl.when(r == nred - 1)
            def _():
                o_ref[...] = acc[...].astype(o_ref.dtype)

    vec_spec = pl.BlockSpec((1, tn), lambda g, i, jn, c, kk: (0, jn))
    res = _call_with_plans(
        body, plans, name=name, grid=grid,
        in_specs=[a_spec, b_spec] + [o_spec] * len(e_rows) + [vec_spec] * len(e_vecs), out_specs=[o_spec] * n_o,
        out_shape=[jax.ShapeDtypeStruct(out_shape, dt) for dt in e_dtypes],
        scratch_shapes=[] if nred == 1 else [pltpu.VMEM((tm, tn), F32)],
        semantics=("parallel", "parallel", "parallel", "arbitrary", "arbitrary"), operands=(a, b, *e_rows, *e_vecs))
    outs = res[0] if epilogue else res[0][0]
    return (outs, res[1]) if plans else outs


def _sigmoid(v):
    return jax.nn.sigmoid(v)


def _row_spec(tm, cols, colblk=0):
    return pl.BlockSpec((tm, cols), lambda i: (i, colblk))


def _vec_spec(cols):
    return pl.BlockSpec((1, cols), lambda i: (0, 0))


def _ffn_in(xb, win, *, name, tm, plans=()):
    t, d = xb.shape
    nj, w, _ = win.shape
    half = nj // 2
    dn = (((1,), (1,)), ((), ()))

    def body(x_ref, wg_ref, wu_ref, a_ref, gu_ref):
        xv = x_ref[...]
        g = lax.dot_general(xv, wg_ref[...], dn, preferred_element_type=F32)
        u = lax.dot_general(xv, wu_ref[...], dn, preferred_element_type=F32)
        a_ref[...] = (g * _sigmoid(g) * u).astype(BF16)
        gu_ref[0] = g.astype(BF16)
        gu_ref[1] = u.astype(BF16)

    (a, gu), riders = _call_with_plans(
        body, plans, name=name, grid=(half, t // tm),
        in_specs=[pl.BlockSpec((tm, d), lambda j, i: (i, 0)),
                  pl.BlockSpec((None, w, d), lambda j, i: (j, 0, 0)),
                  pl.BlockSpec((None, w, d), lambda j, i: (j + half, 0, 0))],
        out_specs=[pl.BlockSpec((None, tm, w), lambda j, i: (j, i, 0)),
                   pl.BlockSpec((2, None, tm, w), lambda j, i: (0, j, i, 0))],
        out_shape=[jax.ShapeDtypeStruct((half, t, w), BF16), jax.ShapeDtypeStruct((2, half, t, w), BF16)],
        scratch_shapes=[], semantics=("parallel", "parallel"), operands=(xb, win, win))
    return a, gu, riders


def _ffn_out_dg(dfb, wout, gu, *, name, tm, plans=()):
    t, d = dfb.shape
    half, w, _ = wout.shape
    dn = (((1,), (1,)), ((), ()))

    def body(df_ref, w_ref, gu_ref, dh_ref):
        da = lax.dot_general(df_ref[...], w_ref[...], dn, preferred_element_type=F32)
        g = gu_ref[0].astype(F32)
        u = gu_ref[1].astype(F32)
        sg = _sigmoid(g)
        dh_ref[0] = (da * u * (sg * (1.0 + g * (1.0 - sg)))).astype(BF16)
        dh_ref[1] = (da * (g * sg)).astype(BF16)

    (out,), riders = _call_with_plans(
        body, plans, name=name, grid=(half, t // tm),
        in_specs=[pl.BlockSpec((tm, d), lambda j, i: (i, 0)),
                  pl.BlockSpec((None, w, d), lambda j, i: (j, 0, 0)),
                  pl.BlockSpec((2, None, tm, w), lambda j, i: (0, j, i, 0))],
        out_specs=[pl.BlockSpec((2, None, tm, w), lambda j, i: (0, j, i, 0))],
        out_shape=[jax.ShapeDtypeStruct((2, half, t, w), BF16)],
        scratch_shapes=[], semantics=("parallel", "parallel"), operands=(dfb, wout, gu))
    return out.reshape(2 * half, t, w), riders


def _ln_stats(r):
    mu = jnp.mean(r, axis=-1, keepdims=True)
    xc = r - mu
    var = jnp.mean(xc * xc, axis=-1, keepdims=True)
    rstd = lax.rsqrt(var + LN_EPS)
    return xc * rstd, rstd


def _ln_epilogue(xin, gamma, beta, fs):
    def fn(p, xv, g, b):
        r = ALPHA * xv + fs * p
        xh, _ = _ln_stats(r)
        y = xh * g + b
        return r, y, y

    return fn, (xin,), (gamma, beta), (F32, F32, BF16)


def _ln_bwd(xin, fparts, gamma, beta, grads, *, name, fs=1.0, ple=False, target=None, tm=512):
    t, d = xin.shape
    nf = len(fparts)
    ng = len(grads)
    coefs = [c for _, c in grads]
    use_t = target is not None
    n_fout = 2 if ple else 1

    def body(*refs):
        pos = 0
        x_ref = refs[pos]; pos += 1
        f_refs = refs[pos:pos + nf]; pos += nf
        g_ref, b_ref = refs[pos:pos + 2]; pos += 2
        gr_refs = refs[pos:pos + ng]; pos += ng
        if use_t:
            t_ref = refs[pos]; pos += 1
        dr_ref = refs[pos]; pos += 1
        fo_refs = refs[pos:pos + n_fout]; pos += n_fout
        dg_ref, db_ref = refs[pos:pos + 2]; pos += 2
        if use_t:
            loss_ref = refs[pos]; pos += 1
        i = pl.program_id(0)

        if ple:
            pe = f_refs[0][...]
            sg = _sigmoid(f_refs[1][...])
            resid = ALPHA * x_ref[...] + pe * sg
        else:
            resid = x_ref[...]
        xh, rstd = _ln_stats(resid)
        gam = g_ref[...]
        if use_t:
            diff = xh * gam + b_ref[...] - t_ref[...]
            dy = diff * (1.0 / d)
            lpart = 0.5 * jnp.sum(jnp.sum(diff * diff, axis=-1, keepdims=True), axis=0, keepdims=True) * (1.0 / d)
        else:
            dy = coefs[0] * gr_refs[0][...]
            for c, r in zip(coefs[1:], gr_refs[1:]):
                dy = dy + c * r[...]
        dxh = dy * gam
        m1 = jnp.mean(dxh, axis=-1, keepdims=True)
        m2 = jnp.mean(dxh * xh, axis=-1, keepdims=True)
        dr = rstd * (dxh - m1 - xh * m2)
        dr_ref[...] = dr
        if ple:
            fo_refs[0][...] = (dr * sg).astype(BF16)
            fo_refs[1][...] = (dr * pe * (sg * (1.0 - sg))).astype(BF16)
        else:
            fo_refs[0][...] = (fs * dr).astype(BF16)
        dgp = jnp.sum(dy * xh, axis=0, keepdims=True)
        dbp = jnp.sum(dy, axis=0, keepdims=True)

        @pl.when(i == 0)
        def _():
            dg_ref[...] = dgp
            db_ref[...] = dbp
            if use_t:
                loss_ref[...] = jnp.broadcast_to(lpart, loss_ref.shape)

        @pl.when(i > 0)
        def _():
            dg_ref[...] += dgp
            db_ref[...] += dbp
            if use_t:
                loss_ref[...] += jnp.broadcast_to(lpart, loss_ref.shape)

    ins = [xin, *fparts, gamma, beta, *[g for g, _ in grads]] + ([target] if use_t else [])
    in_specs = ([_row_spec(tm, d)] * (1 + nf) + [_vec_spec(d), _vec_spec(d)] + [_row_spec(tm, d)] * ng
                + ([_row_spec(tm, d)] if use_t else []))
    out_specs = [_row_spec(tm, d)] * (1 + n_fout) + [_vec_spec(d), _vec_spec(d)] + ([_vec_spec(128)] if use_t else [])
    out_shape = ([jax.ShapeDtypeStruct((t, d), F32)] + [jax.ShapeDtypeStruct((t, d), BF16)] * n_fout
                 + [jax.ShapeDtypeStruct((1, d), F32)] * 2 + ([jax.ShapeDtypeStruct((1, 128), F32)] if use_t else []))
    return pl.pallas_call(
        body, name=name, grid=(t // tm,), in_specs=in_specs, out_specs=out_specs, out_shape=out_shape,
        compiler_params=_cparams(("arbitrary",)),
    )(*ins)


def _conv_fwd(proj, cw, cb, *, name, tm=512):
    t = proj.shape[0]
    c = CONV_CH
    hb = tm // 8

    def body(b_ref, c_ref, h_ref, cp_ref, hp_ref, w_ref, bias_ref, o_ref, q_scr):
        i = pl.program_id(0)
        q = c_ref[...] * h_ref[...]
        halo = jnp.where(i > 0, cp_ref[...] * hp_ref[...], 0.0)
        q_scr[0:8, :] = halo
        q_scr[8:, :] = q
        z = (w_ref[2:3, :] * q + w_ref[1:2, :] * q_scr[pl.ds(7, tm), :] + w_ref[0:1, :] * q_scr[pl.ds(6, tm), :]
             + bias_ref[...])
        o_ref[...] = (b_ref[...] * z).astype(BF16)

    prev = lambda blk: pl.BlockSpec((8, c), lambda i: (jnp.maximum(i * hb - 1, 0), blk))
    return pl.pallas_call(
        body, name=name, grid=(t // tm,),
        in_specs=[_row_spec(tm, c, 0), _row_spec(tm, c, 1), _row_spec(tm, c, 2), prev(1), prev(2),
                  pl.BlockSpec((8, c), lambda i: (0, 0)), _vec_spec(c)],
        out_specs=_row_spec(tm, c),
        out_shape=jax.ShapeDtypeStruct((t, c), BF16),
        scratch_shapes=[pltpu.VMEM((tm + 8, c), F32)],
        compiler_params=_cparams(("parallel",)),
    )(proj, proj, proj, proj, proj, cw, cb)


def _conv_bwd(dyc, proj, cw, cb, *, name, tm=512):
    t = proj.shape[0]
    c = CONV_CH
    hb = tm // 8
    nblk = t // 8

    def body(d_ref, b_ref, c_ref, h_ref, cp_ref, hp_ref, dn_ref, bn_ref, w_ref, bias_ref,
             db_ref, dc_ref, dh_ref, dw_ref, q_scr, z_scr):
        i = pl.program_id(0)
        last = pl.num_programs(0) - 1
        cc = c_ref[...]
        ch = h_ref[...]
        q = cc * ch
        halo = jnp.where(i > 0, cp_ref[...] * hp_ref[...], 0.0)
        q_scr[0:8, :] = halo
        q_scr[8:, :] = q
        w0, w1, w2 = w_ref[0:1, :], w_ref[1:2, :], w_ref[2:3, :]
        qm1 = q_scr[pl.ds(7, tm), :]
        qm2 = q_scr[pl.ds(6, tm), :]
        z = w2 * q + w1 * qm1 + w0 * qm2 + bias_ref[...]
        d = d_ref[...]
        bb = b_ref[...]
        db_ref[...] = (d * z).astype(BF16)
        dz = d * bb
        z_scr[0:tm, :] = dz
        z_scr[tm:, :] = jnp.where(i < last, dn_ref[...] * bn_ref[...], 0.0)
        dq = w2 * dz + w1 * z_scr[pl.ds(1, tm), :] + w0 * z_scr[pl.ds(2, tm), :]
        dc_ref[...] = (dq * ch).astype(BF16)
        dh_ref[...] = (dq * cc).astype(BF16)
        row = lax.broadcasted_iota(jnp.int32, (8, c), 0)
        part = jnp.zeros((8, c), F32)
        for k, term in enumerate((dz * qm2, dz * qm1, dz * q, dz)):
            part = jnp.where(row == k, jnp.sum(term, axis=0, keepdims=True), part)

        @pl.when(i == 0)
        def _():
            dw_ref[...] = part

        @pl.when(i > 0)
        def _():
            dw_ref[...] += part

    prev = lambda blk: pl.BlockSpec((8, c), lambda i: (jnp.maximum(i * hb - 1, 0), blk))
    nxt_p = pl.BlockSpec((8, c), lambda i: (jnp.minimum((i + 1) * hb, nblk - 1), 0))
    nxt_d = pl.BlockSpec((8, c), lambda i: (jnp.minimum((i + 1) * hb, nblk - 1), 0))
    return pl.pallas_call(
        body, name=name, grid=(t // tm,),
        in_specs=[_row_spec(tm, c), _row_spec(tm, c, 0), _row_spec(tm, c, 1), _row_spec(tm, c, 2),
                  prev(1), prev(2), nxt_d, nxt_p, pl.BlockSpec((8, c), lambda i: (0, 0)), _vec_spec(c)],
        out_specs=[_row_spec(tm, c)] * 3 + [pl.BlockSpec((8, c), lambda i: (0, 0))],
        out_shape=[jax.ShapeDtypeStruct((t, c), BF16)] * 3 + [jax.ShapeDtypeStruct((8, c), F32)],
        scratch_shapes=[pltpu.VMEM((tm + 8, c), F32), pltpu.VMEM((tm + 8, c), F32)],
        compiler_params=_cparams(("arbitrary",)),
    )(dyc, proj, proj, proj, proj, proj, dyc, proj, cw, cb)


def _gate_fwd(yconv, glu, proj, *, name, tm=512):
    t, d = yconv.shape

    def body(yc_ref, ga_ref, gb_ref, gc_ref, gs_ref, o_ref):
        yssm = ga_ref[...] * _sigmoid(gb_ref[...])
        o_ref[...] = (_sigmoid(gc_ref[...]) * yc_ref[...] + _sigmoid(gs_ref[...]) * yssm).astype(BF16)

    return pl.pallas_call(
        body, name=name, grid=(t // tm,),
        in_specs=[_row_spec(tm, d), _row_spec(tm, d, 0), _row_spec(tm, d, 1), _row_spec(tm, d, 2), _row_spec(tm, d, 3)],
        out_specs=_row_spec(tm, d), out_shape=jax.ShapeDtypeStruct((t, d), BF16),
        compiler_params=_cparams(("parallel",)),
    )(yconv, glu, glu, proj, proj)


def _gate_bwd(dm, yconv, glu, proj, *, name, tm=512):
    t, d = yconv.shape

    def body(dm_ref, yc_ref, ga_ref, gb_ref, gc_ref, gs_ref, dyc_ref, dglu_ref, dgate_ref):
        dmv = dm_ref[...]
        sc = _sigmoid(gc_ref[...])
        ss = _sigmoid(gs_ref[...])
        sb = _sigmoid(gb_ref[...])
        ga = ga_ref[...]
        yssm = ga * sb
        dyc_ref[...] = (dmv * sc).astype(BF16)
        dgate_ref[:, 0:d] = (dmv * yc_ref[...] * (sc * (1.0 - sc))).astype(BF16)
        dys = dmv * ss
        dgate_ref[:, d:2 * d] = (dmv * yssm * (ss * (1.0 - ss))).astype(BF16)
        dglu_ref[:, 0:d] = (dys * sb).astype(BF16)
        dglu_ref[:, d:2 * d] = (dys * ga * (sb * (1.0 - sb))).astype(BF16)

    return pl.pallas_call(
        body, name=name, grid=(t // tm,),
        in_specs=[_row_spec(tm, d), _row_spec(tm, d), _row_spec(tm, d, 0), _row_spec(tm, d, 1),
                  _row_spec(tm, d, 2), _row_spec(tm, d, 3)],
        out_specs=[_row_spec(tm, d), _row_spec(tm, 2 * d), _row_spec(tm, 2 * d)],
        out_shape=[jax.ShapeDtypeStruct((t, d), BF16), jax.ShapeDtypeStruct((t, 2 * d), BF16),
                   jax.ShapeDtypeStruct((t, 2 * d), BF16)],
        compiler_params=_cparams(("parallel",)),
    )(dm, yconv, glu, glu, proj, proj)


_GELU_C = math.sqrt(2.0 / math.pi)


def _gelu(v):
    return 0.5 * v * (1.0 + jnp.tanh(_GELU_C * (v + 0.044715 * v * v * v)))


def _gelu_grad(v):
    th = jnp.tanh(_GELU_C * (v + 0.044715 * v * v * v))
    return 0.5 * (1.0 + th) + 0.5 * v * (1.0 - th * th) * (_GELU_C * (1.0 + 3.0 * 0.044715 * v * v))


def _cmul(ar, ai, br, bi):
    return ar * br - ai * bi, ar * bi + ai * br


def _scan_fwd(proj, bmat, tab, *, name, plans=()):
    t = proj.shape[0]
    tt, cbw = SCAN_TT, SCAN_CB
    w2 = 2 * cbw

    def body(u_ref, b_ref, tab_ref, s_ref, sb_ref, bu_scr, carry):
        ti = pl.program_id(1)

        @pl.when(ti == 0)
        def _():
            carry[...] = jnp.zeros_like(carry)

        bu_scr[...] = jnp.dot(u_ref[...].astype(BF16), b_ref[...], preferred_element_type=F32)
        row = lax.broadcasted_iota(jnp.int32, (8, cbw), 0)

        def blk(bi, c):
            cr, ci = c
            r0 = pl.multiple_of(bi * 8, 8)
            xr = bu_scr[pl.ds(r0, 8), 0:cbw]
            xi = bu_scr[pl.ds(r0, 8), cbw:w2]
            for k, sh in enumerate((1, 2, 4)):
                kr = tab_ref[k:k + 1, 0:cbw]
                ki = tab_ref[k:k + 1, cbw:w2]
                sr = jnp.where(row >= sh, pltpu.roll(xr, sh, 0), 0.0)
                si = jnp.where(row >= sh, pltpu.roll(xi, sh, 0), 0.0)
                pr, pi = _cmul(kr, ki, sr, si)
                xr = xr + pr
                xi = xi + pi
            pr, pi = _cmul(tab_ref[8:16, 0:cbw], tab_ref[8:16, cbw:w2], cr, ci)
            xr = xr + pr
            xi = xi + pi
            s_ref[pl.ds(r0, 8), 0:cbw] = xr
            s_ref[pl.ds(r0, 8), cbw:w2] = xi
            return (jnp.broadcast_to(xr[7:8, :], (8, cbw)), jnp.broadcast_to(xi[7:8, :], (8, cbw)))

        cr, ci = lax.fori_loop(0, tt // 8, blk, (carry[:, 0:cbw], carry[:, cbw:w2]))
        carry[:, 0:cbw] = cr
        carry[:, cbw:w2] = ci
        sb_ref[...] = s_ref[...].astype(BF16)

    (s, sb), riders = _call_with_plans(
        body, plans, name=name, grid=(SCAN_NCB, t // tt),
        in_specs=[pl.BlockSpec((tt, SCAN_UW), lambda cb, ti: (ti, 3 * SCAN_NCB + cb)),
                  pl.BlockSpec((None, SCAN_UW, w2), lambda cb, ti: (cb, 0, 0)),
                  pl.BlockSpec((16, w2), lambda cb, ti: (0, cb))],
        out_specs=[pl.BlockSpec((tt, w2), lambda cb, ti: (ti, cb))] * 2,
        out_shape=[jax.ShapeDtypeStruct((t, 2 * SSM_CH), F32), jax.ShapeDtypeStruct((t, 2 * SSM_CH), BF16)],
        scratch_shapes=[pltpu.VMEM((tt, w2), F32), pltpu.VMEM((8, w2), F32)],
        semantics=("parallel", "arbitrary"), operands=(proj, bmat, tab))
    return s, sb, riders


def _scan_bwd(dyb, cmat_t, s, tabb, *, name, plans=()):
    t = s.shape[0]
    tt, cbw = SCAN_TT, SCAN_CB
    w2 = 2 * cbw
    nt = t // tt
    hb = tt // 8

    def body(dy_ref, c_ref, s_ref, sp_ref, tab_ref, h_ref, da_ref, g_scr, s_scr, carry):
        ti = pl.program_id(1)

        @pl.when(ti == 0)
        def _():
            carry[...] = jnp.zeros_like(carry)
            da_ref[...] = jnp.zeros_like(da_ref)

        g_scr[...] = jnp.dot(dy_ref[...], c_ref[...], preferred_element_type=F32)
        s_scr[0:8, :] = jnp.where(ti < nt - 1, sp_ref[...], 0.0)
        s_scr[8:, :] = s_ref[...]
        row = lax.broadcasted_iota(jnp.int32, (8, cbw), 0)

        def blk(k, c):
            cr, ci, ar, ai = c
            bi = hb - 1 - k
            r0 = pl.multiple_of(bi * 8, 8)
            xr = g_scr[pl.ds(r0, 8), 0:cbw]
            xi = g_scr[pl.ds(r0, 8), cbw:w2]
            for j, sh in enumerate((1, 2, 4)):
                kr = tab_ref[j:j + 1, 0:cbw]
                ki = tab_ref[j:j + 1, cbw:w2]
                sr = jnp.where(row < 8 - sh, pltpu.roll(xr, 8 - sh, 0), 0.0)
                si = jnp.where(row < 8 - sh, pltpu.roll(xi, 8 - sh, 0), 0.0)
                pr, pi = _cmul(kr, ki, sr, si)
                xr = xr + pr
                xi = xi + pi
            pr, pi = _cmul(tab_ref[8:16, 0:cbw], tab_ref[8:16, cbw:w2], cr, ci)
            xr = xr + pr
            xi = xi + pi
            h_ref[pl.ds(r0, 8), 0:cbw] = xr.astype(BF16)
            h_ref[pl.ds(r0, 8), cbw:w2] = xi.astype(BF16)
            pvr = s_scr[pl.ds(r0, 8), 0:cbw]
            pvi = s_scr[pl.ds(r0, 8), cbw:w2]
            cur_r = s_scr[pl.ds(r0 + 8, 8), 0:cbw]
            cur_i = s_scr[pl.ds(r0 + 8, 8), cbw:w2]
            spr = jnp.where(row == 0, jnp.broadcast_to(pvr[7:8, :], (8, cbw)), pltpu.roll(cur_r, 1, 0))
            spi = jnp.where(row == 0, jnp.broadcast_to(pvi[7:8, :], (8, cbw)), pltpu.roll(cur_i, 1, 0))
            ar = ar + spr * xr + spi * xi
            ai = ai + spr * xi - spi * xr
            return (jnp.broadcast_to(xr[0:1, :], (8, cbw)), jnp.broadcast_to(xi[0:1, :], (8, cbw)), ar, ai)

        z = jnp.zeros((8, cbw), F32)
        cr, ci, ar, ai = lax.fori_loop(0, hb, blk, (carry[:, 0:cbw], carry[:, cbw:w2], z, z))
        carry[:, 0:cbw] = cr
        carry[:, cbw:w2] = ci
        da_ref[:, 0:cbw] += ar
        da_ref[:, cbw:w2] += ai

    rt = lambda ti: nt - 1 - ti
    (h, da), riders = _call_with_plans(
        body, plans, name=name, grid=(SCAN_NCB, nt),
        in_specs=[pl.BlockSpec((tt, SCAN_UW), lambda cb, ti: (rt(ti), cb)),
                  pl.BlockSpec((None, SCAN_UW, w2), lambda cb, ti: (cb, 0, 0)),
                  pl.BlockSpec((tt, w2), lambda cb, ti: (rt(ti), cb)),
                  pl.BlockSpec((8, w2), lambda cb, ti: (jnp.maximum(rt(ti) * hb - 1, 0), cb)),
                  pl.BlockSpec((16, w2), lambda cb, ti: (0, cb))],
        out_specs=[pl.BlockSpec((tt, w2), lambda cb, ti: (rt(ti), cb)),
                   pl.BlockSpec((8, w2), lambda cb, ti: (0, cb))],
        out_shape=[jax.ShapeDtypeStruct((t, 2 * SSM_CH), BF16), jax.ShapeDtypeStruct((8, 2 * SSM_CH), F32)],
        scratch_shapes=[pltpu.VMEM((tt, w2), F32), pltpu.VMEM((tt + 8, w2), F32), pltpu.VMEM((8, w2), F32)],
        semantics=("parallel", "arbitrary"), operands=(dyb, cmat_t, s, s, tabb))
    return h, da, riders


def _s5_out(ymm, proj, dvec, *, name, tm=512):
    t, w = ymm.shape

    def body(y_ref, u_ref, d_ref, yo_ref, sg_ref, ub_ref):
        u = u_ref[...]
        y = y_ref[...] + d_ref[...] * u
        yo_ref[...] = y
        sg_ref[...] = _gelu(y).astype(BF16)
        ub_ref[...] = u.astype(BF16)

    return pl.pallas_call(
        body, name=name, grid=(t // tm,),
        in_specs=[_row_spec(tm, w), _row_spec(tm, w, 3), _vec_spec(w)],
        out_specs=[_row_spec(tm, w)] * 3,
        out_shape=[jax.ShapeDtypeStruct((t, w), F32), jax.ShapeDtypeStruct((t, w), BF16), jax.ShapeDtypeStruct((t, w), BF16)],
        compiler_params=_cparams(("parallel",)),
    )(ymm, proj, dvec)


def _s5_bwd_in(dsg, y, proj, *, name, tm=512):
    t, w = y.shape

    def body(d_ref, y_ref, u_ref, dy_ref, dyb_ref, dd_ref):
        i = pl.program_id(0)
        dy = d_ref[...] * _gelu_grad(y_ref[...])
        dy_ref[...] = dy
        dyb_ref[...] = dy.astype(BF16)
        part = jnp.sum(dy * u_ref[...], axis=0, keepdims=True)

        @pl.when(i == 0)
        def _():
            dd_ref[...] = part

        @pl.when(i > 0)
        def _():
            dd_ref[...] += part

    return pl.pallas_call(
        body, name=name, grid=(t // tm,),
        in_specs=[_row_spec(tm, w), _row_spec(tm, w), _row_spec(tm, w, 3)],
        out_specs=[_row_spec(tm, w), _row_spec(tm, w), _vec_spec(w)],
        out_shape=[jax.ShapeDtypeStruct((t, w), F32), jax.ShapeDtypeStruct((t, w), BF16), jax.ShapeDtypeStruct((1, w), F32)],
        compiler_params=_cparams(("arbitrary",)),
    )(dsg, y, proj)


def _s5_du(dumm, dy, dvec, *, name, tm=512):
    t, w = dy.shape

    def body(a_ref, dy_ref, d_ref, o_ref):
        o_ref[...] = (a_ref[...] + d_ref[...] * dy_ref[...]).astype(BF16)

    return pl.pallas_call(
        body, name=name, grid=(t // tm,), in_specs=[_row_spec(tm, w), _row_spec(tm, w), _vec_spec(w)],
        out_specs=_row_spec(tm, w), out_shape=jax.ShapeDtypeStruct((t, w), BF16),
        compiler_params=_cparams(("parallel",)),
    )(dumm, dy, dvec)


def _s5_discretise(lam_re, lam_im, log_step, b_re, b_im):
    lam = lax.complex(lam_re, lam_im)
    dt = jnp.exp(log_step)[:, None]
    a = jnp.exp(lam * dt)
    bbar = ((a - 1.0) / lam)[..., None] * lax.complex(b_re, b_im)
    return jnp.real(a), jnp.imag(a), jnp.real(bbar), jnp.imag(bbar)


def _perm_cols(re, im):
    lead = re.shape[:-1]
    r = re.reshape(lead + (SCAN_NCB, 1, SCAN_CB))
    i = im.reshape(lead + (SCAN_NCB, 1, SCAN_CB))
    return jnp.concatenate([r, i], axis=-2).reshape(lead + (2 * SSM_CH,))


def _unperm_cols(x):
    lead = x.shape[:-1]
    y = x.reshape(lead + (SCAN_NCB, 2, SCAN_CB))
    return y[..., 0, :].reshape(lead + (SSM_CH,)), y[..., 1, :].reshape(lead + (SSM_CH,))


def _compact(re, im):
    _, r, c = re.shape
    eye = jnp.eye(SCAN_GPB, dtype=re.dtype)

    def half(x):
        x = x.reshape(SCAN_NCB, SCAN_GPB, r, c)
        return (eye[None, :, None, :, None] * x[:, :, :, None, :]).reshape(SCAN_NCB, SCAN_GPB * r, SCAN_GPB * c)

    return jnp.concatenate([half(re), half(im)], axis=-1)


def _compact_extract(x, r):
    c = SSM_STATE
    eye = jnp.eye(SCAN_GPB, dtype=x.dtype)
    y = x.reshape(SCAN_NCB, SCAN_GPB, r, 2, SCAN_GPB, c)
    dg = jnp.sum(y * eye[None, :, None, None, :, None], axis=4).reshape(SSM_GROUPS, r, 2, c)
    return dg[:, :, 0, :], dg[:, :, 1, :]


def _pow_table(ar, ai, descending=False):
    ar = ar.reshape(1, SSM_CH)
    ai = ai.reshape(1, SSM_CH)
    pw = [(ar, ai)]
    for _ in range(7):
        pw.append(_cmul(pw[-1][0], pw[-1][1], ar, ai))
    zero = (jnp.zeros_like(ar), jnp.zeros_like(ar))
    rows = [pw[0], pw[1], pw[3]] + [zero] * 5 + (pw[::-1] if descending else pw)
    re = jnp.concatenate([r for r, _ in rows], axis=0)
    im = jnp.concatenate([i for _, i in rows], axis=0)
    return _perm_cols(re, im)


def _place():
    x, y, c = lax.axis_index("x"), lax.axis_index("y"), lax.axis_index("c")
    chips = [(1 - x, y), (x, 1 - y), (1 - x, 1 - y)]
    return x, y, c, chips


def _dev(px, py, pc):
    return 4 * px + 2 * py + pc


class _Plan:
    def __init__(self, ins, out_shapes, sem_shapes, start, finish, middle=None):
        self.ins, self.out_shapes, self.sem_shapes = list(ins), list(out_shapes), list(sem_shapes)
        self.start, self.finish, self.middle = start, finish, middle


def _split_plan_refs(plans, in_refs, out_refs, sem_refs):
    res, i, o, s = [], 0, 0, 0
    for p in plans:
        ni, no, ns = len(p.ins), len(p.out_shapes), len(p.sem_shapes)
        res.append((in_refs[i:i + ni], out_refs[o:o + no], sem_refs[s:s + ns]))
        i, o, s = i + ni, o + no, s + ns
    return res


def _run_plans(plans, *, name):
    ins = [a for p in plans for a in p.ins]
    outs = [o for p in plans for o in p.out_shapes]
    sems = [s for p in plans for s in p.sem_shapes]
    any_spec = pl.BlockSpec(memory_space=pl.ANY)

    def body(*refs):
        parts = _split_plan_refs(plans, refs[:len(ins)], refs[len(ins):len(ins) + len(outs)], refs[len(ins) + len(outs):])
        for p, r in zip(plans, parts):
            p.start(*r)
        for p, r in zip(plans, parts):
            if p.middle:
                p.middle(*r)
        for p, r in zip(plans, parts):
            p.finish(*r)

    res = pl.pallas_call(body, name=name, in_specs=[any_spec] * len(ins), out_specs=[any_spec] * len(outs),
                         out_shape=outs, scratch_shapes=sems)(*ins)
    return _split_plan_refs(plans, [], res, [])


def _run_plans_on_sequencer(plans, peers_of, *, name, collective_id, after=()):
    ins = [a for p in plans for a in p.ins]
    outs = [o for p in plans for o in p.out_shapes]
    sems = [s for p in plans for s in p.sem_shapes]

    def body(*refs):
        x, y, c, chips = _place()
        peers = peers_of(x, y, c, chips)
        barrier = pltpu.get_barrier_semaphore()
        for peer in peers:
            pl.semaphore_signal(barrier, inc=1, device_id=peer, device_id_type=MESH)
        pl.semaphore_wait(barrier, len(peers))
        n_in = len(ins) + len(after)
        parts = _split_plan_refs(plans, refs[:len(ins)], refs[n_in:n_in + len(outs)], refs[n_in + len(outs):])
        for p, r in zip(plans, parts):
            p.start(*r)
        for p, r in zip(plans, parts):
            if p.middle:
                p.middle(*r)
        for p, r in zip(plans, parts):
            p.finish(*r)

    res = pl.kernel(body, name=name, out_type=outs, mesh=plsc.ScalarSubcoreMesh(axis_name="seq", num_cores=1),
                    scratch_types=sems, compiler_params=pltpu.CompilerParams(collective_id=collective_id))(*ins, *after)
    return _split_plan_refs(plans, [], list(res), [])


def _call_with_plans(body, plans, *, name, grid, in_specs, out_specs, out_shape, scratch_shapes, semantics, operands):
    plans = list(plans)
    if not plans:
        res = pl.pallas_call(body, name=name, grid=grid, in_specs=in_specs, out_specs=out_specs, out_shape=out_shape,
                             scratch_shapes=scratch_shapes, compiler_params=_cparams(semantics))(*operands)
        return list(res), []
    n_in, n_out, n_scr = len(in_specs), len(out_specs), len(scratch_shapes)
    p_ins = [a for p in plans for a in p.ins]
    p_outs = [o for p in plans for o in p.out_shapes]
    p_sems = [s for p in plans for s in p.sem_shapes]
    nsteps = math.prod(grid)
    any_spec = pl.BlockSpec(memory_space=pl.ANY)

    def wrapped(*refs):
        bounds = [n_in, len(p_ins), n_out, len(p_outs), n_scr]
        parts, pos = [], 0
        for b in bounds:
            parts.append(refs[pos:pos + b])
            pos += b
        ins, p_in, outs, p_out, scr = parts
        step = pl.program_id(0)
        for ax in range(1, len(grid)):
            step = step * grid[ax] + pl.program_id(ax)
        riders = _split_plan_refs(plans, p_in, p_out, refs[pos:])

        @pl.when(step == 0)
        def _():
            for p, r in zip(plans, riders):
                p.start(*r)

        mids = [(p, r) for p, r in zip(plans, riders) if p.middle]
        mid_step = nsteps // 2
        split_mid = mids and 0 < mid_step < nsteps - 1
        if split_mid:
            @pl.when(step == mid_step)
            def _():
                for p, r in mids:
                    p.middle(*r)

        body(*ins, *outs, *scr)

        @pl.when(step == nsteps - 1)
        def _():
            if not split_mid:
                for p, r in mids:
                    p.middle(*r)
            for p, r in zip(plans, riders):
                p.finish(*r)

    res = pl.pallas_call(
        wrapped, name=name, grid=grid, in_specs=list(in_specs) + [any_spec] * len(p_ins),
        out_specs=list(out_specs) + [any_spec] * len(p_outs), out_shape=list(out_shape) + p_outs,
        scratch_shapes=list(scratch_shapes) + p_sems, compiler_params=_cparams(("arbitrary",) * len(grid)),
    )(*operands, *p_ins)
    return list(res[:n_out]), [r[1] for r in _split_plan_refs(plans, [], res[n_out:], [])]


def _gather_plan(shards):
    n = len(shards)
    nk = 8

    def make(ins, outs, sems):
        send, recv, lsem = sems
        x, y, c, _ = _place()
        me, sib, xn, yn, dg = (x, y, c), (x, y, 1 - c), (1 - x, y, c), (x, 1 - y, c), (1 - x, 1 - y, c)

        def part(w, block, half):
            ref = outs[w].at[_dev(*block)]
            if half is None:
                return ref
            rows = shards[w].shape[0] // 2
            return ref.at[pl.ds(half * rows, rows)]

        def copy(w, k, block, to, half=None, src=None):
            dst = part(w, block, half)
            return pltpu.make_async_remote_copy(
                src_ref=dst if src is None else src, dst_ref=dst,
                send_sem=send.at[w * nk + k], recv_sem=recv.at[w * nk + k], device_id=to, device_id_type=MESH)

        mine = [pltpu.make_async_copy(ins[w], outs[w].at[_dev(*me)], lsem.at[w]) for w in range(n)]
        return copy, mine, me, sib, xn, yn, dg

    def first_copies(copy, me, sib, xn, yn, ins):
        return [copy(w, k, me, to, src=ins[w]) for w in range(n) for k, to in ((0, sib), (1, xn), (2, yn))]

    def start(ins, outs, sems):
        copy, mine, me, sib, xn, yn, _ = make(ins, outs, sems)
        for cp in mine + first_copies(copy, me, sib, xn, yn, ins):
            cp.start()

    def middle(ins, outs, sems):
        copy, _, me, sib, xn, yn, _ = make(ins, outs, sems)
        for w in range(n):
            copy(w, 1, xn, me).wait_recv()
            copy(w, 3, xn, yn, half=0).start()
            copy(w, 5, xn, sib).start()
        for w in range(n):
            copy(w, 2, yn, me).wait_recv()
            copy(w, 4, yn, xn, half=1).start()
            copy(w, 6, yn, sib).start()

    def finish(ins, outs, sems):
        copy, mine, me, sib, xn, yn, dg = make(ins, outs, sems)
        last = []
        for w in range(n):
            copy(w, 3, dg, me, half=0).wait_recv()
            copy(w, 4, dg, me, half=1).wait_recv()
            fwd = copy(w, 7, dg, sib)
            fwd.start()
            last.append(fwd)
        sx, sy, sd = (1 - me[0], me[1], 1 - me[2]), (me[0], 1 - me[1], 1 - me[2]), (1 - me[0], 1 - me[1], 1 - me[2])
        for w in range(n):
            copy(w, 0, sib, me).wait_recv()
            copy(w, 5, sx, me).wait_recv()
            copy(w, 6, sy, me).wait_recv()
            copy(w, 7, sd, me).wait_recv()
        for cp in first_copies(copy, me, sib, xn, yn, ins) + last:
            cp.wait_send()
        for w in range(n):
            copy(w, 3, xn, yn, half=0).wait_send()
            copy(w, 5, xn, sib).wait_send()
            copy(w, 4, yn, xn, half=1).wait_send()
            copy(w, 6, yn, sib).wait_send()
        for cp in mine:
            cp.wait()

    return _Plan(shards, [jax.ShapeDtypeStruct((N_DEV,) + s.shape, s.dtype) for s in shards],
                 [pltpu.SemaphoreType.DMA((nk * n,)), pltpu.SemaphoreType.DMA((nk * n,)), pltpu.SemaphoreType.DMA((n,))],
                 start, finish, middle)


def _swap_plan(copies_of, n_copies, ins, out_shapes):
    def cps(in_refs, out_refs, sems):
        return copies_of(in_refs, out_refs, sems[0], sems[1])

    def start(in_refs, out_refs, sems):
        for cp in cps(in_refs, out_refs, sems):
            cp.start()

    def finish(in_refs, out_refs, sems):
        all_cps = cps(in_refs, out_refs, sems)
        for cp in all_cps:
            cp.wait_recv()
        for cp in all_cps:
            cp.wait_send()

    return _Plan(ins, out_shapes, [pltpu.SemaphoreType.DMA((n_copies,)), pltpu.SemaphoreType.DMA((n_copies,))],
                 start, finish)


def _sibling_plan(grads):
    n = len(grads)

    def copies(ins, outs, send, recv):
        x, y, c, chips = _place()
        owners = [(x, y)] + chips
        return [pltpu.make_async_remote_copy(
            src_ref=ins[w].at[_dev(*chip, 1 - c)], dst_ref=outs[w].at[k], send_sem=send.at[w * 4 + k],
            recv_sem=recv.at[w * 4 + k], device_id=(x, y, 1 - c), device_id_type=MESH)
            for w in range(n) for k, chip in enumerate(owners)]

    return _swap_plan(copies, 4 * n, grads, [jax.ShapeDtypeStruct((4,) + g.shape[1:], g.dtype) for g in grads])


def _chip_plan(parts, js=(0, 1, 2)):
    n, nj = len(parts), len(js)

    def copies(ins, outs, send, recv):
        x, y, c, chips = _place()
        return [pltpu.make_async_remote_copy(
            src_ref=ins[w].at[j], dst_ref=outs[w * nj + k], send_sem=send.at[w * nj + k],
            recv_sem=recv.at[w * nj + k], device_id=(*chips[j], c), device_id_type=MESH)
            for w in range(n) for k, j in enumerate(js)]

    return _swap_plan(copies, n * nj, parts,
                      [jax.ShapeDtypeStruct(p.shape[1:], p.dtype) for p in parts for _ in js])


UPDATE_TILE_BYTES = 1536 * 1024


def _row_tile(r, c):
    best = 8
    for t in range(8, r + 1, 8):
        if r % t == 0 and t * c * 4 <= UPDATE_TILE_BYTES:
            best = t
    return best


def _chip_partial(g, sib, ids, *, name):
    _, r, c = g.shape
    tr = _row_tile(r, c)

    def body(ids_ref, g_ref, s_ref, o_ref):
        o_ref[...] = (g_ref[...] + s_ref[...]).astype(BF16)

    return pl.pallas_call(
        body, name=name,
        grid_spec=pltpu.PrefetchScalarGridSpec(
            num_scalar_prefetch=1, grid=(3, r // tr),
            in_specs=[pl.BlockSpec((None, tr, c), lambda j, i, ids_ref: (ids_ref[j], i, 0)),
                      pl.BlockSpec((None, tr, c), lambda j, i, ids_ref: (j + 1, i, 0))],
            out_specs=pl.BlockSpec((None, tr, c), lambda j, i, ids_ref: (j, i, 0))),
        out_shape=jax.ShapeDtypeStruct((3, r, c), BF16),
        compiler_params=_cparams(("parallel", "parallel")),
    )(ids, g, sib)


def _adamw_math(w, g, m, v):
    m = ADAM_B1 * m + (1.0 - ADAM_B1) * g
    v = ADAM_B2 * v + (1.0 - ADAM_B2) * (g * g)
    m_hat = m / (1.0 - ADAM_B1 ** ADAM_STEP)
    v_hat = v / (1.0 - ADAM_B2 ** ADAM_STEP)
    delta = -ADAM_LR * (m_hat / (jnp.sqrt(v_hat) + ADAM_EPS) + ADAM_WD * w)
    return delta, m, v


def _shard_update(g, sib, rem, me, w, m, v, *, name):
    r, c = w.shape
    tr = _row_tile(r, c)

    def body(me_ref, g_ref, s_ref, r0_ref, r1_ref, r2_ref, w_ref, m_ref, v_ref, go_ref, d_ref, mo_ref, vo_ref):
        gt = g_ref[...] + s_ref[...]
        gt = gt + r0_ref[...].astype(F32)
        gt = gt + r1_ref[...].astype(F32)
        gt = gt + r2_ref[...].astype(F32)
        go_ref[...] = gt
        d, mn, vn = _adamw_math(w_ref[...], gt, m_ref[...], v_ref[...])
        d_ref[...] = d
        mo_ref[...] = mn
        vo_ref[...] = vn

    blk = lambda k: pl.BlockSpec((None, tr, c), lambda i, me_ref: (k, i, 0))
    plain = pl.BlockSpec((tr, c), lambda i, me_ref: (i, 0))
    return pl.pallas_call(
        body, name=name,
        grid_spec=pltpu.PrefetchScalarGridSpec(
            num_scalar_prefetch=1, grid=(r // tr,),
            in_specs=[pl.BlockSpec((None, tr, c), lambda i, me_ref: (me_ref[0], i, 0)), blk(0), plain, plain, plain,
                      plain, plain, plain],
            out_specs=[plain] * 4),
        out_shape=[jax.ShapeDtypeStruct((r, c), F32)] * 4,
        compiler_params=_cparams(("parallel",)),
    )(me, g, sib, *rem, w, m, v)


def _small_sum(gathered, *, name):
    _, r, c = gathered.shape

    def body(g_ref, go_ref):
        gt = g_ref[0]
        for k in range(1, N_DEV):
            gt = gt + g_ref[k]
        go_ref[...] = gt

    return pl.pallas_call(
        body, name=name, out_shape=jax.ShapeDtypeStruct((r, c), F32),
        compiler_params=pltpu.CompilerParams(vmem_limit_bytes=VMEM_LIMIT),
    )(gathered)


SMALL_UNIT = 1024


def _pack(parts):
    flat = []
    for p in parts:
        f = p.reshape(-1).astype(F32)
        pad = (-f.shape[0]) % SMALL_UNIT
        flat.append(jnp.pad(f, (0, pad)))
    return jnp.concatenate(flat).reshape(-1, 128)


def _unpack(buf, shapes):
    flat = buf.reshape(-1)
    out, off = [], 0
    for s in shapes:
        nel = math.prod(s)
        out.append(flat[off:off + nel].reshape(s))
        off += nel + ((-nel) % SMALL_UNIT)
    return out


def kernel(x, p, ffn1_w_in, ffn1_w_out, ln1_g, ln1_b, mix_w_in, conv_w, conv_b, conv_w_out, ssm_lam_re, ssm_lam_im, ssm_log_step, ssm_b_re, ssm_b_im, ssm_c_re, ssm_c_im, ssm_d, ssm_w_glu, mix_w_out, ln2_g, ln2_b, ffn2_w_in, ffn2_w_out, ln3_g, ln3_b, ple_w_in, ple_w_gate, ln4_g, ln4_b, loss_target, m_ffn1_w_in, m_ffn1_w_out, m_ln1_g, m_ln1_b, m_mix_w_in, m_conv_w, m_conv_b, m_conv_w_out, m_ssm_lam_re, m_ssm_lam_im, m_ssm_log_step, m_ssm_b_re, m_ssm_b_im, m_ssm_c_re, m_ssm_c_im, m_ssm_d, m_ssm_w_glu, m_mix_w_out, m_ln2_g, m_ln2_b, m_ffn2_w_in, m_ffn2_w_out, m_ln3_g, m_ln3_b, m_ple_w_in, m_ple_w_gate, m_ln4_g, m_ln4_b, v_ffn1_w_in, v_ffn1_w_out, v_ln1_g, v_ln1_b, v_mix_w_in, v_conv_w, v_conv_b, v_conv_w_out, v_ssm_lam_re, v_ssm_lam_im, v_ssm_log_step, v_ssm_b_re, v_ssm_b_im, v_ssm_c_re, v_ssm_c_im, v_ssm_d, v_ssm_w_glu, v_mix_w_out, v_ln2_g, v_ln2_b, v_ffn2_w_in, v_ffn2_w_out, v_ln3_g, v_ln3_b, v_ple_w_in, v_ple_w_gate, v_ln4_g, v_ln4_b):
    args = dict(locals())
    big = ['ffn1_w_in', 'ffn1_w_out', 'mix_w_in', 'conv_w_out', 'ssm_w_glu', 'mix_w_out',
           'ffn2_w_in', 'ffn2_w_out', 'ple_w_in', 'ple_w_gate']
    small = ['ln1_g', 'ln1_b', 'conv_b', 'ssm_lam_re', 'ssm_lam_im', 'ssm_log_step', 'ssm_b_re', 'ssm_b_im',
             'ssm_c_re', 'ssm_c_im', 'ssm_d', 'ln2_g', 'ln2_b', 'ln3_g', 'ln3_b', 'ln4_g', 'ln4_b']
    order = ['ffn1_w_in', 'ffn1_w_out', 'ln1_g', 'ln1_b', 'mix_w_in', 'conv_w', 'conv_b', 'conv_w_out',
             'ssm_lam_re', 'ssm_lam_im', 'ssm_log_step', 'ssm_b_re', 'ssm_b_im', 'ssm_c_re', 'ssm_c_im', 'ssm_d',
             'ssm_w_glu', 'mix_w_out', 'ln2_g', 'ln2_b', 'ffn2_w_in', 'ffn2_w_out', 'ln3_g', 'ln3_b',
             'ple_w_in', 'ple_w_gate', 'ln4_g', 'ln4_b']

    t = x.shape[1]
    d = x.shape[2]
    xc_, yc_, cc_ = lax.axis_index("x"), lax.axis_index("y"), lax.axis_index("c")
    me = (4 * xc_ + 2 * yc_ + cc_).astype(jnp.int32)
    cw_cols = conv_w.shape[2]

    turned = ('ffn1_w_in', 'ffn2_w_in')

    def local(a, nm):
        return jnp.swapaxes(a[0], 0, 1) if nm in turned else a[0]

    shard = {nm: local(args[nm], nm).astype(BF16) for nm in big}
    cw_pad = jnp.zeros((16, 128), F32).at[0:3, 0:cw_cols].set(conv_w[0])
    wf = shard['ffn1_w_in'].shape[0]

    def tie(chain, *others):
        out = lax.optimization_barrier((chain, *others))
        return out[0], out[1:]

    def gather(arrays, tag, after=()):
        ((_, got, _),) = _run_plans_on_sequencer(
            [_gather_plan(arrays)], lambda x, y, c, chips: [(x, y, 1 - c), (1 - x, y, c), (x, 1 - y, c)],
            name=f"gather_{tag}", collective_id=2, after=after)
        return got

    w1in, cw_g = gather([shard['ffn1_w_in'], cw_pad], "ffn1_in")
    (w1out_g,) = gather([shard['ffn1_w_out']], "ffn1_out")
    (wmix,) = gather([shard['mix_w_in']], "mix_in")
    wco, wglu, wmo_g = gather([shard[nm] for nm in ('conv_w_out', 'ssm_w_glu', 'mix_w_out')], "mix_rest")
    (w2in,) = gather([shard['ffn2_w_in']], "ffn2_in")
    (w2out_g,) = gather([shard['ffn2_w_out']], "ffn2_out")
    wpin, wgate_g = gather([shard['ple_w_in'], shard['ple_w_gate']], "ple")
    cw_full = jnp.transpose(cw_g[:, 0:3, 0:cw_cols], (1, 0, 2)).reshape(3, N_DEV * cw_cols)
    cw8 = jnp.zeros((8, CONV_CH), F32).at[0:3, :].set(cw_full)

    s5_in = (ssm_lam_re[0], ssm_lam_im[0], ssm_log_step[0], ssm_b_re[0], ssm_b_im[0])
    (a_re, a_im, bb_re, bb_im), s5_vjp = jax.vjp(_s5_discretise, *s5_in)
    tab_f = _pow_table(a_re, a_im)
    tab_b = _pow_table(a_re, -a_im, descending=True)
    bmat_b = _compact(jnp.transpose(bb_re, (0, 2, 1)), jnp.transpose(bb_im, (0, 2, 1))).astype(BF16)
    cmat_tb = _compact(ssm_c_re[0], -ssm_c_im[0]).astype(BF16)
    dvec = ssm_d[0].reshape(1, SSM_W)

    xf = x[0]
    x_b = xf.astype(BF16)
    p_b = p[0, 0].astype(BF16)
    tgt = loss_target[0]
    tq = min(512, t)

    ffn_out = dict(ja='c', jb='c', nj=4, tm=tq, tn=d, tk=wf)
    def side_by_side(wb):
        return jnp.transpose(wb, (1, 0, 2)).reshape(wb.shape[1], N_DEV * wb.shape[2])

    tf = min(1024, t)
    a1, h1, _ = _ffn_in(x_b, w1in, name="ffn1_in", tm=tf)
    w1out = w1out_g.reshape(4, wf, d)
    r1, x1, x1b = _mm(a1, w1out, name="ffn1_out", **ffn_out, epilogue=_ln_epilogue(xf, ln1_g, ln1_b, 0.5))
    proj = _mm(x1b, wmix, name="mix_in", jb='b', jo='b', o_flat=True, nj=8, tm=tq, tn=512, tk=d)
    _, (wco, wglu, wmo_g) = tie(proj, wco, wglu, wmo_g)
    wmo = wmo_g.reshape(d, d)
    ycin = _conv_fwd(proj, cw8, conv_b, name="conv_fwd")
    wco, wglu = side_by_side(wco), side_by_side(wglu)
    yconv = _mm(ycin, wco, name="conv_out", tm=tq, tn=d, tk=CONV_CH)
    s_f, s_b, _ = _scan_fwd(proj, bmat_b, tab_f, name="scan_fwd")
    blk = dict(ja='b', jb='b', jo='b', nj=SCAN_NCB)
    ymm = _mm(s_b, cmat_tb, name="ssm_read", a_flat=True, o_flat=True, tb=True, tm=tq, tn=SCAN_UW, tk=2 * SCAN_CB, **blk)
    ys, sg, u_b = _s5_out(ymm, proj, dvec, name="ssm_out")
    glu = _mm(sg, wglu, name="glu_in", tm=tq, tn=d, tk=SSM_W)
    merged = _gate_fwd(yconv, glu, proj, name="gate_fwd")
    r2, x2, x2b = _mm(merged, wmo, name="mix_out", tm=tq, tn=d, tk=d, epilogue=_ln_epilogue(x1, ln2_g, ln2_b, 1.0))
    a2, h2, _ = _ffn_in(x2b, w2in, name="ffn2_in", tm=tf)
    w2out = w2out_g.reshape(4, wf, d)
    r3, x3, x3b = _mm(a2, w2out, name="ffn2_out", **ffn_out, epilogue=_ln_epilogue(x2, ln3_g, ln3_b, 0.5))
    wgate = wgate_g.reshape(d, d)
    _, (wpin,) = tie(x2b, wpin)
    pe = _mm(p_b, side_by_side(wpin), name="ple_in", tm=tq, tn=d, tk=p_b.shape[1])
    gp = _mm(x3b, wgate, name="ple_gate", tm=tq, tn=d, tk=d)

    dr4, dpe_b, dgp_b, dg4, db4, loss_part = _ln_bwd(x3, [pe, gp], ln4_g, ln4_b, [], name="ple_ln_bwd",
                                                     ple=True, target=tgt)
    gb, sib, rem = {}, {}, {}
    ids = jnp.stack([_dev(1 - xc_, yc_, cc_), _dev(xc_, 1 - yc_, cc_), _dev(1 - xc_, 1 - yc_, cc_)]).astype(jnp.int32)

    def blocked(nm, g):
        return g.reshape((N_DEV,) + args[nm].shape[1:])

    def to_sibling(*names):
        return _sibling_plan([gb[nm] for nm in names])

    def chip_sums(names, sibs):
        for nm, s in zip(names, sibs):
            sib[nm] = s
        return [_chip_partial(gb[nm], sib[nm], ids, name=f"chip_sum_{nm}") for nm in names]

    def on_sequencer(plan, peers, tag, cid, after=()):
        ((_, got, _),) = _run_plans_on_sequencer([plan], peers, name=tag, collective_id=cid, after=after)
        return got

    waiting = []
    arrived = []

    def send_sibling(*names, after=()):
        got = on_sequencer(to_sibling(*names), lambda x, y, c, chips: [(x, y, 1 - c)], f"grad_sibling_{names[0]}", 3,
                           after=after)
        waiting.append((names, got))

    def send_chips(chain, count=None):
        for _ in range(len(waiting) if count is None else count):
            names, got = waiting.pop(0)
            sums = chip_sums(names, got)
            chain, rest = tie(chain, *sums, *arrived)
            r_ = on_sequencer(_chip_plan(list(rest[:len(sums)])), lambda x, y, c, chips: [(*chip, c) for chip in chips],
                              f"grad_chips_{names[0]}", 1)
            arrived[:] = list(r_)
            for i, nm in enumerate(names):
                rem[nm] = r_[3 * i:3 * i + 3]
        return chain

    ffn_in_dg = dict(ja='c', jb='c', nj=8, tm=tq, tn=d, tk=wf)
    ffn_in_wg = dict(ja='b', jo='b', ta=True, nj=8, tm=wf, tn=d, tk=t)
    ffn_out_wg = dict(ja='b', jo='b', ta=True, nj=4, tm=wf, tn=d, tk=t)

    gb['ple_w_in'] = _mm(p_b, dpe_b, name="ple_in_wg", jb='b', jo='b', b_flat=True, ta=True, nj=8,
                         tm=p_b.shape[1], tn=128, tk=t)
    gb['ple_w_gate'] = blocked('ple_w_gate', _mm(x3b, dgp_b, name="ple_gate_wg", ta=True, tm=d, tn=d, tk=t))
    g_ple = ('ple_w_in', 'ple_w_gate')
    dx3_g, (s_,) = _mm(dgp_b, wgate, name="ple_gate_dg", tb=True, tm=tq, tn=d, tk=d, plans=[to_sibling(*g_ple)])
    waiting.append((g_ple, s_))

    dr3, df2_b, dg3, db3 = _ln_bwd(r3, [], ln3_g, ln3_b, [(dr4, ALPHA), (dx3_g, 1.0)], name="ffn2_ln_bwd", fs=0.5)
    df2_b = send_chips(df2_b)
    gb['ffn2_w_out'] = blocked('ffn2_w_out', _mm(a2, df2_b, name="ffn2_out_wg", **ffn_out_wg))
    df2_b, (gb['ffn2_w_out'],) = tie(df2_b, gb['ffn2_w_out'])
    send_sibling('ffn2_w_out')
    dh2, _ = _ffn_out_dg(df2_b, w2out, h2, name="ffn2_out_dg", tm=tf)
    dx2_f = _mm(dh2, w2in, name="ffn2_in_dg", **ffn_in_dg)
    dx2_f = send_chips(dx2_f)
    gb['ffn2_w_in'] = _mm(dh2, x2b, name="ffn2_in_wg", **ffn_in_wg)
    dx2_f, (gb['ffn2_w_in'],) = tie(dx2_f, gb['ffn2_w_in'])

    dr2, dmix_b, dg2, db2 = _ln_bwd(r2, [], ln2_g, ln2_b, [(dr3, ALPHA), (dx2_f, 1.0)], name="mix_ln_bwd")
    dmerged = _mm(dmix_b, wmo, name="mix_out_dg", tb=True, tm=tq, tn=d, tk=d)
    send_sibling('ffn2_w_in', after=(dmerged,))
    gb['mix_w_out'] = blocked('mix_w_out', _mm(merged, dmix_b, name="mix_out_wg", ta=True, tm=d, tn=d, tk=t))
    dyconv_b, dglu_b, dgate_b = _gate_bwd(dmerged, yconv, glu, proj, name="gate_bwd")
    gb['conv_w_out'] = _mm(ycin, dyconv_b, name="conv_out_wg", jb='b', jo='b', b_flat=True, ta=True, nj=8,
                           tm=CONV_CH, tn=128, tk=t)
    dycin = _mm(dyconv_b, wco, name="conv_out_dg", tb=True, tm=tq, tn=CONV_CH, tk=d)
    gb['ssm_w_glu'] = _mm(sg, dglu_b, name="glu_in_wg", jb='b', jo='b', b_flat=True, ta=True, nj=8,
                          tm=SSM_W, tn=256, tk=t)
    dsg = _mm(dglu_b, wglu, name="glu_in_dg", tb=True, tm=tq, tn=SSM_W, tk=2 * d)
    dsg = send_chips(dsg)
    dys, dys_b, dd = _s5_bwd_in(dsg, ys, proj, name="ssm_out_bwd")
    h_b, da_acc, _ = _scan_bwd(dys_b, cmat_tb, s_f, tab_b, name="scan_bwd")
    send_sibling('mix_w_out', 'conv_w_out', 'ssm_w_glu', after=(h_b,))
    dumm = _mm(h_b, bmat_b, name="ssm_write_dg", a_flat=True, o_flat=True, tb=True, tm=tq, tn=SCAN_UW,
               tk=2 * SCAN_CB, **blk)
    du_b = _s5_du(dumm, dys, dvec, name="ssm_du")
    g_bmat = _mm(u_b, h_b, name="ssm_write_wg", a_flat=True, b_flat=True, ta=True, tm=SCAN_UW,
                 tn=2 * SCAN_CB, tk=t, **blk)
    g_cmat = _mm(dys_b, s_b, name="ssm_read_wg", a_flat=True, b_flat=True, ta=True, tm=SCAN_UW,
                 tn=2 * SCAN_CB, tk=t, **blk)
    dcb_b, dcc_b, dch_b, dconv = _conv_bwd(dycin, proj, cw8, conv_b, name="conv_bwd")
    dproj = jnp.concatenate([dcb_b, dcc_b, dch_b, du_b, dgate_b], axis=1)
    dproj = send_chips(dproj)
    gb['mix_w_in'] = _mm(x1b, dproj, name="mix_in_wg", jb='b', jo='b', b_flat=True, ta=True, nj=8,
                         tm=d, tn=512, tk=t)
    send_sibling('mix_w_in')
    dx1_m = _mm(dproj, wmix, name="mix_in_dg", ja='c', jb='c', a_flat=True, tb=True, nj=8, tm=tq, tn=d, tk=512)
    dx1_m, (gb['mix_w_in'],) = tie(dx1_m, gb['mix_w_in'])

    dr1, df1_b, dg1, db1 = _ln_bwd(r1, [], ln1_g, ln1_b, [(dr2, ALPHA), (dx1_m, 1.0)], name="ffn1_ln_bwd", fs=0.5)
    gb['ffn1_w_out'] = blocked('ffn1_w_out', _mm(a1, df1_b, name="ffn1_out_wg", **ffn_out_wg))
    df1_b, (gb['ffn1_w_out'],) = tie(df1_b, gb['ffn1_w_out'])
    send_sibling('ffn1_w_out')
    df1_b = send_chips(df1_b, 1)
    dh1, _ = _ffn_out_dg(df1_b, w1out, h1, name="ffn1_out_dg", tm=tf)
    gb['ffn1_w_in'] = _mm(dh1, x_b, name="ffn1_in_wg", **ffn_in_wg)
    dh1, (gb['ffn1_w_in'],) = tie(dh1, gb['ffn1_w_in'])
    send_sibling('ffn1_w_in')
    dh1 = send_chips(dh1, 1)
    (grad_x,) = _mm(dh1, w1in, name="ffn1_in_dg", **ffn_in_dg,
                    epilogue=(lambda pr, drv: (pr + ALPHA * drv,), (dr1,), (), (F32,)))

    da_sum = jnp.sum(da_acc, axis=0)
    da_re, da_im = _unperm_cols(da_sum)
    gbb_re, gbb_im = [jnp.transpose(v, (0, 2, 1)) for v in _compact_extract(g_bmat, SSM_GROUP)]
    g_c_re, g_c_im_neg = _compact_extract(g_cmat, SSM_GROUP)
    g_c_im = -g_c_im_neg
    g_lam_re, g_lam_im, g_log_step, g_b_re, g_b_im = s5_vjp(
        (da_re.reshape(SSM_GROUPS, SSM_STATE), da_im.reshape(SSM_GROUPS, SSM_STATE), gbb_re, gbb_im))
    g_d = dd.reshape(SSM_GROUPS, SSM_GROUP)

    small_g = {'ln1_g': dg1, 'ln1_b': db1, 'conv_b': dconv[3:4], 'ssm_lam_re': g_lam_re, 'ssm_lam_im': g_lam_im,
               'ssm_log_step': g_log_step, 'ssm_b_re': g_b_re, 'ssm_b_im': g_b_im, 'ssm_c_re': g_c_re,
               'ssm_c_im': g_c_im, 'ssm_d': g_d, 'ln2_g': dg2, 'ln2_b': db2, 'ln3_g': dg3, 'ln3_b': db3,
               'ln4_g': dg4, 'ln4_b': db4}
    small_shapes = [args[nm].shape for nm in small] + [(3, CONV_CH), (1,)]
    g_pack = _pack([small_g[nm] for nm in small] + [dconv[0:3], loss_part[0:1, 0:1]])

    res = {}
    me1, _ = tie(me.reshape(1), grad_x)
    me1 = send_chips(me1, 1)

    def update(nm):
        upd = _shard_update(gb[nm], sib[nm], rem[nm], me1, local(args[nm], nm), local(args['m_' + nm], nm),
                            local(args['v_' + nm], nm), name=f"update_{nm}")
        for key, val in zip(('grad_', 'delta_', 'new_m_', 'new_v_'), upd):
            res[key + nm] = (jnp.swapaxes(val, 0, 1) if nm in turned else val)[None]

    (g_all,) = gather([g_pack], "small", after=(dh1,))
    last = ['ffn1_w_out', 'ffn1_w_in']
    for nm in big:
        if nm not in last:
            update(nm)

    parts = _unpack(_small_sum(g_all, name="small_sum"), small_shapes)
    g_small = dict(zip(small, parts[:len(small)]))
    g_small['conv_w'] = lax.dynamic_slice(parts[len(small)], (0, me * cw_cols), (3, cw_cols))[None]
    loss = parts[-1][0]
    for nm, g_ in g_small.items():
        d_, m_, v_ = _adamw_math(args[nm], g_, args['m_' + nm], args['v_' + nm])
        res['grad_' + nm], res['delta_' + nm], res['new_m_' + nm], res['new_v_' + nm] = g_, d_, m_, v_

    for nm in last:
        update(nm)

    outs = [loss, grad_x[None]]
    for key in ('grad_', 'delta_', 'new_m_', 'new_v_'):
        outs += [res[key + nm] for nm in order]
    return tuple(outs)
```
